```python
import jax, jax.numpy as jnp
from jax import lax
import numpy as np

D_MODEL = 1024
BATCH = 16
SEQ = 2048
DEPTH = 2

RG_WIDTH = 1024
RG_BLOCKS = 16
RG_BLOCK_DIM = RG_WIDTH // RG_BLOCKS
CONV_WIDTH = 4
RG_C = 8.0
SB_HEADS = 8
SB_HEAD_DIM = 64
SB_WIDTH = SB_HEADS * SB_HEAD_DIM
FOX_HEADS = 8
FOX_HEAD_DIM = 64
FOX_WIDTH = FOX_HEADS * FOX_HEAD_DIM
N_BRANCHES = 3
D_FF = 2816
Q_BLOCK = 128
N_SUBLAYERS = 3
EPS = 1e-6
IN_SIZES = (RG_WIDTH, RG_WIDTH, 3 * SB_WIDTH, 3 * FOX_WIDTH, FOX_HEADS, N_BRANCHES * D_MODEL)
N_IN = sum(IN_SIZES)

kernel_name = "hybrid_rglru_stickbreak_fox_macaron_adaln"


def _rmsnorm(x, gain):
    x32 = x.astype(jnp.float32)
    y = x32 * lax.rsqrt(jnp.mean(x32 * x32, axis=-1, keepdims=True) + EPS)
    return y.astype(x.dtype) * gain


def _modulate(h, shift, scale):
    return h * (1.0 + scale[:, None, :]) + shift[:, None, :]


def _swiglu(h, w1, w3, w2):
    return (jax.nn.silu(h @ w1) * (h @ w3)) @ w2


def _causal_depthwise_conv(x, w, b):
    y = lax.conv_general_dilated(
        x, w[:, None, :].astype(x.dtype), window_strides=(1,), padding=[(CONV_WIDTH - 1, 0)],
        dimension_numbers=('NWC', 'WIO', 'NWC'), feature_group_count=x.shape[-1])
    return y + b


def _block_diag(x, w, b):
    bsz, slen, _ = x.shape
    xb = x.reshape(bsz, slen, RG_BLOCKS, RG_BLOCK_DIM)
    return jnp.einsum('bsnd,nde->bsne', xb, w).reshape(bsz, slen, RG_WIDTH) + b


def _rg_lru(x, w_a, b_a, w_x, b_x, lam):
    x32 = x.astype(jnp.float32)
    r = jax.nn.sigmoid(_block_diag(x, w_a, b_a).astype(jnp.float32))
    i = jax.nn.sigmoid(_block_diag(x, w_x, b_x).astype(jnp.float32))
    log_a = -RG_C * r * jax.nn.softplus(-lam.astype(jnp.float32))
    a = jnp.exp(log_a)
    u = jnp.sqrt(-jnp.expm1(2.0 * log_a)) * (i * x32)

    def combine(left, right):
        a_l, b_l = left
        a_r, b_r = right
        return a_l * a_r, a_r * b_l + b_r

    _, h = lax.associative_scan(combine, (a, u), axis=1)
    return h.astype(x.dtype)


def _split_qkv(t, n_heads, head_dim):
    bsz, slen, _ = t.shape
    t = t.reshape(bsz, slen, 3, n_heads, head_dim).transpose(2, 0, 3, 1, 4)
    return t[0], t[1], t[2]


def _merge_heads(o):
    bsz, nh, slen, hd = o.shape
    return o.transpose(0, 2, 1, 3).reshape(bsz, slen, nh * hd)


def _stick_breaking_attention(q, k, v):
    slen, hd = q.shape[2], q.shape[3]
    scale = hd ** -0.5
    outs = []
    for blk in range(slen // Q_BLOCK):
        q0, q1 = blk * Q_BLOCK, (blk + 1) * Q_BLOCK
        z = jnp.einsum('bhqd,bhkd->bhqk', q[:, :, q0:q1], k[:, :, :q1],
                       preferred_element_type=jnp.float32) * scale
        t_idx = q0 + jnp.arange(Q_BLOCK)[:, None]
        s_idx = jnp.arange(q1)[None, :]
        strict = s_idx < t_idx
        log_keep = jnp.where(strict, jax.nn.log_sigmoid(-z), 0.0)
        suffix = lax.cumsum(log_keep, axis=3, reverse=True) - log_keep
        w = jnp.where(strict, jnp.exp(jax.nn.log_sigmoid(z) + suffix), 0.0)
        outs.append(jnp.einsum('bhqk,bhkd->bhqd', w.astype(v.dtype), v[:, :, :q1]))
    return jnp.concatenate(outs, axis=2)


def _forgetting_attention(q, k, v, log_f):
    slen, hd = q.shape[2], q.shape[3]
    scale = hd ** -0.5
    cum = lax.cumsum(log_f, axis=2)
    outs = []
    for blk in range(slen // Q_BLOCK):
        q0, q1 = blk * Q_BLOCK, (blk + 1) * Q_BLOCK
        z = jnp.einsum('bhqd,bhkd->bhqk', q[:, :, q0:q1], k[:, :, :q1],
                       preferred_element_type=jnp.float32) * scale
        z = z + cum[:, :, q0:q1, None] - cum[:, :, None, :q1]
        causal = jnp.arange(q1)[None, :] <= (q0 + jnp.arange(Q_BLOCK)[:, None])
        p = jax.nn.softmax(jnp.where(causal, z, -jnp.inf), axis=-1)
        outs.append(jnp.einsum('bhqk,bhkd->bhqd', p.astype(v.dtype), v[:, :, :q1]))
    return jnp.concatenate(outs, axis=2)


def _hybrid_mixer(h, w_in, conv_w, conv_b, rg_wa, rg_ba, rg_wx, rg_bx, rg_lam, fox_bf, merge_b,
                  w_rg, w_sb, w_fox, w_o):
    bsz, slen, _ = h.shape
    proj = h @ w_in
    cuts = np.cumsum(IN_SIZES)[:-1].tolist()
    rg_x, rg_gate, sb_qkv, fox_qkv, fox_f, merge = jnp.split(proj, cuts, axis=-1)
    xa = _causal_depthwise_conv(rg_x, conv_w, conv_b)
    ya = jax.nn.gelu(rg_gate) * _rg_lru(xa, rg_wa, rg_ba, rg_wx, rg_bx, rg_lam)
    q_b, k_b, v_b = _split_qkv(sb_qkv, SB_HEADS, SB_HEAD_DIM)
    yb = _merge_heads(_stick_breaking_attention(q_b, k_b, v_b))
    q_c, k_c, v_c = _split_qkv(fox_qkv, FOX_HEADS, FOX_HEAD_DIM)
    log_f = jax.nn.log_sigmoid((fox_f + fox_bf).astype(jnp.float32)).transpose(0, 2, 1)
    yc = _merge_heads(_forgetting_attention(q_c, k_c, v_c, log_f))
    g = jax.nn.sigmoid(merge + merge_b).reshape(bsz, slen, N_BRANCHES, D_MODEL)
    mixed = g[:, :, 0] * (ya @ w_rg) + g[:, :, 1] * (yb @ w_sb) + g[:, :, 2] * (yc @ w_fox)
    return mixed @ w_o


def _fwd_setup_inputs(seed: int = 0) -> dict:
    key = jax.random.key(seed)
    ks = jax.random.split(key, 32)
    f32 = jnp.float32
    L, D = DEPTH, D_MODEL

    def nrm(k, shape, fan_in, mult=1.0):
        return jax.random.normal(k, shape, f32) * (mult * fan_in ** -0.5)

    def gain(k, shape):
        return 1.0 + 0.01 * jax.random.normal(k, shape, f32)

    def small(k, shape):
        return 0.01 * jax.random.normal(k, shape, f32)

    a_c = jax.random.uniform(ks[12], (L, RG_WIDTH), f32, 0.9, 0.999)
    a = a_c ** (1.0 / RG_C)
    rg_lam = jnp.log(a) - jnp.log1p(-a)
    return {
        "x": jax.random.normal(ks[0], (BATCH, SEQ, D), f32),
        "c": jax.random.normal(ks[1], (BATCH, D), f32),
        "ffn1_norm": gain(ks[2], (L, D)),
        "ffn1_w1": nrm(ks[3], (L, D, D_FF), D),
        "ffn1_w3": nrm(ks[4], (L, D, D_FF), D),
        "ffn1_w2": nrm(ks[5], (L, D_FF, D), D_FF),
        "mix_norm": gain(ks[6], (L, D)),
        "w_in": nrm(ks[7], (L, D, N_IN), D),
        "conv_w": nrm(ks[8], (L, CONV_WIDTH, RG_WIDTH), CONV_WIDTH),
        "conv_b": small(ks[9], (L, RG_WIDTH)),
        "rg_wa": nrm(ks[10], (L, RG_BLOCKS, RG_BLOCK_DIM, RG_BLOCK_DIM), RG_BLOCK_DIM),
        "rg_ba": small(ks[11], (L, RG_WIDTH)),
        "rg_wx": nrm(ks[13], (L, RG_BLOCKS, RG_BLOCK_DIM, RG_BLOCK_DIM), RG_BLOCK_DIM),
        "rg_bx": small(ks[14], (L, RG_WIDTH)),
        "rg_lam": rg_lam,
        "fox_bf": jax.random.uniform(ks[15], (L, FOX_HEADS), f32, 2.0, 5.0),
        "merge_b": small(ks[16], (L, N_BRANCHES * D)),
        "w_rg": nrm(ks[17], (L, RG_WIDTH, D), RG_WIDTH),
        "w_sb": nrm(ks[18], (L, SB_WIDTH, D), SB_WIDTH),
        "w_fox": nrm(ks[19], (L, FOX_WIDTH, D), FOX_WIDTH),
        "w_o": nrm(ks[20], (L, D, D), D),
        "ffn2_norm": gain(ks[21], (L, D)),
        "ffn2_w1": nrm(ks[22], (L, D, D_FF), D),
        "ffn2_w3": nrm(ks[23], (L, D, D_FF), D),
        "ffn2_w2": nrm(ks[24], (L, D_FF, D), D_FF),
        "ada_w": nrm(ks[25], (L, D, N_SUBLAYERS * 3 * D), D, 0.1),
        "ada_b": small(ks[26], (L, N_SUBLAYERS * 3 * D)),
        "final_norm": gain(ks[27], (D,)),
        "final_ada_w": nrm(ks[28], (D, 2 * D), D, 0.1),
        "final_ada_b": small(ks[29], (2 * D,)),
    }


def _fwd_reference(x, c, ffn1_norm, ffn1_w1, ffn1_w3, ffn1_w2, mix_norm, w_in, conv_w, conv_b,
              rg_wa, rg_ba, rg_wx, rg_bx, rg_lam, fox_bf, merge_b, w_rg, w_sb, w_fox, w_o,
              ffn2_norm, ffn2_w1, ffn2_w3, ffn2_w2, ada_w, ada_b, final_norm, final_ada_w,
              final_ada_b):
    bsz = x.shape[0]
    c_act = jax.nn.silu(c)
    for l in range(DEPTH):
        mod = (c_act @ ada_w[l] + ada_b[l]).reshape(bsz, N_SUBLAYERS, 3, D_MODEL)
        h = _modulate(_rmsnorm(x, ffn1_norm[l]), mod[:, 0, 0], mod[:, 0, 1])
        x = x + 0.5 * (1.0 + mod[:, 0, 2])[:, None, :] * _swiglu(h, ffn1_w1[l], ffn1_w3[l], ffn1_w2[l])
        h = _modulate(_rmsnorm(x, mix_norm[l]), mod[:, 1, 0], mod[:, 1, 1])
        y = _hybrid_mixer(h, w_in[l], conv_w[l], conv_b[l], rg_wa[l], rg_ba[l], rg_wx[l], rg_bx[l],
                          rg_lam[l], fox_bf[l], merge_b[l], w_rg[l], w_sb[l], w_fox[l], w_o[l])
        x = x + (1.0 + mod[:, 1, 2])[:, None, :] * y
        h = _modulate(_rmsnorm(x, ffn2_norm[l]), mod[:, 2, 0], mod[:, 2, 1])
        x = x + 0.5 * (1.0 + mod[:, 2, 2])[:, None, :] * _swiglu(h, ffn2_w1[l], ffn2_w3[l], ffn2_w2[l])
    fm = (c_act @ final_ada_w + final_ada_b).reshape(bsz, 2, D_MODEL)
    return _modulate(_rmsnorm(x, final_norm), fm[:, 0], fm[:, 1])


import jax as _jax
import jax.numpy as _jnp

TWIN_FORMAT = 'train_step'
FWD_PARAMS = ['x', 'c', 'ffn1_norm', 'ffn1_w1', 'ffn1_w3', 'ffn1_w2', 'mix_norm', 'w_in', 'conv_w', 'conv_b', 'rg_wa', 'rg_ba', 'rg_wx', 'rg_bx', 'rg_lam', 'fox_bf', 'merge_b', 'w_rg', 'w_sb', 'w_fox', 'w_o', 'ffn2_norm', 'ffn2_w1', 'ffn2_w3', 'ffn2_w2', 'ada_w', 'ada_b', 'final_norm', 'final_ada_w', 'final_ada_b']
TWIN_WEIGHTS = ['ffn1_norm', 'ffn1_w1', 'ffn1_w3', 'ffn1_w2', 'mix_norm', 'w_in', 'conv_w', 'conv_b', 'rg_wa', 'rg_ba', 'rg_wx', 'rg_bx', 'rg_lam', 'fox_bf', 'merge_b', 'w_rg', 'w_sb', 'w_fox', 'w_o', 'ffn2_norm', 'ffn2_w1', 'ffn2_w3', 'ffn2_w2', 'ada_w', 'ada_b', 'final_norm', 'final_ada_w', 'final_ada_b']
TWIN_DIFF_INPUT = 'x'
TWIN_INPUTS = ['x', 'c', 'ffn1_norm', 'ffn1_w1', 'ffn1_w3', 'ffn1_w2', 'mix_norm', 'w_in', 'conv_w', 'conv_b', 'rg_wa', 'rg_ba', 'rg_wx', 'rg_bx', 'rg_lam', 'fox_bf', 'merge_b', 'w_rg', 'w_sb', 'w_fox', 'w_o', 'ffn2_norm', 'ffn2_w1', 'ffn2_w3', 'ffn2_w2', 'ada_w', 'ada_b', 'final_norm', 'final_ada_w', 'final_ada_b', 'loss_target', 'm_ffn1_norm', 'm_ffn1_w1', 'm_ffn1_w3', 'm_ffn1_w2', 'm_mix_norm', 'm_w_in', 'm_conv_w', 'm_conv_b', 'm_rg_wa', 'm_rg_ba', 'm_rg_wx', 'm_rg_bx', 'm_rg_lam', 'm_fox_bf', 'm_merge_b', 'm_w_rg', 'm_w_sb', 'm_w_fox', 'm_w_o', 'm_ffn2_norm', 'm_ffn2_w1', 'm_ffn2_w3', 'm_ffn2_w2', 'm_ada_w', 'm_ada_b', 'm_final_norm', 'm_final_ada_w', 'm_final_ada_b', 'v_ffn1_norm', 'v_ffn1_w1', 'v_ffn1_w3', 'v_ffn1_w2', 'v_mix_norm', 'v_w_in', 'v_conv_w', 'v_conv_b', 'v_rg_wa', 'v_rg_ba', 'v_rg_wx', 'v_rg_bx', 'v_rg_lam', 'v_fox_bf', 'v_merge_b', 'v_w_rg', 'v_w_sb', 'v_w_fox', 'v_w_o', 'v_ffn2_norm', 'v_ffn2_w1', 'v_ffn2_w3', 'v_ffn2_w2', 'v_ada_w', 'v_ada_b', 'v_final_norm', 'v_final_ada_w', 'v_final_ada_b']
TWIN_OUTPUTS = ['loss', 'grad_x', 'grad_ffn1_norm', 'grad_ffn1_w1', 'grad_ffn1_w3', 'grad_ffn1_w2', 'grad_mix_norm', 'grad_w_in', 'grad_conv_w', 'grad_conv_b', 'grad_rg_wa', 'grad_rg_ba', 'grad_rg_wx', 'grad_rg_bx', 'grad_rg_lam', 'grad_fox_bf', 'grad_merge_b', 'grad_w_rg', 'grad_w_sb', 'grad_w_fox', 'grad_w_o', 'grad_ffn2_norm', 'grad_ffn2_w1', 'grad_ffn2_w3', 'grad_ffn2_w2', 'grad_ada_w', 'grad_ada_b', 'grad_final_norm', 'grad_final_ada_w', 'grad_final_ada_b', 'delta_ffn1_norm', 'delta_ffn1_w1', 'delta_ffn1_w3', 'delta_ffn1_w2', 'delta_mix_norm', 'delta_w_in', 'delta_conv_w', 'delta_conv_b', 'delta_rg_wa', 'delta_rg_ba', 'delta_rg_wx', 'delta_rg_bx', 'delta_rg_lam', 'delta_fox_bf', 'delta_merge_b', 'delta_w_rg', 'delta_w_sb', 'delta_w_fox', 'delta_w_o', 'delta_ffn2_norm', 'delta_ffn2_w1', 'delta_ffn2_w3', 'delta_ffn2_w2', 'delta_ada_w', 'delta_ada_b', 'delta_final_norm', 'delta_final_ada_w', 'delta_final_ada_b', 'new_m_ffn1_norm', 'new_m_ffn1_w1', 'new_m_ffn1_w3', 'new_m_ffn1_w2', 'new_m_mix_norm', 'new_m_w_in', 'new_m_conv_w', 'new_m_conv_b', 'new_m_rg_wa', 'new_m_rg_ba', 'new_m_rg_wx', 'new_m_rg_bx', 'new_m_rg_lam', 'new_m_fox_bf', 'new_m_merge_b', 'new_m_w_rg', 'new_m_w_sb', 'new_m_w_fox', 'new_m_w_o', 'new_m_ffn2_norm', 'new_m_ffn2_w1', 'new_m_ffn2_w3', 'new_m_ffn2_w2', 'new_m_ada_w', 'new_m_ada_b', 'new_m_final_norm', 'new_m_final_ada_w', 'new_m_final_ada_b', 'new_v_ffn1_norm', 'new_v_ffn1_w1', 'new_v_ffn1_w3', 'new_v_ffn1_w2', 'new_v_mix_norm', 'new_v_w_in', 'new_v_conv_w', 'new_v_conv_b', 'new_v_rg_wa', 'new_v_rg_ba', 'new_v_rg_wx', 'new_v_rg_bx', 'new_v_rg_lam', 'new_v_fox_bf', 'new_v_merge_b', 'new_v_w_rg', 'new_v_w_sb', 'new_v_w_fox', 'new_v_w_o', 'new_v_ffn2_norm', 'new_v_ffn2_w1', 'new_v_ffn2_w3', 'new_v_ffn2_w2', 'new_v_ada_w', 'new_v_ada_b', 'new_v_final_norm', 'new_v_final_ada_w', 'new_v_final_ada_b']
TWIN_LEAF_KINDS = {'loss': 'loss', 'grad_x': 'grad_x', 'grad_ffn1_norm': 'grad_w', 'grad_ffn1_w1': 'grad_w', 'grad_ffn1_w3': 'grad_w', 'grad_ffn1_w2': 'grad_w', 'grad_mix_norm': 'grad_w', 'grad_w_in': 'grad_w', 'grad_conv_w': 'grad_w', 'grad_conv_b': 'grad_w', 'grad_rg_wa': 'grad_w', 'grad_rg_ba': 'grad_w', 'grad_rg_wx': 'grad_w', 'grad_rg_bx': 'grad_w', 'grad_rg_lam': 'grad_w', 'grad_fox_bf': 'grad_w', 'grad_merge_b': 'grad_w', 'grad_w_rg': 'grad_w', 'grad_w_sb': 'grad_w', 'grad_w_fox': 'grad_w', 'grad_w_o': 'grad_w', 'grad_ffn2_norm': 'grad_w', 'grad_ffn2_w1': 'grad_w', 'grad_ffn2_w3': 'grad_w', 'grad_ffn2_w2': 'grad_w', 'grad_ada_w': 'grad_w', 'grad_ada_b': 'grad_w', 'grad_final_norm': 'grad_w', 'grad_final_ada_w': 'grad_w', 'grad_final_ada_b': 'grad_w', 'delta_ffn1_norm': 'delta_w', 'delta_ffn1_w1': 'delta_w', 'delta_ffn1_w3': 'delta_w', 'delta_ffn1_w2': 'delta_w', 'delta_mix_norm': 'delta_w', 'delta_w_in': 'delta_w', 'delta_conv_w': 'delta_w', 'delta_conv_b': 'delta_w', 'delta_rg_wa': 'delta_w', 'delta_rg_ba': 'delta_w', 'delta_rg_wx': 'delta_w', 'delta_rg_bx': 'delta_w', 'delta_rg_lam': 'delta_w', 'delta_fox_bf': 'delta_w', 'delta_merge_b': 'delta_w', 'delta_w_rg': 'delta_w', 'delta_w_sb': 'delta_w', 'delta_w_fox': 'delta_w', 'delta_w_o': 'delta_w', 'delta_ffn2_norm': 'delta_w', 'delta_ffn2_w1': 'delta_w', 'delta_ffn2_w3': 'delta_w', 'delta_ffn2_w2': 'delta_w', 'delta_ada_w': 'delta_w', 'delta_ada_b': 'delta_w', 'delta_final_norm': 'delta_w', 'delta_final_ada_w': 'delta_w', 'delta_final_ada_b': 'delta_w', 'new_m_ffn1_norm': 'new_m', 'new_m_ffn1_w1': 'new_m', 'new_m_ffn1_w3': 'new_m', 'new_m_ffn1_w2': 'new_m', 'new_m_mix_norm': 'new_m', 'new_m_w_in': 'new_m', 'new_m_conv_w': 'new_m', 'new_m_conv_b': 'new_m', 'new_m_rg_wa': 'new_m', 'new_m_rg_ba': 'new_m', 'new_m_rg_wx': 'new_m', 'new_m_rg_bx': 'new_m', 'new_m_rg_lam': 'new_m', 'new_m_fox_bf': 'new_m', 'new_m_merge_b': 'new_m', 'new_m_w_rg': 'new_m', 'new_m_w_sb': 'new_m', 'new_m_w_fox': 'new_m', 'new_m_w_o': 'new_m', 'new_m_ffn2_norm': 'new_m', 'new_m_ffn2_w1': 'new_m', 'new_m_ffn2_w3': 'new_m', 'new_m_ffn2_w2': 'new_m', 'new_m_ada_w': 'new_m', 'new_m_ada_b': 'new_m', 'new_m_final_norm': 'new_m', 'new_m_final_ada_w': 'new_m', 'new_m_final_ada_b': 'new_m', 'new_v_ffn1_norm': 'new_v', 'new_v_ffn1_w1': 'new_v', 'new_v_ffn1_w3': 'new_v', 'new_v_ffn1_w2': 'new_v', 'new_v_mix_norm': 'new_v', 'new_v_w_in': 'new_v', 'new_v_conv_w': 'new_v', 'new_v_conv_b': 'new_v', 'new_v_rg_wa': 'new_v', 'new_v_rg_ba': 'new_v', 'new_v_rg_wx': 'new_v', 'new_v_rg_bx': 'new_v', 'new_v_rg_lam': 'new_v', 'new_v_fox_bf': 'new_v', 'new_v_merge_b': 'new_v', 'new_v_w_rg': 'new_v', 'new_v_w_sb': 'new_v', 'new_v_w_fox': 'new_v', 'new_v_w_o': 'new_v', 'new_v_ffn2_norm': 'new_v', 'new_v_ffn2_w1': 'new_v', 'new_v_ffn2_w3': 'new_v', 'new_v_ffn2_w2': 'new_v', 'new_v_ada_w': 'new_v', 'new_v_ada_b': 'new_v', 'new_v_final_norm': 'new_v', 'new_v_final_ada_w': 'new_v', 'new_v_final_ada_b': 'new_v'}


def _forward(args):
    return _fwd_reference(*[args[k] for k in FWD_PARAMS])


def _output_shape():
    out = _jax.eval_shape(lambda: _forward(_fwd_setup_inputs(0)))
    return out.shape, out.dtype

N_MICROBATCH = 1
ADAM_LR = 0.001
ADAM_B1 = 0.9
ADAM_B2 = 0.999
ADAM_EPS = 1e-08
ADAM_WD = 0.01
ADAM_STEP = 10
PER_EXAMPLE_BATCH_AXIS = {'x': 0, 'c': 0, 'loss_target': 0}
SHARED_INPUTS = []
_WEIGHT_DTYPES = {'ffn1_norm': _jnp.float32, 'ffn1_w1': _jnp.float32, 'ffn1_w3': _jnp.float32, 'ffn1_w2': _jnp.float32, 'mix_norm': _jnp.float32, 'w_in': _jnp.float32, 'conv_w': _jnp.float32, 'conv_b': _jnp.float32, 'rg_wa': _jnp.float32, 'rg_ba': _jnp.float32, 'rg_wx': _jnp.float32, 'rg_bx': _jnp.float32, 'rg_lam': _jnp.float32, 'fox_bf': _jnp.float32, 'merge_b': _jnp.float32, 'w_rg': _jnp.float32, 'w_sb': _jnp.float32, 'w_fox': _jnp.float32, 'w_o': _jnp.float32, 'ffn2_norm': _jnp.float32, 'ffn2_w1': _jnp.float32, 'ffn2_w3': _jnp.float32, 'ffn2_w2': _jnp.float32, 'ada_w': _jnp.float32, 'ada_b': _jnp.float32, 'final_norm': _jnp.float32, 'final_ada_w': _jnp.float32, 'final_ada_b': _jnp.float32}
MOMENT_SCALE = {'ffn1_norm': 8.132552e-02, 'ffn1_w1': 3.489236e-02, 'ffn1_w3': 3.414100e-02, 'ffn1_w2': 5.668465e-02, 'mix_norm': 1.152283e-01, 'w_in': 4.199293e-02, 'conv_w': 8.319628e-02, 'conv_b': 1.020948e+00, 'rg_wa': 3.041729e-02, 'rg_ba': 2.360873e-02, 'rg_wx': 5.703871e-02, 'rg_bx': 2.905689e-02, 'rg_lam': 4.338345e-02, 'fox_bf': 3.485919e-01, 'merge_b': 2.369493e-02, 'w_rg': 8.201745e-02, 'w_sb': 6.115180e-02, 'w_fox': 2.801241e-02, 'w_o': 9.962562e-02, 'ffn2_norm': 6.905695e-02, 'ffn2_w1': 2.873827e-02, 'ffn2_w3': 2.787435e-02, 'ffn2_w2': 4.624490e-02, 'ada_w': 1.939266e-01, 'ada_b': 3.680223e-01, 'final_norm': 3.233654e+01, 'final_ada_w': 8.234653e+00, 'final_ada_b': 2.281167e+01}


def _to_microbatches(a, axis):
    t = _jnp.moveaxis(a, axis, 0)
    t = t.reshape((N_MICROBATCH, t.shape[0] // N_MICROBATCH) + t.shape[1:])
    return _jnp.moveaxis(t, 1, axis + 1)


def setup_inputs(seed: int = 0) -> dict:
    inp = _fwd_setup_inputs(seed)
    key = _jax.random.fold_in(_jax.random.key(seed), 7919)
    shape, _ = _output_shape()
    out = dict(inp)
    out["loss_target"] = _jax.random.normal(_jax.random.fold_in(key, 0), shape, _jnp.float32)
    for i, name in enumerate(TWIN_WEIGHTS):
        w = inp[name].astype(_jnp.float32)
        if MOMENT_SCALE is None:
            s = _jnp.sqrt(_jnp.mean(_jnp.square(w)) + 1e-30)
        else:
            s = MOMENT_SCALE[name]
        km, kv = _jax.random.split(_jax.random.fold_in(key, i + 1))
        out[name] = w
        out["m_" + name] = s * _jax.random.normal(km, w.shape, _jnp.float32)
        out["v_" + name] = (s * s) * _jax.random.uniform(kv, w.shape, _jnp.float32, 0.5, 1.5)
    if N_MICROBATCH > 1:
        for name, axis in PER_EXAMPLE_BATCH_AXIS.items():
            out[name] = _to_microbatches(out[name], axis)
    return {'x': out['x'], 'c': out['c'], 'ffn1_norm': out['ffn1_norm'], 'ffn1_w1': out['ffn1_w1'], 'ffn1_w3': out['ffn1_w3'], 'ffn1_w2': out['ffn1_w2'], 'mix_norm': out['mix_norm'], 'w_in': out['w_in'], 'conv_w': out['conv_w'], 'conv_b': out['conv_b'], 'rg_wa': out['rg_wa'], 'rg_ba': out['rg_ba'], 'rg_wx': out['rg_wx'], 'rg_bx': out['rg_bx'], 'rg_lam': out['rg_lam'], 'fox_bf': out['fox_bf'], 'merge_b': out['merge_b'], 'w_rg': out['w_rg'], 'w_sb': out['w_sb'], 'w_fox': out['w_fox'], 'w_o': out['w_o'], 'ffn2_norm': out['ffn2_norm'], 'ffn2_w1': out['ffn2_w1'], 'ffn2_w3': out['ffn2_w3'], 'ffn2_w2': out['ffn2_w2'], 'ada_w': out['ada_w'], 'ada_b': out['ada_b'], 'final_norm': out['final_norm'], 'final_ada_w': out['final_ada_w'], 'final_ada_b': out['final_ada_b'], 'loss_target': out['loss_target'], 'm_ffn1_norm': out['m_ffn1_norm'], 'm_ffn1_w1': out['m_ffn1_w1'], 'm_ffn1_w3': out['m_ffn1_w3'], 'm_ffn1_w2': out['m_ffn1_w2'], 'm_mix_norm': out['m_mix_norm'], 'm_w_in': out['m_w_in'], 'm_conv_w': out['m_conv_w'], 'm_conv_b': out['m_conv_b'], 'm_rg_wa': out['m_rg_wa'], 'm_rg_ba': out['m_rg_ba'], 'm_rg_wx': out['m_rg_wx'], 'm_rg_bx': out['m_rg_bx'], 'm_rg_lam': out['m_rg_lam'], 'm_fox_bf': out['m_fox_bf'], 'm_merge_b': out['m_merge_b'], 'm_w_rg': out['m_w_rg'], 'm_w_sb': out['m_w_sb'], 'm_w_fox': out['m_w_fox'], 'm_w_o': out['m_w_o'], 'm_ffn2_norm': out['m_ffn2_norm'], 'm_ffn2_w1': out['m_ffn2_w1'], 'm_ffn2_w3': out['m_ffn2_w3'], 'm_ffn2_w2': out['m_ffn2_w2'], 'm_ada_w': out['m_ada_w'], 'm_ada_b': out['m_ada_b'], 'm_final_norm': out['m_final_norm'], 'm_final_ada_w': out['m_final_ada_w'], 'm_final_ada_b': out['m_final_ada_b'], 'v_ffn1_norm': out['v_ffn1_norm'], 'v_ffn1_w1': out['v_ffn1_w1'], 'v_ffn1_w3': out['v_ffn1_w3'], 'v_ffn1_w2': out['v_ffn1_w2'], 'v_mix_norm': out['v_mix_norm'], 'v_w_in': out['v_w_in'], 'v_conv_w': out['v_conv_w'], 'v_conv_b': out['v_conv_b'], 'v_rg_wa': out['v_rg_wa'], 'v_rg_ba': out['v_rg_ba'], 'v_rg_wx': out['v_rg_wx'], 'v_rg_bx': out['v_rg_bx'], 'v_rg_lam': out['v_rg_lam'], 'v_fox_bf': out['v_fox_bf'], 'v_merge_b': out['v_merge_b'], 'v_w_rg': out['v_w_rg'], 'v_w_sb': out['v_w_sb'], 'v_w_fox': out['v_w_fox'], 'v_w_o': out['v_w_o'], 'v_ffn2_norm': out['v_ffn2_norm'], 'v_ffn2_w1': out['v_ffn2_w1'], 'v_ffn2_w3': out['v_ffn2_w3'], 'v_ffn2_w2': out['v_ffn2_w2'], 'v_ada_w': out['v_ada_w'], 'v_ada_b': out['v_ada_b'], 'v_final_norm': out['v_final_norm'], 'v_final_ada_w': out['v_final_ada_w'], 'v_final_ada_b': out['v_final_ada_b']}


def _loss(weights, diff, rest, loss_target):
    with _jax.named_scope("forward"):
        args = {**rest, TWIN_DIFF_INPUT: diff, **{k: w.astype(_WEIGHT_DTYPES[k]) for k, w in weights.items()}}
        y = _forward(args)
    with _jax.named_scope("loss_head"):
        err = _jnp.square(y.astype(_jnp.float32) - loss_target)
        return 0.5 * _jnp.sum(_jnp.mean(err, axis=-1)) if err.ndim else 0.5 * err


def _adamw(w, g, m, v):
    m = ADAM_B1 * m + (1.0 - ADAM_B1) * g
    v = ADAM_B2 * v + (1.0 - ADAM_B2) * _jnp.square(g)
    m_hat = m / (1.0 - ADAM_B1 ** ADAM_STEP)
    v_hat = v / (1.0 - ADAM_B2 ** ADAM_STEP)
    delta = -ADAM_LR * (m_hat / (_jnp.sqrt(v_hat) + ADAM_EPS) + ADAM_WD * w)
    return delta, m, v


def reference(x, c, ffn1_norm, ffn1_w1, ffn1_w3, ffn1_w2, mix_norm, w_in, conv_w, conv_b, rg_wa, rg_ba, rg_wx, rg_bx, rg_lam, fox_bf, merge_b, w_rg, w_sb, w_fox, w_o, ffn2_norm, ffn2_w1, ffn2_w3, ffn2_w2, ada_w, ada_b, final_norm, final_ada_w, final_ada_b, loss_target, m_ffn1_norm, m_ffn1_w1, m_ffn1_w3, m_ffn1_w2, m_mix_norm, m_w_in, m_conv_w, m_conv_b, m_rg_wa, m_rg_ba, m_rg_wx, m_rg_bx, m_rg_lam, m_fox_bf, m_merge_b, m_w_rg, m_w_sb, m_w_fox, m_w_o, m_ffn2_norm, m_ffn2_w1, m_ffn2_w3, m_ffn2_w2, m_ada_w, m_ada_b, m_final_norm, m_final_ada_w, m_final_ada_b, v_ffn1_norm, v_ffn1_w1, v_ffn1_w3, v_ffn1_w2, v_mix_norm, v_w_in, v_conv_w, v_conv_b, v_rg_wa, v_rg_ba, v_rg_wx, v_rg_bx, v_rg_lam, v_fox_bf, v_merge_b, v_w_rg, v_w_sb, v_w_fox, v_w_o, v_ffn2_norm, v_ffn2_w1, v_ffn2_w3, v_ffn2_w2, v_ada_w, v_ada_b, v_final_norm, v_final_ada_w, v_final_ada_b):
    given = dict(x=x, c=c, ffn1_norm=ffn1_norm, ffn1_w1=ffn1_w1, ffn1_w3=ffn1_w3, ffn1_w2=ffn1_w2, mix_norm=mix_norm, w_in=w_in, conv_w=conv_w, conv_b=conv_b, rg_wa=rg_wa, rg_ba=rg_ba, rg_wx=rg_wx, rg_bx=rg_bx, rg_lam=rg_lam, fox_bf=fox_bf, merge_b=merge_b, w_rg=w_rg, w_sb=w_sb, w_fox=w_fox, w_o=w_o, ffn2_norm=ffn2_norm, ffn2_w1=ffn2_w1, ffn2_w3=ffn2_w3, ffn2_w2=ffn2_w2, ada_w=ada_w, ada_b=ada_b, final_norm=final_norm, final_ada_w=final_ada_w, final_ada_b=final_ada_b, loss_target=loss_target, m_ffn1_norm=m_ffn1_norm, m_ffn1_w1=m_ffn1_w1, m_ffn1_w3=m_ffn1_w3, m_ffn1_w2=m_ffn1_w2, m_mix_norm=m_mix_norm, m_w_in=m_w_in, m_conv_w=m_conv_w, m_conv_b=m_conv_b, m_rg_wa=m_rg_wa, m_rg_ba=m_rg_ba, m_rg_wx=m_rg_wx, m_rg_bx=m_rg_bx, m_rg_lam=m_rg_lam, m_fox_bf=m_fox_bf, m_merge_b=m_merge_b, m_w_rg=m_w_rg, m_w_sb=m_w_sb, m_w_fox=m_w_fox, m_w_o=m_w_o, m_ffn2_norm=m_ffn2_norm, m_ffn2_w1=m_ffn2_w1, m_ffn2_w3=m_ffn2_w3, m_ffn2_w2=m_ffn2_w2, m_ada_w=m_ada_w, m_ada_b=m_ada_b, m_final_norm=m_final_norm, m_final_ada_w=m_final_ada_w, m_final_ada_b=m_final_ada_b, v_ffn1_norm=v_ffn1_norm, v_ffn1_w1=v_ffn1_w1, v_ffn1_w3=v_ffn1_w3, v_ffn1_w2=v_ffn1_w2, v_mix_norm=v_mix_norm, v_w_in=v_w_in, v_conv_w=v_conv_w, v_conv_b=v_conv_b, v_rg_wa=v_rg_wa, v_rg_ba=v_rg_ba, v_rg_wx=v_rg_wx, v_rg_bx=v_rg_bx, v_rg_lam=v_rg_lam, v_fox_bf=v_fox_bf, v_merge_b=v_merge_b, v_w_rg=v_w_rg, v_w_sb=v_w_sb, v_w_fox=v_w_fox, v_w_o=v_w_o, v_ffn2_norm=v_ffn2_norm, v_ffn2_w1=v_ffn2_w1, v_ffn2_w3=v_ffn2_w3, v_ffn2_w2=v_ffn2_w2, v_ada_w=v_ada_w, v_ada_b=v_ada_b, v_final_norm=v_final_norm, v_final_ada_w=v_final_ada_w, v_final_ada_b=v_final_ada_b)
    weights = {n: given[n] for n in TWIN_WEIGHTS}
    shared = {n: given[n] for n in SHARED_INPUTS}
    per_example = {n: given[n] for n in ['x', 'c']}
    grad_fn = _jax.value_and_grad(_loss, argnums=(0, 1))

    def one_microbatch(ex, loss_target):
        ex = dict(ex)
        diff = ex.pop(TWIN_DIFF_INPUT)
        return grad_fn(weights, diff, {**shared, **ex}, loss_target)

    if N_MICROBATCH == 1:
        loss, (grad_w, grad_x) = one_microbatch(per_example, given["loss_target"])
    else:
        def body(carry, xs):
            loss_sum, grad_sum = carry
            l_k, (gw_k, gx_k) = one_microbatch(xs[0], xs[1])
            with _jax.named_scope("update"):
                return (loss_sum + l_k, _jax.tree.map(_jnp.add, grad_sum, gw_k)), gx_k

        init = (_jnp.zeros((), _jnp.float32), _jax.tree.map(_jnp.zeros_like, weights))
        (loss, grad_w), grad_x = _jax.lax.scan(body, init, (per_example, given["loss_target"]))
    with _jax.named_scope("update"):
        delta_w, new_m, new_v = {}, {}, {}
        for n in TWIN_WEIGHTS:
            delta_w[n], new_m[n], new_v[n] = _adamw(weights[n], grad_w[n], given["m_" + n], given["v_" + n])
    return (loss, grad_x, *[grad_w[n] for n in TWIN_WEIGHTS], *[delta_w[n] for n in TWIN_WEIGHTS],
            *[new_m[n] for n in TWIN_WEIGHTS], *[new_v[n] for n in TWIN_WEIGHTS])
```

```python
import functools
import math

import jax
import jax.numpy as jnp
from jax import lax
from jax.experimental import pallas as pl
from jax.experimental.pallas import tpu as pltpu

F32 = jnp.float32
BF16 = jnp.bfloat16

NUM_CHIPS = 4
NUM_DEVICES = 8
HEAD_DIM = 64
LANE = 128
SUBLANE = 8
VMEM_LIMIT = 56 * 1024 * 1024
EPS = 1e-6
RG_C = 8.0
ADAM_LR = 0.001
ADAM_B1 = 0.9
ADAM_B2 = 0.999
ADAM_EPS = 1e-08
ADAM_WD = 0.01
ADAM_STEP = 10
MESH = pl.DeviceIdType.MESH
ANY = pl.BlockSpec(memory_space=pl.ANY)


def _params(n_grid=0):
    return pltpu.CompilerParams(vmem_limit_bytes=VMEM_LIMIT)


def _tile(dim, pref):
    if dim <= pref:
        return dim
    t = (pref // LANE) * LANE
    while t >= LANE:
        if dim % t == 0:
            return t
        t -= LANE
    return dim


def _sigmoid(x):
    return 1.0 / (1.0 + jnp.exp(-x))


def _softplus(x):
    return jnp.maximum(x, 0.0) + jnp.log(1.0 + jnp.exp(-jnp.abs(x)))


def _expm1(x):
    small = x * (1.0 + x * (0.5 + x * (1.0 / 6.0 + x * (1.0 / 24.0))))
    return jnp.where(jnp.abs(x) < 0.01, small, jnp.exp(x) - 1.0)


_GELU_K = math.sqrt(2.0 / math.pi)


def _gelu_and_grad(x):
    inner = _GELU_K * (x + 0.044715 * x * x * x)
    t = jnp.tanh(inner)
    val = 0.5 * x * (1.0 + t)
    dinner = _GELU_K * (1.0 + 3.0 * 0.044715 * x * x)
    grad = 0.5 * (1.0 + t) + 0.5 * x * (1.0 - t * t) * dinner
    return val, grad


NN = ((1,), (0,))
NT = ((1,), (1,))
TN = ((0,), (0,))


def _matmul(name, a, b, out_shape, out_dtype, grid, a_spec, b_spec, o_spec, dims, acc=None):
    nk = grid[-1]
    kax = len(grid) - 1
    o_block = tuple(d for d in o_spec.block_shape if d is not None)

    def body(*refs):
        if acc is None:
            a_ref, b_ref, o_ref = refs[:3]
            c_ref = None
        else:
            a_ref, b_ref, c_ref, o_ref = refs[:4]
        p = lax.dot_general(a_ref[...], b_ref[...], (dims, ((), ())), preferred_element_type=F32)
        if nk == 1:
            if c_ref is not None:
                p = p + c_ref[...].astype(F32)
            o_ref[...] = p.astype(o_ref.dtype)
            return
        acc_ref = refs[-1]
        k = pl.program_id(kax)

        @pl.when(k == 0)
        def _():
            acc_ref[...] = p if c_ref is None else p + c_ref[...].astype(F32)

        @pl.when(k > 0)
        def _():
            acc_ref[...] += p

        @pl.when(k == nk - 1)
        def _():
            o_ref[...] = acc_ref[...].astype(o_ref.dtype)

    in_specs = [a_spec, b_spec]
    args = [a, b]
    if acc is not None:
        in_specs.append(pl.BlockSpec(o_spec.block_shape, o_spec.index_map))
        args.append(acc)
    return pl.pallas_call(
        body, name=name, grid=grid, in_specs=in_specs, out_specs=o_spec,
        out_shape=jax.ShapeDtypeStruct(out_shape, out_dtype),
        scratch_shapes=[pltpu.VMEM(o_block, F32)] if nk > 1 else [],
        compiler_params=_params(),
    )(*args)


def _mm(name, a, b, dims, out_dtype, acc=None, tm=512, tn=512, tk=2048):
    if dims == NN:
        (m, kk), n = a.shape, b.shape[1]
    elif dims == NT:
        (m, kk), n = a.shape, b.shape[0]
    else:
        (kk, m), n = a.shape, b.shape[1]
    tm, tn, tk = _tile(m, tm), _tile(n, tn), _tile(kk, tk)
    grid = (m // tm, n // tn, kk // tk)
    if dims == NN:
        a_spec = pl.BlockSpec((tm, tk), lambda i, j, k: (i, k))
        b_spec = pl.BlockSpec((tk, tn), lambda i, j, k: (k, j))
    elif dims == NT:
        a_spec = pl.BlockSpec((tm, tk), lambda i, j, k: (i, k))
        b_spec = pl.BlockSpec((tn, tk), lambda i, j, k: (j, k))
    else:
        a_spec = pl.BlockSpec((tk, tm), lambda i, j, k: (k, i))
        b_spec = pl.BlockSpec((tk, tn), lambda i, j, k: (k, j))
    o_spec = pl.BlockSpec((tm, tn), lambda i, j, k: (i, j))
    return _matmul(name, a, b, (m, n), out_dtype, grid, a_spec, b_spec, o_spec, dims, acc)


def _row_grid(t_rows, seq, pref=256):
    tm = _tile(seq, pref)
    return tm, seq // tm


def _normmod(name, x, gain, shift, scale, seq):
    t_rows, d = x.shape
    bl = t_rows // seq
    tm, per = _row_grid(t_rows, seq)

    def body(x_ref, g_ref, sh_ref, sc_ref, o_ref):
        xv = x_ref[...]
        rstd = lax.rsqrt(jnp.mean(xv * xv, axis=-1, keepdims=True) + EPS)
        hn = (xv * rstd) * g_ref[...]
        o_ref[...] = (hn * (1.0 + sc_ref[...]) + sh_ref[...]).astype(o_ref.dtype)

    row = pl.BlockSpec((tm, d), lambda b, i: (b * per + i, 0))
    vec = pl.BlockSpec((None, 1, d), lambda b, i: (b, 0, 0))
    return pl.pallas_call(
        body, name=name, grid=(bl, per),
        in_specs=[row, pl.BlockSpec((1, d), lambda b, i: (0, 0)), vec, vec],
        out_specs=row, out_shape=jax.ShapeDtypeStruct((t_rows, d), BF16),
        compiler_params=_params(),
    )(x, gain, shift, scale)


def _normmod_bwd(name, x, dh, dxo, gain, scale, seq):
    t_rows, d = x.shape
    bl = t_rows // seq
    tm, per = _row_grid(t_rows, seq)

    def body(x_ref, dh_ref, dxo_ref, g_ref, sc_ref, dx_ref, dsh_ref, dsc_ref, dg_ref):
        b, i = pl.program_id(0), pl.program_id(1)
        xv = x_ref[...]
        dhv = dh_ref[...]
        rstd = lax.rsqrt(jnp.mean(xv * xv, axis=-1, keepdims=True) + EPS)
        xhat = xv * rstd
        gain_v = g_ref[...]
        dhn = dhv * (1.0 + sc_ref[...])
        dxhat = dhn * gain_v
        dx = rstd * (dxhat - xhat * jnp.mean(dxhat * xhat, axis=-1, keepdims=True))
        dx_ref[...] = dxo_ref[...] + dx

        @pl.when(i == 0)
        def _():
            dsh_ref[...] = jnp.zeros_like(dsh_ref)
            dsc_ref[...] = jnp.zeros_like(dsc_ref)

        @pl.when((i == 0) & (b == 0))
        def _():
            dg_ref[...] = jnp.zeros_like(dg_ref)

        dsh_ref[...] += jnp.sum(dhv, axis=0, keepdims=True)
        dsc_ref[...] += jnp.sum(dhv * (xhat * gain_v), axis=0, keepdims=True)
        dg_ref[...] += jnp.sum(dhn * xhat, axis=0, keepdims=True)

    row = pl.BlockSpec((tm, d), lambda b, i: (b * per + i, 0))
    vec = pl.BlockSpec((None, 1, d), lambda b, i: (b, 0, 0))
    one = pl.BlockSpec((1, d), lambda b, i: (0, 0))
    return pl.pallas_call(
        body, name=name, grid=(bl, per),
        in_specs=[row, row, row, one, vec],
        out_specs=[row, vec, vec, one],
        out_shape=[jax.ShapeDtypeStruct((t_rows, d), F32), jax.ShapeDtypeStruct((bl, 1, d), F32),
                   jax.ShapeDtypeStruct((bl, 1, d), F32), jax.ShapeDtypeStruct((1, d), F32)],
        compiler_params=_params(),
    )(x, dh, dxo, gain, scale)


def _resid(name, x, y, gate, coef, seq):
    t_rows, d = x.shape
    bl = t_rows // seq
    tm, per = _row_grid(t_rows, seq)

    def body(x_ref, y_ref, g_ref, o_ref):
        o_ref[...] = x_ref[...] + (coef * (1.0 + g_ref[...])) * y_ref[...]

    row = pl.BlockSpec((tm, d), lambda b, i: (b * per + i, 0))
    vec = pl.BlockSpec((None, 1, d), lambda b, i: (b, 0, 0))
    return pl.pallas_call(
        body, name=name, grid=(bl, per), in_specs=[row, row, vec], out_specs=row,
        out_shape=jax.ShapeDtypeStruct((t_rows, d), F32), compiler_params=_params(),
    )(x, y, gate)


def _resid_bwd(name, dxo, y, gate, coef, seq):
    t_rows, d = dxo.shape
    bl = t_rows // seq
    tm, per = _row_grid(t_rows, seq)

    def body(dxo_ref, y_ref, g_ref, dy_ref, dg_ref):
        i = pl.program_id(1)
        dxov = dxo_ref[...]
        dy_ref[...] = ((coef * (1.0 + g_ref[...])) * dxov).astype(dy_ref.dtype)

        @pl.when(i == 0)
        def _():
            dg_ref[...] = jnp.zeros_like(dg_ref)

        dg_ref[...] += jnp.sum((coef * y_ref[...]) * dxov, axis=0, keepdims=True)

    row = pl.BlockSpec((tm, d), lambda b, i: (b * per + i, 0))
    vec = pl.BlockSpec((None, 1, d), lambda b, i: (b, 0, 0))
    return pl.pallas_call(
        body, name=name, grid=(bl, per), in_specs=[row, row, vec], out_specs=[row, vec],
        out_shape=[jax.ShapeDtypeStruct((t_rows, d), BF16), jax.ShapeDtypeStruct((bl, 1, d), F32)],
        compiler_params=_params(),
    )(dxo, y, gate)


def _final_loss(name, x, tgt, gain, shift, scale, seq):
    t_rows, d = x.shape
    bl = t_rows // seq
    tm, per = _row_grid(t_rows, seq)

    def body(x_ref, t_ref, g_ref, sh_ref, sc_ref, l_ref, dx_ref, dsh_ref, dsc_ref, dg_ref):
        b, i = pl.program_id(0), pl.program_id(1)
        xv = x_ref[...]
        rstd = lax.rsqrt(jnp.mean(xv * xv, axis=-1, keepdims=True) + EPS)
        xhat = xv * rstd
        gain_v = g_ref[...]
        hn = xhat * gain_v
        yv = hn * (1.0 + sc_ref[...]) + sh_ref[...]
        err = yv - t_ref[...]
        dyv = err * (1.0 / d)
        dhn = dyv * (1.0 + sc_ref[...])
        dxhat = dhn * gain_v
        dx_ref[...] = rstd * (dxhat - xhat * jnp.mean(dxhat * xhat, axis=-1, keepdims=True))

        @pl.when(i == 0)
        def _():
            l_ref[...] = jnp.zeros_like(l_ref)
            dsh_ref[...] = jnp.zeros_like(dsh_ref)
            dsc_ref[...] = jnp.zeros_like(dsc_ref)

        @pl.when((i == 0) & (b == 0))
        def _():
            dg_ref[...] = jnp.zeros_like(dg_ref)

        part = jnp.sum(jnp.sum(err * err, axis=-1, keepdims=True), axis=0, keepdims=True) * (0.5 / d)
        l_ref[...] += jnp.broadcast_to(part, l_ref.shape)
        dsh_ref[...] += jnp.sum(dyv, axis=0, keepdims=True)
        dsc_ref[...] += jnp.sum(dyv * hn, axis=0, keepdims=True)
        dg_ref[...] += jnp.sum(dhn * xhat, axis=0, keepdims=True)

    row = pl.BlockSpec((tm, d), lambda b, i: (b * per + i, 0))
    vec = pl.BlockSpec((None, 1, d), lambda b, i: (b, 0, 0))
    one = pl.BlockSpec((1, d), lambda b, i: (0, 0))
    lvec = pl.BlockSpec((None, 1, LANE), lambda b, i: (b, 0, 0))
    return pl.pallas_call(
        body, name=name, grid=(bl, per),
        in_specs=[row, row, one, vec, vec],
        out_specs=[lvec, row, vec, vec, one],
        out_shape=[jax.ShapeDtypeStruct((bl, 1, LANE), F32), jax.ShapeDtypeStruct((t_rows, d), F32),
                   jax.ShapeDtypeStruct((bl, 1, d), F32), jax.ShapeDtypeStruct((bl, 1, d), F32),
                   jax.ShapeDtypeStruct((1, d), F32)],
        compiler_params=_params(),
    )(x, tgt, gain, shift, scale)


def _swiglu(name, a, b):
    g, t_rows, fs = a.shape
    a2, b2 = a.reshape(g * t_rows, fs), b.reshape(g * t_rows, fs)
    tm = _tile(t_rows, 512)

    def body(a_ref, b_ref, o_ref):
        av = a_ref[...]
        o_ref[...] = (av * _sigmoid(av) * b_ref[...]).astype(o_ref.dtype)

    spec = pl.BlockSpec((tm, fs), lambda i: (i, 0))
    out = pl.pallas_call(
        body, name=name, grid=(g * t_rows // tm,), in_specs=[spec, spec], out_specs=spec,
        out_shape=jax.ShapeDtypeStruct((g * t_rows, fs), BF16), compiler_params=_params(),
    )(a2, b2)
    return out.reshape(g, t_rows, fs)


def _swiglu_bwd(name, dg, a, b):
    g, t_rows, fs = a.shape
    flat = lambda v: v.reshape(g * t_rows, fs)
    tm = _tile(t_rows, 512)

    def body(dg_ref, a_ref, b_ref, da_ref, db_ref):
        av, dgv = a_ref[...], dg_ref[...]
        sig = _sigmoid(av)
        da_ref[...] = (dgv * b_ref[...] * (sig * (1.0 + av * (1.0 - sig)))).astype(da_ref.dtype)
        db_ref[...] = (dgv * (av * sig)).astype(db_ref.dtype)

    spec = pl.BlockSpec((tm, fs), lambda i: (i, 0))
    da, db = pl.pallas_call(
        body, name=name, grid=(g * t_rows // tm,), in_specs=[spec, spec, spec], out_specs=[spec, spec],
        out_shape=[jax.ShapeDtypeStruct((g * t_rows, fs), BF16)] * 2, compiler_params=_params(),
    )(flat(dg), flat(a), flat(b))
    return da.reshape(g, t_rows, fs), db.reshape(g, t_rows, fs)


def _ffn_fwd(tag, w, l, pre, x, mod, seq):
    t_rows, d = x.shape
    w1, w3, w2 = w[pre + "w1"], w[pre + "w3"], w[pre + "w2"]
    ng, fs = w1.shape[0], w1.shape[3]
    shift, scale, gate = mod
    h = _normmod(tag + "_norm", x, w[pre + "norm"][l][None], shift, scale, seq)
    tm = _tile(t_rows, 512)

    def up(name, wg):
        return _matmul(
            name, h, wg, (ng, t_rows, fs), F32, (ng, t_rows // tm, 1),
            pl.BlockSpec((tm, d), lambda g, i, k: (i, 0)),
            pl.BlockSpec((None, None, d, fs), lambda g, i, k: (g, l, 0, 0)),
            pl.BlockSpec((None, tm, fs), lambda g, i, k: (g, i, 0)), NN)

    a = up(tag + "_up1", w1)
    b = up(tag + "_up3", w3)
    gact = _swiglu(tag + "_act", a, b)
    tn = _tile(d, 512)
    y = _matmul(
        tag + "_down", gact, w2, (t_rows, d), F32, (t_rows // tm, d // tn, ng),
        pl.BlockSpec((None, tm, fs), lambda i, j, k: (k, i, 0)),
        pl.BlockSpec((None, None, fs, tn), lambda i, j, k: (k, l, 0, j)),
        pl.BlockSpec((tm, tn), lambda i, j, k: (i, j)), NN)
    xn = _resid(tag + "_res", x, y, gate, 0.5, seq)
    return xn, (x, h, a, b, gact, y)


def _ffn_bwd(tag, w, l, pre, saved, mod, dxo, seq):
    x, h, a, b, gact, y = saved
    t_rows, d = x.shape
    w1, w3, w2 = w[pre + "w1"], w[pre + "w3"], w[pre + "w2"]
    ng, fs = w1.shape[0], w1.shape[3]
    shift, scale, gate = mod
    tm = _tile(t_rows, 512)
    tn = _tile(d, 512)
    tk = _tile(t_rows, 1024)
    dy, dgate = _resid_bwd(tag + "_res_bwd", dxo, y, gate, 0.5, seq)
    dg = _matmul(
        tag + "_down_dx", dy, w2, (ng, t_rows, fs), F32, (ng, t_rows // tm, 1),
        pl.BlockSpec((tm, d), lambda g, i, k: (i, 0)),
        pl.BlockSpec((None, None, fs, d), lambda g, i, k: (g, l, 0, 0)),
        pl.BlockSpec((None, tm, fs), lambda g, i, k: (g, i, 0)), NT)
    da, db = _swiglu_bwd(tag + "_act_bwd", dg, a, b)
    dw2 = _matmul(
        tag + "_down_dw", gact, dy, (ng, fs, d), BF16, (ng, d // tn, t_rows // tk),
        pl.BlockSpec((None, tk, fs), lambda g, j, k: (g, k, 0)),
        pl.BlockSpec((tk, tn), lambda g, j, k: (k, j)),
        pl.BlockSpec((None, fs, tn), lambda g, j, k: (g, 0, j)), TN)

    def up_dw(name, dab):
        return _matmul(
            name, h, dab, (ng, d, fs), BF16, (ng, d // tn, t_rows // tk),
            pl.BlockSpec((tk, tn), lambda g, i, k: (k, i)),
            pl.BlockSpec((None, tk, fs), lambda g, i, k: (g, k, 0)),
            pl.BlockSpec((None, tn, fs), lambda g, i, k: (g, i, 0)), TN)

    dw1 = up_dw(tag + "_up1_dw", da)
    dw3 = up_dw(tag + "_up3_dw", db)

    def up_dx(name, dab, wg, acc):
        return _matmul(
            name, dab, wg, (t_rows, d), F32, (t_rows // tm, d // tn, ng),
            pl.BlockSpec((None, tm, fs), lambda i, j, k: (k, i, 0)),
            pl.BlockSpec((None, None, tn, fs), lambda i, j, k: (k, l, j, 0)),
            pl.BlockSpec((tm, tn), lambda i, j, k: (i, j)), NT, acc)

    dh = up_dx(tag + "_up1_dx", da, w1, None)
    dh = up_dx(tag + "_up3_dx", db, w3, dh)
    dx, dshift, dscale, dgain = _normmod_bwd(tag + "_norm_bwd", x, dh, dxo, w[pre + "norm"][l][None], scale, seq)
    grads = {pre + "w1": dw1, pre + "w3": dw3, pre + "w2": dw2, pre + "norm": dgain}
    return dx, (dshift, dscale, dgate), grads


def _shift_down(v, s, row):
    if s == 0:
        return v
    return jnp.where(row >= s, pltpu.roll(v, s, 0), 0.0)


def _shift_up(v, s, row):
    if s == 0:
        return v
    n = v.shape[0]
    return jnp.where(row < n - s, pltpu.roll(v, n - s, 0), 0.0)


def _scan_fwd(a, u, row):
    n = a.shape[0]
    s = 1
    while s < n:
        ok = row >= s
        a_sh = pltpu.roll(a, s, 0)
        u_sh = pltpu.roll(u, s, 0)
        u = jnp.where(ok, a * u_sh + u, u)
        a = jnp.where(ok, a * a_sh, a)
        s *= 2
    return u


def _scan_bwd(a_next, g, row):
    n = g.shape[0]
    a, u = a_next, g
    s = 1
    while s < n:
        ok = row < n - s
        a_sh = pltpu.roll(a, n - s, 0)
        u_sh = pltpu.roll(u, n - s, 0)
        u = jnp.where(ok, a * u_sh + u, u)
        a = jnp.where(ok, a * a_sh, a)
        s *= 2
    return u


def _rg_specs(seq, cw, nc, off_gate):
    slab = lambda off: pl.BlockSpec((seq, cw), lambda c, b: (b, off + c))
    par = lambda rows: pl.BlockSpec((rows, cw), lambda c, b: (0, c))
    wbd = pl.BlockSpec((None, cw, cw), lambda c, b: (c, 0, 0))
    return slab, par, wbd


def _rg_fwd(name, proj, p, seq, chans):
    t_rows = proj.shape[0]
    bl = t_rows // seq
    cw = LANE
    nc = chans // cw
    slab, par, wbd = _rg_specs(seq, cw, nc, nc)

    def body(x_ref, gt_ref, cw_ref, cb_ref, wa_ref, ba_ref, wx_ref, bx_ref, lam_ref,
             xa_ref, r_ref, i_ref, h_ref, ya_ref):
        row = lax.broadcasted_iota(jnp.int32, (seq, cw), 0)
        xv = x_ref[...]
        xa = jnp.zeros_like(xv) + cb_ref[...]
        for k in range(4):
            xa = xa + cw_ref[k:k + 1, :] * _shift_down(xv, 3 - k, row)
        xab = xa.astype(BF16)
        r = _sigmoid(jnp.dot(xab, wa_ref[...], preferred_element_type=F32) + ba_ref[...])
        ig = _sigmoid(jnp.dot(xab, wx_ref[...], preferred_element_type=F32) + bx_ref[...])
        log_a = (-RG_C) * r * _softplus(-lam_ref[...])
        a = jnp.exp(log_a)
        u = jnp.sqrt(-_expm1(2.0 * log_a)) * (ig * xa)
        h = _scan_fwd(a, u, row)
        gel, _ = _gelu_and_grad(gt_ref[...])
        xa_ref[...] = xa
        r_ref[...] = r
        i_ref[...] = ig
        h_ref[...] = h
        ya_ref[...] = (gel * h).astype(ya_ref.dtype)

    out = pl.BlockSpec((seq, cw), lambda c, b: (b, c))
    f = jax.ShapeDtypeStruct((t_rows, chans), F32)
    return pl.pallas_call(
        body, name=name, grid=(nc, bl),
        in_specs=[slab(0), slab(nc), par(4), par(1), wbd, par(1), wbd, par(1), par(1)],
        out_specs=[out] * 5,
        out_shape=[f, f, f, f, jax.ShapeDtypeStruct((t_rows, chans), BF16)],
        compiler_params=_params(),
    )(proj, proj, p["conv_w"], p["conv_b"], p["wa"], p["ba"], p["wx"], p["bx"], p["lam"])


def _rg_bwd(name, proj, dya, saved, p, seq, chans):
    xa_s, r_s, i_s, h_s = saved
    t_rows = proj.shape[0]
    bl = t_rows // seq
    cw = LANE
    nc = chans // cw
    slab, par, wbd = _rg_specs(seq, cw, nc, nc)

    def body(x_ref, gt_ref, dya_ref, xa_ref, r_ref, i_ref, h_ref, cw_ref, wa_ref, wx_ref, lam_ref,
             dx_ref, dgt_ref, sm_ref, dwa_ref, dwx_ref):
        b = pl.program_id(1)
        row = lax.broadcasted_iota(jnp.int32, (seq, cw), 0)
        xv, xa, r, ig, h = x_ref[...], xa_ref[...], r_ref[...], i_ref[...], h_ref[...]
        dyav = dya_ref[...]
        gel, dgel = _gelu_and_grad(gt_ref[...])
        dgt_ref[...] = (dyav * h * dgel).astype(dgt_ref.dtype)
        dh = dyav * gel
        lam = lam_ref[...]
        sp = _softplus(-lam)
        log_a = (-RG_C) * r * sp
        a = jnp.exp(log_a)
        s = jnp.sqrt(-_expm1(2.0 * log_a))
        lamb = _scan_bwd(_shift_up(a, 1, row), dh, row)
        da = lamb * _shift_down(h, 1, row)
        xi = ig * xa
        ds = lamb * xi
        dxi = lamb * s
        dlog = da * a - ds * (a * a) / s
        dr = dlog * ((-RG_C) * sp)
        dsp = jnp.sum(dlog * ((-RG_C) * r), axis=0, keepdims=True)
        dlam = -dsp * _sigmoid(-lam)
        dzr = dr * r * (1.0 - r)
        dzi = (dxi * xa) * ig * (1.0 - ig)
        dzrb, dzib, xab = dzr.astype(BF16), dzi.astype(BF16), xa.astype(BF16)
        dxa = dxi * ig
        dxa = dxa + lax.dot_general(dzrb, wa_ref[...], (NT, ((), ())), preferred_element_type=F32)
        dxa = dxa + lax.dot_general(dzib, wx_ref[...], (NT, ((), ())), preferred_element_type=F32)
        dwa = lax.dot_general(xab, dzrb, (TN, ((), ())), preferred_element_type=F32)
        dwx = lax.dot_general(xab, dzib, (TN, ((), ())), preferred_element_type=F32)
        dxv = jnp.zeros_like(xv)
        rows = []
        for k in range(4):
            dxv = dxv + cw_ref[k:k + 1, :] * _shift_up(dxa, 3 - k, row)
            rows.append(jnp.sum(dxa * _shift_down(xv, 3 - k, row), axis=0, keepdims=True))
        dx_ref[...] = dxv.astype(dx_ref.dtype)
        rows += [jnp.sum(dxa, axis=0, keepdims=True), jnp.sum(dzr, axis=0, keepdims=True),
                 jnp.sum(dzi, axis=0, keepdims=True), dlam]

        @pl.when(b == 0)
        def _():
            sm_ref[...] = jnp.zeros_like(sm_ref)
            dwa_ref[...] = jnp.zeros_like(dwa_ref)
            dwx_ref[...] = jnp.zeros_like(dwx_ref)

        for k, val in enumerate(rows):
            sm_ref[k:k + 1, :] += val
        dwa_ref[...] += dwa
        dwx_ref[...] += dwx

    plain = pl.BlockSpec((seq, cw), lambda c, b: (b, c))
    return pl.pallas_call(
        body, name=name, grid=(nc, bl),
        in_specs=[slab(0), slab(nc), plain, plain, plain, plain, plain, par(4), wbd, wbd, par(1)],
        out_specs=[plain, plain, par(8), wbd, wbd],
        out_shape=[jax.ShapeDtypeStruct((t_rows, chans), BF16), jax.ShapeDtypeStruct((t_rows, chans), BF16),
                   jax.ShapeDtypeStruct((8, chans), F32),
                   jax.ShapeDtypeStruct((nc, cw, cw), F32), jax.ShapeDtypeStruct((nc, cw, cw), F32)],
        compiler_params=_params(),
    )(proj, proj, dya, xa_s, r_s, i_s, h_s, p["conv_w"], p["wa"], p["wx"], p["lam"])


ATT_BLOCK = 256


def _tri(n, kind):
    r = lax.broadcasted_iota(jnp.int32, (n, n), 0)
    c = lax.broadcasted_iota(jnp.int32, (n, n), 1)
    m = {"gt": r > c, "le": r <= c, "lt": r < c}[kind]
    return m.astype(BF16)


def _cumsum_mm(v, tri):
    hi = v.astype(BF16)
    lo = (v - hi.astype(F32)).astype(BF16)
    return jnp.dot(hi, tri, preferred_element_type=F32) + jnp.dot(lo, tri, preferred_element_type=F32)


def _att_specs(seq, blk):
    qs = pl.BlockSpec((None, None, blk, HEAD_DIM), lambda b, h, i: (b, h, i, 0))
    ks = pl.BlockSpec((None, None, seq, HEAD_DIM), lambda b, h, i: (b, h, 0, 0))
    col = pl.BlockSpec((None, None, blk, 1), lambda b, h, i: (b, h, i, 0))
    lane = pl.BlockSpec((None, None, 1, seq), lambda b, h, i: (b, h, 0, 0))
    return qs, ks, col, lane


def _sb_fwd(name, q, k, v):
    bl, nh, seq, hd = q.shape
    blk = _tile(seq, ATT_BLOCK)
    scale = hd ** -0.5
    qs, ks, col, _ = _att_specs(seq, blk)

    def body(q_ref, k_ref, v_ref, o_ref, lt_ref):
        qi = pl.program_id(2)
        qv = q_ref[...]
        row = lax.broadcasted_iota(jnp.int32, (blk, blk), 0)
        cix = lax.broadcasted_iota(jnp.int32, (blk, blk), 1)
        tri = _tri(blk, "gt")

        def step(it, carry):
            acc, cl = carry
            kb = qi - it
            ks_ = pl.multiple_of(kb * blk, blk)
            kv = k_ref[pl.ds(ks_, blk), :]
            vv = v_ref[pl.ds(ks_, blk), :]
            z = lax.dot_general(qv, kv, (NT, ((), ())), preferred_element_type=F32) * scale
            strict = (kb * blk + cix) < (qi * blk + row)
            sp = _softplus(z)
            lk = jnp.where(strict, -sp, 0.0)
            suffix = cl + _cumsum_mm(lk, tri)
            wgt = jnp.where(strict, jnp.exp(z - sp + suffix), 0.0)
            acc = acc + jnp.dot(wgt.astype(BF16), vv, preferred_element_type=F32)
            cl = cl + jnp.sum(lk, axis=1, keepdims=True)
            return acc, cl

        acc, cl = lax.fori_loop(0, qi + 1, step, (jnp.zeros((blk, hd), F32), jnp.zeros((blk, 1), F32)))
        o_ref[...] = acc
        lt_ref[...] = cl

    return pl.pallas_call(
        body, name=name, grid=(bl, nh, seq // blk), in_specs=[qs, ks, ks], out_specs=[qs, col],
        out_shape=[jax.ShapeDtypeStruct((bl, nh, seq, hd), F32), jax.ShapeDtypeStruct((bl, nh, seq, 1), F32)],
        compiler_params=_params(),
    )(q, k, v)


def _sb_bwd(name, q, k, v, ltot, do):
    bl, nh, seq, hd = q.shape
    blk = _tile(seq, ATT_BLOCK)
    scale = hd ** -0.5
    qs, ks, col, _ = _att_specs(seq, blk)

    def body(q_ref, k_ref, v_ref, lt_ref, do_ref, dq_ref, dk_ref, dv_ref):
        qi = pl.program_id(2)

        @pl.when(qi == 0)
        def _():
            dk_ref[...] = jnp.zeros_like(dk_ref)
            dv_ref[...] = jnp.zeros_like(dv_ref)

        qv = q_ref[...]
        dob = do_ref[...].astype(BF16)
        ltv = lt_ref[...]
        row = lax.broadcasted_iota(jnp.int32, (blk, blk), 0)
        cix = lax.broadcasted_iota(jnp.int32, (blk, blk), 1)
        tri_le = _tri(blk, "le")
        tri_lt = _tri(blk, "lt")

        def step(kb, carry):
            dq, cl, ce = carry
            ks_ = pl.multiple_of(kb * blk, blk)
            kv = k_ref[pl.ds(ks_, blk), :]
            vv = v_ref[pl.ds(ks_, blk), :]
            z = lax.dot_general(qv, kv, (NT, ((), ())), preferred_element_type=F32) * scale
            strict = (kb * blk + cix) < (qi * blk + row)
            sp = _softplus(z)
            lk = jnp.where(strict, -sp, 0.0)
            suffix = ltv - cl - _cumsum_mm(lk, tri_le)
            sig = jnp.exp(z - sp)
            wgt = jnp.where(strict, sig * jnp.exp(suffix), 0.0)
            dw = lax.dot_general(dob, vv, (NT, ((), ())), preferred_element_type=F32)
            e = dw * wgt
            pre = ce + _cumsum_mm(e, tri_lt)
            dz = jnp.where(strict, e * (1.0 - sig) - pre * sig, 0.0)
            dzb = (dz * scale).astype(BF16)
            dq = dq + jnp.dot(dzb, kv, preferred_element_type=F32)
            dk_ref[pl.ds(ks_, blk), :] += lax.dot_general(dzb, qv, (TN, ((), ())), preferred_element_type=F32)
            dv_ref[pl.ds(ks_, blk), :] += lax.dot_general(wgt.astype(BF16), dob, (TN, ((), ())),
                                                          preferred_element_type=F32)
            return dq, cl + jnp.sum(lk, axis=1, keepdims=True), ce + jnp.sum(e, axis=1, keepdims=True)

        zero = jnp.zeros((blk, 1), F32)
        dq, _, _ = lax.fori_loop(0, qi + 1, step, (jnp.zeros((blk, hd), F32), zero, zero))
        dq_ref[...] = dq

    f = jax.ShapeDtypeStruct((bl, nh, seq, hd), F32)
    return pl.pallas_call(
        body, name=name, grid=(bl, nh, seq // blk), in_specs=[qs, ks, ks, col, qs], out_specs=[qs, ks, ks],
        out_shape=[f, f, f], compiler_params=_params(),
    )(q, k, v, ltot, do)


NEG = -1e30


def _fox_fwd(name, q, k, v, cum_q, cum_k):
    bl, nh, seq, hd = q.shape
    blk = _tile(seq, ATT_BLOCK)
    scale = hd ** -0.5
    qs, ks, col, lane = _att_specs(seq, blk)

    def body(q_ref, k_ref, v_ref, cq_ref, ck_ref, o_ref, lse_ref):
        qi = pl.program_id(2)
        qv = q_ref[...]
        cq = cq_ref[...]
        row = lax.broadcasted_iota(jnp.int32, (blk, blk), 0)
        cix = lax.broadcasted_iota(jnp.int32, (blk, blk), 1)

        def step(kb, carry):
            m, lsum, acc = carry
            ks_ = pl.multiple_of(kb * blk, blk)
            kv = k_ref[pl.ds(ks_, blk), :]
            vv = v_ref[pl.ds(ks_, blk), :]
            z = lax.dot_general(qv, kv, (NT, ((), ())), preferred_element_type=F32) * scale
            z = z + cq - ck_ref[:, pl.ds(ks_, blk)]
            z = jnp.where((kb * blk + cix) <= (qi * blk + row), z, NEG)
            m_new = jnp.maximum(m, jnp.max(z, axis=1, keepdims=True))
            pv = jnp.exp(z - m_new)
            alpha = jnp.exp(m - m_new)
            lsum = alpha * lsum + jnp.sum(pv, axis=1, keepdims=True)
            acc = alpha * acc + jnp.dot(pv.astype(BF16), vv, preferred_element_type=F32)
            return m_new, lsum, acc

        init = (jnp.full((blk, 1), NEG, F32), jnp.zeros((blk, 1), F32), jnp.zeros((blk, hd), F32))
        m, lsum, acc = lax.fori_loop(0, qi + 1, step, init)
        o_ref[...] = acc / lsum
        lse_ref[...] = m + jnp.log(lsum)

    return pl.pallas_call(
        body, name=name, grid=(bl, nh, seq // blk), in_specs=[qs, ks, ks, col, lane], out_specs=[qs, col],
        out_shape=[jax.ShapeDtypeStruct((bl, nh, seq, hd), F32), jax.ShapeDtypeStruct((bl, nh, seq, 1), F32)],
        compiler_params=_params(),
    )(q, k, v, cum_q, cum_k)


def _fox_bwd(name, q, k, v, cum_q, cum_k, lse, o, do):
    bl, nh, seq, hd = q.shape
    blk = _tile(seq, ATT_BLOCK)
    scale = hd ** -0.5
    qs, ks, col, lane = _att_specs(seq, blk)

    def body(q_ref, k_ref, v_ref, cq_ref, ck_ref, lse_ref, o_ref, do_ref, dq_ref, dk_ref, dv_ref, dcq_ref, dck_ref):
        qi = pl.program_id(2)

        @pl.when(qi == 0)
        def _():
            dk_ref[...] = jnp.zeros_like(dk_ref)
            dv_ref[...] = jnp.zeros_like(dv_ref)
            dck_ref[...] = jnp.zeros_like(dck_ref)

        qv = q_ref[...]
        dov = do_ref[...]
        dob = dov.astype(BF16)
        delta = jnp.sum(dov * o_ref[...], axis=1, keepdims=True)
        shift = cq_ref[...] - lse_ref[...]
        row = lax.broadcasted_iota(jnp.int32, (blk, blk), 0)
        cix = lax.broadcasted_iota(jnp.int32, (blk, blk), 1)

        def step(kb, carry):
            dq, dcq = carry
            ks_ = pl.multiple_of(kb * blk, blk)
            kv = k_ref[pl.ds(ks_, blk), :]
            vv = v_ref[pl.ds(ks_, blk), :]
            z = lax.dot_general(qv, kv, (NT, ((), ())), preferred_element_type=F32) * scale
            z = z + shift - ck_ref[:, pl.ds(ks_, blk)]
            pv = jnp.where((kb * blk + cix) <= (qi * blk + row), jnp.exp(z), 0.0)
            dp = lax.dot_general(dob, vv, (NT, ((), ())), preferred_element_type=F32)
            ds = pv * (dp - delta)
            dsb = (ds * scale).astype(BF16)
            dq = dq + jnp.dot(dsb, kv, preferred_element_type=F32)
            dk_ref[pl.ds(ks_, blk), :] += lax.dot_general(dsb, qv, (TN, ((), ())), preferred_element_type=F32)
            dv_ref[pl.ds(ks_, blk), :] += lax.dot_general(pv.astype(BF16), dob, (TN, ((), ())),
                                                          preferred_element_type=F32)
            dck_ref[:, pl.ds(ks_, blk)] += -jnp.sum(ds, axis=0, keepdims=True)
            return dq, dcq + jnp.sum(ds, axis=1, keepdims=True)

        dq, dcq = lax.fori_loop(0, qi + 1, step, (jnp.zeros((blk, hd), F32), jnp.zeros((blk, 1), F32)))
        dq_ref[...] = dq
        dcq_ref[...] = dcq

    f = jax.ShapeDtypeStruct((bl, nh, seq, hd), F32)
    return pl.pallas_call(
        body, name=name, grid=(bl, nh, seq // blk),
        in_specs=[qs, ks, ks, col, lane, col, qs, qs], out_specs=[qs, ks, ks, col, lane],
        out_shape=[f, f, f, jax.ShapeDtypeStruct((bl, nh, seq, 1), F32), jax.ShapeDtypeStruct((bl, nh, 1, seq), F32)],
        compiler_params=_params(),
    )(q, k, v, cum_q, cum_k, lse, o, do)


def _lane_cumsum(v, reverse):
    n = v.shape[1]
    cix = lax.broadcasted_iota(jnp.int32, v.shape, 1)
    s = 1
    while s < n:
        if reverse:
            v = v + jnp.where(cix < n - s, pltpu.roll(v, n - s, 1), 0.0)
        else:
            v = v + jnp.where(cix >= s, pltpu.roll(v, s, 1), 0.0)
        s *= 2
    return v


def _forget_cum(name, fl, bf):
    def body(fl_ref, bf_ref, o_ref):
        xv = fl_ref[...] + bf_ref[...]
        o_ref[...] = _lane_cumsum(-_softplus(-xv), False)

    return pl.pallas_call(body, name=name, out_shape=jax.ShapeDtypeStruct(fl.shape, F32),
                          compiler_params=_params())(fl, bf)


def _forget_cum_bwd(name, fl, bf, dcum, nh):
    rows = fl.shape[0]

    def body(fl_ref, bf_ref, dc_ref, dfl_ref, dbf_ref):
        xv = fl_ref[...] + bf_ref[...]
        dlogf = _lane_cumsum(dc_ref[...], True)
        dfl = dlogf * _sigmoid(-xv)
        dfl_ref[...] = dfl
        per_row = jnp.sum(dfl, axis=1, keepdims=True)
        tot = per_row[0:nh]
        for b in range(1, rows // nh):
            tot = tot + per_row[b * nh:(b + 1) * nh]
        dbf_ref[...] = tot

    return pl.pallas_call(
        body, name=name,
        out_shape=[jax.ShapeDtypeStruct(fl.shape, F32), jax.ShapeDtypeStruct((nh, 1), F32)],
        compiler_params=_params(),
    )(fl, bf, dcum)


def _merge_fwd(name, proj, off, merge_b, pa, pb, pc):
    t_rows, d = pa.shape
    tm = _tile(t_rows, 256)

    def body(l0, l1, l2, mb, a_ref, b_ref, c_ref, o_ref):
        g0 = _sigmoid(l0[...] + mb[:, 0:d])
        g1 = _sigmoid(l1[...] + mb[:, d:2 * d])
        g2 = _sigmoid(l2[...] + mb[:, 2 * d:3 * d])
        o_ref[...] = (g0 * a_ref[...] + g1 * b_ref[...] + g2 * c_ref[...]).astype(o_ref.dtype)

    row = pl.BlockSpec((tm, d), lambda i: (i, 0))
    lg = lambda j: pl.BlockSpec((tm, d), lambda i: (i, off + j))
    return pl.pallas_call(
        body, name=name, grid=(t_rows // tm,),
        in_specs=[lg(0), lg(1), lg(2), pl.BlockSpec((1, 3 * d), lambda i: (0, 0)), row, row, row],
        out_specs=row, out_shape=jax.ShapeDtypeStruct((t_rows, d), BF16), compiler_params=_params(),
    )(proj, proj, proj, merge_b, pa, pb, pc)


def _merge_bwd(name, proj, off, merge_b, pa, pb, pc, dmixed):
    t_rows, d = pa.shape
    tm = _tile(t_rows, 256)

    def body(l0, l1, l2, mb, a_ref, b_ref, c_ref, dm_ref, da_ref, db_ref, dc_ref, dl_ref, dmb_ref):
        i = pl.program_id(0)
        dm = dm_ref[...]
        parts = []
        for j, (lref, pref, dref) in enumerate(((l0, a_ref, da_ref), (l1, b_ref, db_ref), (l2, c_ref, dc_ref))):
            g = _sigmoid(lref[...] + mb[:, j * d:(j + 1) * d])
            dref[...] = (g * dm).astype(dref.dtype)
            dl = dm * pref[...] * g * (1.0 - g)
            dl_ref[:, j * d:(j + 1) * d] = dl.astype(dl_ref.dtype)
            parts.append(jnp.sum(dl, axis=0, keepdims=True))
        tot = jnp.concatenate(parts, axis=1)

        @pl.when(i == 0)
        def _():
            dmb_ref[...] = tot

        @pl.when(i > 0)
        def _():
            dmb_ref[...] += tot

    row = pl.BlockSpec((tm, d), lambda i: (i, 0))
    lg = lambda j: pl.BlockSpec((tm, d), lambda i: (i, off + j))
    one = pl.BlockSpec((1, 3 * d), lambda i: (0, 0))
    b16 = jax.ShapeDtypeStruct((t_rows, d), BF16)
    return pl.pallas_call(
        body, name=name, grid=(t_rows // tm,),
        in_specs=[lg(0), lg(1), lg(2), one, row, row, row, row],
        out_specs=[row, row, row, pl.BlockSpec((tm, 3 * d), lambda i: (i, 0)), one],
        out_shape=[b16, b16, b16, jax.ShapeDtypeStruct((t_rows, 3 * d), BF16), jax.ShapeDtypeStruct((1, 3 * d), F32)],
        compiler_params=_params(),
    )(proj, proj, proj, merge_b, pa, pb, pc, dmixed)


def _heads(t2d, bl, seq, nh):
    t = t2d.reshape(bl, seq, 3, nh, HEAD_DIM).transpose(2, 0, 3, 1, 4)
    return t[0], t[1], t[2]


def _unheads(parts, bl, seq, nh):
    t = jnp.stack(parts, axis=0).transpose(1, 3, 0, 2, 4)
    return t.reshape(bl * seq, 3 * nh * HEAD_DIM)


def _grouped_nn(name, a, wg, l, out_dtype, acc=None):
    t_rows, kk = a.shape
    ng, ncol = wg.shape[0], wg.shape[3]
    tm = _tile(t_rows, 512)
    return _matmul(
        name, a, wg, (t_rows, ng * ncol), out_dtype, (t_rows // tm, ng, 1),
        pl.BlockSpec((tm, kk), lambda i, g, k: (i, 0)),
        pl.BlockSpec((None, None, kk, ncol), lambda i, g, k: (g, l, 0, 0)),
        pl.BlockSpec((tm, ncol), lambda i, g, k: (i, g)), NN, acc)


def _grouped_nt(name, da, wg, l, out_dtype):
    t_rows = da.shape[0]
    ng, kk, ncol = wg.shape[0], wg.shape[2], wg.shape[3]
    tm = _tile(t_rows, 512)
    return _matmul(
        name, da, wg, (t_rows, kk), out_dtype, (t_rows // tm, 1, ng),
        pl.BlockSpec((tm, ncol), lambda i, j, k: (i, k)),
        pl.BlockSpec((None, None, kk, ncol), lambda i, j, k: (k, l, 0, 0)),
        pl.BlockSpec((tm, kk), lambda i, j, k: (i, 0)), NT)


def _grouped_tn(name, a, da, ng, out_dtype):
    t_rows, kk = a.shape
    ncol = da.shape[1] // ng
    tk = _tile(t_rows, 1024)
    return _matmul(
        name, a, da, (ng, kk, ncol), out_dtype, (ng, 1, t_rows // tk),
        pl.BlockSpec((tk, kk), lambda g, j, k: (k, 0)),
        pl.BlockSpec((tk, ncol), lambda g, j, k: (k, g)),
        pl.BlockSpec((None, kk, ncol), lambda g, j, k: (g, 0, 0)), TN)


def _rows_nn(name, a, wr, l, out_dtype):
    t_rows = a.shape[0]
    ng, kg, n = wr.shape[0], wr.shape[2], wr.shape[3]
    tm, tn = _tile(t_rows, 512), _tile(n, 512)
    return _matmul(
        name, a, wr, (t_rows, n), out_dtype, (t_rows // tm, n // tn, ng),
        pl.BlockSpec((tm, kg), lambda i, j, k: (i, k)),
        pl.BlockSpec((None, None, kg, tn), lambda i, j, k: (k, l, 0, j)),
        pl.BlockSpec((tm, tn), lambda i, j, k: (i, j)), NN)


def _rows_nt(name, dy, wr, l, out_dtype):
    t_rows, n = dy.shape
    ng, kg = wr.shape[0], wr.shape[2]
    tm = _tile(t_rows, 512)
    return _matmul(
        name, dy, wr, (t_rows, ng * kg), out_dtype, (t_rows // tm, ng, 1),
        pl.BlockSpec((tm, n), lambda i, g, k: (i, 0)),
        pl.BlockSpec((None, None, kg, n), lambda i, g, k: (g, l, 0, 0)),
        pl.BlockSpec((tm, kg), lambda i, g, k: (i, g)), NT)


def _rows_tn(name, a, dy, ng, out_dtype):
    t_rows, n = dy.shape
    kg = a.shape[1] // ng
    tk, tn = _tile(t_rows, 1024), _tile(n, 512)
    return _matmul(
        name, a, dy, (ng, kg, n), out_dtype, (ng, n // tn, t_rows // tk),
        pl.BlockSpec((tk, kg), lambda g, j, k: (k, g)),
        pl.BlockSpec((tk, tn), lambda g, j, k: (k, j)),
        pl.BlockSpec((None, kg, tn), lambda g, j, k: (g, 0, j)), TN)


def _mix_fwd(tag, w, l, x, mod, seq):
    t_rows, d = x.shape
    bl = t_rows // seq
    shift, scale, gate = mod
    lay = w["layout"]
    chans, nh = lay["chans"], lay["heads"]
    wq = 3 * nh * HEAD_DIM
    h = _normmod(tag + "_norm", x, w["mix_norm"][l][None], shift, scale, seq)
    proj = _mm(tag + "_in", h, w["w_inr"][l], NN, F32, tn=640)
    rgp = w["rg"][l]
    xa, r, ig, hs, ya = _rg_fwd(tag + "_rg", proj, rgp, seq, chans)
    o_sb = 2 * chans
    q_b, k_b, v_b = _heads(proj[:, o_sb:o_sb + wq].astype(BF16), bl, seq, nh)
    ob, ltot = _sb_fwd(tag + "_sb", q_b, k_b, v_b)
    q_c, k_c, v_c = _heads(proj[:, o_sb + wq:o_sb + 2 * wq].astype(BF16), bl, seq, nh)
    o_f = o_sb + 2 * wq + 3 * d
    fl = proj[:, o_f:o_f + nh].reshape(bl, seq, nh).transpose(0, 2, 1).reshape(bl * nh, seq)
    bf = jnp.tile(w["fox_bf"][l].reshape(nh, 1), (bl, 1))
    cum = _forget_cum(tag + "_cum", fl, bf)
    cum_q = cum.reshape(bl, nh, seq, 1)
    cum_k = cum.reshape(bl, nh, 1, seq)
    oc, lse = _fox_fwd(tag + "_fox", q_c, k_c, v_c, cum_q, cum_k)
    merge_heads = lambda o: o.transpose(0, 2, 1, 3).reshape(t_rows, nh * HEAD_DIM).astype(BF16)
    yb, yc = merge_heads(ob), merge_heads(oc)
    pa = _rows_nn(tag + "_prg", ya, w["w_rg"], l, F32)
    pb = _grouped_nn(tag + "_psb", yb, w["w_sb"], l, F32)
    pc = _grouped_nn(tag + "_pfox", yc, w["w_fox"], l, F32)
    moff = (o_sb + 2 * wq) // d
    mb = w["merge_b"][l][None]
    mixed = _merge_fwd(tag + "_merge", proj, moff, mb, pa, pb, pc)
    y = _rows_nn(tag + "_out", mixed, w["w_o"], l, F32)
    xn = _resid(tag + "_res", x, y, gate, 1.0, seq)
    saved = dict(x=x, h=h, proj=proj, rg=(xa, r, ig, hs), ya=ya, sb=(q_b, k_b, v_b, ltot),
                 fox=(q_c, k_c, v_c, cum_q, cum_k, lse, oc), fl=fl, bf=bf, yb=yb, yc=yc,
                 pa=pa, pb=pb, pc=pc, mixed=mixed, y=y)
    return xn, saved


def _mix_bwd(tag, w, l, s, mod, dxo, seq):
    x = s["x"]
    t_rows, d = x.shape
    bl = t_rows // seq
    shift, scale, gate = mod
    lay = w["layout"]
    chans, nh = lay["chans"], lay["heads"]
    wq = 3 * nh * HEAD_DIM
    o_sb = 2 * chans
    moff = (o_sb + 2 * wq) // d
    mb = w["merge_b"][l][None]
    ng = NUM_CHIPS
    dy, dgate = _resid_bwd(tag + "_res_bwd", dxo, s["y"], gate, 1.0, seq)
    dmixed = _rows_nt(tag + "_out_dx", dy, w["w_o"], l, F32)
    dw_o = _rows_tn(tag + "_out_dw", s["mixed"], dy, ng, BF16)
    dpa, dpb, dpc, dlog, dmb = _merge_bwd(tag + "_merge_bwd", s["proj"], moff, mb, s["pa"], s["pb"], s["pc"], dmixed)
    dya = _rows_nt(tag + "_prg_dx", dpa, w["w_rg"], l, F32)
    dw_rg = _rows_tn(tag + "_prg_dw", s["ya"], dpa, ng, BF16)
    dyb = _grouped_nt(tag + "_psb_dx", dpb, w["w_sb"], l, F32)
    dw_sb = _grouped_tn(tag + "_psb_dw", s["yb"], dpb, ng, BF16)
    dyc = _grouped_nt(tag + "_pfox_dx", dpc, w["w_fox"], l, F32)
    dw_fox = _grouped_tn(tag + "_pfox_dw", s["yc"], dpc, ng, BF16)
    split_heads = lambda v: v.reshape(bl, seq, nh, HEAD_DIM).transpose(0, 2, 1, 3)
    q_b, k_b, v_b, ltot = s["sb"]
    dqkv_b = _unheads(_sb_bwd(tag + "_sb_bwd", q_b, k_b, v_b, ltot, split_heads(dyb)), bl, seq, nh)
    q_c, k_c, v_c, cum_q, cum_k, lse, oc = s["fox"]
    dq_c, dk_c, dv_c, dcq, dck = _fox_bwd(tag + "_fox_bwd", q_c, k_c, v_c, cum_q, cum_k, lse, oc, split_heads(dyc))
    dqkv_c = _unheads((dq_c, dk_c, dv_c), bl, seq, nh)
    dcum = dcq.reshape(bl * nh, seq) + dck.reshape(bl * nh, seq)
    dfl, dbf = _forget_cum_bwd(tag + "_cum_bwd", s["fl"], s["bf"], dcum, nh)
    dfl_t = dfl.reshape(bl, nh, seq).transpose(0, 2, 1).reshape(t_rows, nh)
    dfl_pad = jnp.pad(dfl_t, ((0, 0), (0, LANE - nh))).astype(BF16)
    rgp = w["rg"][l]
    drgx, dgt, rg_small, dwa, dwx = _rg_bwd(tag + "_rg_bwd", s["proj"], dya, s["rg"], rgp, seq, chans)
    dproj = jnp.concatenate([drgx, dgt, dqkv_b.astype(BF16), dqkv_c.astype(BF16), dlog, dfl_pad], axis=1)
    dh = _mm(tag + "_in_dx", dproj, w["w_inr"][l], NT, F32)
    dw_inr = _mm(tag + "_in_dw", s["h"], dproj, TN, BF16, tn=640)
    dx, dshift, dscale, dgain = _normmod_bwd(tag + "_norm_bwd", x, dh, dxo, w["mix_norm"][l][None], scale, seq)
    grads = dict(w_inr=dw_inr, w_rg=dw_rg, w_sb=dw_sb, w_fox=dw_fox, w_o=dw_o, mix_norm=dgain,
                 rg_small=rg_small, rg_dwa=dwa, rg_dwx=dwx, fox_bf=dbf, merge_b=dmb)
    return dx, (dshift, dscale, dgate), grads


MOD_ROWS = 16


def _silu_rows(name, c):
    bl, d = c.shape
    cp = jnp.pad(c, ((0, MOD_ROWS - bl), (0, 0)))

    def body(c_ref, o_ref):
        v = c_ref[...]
        o_ref[...] = v * _sigmoid(v)

    return pl.pallas_call(body, name=name, out_shape=jax.ShapeDtypeStruct((MOD_ROWS, d), F32),
                          compiler_params=_params())(cp)


def _blockdiag(wb):
    nb, bd, _ = wb.shape
    per = LANE // bd
    t = wb.reshape(nb // per, per, bd, 1, bd)
    eye = jnp.eye(per, dtype=wb.dtype).reshape(1, per, 1, per, 1)
    return (t * eye).reshape(nb // per, LANE, LANE).astype(BF16)


def _unblockdiag(t, bd):
    n = t.shape[0]
    per = LANE // bd
    t5 = t.reshape(n, per, bd, per, bd)
    return jnp.stack([t5[:, p, :, p, :] for p in range(per)], axis=1).reshape(n * per, bd, bd)


def _prepare(gw, a, d, chans, nh):
    depth = a["ada_b"].shape[0]
    wq = 3 * nh * HEAD_DIM
    o_m = 2 * chans + 2 * wq
    w = {"layout": dict(chans=chans, heads=nh)}
    for n in ("ffn1_w1", "ffn1_w3", "ffn1_w2", "ffn2_w1", "ffn2_w3", "ffn2_w2", "w_rg", "w_sb", "w_fox", "w_o", "ada_w"):
        w[n] = gw[n]
    for n in ("ffn1_norm", "ffn2_norm", "mix_norm", "fox_bf", "merge_b", "ada_b", "final_norm", "final_ada_b"):
        w[n] = a[n]
    w["final_ada_w"] = gw["final_ada_w"].reshape(NUM_CHIPS, 1, d, -1)
    w_inr, rg = [], []
    for l in range(depth):
        full = gw["w_in"][:, l].transpose(1, 0, 2).reshape(d, -1)
        f_cols = jnp.pad(full[:, o_m:o_m + nh], ((0, 0), (0, LANE - nh)))
        w_inr.append(jnp.concatenate([full[:, :o_m], full[:, o_m + nh:], f_cols], axis=1))
        conv_w = gw["conv_w"][:, l].transpose(1, 0, 2).reshape(-1, chans)
        rg.append(dict(conv_w=conv_w, conv_b=a["conv_b"][l][None], ba=a["rg_ba"][l][None], bx=a["rg_bx"][l][None],
                       lam=a["rg_lam"][l][None], wa=_blockdiag(a["rg_wa"][l]), wx=_blockdiag(a["rg_wx"][l])))
    w["w_inr"], w["rg"] = w_inr, rg
    return w


def _local_step(w, x, c, tgt):
    bl, seq, d = x.shape
    t_rows = bl * seq
    depth = w["ada_b"].shape[0]
    c_act = _silu_rows("c_act", c)
    c_b = c_act.astype(BF16)
    mods = []
    for l in range(depth):
        bias = jnp.broadcast_to(w["ada_b"][l][None], (MOD_ROWS, 9 * d))
        mod = _grouped_nn(f"ada{l}", c_b, w["ada_w"], l, F32, acc=bias)[:bl].reshape(bl, 9, 1, d)
        mods.append([(mod[:, 3 * k], mod[:, 3 * k + 1], mod[:, 3 * k + 2]) for k in range(3)])
    fbias = jnp.broadcast_to(w["final_ada_b"][None], (MOD_ROWS, 2 * d))
    fm = _grouped_nn("ada_final", c_b, w["final_ada_w"], 0, F32, acc=fbias)[:bl].reshape(bl, 2, 1, d)
    saved = []
    xc = x.reshape(t_rows, d)
    for l in range(depth):
        xc, s1 = _ffn_fwd(f"l{l}_ffn1", w, l, "ffn1_", xc, mods[l][0], seq)
        xc, s2 = _mix_fwd(f"l{l}_mix", w, l, xc, mods[l][1], seq)
        xc, s3 = _ffn_fwd(f"l{l}_ffn2", w, l, "ffn2_", xc, mods[l][2], seq)
        saved.append((s1, s2, s3))
    lpart, dx, dfs, dfc, dfg = _final_loss("final", xc, tgt.reshape(t_rows, d), w["final_norm"][None],
                                           fm[:, 0], fm[:, 1], seq)
    loss = jnp.sum(lpart[:, 0, 0])
    grads = {"final_norm": dfg, "layers": [None] * depth}
    dmods = [None] * depth
    for l in reversed(range(depth)):
        s1, s2, s3 = saved[l]
        dx, dm3, g3 = _ffn_bwd(f"l{l}_ffn2", w, l, "ffn2_", s3, mods[l][2], dx, seq)
        dx, dm2, g2 = _mix_bwd(f"l{l}_mix", w, l, s2, mods[l][1], dx, seq)
        dx, dm1, g1 = _ffn_bwd(f"l{l}_ffn1", w, l, "ffn1_", s1, mods[l][0], dx, seq)
        dmods[l] = jnp.concatenate([*dm1, *dm2, *dm3], axis=1).reshape(bl, 9 * d)
        grads["layers"][l] = {**g1, **g2, **g3}
    dfm = jnp.concatenate([dfs, dfc], axis=1).reshape(bl, 2 * d)
    return loss, dx.reshape(bl, seq, d), grads, dmods, dfm, c_act[:bl]


def _mesh_pos():
    return lax.axis_index("x"), lax.axis_index("y"), lax.axis_index("c")


def _exchange_chips(name, arrs, scatter):
    n = len(arrs)

    def body(*refs):
        ins, outs = refs[:n], refs[n:2 * n]
        send, recv, loc = refs[2 * n:]
        x, y, c = _mesh_pos()
        me = 2 * x + y
        chips = ((1 - x, y), (x, 1 - y), (1 - x, 1 - y))

        def src(i, chip):
            return ins[i].at[chip] if scatter else ins[i]

        local = [pltpu.make_async_copy(src(i, me), outs[i].at[me], loc.at[i]) for i in range(n)]
        for cp in local:
            cp.start()
        sends = []
        for j, (px, py) in enumerate(chips):
            for i in range(n):
                sends.append(pltpu.make_async_remote_copy(
                    src_ref=src(i, 2 * px + py), dst_ref=outs[i].at[me], send_sem=send.at[j * n + i],
                    recv_sem=recv.at[j * n + i], device_id=(px, py, c), device_id_type=MESH))
        for s in sends:
            s.start()
        for j, (px, py) in enumerate(chips):
            for i in range(n):
                pltpu.make_async_remote_copy(
                    src_ref=src(i, me), dst_ref=outs[i].at[2 * px + py], send_sem=send.at[j * n + i],
                    recv_sem=recv.at[j * n + i], device_id=(px, py, c), device_id_type=MESH).wait_recv()
        for s in sends:
            s.wait_send()
        for cp in local:
            cp.wait()

    out_shape = [jax.ShapeDtypeStruct(a.shape if scatter else (NUM_CHIPS,) + a.shape, a.dtype) for a in arrs]
    return pl.pallas_call(
        body, name=name, in_specs=[ANY] * n, out_specs=[ANY] * n, out_shape=out_shape,
        scratch_shapes=[pltpu.SemaphoreType.DMA((3 * n,)), pltpu.SemaphoreType.DMA((3 * n,)),
                        pltpu.SemaphoreType.DMA((n,))],
    )(*arrs)


def _swap_sibling(name, arrs):
    n = len(arrs)

    def body(*refs):
        ins, outs = refs[:n], refs[n:2 * n]
        send, recv = refs[2 * n:]
        x, y, c = _mesh_pos()
        copies = [pltpu.make_async_remote_copy(
            src_ref=ins[i], dst_ref=outs[i], send_sem=send.at[i], recv_sem=recv.at[i],
            device_id=(x, y, 1 - c), device_id_type=MESH) for i in range(n)]
        for cp in copies:
            cp.start()
        for cp in copies:
            cp.wait()

    return pl.pallas_call(
        body, name=name, in_specs=[ANY] * n, out_specs=[ANY] * n,
        out_shape=[jax.ShapeDtypeStruct(a.shape, a.dtype) for a in arrs],
        scratch_shapes=[pltpu.SemaphoreType.DMA((n,)), pltpu.SemaphoreType.DMA((n,))],
    )(*arrs)


def _gather_all(name, pack):
    def body(in_ref, out_ref, send, recv, loc):
        x, y, c = _mesh_pos()
        me = 4 * x + 2 * y + c
        mine = pltpu.make_async_copy(in_ref, out_ref.at[me], loc)
        mine.start()
        peers = []
        for mask in range(1, NUM_DEVICES):
            px = 1 - x if mask & 4 else x
            py = 1 - y if mask & 2 else y
            pc = 1 - c if mask & 1 else c
            peers.append((px, py, pc))
        sends = [pltpu.make_async_remote_copy(
            src_ref=in_ref, dst_ref=out_ref.at[me], send_sem=send.at[k], recv_sem=recv.at[k],
            device_id=p, device_id_type=MESH) for k, p in enumerate(peers)]
        for s in sends:
            s.start()
        for k, (px, py, pc) in enumerate(peers):
            pltpu.make_async_remote_copy(
                src_ref=in_ref, dst_ref=out_ref.at[4 * px + 2 * py + pc], send_sem=send.at[k], recv_sem=recv.at[k],
                device_id=(px, py, pc), device_id_type=MESH).wait_recv()
        for s in sends:
            s.wait_send()
        mine.wait()

    return pl.pallas_call(
        body, name=name, in_specs=[ANY], out_specs=ANY,
        out_shape=jax.ShapeDtypeStruct((NUM_DEVICES,) + pack.shape, pack.dtype),
        scratch_shapes=[pltpu.SemaphoreType.DMA((NUM_DEVICES - 1,)), pltpu.SemaphoreType.DMA((NUM_DEVICES - 1,)),
                        pltpu.SemaphoreType.DMA],
    )(pack)


def _rtile(rows, pref):
    if rows <= pref:
        return rows
    t = (pref // SUBLANE) * SUBLANE
    while t >= SUBLANE:
        if rows % t == 0:
            return t
        t -= SUBLANE
    return rows


def _sum_slots(name, slots):
    g, rows, cols = slots.shape
    tr = _rtile(rows, 256)

    def body(s_ref, o_ref):
        acc = s_ref[0].astype(F32)
        for k in range(1, g):
            acc = acc + s_ref[k].astype(F32)
        o_ref[...] = acc

    return pl.pallas_call(
        body, name=name, grid=(rows // tr,),
        in_specs=[pl.BlockSpec((g, tr, cols), lambda i: (0, i, 0))],
        out_specs=pl.BlockSpec((tr, cols), lambda i: (i, 0)),
        out_shape=jax.ShapeDtypeStruct((rows, cols), F32), compiler_params=_params(),
    )(slots)


def _adamw(name, gparts, w, m, v, l=None):
    rows, cols = gparts[0].shape
    tr = _rtile(rows, 128)
    npart = len(gparts)

    def body(*refs):
        g = refs[0][...]
        for r in refs[1:npart]:
            g = g + r[...]
        w_ref, m_ref, v_ref, g_o, d_o, m_o, v_o = refs[npart:]
        mn = ADAM_B1 * m_ref[...] + (1.0 - ADAM_B1) * g
        vn = ADAM_B2 * v_ref[...] + (1.0 - ADAM_B2) * (g * g)
        m_hat = mn / (1.0 - ADAM_B1 ** ADAM_STEP)
        v_hat = vn / (1.0 - ADAM_B2 ** ADAM_STEP)
        g_o[...] = g
        d_o[...] = -ADAM_LR * (m_hat / (jnp.sqrt(v_hat) + ADAM_EPS) + ADAM_WD * w_ref[...])
        m_o[...] = mn
        v_o[...] = vn

    gspec = pl.BlockSpec((tr, cols), lambda i: (i, 0))
    wspec = gspec if l is None else pl.BlockSpec((None, tr, cols), lambda i: (l, i, 0))
    f = jax.ShapeDtypeStruct((rows, cols), F32)
    return pl.pallas_call(
        body, name=name, grid=(rows // tr,), in_specs=[gspec] * npart + [wspec] * 3, out_specs=[gspec] * 4,
        out_shape=[f] * 4, compiler_params=_params(),
    )(*gparts, w, m, v)


def _colsum(name, a):
    def body(a_ref, o_ref):
        o_ref[...] = jnp.sum(a_ref[...], axis=0, keepdims=True)

    return pl.pallas_call(body, name=name, out_shape=jax.ShapeDtypeStruct((1, a.shape[1]), F32),
                          compiler_params=_params())(a)


PACK_UNIT = SUBLANE * LANE


def _pack(items):
    flat, layout, o = [], [], 0
    for it in items:
        n = it.size
        pad = -n % PACK_UNIT
        flat.append(jnp.pad(it.reshape(-1).astype(F32), (0, pad)))
        layout.append((o, n, it.shape))
        o += n + pad
    return jnp.concatenate(flat).reshape(-1, LANE), layout


def _unpack(pack, layout):
    flat = pack.reshape(-1)
    return [flat[o:o + n].reshape(shape) for o, n, shape in layout]


WEIGHTS = ("ffn1_norm", "ffn1_w1", "ffn1_w3", "ffn1_w2", "mix_norm", "w_in", "conv_w", "conv_b", "rg_wa", "rg_ba",
           "rg_wx", "rg_bx", "rg_lam", "fox_bf", "merge_b", "w_rg", "w_sb", "w_fox", "w_o", "ffn2_norm", "ffn2_w1",
           "ffn2_w3", "ffn2_w2", "ada_w", "ada_b", "final_norm", "final_ada_w", "final_ada_b")
GATHERED = ("ffn1_w1", "ffn1_w3", "ffn1_w2", "w_in", "w_rg", "w_sb", "w_fox", "w_o", "ffn2_w1", "ffn2_w3", "ffn2_w2",
            "ada_w", "final_ada_w")
SCATTERED = GATHERED[:11]
SMALL = ("ffn1_norm", "mix_norm", "ffn2_norm", "rg_small", "rg_wa", "rg_wx", "fox_bf", "merge_b")


def _step(a):
    x, c, tgt = a["x"], a["c"], a["loss_target"]
    bl, seq, d = x.shape
    depth, nh = a["fox_bf"].shape
    chans = a["rg_lam"].shape[1]
    bd = a["rg_wa"].shape[2]
    wq = 3 * nh * HEAD_DIM
    o_m = 2 * chans + 2 * wq
    mx, my, _ = _mesh_pos()
    me = 2 * mx + my

    got = _exchange_chips("gather_weights", [a[n].astype(BF16) for n in GATHERED] + [a["conv_w"]], scatter=False)
    gw = dict(zip(GATHERED + ("conv_w",), got))
    w = _prepare(gw, a, d, chans, nh)
    loss, grad_x, grads, dmods, dfm, c_act = _local_step(w, x, c, tgt)
    loss = lax.psum(loss, ("x", "y", "c"))

    rs_in = []
    for l in range(depth):
        for n in SCATTERED:
            if n == "w_in":
                g = grads["layers"][l]["w_inr"]
                orig = jnp.concatenate([g[:, :o_m], g[:, o_m + 3 * d:o_m + 3 * d + nh], g[:, o_m:o_m + 3 * d]], axis=1)
                rs_in.append(orig.reshape(d, NUM_CHIPS, -1).transpose(1, 0, 2))
            else:
                rs_in.append(grads["layers"][l][n])
    slots = _exchange_chips("scatter_grads", rs_in, scatter=True)
    partial = [_sum_slots(f"sum_grads_{k}", s) for k, s in enumerate(slots)]
    other = _swap_sibling("swap_partials", partial)

    out = {}

    def put(n, res, per_layer):
        for kind, val in zip(("grad_", "delta_", "new_m_", "new_v_"), res):
            out[kind + n] = jnp.stack(val).reshape(a[n].shape) if per_layer else val.reshape(a[n].shape)

    def flat3(v):
        return v.reshape(depth, -1, v.shape[-1])

    k = 0
    dense = {n: [] for n in SCATTERED}
    for l in range(depth):
        for n in SCATTERED:
            dense[n].append(_adamw(f"adamw_{n}_{l}", [partial[k], other[k]], flat3(a[n]), flat3(a["m_" + n]),
                                   flat3(a["v_" + n]), l))
            k += 1
    for n in SCATTERED:
        put(n, [[dense[n][l][j] for l in range(depth)] for j in range(4)], True)

    items = []
    for l in range(depth):
        g = grads["layers"][l]
        items +=[g["ffn1_norm"], g["mix_norm"], g["ffn2_norm"], g["rg_small"], _unblockdiag(g["rg_dwa"], bd),
                  _unblockdiag(g["rg_dwx"], bd), g["fox_bf"], g["merge_b"], dmods[l]]
    items += [grads["final_norm"], dfm, c_act]
    pack, layout = _pack(items)
    gath = _gather_all("gather_small", pack)
    tot = _sum_slots("sum_small", gath)

    def wpack(pre):
        its = []
        for l in range(depth):
            rg_rows = jnp.concatenate([jnp.zeros((4, chans), F32), a[pre + "conv_b"][l][None], a[pre + "rg_ba"][l][None],
                                       a[pre + "rg_bx"][l][None], a[pre + "rg_lam"][l][None]], axis=0)
            its += [a[pre + "ffn1_norm"][l], a[pre + "mix_norm"][l], a[pre + "ffn2_norm"][l], rg_rows,
                    a[pre + "rg_wa"][l], a[pre + "rg_wx"][l], a[pre + "fox_bf"][l], a[pre + "merge_b"][l],
                    jnp.zeros((bl, 9 * d), F32)]
        its += [a[pre + "final_norm"], jnp.zeros((bl, 2 * d), F32), jnp.zeros((bl, d), F32)]
        return _pack(its)[0]

    res_small = [_unpack(r, layout) for r in _adamw("adamw_small", [tot], wpack(""), wpack("m_"), wpack("v_"))]
    per = len(SMALL) + 1
    for j, n in enumerate(SMALL):
        if n == "rg_small":
            for row, nm in ((4, "conv_b"), (5, "rg_ba"), (6, "rg_bx"), (7, "rg_lam")):
                put(nm, [[r[l * per + j][row] for l in range(depth)] for r in res_small], True)
        else:
            put(n, [[r[l * per + j] for l in range(depth)] for r in res_small], True)
    put("final_norm", [r[depth * per] for r in res_small], False)

    gflat = gath.reshape(NUM_DEVICES, -1)

    def rows_of(idx):
        o, n, shape = layout[idx]
        return gflat[:, o:o + n].reshape(NUM_DEVICES * shape[0], shape[1])

    c_all = rows_of(depth * per + 2).astype(BF16)
    late_g, ada = [], []
    for l in range(depth):
        dmod_all = rows_of(l * per + per - 1)
        late_g.append(_colsum(f"ada_b_grad_{l}", dmod_all))
        ncol = a["ada_w"].shape[2]
        mine = lax.dynamic_slice_in_dim(dmod_all, me * ncol, ncol, axis=1).astype(BF16)
        gl = _mm(f"ada_w_grad_{l}", c_all, mine, TN, F32)
        ada.append(_adamw(f"adamw_ada_w_{l}", [gl], a["ada_w"], a["m_ada_w"], a["v_ada_w"], l))
    put("ada_w", [[ada[l][j] for l in range(depth)] for j in range(4)], True)
    dfm_all = rows_of(depth * per + 1)
    late_g.append(_colsum("final_ada_b_grad", dfm_all))
    ncol = a["final_ada_w"].shape[1]
    mine = lax.dynamic_slice_in_dim(dfm_all, me * ncol, ncol, axis=1).astype(BF16)
    gl = _mm("final_ada_w_grad", c_all, mine, TN, F32)
    put("final_ada_w", _adamw("adamw_final_ada_w", [gl], a["final_ada_w"], a["m_final_ada_w"], a["v_final_ada_w"]),
        False)
    cshard = a["conv_w"].shape[2]
    for l in range(depth):
        rg_tot = res_small[0][l * per + SMALL.index("rg_small")]
        late_g.append(lax.dynamic_slice_in_dim(rg_tot[:4], me * cshard, cshard, axis=1))
    gp2, layout2 = _pack(late_g)

    def wpack2(pre):
        return _pack([a[pre + "ada_b"][l][None] for l in range(depth)] + [a[pre + "final_ada_b"][None]]
                     + [a[pre + "conv_w"][l] for l in range(depth)])[0]

    res_late = [_unpack(r, layout2) for r in _adamw("adamw_late", [gp2], wpack2(""), wpack2("m_"), wpack2("v_"))]
    put("ada_b", [[r[l] for l in range(depth)] for r in res_late], True)
    put("final_ada_b", [r[depth] for r in res_late], False)
    put("conv_w", [[r[depth + 1 + l] for l in range(depth)] for r in res_late], True)

    outs = [loss, grad_x]
    for kind in ("grad_", "delta_", "new_m_", "new_v_"):
        outs += [out[kind + n] for n in WEIGHTS]
    return tuple(outs)


def kernel(x, c, ffn1_norm, ffn1_w1, ffn1_w3, ffn1_w2, mix_norm, w_in, conv_w, conv_b, rg_wa, rg_ba, rg_wx, rg_bx, rg_lam, fox_bf, merge_b, w_rg, w_sb, w_fox, w_o, ffn2_norm, ffn2_w1, ffn2_w3, ffn2_w2, ada_w, ada_b, final_norm, final_ada_w, final_ada_b, loss_target, m_ffn1_norm, m_ffn1_w1, m_ffn1_w3, m_ffn1_w2, m_mix_norm, m_w_in, m_conv_w, m_conv_b, m_rg_wa, m_rg_ba, m_rg_wx, m_rg_bx, m_rg_lam, m_fox_bf, m_merge_b, m_w_rg, m_w_sb, m_w_fox, m_w_o, m_ffn2_norm, m_ffn2_w1, m_ffn2_w3, m_ffn2_w2, m_ada_w, m_ada_b, m_final_norm, m_final_ada_w, m_final_ada_b, v_ffn1_norm, v_ffn1_w1, v_ffn1_w3, v_ffn1_w2, v_mix_norm, v_w_in, v_conv_w, v_conv_b, v_rg_wa, v_rg_ba, v_rg_wx, v_rg_bx, v_rg_lam, v_fox_bf, v_merge_b, v_w_rg, v_w_sb, v_w_fox, v_w_o, v_ffn2_norm, v_ffn2_w1, v_ffn2_w3, v_ffn2_w2, v_ada_w, v_ada_b, v_final_norm, v_final_ada_w, v_final_ada_b):
    args = dict(locals())
    return _step(args)
```

```python
import math

import jax
import jax.numpy as jnp
from jax import lax
from jax.experimental import pallas as pl
from jax.experimental.pallas import tpu as pltpu

F32 = jnp.float32
BF16 = jnp.bfloat16

NUM_CHIPS = 4
NUM_DEVICES = 8
HEAD_DIM = 64
LANE = 128
SUBLANE = 8
VMEM_LIMIT = 56 * 1024 * 1024
EPS = 1e-6
RG_C = 8.0
ADAM_LR = 0.001
ADAM_B1 = 0.9
ADAM_B2 = 0.999
ADAM_EPS = 1e-08
ADAM_WD = 0.01
ADAM_STEP = 10
MESH = pl.DeviceIdType.MESH
ANY = pl.BlockSpec(memory_space=pl.ANY)


def _params():
    return pltpu.CompilerParams(vmem_limit_bytes=VMEM_LIMIT)


def _tile(dim, pref):
    if dim <= pref:
        return dim
    t = (pref // LANE) * LANE
    while t >= LANE:
        if dim % t == 0:
            return t
        t -= LANE
    return dim


def _rtile(rows, pref, unit=2 * SUBLANE):
    if rows <= pref:
        return rows
    t = (pref // unit) * unit
    while t >= unit:
        if rows % t == 0:
            return t
        t -= unit
    return rows


def _sigmoid(x):
    return 1.0 / (1.0 + jnp.exp(-x))


def _softplus(x):
    return jnp.maximum(x, 0.0) + jnp.log(1.0 + jnp.exp(-jnp.abs(x)))


def _expm1(x):
    small = x * (1.0 + x * (0.5 + x * (1.0 / 6.0 + x * (1.0 / 24.0))))
    return jnp.where(jnp.abs(x) < 0.01, small, jnp.exp(x) - 1.0)


_GELU_K = math.sqrt(2.0 / math.pi)


def _gelu_and_grad(x):
    inner = _GELU_K * (x + 0.044715 * x * x * x)
    t = jnp.tanh(inner)
    val = 0.5 * x * (1.0 + t)
    dinner = _GELU_K * (1.0 + 3.0 * 0.044715 * x * x)
    grad = 0.5 * (1.0 + t) + 0.5 * x * (1.0 - t * t) * dinner
    return val, grad


NN = ((1,), (0,))
NT = ((1,), (1,))
TN = ((0,), (0,))
ALL = slice(None)


def _mmk(name, ops, specs, terms, out_shape, out_dtype, grid, o_spec, acc=None):
    n_ops = len(ops)

    def body(*refs):
        o_ref = refs[-1]
        p = None
        for ia, xa, ib, xb, dims in terms:
            t = lax.dot_general(refs[ia][xa], refs[ib][xb], (dims, ((), ())), preferred_element_type=F32)
            p = t if p is None else p + t
        if acc is not None:
            p = p + refs[n_ops][...].astype(F32)
        o_ref[...] = p.astype(o_ref.dtype)

    in_specs = list(specs)
    args = list(ops)
    if acc is not None:
        in_specs.append(pl.BlockSpec(o_spec.block_shape, o_spec.index_map))
        args.append(acc)
    return pl.pallas_call(
        body, name=name, grid=grid, in_specs=in_specs, out_specs=o_spec,
        out_shape=jax.ShapeDtypeStruct(out_shape, out_dtype), compiler_params=_params(),
    )(*args)


def _mm(name, a, b, dims, out_dtype, acc=None, tm=512, tn=512):
    if dims == NN:
        (m, kk), n = a.shape, b.shape[1]
    elif dims == NT:
        (m, kk), n = a.shape, b.shape[0]
    else:
        (kk, m), n = a.shape, b.shape[1]
    tm, tn = _tile(m, tm), _tile(n, tn)
    if dims == TN:
        a_spec = pl.BlockSpec((kk, tm), lambda i, j: (0, i))
    else:
        a_spec = pl.BlockSpec((tm, kk), lambda i, j: (i, 0))
    if dims == NT:
        b_spec = pl.BlockSpec((tn, kk), lambda i, j: (j, 0))
    else:
        b_spec = pl.BlockSpec((kk, tn), lambda i, j: (0, j))
    return _mmk(name, [a, b], [a_spec, b_spec], [(0, ALL, 1, ALL, dims)], (m, n), out_dtype,
                (m // tm, n // tn), pl.BlockSpec((tm, tn), lambda i, j: (i, j)), acc)


def _row_grid(t_rows, seq, pref=256):
    tm = _tile(seq, pref)
    return tm, seq // tm


def _normmod(name, x, gain, shift, scale, seq):
    t_rows, d = x.shape
    bl = t_rows // seq
    tm, per = _row_grid(t_rows, seq)

    def body(x_ref, g_ref, sh_ref, sc_ref, o_ref):
        xv = x_ref[...]
        rstd = lax.rsqrt(jnp.mean(xv * xv, axis=-1, keepdims=True) + EPS)
        hn = (xv * rstd) * g_ref[...]
        o_ref[...] = (hn * (1.0 + sc_ref[...]) + sh_ref[...]).astype(o_ref.dtype)

    row = pl.BlockSpec((tm, d), lambda b, i: (b * per + i, 0))
    vec = pl.BlockSpec((None, 1, d), lambda b, i: (b, 0, 0))
    return pl.pallas_call(
        body, name=name, grid=(bl, per),
        in_specs=[row, pl.BlockSpec((1, d), lambda b, i: (0, 0)), vec, vec],
        out_specs=row, out_shape=jax.ShapeDtypeStruct((t_rows, d), BF16),
        compiler_params=_params(),
    )(x, gain, shift, scale)


def _normmod_bwd(name, x, dh, dxo, gain, scale, seq):
    t_rows, d = x.shape
    bl = t_rows // seq
    tm, per = _row_grid(t_rows, seq)

    def body(x_ref, dh_ref, dxo_ref, g_ref, sc_ref, dx_ref, dsh_ref, dsc_ref, dg_ref):
        b, i = pl.program_id(0), pl.program_id(1)
        xv = x_ref[...]
        dhv = dh_ref[...]
        rstd = lax.rsqrt(jnp.mean(xv * xv, axis=-1, keepdims=True) + EPS)
        xhat = xv * rstd
        gain_v = g_ref[...]
        dhn = dhv * (1.0 + sc_ref[...])
        dxhat = dhn * gain_v
        dx = rstd * (dxhat - xhat * jnp.mean(dxhat * xhat, axis=-1, keepdims=True))
        dx_ref[...] = dxo_ref[...] + dx

        @pl.when(i == 0)
        def _():
            dsh_ref[...] = jnp.zeros_like(dsh_ref)
            dsc_ref[...] = jnp.zeros_like(dsc_ref)

        @pl.when((i == 0) & (b == 0))
        def _():
            dg_ref[...] = jnp.zeros_like(dg_ref)

        dsh_ref[...] += jnp.sum(dhv, axis=0, keepdims=True)
        dsc_ref[...] += jnp.sum(dhv * (xhat * gain_v), axis=0, keepdims=True)
        dg_ref[...] += jnp.sum(dhn * xhat, axis=0, keepdims=True)

    row = pl.BlockSpec((tm, d), lambda b, i: (b * per + i, 0))
    vec = pl.BlockSpec((None, 1, d), lambda b, i: (b, 0, 0))
    one = pl.BlockSpec((1, d), lambda b, i: (0, 0))
    return pl.pallas_call(
        body, name=name, grid=(bl, per),
        in_specs=[row, row, row, one, vec],
        out_specs=[row, vec, vec, one],
        out_shape=[jax.ShapeDtypeStruct((t_rows, d), F32), jax.ShapeDtypeStruct((bl, 1, d), F32),
                   jax.ShapeDtypeStruct((bl, 1, d), F32), jax.ShapeDtypeStruct((1, d), F32)],
        compiler_params=_params(),
    )(x, dh, dxo, gain, scale)


def _resid(name, x, y, gate, coef, seq):
    t_rows, d = x.shape
    bl = t_rows // seq
    tm, per = _row_grid(t_rows, seq)

    def body(x_ref, y_ref, g_ref, o_ref):
        o_ref[...] = x_ref[...] + (coef * (1.0 + g_ref[...])) * y_ref[...]

    row = pl.BlockSpec((tm, d), lambda b, i: (b * per + i, 0))
    vec = pl.BlockSpec((None, 1, d), lambda b, i: (b, 0, 0))
    return pl.pallas_call(
        body, name=name, grid=(bl, per), in_specs=[row, row, vec], out_specs=row,
        out_shape=jax.ShapeDtypeStruct((t_rows, d), F32), compiler_params=_params(),
    )(x, y, gate)


def _resid_bwd(name, dxo, y, gate, coef, seq):
    t_rows, d = dxo.shape
    bl = t_rows // seq
    tm, per = _row_grid(t_rows, seq)

    def body(dxo_ref, y_ref, g_ref, dy_ref, dg_ref):
        i = pl.program_id(1)
        dxov = dxo_ref[...]
        dy_ref[...] = ((coef * (1.0 + g_ref[...])) * dxov).astype(dy_ref.dtype)

        @pl.when(i == 0)
        def _():
            dg_ref[...] = jnp.zeros_like(dg_ref)

        dg_ref[...] += jnp.sum((coef * y_ref[...]) * dxov, axis=0, keepdims=True)

    row = pl.BlockSpec((tm, d), lambda b, i: (b * per + i, 0))
    vec = pl.BlockSpec((None, 1, d), lambda b, i: (b, 0, 0))
    return pl.pallas_call(
        body, name=name, grid=(bl, per), in_specs=[row, row, vec], out_specs=[row, vec],
        out_shape=[jax.ShapeDtypeStruct((t_rows, d), BF16), jax.ShapeDtypeStruct((bl, 1, d), F32)],
        compiler_params=_params(),
    )(dxo, y, gate)


def _final_loss(name, x, tgt, gain, shift, scale, seq):
    t_rows, d = x.shape
    bl = t_rows // seq
    tm, per = _row_grid(t_rows, seq)

    def body(x_ref, t_ref, g_ref, sh_ref, sc_ref, l_ref, dx_ref, dsh_ref, dsc_ref, dg_ref):
        b, i = pl.program_id(0), pl.program_id(1)
        xv = x_ref[...]
        rstd = lax.rsqrt(jnp.mean(xv * xv, axis=-1, keepdims=True) + EPS)
        xhat = xv * rstd
        gain_v = g_ref[...]
        hn = xhat * gain_v
        yv = hn * (1.0 + sc_ref[...]) + sh_ref[...]
        err = yv - t_ref[...]
        dyv = err * (1.0 / d)
        dhn = dyv * (1.0 + sc_ref[...])
        dxhat = dhn * gain_v
        dx_ref[...] = rstd * (dxhat - xhat * jnp.mean(dxhat * xhat, axis=-1, keepdims=True))

        @pl.when(i == 0)
        def _():
            l_ref[...] = jnp.zeros_like(l_ref)
            dsh_ref[...] = jnp.zeros_like(dsh_ref)
            dsc_ref[...] = jnp.zeros_like(dsc_ref)

        @pl.when((i == 0) & (b == 0))
        def _():
            dg_ref[...] = jnp.zeros_like(dg_ref)

        part = jnp.sum(jnp.sum(err * err, axis=-1, keepdims=True), axis=0, keepdims=True) * (0.5 / d)
        l_ref[...] += jnp.broadcast_to(part, l_ref.shape)
        dsh_ref[...] += jnp.sum(dyv, axis=0, keepdims=True)
        dsc_ref[...] += jnp.sum(dyv * hn, axis=0, keepdims=True)
        dg_ref[...] += jnp.sum(dhn * xhat, axis=0, keepdims=True)

    row = pl.BlockSpec((tm, d), lambda b, i: (b * per + i, 0))
    vec = pl.BlockSpec((None, 1, d), lambda b, i: (b, 0, 0))
    one = pl.BlockSpec((1, d), lambda b, i: (0, 0))
    lvec = pl.BlockSpec((None, 1, LANE), lambda b, i: (b, 0, 0))
    return pl.pallas_call(
        body, name=name, grid=(bl, per),
        in_specs=[row, row, one, vec, vec],
        out_specs=[lvec, row, vec, vec, one],
        out_shape=[jax.ShapeDtypeStruct((bl, 1, LANE), F32), jax.ShapeDtypeStruct((t_rows, d), F32),
                   jax.ShapeDtypeStruct((bl, 1, d), F32), jax.ShapeDtypeStruct((bl, 1, d), F32),
                   jax.ShapeDtypeStruct((1, d), F32)],
        compiler_params=_params(),
    )(x, tgt, gain, shift, scale)


def _ffn_up(name, h, w1, w3, l):
    t_rows, d = h.shape
    ng, fs = w1.shape[0], w1.shape[3]
    tm = _tile(t_rows, 512)

    def body(h_ref, w1_ref, w3_ref, a_ref, b_ref, g_ref):
        hv = h_ref[...]
        av = jnp.dot(hv, w1_ref[...], preferred_element_type=F32)
        bv = jnp.dot(hv, w3_ref[...], preferred_element_type=F32)
        a_ref[...] = av
        b_ref[...] = bv
        g_ref[...] = (av * _sigmoid(av) * bv).astype(g_ref.dtype)

    wspec = pl.BlockSpec((None, None, d, fs), lambda g, i: (g, l, 0, 0))
    out = pl.BlockSpec((None, tm, fs), lambda g, i: (g, i, 0))
    f = jax.ShapeDtypeStruct((ng, t_rows, fs), F32)
    return pl.pallas_call(
        body, name=name, grid=(ng, t_rows // tm),
        in_specs=[pl.BlockSpec((tm, d), lambda g, i: (i, 0)), wspec, wspec], out_specs=[out, out, out],
        out_shape=[f, f, jax.ShapeDtypeStruct((ng, t_rows, fs), BF16)], compiler_params=_params(),
    )(h, w1, w3)


def _ffn_down_dx(name, dy, w2, a, b, l):
    t_rows, d = dy.shape
    ng, fs = w2.shape[0], w2.shape[2]
    tm = _tile(t_rows, 512)

    def body(dy_ref, w2_ref, a_ref, b_ref, da_ref, db_ref):
        dgv = lax.dot_general(dy_ref[...], w2_ref[...], (NT, ((), ())), preferred_element_type=F32)
        av = a_ref[...]
        sig = _sigmoid(av)
        da_ref[...] = (dgv * b_ref[...] * (sig * (1.0 + av * (1.0 - sig)))).astype(da_ref.dtype)
        db_ref[...] = (dgv * (av * sig)).astype(db_ref.dtype)

    blk = pl.BlockSpec((None, tm, fs), lambda g, i: (g, i, 0))
    o = jax.ShapeDtypeStruct((ng, t_rows, fs), BF16)
    return pl.pallas_call(
        body, name=name, grid=(ng, t_rows // tm),
        in_specs=[pl.BlockSpec((tm, d), lambda g, i: (i, 0)),
                  pl.BlockSpec((None, None, fs, d), lambda g, i: (g, l, 0, 0)), blk, blk],
        out_specs=[blk, blk], out_shape=[o, o], compiler_params=_params(),
    )(dy, w2, a, b)


def _ffn_up_dw(name, h, da, db):
    t_rows, d = h.shape
    ng, fs = da.shape[0], da.shape[2]
    tn = _tile(d, 512)

    def body(h_ref, da_ref, db_ref, o1_ref, o3_ref):
        hv = h_ref[...]
        o1_ref[...] = lax.dot_general(hv, da_ref[...], (TN, ((), ())), preferred_element_type=F32).astype(o1_ref.dtype)
        o3_ref[...] = lax.dot_general(hv, db_ref[...], (TN, ((), ())), preferred_element_type=F32).astype(o3_ref.dtype)

    dspec = pl.BlockSpec((None, t_rows, fs), lambda g, i: (g, 0, 0))
    out = pl.BlockSpec((None, tn, fs), lambda g, i: (g, i, 0))
    o = jax.ShapeDtypeStruct((ng, d, fs), BF16)
    return pl.pallas_call(
        body, name=name, grid=(ng, d // tn),
        in_specs=[pl.BlockSpec((t_rows, tn), lambda g, i: (0, i)), dspec, dspec],
        out_specs=[out, out], out_shape=[o, o], compiler_params=_params(),
    )(h, da, db)


def _ffn_fwd(tag, w, l, pre, x, mod, seq):
    t_rows, d = x.shape
    w1, w3, w2 = w[pre + "w1"], w[pre + "w3"], w[pre + "w2"]
    ng, fs = w1.shape[0], w1.shape[3]
    shift, scale, gate = mod
    h = _normmod(tag + "_norm", x, w[pre + "norm"][l][None], shift, scale, seq)
    a, b, gact = _ffn_up(tag + "_up", h, w1, w3, l)
    tm, tn = _tile(t_rows, 512), _tile(d, 512)
    y = _mmk(tag + "_down", [gact, w2],
             [pl.BlockSpec((ng, tm, fs), lambda i, j: (0, i, 0)),
              pl.BlockSpec((ng, None, fs, tn), lambda i, j: (0, l, 0, j))],
             [(0, g, 1, g, NN) for g in range(ng)], (t_rows, d), F32, (t_rows // tm, d // tn),
             pl.BlockSpec((tm, tn), lambda i, j: (i, j)))
    xn = _resid(tag + "_res", x, y, gate, 0.5, seq)
    return xn, (x, h, a, b, gact, y)


def _ffn_bwd(tag, w, l, pre, saved, mod, dxo, seq):
    x, h, a, b, gact, y = saved
    t_rows, d = x.shape
    w1, w3, w2 = w[pre + "w1"], w[pre + "w3"], w[pre + "w2"]
    ng, fs = w1.shape[0], w1.shape[3]
    shift, scale, gate = mod
    tm, tn = _tile(t_rows, 512), _tile(d, 512)
    dy, dgate = _resid_bwd(tag + "_res_bwd", dxo, y, gate, 0.5, seq)
    da, db = _ffn_down_dx(tag + "_down_dx", dy, w2, a, b, l)
    dw2 = _mmk(tag + "_down_dw", [gact, dy],
               [pl.BlockSpec((None, t_rows, fs), lambda g, j: (g, 0, 0)),
                pl.BlockSpec((t_rows, tn), lambda g, j: (0, j))],
               [(0, ALL, 1, ALL, TN)], (ng, fs, d), BF16, (ng, d // tn),
               pl.BlockSpec((None, fs, tn), lambda g, j: (g, 0, j)))
    dw1, dw3 = _ffn_up_dw(tag + "_up_dw", h, da, db)
    dspec = pl.BlockSpec((ng, tm, fs), lambda i, j: (0, i, 0))
    wspec = pl.BlockSpec((ng, None, tn, fs), lambda i, j: (0, l, j, 0))
    dh = _mmk(tag + "_up_dx", [da, db, w1, w3], [dspec, dspec, wspec, wspec],
              [(0, g, 2, g, NT) for g in range(ng)] + [(1, g, 3, g, NT) for g in range(ng)],
              (t_rows, d), F32, (t_rows // tm, d // tn), pl.BlockSpec((tm, tn), lambda i, j: (i, j)))
    dx, dshift, dscale, dgain = _normmod_bwd(tag + "_norm_bwd", x, dh, dxo, w[pre + "norm"][l][None], scale, seq)
    grads = {pre + "w1": dw1, pre + "w3": dw3, pre + "w2": dw2, pre + "norm": dgain}
    return dx, (dshift, dscale, dgate), grads


def _shift_down(v, s, row):
    if s == 0:
        return v
    return jnp.where(row >= s, pltpu.roll(v, s, 0), 0.0)


def _shift_up(v, s, row):
    if s == 0:
        return v
    n = v.shape[0]
    return jnp.where(row < n - s, pltpu.roll(v, n - s, 0), 0.0)


def _scan_fwd(a, u, row):
    n = a.shape[0]
    s = 1
    while s < n:
        ok = row >= s
        a_sh = pltpu.roll(a, s, 0)
        u_sh = pltpu.roll(u, s, 0)
        u = jnp.where(ok, a * u_sh + u, u)
        a = jnp.where(ok, a * a_sh, a)
        s *= 2
    return u


def _scan_bwd(a_next, g, row):
    n = g.shape[0]
    a, u = a_next, g
    s = 1
    while s < n:
        ok = row < n - s
        a_sh = pltpu.roll(a, n - s, 0)
        u_sh = pltpu.roll(u, n - s, 0)
        u = jnp.where(ok, a * u_sh + u, u)
        a = jnp.where(ok, a * a_sh, a)
        s *= 2
    return u


def _rg_specs(seq, cw):
    slab = lambda off: pl.BlockSpec((seq, cw), lambda c, b: (b, off + c))
    par = lambda rows: pl.BlockSpec((rows, cw), lambda c, b: (0, c))
    wbd = pl.BlockSpec((None, cw, cw), lambda c, b: (c, 0, 0))
    return slab, par, wbd


def _rg_fwd(name, proj, p, seq, chans):
    t_rows = proj.shape[0]
    bl = t_rows // seq
    cw = LANE
    nc = chans // cw
    slab, par, wbd = _rg_specs(seq, cw)

    def body(x_ref, gt_ref, cw_ref, cb_ref, wa_ref, ba_ref, wx_ref, bx_ref, lam_ref,
             xa_ref, r_ref, i_ref, h_ref, ya_ref):
        row = lax.broadcasted_iota(jnp.int32, (seq, cw), 0)
        xv = x_ref[...]
        xa = jnp.zeros_like(xv) + cb_ref[...]
        for k in range(4):
            xa = xa + cw_ref[k:k + 1, :] * _shift_down(xv, 3 - k, row)
        xab = xa.astype(BF16)
        r = _sigmoid(jnp.dot(xab, wa_ref[...], preferred_element_type=F32) + ba_ref[...])
        ig = _sigmoid(jnp.dot(xab, wx_ref[...], preferred_element_type=F32) + bx_ref[...])
        log_a = (-RG_C) * r * _softplus(-lam_ref[...])
        a = jnp.exp(log_a)
        u = jnp.sqrt(-_expm1(2.0 * log_a)) * (ig * xa)
        h = _scan_fwd(a, u, row)
        gel, _ = _gelu_and_grad(gt_ref[...])
        xa_ref[...] = xa
        r_ref[...] = r
        i_ref[...] = ig
        h_ref[...] = h
        ya_ref[...] = (gel * h).astype(ya_ref.dtype)

    out = pl.BlockSpec((seq, cw), lambda c, b: (b, c))
    f = jax.ShapeDtypeStruct((t_rows, chans), F32)
    return pl.pallas_call(
        body, name=name, grid=(nc, bl),
        in_specs=[slab(0), slab(nc), par(4), par(1), wbd, par(1), wbd, par(1), par(1)],
        out_specs=[out] * 5,
        out_shape=[f, f, f, f, jax.ShapeDtypeStruct((t_rows, chans), BF16)],
        compiler_params=_params(),
    )(proj, proj, p["conv_w"], p["conv_b"], p["wa"], p["ba"], p["wx"], p["bx"], p["lam"])


def _rg_bwd(name, proj, dya, saved, p, seq, chans):
    xa_s, r_s, i_s, h_s = saved
    t_rows = proj.shape[0]
    bl = t_rows // seq
    cw = LANE
    nc = chans // cw
    slab, par, wbd = _rg_specs(seq, cw)

    def body(x_ref, gt_ref, dya_ref, xa_ref, r_ref, i_ref, h_ref, cw_ref, wa_ref, wx_ref, lam_ref,
             dx_ref, dgt_ref, sm_ref, dwa_ref, dwx_ref):
        b = pl.program_id(1)
        row = lax.broadcasted_iota(jnp.int32, (seq, cw), 0)
        xv, xa, r, ig, h = x_ref[...], xa_ref[...], r_ref[...], i_ref[...], h_ref[...]
        dyav = dya_ref[...]
        gel, dgel = _gelu_and_grad(gt_ref[...])
        dgt_ref[...] = (dyav * h * dgel).astype(dgt_ref.dtype)
        dh = dyav * gel
        lam = lam_ref[...]
        sp = _softplus(-lam)
        log_a = (-RG_C) * r * sp
        a = jnp.exp(log_a)
        s = jnp.sqrt(-_expm1(2.0 * log_a))
        lamb = _scan_bwd(_shift_up(a, 1, row), dh, row)
        da = lamb * _shift_down(h, 1, row)
        xi = ig * xa
        ds = lamb * xi
        dxi = lamb * s
        dlog = da * a - ds * (a * a) / s
        dr = dlog * ((-RG_C) * sp)
        dsp = jnp.sum(dlog * ((-RG_C) * r), axis=0, keepdims=True)
        dlam = -dsp * _sigmoid(-lam)
        dzr = dr * r * (1.0 - r)
        dzi = (dxi * xa) * ig * (1.0 - ig)
        dzrb, dzib, xab = dzr.astype(BF16), dzi.astype(BF16), xa.astype(BF16)
        dxa = dxi * ig
        dxa = dxa + lax.dot_general(dzrb, wa_ref[...], (NT, ((), ())), preferred_element_type=F32)
        dxa = dxa + lax.dot_general(dzib, wx_ref[...], (NT, ((), ())), preferred_element_type=F32)
        dwa = lax.dot_general(xab, dzrb, (TN, ((), ())), preferred_element_type=F32)
        dwx = lax.dot_general(xab, dzib, (TN, ((), ())), preferred_element_type=F32)
        dxv = jnp.zeros_like(xv)
        rows = []
        for k in range(4):
            dxv = dxv + cw_ref[k:k + 1, :] * _shift_up(dxa, 3 - k, row)
            rows.append(jnp.sum(dxa * _shift_down(xv, 3 - k, row), axis=0, keepdims=True))
        dx_ref[...] = dxv.astype(dx_ref.dtype)
        rows += [jnp.sum(dxa, axis=0, keepdims=True), jnp.sum(dzr, axis=0, keepdims=True),
                 jnp.sum(dzi, axis=0, keepdims=True), dlam]

        @pl.when(b == 0)
        def _():
            sm_ref[...] = jnp.zeros_like(sm_ref)
            dwa_ref[...] = jnp.zeros_like(dwa_ref)
            dwx_ref[...] = jnp.zeros_like(dwx_ref)

        for k, val in enumerate(rows):
            sm_ref[k:k + 1, :] += val
        dwa_ref[...] += dwa
        dwx_ref[...] += dwx

    plain = pl.BlockSpec((seq, cw), lambda c, b: (b, c))
    return pl.pallas_call(
        body, name=name, grid=(nc, bl),
        in_specs=[slab(0), slab(nc), plain, plain, plain, plain, plain, par(4), wbd, wbd, par(1)],
        out_specs=[plain, plain, par(8), wbd, wbd],
        out_shape=[jax.ShapeDtypeStruct((t_rows, chans), BF16), jax.ShapeDtypeStruct((t_rows, chans), BF16),
                   jax.ShapeDtypeStruct((8, chans), F32),
                   jax.ShapeDtypeStruct((nc, cw, cw), F32), jax.ShapeDtypeStruct((nc, cw, cw), F32)],
        compiler_params=_params(),
    )(proj, proj, dya, xa_s, r_s, i_s, h_s, p["conv_w"], p["wa"], p["wx"], p["lam"])


ATT_BLOCK = 256
PAIR = LANE // HEAD_DIM
NEG = -1e30


def _tri(n, kind):
    r = lax.broadcasted_iota(jnp.int32, (n, n), 0)
    c = lax.broadcasted_iota(jnp.int32, (n, n), 1)
    m = {"gt": r > c, "le": r <= c, "lt": r < c}[kind]
    return m.astype(BF16)


def _cumsum_mm(v, tri):
    hi = v.astype(BF16)
    lo = (v - hi.astype(F32)).astype(BF16)
    return jnp.dot(hi, tri, preferred_element_type=F32) + jnp.dot(lo, tri, preferred_element_type=F32)


def _head_masks():
    lane = lax.broadcasted_iota(jnp.int32, (1, LANE), 1)
    return [(lane >= h * HEAD_DIM) & (lane < (h + 1) * HEAD_DIM) for h in range(PAIR)]


def _only(mask, v):
    return jnp.where(mask, v, jnp.zeros_like(v))


def _att_specs(seq, blk, nq, off):
    npair = None
    qs = lambda o: pl.BlockSpec((blk, LANE), lambda b, p, i: (b * nq + i, o + p))
    ks = lambda o: pl.BlockSpec((seq, LANE), lambda b, p, i: (b, o + p))
    col = pl.BlockSpec((None, PAIR, blk, 1), lambda b, p, i: (b, p, i, 0))
    lane = pl.BlockSpec((None, PAIR, 1, seq), lambda b, p, i: (b, p, 0, 0))
    return qs, ks, col, lane


def _sb_fwd(name, qkv, off, width, bl, seq):
    t_rows = qkv.shape[0]
    blk = _tile(seq, ATT_BLOCK)
    nq = seq // blk
    nb = width // LANE
    scale = HEAD_DIM ** -0.5
    qs, ks, col, _ = _att_specs(seq, blk, nq, off)

    def body(q_ref, k_ref, v_ref, o_ref, lt_ref):
        qi = pl.program_id(2)
        masks = _head_masks()
        qv = q_ref[...]
        qh = [_only(m, qv) for m in masks]
        row = lax.broadcasted_iota(jnp.int32, (blk, blk), 0)
        cix = lax.broadcasted_iota(jnp.int32, (blk, blk), 1)
        tri = _tri(blk, "gt")

        def step(it, carry):
            acc, cls = carry
            kb = qi - it
            ks_ = pl.multiple_of(kb * blk, blk)
            kv = k_ref[pl.ds(ks_, blk), :]
            vv = v_ref[pl.ds(ks_, blk), :]
            strict = (kb * blk + cix) < (qi * blk + row)
            new_cls = []
            for h in range(PAIR):
                z = lax.dot_general(qh[h], kv, (NT, ((), ())), preferred_element_type=F32) * scale
                sp = _softplus(z)
                lk = jnp.where(strict, -sp, 0.0)
                suffix = cls[h] + _cumsum_mm(lk, tri)
                wgt = jnp.where(strict, jnp.exp(z - sp + suffix), 0.0)
                acc = acc + jnp.dot(wgt.astype(BF16), _only(masks[h], vv), preferred_element_type=F32)
                new_cls.append(cls[h] + jnp.sum(lk, axis=1, keepdims=True))
            return acc, tuple(new_cls)

        zero = jnp.zeros((blk, 1), F32)
        acc, cls = lax.fori_loop(0, qi + 1, step, (jnp.zeros((blk, LANE), F32), (zero,) * PAIR))
        o_ref[...] = acc.astype(o_ref.dtype)
        for h in range(PAIR):
            lt_ref[h] = cls[h]

    return pl.pallas_call(
        body, name=name, grid=(bl, nb, nq), in_specs=[qs(off), ks(off + nb), ks(off + 2 * nb)],
        out_specs=[qs(0), col],
        out_shape=[jax.ShapeDtypeStruct((t_rows, width), BF16),
                   jax.ShapeDtypeStruct((bl, nb * PAIR, seq, 1), F32)],
        compiler_params=_params(),
    )(qkv, qkv, qkv)


def _sb_bwd(name, qkv, off, width, bl, seq, ltot, do):
    t_rows = qkv.shape[0]
    blk = _tile(seq, ATT_BLOCK)
    nq = seq // blk
    nb = width // LANE
    scale = HEAD_DIM ** -0.5
    qs, ks, col, _ = _att_specs(seq, blk, nq, off)

    def body(q_ref, k_ref, v_ref, lt_ref, do_ref, dq_ref, dk_ref, dv_ref, dk_acc, dv_acc):
        qi = pl.program_id(2)

        @pl.when(qi == 0)
        def _():
            dk_acc[...] = jnp.zeros_like(dk_acc)
            dv_acc[...] = jnp.zeros_like(dv_acc)

        masks = _head_masks()
        qv = q_ref[...]
        dov = do_ref[...].astype(BF16)
        qh = [_only(m, qv) for m in masks]
        doh = [_only(m, dov) for m in masks]
        ltv = [lt_ref[h] for h in range(PAIR)]
        row = lax.broadcasted_iota(jnp.int32, (blk, blk), 0)
        cix = lax.broadcasted_iota(jnp.int32, (blk, blk), 1)
        tri_le = _tri(blk, "le")
        tri_lt = _tri(blk, "lt")

        def step(kb, carry):
            dq, cls, ces = carry
            ks_ = pl.multiple_of(kb * blk, blk)
            kv = k_ref[pl.ds(ks_, blk), :]
            vv = v_ref[pl.ds(ks_, blk), :]
            strict = (kb * blk + cix) < (qi * blk + row)
            dk_new = jnp.zeros((blk, LANE), F32)
            dv_new = jnp.zeros((blk, LANE), F32)
            new_cls, new_ces = [], []
            for h in range(PAIR):
                z = lax.dot_general(qh[h], kv, (NT, ((), ())), preferred_element_type=F32) * scale
                sp = _softplus(z)
                lk = jnp.where(strict, -sp, 0.0)
                suffix = ltv[h] - cls[h] - _cumsum_mm(lk, tri_le)
                sig = jnp.exp(z - sp)
                wgt = jnp.where(strict, sig * jnp.exp(suffix), 0.0)
                dw = lax.dot_general(doh[h], vv, (NT, ((), ())), preferred_element_type=F32)
                e = dw * wgt
                pre = ces[h] + _cumsum_mm(e, tri_lt)
                dz = jnp.where(strict, e * (1.0 - sig) - pre * sig, 0.0)
                dzb = (dz * scale).astype(BF16)
                dq = dq + jnp.dot(dzb, _only(masks[h], kv), preferred_element_type=F32)
                dk_new = dk_new + lax.dot_general(dzb, qh[h], (TN, ((), ())), preferred_element_type=F32)
                dv_new = dv_new + lax.dot_general(wgt.astype(BF16), doh[h], (TN, ((), ())),
                                                  preferred_element_type=F32)
                new_cls.append(cls[h] + jnp.sum(lk, axis=1, keepdims=True))
                new_ces.append(ces[h] + jnp.sum(e, axis=1, keepdims=True))
            dk_acc[pl.ds(ks_, blk), :] += dk_new
            dv_acc[pl.ds(ks_, blk), :] += dv_new
            return dq, tuple(new_cls), tuple(new_ces)

        zero = (jnp.zeros((blk, 1), F32),) * PAIR
        dq, _, _ = lax.fori_loop(0, qi + 1, step, (jnp.zeros((blk, LANE), F32), zero, zero))
        dq_ref[...] = dq.astype(dq_ref.dtype)

        @pl.when(qi == nq - 1)
        def _():
            dk_ref[...] = dk_acc[...].astype(dk_ref.dtype)
            dv_ref[...] = dv_acc[...].astype(dv_ref.dtype)

    o = jax.ShapeDtypeStruct((t_rows, width), BF16)
    return pl.pallas_call(
        body, name=name, grid=(bl, nb, nq),
        in_specs=[qs(off), ks(off + nb), ks(off + 2 * nb), col, qs(0)], out_specs=[qs(0), ks(0), ks(0)],
        out_shape=[o, o, o], scratch_shapes=[pltpu.VMEM((seq, LANE), F32), pltpu.VMEM((seq, LANE), F32)],
        compiler_params=_params(),
    )(qkv, qkv, qkv, ltot, do)


def _fox_fwd(name, qkv, off, width, bl, seq, cum_q, cum_k):
    t_rows = qkv.shape[0]
    blk = _tile(seq, ATT_BLOCK)
    nq = seq // blk
    nb = width // LANE
    scale = HEAD_DIM ** -0.5
    qs, ks, col, lane = _att_specs(seq, blk, nq, off)

    def body(q_ref, k_ref, v_ref, cq_ref, ck_ref, ob_ref, of_ref, lse_ref):
        qi = pl.program_id(2)
        masks = _head_masks()
        qv = q_ref[...]
        qh = [_only(m, qv) for m in masks]
        cq = [cq_ref[h] for h in range(PAIR)]
        row = lax.broadcasted_iota(jnp.int32, (blk, blk), 0)
        cix = lax.broadcasted_iota(jnp.int32, (blk, blk), 1)

        def step(kb, carry):
            ms, ls, accs = carry
            ks_ = pl.multiple_of(kb * blk, blk)
            kv = k_ref[pl.ds(ks_, blk), :]
            vv = v_ref[pl.ds(ks_, blk), :]
            causal = (kb * blk + cix) <= (qi * blk + row)
            nm, nl, na = [], [], []
            for h in range(PAIR):
                z = lax.dot_general(qh[h], kv, (NT, ((), ())), preferred_element_type=F32) * scale
                z = z + cq[h] - ck_ref[h, :, pl.ds(ks_, blk)]
                z = jnp.where(causal, z, NEG)
                m_new = jnp.maximum(ms[h], jnp.max(z, axis=1, keepdims=True))
                pv = jnp.exp(z - m_new)
                alpha = jnp.exp(ms[h] - m_new)
                nm.append(m_new)
                nl.append(alpha * ls[h] + jnp.sum(pv, axis=1, keepdims=True))
                na.append(alpha * accs[h] + jnp.dot(pv.astype(BF16), _only(masks[h], vv),
                                                    preferred_element_type=F32))
            return tuple(nm), tuple(nl), tuple(na)

        init = ((jnp.full((blk, 1), NEG, F32),) * PAIR, (jnp.zeros((blk, 1), F32),) * PAIR,
                (jnp.zeros((blk, LANE), F32),) * PAIR)
        ms, ls, accs = lax.fori_loop(0, qi + 1, step, init)
        out = accs[0] / ls[0]
        for h in range(1, PAIR):
            out = out + accs[h] / ls[h]
        ob_ref[...] = out.astype(ob_ref.dtype)
        of_ref[...] = out
        for h in range(PAIR):
            lse_ref[h] = ms[h] + jnp.log(ls[h])

    return pl.pallas_call(
        body, name=name, grid=(bl, nb, nq),
        in_specs=[qs(off), ks(off + nb), ks(off + 2 * nb), col, lane], out_specs=[qs(0), qs(0), col],
        out_shape=[jax.ShapeDtypeStruct((t_rows, width), BF16), jax.ShapeDtypeStruct((t_rows, width), F32),
                   jax.ShapeDtypeStruct((bl, nb * PAIR, seq, 1), F32)],
        compiler_params=_params(),
    )(qkv, qkv, qkv, cum_q, cum_k)


def _fox_bwd(name, qkv, off, width, bl, seq, cum_q, cum_k, lse, o, do):
    t_rows = qkv.shape[0]
    blk = _tile(seq, ATT_BLOCK)
    nq = seq // blk
    nb = width // LANE
    scale = HEAD_DIM ** -0.5
    qs, ks, col, lane = _att_specs(seq, blk, nq, off)

    def body(q_ref, k_ref, v_ref, cq_ref, ck_ref, lse_ref, o_ref, do_ref,
             dq_ref, dk_ref, dv_ref, dcq_ref, dck_ref, dk_acc, dv_acc):
        qi = pl.program_id(2)

        @pl.when(qi == 0)
        def _():
            dk_acc[...] = jnp.zeros_like(dk_acc)
            dv_acc[...] = jnp.zeros_like(dv_acc)
            dck_ref[...] = jnp.zeros_like(dck_ref)

        masks = _head_masks()
        qv = q_ref[...]
        dof = do_ref[...]
        dov = dof.astype(BF16)
        prod = dof * o_ref[...]
        qh = [_only(m, qv) for m in masks]
        doh = [_only(m, dov) for m in masks]
        delta = [jnp.sum(_only(m, prod), axis=1, keepdims=True) for m in masks]
        shift = [cq_ref[h] - lse_ref[h] for h in range(PAIR)]
        row = lax.broadcasted_iota(jnp.int32, (blk, blk), 0)
        cix = lax.broadcasted_iota(jnp.int32, (blk, blk), 1)

        def step(kb, carry):
            dq, dcqs = carry
            ks_ = pl.multiple_of(kb * blk, blk)
            kv = k_ref[pl.ds(ks_, blk), :]
            vv = v_ref[pl.ds(ks_, blk), :]
            causal = (kb * blk + cix) <= (qi * blk + row)
            dk_new = jnp.zeros((blk, LANE), F32)
            dv_new = jnp.zeros((blk, LANE), F32)
            new_dcq = []
            for h in range(PAIR):
                z = lax.dot_general(qh[h], kv, (NT, ((), ())), preferred_element_type=F32) * scale
                z = z + shift[h] - ck_ref[h, :, pl.ds(ks_, blk)]
                pv = jnp.where(causal, jnp.exp(z), 0.0)
                dp = lax.dot_general(doh[h], vv, (NT, ((), ())), preferred_element_type=F32)
                ds = pv * (dp - delta[h])
                dsb = (ds * scale).astype(BF16)
                dq = dq + jnp.dot(dsb, _only(masks[h], kv), preferred_element_type=F32)
                dk_new = dk_new + lax.dot_general(dsb, qh[h], (TN, ((), ())), preferred_element_type=F32)
                dv_new = dv_new + lax.dot_general(pv.astype(BF16), doh[h], (TN, ((), ())),
                                                  preferred_element_type=F32)
                dck_ref[h, :, pl.ds(ks_, blk)] += -jnp.sum(ds, axis=0, keepdims=True)
                new_dcq.append(dcqs[h] + jnp.sum(ds, axis=1, keepdims=True))
            dk_acc[pl.ds(ks_, blk), :] += dk_new
            dv_acc[pl.ds(ks_, blk), :] += dv_new
            return dq, tuple(new_dcq)

        zero = (jnp.zeros((blk, 1), F32),) * PAIR
        dq, dcqs = lax.fori_loop(0, qi + 1, step, (jnp.zeros((blk, LANE), F32), zero))
        dq_ref[...] = dq.astype(dq_ref.dtype)
        for h in range(PAIR):
            dcq_ref[h] = dcqs[h]

        @pl.when(qi == nq - 1)
        def _():
            dk_ref[...] = dk_acc[...].astype(dk_ref.dtype)
            dv_ref[...] = dv_acc[...].astype(dv_ref.dtype)

    ob = jax.ShapeDtypeStruct((t_rows, width), BF16)
    nh = nb * PAIR
    return pl.pallas_call(
        body, name=name, grid=(bl, nb, nq),
        in_specs=[qs(off), ks(off + nb), ks(off + 2 * nb), col, lane, col, qs(0), qs(0)],
        out_specs=[qs(0), ks(0), ks(0), col, lane],
        out_shape=[ob, ob, ob, jax.ShapeDtypeStruct((bl, nh, seq, 1), F32), jax.ShapeDtypeStruct((bl, nh, 1, seq), F32)],
        scratch_shapes=[pltpu.VMEM((seq, LANE), F32), pltpu.VMEM((seq, LANE), F32)],
        compiler_params=_params(),
    )(qkv, qkv, qkv, cum_q, cum_k, lse, o, do)


def _lane_cumsum(v, reverse):
    n = v.shape[1]
    cix = lax.broadcasted_iota(jnp.int32, v.shape, 1)
    s = 1
    while s < n:
        if reverse:
            v = v + jnp.where(cix < n - s, pltpu.roll(v, n - s, 1), 0.0)
        else:
            v = v + jnp.where(cix >= s, pltpu.roll(v, s, 1), 0.0)
        s *= 2
    return v


def _forget_cum(name, fl, bf):
    def body(fl_ref, bf_ref, o_ref):
        xv = fl_ref[...] + bf_ref[...]
        o_ref[...] = _lane_cumsum(-_softplus(-xv), False)

    return pl.pallas_call(body, name=name, out_shape=jax.ShapeDtypeStruct(fl.shape, F32),
                          compiler_params=_params())(fl, bf)


def _forget_cum_bwd(name, fl, bf, dcum, nh):
    rows = fl.shape[0]

    def body(fl_ref, bf_ref, dc_ref, dfl_ref, dbf_ref):
        xv = fl_ref[...] + bf_ref[...]
        dlogf = _lane_cumsum(dc_ref[...], True)
        dfl = dlogf * _sigmoid(-xv)
        dfl_ref[...] = dfl
        per_row = jnp.sum(dfl, axis=1, keepdims=True)
        tot = per_row[0:nh]
        for b in range(1, rows // nh):
            tot = tot + per_row[b * nh:(b + 1) * nh]
        dbf_ref[...] = tot

    return pl.pallas_call(
        body, name=name,
        out_shape=[jax.ShapeDtypeStruct(fl.shape, F32), jax.ShapeDtypeStruct((nh, 1), F32)],
        compiler_params=_params(),
    )(fl, bf, dcum)


def _merge_fwd(name, proj, off, merge_b, pa, pb, pc):
    t_rows, d = pa.shape
    tm = _tile(t_rows, 256)

    def body(l0, l1, l2, mb, a_ref, b_ref, c_ref, o_ref):
        g0 = _sigmoid(l0[...] + mb[:, 0:d])
        g1 = _sigmoid(l1[...] + mb[:, d:2 * d])
        g2 = _sigmoid(l2[...] + mb[:, 2 * d:3 * d])
        o_ref[...] = (g0 * a_ref[...] + g1 * b_ref[...] + g2 * c_ref[...]).astype(o_ref.dtype)

    row = pl.BlockSpec((tm, d), lambda i: (i, 0))
    lg = lambda j: pl.BlockSpec((tm, d), lambda i: (i, off + j))
    return pl.pallas_call(
        body, name=name, grid=(t_rows // tm,),
        in_specs=[lg(0), lg(1), lg(2), pl.BlockSpec((1, 3 * d), lambda i: (0, 0)), row, row, row],
        out_specs=row, out_shape=jax.ShapeDtypeStruct((t_rows, d), BF16), compiler_params=_params(),
    )(proj, proj, proj, merge_b, pa, pb, pc)


def _merge_bwd(name, proj, off, merge_b, pa, pb, pc, dmixed):
    t_rows, d = pa.shape
    tm = _tile(t_rows, 256)

    def body(l0, l1, l2, mb, a_ref, b_ref, c_ref, dm_ref, da_ref, db_ref, dc_ref, dl_ref, dmb_ref):
        i = pl.program_id(0)
        dm = dm_ref[...]
        parts = []
        for j, (lref, pref, dref) in enumerate(((l0, a_ref, da_ref), (l1, b_ref, db_ref), (l2, c_ref, dc_ref))):
            g = _sigmoid(lref[...] + mb[:, j * d:(j + 1) * d])
            dref[...] = (g * dm).astype(dref.dtype)
            dl = dm * pref[...] * g * (1.0 - g)
            dl_ref[:, j * d:(j + 1) * d] = dl.astype(dl_ref.dtype)
            parts.append(jnp.sum(dl, axis=0, keepdims=True))
        tot = jnp.concatenate(parts, axis=1)

        @pl.when(i == 0)
        def _():
            dmb_ref[...] = tot

        @pl.when(i > 0)
        def _():
            dmb_ref[...] += tot

    row = pl.BlockSpec((tm, d), lambda i: (i, 0))
    lg = lambda j: pl.BlockSpec((tm, d), lambda i: (i, off + j))
    one = pl.BlockSpec((1, 3 * d), lambda i: (0, 0))
    b16 = jax.ShapeDtypeStruct((t_rows, d), BF16)
    return pl.pallas_call(
        body, name=name, grid=(t_rows // tm,),
        in_specs=[lg(0), lg(1), lg(2), one, row, row, row, row],
        out_specs=[row, row, row, pl.BlockSpec((tm, 3 * d), lambda i: (i, 0)), one],
        out_shape=[b16, b16, b16, jax.ShapeDtypeStruct((t_rows, 3 * d), BF16), jax.ShapeDtypeStruct((1, 3 * d), F32)],
        compiler_params=_params(),
    )(proj, proj, proj, merge_b, pa, pb, pc, dmixed)


def _grouped_nn(name, a, wg, l, out_dtype):
    t_rows, kk = a.shape
    ng, ncol = wg.shape[0], wg.shape[3]
    tm = _tile(t_rows, 512)
    return _mmk(name, [a, wg],
                [pl.BlockSpec((tm, kk), lambda i, g: (i, 0)),
                 pl.BlockSpec((None, None, kk, ncol), lambda i, g: (g, l, 0, 0))],
                [(0, ALL, 1, ALL, NN)], (t_rows, ng * ncol), out_dtype, (t_rows // tm, ng),
                pl.BlockSpec((tm, ncol), lambda i, g: (i, g)))


def _grouped_nt(name, da, wg, l, out_dtype):
    t_rows = da.shape[0]
    ng, kk, ncol = wg.shape[0], wg.shape[2], wg.shape[3]
    tm = _tile(t_rows, 512)
    return _mmk(name, [da, wg],
                [pl.BlockSpec((tm, ng * ncol), lambda i: (i, 0)),
                 pl.BlockSpec((ng, None, kk, ncol), lambda i: (0, l, 0, 0))],
                [(0, (ALL, slice(g * ncol, (g + 1) * ncol)), 1, g, NT) for g in range(ng)],
                (t_rows, kk), out_dtype, (t_rows // tm,), pl.BlockSpec((tm, kk), lambda i: (i, 0)))


def _grouped_tn(name, a, da, ng, out_dtype):
    t_rows, kk = a.shape
    ncol = da.shape[1] // ng
    return _mmk(name, [a, da],
                [pl.BlockSpec((t_rows, kk), lambda g: (0, 0)), pl.BlockSpec((t_rows, ncol), lambda g: (0, g))],
                [(0, ALL, 1, ALL, TN)], (ng, kk, ncol), out_dtype, (ng,),
                pl.BlockSpec((None, kk, ncol), lambda g: (g, 0, 0)))


def _rows_nn(name, a, wr, l, out_dtype):
    t_rows = a.shape[0]
    ng, kg, n = wr.shape[0], wr.shape[2], wr.shape[3]
    tm, tn = _tile(t_rows, 512), _tile(n, 512)
    return _mmk(name, [a, wr],
                [pl.BlockSpec((tm, ng * kg), lambda i, j: (i, 0)),
                 pl.BlockSpec((ng, None, kg, tn), lambda i, j: (0, l, 0, j))],
                [(0, (ALL, slice(g * kg, (g + 1) * kg)), 1, g, NN) for g in range(ng)],
                (t_rows, n), out_dtype, (t_rows // tm, n // tn), pl.BlockSpec((tm, tn), lambda i, j: (i, j)))


def _rows_nt(name, dy, wr, l, out_dtype):
    t_rows, n = dy.shape
    ng, kg = wr.shape[0], wr.shape[2]
    tm = _tile(t_rows, 512)
    return _mmk(name, [dy, wr],
                [pl.BlockSpec((tm, n), lambda i, g: (i, 0)),
                 pl.BlockSpec((None, None, kg, n), lambda i, g: (g, l, 0, 0))],
                [(0, ALL, 1, ALL, NT)], (t_rows, ng * kg), out_dtype, (t_rows // tm, ng),
                pl.BlockSpec((tm, kg), lambda i, g: (i, g)))


def _rows_tn(name, a, dy, ng, out_dtype):
    t_rows, n = dy.shape
    kg = a.shape[1] // ng
    tn = _tile(n, 512)
    return _mmk(name, [a, dy],
                [pl.BlockSpec((t_rows, kg), lambda g, j: (0, g)), pl.BlockSpec((t_rows, tn), lambda g, j: (0, j))],
                [(0, ALL, 1, ALL, TN)], (ng, kg, n), out_dtype, (ng, n // tn),
                pl.BlockSpec((None, kg, tn), lambda g, j: (g, 0, j)))


def _mix_fwd(tag, w, l, x, mod, seq):
    t_rows, d = x.shape
    bl = t_rows // seq
    shift, scale, gate = mod
    chans, nh = w["layout"]["chans"], w["layout"]["heads"]
    width = nh * HEAD_DIM
    nb = width // LANE
    h = _normmod(tag + "_norm", x, w["mix_norm"][l][None], shift, scale, seq)
    proj = _mm(tag + "_in_a", h, w["w_a"][l], NN, F32)
    qkv = _mm(tag + "_in_b", h, w["w_b"][l], NN, BF16)
    flp = _mm(tag + "_in_f", h, w["w_f"][l], NN, F32)
    xa, r, ig, hs, ya = _rg_fwd(tag + "_rg", proj, w["rg"][l], seq, chans)
    yb, ltot = _sb_fwd(tag + "_sb", qkv, 0, width, bl, seq)
    fl = flp[:, :nh].reshape(bl, seq, nh).transpose(0, 2, 1).reshape(bl * nh, seq)
    bf = jnp.tile(w["fox_bf"][l].reshape(nh, 1), (bl, 1))
    cum = _forget_cum(tag + "_cum", fl, bf)
    cum_q = cum.reshape(bl, nh, seq, 1)
    cum_k = cum.reshape(bl, nh, 1, seq)
    yc, oc, lse = _fox_fwd(tag + "_fox", qkv, 3 * nb, width, bl, seq, cum_q, cum_k)
    pa = _rows_nn(tag + "_prg", ya, w["w_rg"], l, F32)
    pb = _grouped_nn(tag + "_psb", yb, w["w_sb"], l, F32)
    pc = _grouped_nn(tag + "_pfox", yc, w["w_fox"], l, F32)
    moff = 2 * chans // d
    mb = w["merge_b"][l][None]
    mixed = _merge_fwd(tag + "_merge", proj, moff, mb, pa, pb, pc)
    y = _rows_nn(tag + "_out", mixed, w["w_o"], l, F32)
    xn = _resid(tag + "_res", x, y, gate, 1.0, seq)
    saved = dict(x=x, h=h, proj=proj, qkv=qkv, rg=(xa, r, ig, hs), ya=ya, ltot=ltot,
                 fox=(cum_q, cum_k, lse, oc), fl=fl, bf=bf, yb=yb, yc=yc, pa=pa, pb=pb, pc=pc, mixed=mixed, y=y)
    return xn, saved


def _mix_bwd(tag, w, l, s, mod, dxo, seq):
    x = s["x"]
    t_rows, d = x.shape
    bl = t_rows // seq
    shift, scale, gate = mod
    chans, nh = w["layout"]["chans"], w["layout"]["heads"]
    width = nh * HEAD_DIM
    nb = width // LANE
    moff = 2 * chans // d
    mb = w["merge_b"][l][None]
    ng = NUM_CHIPS
    dy, dgate = _resid_bwd(tag + "_res_bwd", dxo, s["y"], gate, 1.0, seq)
    dmixed = _rows_nt(tag + "_out_dx", dy, w["w_o"], l, F32)
    dw_o = _rows_tn(tag + "_out_dw", s["mixed"], dy, ng, BF16)
    dpa, dpb, dpc, dlog, dmb = _merge_bwd(tag + "_merge_bwd", s["proj"], moff, mb, s["pa"], s["pb"], s["pc"], dmixed)
    dya = _rows_nt(tag + "_prg_dx", dpa, w["w_rg"], l, F32)
    dw_rg = _rows_tn(tag + "_prg_dw", s["ya"], dpa, ng, BF16)
    dyb = _grouped_nt(tag + "_psb_dx", dpb, w["w_sb"], l, F32)
    dw_sb = _grouped_tn(tag + "_psb_dw", s["yb"], dpb, ng, BF16)
    dyc = _grouped_nt(tag + "_pfox_dx", dpc, w["w_fox"], l, F32)
    dw_fox = _grouped_tn(tag + "_pfox_dw", s["yc"], dpc, ng, BF16)
    qkv = s["qkv"]
    dq_b, dk_b, dv_b = _sb_bwd(tag + "_sb_bwd", qkv, 0, width, bl, seq, s["ltot"], dyb)
    cum_q, cum_k, lse, oc = s["fox"]
    dq_c, dk_c, dv_c, dcq, dck = _fox_bwd(tag + "_fox_bwd", qkv, 3 * nb, width, bl, seq, cum_q, cum_k, lse, oc, dyc)
    dcum = dcq.reshape(bl * nh, seq) + dck.reshape(bl * nh, seq)
    dfl, dbf = _forget_cum_bwd(tag + "_cum_bwd", s["fl"], s["bf"], dcum, nh)
    dfl_t = dfl.reshape(bl, nh, seq).transpose(0, 2, 1).reshape(t_rows, nh)
    dflp = jnp.pad(dfl_t, ((0, 0), (0, LANE - nh))).astype(BF16)
    drgx, dgt, rg_small, dwa, dwx = _rg_bwd(tag + "_rg_bwd", s["proj"], dya, s["rg"], w["rg"][l], seq, chans)
    dproj = jnp.concatenate([drgx, dgt, dlog], axis=1)
    dqkv = jnp.concatenate([dq_b, dk_b, dv_b, dq_c, dk_c, dv_c], axis=1)
    w_a, w_b, w_f = w["w_a"][l], w["w_b"][l], w["w_f"][l]
    pa_w, pb_w = w_a.shape[1], w_b.shape[1]
    tm, tn = _tile(t_rows, 512), _tile(d, 512)
    rows = lambda n: pl.BlockSpec((tm, n), lambda i, j: (i, 0))
    wrow = lambda n: pl.BlockSpec((tn, n), lambda i, j: (j, 0))
    dh = _mmk(tag + "_in_dx", [dproj, dqkv, dflp, w_a, w_b, w_f],
              [rows(pa_w), rows(pb_w), rows(LANE), wrow(pa_w), wrow(pb_w), wrow(LANE)],
              [(0, ALL, 3, ALL, NT), (1, ALL, 4, ALL, NT), (2, ALL, 5, ALL, NT)],
              (t_rows, d), F32, (t_rows // tm, d // tn), pl.BlockSpec((tm, tn), lambda i, j: (i, j)))
    hb = s["h"]
    dw_a = _mm(tag + "_in_a_dw", hb, dproj, TN, BF16)
    dw_b = _mm(tag + "_in_b_dw", hb, dqkv, TN, BF16)
    dw_f = _mm(tag + "_in_f_dw", hb, dflp, TN, BF16)
    dx, dshift, dscale, dgain = _normmod_bwd(tag + "_norm_bwd", x, dh, dxo, w["mix_norm"][l][None], scale, seq)
    grads = dict(w_in=(dw_a, dw_b, dw_f), w_rg=dw_rg, w_sb=dw_sb, w_fox=dw_fox, w_o=dw_o, mix_norm=dgain,
                 rg_small=rg_small, rg_dwa=dwa, rg_dwx=dwx, fox_bf=dbf, merge_b=dmb)
    return dx, (dshift, dscale, dgate), grads


def _silu(name, c):
    def body(c_ref, o_ref):
        v = c_ref[...]
        o_ref[...] = v * _sigmoid(v)

    return pl.pallas_call(body, name=name, out_shape=jax.ShapeDtypeStruct(c.shape, F32),
                          compiler_params=_params())(c)


def _blockdiag(wb):
    nb, bd, _ = wb.shape
    per = LANE // bd
    t = wb.reshape(nb // per, per, bd, 1, bd)
    eye = jnp.eye(per, dtype=wb.dtype).reshape(1, per, 1, per, 1)
    return (t * eye).reshape(nb // per, LANE, LANE).astype(BF16)


def _unblockdiag(t, bd):
    n = t.shape[0]
    per = LANE // bd
    t5 = t.reshape(n, per, bd, per, bd)
    return jnp.stack([t5[:, p, :, p, :] for p in range(per)], axis=1).reshape(n * per, bd, bd)


def _prepare(gw, a, d, chans, nh):
    depth = a["ada_b"].shape[0]
    wq = 3 * nh * HEAD_DIM
    o_m = 2 * chans + 2 * wq
    w = {"layout": dict(chans=chans, heads=nh)}
    for n in ("ffn1_w1", "ffn1_w3", "ffn1_w2", "ffn2_w1", "ffn2_w3", "ffn2_w2", "w_rg", "w_sb", "w_fox", "w_o"):
        w[n] = gw[n]
    for n in ("ffn1_norm", "ffn2_norm", "mix_norm", "fox_bf", "merge_b", "final_norm"):
        w[n] = a[n]
    w_a, w_b, w_f, rg = [], [], [], []
    for l in range(depth):
        full = gw["w_in"][:, l].transpose(1, 0, 2).reshape(d, -1)
        w_a.append(jnp.concatenate([full[:, :2 * chans], full[:, o_m + nh:]], axis=1))
        w_b.append(full[:, 2 * chans:o_m])
        w_f.append(jnp.pad(full[:, o_m:o_m + nh], ((0, 0), (0, LANE - nh))))
        conv_w = gw["conv_w"][:, l].transpose(1, 0, 2).reshape(-1, chans)
        rg.append(dict(conv_w=conv_w, conv_b=a["conv_b"][l][None], ba=a["rg_ba"][l][None], bx=a["rg_bx"][l][None],
                       lam=a["rg_lam"][l][None], wa=_blockdiag(a["rg_wa"][l]), wx=_blockdiag(a["rg_wx"][l])))
    w["w_a"], w["w_b"], w["w_f"], w["rg"] = w_a, w_b, w_f, rg
    return w


def _local_step(w, x, tgt, mods, fm):
    bl, seq, d = x.shape
    t_rows = bl * seq
    depth = len(mods)
    mod3 = []
    for l in range(depth):
        m4 = mods[l].reshape(bl, 9, 1, d)
        mod3.append([(m4[:, 3 * k], m4[:, 3 * k + 1], m4[:, 3 * k + 2]) for k in range(3)])
    fm4 = fm.reshape(bl, 2, 1, d)
    saved = []
    xc = x.reshape(t_rows, d)
    for l in range(depth):
        xc, s1 = _ffn_fwd(f"l{l}_ffn1", w, l, "ffn1_", xc, mod3[l][0], seq)
        xc, s2 = _mix_fwd(f"l{l}_mix", w, l, xc, mod3[l][1], seq)
        xc, s3 = _ffn_fwd(f"l{l}_ffn2", w, l, "ffn2_", xc, mod3[l][2], seq)
        saved.append((s1, s2, s3))
    lpart, dx, dfs, dfc, dfg = _final_loss("final", xc, tgt.reshape(t_rows, d), w["final_norm"][None],
                                           fm4[:, 0], fm4[:, 1], seq)
    loss = jnp.sum(lpart[:, 0, 0])
    grads = {"final_norm": dfg, "layers": [None] * depth}
    dmods = [None] * depth
    for l in reversed(range(depth)):
        s1, s2, s3 = saved[l]
        dx, dm3, g3 = _ffn_bwd(f"l{l}_ffn2", w, l, "ffn2_", s3, mod3[l][2], dx, seq)
        dx, dm2, g2 = _mix_bwd(f"l{l}_mix", w, l, s2, mod3[l][1], dx, seq)
        dx, dm1, g1 = _ffn_bwd(f"l{l}_ffn1", w, l, "ffn1_", s1, mod3[l][0], dx, seq)
        dmods[l] = jnp.concatenate([*dm1, *dm2, *dm3], axis=1).reshape(bl, 9 * d)
        grads["layers"][l] = {**g1, **g2, **g3}
    dfm = jnp.concatenate([dfs, dfc], axis=1).reshape(bl, 2 * d)
    return loss, dx.reshape(bl, seq, d), grads, dmods, dfm


def _mesh_pos():
    return lax.axis_index("x"), lax.axis_index("y"), lax.axis_index("c")


def _other_chips(x, y):
    return ((1 - x, y), (x, 1 - y), (1 - x, 1 - y))


def _gather_two_level(name, arrs):
    n = len(arrs)

    def body(*refs):
        ins, outs = refs[:n], refs[n:2 * n]
        send, recv, send2, recv2, loc = refs[2 * n:]
        x, y, c = _mesh_pos()
        me = 2 * x + y
        chips = _other_chips(x, y)
        sib = (x, y, 1 - c)
        local = [pltpu.make_async_copy(ins[i], outs[i].at[me], loc.at[i]) for i in range(n)]
        for cp in local:
            cp.start()
        first = []
        for j, (px, py) in enumerate(chips):
            for i in range(n):
                first.append(pltpu.make_async_remote_copy(
                    src_ref=ins[i].at[c], dst_ref=outs[i].at[me, c], send_sem=send.at[j * n + i],
                    recv_sem=recv.at[j * n + i], device_id=(px, py, c), device_id_type=MESH))
        for cp in first:
            cp.start()
        passed = []
        for j, (px, py) in enumerate(chips):
            for i in range(n):
                landed = outs[i].at[2 * px + py, c]
                pltpu.make_async_remote_copy(
                    src_ref=ins[i].at[c], dst_ref=landed, send_sem=send.at[j * n + i],
                    recv_sem=recv.at[j * n + i], device_id=(px, py, c), device_id_type=MESH).wait_recv()
                fwd = pltpu.make_async_remote_copy(
                    src_ref=landed, dst_ref=landed, send_sem=send2.at[j * n + i],
                    recv_sem=recv2.at[j * n + i], device_id=sib, device_id_type=MESH)
                fwd.start()
                passed.append(fwd)
        for j, (px, py) in enumerate(chips):
            for i in range(n):
                theirs = outs[i].at[2 * px + py, 1 - c]
                pltpu.make_async_remote_copy(
                    src_ref=theirs, dst_ref=theirs, send_sem=send2.at[j * n + i],
                    recv_sem=recv2.at[j * n + i], device_id=sib, device_id_type=MESH).wait_recv()
        for cp in first + passed:
            cp.wait_send()
        for cp in local:
            cp.wait()

    return pl.pallas_call(
        body, name=name, in_specs=[ANY] * n, out_specs=[ANY] * n,
        out_shape=[jax.ShapeDtypeStruct((NUM_CHIPS,) + a.shape, a.dtype) for a in arrs],
        scratch_shapes=[pltpu.SemaphoreType.DMA((3 * n,)), pltpu.SemaphoreType.DMA((3 * n,)),
                        pltpu.SemaphoreType.DMA((3 * n,)), pltpu.SemaphoreType.DMA((3 * n,)),
                        pltpu.SemaphoreType.DMA((n,))],
    )(*arrs)


def _split_to_sibling(name, arrs):
    n = len(arrs)

    def body(*refs):
        ins, mine, theirs = refs[:n], refs[n:2 * n], refs[2 * n:3 * n]
        send, recv, loc = refs[3 * n:]
        x, y, c = _mesh_pos()
        copies, local = [], []
        for i in range(n):
            half = ins[i].shape[1] // 2
            local.append(pltpu.make_async_copy(ins[i].at[:, pl.ds(c * half, half)], mine[i], loc.at[i]))
            copies.append(pltpu.make_async_remote_copy(
                src_ref=ins[i].at[:, pl.ds((1 - c) * half, half)], dst_ref=theirs[i], send_sem=send.at[i],
                recv_sem=recv.at[i], device_id=(x, y, 1 - c), device_id_type=MESH))
        for cp in local + copies:
            cp.start()
        for cp in copies:
            cp.wait()
        for cp in local:
            cp.wait()

    halves = [jax.ShapeDtypeStruct((a.shape[0], a.shape[1] // 2, a.shape[2]), a.dtype) for a in arrs]
    return pl.pallas_call(
        body, name=name, in_specs=[ANY] * n, out_specs=[ANY] * (2 * n), out_shape=halves + halves,
        scratch_shapes=[pltpu.SemaphoreType.DMA((n,)), pltpu.SemaphoreType.DMA((n,)), pltpu.SemaphoreType.DMA((n,))],
    )(*arrs)


def _scatter_chips(name, arrs):
    n = len(arrs)

    def body(*refs):
        ins, outs = refs[:n], refs[n:2 * n]
        send, recv, loc = refs[2 * n:]
        x, y, c = _mesh_pos()
        me = 2 * x + y
        chips = _other_chips(x, y)
        local = [pltpu.make_async_copy(ins[i].at[me], outs[i].at[me], loc.at[i]) for i in range(n)]
        for cp in local:
            cp.start()
        sends = []
        for j, (px, py) in enumerate(chips):
            for i in range(n):
                sends.append(pltpu.make_async_remote_copy(
                    src_ref=ins[i].at[2 * px + py], dst_ref=outs[i].at[me], send_sem=send.at[j * n + i],
                    recv_sem=recv.at[j * n + i], device_id=(px, py, c), device_id_type=MESH))
        for s in sends:
            s.start()
        for j, (px, py) in enumerate(chips):
            for i in range(n):
                pltpu.make_async_remote_copy(
                    src_ref=ins[i].at[me], dst_ref=outs[i].at[2 * px + py], send_sem=send.at[j * n + i],
                    recv_sem=recv.at[j * n + i], device_id=(px, py, c), device_id_type=MESH).wait_recv()
        for s in sends:
            s.wait_send()
        for cp in local:
            cp.wait()

    return pl.pallas_call(
        body, name=name, in_specs=[ANY] * n, out_specs=[ANY] * n,
        out_shape=[jax.ShapeDtypeStruct(a.shape, a.dtype) for a in arrs],
        scratch_shapes=[pltpu.SemaphoreType.DMA((3 * n,)), pltpu.SemaphoreType.DMA((3 * n,)),
                        pltpu.SemaphoreType.DMA((n,))],
    )(*arrs)


def _join_halves(name, arrs):
    n = len(arrs)

    def body(*refs):
        ins, outs = refs[:n], refs[n:2 * n]
        send, recv, loc = refs[2 * n:]
        x, y, c = _mesh_pos()
        copies, local = [], []
        for i in range(n):
            half = ins[i].shape[0]
            local.append(pltpu.make_async_copy(ins[i], outs[i].at[pl.ds(c * half, half)], loc.at[i]))
            copies.append(pltpu.make_async_remote_copy(
                src_ref=ins[i], dst_ref=outs[i].at[pl.ds(c * half, half)], send_sem=send.at[i],
                recv_sem=recv.at[i], device_id=(x, y, 1 - c), device_id_type=MESH))
        for cp in local + copies:
            cp.start()
        for i in range(n):
            half = ins[i].shape[0]
            copies[i].wait_send()
            pltpu.make_async_remote_copy(
                src_ref=ins[i], dst_ref=outs[i].at[pl.ds((1 - c) * half, half)], send_sem=send.at[i],
                recv_sem=recv.at[i], device_id=(x, y, 1 - c), device_id_type=MESH).wait_recv()
        for cp in local:
            cp.wait()

    return pl.pallas_call(
        body, name=name, in_specs=[ANY] * n, out_specs=[ANY] * n,
        out_shape=[jax.ShapeDtypeStruct((2 * a.shape[0], a.shape[1]), a.dtype) for a in arrs],
        scratch_shapes=[pltpu.SemaphoreType.DMA((n,)), pltpu.SemaphoreType.DMA((n,)), pltpu.SemaphoreType.DMA((n,))],
    )(*arrs)


def _gather_all(name, pack):
    def body(in_ref, out_ref, send, recv, loc):
        x, y, c = _mesh_pos()
        me = 4 * x + 2 * y + c
        mine = pltpu.make_async_copy(in_ref, out_ref.at[me], loc)
        mine.start()
        peers = []
        for mask in range(1, NUM_DEVICES):
            px = 1 - x if mask & 4 else x
            py = 1 - y if mask & 2 else y
            pc = 1 - c if mask & 1 else c
            peers.append((px, py, pc))
        sends = [pltpu.make_async_remote_copy(
            src_ref=in_ref, dst_ref=out_ref.at[me], send_sem=send.at[k], recv_sem=recv.at[k],
            device_id=p, device_id_type=MESH) for k, p in enumerate(peers)]
        for s in sends:
            s.start()
        for k, (px, py, pc) in enumerate(peers):
            pltpu.make_async_remote_copy(
                src_ref=in_ref, dst_ref=out_ref.at[4 * px + 2 * py + pc], send_sem=send.at[k], recv_sem=recv.at[k],
                device_id=(px, py, pc), device_id_type=MESH).wait_recv()
        for s in sends:
            s.wait_send()
        mine.wait()

    return pl.pallas_call(
        body, name=name, in_specs=[ANY], out_specs=ANY,
        out_shape=jax.ShapeDtypeStruct((NUM_DEVICES,) + pack.shape, pack.dtype),
        scratch_shapes=[pltpu.SemaphoreType.DMA((NUM_DEVICES - 1,)), pltpu.SemaphoreType.DMA((NUM_DEVICES - 1,)),
                        pltpu.SemaphoreType.DMA],
    )(pack)


def _sum_slots(name, slots, out_dtype):
    g, rows, cols = slots.shape
    tr = _rtile(rows, 256)

    def body(s_ref, o_ref):
        acc = s_ref[0].astype(F32)
        for k in range(1, g):
            acc = acc + s_ref[k].astype(F32)
        o_ref[...] = acc.astype(o_ref.dtype)

    return pl.pallas_call(
        body, name=name, grid=(rows // tr,),
        in_specs=[pl.BlockSpec((g, tr, cols), lambda i: (0, i, 0))],
        out_specs=pl.BlockSpec((tr, cols), lambda i: (i, 0)),
        out_shape=jax.ShapeDtypeStruct((rows, cols), out_dtype), compiler_params=_params(),
    )(slots)


def _add_pair(name, p, q):
    g, rows, cols = p.shape
    tr = _rtile(rows, 128)

    def body(p_ref, q_ref, o_ref):
        o_ref[...] = (p_ref[...].astype(F32) + q_ref[...].astype(F32)).astype(o_ref.dtype)

    spec = pl.BlockSpec((g, tr, cols), lambda i: (0, i, 0))
    return pl.pallas_call(
        body, name=name, grid=(rows // tr,), in_specs=[spec, spec], out_specs=spec,
        out_shape=jax.ShapeDtypeStruct(p.shape, BF16), compiler_params=_params(),
    )(p, q)


def _adamw(name, g, w, m, v, l=None):
    rows, cols = g.shape
    tr = _rtile(rows, 128)

    def body(g_ref, w_ref, m_ref, v_ref, d_o, m_o, v_o):
        gv = g_ref[...]
        mn = ADAM_B1 * m_ref[...] + (1.0 - ADAM_B1) * gv
        vn = ADAM_B2 * v_ref[...] + (1.0 - ADAM_B2) * (gv * gv)
        m_hat = mn / (1.0 - ADAM_B1 ** ADAM_STEP)
        v_hat = vn / (1.0 - ADAM_B2 ** ADAM_STEP)
        d_o[...] = -ADAM_LR * (m_hat / (jnp.sqrt(v_hat) + ADAM_EPS) + ADAM_WD * w_ref[...])
        m_o[...] = mn
        v_o[...] = vn

    gspec = pl.BlockSpec((tr, cols), lambda i: (i, 0))
    wspec = gspec if l is None else pl.BlockSpec((None, tr, cols), lambda i: (l, i, 0))
    f = jax.ShapeDtypeStruct((rows, cols), F32)
    return pl.pallas_call(
        body, name=name, grid=(rows // tr,), in_specs=[gspec] + [wspec] * 3, out_specs=[gspec] * 3,
        out_shape=[f] * 3, compiler_params=_params(),
    )(g, w, m, v)


def _colsum(name, a):
    def body(a_ref, o_ref):
        o_ref[...] = jnp.sum(a_ref[...], axis=0, keepdims=True)

    return pl.pallas_call(body, name=name, out_shape=jax.ShapeDtypeStruct((1, a.shape[1]), F32),
                          compiler_params=_params())(a)


PACK_UNIT = SUBLANE * LANE


def _pack(items):
    flat, layout, o = [], [], 0
    for it in items:
        n = it.size
        pad = -n % PACK_UNIT
        flat.append(jnp.pad(it.reshape(-1).astype(F32), (0, pad)))
        layout.append((o, n, it.shape))
        o += n + pad
    return jnp.concatenate(flat).reshape(-1, LANE), layout


def _unpack(pack, layout):
    flat = pack.reshape(-1)
    return [flat[o:o + n].reshape(shape) for o, n, shape in layout]


WEIGHTS = ("ffn1_norm", "ffn1_w1", "ffn1_w3", "ffn1_w2", "mix_norm", "w_in", "conv_w", "conv_b", "rg_wa", "rg_ba",
           "rg_wx", "rg_bx", "rg_lam", "fox_bf", "merge_b", "w_rg", "w_sb", "w_fox", "w_o", "ffn2_norm", "ffn2_w1",
           "ffn2_w3", "ffn2_w2", "ada_w", "ada_b", "final_norm", "final_ada_w", "final_ada_b")
DENSE = ("ffn1_w1", "ffn1_w3", "ffn1_w2", "w_in", "w_rg", "w_sb", "w_fox", "w_o", "ffn2_w1", "ffn2_w3", "ffn2_w2")
SMALL = ("ffn1_norm", "mix_norm", "ffn2_norm", "rg_small", "rg_wa", "rg_wx", "fox_bf", "merge_b")


def _step(a):
    x, c, tgt = a["x"], a["c"], a["loss_target"]
    bl, seq, d = x.shape
    depth, nh = a["fox_bf"].shape
    chans = a["rg_lam"].shape[1]
    bd = a["rg_wa"].shape[2]
    wq = 3 * nh * HEAD_DIM
    o_m = 2 * chans + 2 * wq
    batch = NUM_DEVICES * bl
    mx, my, mc = _mesh_pos()
    me = 2 * mx + my
    dev = 4 * mx + 2 * my + mc

    c_rows = -(-bl * d // LANE // SUBLANE) * SUBLANE
    c_pack = jnp.pad(c.reshape(-1, LANE), ((0, c_rows - bl * d // LANE), (0, 0)))
    c_all = _gather_all("gather_c", c_pack)[:, :bl * d // LANE].reshape(batch, d)
    c_act = _silu("c_act", c_all)
    c_b = c_act.astype(BF16)
    ncol, fcol = a["ada_w"].shape[2], a["final_ada_w"].shape[1]
    cols = []
    for l in range(depth):
        bias = jnp.broadcast_to(lax.dynamic_slice_in_dim(a["ada_b"][l], me * ncol, ncol)[None], (batch, ncol))
        cols.append(_mm(f"ada{l}", c_b, a["ada_w"][l].astype(BF16), NN, F32, acc=bias))
    bias = jnp.broadcast_to(lax.dynamic_slice_in_dim(a["final_ada_b"], me * fcol, fcol)[None], (batch, fcol))
    cols.append(_mm("ada_final", c_b, a["final_ada_w"].astype(BF16), NN, F32, acc=bias))
    mod_cols = jnp.concatenate(cols, axis=1).reshape(2, batch // 2, depth * ncol + fcol)

    names = DENSE + ("conv_w", "mod_cols")
    got = _gather_two_level("gather_weights", [a[n].astype(BF16) for n in DENSE] + [a["conv_w"], mod_cols])
    gw = dict(zip(names, got))
    w = _prepare(gw, a, d, chans, nh)
    mod_all = gw["mod_cols"].reshape(NUM_CHIPS, batch, -1)
    mine = lambda full: lax.dynamic_slice_in_dim(full, dev * bl, bl, axis=0)
    mods = [mine(mod_all[:, :, l * ncol:(l + 1) * ncol].transpose(1, 0, 2).reshape(batch, NUM_CHIPS * ncol))
            for l in range(depth)]
    fm = mine(mod_all[:, :, depth * ncol:].transpose(1, 0, 2).reshape(batch, NUM_CHIPS * fcol))

    loss, grad_x, grads, dmods, dfm = _local_step(w, x, tgt, mods, fm)
    loss = lax.psum(loss, ("x", "y", "c"))

    rs_in = []
    for l in range(depth):
        for n in DENSE:
            if n == "w_in":
                ga, gb, gf = grads["layers"][l]["w_in"]
                orig = jnp.concatenate([ga[:, :2 * chans], gb, gf[:, :nh], ga[:, 2 * chans:]], axis=1)
                rs_in.append(orig.reshape(d, NUM_CHIPS, -1).transpose(1, 0, 2))
            else:
                rs_in.append(grads["layers"][l][n])
    halves = _split_to_sibling("split_grads", rs_in)
    nrs = len(rs_in)
    chip_part = [_add_pair(f"add_cores_{k}", halves[k], halves[nrs + k]) for k in range(nrs)]
    slots = _scatter_chips("scatter_grads", chip_part)
    reduced = [_sum_slots(f"sum_chips_{k}", s, F32) for k, s in enumerate(slots)]
    full = _join_halves("join_grads", reduced)

    out = {}

    def put(n, res, per_layer):
        for kind, val in zip(("grad_", "delta_", "new_m_", "new_v_"), res):
            out[kind + n] = jnp.stack(val).reshape(a[n].shape) if per_layer else val.reshape(a[n].shape)

    def flat3(v):
        return v.reshape(depth, -1, v.shape[-1])

    k = 0
    dense = {n: [] for n in DENSE}
    for l in range(depth):
        for n in DENSE:
            res = _adamw(f"adamw_{n}_{l}", full[k], flat3(a[n]), flat3(a["m_" + n]), flat3(a["v_" + n]), l)
            dense[n].append([full[k]] + list(res))
            k += 1
    for n in DENSE:
        put(n, [[dense[n][l][j] for l in range(depth)] for j in range(4)], True)

    items = []
    for l in range(depth):
        g = grads["layers"][l]
        items += [g["ffn1_norm"], g["mix_norm"], g["ffn2_norm"], g["rg_small"], _unblockdiag(g["rg_dwa"], bd),
                  _unblockdiag(g["rg_dwx"], bd), g["fox_bf"], g["merge_b"], dmods[l]]
    items += [grads["final_norm"], dfm]
    pack, layout = _pack(items)
    gath = _gather_all("gather_small", pack)
    tot = _sum_slots("sum_small", gath, F32)

    def wpack(pre):
        its = []
        for l in range(depth):
            rg_rows = jnp.concatenate([jnp.zeros((4, chans), F32), a[pre + "conv_b"][l][None], a[pre + "rg_ba"][l][None],
                                       a[pre + "rg_bx"][l][None], a[pre + "rg_lam"][l][None]], axis=0)
            its += [a[pre + "ffn1_norm"][l], a[pre + "mix_norm"][l], a[pre + "ffn2_norm"][l], rg_rows,
                    a[pre + "rg_wa"][l], a[pre + "rg_wx"][l], a[pre + "fox_bf"][l], a[pre + "merge_b"][l],
                    jnp.zeros((bl, 9 * d), F32)]
        its += [a[pre + "final_norm"], jnp.zeros((bl, 2 * d), F32)]
        return _pack(its)[0]

    res_small = [_unpack(r, layout) for r in [tot] + list(_adamw("adamw_small", tot, wpack(""), wpack("m_"), wpack("v_")))]
    per = len(SMALL) + 1
    for j, n in enumerate(SMALL):
        if n == "rg_small":
            for row, nm in ((4, "conv_b"), (5, "rg_ba"), (6, "rg_bx"), (7, "rg_lam")):
                put(nm, [[r[l * per + j][row] for l in range(depth)] for r in res_small], True)
        else:
            put(n, [[r[l * per + j] for l in range(depth)] for r in res_small], True)
    put("final_norm", [r[depth * per] for r in res_small], False)

    gflat = gath.reshape(NUM_DEVICES, -1)

    def rows_of(idx):
        o, n, shape = layout[idx]
        return gflat[:, o:o + n].reshape(NUM_DEVICES * shape[0], shape[1])

    late_g, ada = [], []
    for l in range(depth):
        dmod_all = rows_of(l * per + per - 1)
        late_g.append(_colsum(f"ada_b_grad_{l}", dmod_all))
        cut = lax.dynamic_slice_in_dim(dmod_all, me * ncol, ncol, axis=1).astype(BF16)
        gl = _mm(f"ada_w_grad_{l}", c_b, cut, TN, F32)
        ada.append([gl] + list(_adamw(f"adamw_ada_w_{l}", gl, a["ada_w"], a["m_ada_w"], a["v_ada_w"], l)))
    put("ada_w", [[ada[l][j] for l in range(depth)] for j in range(4)], True)
    dfm_all = rows_of(depth * per + 1)
    late_g.append(_colsum("final_ada_b_grad", dfm_all))
    cut = lax.dynamic_slice_in_dim(dfm_all, me * fcol, fcol, axis=1).astype(BF16)
    gl = _mm("final_ada_w_grad", c_b, cut, TN, F32)
    put("final_ada_w", [gl] + list(_adamw("adamw_final_ada_w", gl, a["final_ada_w"], a["m_final_ada_w"],
                                          a["v_final_ada_w"])), False)
    cshard = a["conv_w"].shape[2]
    for l in range(depth):
        rg_tot = res_small[0][l * per + SMALL.index("rg_small")]
        late_g.append(lax.dynamic_slice_in_dim(rg_tot[:4], me * cshard, cshard, axis=1))
    gp2, layout2 = _pack(late_g)

    def wpack2(pre):
        return _pack([a[pre + "ada_b"][l][None] for l in range(depth)] + [a[pre + "final_ada_b"][None]]
                     + [a[pre + "conv_w"][l] for l in range(depth)])[0]

    res_late = [_unpack(r, layout2) for r in [gp2] + list(_adamw("adamw_late", gp2, wpack2(""), wpack2("m_"), wpack2("v_")))]
    put("ada_b", [[r[l] for l in range(depth)] for r in res_late], True)
    put("final_ada_b", [r[depth] for r in res_late], False)
    put("conv_w", [[r[depth + 1 + l] for l in range(depth)] for r in res_late], True)

    outs = [loss, grad_x]
    for kind in ("grad_", "delta_", "new_m_", "new_v_"):
        outs += [out[kind + n] for n in WEIGHTS]
    return tuple(outs)


def kernel(x, c, ffn1_norm, ffn1_w1, ffn1_w3, ffn1_w2, mix_norm, w_in, conv_w, conv_b, rg_wa, rg_ba, rg_wx, rg_bx, rg_lam, fox_bf, merge_b, w_rg, w_sb, w_fox, w_o, ffn2_norm, ffn2_w1, ffn2_w3, ffn2_w2, ada_w, ada_b, final_norm, final_ada_w, final_ada_b, loss_target, m_ffn1_norm, m_ffn1_w1, m_ffn1_w3, m_ffn1_w2, m_mix_norm, m_w_in, m_conv_w, m_conv_b, m_rg_wa, m_rg_ba, m_rg_wx, m_rg_bx, m_rg_lam, m_fox_bf, m_merge_b, m_w_rg, m_w_sb, m_w_fox, m_w_o, m_ffn2_norm, m_ffn2_w1, m_ffn2_w3, m_ffn2_w2, m_ada_w, m_ada_b, m_final_norm, m_final_ada_w, m_final_ada_b, v_ffn1_norm, v_ffn1_w1, v_ffn1_w3, v_ffn1_w2, v_mix_norm, v_w_in, v_conv_w, v_conv_b, v_rg_wa, v_rg_ba, v_rg_wx, v_rg_bx, v_rg_lam, v_fox_bf, v_merge_b, v_w_rg, v_w_sb, v_w_fox, v_w_o, v_ffn2_norm, v_ffn2_w1, v_ffn2_w3, v_ffn2_w2, v_ada_w, v_ada_b, v_final_norm, v_final_ada_w, v_final_ada_b):
    args = dict(locals())
    return _step(args)
```

```python
import math

import jax
import jax.numpy as jnp
from jax import lax
from jax.experimental import pallas as pl
from jax.experimental.pallas import tpu as pltpu

F32 = jnp.float32
BF16 = jnp.bfloat16

NUM_CHIPS = 4
NUM_DEVICES = 8
HEAD_DIM = 64
LANE = 128
SUBLANE = 8
VMEM_LIMIT = 56 * 1024 * 1024
EPS = 1e-6
RG_C = 8.0
ADAM_LR = 0.001
ADAM_B1 = 0.9
ADAM_B2 = 0.999
ADAM_EPS = 1e-08
ADAM_WD = 0.01
ADAM_STEP = 10
MESH = pl.DeviceIdType.MESH
ANY = pl.BlockSpec(memory_space=pl.ANY)


def _params():
    return pltpu.CompilerParams(vmem_limit_bytes=VMEM_LIMIT)


def _tile(dim, pref):
    if dim <= pref:
        return dim
    t = (pref // LANE) * LANE
    while t >= LANE:
        if dim % t == 0:
            return t
        t -= LANE
    return dim


def _rtile(rows, pref, unit=2 * SUBLANE):
    if rows <= pref:
        return rows
    t = (pref // unit) * unit
    while t >= unit:
        if rows % t == 0:
            return t
        t -= unit
    return rows


def _sigmoid(x):
    return 1.0 / (1.0 + jnp.exp(-x))


def _softplus(x):
    return jnp.maximum(x, 0.0) + jnp.log(1.0 + jnp.exp(-jnp.abs(x)))


def _expm1(x):
    small = x * (1.0 + x * (0.5 + x * (1.0 / 6.0 + x * (1.0 / 24.0))))
    return jnp.where(jnp.abs(x) < 0.01, small, jnp.exp(x) - 1.0)


_GELU_K = math.sqrt(2.0 / math.pi)


def _gelu_and_grad(x):
    inner = _GELU_K * (x + 0.044715 * x * x * x)
    t = jnp.tanh(inner)
    val = 0.5 * x * (1.0 + t)
    dinner = _GELU_K * (1.0 + 3.0 * 0.044715 * x * x)
    grad = 0.5 * (1.0 + t) + 0.5 * x * (1.0 - t * t) * dinner
    return val, grad


NN = ((1,), (0,))
NT = ((1,), (1,))
TN = ((0,), (0,))
ALL = slice(None)


def _mmk(name, ops, specs, terms, out_shape, out_dtype, grid, o_spec, acc=None):
    n_ops = len(ops)

    def body(*refs):
        o_ref = refs[-1]
        p = None
        for ia, xa, ib, xb, dims in terms:
            t = lax.dot_general(refs[ia][xa], refs[ib][xb], (dims, ((), ())), preferred_element_type=F32)
            p = t if p is None else p + t
        if acc is not None:
            p = p + refs[n_ops][...].astype(F32)
        o_ref[...] = p.astype(o_ref.dtype)

    in_specs = list(specs)
    args = list(ops)
    if acc is not None:
        in_specs.append(pl.BlockSpec(o_spec.block_shape, o_spec.index_map))
        args.append(acc)
    return pl.pallas_call(
        body, name=name, grid=grid, in_specs=in_specs, out_specs=o_spec,
        out_shape=jax.ShapeDtypeStruct(out_shape, out_dtype), compiler_params=_params(),
    )(*args)


def _mm(name, a, b, dims, out_dtype, acc=None, tm=512, tn=512):
    if dims == NN:
        (m, kk), n = a.shape, b.shape[1]
    elif dims == NT:
        (m, kk), n = a.shape, b.shape[0]
    else:
        (kk, m), n = a.shape, b.shape[1]
    tm, tn = _tile(m, tm), _tile(n, tn)
    if dims == TN:
        a_spec = pl.BlockSpec((kk, tm), lambda i, j: (0, i))
    else:
        a_spec = pl.BlockSpec((tm, kk), lambda i, j: (i, 0))
    if dims == NT:
        b_spec = pl.BlockSpec((tn, kk), lambda i, j: (j, 0))
    else:
        b_spec = pl.BlockSpec((kk, tn), lambda i, j: (0, j))
    return _mmk(name, [a, b], [a_spec, b_spec], [(0, ALL, 1, ALL, dims)], (m, n), out_dtype,
                (m // tm, n // tn), pl.BlockSpec((tm, tn), lambda i, j: (i, j)), acc)


def _row_grid(t_rows, seq, pref=256):
    tm = _tile(seq, pref)
    return tm, seq // tm


def _normmod(name, x, gain, shift, scale, seq):
    t_rows, d = x.shape
    bl = t_rows // seq
    tm, per = _row_grid(t_rows, seq)

    def body(x_ref, g_ref, sh_ref, sc_ref, o_ref):
        xv = x_ref[...]
        rstd = lax.rsqrt(jnp.mean(xv * xv, axis=-1, keepdims=True) + EPS)
        hn = (xv * rstd) * g_ref[...]
        o_ref[...] = (hn * (1.0 + sc_ref[...]) + sh_ref[...]).astype(o_ref.dtype)

    row = pl.BlockSpec((tm, d), lambda b, i: (b * per + i, 0))
    vec = pl.BlockSpec((None, 1, d), lambda b, i: (b, 0, 0))
    return pl.pallas_call(
        body, name=name, grid=(bl, per),
        in_specs=[row, pl.BlockSpec((1, d), lambda b, i: (0, 0)), vec, vec],
        out_specs=row, out_shape=jax.ShapeDtypeStruct((t_rows, d), BF16),
        compiler_params=_params(),
    )(x, gain, shift, scale)


def _normmod_bwd(name, x, dh, dxo, gain, scale, seq):
    t_rows, d = x.shape
    bl = t_rows // seq
    tm, per = _row_grid(t_rows, seq)

    def body(x_ref, dh_ref, dxo_ref, g_ref, sc_ref, dx_ref, dsh_ref, dsc_ref, dg_ref):
        b, i = pl.program_id(0), pl.program_id(1)
        xv = x_ref[...]
        dhv = dh_ref[...]
        rstd = lax.rsqrt(jnp.mean(xv * xv, axis=-1, keepdims=True) + EPS)
        xhat = xv * rstd
        gain_v = g_ref[...]
        dhn = dhv * (1.0 + sc_ref[...])
        dxhat = dhn * gain_v
        dx = rstd * (dxhat - xhat * jnp.mean(dxhat * xhat, axis=-1, keepdims=True))
        dx_ref[...] = dxo_ref[...] + dx

        @pl.when(i == 0)
        def _():
            dsh_ref[...] = jnp.zeros_like(dsh_ref)
            dsc_ref[...] = jnp.zeros_like(dsc_ref)

        @pl.when((i == 0) & (b == 0))
        def _():
            dg_ref[...] = jnp.zeros_like(dg_ref)

        dsh_ref[...] += jnp.sum(dhv, axis=0, keepdims=True)
        dsc_ref[...] += jnp.sum(dhv * (xhat * gain_v), axis=0, keepdims=True)
        dg_ref[...] += jnp.sum(dhn * xhat, axis=0, keepdims=True)

    row = pl.BlockSpec((tm, d), lambda b, i: (b * per + i, 0))
    vec = pl.BlockSpec((None, 1, d), lambda b, i: (b, 0, 0))
    one = pl.BlockSpec((1, d), lambda b, i: (0, 0))
    return pl.pallas_call(
        body, name=name, grid=(bl, per),
        in_specs=[row, row, row, one, vec],
        out_specs=[row, vec, vec, one],
        out_shape=[jax.ShapeDtypeStruct((t_rows, d), F32), jax.ShapeDtypeStruct((bl, 1, d), F32),
                   jax.ShapeDtypeStruct((bl, 1, d), F32), jax.ShapeDtypeStruct((1, d), F32)],
        compiler_params=_params(),
    )(x, dh, dxo, gain, scale)


def _resid(name, x, y, gate, coef, seq):
    t_rows, d = x.shape
    bl = t_rows // seq
    tm, per = _row_grid(t_rows, seq)

    def body(x_ref, y_ref, g_ref, o_ref):
        o_ref[...] = x_ref[...] + (coef * (1.0 + g_ref[...])) * y_ref[...]

    row = pl.BlockSpec((tm, d), lambda b, i: (b * per + i, 0))
    vec = pl.BlockSpec((None, 1, d), lambda b, i: (b, 0, 0))
    return pl.pallas_call(
        body, name=name, grid=(bl, per), in_specs=[row, row, vec], out_specs=row,
        out_shape=jax.ShapeDtypeStruct((t_rows, d), F32), compiler_params=_params(),
    )(x, y, gate)


def _resid_bwd(name, dxo, y, gate, coef, seq):
    t_rows, d = dxo.shape
    bl = t_rows // seq
    tm, per = _row_grid(t_rows, seq)

    def body(dxo_ref, y_ref, g_ref, dy_ref, dg_ref):
        i = pl.program_id(1)
        dxov = dxo_ref[...]
        dy_ref[...] = ((coef * (1.0 + g_ref[...])) * dxov).astype(dy_ref.dtype)

        @pl.when(i == 0)
        def _():
            dg_ref[...] = jnp.zeros_like(dg_ref)

        dg_ref[...] += jnp.sum((coef * y_ref[...]) * dxov, axis=0, keepdims=True)

    row = pl.BlockSpec((tm, d), lambda b, i: (b * per + i, 0))
    vec = pl.BlockSpec((None, 1, d), lambda b, i: (b, 0, 0))
    return pl.pallas_call(
        body, name=name, grid=(bl, per), in_specs=[row, row, vec], out_specs=[row, vec],
        out_shape=[jax.ShapeDtypeStruct((t_rows, d), BF16), jax.ShapeDtypeStruct((bl, 1, d), F32)],
        compiler_params=_params(),
    )(dxo, y, gate)


def _final_loss(name, x, tgt, gain, shift, scale, seq):
    t_rows, d = x.shape
    bl = t_rows // seq
    tm, per = _row_grid(t_rows, seq)

    def body(x_ref, t_ref, g_ref, sh_ref, sc_ref, l_ref, dx_ref, dsh_ref, dsc_ref, dg_ref):
        b, i = pl.program_id(0), pl.program_id(1)
        xv = x_ref[...]
        rstd = lax.rsqrt(jnp.mean(xv * xv, axis=-1, keepdims=True) + EPS)
        xhat = xv * rstd
        gain_v = g_ref[...]
        hn = xhat * gain_v
        yv = hn * (1.0 + sc_ref[...]) + sh_ref[...]
        err = yv - t_ref[...]
        dyv = err * (1.0 / d)
        dhn = dyv * (1.0 + sc_ref[...])
        dxhat = dhn * gain_v
        dx_ref[...] = rstd * (dxhat - xhat * jnp.mean(dxhat * xhat, axis=-1, keepdims=True))

        @pl.when(i == 0)
        def _():
            l_ref[...] = jnp.zeros_like(l_ref)
            dsh_ref[...] = jnp.zeros_like(dsh_ref)
            dsc_ref[...] = jnp.zeros_like(dsc_ref)

        @pl.when((i == 0) & (b == 0))
        def _():
            dg_ref[...] = jnp.zeros_like(dg_ref)

        part = jnp.sum(jnp.sum(err * err, axis=-1, keepdims=True), axis=0, keepdims=True) * (0.5 / d)
        l_ref[...] += jnp.broadcast_to(part, l_ref.shape)
        dsh_ref[...] += jnp.sum(dyv, axis=0, keepdims=True)
        dsc_ref[...] += jnp.sum(dyv * hn, axis=0, keepdims=True)
        dg_ref[...] += jnp.sum(dhn * xhat, axis=0, keepdims=True)

    row = pl.BlockSpec((tm, d), lambda b, i: (b * per + i, 0))
    vec = pl.BlockSpec((None, 1, d), lambda b, i: (b, 0, 0))
    one = pl.BlockSpec((1, d), lambda b, i: (0, 0))
    lvec = pl.BlockSpec((None, 1, LANE), lambda b, i: (b, 0, 0))
    return pl.pallas_call(
        body, name=name, grid=(bl, per),
        in_specs=[row, row, one, vec, vec],
        out_specs=[lvec, row, vec, vec, one],
        out_shape=[jax.ShapeDtypeStruct((bl, 1, LANE), F32), jax.ShapeDtypeStruct((t_rows, d), F32),
                   jax.ShapeDtypeStruct((bl, 1, d), F32), jax.ShapeDtypeStruct((bl, 1, d), F32),
                   jax.ShapeDtypeStruct((1, d), F32)],
        compiler_params=_params(),
    )(x, tgt, gain, shift, scale)


def _ffn_up(name, h, w1, w3, l):
    t_rows, d = h.shape
    ng, fs = w1.shape[0], w1.shape[3]
    tm = _tile(t_rows, 512)

    def body(h_ref, w1_ref, w3_ref, a_ref, b_ref, g_ref):
        hv = h_ref[...]
        av = jnp.dot(hv, w1_ref[...], preferred_element_type=F32)
        bv = jnp.dot(hv, w3_ref[...], preferred_element_type=F32)
        a_ref[...] = av
        b_ref[...] = bv
        g_ref[...] = (av * _sigmoid(av) * bv).astype(g_ref.dtype)

    wspec = pl.BlockSpec((None, None, d, fs), lambda g, i: (g, l, 0, 0))
    out = pl.BlockSpec((None, tm, fs), lambda g, i: (g, i, 0))
    f = jax.ShapeDtypeStruct((ng, t_rows, fs), F32)
    return pl.pallas_call(
        body, name=name, grid=(ng, t_rows // tm),
        in_specs=[pl.BlockSpec((tm, d), lambda g, i: (i, 0)), wspec, wspec], out_specs=[out, out, out],
        out_shape=[f, f, jax.ShapeDtypeStruct((ng, t_rows, fs), BF16)], compiler_params=_params(),
    )(h, w1, w3)


def _ffn_down_dx(name, dy, w2, a, b, l):
    t_rows, d = dy.shape
    ng, fs = w2.shape[0], w2.shape[2]
    tm = _tile(t_rows, 512)

    def body(dy_ref, w2_ref, a_ref, b_ref, da_ref, db_ref):
        dgv = lax.dot_general(dy_ref[...], w2_ref[...], (NT, ((), ())), preferred_element_type=F32)
        av = a_ref[...]
        sig = _sigmoid(av)
        da_ref[...] = (dgv * b_ref[...] * (sig * (1.0 + av * (1.0 - sig)))).astype(da_ref.dtype)
        db_ref[...] = (dgv * (av * sig)).astype(db_ref.dtype)

    blk = pl.BlockSpec((None, tm, fs), lambda g, i: (g, i, 0))
    o = jax.ShapeDtypeStruct((ng, t_rows, fs), BF16)
    return pl.pallas_call(
        body, name=name, grid=(ng, t_rows // tm),
        in_specs=[pl.BlockSpec((tm, d), lambda g, i: (i, 0)),
                  pl.BlockSpec((None, None, fs, d), lambda g, i: (g, l, 0, 0)), blk, blk],
        out_specs=[blk, blk], out_shape=[o, o], compiler_params=_params(),
    )(dy, w2, a, b)


def _ffn_up_dw(name, h, da, db):
    t_rows, d = h.shape
    ng, fs = da.shape[0], da.shape[2]
    tn = _tile(d, 512)

    def body(h_ref, da_ref, db_ref, o1_ref, o3_ref):
        hv = h_ref[...]
        o1_ref[...] = lax.dot_general(hv, da_ref[...], (TN, ((), ())), preferred_element_type=F32).astype(o1_ref.dtype)
        o3_ref[...] = lax.dot_general(hv, db_ref[...], (TN, ((), ())), preferred_element_type=F32).astype(o3_ref.dtype)

    dspec = pl.BlockSpec((None, t_rows, fs), lambda g, i: (g, 0, 0))
    out = pl.BlockSpec((None, tn, fs), lambda g, i: (g, i, 0))
    o = jax.ShapeDtypeStruct((ng, d, fs), BF16)
    return pl.pallas_call(
        body, name=name, grid=(ng, d // tn),
        in_specs=[pl.BlockSpec((t_rows, tn), lambda g, i: (0, i)), dspec, dspec],
        out_specs=[out, out], out_shape=[o, o], compiler_params=_params(),
    )(h, da, db)


def _ffn_fwd(tag, w, l, pre, x, mod, seq):
    t_rows, d = x.shape
    w1, w3, w2 = w[pre + "w1"], w[pre + "w3"], w[pre + "w2"]
    ng, fs = w1.shape[0], w1.shape[3]
    shift, scale, gate = mod
    h = _normmod(tag + "_norm", x, w[pre + "norm"][l][None], shift, scale, seq)
    a, b, gact = _ffn_up(tag + "_up", h, w1, w3, l)
    tm, tn = _tile(t_rows, 512), _tile(d, 512)
    y = _mmk(tag + "_down", [gact, w2],
             [pl.BlockSpec((ng, tm, fs), lambda i, j: (0, i, 0)),
              pl.BlockSpec((ng, None, fs, tn), lambda i, j: (0, l, 0, j))],
             [(0, g, 1, g, NN) for g in range(ng)], (t_rows, d), F32, (t_rows // tm, d // tn),
             pl.BlockSpec((tm, tn), lambda i, j: (i, j)))
    xn = _resid(tag + "_res", x, y, gate, 0.5, seq)
    return xn, (x, h, a, b, gact, y)


def _ffn_bwd(tag, w, l, pre, saved, mod, dxo, seq):
    x, h, a, b, gact, y = saved
    t_rows, d = x.shape
    w1, w3, w2 = w[pre + "w1"], w[pre + "w3"], w[pre + "w2"]
    ng, fs = w1.shape[0], w1.shape[3]
    shift, scale, gate = mod
    tm, tn = _tile(t_rows, 512), _tile(d, 512)
    dy, dgate = _resid_bwd(tag + "_res_bwd", dxo, y, gate, 0.5, seq)
    da, db = _ffn_down_dx(tag + "_down_dx", dy, w2, a, b, l)
    dw2 = _mmk(tag + "_down_dw", [gact, dy],
               [pl.BlockSpec((None, t_rows, fs), lambda g, j: (g, 0, 0)),
                pl.BlockSpec((t_rows, tn), lambda g, j: (0, j))],
               [(0, ALL, 1, ALL, TN)], (ng, fs, d), BF16, (ng, d // tn),
               pl.BlockSpec((None, fs, tn), lambda g, j: (g, 0, j)))
    dw1, dw3 = _ffn_up_dw(tag + "_up_dw", h, da, db)
    dspec = pl.BlockSpec((ng, tm, fs), lambda i, j: (0, i, 0))
    wspec = pl.BlockSpec((ng, None, tn, fs), lambda i, j: (0, l, j, 0))
    dh = _mmk(tag + "_up_dx", [da, db, w1, w3], [dspec, dspec, wspec, wspec],
              [(0, g, 2, g, NT) for g in range(ng)] + [(1, g, 3, g, NT) for g in range(ng)],
              (t_rows, d), F32, (t_rows // tm, d // tn), pl.BlockSpec((tm, tn), lambda i, j: (i, j)))
    dx, dshift, dscale, dgain = _normmod_bwd(tag + "_norm_bwd", x, dh, dxo, w[pre + "norm"][l][None], scale, seq)
    grads = {pre + "w1": dw1, pre + "w3": dw3, pre + "w2": dw2, pre + "norm": dgain}
    return dx, (dshift, dscale, dgate), grads


def _shift_down(v, s, row):
    if s == 0:
        return v
    return jnp.where(row >= s, pltpu.roll(v, s, 0), 0.0)


def _shift_up(v, s, row):
    if s == 0:
        return v
    n = v.shape[0]
    return jnp.where(row < n - s, pltpu.roll(v, n - s, 0), 0.0)


def _scan_fwd(a, u, row):
    n = a.shape[0]
    s = 1
    while s < n:
        ok = row >= s
        a_sh = pltpu.roll(a, s, 0)
        u_sh = pltpu.roll(u, s, 0)
        u = jnp.where(ok, a * u_sh + u, u)
        a = jnp.where(ok, a * a_sh, a)
        s *= 2
    return u


def _scan_bwd(a_next, g, row):
    n = g.shape[0]
    a, u = a_next, g
    s = 1
    while s < n:
        ok = row < n - s
        a_sh = pltpu.roll(a, n - s, 0)
        u_sh = pltpu.roll(u, n - s, 0)
        u = jnp.where(ok, a * u_sh + u, u)
        a = jnp.where(ok, a * a_sh, a)
        s *= 2
    return u


def _rg_specs(seq, cw):
    slab = lambda off: pl.BlockSpec((seq, cw), lambda c, b: (b, off + c))
    par = lambda rows: pl.BlockSpec((rows, cw), lambda c, b: (0, c))
    wbd = pl.BlockSpec((None, cw, cw), lambda c, b: (c, 0, 0))
    return slab, par, wbd


def _rg_fwd(name, proj, p, seq, chans):
    t_rows = proj.shape[0]
    bl = t_rows // seq
    cw = LANE
    nc = chans // cw
    slab, par, wbd = _rg_specs(seq, cw)

    def body(x_ref, gt_ref, cw_ref, cb_ref, wa_ref, ba_ref, wx_ref, bx_ref, lam_ref,
             xa_ref, r_ref, i_ref, h_ref, ya_ref):
        row = lax.broadcasted_iota(jnp.int32, (seq, cw), 0)
        xv = x_ref[...]
        xa = jnp.zeros_like(xv) + cb_ref[...]
        for k in range(4):
            xa = xa + cw_ref[k:k + 1, :] * _shift_down(xv, 3 - k, row)
        xab = xa.astype(BF16)
        r = _sigmoid(jnp.dot(xab, wa_ref[...], preferred_element_type=F32) + ba_ref[...])
        ig = _sigmoid(jnp.dot(xab, wx_ref[...], preferred_element_type=F32) + bx_ref[...])
        log_a = (-RG_C) * r * _softplus(-lam_ref[...])
        a = jnp.exp(log_a)
        u = jnp.sqrt(-_expm1(2.0 * log_a)) * (ig * xa)
        h = _scan_fwd(a, u, row)
        gel, _ = _gelu_and_grad(gt_ref[...])
        xa_ref[...] = xa
        r_ref[...] = r
        i_ref[...] = ig
        h_ref[...] = h
        ya_ref[...] = (gel * h).astype(ya_ref.dtype)

    out = pl.BlockSpec((seq, cw), lambda c, b: (b, c))
    f = jax.ShapeDtypeStruct((t_rows, chans), F32)
    return pl.pallas_call(
        body, name=name, grid=(nc, bl),
        in_specs=[slab(0), slab(nc), par(4), par(1), wbd, par(1), wbd, par(1), par(1)],
        out_specs=[out] * 5,
        out_shape=[f, f, f, f, jax.ShapeDtypeStruct((t_rows, chans), BF16)],
        compiler_params=_params(),
    )(proj, proj, p["conv_w"], p["conv_b"], p["wa"], p["ba"], p["wx"], p["bx"], p["lam"])


def _rg_bwd(name, proj, dya, saved, p, seq, chans):
    xa_s, r_s, i_s, h_s = saved
    t_rows = proj.shape[0]
    bl = t_rows // seq
    cw = LANE
    nc = chans // cw
    slab, par, wbd = _rg_specs(seq, cw)

    def body(x_ref, gt_ref, dya_ref, xa_ref, r_ref, i_ref, h_ref, cw_ref, wa_ref, wx_ref, lam_ref,
             dx_ref, dgt_ref, sm_ref, dwa_ref, dwx_ref):
        b = pl.program_id(1)
        row = lax.broadcasted_iota(jnp.int32, (seq, cw), 0)
        xv, xa, r, ig, h = x_ref[...], xa_ref[...], r_ref[...], i_ref[...], h_ref[...]
        dyav = dya_ref[...]
        gel, dgel = _gelu_and_grad(gt_ref[...])
        dgt_ref[...] = (dyav * h * dgel).astype(dgt_ref.dtype)
        dh = dyav * gel
        lam = lam_ref[...]
        sp = _softplus(-lam)
        log_a = (-RG_C) * r * sp
        a = jnp.exp(log_a)
        s = jnp.sqrt(-_expm1(2.0 * log_a))
        lamb = _scan_bwd(_shift_up(a, 1, row), dh, row)
        da = lamb * _shift_down(h, 1, row)
        xi = ig * xa
        ds = lamb * xi
        dxi = lamb * s
        dlog = da * a - ds * (a * a) / s
        dr = dlog * ((-RG_C) * sp)
        dsp = jnp.sum(dlog * ((-RG_C) * r), axis=0, keepdims=True)
        dlam = -dsp * _sigmoid(-lam)
        dzr = dr * r * (1.0 - r)
        dzi = (dxi * xa) * ig * (1.0 - ig)
        dzrb, dzib, xab = dzr.astype(BF16), dzi.astype(BF16), xa.astype(BF16)
        dxa = dxi * ig
        dxa = dxa + lax.dot_general(dzrb, wa_ref[...], (NT, ((), ())), preferred_element_type=F32)
        dxa = dxa + lax.dot_general(dzib, wx_ref[...], (NT, ((), ())), preferred_element_type=F32)
        dwa = lax.dot_general(xab, dzrb, (TN, ((), ())), preferred_element_type=F32)
        dwx = lax.dot_general(xab, dzib, (TN, ((), ())), preferred_element_type=F32)
        dxv = jnp.zeros_like(xv)
        rows = []
        for k in range(4):
            dxv = dxv + cw_ref[k:k + 1, :] * _shift_up(dxa, 3 - k, row)
            rows.append(jnp.sum(dxa * _shift_down(xv, 3 - k, row), axis=0, keepdims=True))
        dx_ref[...] = dxv.astype(dx_ref.dtype)
        rows += [jnp.sum(dxa, axis=0, keepdims=True), jnp.sum(dzr, axis=0, keepdims=True),
                 jnp.sum(dzi, axis=0, keepdims=True), dlam]

        @pl.when(b == 0)
        def _():
            sm_ref[...] = jnp.zeros_like(sm_ref)
            dwa_ref[...] = jnp.zeros_like(dwa_ref)
            dwx_ref[...] = jnp.zeros_like(dwx_ref)

        for k, val in enumerate(rows):
            sm_ref[k:k + 1, :] += val
        dwa_ref[...] += dwa
        dwx_ref[...] += dwx

    plain = pl.BlockSpec((seq, cw), lambda c, b: (b, c))
    return pl.pallas_call(
        body, name=name, grid=(nc, bl),
        in_specs=[slab(0), slab(nc), plain, plain, plain, plain, plain, par(4), wbd, wbd, par(1)],
        out_specs=[plain, plain, par(8), wbd, wbd],
        out_shape=[jax.ShapeDtypeStruct((t_rows, chans), BF16), jax.ShapeDtypeStruct((t_rows, chans), BF16),
                   jax.ShapeDtypeStruct((8, chans), F32),
                   jax.ShapeDtypeStruct((nc, cw, cw), F32), jax.ShapeDtypeStruct((nc, cw, cw), F32)],
        compiler_params=_params(),
    )(proj, proj, dya, xa_s, r_s, i_s, h_s, p["conv_w"], p["wa"], p["wx"], p["lam"])


ATT_Q_BLOCK = 128
ATT_K_BLOCK = 256
PAIR = LANE // HEAD_DIM
NEG = -1e30
SCALE = HEAD_DIM ** -0.5
assert math.log2(HEAD_DIM) % 2 == 0


def _att_blocks(seq):
    return _tile(seq, ATT_Q_BLOCK), _tile(seq, ATT_K_BLOCK)


def _key_blocks(qi, tq, bk):
    return (qi * tq) // bk, (qi * tq + tq - 1) // bk + 1


def _tri(n, kind):
    r = lax.broadcasted_iota(jnp.int32, (n, n), 0)
    c = lax.broadcasted_iota(jnp.int32, (n, n), 1)
    m = {"gt": r > c, "le": r <= c, "lt": r < c}[kind]
    return m.astype(BF16)


def _cumsum_mm(v, tri):
    hi = v.astype(BF16)
    lo = (v - hi.astype(F32)).astype(BF16)
    return jnp.dot(hi, tri, preferred_element_type=F32) + jnp.dot(lo, tri, preferred_element_type=F32)


def _head_masks():
    lane = lax.broadcasted_iota(jnp.int32, (1, LANE), 1)
    return [(lane >= h * HEAD_DIM) & (lane < (h + 1) * HEAD_DIM) for h in range(PAIR)]


def _only(mask, v):
    return jnp.where(mask, v, jnp.zeros_like(v))


def _att_specs(seq, blk, nq, off):
    npair = None
    qs = lambda o: pl.BlockSpec((blk, LANE), lambda b, p, i: (b * nq + i, o + p))
    ks = lambda o: pl.BlockSpec((seq, LANE), lambda b, p, i: (b, o + p))
    col = pl.BlockSpec((None, PAIR, blk, 1), lambda b, p, i: (b, p, i, 0))
    lane = pl.BlockSpec((None, PAIR, 1, seq), lambda b, p, i: (b, p, 0, 0))
    return qs, ks, col, lane


def _sb_fwd(name, qkv, off, width, bl, seq):
    t_rows = qkv.shape[0]
    tq, bk = _att_blocks(seq)
    nq = seq // tq
    nb = width // LANE
    qs, ks, col, _ = _att_specs(seq, tq, nq, off)

    def body(q_ref, k_ref, v_ref, o_ref, lt_ref):
        qi = pl.program_id(2)
        masks = _head_masks()
        qv = q_ref[...] * SCALE
        qh = [_only(m, qv) for m in masks]
        row = lax.broadcasted_iota(jnp.int32, (tq, bk), 0)
        cix = lax.broadcasted_iota(jnp.int32, (tq, bk), 1)
        tri = _tri(bk, "gt")

        def step(masked, top):
            def go(it, carry):
                acc, cls = carry
                kb = top - it
                ks_ = pl.multiple_of(kb * bk, bk)
                kv = k_ref[pl.ds(ks_, bk), :]
                vv = v_ref[pl.ds(ks_, bk), :]
                strict = (kb * bk + cix) < (qi * tq + row)
                new_cls = []
                for h in range(PAIR):
                    z = lax.dot_general(qh[h], kv, (NT, ((), ())), preferred_element_type=F32)
                    sp = _softplus(z)
                    lk = jnp.where(strict, -sp, 0.0) if masked else -sp
                    wgt = jnp.exp(z - sp + (cls[h] + _cumsum_mm(lk, tri)))
                    if masked:
                        wgt = jnp.where(strict, wgt, 0.0)
                    acc = acc + jnp.dot(wgt.astype(BF16), _only(masks[h], vv), preferred_element_type=F32)
                    new_cls.append(cls[h] + jnp.sum(lk, axis=1, keepdims=True))
                return acc, tuple(new_cls)
            return go

        n_full, n_all = _key_blocks(qi, tq, bk)
        zero = jnp.zeros((tq, 1), F32)
        carry = (jnp.zeros((tq, LANE), F32), (zero,) * PAIR)
        carry = lax.fori_loop(0, n_all - n_full, step(True, n_all - 1), carry)
        acc, cls = lax.fori_loop(0, n_full, step(False, n_full - 1), carry)
        o_ref[...] = acc.astype(o_ref.dtype)
        for h in range(PAIR):
            lt_ref[h] = cls[h]

    return pl.pallas_call(
        body, name=name, grid=(bl, nb, nq), in_specs=[qs(off), ks(off + nb), ks(off + 2 * nb)],
        out_specs=[qs(0), col],
        out_shape=[jax.ShapeDtypeStruct((t_rows, width), BF16),
                   jax.ShapeDtypeStruct((bl, nb * PAIR, seq, 1), F32)],
        compiler_params=_params(),
    )(qkv, qkv, qkv)


def _sb_bwd(name, qkv, off, width, bl, seq, ltot, do):
    t_rows = qkv.shape[0]
    tq, bk = _att_blocks(seq)
    nq = seq // tq
    nb = width // LANE
    qs, ks, col, _ = _att_specs(seq, tq, nq, off)

    def body(q_ref, k_ref, v_ref, lt_ref, do_ref, dq_ref, dk_ref, dv_ref, dk_acc, dv_acc):
        qi = pl.program_id(2)

        @pl.when(qi == 0)
        def _():
            dk_acc[...] = jnp.zeros_like(dk_acc)
            dv_acc[...] = jnp.zeros_like(dv_acc)

        masks = _head_masks()
        qv = q_ref[...] * SCALE
        dov = do_ref[...].astype(BF16)
        qh = [_only(m, qv) for m in masks]
        doh = [_only(m, dov) for m in masks]
        ltv = [lt_ref[h] for h in range(PAIR)]
        row = lax.broadcasted_iota(jnp.int32, (tq, bk), 0)
        cix = lax.broadcasted_iota(jnp.int32, (tq, bk), 1)
        tri_le = _tri(bk, "le")
        tri_lt = _tri(bk, "lt")

        def step(masked):
            def go(kb, carry):
                dq, cls, ces = carry
                ks_ = pl.multiple_of(kb * bk, bk)
                kv = k_ref[pl.ds(ks_, bk), :]
                vv = v_ref[pl.ds(ks_, bk), :]
                kvs = kv * SCALE
                strict = (kb * bk + cix) < (qi * tq + row)
                dk_new = jnp.zeros((bk, LANE), F32)
                dv_new = jnp.zeros((bk, LANE), F32)
                new_cls, new_ces = [], []
                for h in range(PAIR):
                    z = lax.dot_general(qh[h], kv, (NT, ((), ())), preferred_element_type=F32)
                    sp = _softplus(z)
                    lk = jnp.where(strict, -sp, 0.0) if masked else -sp
                    sig = jnp.exp(z - sp)
                    wgt = sig * jnp.exp(ltv[h] - cls[h] - _cumsum_mm(lk, tri_le))
                    if masked:
                        wgt = jnp.where(strict, wgt, 0.0)
                    dw = lax.dot_general(doh[h], vv, (NT, ((), ())), preferred_element_type=F32)
                    e = dw * wgt
                    pre = ces[h] + _cumsum_mm(e, tri_lt)
                    dz = e * (1.0 - sig) - pre * sig
                    if masked:
                        dz = jnp.where(strict, dz, 0.0)
                    dzb = dz.astype(BF16)
                    dq = dq + jnp.dot(dzb, _only(masks[h], kvs), preferred_element_type=F32)
                    dk_new = dk_new + lax.dot_general(dzb, qh[h], (TN, ((), ())), preferred_element_type=F32)
                    dv_new = dv_new + lax.dot_general(wgt.astype(BF16), doh[h], (TN, ((), ())),
                                                      preferred_element_type=F32)
                    new_cls.append(cls[h] + jnp.sum(lk, axis=1, keepdims=True))
                    new_ces.append(ces[h] + jnp.sum(e, axis=1, keepdims=True))
                dk_acc[pl.ds(ks_, bk), :] += dk_new
                dv_acc[pl.ds(ks_, bk), :] += dv_new
                return dq, tuple(new_cls), tuple(new_ces)
            return go

        n_full, n_all = _key_blocks(qi, tq, bk)
        zero = (jnp.zeros((tq, 1), F32),) * PAIR
        carry = lax.fori_loop(0, n_full, step(False), (jnp.zeros((tq, LANE), F32), zero, zero))
        dq, _, _ = lax.fori_loop(n_full, n_all, step(True), carry)
        dq_ref[...] = dq.astype(dq_ref.dtype)

        @pl.when(qi == nq - 1)
        def _():
            dk_ref[...] = dk_acc[...].astype(dk_ref.dtype)
            dv_ref[...] = dv_acc[...].astype(dv_ref.dtype)

    o = jax.ShapeDtypeStruct((t_rows, width), BF16)
    return pl.pallas_call(
        body, name=name, grid=(bl, nb, nq),
        in_specs=[qs(off), ks(off + nb), ks(off + 2 * nb), col, qs(0)], out_specs=[qs(0), ks(0), ks(0)],
        out_shape=[o, o, o], scratch_shapes=[pltpu.VMEM((seq, LANE), F32), pltpu.VMEM((seq, LANE), F32)],
        compiler_params=_params(),
    )(qkv, qkv, qkv, ltot, do)


def _fox_fwd(name, qkv, off, width, bl, seq, cum_q, cum_k):
    t_rows = qkv.shape[0]
    tq, bk = _att_blocks(seq)
    nq = seq // tq
    nb = width // LANE
    qs, ks, col, lane = _att_specs(seq, tq, nq, off)

    def body(q_ref, k_ref, v_ref, cq_ref, ck_ref, ob_ref, of_ref, lse_ref):
        qi = pl.program_id(2)
        masks = _head_masks()
        qv = q_ref[...] * SCALE
        qh = [_only(m, qv) for m in masks]
        cq = [cq_ref[h] for h in range(PAIR)]
        row = lax.broadcasted_iota(jnp.int32, (tq, bk), 0)
        cix = lax.broadcasted_iota(jnp.int32, (tq, bk), 1)

        def step(masked):
            def go(kb, carry):
                ms, ls, accs = carry
                ks_ = pl.multiple_of(kb * bk, bk)
                kv = k_ref[pl.ds(ks_, bk), :]
                vv = v_ref[pl.ds(ks_, bk), :]
                nm, nl, na = [], [], []
                for h in range(PAIR):
                    z = lax.dot_general(qh[h], kv, (NT, ((), ())), preferred_element_type=F32)
                    z = z + cq[h] - ck_ref[h, :, pl.ds(ks_, bk)]
                    if masked:
                        z = jnp.where((kb * bk + cix) <= (qi * tq + row), z, NEG)
                    m_new = jnp.maximum(ms[h], jnp.max(z, axis=1, keepdims=True))
                    pv = jnp.exp(z - m_new)
                    alpha = jnp.exp(ms[h] - m_new)
                    nm.append(m_new)
                    nl.append(alpha * ls[h] + jnp.sum(pv, axis=1, keepdims=True))
                    na.append(alpha * accs[h] + jnp.dot(pv.astype(BF16), _only(masks[h], vv),
                                                        preferred_element_type=F32))
                return tuple(nm), tuple(nl), tuple(na)
            return go

        n_full, n_all = _key_blocks(qi, tq, bk)
        init = ((jnp.full((tq, 1), NEG, F32),) * PAIR, (jnp.zeros((tq, 1), F32),) * PAIR,
                (jnp.zeros((tq, LANE), F32),) * PAIR)
        carry = lax.fori_loop(0, n_full, step(False), init)
        ms, ls, accs = lax.fori_loop(n_full, n_all, step(True), carry)
        out = accs[0] / ls[0]
        for h in range(1, PAIR):
            out = out + accs[h] / ls[h]
        ob_ref[...] = out.astype(ob_ref.dtype)
        of_ref[...] = out
        for h in range(PAIR):
            lse_ref[h] = ms[h] + jnp.log(ls[h])

    return pl.pallas_call(
        body, name=name, grid=(bl, nb, nq),
        in_specs=[qs(off), ks(off + nb), ks(off + 2 * nb), col, lane], out_specs=[qs(0), qs(0), col],
        out_shape=[jax.ShapeDtypeStruct((t_rows, width), BF16), jax.ShapeDtypeStruct((t_rows, width), F32),
                   jax.ShapeDtypeStruct((bl, nb * PAIR, seq, 1), F32)],
        compiler_params=_params(),
    )(qkv, qkv, qkv, cum_q, cum_k)


def _fox_bwd(name, qkv, off, width, bl, seq, cum_q, cum_k, lse, o, do):
    t_rows = qkv.shape[0]
    tq, bk = _att_blocks(seq)
    nq = seq // tq
    nb = width // LANE
    qs, ks, col, lane = _att_specs(seq, tq, nq, off)

    def body(q_ref, k_ref, v_ref, cq_ref, ck_ref, lse_ref, o_ref, do_ref,
             dq_ref, dk_ref, dv_ref, dcq_ref, dck_ref, dk_acc, dv_acc):
        qi = pl.program_id(2)

        @pl.when(qi == 0)
        def _():
            dk_acc[...] = jnp.zeros_like(dk_acc)
            dv_acc[...] = jnp.zeros_like(dv_acc)
            dck_ref[...] = jnp.zeros_like(dck_ref)

        masks = _head_masks()
        qv = q_ref[...] * SCALE
        dof = do_ref[...]
        dov = dof.astype(BF16)
        prod = dof * o_ref[...]
        qh = [_only(m, qv) for m in masks]
        doh = [_only(m, dov) for m in masks]
        delta = [jnp.sum(_only(m, prod), axis=1, keepdims=True) for m in masks]
        shift = [cq_ref[h] - lse_ref[h] for h in range(PAIR)]
        row = lax.broadcasted_iota(jnp.int32, (tq, bk), 0)
        cix = lax.broadcasted_iota(jnp.int32, (tq, bk), 1)

        def step(masked):
            def go(kb, carry):
                dq, dcqs = carry
                ks_ = pl.multiple_of(kb * bk, bk)
                kv = k_ref[pl.ds(ks_, bk), :]
                vv = v_ref[pl.ds(ks_, bk), :]
                kvs = kv * SCALE
                dk_new = jnp.zeros((bk, LANE), F32)
                dv_new = jnp.zeros((bk, LANE), F32)
                new_dcq = []
                for h in range(PAIR):
                    z = lax.dot_general(qh[h], kv, (NT, ((), ())), preferred_element_type=F32)
                    pv = jnp.exp(z + shift[h] - ck_ref[h, :, pl.ds(ks_, bk)])
                    if masked:
                        pv = jnp.where((kb * bk + cix) <= (qi * tq + row), pv, 0.0)
                    dp = lax.dot_general(doh[h], vv, (NT, ((), ())), preferred_element_type=F32)
                    ds = pv * (dp - delta[h])
                    dsb = ds.astype(BF16)
                    dq = dq + jnp.dot(dsb, _only(masks[h], kvs), preferred_element_type=F32)
                    dk_new = dk_new + lax.dot_general(dsb, qh[h], (TN, ((), ())), preferred_element_type=F32)
                    dv_new = dv_new + lax.dot_general(pv.astype(BF16), doh[h], (TN, ((), ())),
                                                      preferred_element_type=F32)
                    dck_ref[h, :, pl.ds(ks_, bk)] += -jnp.sum(ds, axis=0, keepdims=True)
                    new_dcq.append(dcqs[h] + jnp.sum(ds, axis=1, keepdims=True))
                dk_acc[pl.ds(ks_, bk), :] += dk_new
                dv_acc[pl.ds(ks_, bk), :] += dv_new
                return dq, tuple(new_dcq)
            return go

        n_full, n_all = _key_blocks(qi, tq, bk)
        zero = (jnp.zeros((tq, 1), F32),) * PAIR
        carry = lax.fori_loop(0, n_full, step(False), (jnp.zeros((tq, LANE), F32), zero))
        dq, dcqs = lax.fori_loop(n_full, n_all, step(True), carry)
        dq_ref[...] = dq.astype(dq_ref.dtype)
        for h in range(PAIR):
            dcq_ref[h] = dcqs[h]

        @pl.when(qi == nq - 1)
        def _():
            dk_ref[...] = dk_acc[...].astype(dk_ref.dtype)
            dv_ref[...] = dv_acc[...].astype(dv_ref.dtype)

    ob = jax.ShapeDtypeStruct((t_rows, width), BF16)
    nh = nb * PAIR
    return pl.pallas_call(
        body, name=name, grid=(bl, nb, nq),
        in_specs=[qs(off), ks(off + nb), ks(off + 2 * nb), col, lane, col, qs(0), qs(0)],
        out_specs=[qs(0), ks(0), ks(0), col, lane],
        out_shape=[ob, ob, ob, jax.ShapeDtypeStruct((bl, nh, seq, 1), F32), jax.ShapeDtypeStruct((bl, nh, 1, seq), F32)],
        scratch_shapes=[pltpu.VMEM((seq, LANE), F32), pltpu.VMEM((seq, LANE), F32)],
        compiler_params=_params(),
    )(qkv, qkv, qkv, cum_q, cum_k, lse, o, do)


def _lane_cumsum(v, reverse):
    n = v.shape[1]
    cix = lax.broadcasted_iota(jnp.int32, v.shape, 1)
    s = 1
    while s < n:
        if reverse:
            v = v + jnp.where(cix < n - s, pltpu.roll(v, n - s, 1), 0.0)
        else:
            v = v + jnp.where(cix >= s, pltpu.roll(v, s, 1), 0.0)
        s *= 2
    return v


def _forget_cum(name, fl, bf):
    def body(fl_ref, bf_ref, o_ref):
        xv = fl_ref[...] + bf_ref[...]
        o_ref[...] = _lane_cumsum(-_softplus(-xv), False)

    return pl.pallas_call(body, name=name, out_shape=jax.ShapeDtypeStruct(fl.shape, F32),
                          compiler_params=_params())(fl, bf)


def _forget_cum_bwd(name, fl, bf, dcum, nh):
    rows = fl.shape[0]

    def body(fl_ref, bf_ref, dc_ref, dfl_ref, dbf_ref):
        xv = fl_ref[...] + bf_ref[...]
        dlogf = _lane_cumsum(dc_ref[...], True)
        dfl = dlogf * _sigmoid(-xv)
        dfl_ref[...] = dfl
        per_row = jnp.sum(dfl, axis=1, keepdims=True)
        tot = per_row[0:nh]
        for b in range(1, rows // nh):
            tot = tot + per_row[b * nh:(b + 1) * nh]
        dbf_ref[...] = tot

    return pl.pallas_call(
        body, name=name,
        out_shape=[jax.ShapeDtypeStruct(fl.shape, F32), jax.ShapeDtypeStruct((nh, 1), F32)],
        compiler_params=_params(),
    )(fl, bf, dcum)


def _merge_fwd(name, proj, off, merge_b, pa, pb, pc):
    t_rows, d = pa.shape
    tm = _tile(t_rows, 256)

    def body(l0, l1, l2, mb, a_ref, b_ref, c_ref, o_ref):
        g0 = _sigmoid(l0[...] + mb[:, 0:d])
        g1 = _sigmoid(l1[...] + mb[:, d:2 * d])
        g2 = _sigmoid(l2[...] + mb[:, 2 * d:3 * d])
        o_ref[...] = (g0 * a_ref[...] + g1 * b_ref[...] + g2 * c_ref[...]).astype(o_ref.dtype)

    row = pl.BlockSpec((tm, d), lambda i: (i, 0))
    lg = lambda j: pl.BlockSpec((tm, d), lambda i: (i, off + j))
    return pl.pallas_call(
        body, name=name, grid=(t_rows // tm,),
        in_specs=[lg(0), lg(1), lg(2), pl.BlockSpec((1, 3 * d), lambda i: (0, 0)), row, row, row],
        out_specs=row, out_shape=jax.ShapeDtypeStruct((t_rows, d), BF16), compiler_params=_params(),
    )(proj, proj, proj, merge_b, pa, pb, pc)


def _merge_bwd(name, proj, off, merge_b, pa, pb, pc, dmixed):
    t_rows, d = pa.shape
    tm = _tile(t_rows, 256)

    def body(l0, l1, l2, mb, a_ref, b_ref, c_ref, dm_ref, da_ref, db_ref, dc_ref, dl_ref, dmb_ref):
        i = pl.program_id(0)
        dm = dm_ref[...]
        parts = []
        for j, (lref, pref, dref) in enumerate(((l0, a_ref, da_ref), (l1, b_ref, db_ref), (l2, c_ref, dc_ref))):
            g = _sigmoid(lref[...] + mb[:, j * d:(j + 1) * d])
            dref[...] = (g * dm).astype(dref.dtype)
            dl = dm * pref[...] * g * (1.0 - g)
            dl_ref[:, j * d:(j + 1) * d] = dl.astype(dl_ref.dtype)
            parts.append(jnp.sum(dl, axis=0, keepdims=True))
        tot = jnp.concatenate(parts, axis=1)

        @pl.when(i == 0)
        def _():
            dmb_ref[...] = tot

        @pl.when(i > 0)
        def _():
            dmb_ref[...] += tot

    row = pl.BlockSpec((tm, d), lambda i: (i, 0))
    lg = lambda j: pl.BlockSpec((tm, d), lambda i: (i, off + j))
    one = pl.BlockSpec((1, 3 * d), lambda i: (0, 0))
    b16 = jax.ShapeDtypeStruct((t_rows, d), BF16)
    return pl.pallas_call(
        body, name=name, grid=(t_rows // tm,),
        in_specs=[lg(0), lg(1), lg(2), one, row, row, row, row],
        out_specs=[row, row, row, pl.BlockSpec((tm, 3 * d), lambda i: (i, 0)), one],
        out_shape=[b16, b16, b16, jax.ShapeDtypeStruct((t_rows, 3 * d), BF16), jax.ShapeDtypeStruct((1, 3 * d), F32)],
        compiler_params=_params(),
    )(proj, proj, proj, merge_b, pa, pb, pc, dmixed)


def _grouped_nn(name, a, wg, l, out_dtype):
    t_rows, kk = a.shape
    ng, ncol = wg.shape[0], wg.shape[3]
    tm = _tile(t_rows, 512)
    return _mmk(name, [a, wg],
                [pl.BlockSpec((tm, kk), lambda i, g: (i, 0)),
                 pl.BlockSpec((None, None, kk, ncol), lambda i, g: (g, l, 0, 0))],
                [(0, ALL, 1, ALL, NN)], (t_rows, ng * ncol), out_dtype, (t_rows // tm, ng),
                pl.BlockSpec((tm, ncol), lambda i, g: (i, g)))


def _grouped_nt(name, da, wg, l, out_dtype):
    t_rows = da.shape[0]
    ng, kk, ncol = wg.shape[0], wg.shape[2], wg.shape[3]
    tm = _tile(t_rows, 512)
    return _mmk(name, [da, wg],
                [pl.BlockSpec((tm, ng * ncol), lambda i: (i, 0)),
                 pl.BlockSpec((ng, None, kk, ncol), lambda i: (0, l, 0, 0))],
                [(0, (ALL, slice(g * ncol, (g + 1) * ncol)), 1, g, NT) for g in range(ng)],
                (t_rows, kk), out_dtype, (t_rows // tm,), pl.BlockSpec((tm, kk), lambda i: (i, 0)))


def _grouped_tn(name, a, da, ng, out_dtype):
    t_rows, kk = a.shape
    ncol = da.shape[1] // ng
    return _mmk(name, [a, da],
                [pl.BlockSpec((t_rows, kk), lambda g: (0, 0)), pl.BlockSpec((t_rows, ncol), lambda g: (0, g))],
                [(0, ALL, 1, ALL, TN)], (ng, kk, ncol), out_dtype, (ng,),
                pl.BlockSpec((None, kk, ncol), lambda g: (g, 0, 0)))


def _rows_nn(name, a, wr, l, out_dtype):
    t_rows = a.shape[0]
    ng, kg, n = wr.shape[0], wr.shape[2], wr.shape[3]
    tm, tn = _tile(t_rows, 512), _tile(n, 512)
    return _mmk(name, [a, wr],
                [pl.BlockSpec((tm, ng * kg), lambda i, j: (i, 0)),
                 pl.BlockSpec((ng, None, kg, tn), lambda i, j: (0, l, 0, j))],
                [(0, (ALL, slice(g * kg, (g + 1) * kg)), 1, g, NN) for g in range(ng)],
                (t_rows, n), out_dtype, (t_rows // tm, n // tn), pl.BlockSpec((tm, tn), lambda i, j: (i, j)))


def _rows_nt(name, dy, wr, l, out_dtype):
    t_rows, n = dy.shape
    ng, kg = wr.shape[0], wr.shape[2]
    tm = _tile(t_rows, 512)
    return _mmk(name, [dy, wr],
                [pl.BlockSpec((tm, n), lambda i, g: (i, 0)),
                 pl.BlockSpec((None, None, kg, n), lambda i, g: (g, l, 0, 0))],
                [(0, ALL, 1, ALL, NT)], (t_rows, ng * kg), out_dtype, (t_rows // tm, ng),
                pl.BlockSpec((tm, kg), lambda i, g: (i, g)))


def _rows_tn(name, a, dy, ng, out_dtype):
    t_rows, n = dy.shape
    kg = a.shape[1] // ng
    tn = _tile(n, 512)
    return _mmk(name, [a, dy],
                [pl.BlockSpec((t_rows, kg), lambda g, j: (0, g)), pl.BlockSpec((t_rows, tn), lambda g, j: (0, j))],
                [(0, ALL, 1, ALL, TN)], (ng, kg, n), out_dtype, (ng, n // tn),
                pl.BlockSpec((None, kg, tn), lambda g, j: (g, 0, j)))


def _mix_fwd(tag, w, l, x, mod, seq):
    t_rows, d = x.shape
    bl = t_rows // seq
    shift, scale, gate = mod
    chans, nh = w["layout"]["chans"], w["layout"]["heads"]
    width = nh * HEAD_DIM
    nb = width // LANE
    h = _normmod(tag + "_norm", x, w["mix_norm"][l][None], shift, scale, seq)
    proj = _mm(tag + "_in_a", h, w["w_a"][l], NN, F32)
    qkv = _mm(tag + "_in_b", h, w["w_b"][l], NN, BF16)
    flp = _mm(tag + "_in_f", h, w["w_f"][l], NN, F32)
    xa, r, ig, hs, ya = _rg_fwd(tag + "_rg", proj, w["rg"][l], seq, chans)
    yb, ltot = _sb_fwd(tag + "_sb", qkv, 0, width, bl, seq)
    fl = flp[:, :nh].reshape(bl, seq, nh).transpose(0, 2, 1).reshape(bl * nh, seq)
    bf = jnp.tile(w["fox_bf"][l].reshape(nh, 1), (bl, 1))
    cum = _forget_cum(tag + "_cum", fl, bf)
    cum_q = cum.reshape(bl, nh, seq, 1)
    cum_k = cum.reshape(bl, nh, 1, seq)
    yc, oc, lse = _fox_fwd(tag + "_fox", qkv, 3 * nb, width, bl, seq, cum_q, cum_k)
    pa = _rows_nn(tag + "_prg", ya, w["w_rg"], l, F32)
    pb = _grouped_nn(tag + "_psb", yb, w["w_sb"], l, F32)
    pc = _grouped_nn(tag + "_pfox", yc, w["w_fox"], l, F32)
    moff = 2 * chans // d
    mb = w["merge_b"][l][None]
    mixed = _merge_fwd(tag + "_merge", proj, moff, mb, pa, pb, pc)
    y = _rows_nn(tag + "_out", mixed, w["w_o"], l, F32)
    xn = _resid(tag + "_res", x, y, gate, 1.0, seq)
    saved = dict(x=x, h=h, proj=proj, qkv=qkv, rg=(xa, r, ig, hs), ya=ya, ltot=ltot,
                 fox=(cum_q, cum_k, lse, oc), fl=fl, bf=bf, yb=yb, yc=yc, pa=pa, pb=pb, pc=pc, mixed=mixed, y=y)
    return xn, saved


def _mix_bwd(tag, w, l, s, mod, dxo, seq):
    x = s["x"]
    t_rows, d = x.shape
    bl = t_rows // seq
    shift, scale, gate = mod
    chans, nh = w["layout"]["chans"], w["layout"]["heads"]
    width = nh * HEAD_DIM
    nb = width // LANE
    moff = 2 * chans // d
    mb = w["merge_b"][l][None]
    ng = NUM_CHIPS
    dy, dgate = _resid_bwd(tag + "_res_bwd", dxo, s["y"], gate, 1.0, seq)
    dmixed = _rows_nt(tag + "_out_dx", dy, w["w_o"], l, F32)
    dw_o = _rows_tn(tag + "_out_dw", s["mixed"], dy, ng, BF16)
    dpa, dpb, dpc, dlog, dmb = _merge_bwd(tag + "_merge_bwd", s["proj"], moff, mb, s["pa"], s["pb"], s["pc"], dmixed)
    dya = _rows_nt(tag + "_prg_dx", dpa, w["w_rg"], l, F32)
    dw_rg = _rows_tn(tag + "_prg_dw", s["ya"], dpa, ng, BF16)
    dyb = _grouped_nt(tag + "_psb_dx", dpb, w["w_sb"], l, F32)
    dw_sb = _grouped_tn(tag + "_psb_dw", s["yb"], dpb, ng, BF16)
    dyc = _grouped_nt(tag + "_pfox_dx", dpc, w["w_fox"], l, F32)
    dw_fox = _grouped_tn(tag + "_pfox_dw", s["yc"], dpc, ng, BF16)
    qkv = s["qkv"]
    dq_b, dk_b, dv_b = _sb_bwd(tag + "_sb_bwd", qkv, 0, width, bl, seq, s["ltot"], dyb)
    cum_q, cum_k, lse, oc = s["fox"]
    dq_c, dk_c, dv_c, dcq, dck = _fox_bwd(tag + "_fox_bwd", qkv, 3 * nb, width, bl, seq, cum_q, cum_k, lse, oc, dyc)
    dcum = dcq.reshape(bl * nh, seq) + dck.reshape(bl * nh, seq)
    dfl, dbf = _forget_cum_bwd(tag + "_cum_bwd", s["fl"], s["bf"], dcum, nh)
    dfl_t = dfl.reshape(bl, nh, seq).transpose(0, 2, 1).reshape(t_rows, nh)
    dflp = jnp.pad(dfl_t, ((0, 0), (0, LANE - nh))).astype(BF16)
    drgx, dgt, rg_small, dwa, dwx = _rg_bwd(tag + "_rg_bwd", s["proj"], dya, s["rg"], w["rg"][l], seq, chans)
    dproj = jnp.concatenate([drgx, dgt, dlog], axis=1)
    dqkv = jnp.concatenate([dq_b, dk_b, dv_b, dq_c, dk_c, dv_c], axis=1)
    w_a, w_b, w_f = w["w_a"][l], w["w_b"][l], w["w_f"][l]
    pa_w, pb_w = w_a.shape[1], w_b.shape[1]
    tm, tn = _tile(t_rows, 512), _tile(d, 512)
    rows = lambda n: pl.BlockSpec((tm, n), lambda i, j: (i, 0))
    wrow = lambda n: pl.BlockSpec((tn, n), lambda i, j: (j, 0))
    dh = _mmk(tag + "_in_dx", [dproj, dqkv, dflp, w_a, w_b, w_f],
              [rows(pa_w), rows(pb_w), rows(LANE), wrow(pa_w), wrow(pb_w), wrow(LANE)],
              [(0, ALL, 3, ALL, NT), (1, ALL, 4, ALL, NT), (2, ALL, 5, ALL, NT)],
              (t_rows, d), F32, (t_rows // tm, d // tn), pl.BlockSpec((tm, tn), lambda i, j: (i, j)))
    hb = s["h"]
    dw_a = _mm(tag + "_in_a_dw", hb, dproj, TN, BF16)
    dw_b = _mm(tag + "_in_b_dw", hb, dqkv, TN, BF16)
    dw_f = _mm(tag + "_in_f_dw", hb, dflp, TN, BF16)
    dx, dshift, dscale, dgain = _normmod_bwd(tag + "_norm_bwd", x, dh, dxo, w["mix_norm"][l][None], scale, seq)
    grads = dict(w_in=(dw_a, dw_b, dw_f), w_rg=dw_rg, w_sb=dw_sb, w_fox=dw_fox, w_o=dw_o, mix_norm=dgain,
                 rg_small=rg_small, rg_dwa=dwa, rg_dwx=dwx, fox_bf=dbf, merge_b=dmb)
    return dx, (dshift, dscale, dgate), grads


def _silu(name, c):
    def body(c_ref, o_ref):
        v = c_ref[...]
        o_ref[...] = v * _sigmoid(v)

    return pl.pallas_call(body, name=name, out_shape=jax.ShapeDtypeStruct(c.shape, F32),
                          compiler_params=_params())(c)


def _blockdiag(wb):
    nb, bd, _ = wb.shape
    per = LANE // bd
    t = wb.reshape(nb // per, per, bd, 1, bd)
    eye = jnp.eye(per, dtype=wb.dtype).reshape(1, per, 1, per, 1)
    return (t * eye).reshape(nb // per, LANE, LANE).astype(BF16)


def _unblockdiag(t, bd):
    n = t.shape[0]
    per = LANE // bd
    t5 = t.reshape(n, per, bd, per, bd)
    return jnp.stack([t5[:, p, :, p, :] for p in range(per)], axis=1).reshape(n * per, bd, bd)


def _prepare(gw, a, d, chans, nh):
    depth = a["ada_b"].shape[0]
    wq = 3 * nh * HEAD_DIM
    o_m = 2 * chans + 2 * wq
    w = {"layout": dict(chans=chans, heads=nh)}
    for n in ("ffn1_w1", "ffn1_w3", "ffn1_w2", "ffn2_w1", "ffn2_w3", "ffn2_w2", "w_rg", "w_sb", "w_fox", "w_o"):
        w[n] = gw[n]
    for n in ("ffn1_norm", "ffn2_norm", "mix_norm", "fox_bf", "merge_b", "final_norm"):
        w[n] = a[n]
    w_a, w_b, w_f, rg = [], [], [], []
    for l in range(depth):
        full = gw["w_in"][:, l].transpose(1, 0, 2).reshape(d, -1)
        w_a.append(jnp.concatenate([full[:, :2 * chans], full[:, o_m + nh:]], axis=1))
        w_b.append(full[:, 2 * chans:o_m])
        w_f.append(jnp.pad(full[:, o_m:o_m + nh], ((0, 0), (0, LANE - nh))))
        conv_w = gw["conv_w"][:, l].transpose(1, 0, 2).reshape(-1, chans)
        rg.append(dict(conv_w=conv_w, conv_b=a["conv_b"][l][None], ba=a["rg_ba"][l][None], bx=a["rg_bx"][l][None],
                       lam=a["rg_lam"][l][None], wa=_blockdiag(a["rg_wa"][l]), wx=_blockdiag(a["rg_wx"][l])))
    w["w_a"], w["w_b"], w["w_f"], w["rg"] = w_a, w_b, w_f, rg
    return w


def _local_step(w, x, tgt, mods, fm):
    bl, seq, d = x.shape
    t_rows = bl * seq
    depth = len(mods)
    mod3 = []
    for l in range(depth):
        m4 = mods[l].reshape(bl, 9, 1, d)
        mod3.append([(m4[:, 3 * k], m4[:, 3 * k + 1], m4[:, 3 * k + 2]) for k in range(3)])
    fm4 = fm.reshape(bl, 2, 1, d)
    saved = []
    xc = x.reshape(t_rows, d)
    for l in range(depth):
        xc, s1 = _ffn_fwd(f"l{l}_ffn1", w, l, "ffn1_", xc, mod3[l][0], seq)
        xc, s2 = _mix_fwd(f"l{l}_mix", w, l, xc, mod3[l][1], seq)
        xc, s3 = _ffn_fwd(f"l{l}_ffn2", w, l, "ffn2_", xc, mod3[l][2], seq)
        saved.append((s1, s2, s3))
    lpart, dx, dfs, dfc, dfg = _final_loss("final", xc, tgt.reshape(t_rows, d), w["final_norm"][None],
                                           fm4[:, 0], fm4[:, 1], seq)
    loss = jnp.sum(lpart[:, 0, 0])
    grads = {"final_norm": dfg, "layers": [None] * depth}
    dmods = [None] * depth
    for l in reversed(range(depth)):
        s1, s2, s3 = saved[l]
        dx, dm3, g3 = _ffn_bwd(f"l{l}_ffn2", w, l, "ffn2_", s3, mod3[l][2], dx, seq)
        dx, dm2, g2 = _mix_bwd(f"l{l}_mix", w, l, s2, mod3[l][1], dx, seq)
        dx, dm1, g1 = _ffn_bwd(f"l{l}_ffn1", w, l, "ffn1_", s1, mod3[l][0], dx, seq)
        dmods[l] = jnp.concatenate([*dm1, *dm2, *dm3], axis=1).reshape(bl, 9 * d)
        grads["layers"][l] = {**g1, **g2, **g3}
    dfm = jnp.concatenate([dfs, dfc], axis=1).reshape(bl, 2 * d)
    return loss, dx.reshape(bl, seq, d), grads, dmods, dfm


def _mesh_pos():
    return lax.axis_index("x"), lax.axis_index("y"), lax.axis_index("c")


def _other_chips(x, y):
    return ((1 - x, y), (x, 1 - y), (1 - x, 1 - y))


def _gather_two_level(name, arrs):
    n = len(arrs)

    def body(*refs):
        ins, outs = refs[:n], refs[n:2 * n]
        send, recv, send2, recv2, loc = refs[2 * n:]
        x, y, c = _mesh_pos()
        me = 2 * x + y
        chips = _other_chips(x, y)
        sib = (x, y, 1 - c)
        local = [pltpu.make_async_copy(ins[i], outs[i].at[me], loc.at[i]) for i in range(n)]
        for cp in local:
            cp.start()
        first = []
        for j, (px, py) in enumerate(chips):
            for i in range(n):
                first.append(pltpu.make_async_remote_copy(
                    src_ref=ins[i].at[c], dst_ref=outs[i].at[me, c], send_sem=send.at[j * n + i],
                    recv_sem=recv.at[j * n + i], device_id=(px, py, c), device_id_type=MESH))
        for cp in first:
            cp.start()
        passed = []
        for j, (px, py) in enumerate(chips):
            for i in range(n):
                landed = outs[i].at[2 * px + py, c]
                pltpu.make_async_remote_copy(
                    src_ref=ins[i].at[c], dst_ref=landed, send_sem=send.at[j * n + i],
                    recv_sem=recv.at[j * n + i], device_id=(px, py, c), device_id_type=MESH).wait_recv()
                fwd = pltpu.make_async_remote_copy(
                    src_ref=landed, dst_ref=landed, send_sem=send2.at[j * n + i],
                    recv_sem=recv2.at[j * n + i], device_id=sib, device_id_type=MESH)
                fwd.start()
                passed.append(fwd)
        for j, (px, py) in enumerate(chips):
            for i in range(n):
                theirs = outs[i].at[2 * px + py, 1 - c]
                pltpu.make_async_remote_copy(
                    src_ref=theirs, dst_ref=theirs, send_sem=send2.at[j * n + i],
                    recv_sem=recv2.at[j * n + i], device_id=sib, device_id_type=MESH).wait_recv()
        for cp in first + passed:
            cp.wait_send()
        for cp in local:
            cp.wait()

    return pl.pallas_call(
        body, name=name, in_specs=[ANY] * n, out_specs=[ANY] * n,
        out_shape=[jax.ShapeDtypeStruct((NUM_CHIPS,) + a.shape, a.dtype) for a in arrs],
        scratch_shapes=[pltpu.SemaphoreType.DMA((3 * n,)), pltpu.SemaphoreType.DMA((3 * n,)),
                        pltpu.SemaphoreType.DMA((3 * n,)), pltpu.SemaphoreType.DMA((3 * n,)),
                        pltpu.SemaphoreType.DMA((n,))],
    )(*arrs)


def _split_to_sibling(name, arrs):
    n = len(arrs)
    slabs = arrs[0].shape[0]

    def body(*refs):
        ins, mine, theirs = refs[:n], refs[n:2 * n], refs[2 * n:3 * n]
        send, recv, loc = refs[3 * n:]
        x, y, c = _mesh_pos()
        sib = (x, y, 1 - c)
        for i in range(n):
            for s in range(slabs):
                pltpu.make_async_copy(ins[i].at[s, c], mine[i].at[s], loc.at[i]).start()
                pltpu.make_async_remote_copy(
                    src_ref=ins[i].at[s, 1 - c], dst_ref=theirs[i].at[s], send_sem=send.at[i],
                    recv_sem=recv.at[i], device_id=sib, device_id_type=MESH).start()
        for i in range(n):
            pltpu.make_async_remote_copy(
                src_ref=mine[i], dst_ref=theirs[i], send_sem=send.at[i], recv_sem=recv.at[i],
                device_id=sib, device_id_type=MESH).wait()
            pltpu.make_async_copy(theirs[i], mine[i], loc.at[i]).wait()

    halves = [jax.ShapeDtypeStruct((a.shape[0],) + a.shape[2:], a.dtype) for a in arrs]
    return pl.pallas_call(
        body, name=name, in_specs=[ANY] * n, out_specs=[ANY] * (2 * n), out_shape=halves + halves,
        scratch_shapes=[pltpu.SemaphoreType.DMA((n,)), pltpu.SemaphoreType.DMA((n,)), pltpu.SemaphoreType.DMA((n,))],
    )(*arrs)


def _scatter_chips(name, arrs):
    n = len(arrs)

    def body(*refs):
        ins, outs = refs[:n], refs[n:2 * n]
        send, recv, loc = refs[2 * n:]
        x, y, c = _mesh_pos()
        me = 2 * x + y
        chips = _other_chips(x, y)
        local = [pltpu.make_async_copy(ins[i].at[me], outs[i].at[me], loc.at[i]) for i in range(n)]
        for cp in local:
            cp.start()
        sends = []
        for j, (px, py) in enumerate(chips):
            for i in range(n):
                sends.append(pltpu.make_async_remote_copy(
                    src_ref=ins[i].at[2 * px + py], dst_ref=outs[i].at[me], send_sem=send.at[j * n + i],
                    recv_sem=recv.at[j * n + i], device_id=(px, py, c), device_id_type=MESH))
        for s in sends:
            s.start()
        for j, (px, py) in enumerate(chips):
            for i in range(n):
                pltpu.make_async_remote_copy(
                    src_ref=ins[i].at[me], dst_ref=outs[i].at[2 * px + py], send_sem=send.at[j * n + i],
                    recv_sem=recv.at[j * n + i], device_id=(px, py, c), device_id_type=MESH).wait_recv()
        for s in sends:
            s.wait_send()
        for cp in local:
            cp.wait()

    return pl.pallas_call(
        body, name=name, in_specs=[ANY] * n, out_specs=[ANY] * n,
        out_shape=[jax.ShapeDtypeStruct(a.shape, a.dtype) for a in arrs],
        scratch_shapes=[pltpu.SemaphoreType.DMA((3 * n,)), pltpu.SemaphoreType.DMA((3 * n,)),
                        pltpu.SemaphoreType.DMA((n,))],
    )(*arrs)


def _join_halves(name, arrs):
    n = len(arrs)

    def body(*refs):
        ins, outs = refs[:n], refs[n:2 * n]
        send, recv, loc = refs[2 * n:]
        x, y, c = _mesh_pos()
        copies, local = [], []
        for i in range(n):
            local.append(pltpu.make_async_copy(ins[i], outs[i].at[c], loc.at[i]))
            copies.append(pltpu.make_async_remote_copy(
                src_ref=ins[i], dst_ref=outs[i].at[c], send_sem=send.at[i],
                recv_sem=recv.at[i], device_id=(x, y, 1 - c), device_id_type=MESH))
        for cp in local + copies:
            cp.start()
        for i in range(n):
            copies[i].wait_send()
            pltpu.make_async_remote_copy(
                src_ref=ins[i], dst_ref=outs[i].at[1 - c], send_sem=send.at[i],
                recv_sem=recv.at[i], device_id=(x, y, 1 - c), device_id_type=MESH).wait_recv()
        for cp in local:
            cp.wait()

    return pl.pallas_call(
        body, name=name, in_specs=[ANY] * n, out_specs=[ANY] * n,
        out_shape=[jax.ShapeDtypeStruct((2,) + a.shape, a.dtype) for a in arrs],
        scratch_shapes=[pltpu.SemaphoreType.DMA((n,)), pltpu.SemaphoreType.DMA((n,)), pltpu.SemaphoreType.DMA((n,))],
    )(*arrs)


def _gather_all(name, pack):
    def body(in_ref, out_ref, send, recv, loc):
        x, y, c = _mesh_pos()
        me = 4 * x + 2 * y + c
        mine = pltpu.make_async_copy(in_ref, out_ref.at[me], loc)
        mine.start()
        peers = []
        for mask in range(1, NUM_DEVICES):
            px = 1 - x if mask & 4 else x
            py = 1 - y if mask & 2 else y
            pc = 1 - c if mask & 1 else c
            peers.append((px, py, pc))
        sends = [pltpu.make_async_remote_copy(
            src_ref=in_ref, dst_ref=out_ref.at[me], send_sem=send.at[k], recv_sem=recv.at[k],
            device_id=p, device_id_type=MESH) for k, p in enumerate(peers)]
        for s in sends:
            s.start()
        for k, (px, py, pc) in enumerate(peers):
            pltpu.make_async_remote_copy(
                src_ref=in_ref, dst_ref=out_ref.at[4 * px + 2 * py + pc], send_sem=send.at[k], recv_sem=recv.at[k],
                device_id=(px, py, pc), device_id_type=MESH).wait_recv()
        for s in sends:
            s.wait_send()
        mine.wait()

    return pl.pallas_call(
        body, name=name, in_specs=[ANY], out_specs=ANY,
        out_shape=jax.ShapeDtypeStruct((NUM_DEVICES,) + pack.shape, pack.dtype),
        scratch_shapes=[pltpu.SemaphoreType.DMA((NUM_DEVICES - 1,)), pltpu.SemaphoreType.DMA((NUM_DEVICES - 1,)),
                        pltpu.SemaphoreType.DMA],
    )(pack)


def _sum_slots(name, slots, out_dtype):
    g, rows, cols = slots.shape
    tr = _rtile(rows, 256)

    def body(s_ref, o_ref):
        acc = s_ref[0].astype(F32)
        for k in range(1, g):
            acc = acc + s_ref[k].astype(F32)
        o_ref[...] = acc.astype(o_ref.dtype)

    return pl.pallas_call(
        body, name=name, grid=(rows // tr,),
        in_specs=[pl.BlockSpec((g, tr, cols), lambda i: (0, i, 0))],
        out_specs=pl.BlockSpec((tr, cols), lambda i: (i, 0)),
        out_shape=jax.ShapeDtypeStruct((rows, cols), out_dtype), compiler_params=_params(),
    )(slots)


def _add_pair(name, p, q):
    g, rows, cols = p.shape
    tr = _rtile(rows, 128)

    def body(p_ref, q_ref, o_ref):
        o_ref[...] = (p_ref[...].astype(F32) + q_ref[...].astype(F32)).astype(o_ref.dtype)

    spec = pl.BlockSpec((g, tr, cols), lambda i: (0, i, 0))
    return pl.pallas_call(
        body, name=name, grid=(rows // tr,), in_specs=[spec, spec], out_specs=spec,
        out_shape=jax.ShapeDtypeStruct(p.shape, BF16), compiler_params=_params(),
    )(p, q)


def _adamw(name, g, w, m, v, l=None):
    rows, cols = g.shape
    tr = _rtile(rows, 128)

    def body(g_ref, w_ref, m_ref, v_ref, d_o, m_o, v_o):
        gv = g_ref[...]
        mn = ADAM_B1 * m_ref[...] + (1.0 - ADAM_B1) * gv
        vn = ADAM_B2 * v_ref[...] + (1.0 - ADAM_B2) * (gv * gv)
        m_hat = mn / (1.0 - ADAM_B1 ** ADAM_STEP)
        v_hat = vn / (1.0 - ADAM_B2 ** ADAM_STEP)
        d_o[...] = -ADAM_LR * (m_hat / (jnp.sqrt(v_hat) + ADAM_EPS) + ADAM_WD * w_ref[...])
        m_o[...] = mn
        v_o[...] = vn

    gspec = pl.BlockSpec((tr, cols), lambda i: (i, 0))
    wspec = gspec if l is None else pl.BlockSpec((None, tr, cols), lambda i: (l, i, 0))
    f = jax.ShapeDtypeStruct((rows, cols), F32)
    return pl.pallas_call(
        body, name=name, grid=(rows // tr,), in_specs=[gspec] + [wspec] * 3, out_specs=[gspec] * 3,
        out_shape=[f] * 3, compiler_params=_params(),
    )(g, w, m, v)


def _adamw_layers(name, g0, g1, w, m, v):
    rows, cols = g0.shape
    tr = _rtile(rows, 128)
    nt = rows // tr

    def body(g0_ref, g1_ref, w_ref, m_ref, v_ref, g_o, d_o, m_o, v_o):
        gv = jnp.where(pl.program_id(0) == 0, g0_ref[...], g1_ref[...])
        mn = ADAM_B1 * m_ref[...] + (1.0 - ADAM_B1) * gv
        vn = ADAM_B2 * v_ref[...] + (1.0 - ADAM_B2) * (gv * gv)
        m_hat = mn / (1.0 - ADAM_B1 ** ADAM_STEP)
        v_hat = vn / (1.0 - ADAM_B2 ** ADAM_STEP)
        g_o[...] = gv
        d_o[...] = -ADAM_LR * (m_hat / (jnp.sqrt(v_hat) + ADAM_EPS) + ADAM_WD * w_ref[...])
        m_o[...] = mn
        v_o[...] = vn

    g0spec = pl.BlockSpec((tr, cols), lambda l, i: (i * (1 - l) + (nt - 1) * l, 0))
    g1spec = pl.BlockSpec((tr, cols), lambda l, i: (i * l, 0))
    wspec = pl.BlockSpec((None, tr, cols), lambda l, i: (l, i, 0))
    f = jax.ShapeDtypeStruct((2, rows, cols), F32)
    return pl.pallas_call(
        body, name=name, grid=(2, nt), in_specs=[g0spec, g1spec, wspec, wspec, wspec], out_specs=[wspec] * 4,
        out_shape=[f] * 4, compiler_params=_params(),
    )(g0, g1, w, m, v)


def _colsum(name, a):
    def body(a_ref, o_ref):
        o_ref[...] = jnp.sum(a_ref[...], axis=0, keepdims=True)

    return pl.pallas_call(body, name=name, out_shape=jax.ShapeDtypeStruct((1, a.shape[1]), F32),
                          compiler_params=_params())(a)


PACK_UNIT = SUBLANE * LANE


def _pack(items):
    flat, layout, o = [], [], 0
    for it in items:
        n = it.size
        pad = -n % PACK_UNIT
        flat.append(jnp.pad(it.reshape(-1).astype(F32), (0, pad)))
        layout.append((o, n, it.shape))
        o += n + pad
    return jnp.concatenate(flat).reshape(-1, LANE), layout


def _unpack(pack, layout):
    flat = pack.reshape(-1)
    return [flat[o:o + n].reshape(shape) for o, n, shape in layout]


WEIGHTS = ("ffn1_norm", "ffn1_w1", "ffn1_w3", "ffn1_w2", "mix_norm", "w_in", "conv_w", "conv_b", "rg_wa", "rg_ba",
           "rg_wx", "rg_bx", "rg_lam", "fox_bf", "merge_b", "w_rg", "w_sb", "w_fox", "w_o", "ffn2_norm", "ffn2_w1",
           "ffn2_w3", "ffn2_w2", "ada_w", "ada_b", "final_norm", "final_ada_w", "final_ada_b")
DENSE = ("ffn1_w1", "ffn1_w3", "ffn1_w2", "w_in", "w_rg", "w_sb", "w_fox", "w_o", "ffn2_w1", "ffn2_w3", "ffn2_w2")
SMALL = ("ffn1_norm", "mix_norm", "ffn2_norm", "rg_small", "rg_wa", "rg_wx", "fox_bf", "merge_b")


def _step(a):
    x, c, tgt = a["x"], a["c"], a["loss_target"]
    bl, seq, d = x.shape
    depth, nh = a["fox_bf"].shape
    chans = a["rg_lam"].shape[1]
    bd = a["rg_wa"].shape[2]
    wq = 3 * nh * HEAD_DIM
    o_m = 2 * chans + 2 * wq
    batch = NUM_DEVICES * bl
    mx, my, mc = _mesh_pos()
    me = 2 * mx + my
    dev = 4 * mx + 2 * my + mc

    c_rows = -(-bl * d // LANE // SUBLANE) * SUBLANE
    c_pack = jnp.pad(c.reshape(-1, LANE), ((0, c_rows - bl * d // LANE), (0, 0)))
    c_all = _gather_all("gather_c", c_pack)[:, :bl * d // LANE].reshape(batch, d)
    c_act = _silu("c_act", c_all)
    c_b = c_act.astype(BF16)
    ncol, fcol = a["ada_w"].shape[2], a["final_ada_w"].shape[1]
    cols = []
    for l in range(depth):
        bias = jnp.broadcast_to(lax.dynamic_slice_in_dim(a["ada_b"][l], me * ncol, ncol)[None], (batch, ncol))
        cols.append(_mm(f"ada{l}", c_b, a["ada_w"][l].astype(BF16), NN, F32, acc=bias))
    bias = jnp.broadcast_to(lax.dynamic_slice_in_dim(a["final_ada_b"], me * fcol, fcol)[None], (batch, fcol))
    cols.append(_mm("ada_final", c_b, a["final_ada_w"].astype(BF16), NN, F32, acc=bias))
    mod_cols = jnp.concatenate(cols, axis=1).reshape(2, batch // 2, depth * ncol + fcol)

    names = DENSE + ("conv_w", "mod_cols")
    got = _gather_two_level("gather_weights", [a[n].astype(BF16) for n in DENSE] + [a["conv_w"], mod_cols])
    gw = dict(zip(names, got))
    w = _prepare(gw, a, d, chans, nh)
    mod_all = gw["mod_cols"].reshape(NUM_CHIPS, batch, -1)
    mine = lambda full: lax.dynamic_slice_in_dim(full, dev * bl, bl, axis=0)
    mods = [mine(mod_all[:, :, l * ncol:(l + 1) * ncol].transpose(1, 0, 2).reshape(batch, NUM_CHIPS * ncol))
            for l in range(depth)]
    fm = mine(mod_all[:, :, depth * ncol:].transpose(1, 0, 2).reshape(batch, NUM_CHIPS * fcol))

    loss, grad_x, grads, dmods, dfm = _local_step(w, x, tgt, mods, fm)
    loss = lax.psum(loss, ("x", "y", "c"))

    rs_in = []
    for l in range(depth):
        for n in DENSE:
            if n == "w_in":
                ga, gb, gf = grads["layers"][l]["w_in"]
                orig = jnp.concatenate([ga[:, :2 * chans], gb, gf[:, :nh], ga[:, 2 * chans:]], axis=1)
                rs_in.append(orig.reshape(d, NUM_CHIPS, -1).transpose(1, 0, 2))
            else:
                rs_in.append(grads["layers"][l][n])
    halves = _split_to_sibling("split_grads", [g.reshape(g.shape[0], 2, g.shape[1] // 2, g.shape[2]) for g in rs_in])
    nrs = len(rs_in)
    chip_part = [_add_pair(f"add_cores_{k}", halves[k], halves[nrs + k]) for k in range(nrs)]
    slots = _scatter_chips("scatter_grads", chip_part)
    reduced = [_sum_slots(f"sum_chips_{k}", s, F32) for k, s in enumerate(slots)]
    full = [f.reshape(-1, f.shape[-1]) for f in _join_halves("join_grads", reduced)]

    out = {}

    def put(n, res, per_layer):
        for kind, val in zip(("grad_", "delta_", "new_m_", "new_v_"), res):
            out[kind + n] = jnp.stack(val).reshape(a[n].shape) if per_layer else val.reshape(a[n].shape)

    def flat3(v):
        return v.reshape(depth, -1, v.shape[-1])

    assert depth == 2
    for k, n in enumerate(DENSE):
        put(n, _adamw_layers(f"adamw_{n}", full[k], full[len(DENSE) + k], flat3(a[n]), flat3(a["m_" + n]),
                             flat3(a["v_" + n])), False)

    items = []
    for l in range(depth):
        g = grads["layers"][l]
        items += [g["ffn1_norm"], g["mix_norm"], g["ffn2_norm"], g["rg_small"], _unblockdiag(g["rg_dwa"], bd),
                  _unblockdiag(g["rg_dwx"], bd), g["fox_bf"], g["merge_b"], dmods[l]]
    items += [grads["final_norm"], dfm]
    pack, layout = _pack(items)
    gath = _gather_all("gather_small", pack)
    tot = _sum_slots("sum_small", gath, F32)

    def wpack(pre):
        its = []
        for l in range(depth):
            rg_rows = jnp.concatenate([jnp.zeros((4, chans), F32), a[pre + "conv_b"][l][None], a[pre + "rg_ba"][l][None],
                                       a[pre + "rg_bx"][l][None], a[pre + "rg_lam"][l][None]], axis=0)
            its += [a[pre + "ffn1_norm"][l], a[pre + "mix_norm"][l], a[pre + "ffn2_norm"][l], rg_rows,
                    a[pre + "rg_wa"][l], a[pre + "rg_wx"][l], a[pre + "fox_bf"][l], a[pre + "merge_b"][l],
                    jnp.zeros((bl, 9 * d), F32)]
        its += [a[pre + "final_norm"], jnp.zeros((bl, 2 * d), F32)]
        return _pack(its)[0]

    res_small = [_unpack(r, layout) for r in [tot] + list(_adamw("adamw_small", tot, wpack(""), wpack("m_"), wpack("v_")))]
    per = len(SMALL) + 1
    for j, n in enumerate(SMALL):
        if n == "rg_small":
            for row, nm in ((4, "conv_b"), (5, "rg_ba"), (6, "rg_bx"), (7, "rg_lam")):
                put(nm, [[r[l * per + j][row] for l in range(depth)] for r in res_small], True)
        else:
            put(n, [[r[l * per + j] for l in range(depth)] for r in res_small], True)
    put("final_norm", [r[depth * per] for r in res_small], False)

    gflat = gath.reshape(NUM_DEVICES, -1)

    def rows_of(idx):
        o, n, shape = layout[idx]
        return gflat[:, o:o + n].reshape(NUM_DEVICES * shape[0], shape[1])

    late_g, ada = [], []
    for l in range(depth):
        dmod_all = rows_of(l * per + per - 1)
        late_g.append(_colsum(f"ada_b_grad_{l}", dmod_all))
        cut = lax.dynamic_slice_in_dim(dmod_all, me * ncol, ncol, axis=1).astype(BF16)
        ada.append(_mm(f"ada_w_grad_{l}", c_b, cut, TN, F32))
    put("ada_w", _adamw_layers("adamw_ada_w", ada[0], ada[1], a["ada_w"], a["m_ada_w"], a["v_ada_w"]), False)
    dfm_all = rows_of(depth * per + 1)
    late_g.append(_colsum("final_ada_b_grad", dfm_all))
    cut = lax.dynamic_slice_in_dim(dfm_all, me * fcol, fcol, axis=1).astype(BF16)
    gl = _mm("final_ada_w_grad", c_b, cut, TN, F32)
    put("final_ada_w", [gl] + list(_adamw("adamw_final_ada_w", gl, a["final_ada_w"], a["m_final_ada_w"],
                                          a["v_final_ada_w"])), False)
    cshard = a["conv_w"].shape[2]
    for l in range(depth):
        rg_tot = res_small[0][l * per + SMALL.index("rg_small")]
        late_g.append(lax.dynamic_slice_in_dim(rg_tot[:4], me * cshard, cshard, axis=1))
    gp2, layout2 = _pack(late_g)

    def wpack2(pre):
        return _pack([a[pre + "ada_b"][l][None] for l in range(depth)] + [a[pre + "final_ada_b"][None]]
                     + [a[pre + "conv_w"][l] for l in range(depth)])[0]

    res_late = [_unpack(r, layout2) for r in [gp2] + list(_adamw("adamw_late", gp2, wpack2(""), wpack2("m_"), wpack2("v_")))]
    put("ada_b", [[r[l] for l in range(depth)] for r in res_late], True)
    put("final_ada_b", [r[depth] for r in res_late], False)
    put("conv_w", [[r[depth + 1 + l] for l in range(depth)] for r in res_late], True)

    outs = [loss, grad_x]
    for kind in ("grad_", "delta_", "new_m_", "new_v_"):
        outs += [out[kind + n] for n in WEIGHTS]
    return tuple(outs)


def kernel(x, c, ffn1_norm, ffn1_w1, ffn1_w3, ffn1_w2, mix_norm, w_in, conv_w, conv_b, rg_wa, rg_ba, rg_wx, rg_bx, rg_lam, fox_bf, merge_b, w_rg, w_sb, w_fox, w_o, ffn2_norm, ffn2_w1, ffn2_w3, ffn2_w2, ada_w, ada_b, final_norm, final_ada_w, final_ada_b, loss_target, m_ffn1_norm, m_ffn1_w1, m_ffn1_w3, m_ffn1_w2, m_mix_norm, m_w_in, m_conv_w, m_conv_b, m_rg_wa, m_rg_ba, m_rg_wx, m_rg_bx, m_rg_lam, m_fox_bf, m_merge_b, m_w_rg, m_w_sb, m_w_fox, m_w_o, m_ffn2_norm, m_ffn2_w1, m_ffn2_w3, m_ffn2_w2, m_ada_w, m_ada_b, m_final_norm, m_final_ada_w, m_final_ada_b, v_ffn1_norm, v_ffn1_w1, v_ffn1_w3, v_ffn1_w2, v_mix_norm, v_w_in, v_conv_w, v_conv_b, v_rg_wa, v_rg_ba, v_rg_wx, v_rg_bx, v_rg_lam, v_fox_bf, v_merge_b, v_w_rg, v_w_sb, v_w_fox, v_w_o, v_ffn2_norm, v_ffn2_w1, v_ffn2_w3, v_ffn2_w2, v_ada_w, v_ada_b, v_final_norm, v_final_ada_w, v_final_ada_b):
    args = dict(locals())
    return _step(args)
```

```python
import math

import jax
import jax.numpy as jnp
from jax import lax
from jax.experimental import pallas as pl
from jax.experimental.pallas import tpu as pltpu

F32 = jnp.float32
BF16 = jnp.bfloat16

NUM_CHIPS = 4
NUM_DEVICES = 8
HEAD_DIM = 64
LANE = 128
SUBLANE = 8
VMEM_LIMIT = 56 * 1024 * 1024
EPS = 1e-6
RG_C = 8.0
ADAM_LR = 0.001
ADAM_B1 = 0.9
ADAM_B2 = 0.999
ADAM_EPS = 1e-08
ADAM_WD = 0.01
ADAM_STEP = 10
MESH = pl.DeviceIdType.MESH
ANY = pl.BlockSpec(memory_space=pl.ANY)
SCALAR = pl.BlockSpec(memory_space=pltpu.SMEM)


def _params():
    return pltpu.CompilerParams(vmem_limit_bytes=VMEM_LIMIT)


def _tile(dim, pref):
    if dim <= pref:
        return dim
    t = (pref // LANE) * LANE
    while t >= LANE:
        if dim % t == 0:
            return t
        t -= LANE
    return dim


def _rtile(rows, pref, unit=2 * SUBLANE):
    if rows <= pref:
        return rows
    t = (pref // unit) * unit
    while t >= unit:
        if rows % t == 0:
            return t
        t -= unit
    return rows


def _sigmoid(x):
    return 1.0 / (1.0 + jnp.exp(-x))


def _softplus(x):
    return jnp.maximum(x, 0.0) + jnp.log(1.0 + jnp.exp(-jnp.abs(x)))


def _expm1(x):
    small = x * (1.0 + x * (0.5 + x * (1.0 / 6.0 + x * (1.0 / 24.0))))
    return jnp.where(jnp.abs(x) < 0.01, small, jnp.exp(x) - 1.0)


_GELU_K = math.sqrt(2.0 / math.pi)


def _gelu_and_grad(x):
    inner = _GELU_K * (x + 0.044715 * x * x * x)
    t = jnp.tanh(inner)
    val = 0.5 * x * (1.0 + t)
    dinner = _GELU_K * (1.0 + 3.0 * 0.044715 * x * x)
    grad = 0.5 * (1.0 + t) + 0.5 * x * (1.0 - t * t) * dinner
    return val, grad


NN = ((1,), (0,))
NT = ((1,), (1,))
TN = ((0,), (0,))
ALL = slice(None)


def _mmk(name, ops, specs, terms, out_shape, out_dtype, grid, o_spec, acc=None):
    n_ops = len(ops)

    def body(*refs):
        o_ref = refs[-1]
        p = None
        for ia, xa, ib, xb, dims in terms:
            t = lax.dot_general(refs[ia][xa], refs[ib][xb], (dims, ((), ())), preferred_element_type=F32)
            p = t if p is None else p + t
        if acc is not None:
            p = p + refs[n_ops][...].astype(F32)
        o_ref[...] = p.astype(o_ref.dtype)

    in_specs = list(specs)
    args = list(ops)
    if acc is not None:
        in_specs.append(pl.BlockSpec(o_spec.block_shape, o_spec.index_map))
        args.append(acc)
    return pl.pallas_call(
        body, name=name, grid=grid, in_specs=in_specs, out_specs=o_spec,
        out_shape=jax.ShapeDtypeStruct(out_shape, out_dtype), compiler_params=_params(),
    )(*args)


def _mm(name, a, b, dims, out_dtype, acc=None, tm=512, tn=512):
    if dims == NN:
        (m, kk), n = a.shape, b.shape[1]
    elif dims == NT:
        (m, kk), n = a.shape, b.shape[0]
    else:
        (kk, m), n = a.shape, b.shape[1]
    tm, tn = _tile(m, tm), _tile(n, tn)
    if dims == TN:
        a_spec = pl.BlockSpec((kk, tm), lambda i, j: (0, i))
    else:
        a_spec = pl.BlockSpec((tm, kk), lambda i, j: (i, 0))
    if dims == NT:
        b_spec = pl.BlockSpec((tn, kk), lambda i, j: (j, 0))
    else:
        b_spec = pl.BlockSpec((kk, tn), lambda i, j: (0, j))
    return _mmk(name, [a, b], [a_spec, b_spec], [(0, ALL, 1, ALL, dims)], (m, n), out_dtype,
                (m // tm, n // tn), pl.BlockSpec((tm, tn), lambda i, j: (i, j)), acc)


def _row_grid(t_rows, seq, pref=256):
    tm = _tile(seq, pref)
    return tm, seq // tm


def _normmod(name, x, gain, shift, scale, seq):
    t_rows, d = x.shape
    bl = t_rows // seq
    tm, per = _row_grid(t_rows, seq)

    def body(x_ref, g_ref, sh_ref, sc_ref, o_ref):
        xv = x_ref[...]
        rstd = lax.rsqrt(jnp.mean(xv * xv, axis=-1, keepdims=True) + EPS)
        hn = (xv * rstd) * g_ref[...]
        o_ref[...] = (hn * (1.0 + sc_ref[...]) + sh_ref[...]).astype(o_ref.dtype)

    row = pl.BlockSpec((tm, d), lambda b, i: (b * per + i, 0))
    vec = pl.BlockSpec((None, 1, d), lambda b, i: (b, 0, 0))
    return pl.pallas_call(
        body, name=name, grid=(bl, per),
        in_specs=[row, pl.BlockSpec((1, d), lambda b, i: (0, 0)), vec, vec],
        out_specs=row, out_shape=jax.ShapeDtypeStruct((t_rows, d), BF16),
        compiler_params=_params(),
    )(x, gain, shift, scale)


def _normmod_bwd(name, x, dh, dxo, gain, scale, seq):
    t_rows, d = x.shape
    bl = t_rows // seq
    tm, per = _row_grid(t_rows, seq)

    def body(x_ref, dh_ref, dxo_ref, g_ref, sc_ref, dx_ref, dsh_ref, dsc_ref, dg_ref):
        b, i = pl.program_id(0), pl.program_id(1)
        xv = x_ref[...]
        dhv = dh_ref[...]
        rstd = lax.rsqrt(jnp.mean(xv * xv, axis=-1, keepdims=True) + EPS)
        xhat = xv * rstd
        gain_v = g_ref[...]
        dhn = dhv * (1.0 + sc_ref[...])
        dxhat = dhn * gain_v
        dx = rstd * (dxhat - xhat * jnp.mean(dxhat * xhat, axis=-1, keepdims=True))
        dx_ref[...] = dxo_ref[...] + dx

        @pl.when(i == 0)
        def _():
            dsh_ref[...] = jnp.zeros_like(dsh_ref)
            dsc_ref[...] = jnp.zeros_like(dsc_ref)

        @pl.when((i == 0) & (b == 0))
        def _():
            dg_ref[...] = jnp.zeros_like(dg_ref)

        dsh_ref[...] += jnp.sum(dhv, axis=0, keepdims=True)
        dsc_ref[...] += jnp.sum(dhv * (xhat * gain_v), axis=0, keepdims=True)
        dg_ref[...] += jnp.sum(dhn * xhat, axis=0, keepdims=True)

    row = pl.BlockSpec((tm, d), lambda b, i: (b * per + i, 0))
    vec = pl.BlockSpec((None, 1, d), lambda b, i: (b, 0, 0))
    one = pl.BlockSpec((1, d), lambda b, i: (0, 0))
    return pl.pallas_call(
        body, name=name, grid=(bl, per),
        in_specs=[row, row, row, one, vec],
        out_specs=[row, vec, vec, one],
        out_shape=[jax.ShapeDtypeStruct((t_rows, d), F32), jax.ShapeDtypeStruct((bl, 1, d), F32),
                   jax.ShapeDtypeStruct((bl, 1, d), F32), jax.ShapeDtypeStruct((1, d), F32)],
        compiler_params=_params(),
    )(x, dh, dxo, gain, scale)


def _resid(name, x, y, gate, coef, seq):
    t_rows, d = x.shape
    bl = t_rows // seq
    tm, per = _row_grid(t_rows, seq)

    def body(x_ref, y_ref, g_ref, o_ref):
        o_ref[...] = x_ref[...] + (coef * (1.0 + g_ref[...])) * y_ref[...]

    row = pl.BlockSpec((tm, d), lambda b, i: (b * per + i, 0))
    vec = pl.BlockSpec((None, 1, d), lambda b, i: (b, 0, 0))
    return pl.pallas_call(
        body, name=name, grid=(bl, per), in_specs=[row, row, vec], out_specs=row,
        out_shape=jax.ShapeDtypeStruct((t_rows, d), F32), compiler_params=_params(),
    )(x, y, gate)


def _resid_bwd(name, dxo, y, gate, coef, seq):
    t_rows, d = dxo.shape
    bl = t_rows // seq
    tm, per = _row_grid(t_rows, seq)

    def body(dxo_ref, y_ref, g_ref, dy_ref, dg_ref):
        i = pl.program_id(1)
        dxov = dxo_ref[...]
        dy_ref[...] = ((coef * (1.0 + g_ref[...])) * dxov).astype(dy_ref.dtype)

        @pl.when(i == 0)
        def _():
            dg_ref[...] = jnp.zeros_like(dg_ref)

        dg_ref[...] += jnp.sum((coef * y_ref[...]) * dxov, axis=0, keepdims=True)

    row = pl.BlockSpec((tm, d), lambda b, i: (b * per + i, 0))
    vec = pl.BlockSpec((None, 1, d), lambda b, i: (b, 0, 0))
    return pl.pallas_call(
        body, name=name, grid=(bl, per), in_specs=[row, row, vec], out_specs=[row, vec],
        out_shape=[jax.ShapeDtypeStruct((t_rows, d), BF16), jax.ShapeDtypeStruct((bl, 1, d), F32)],
        compiler_params=_params(),
    )(dxo, y, gate)


def _final_loss(name, x, tgt, gain, shift, scale, seq):
    t_rows, d = x.shape
    bl = t_rows // seq
    tm, per = _row_grid(t_rows, seq)

    def body(x_ref, t_ref, g_ref, sh_ref, sc_ref, l_ref, dx_ref, dsh_ref, dsc_ref, dg_ref):
        b, i = pl.program_id(0), pl.program_id(1)
        xv = x_ref[...]
        rstd = lax.rsqrt(jnp.mean(xv * xv, axis=-1, keepdims=True) + EPS)
        xhat = xv * rstd
        gain_v = g_ref[...]
        hn = xhat * gain_v
        yv = hn * (1.0 + sc_ref[...]) + sh_ref[...]
        err = yv - t_ref[...]
        dyv = err * (1.0 / d)
        dhn = dyv * (1.0 + sc_ref[...])
        dxhat = dhn * gain_v
        dx_ref[...] = rstd * (dxhat - xhat * jnp.mean(dxhat * xhat, axis=-1, keepdims=True))

        @pl.when(i == 0)
        def _():
            l_ref[...] = jnp.zeros_like(l_ref)
            dsh_ref[...] = jnp.zeros_like(dsh_ref)
            dsc_ref[...] = jnp.zeros_like(dsc_ref)

        @pl.when((i == 0) & (b == 0))
        def _():
            dg_ref[...] = jnp.zeros_like(dg_ref)

        part = jnp.sum(jnp.sum(err * err, axis=-1, keepdims=True), axis=0, keepdims=True) * (0.5 / d)
        l_ref[...] += jnp.broadcast_to(part, l_ref.shape)
        dsh_ref[...] += jnp.sum(dyv, axis=0, keepdims=True)
        dsc_ref[...] += jnp.sum(dyv * hn, axis=0, keepdims=True)
        dg_ref[...] += jnp.sum(dhn * xhat, axis=0, keepdims=True)

    row = pl.BlockSpec((tm, d), lambda b, i: (b * per + i, 0))
    vec = pl.BlockSpec((None, 1, d), lambda b, i: (b, 0, 0))
    one = pl.BlockSpec((1, d), lambda b, i: (0, 0))
    lvec = pl.BlockSpec((None, 1, LANE), lambda b, i: (b, 0, 0))
    return pl.pallas_call(
        body, name=name, grid=(bl, per),
        in_specs=[row, row, one, vec, vec],
        out_specs=[lvec, row, vec, vec, one],
        out_shape=[jax.ShapeDtypeStruct((bl, 1, LANE), F32), jax.ShapeDtypeStruct((t_rows, d), F32),
                   jax.ShapeDtypeStruct((bl, 1, d), F32), jax.ShapeDtypeStruct((bl, 1, d), F32),
                   jax.ShapeDtypeStruct((1, d), F32)],
        compiler_params=_params(),
    )(x, tgt, gain, shift, scale)


def _ffn_up(name, h, w1, w3, l):
    t_rows, d = h.shape
    ng, fs = w1.shape[0], w1.shape[3]
    tm = _tile(t_rows, 512)

    def body(h_ref, w1_ref, w3_ref, a_ref, b_ref, g_ref):
        hv = h_ref[...]
        av = jnp.dot(hv, w1_ref[...], preferred_element_type=F32)
        bv = jnp.dot(hv, w3_ref[...], preferred_element_type=F32)
        a_ref[...] = av
        b_ref[...] = bv
        g_ref[...] = (av * _sigmoid(av) * bv).astype(g_ref.dtype)

    wspec = pl.BlockSpec((None, None, d, fs), lambda g, i: (g, l, 0, 0))
    out = pl.BlockSpec((None, tm, fs), lambda g, i: (g, i, 0))
    f = jax.ShapeDtypeStruct((ng, t_rows, fs), F32)
    return pl.pallas_call(
        body, name=name, grid=(ng, t_rows // tm),
        in_specs=[pl.BlockSpec((tm, d), lambda g, i: (i, 0)), wspec, wspec], out_specs=[out, out, out],
        out_shape=[f, f, jax.ShapeDtypeStruct((ng, t_rows, fs), BF16)], compiler_params=_params(),
    )(h, w1, w3)


def _ffn_down_dx(name, dy, w2, a, b, l):
    t_rows, d = dy.shape
    ng, fs = w2.shape[0], w2.shape[2]
    tm = _tile(t_rows, 512)

    def body(dy_ref, w2_ref, a_ref, b_ref, da_ref, db_ref):
        dgv = lax.dot_general(dy_ref[...], w2_ref[...], (NT, ((), ())), preferred_element_type=F32)
        av = a_ref[...]
        sig = _sigmoid(av)
        da_ref[...] = (dgv * b_ref[...] * (sig * (1.0 + av * (1.0 - sig)))).astype(da_ref.dtype)
        db_ref[...] = (dgv * (av * sig)).astype(db_ref.dtype)

    blk = pl.BlockSpec((None, tm, fs), lambda g, i: (g, i, 0))
    o = jax.ShapeDtypeStruct((ng, t_rows, fs), BF16)
    return pl.pallas_call(
        body, name=name, grid=(ng, t_rows // tm),
        in_specs=[pl.BlockSpec((tm, d), lambda g, i: (i, 0)),
                  pl.BlockSpec((None, None, fs, d), lambda g, i: (g, l, 0, 0)), blk, blk],
        out_specs=[blk, blk], out_shape=[o, o], compiler_params=_params(),
    )(dy, w2, a, b)


def _ffn_up_dw(name, h, da, db):
    t_rows, d = h.shape
    ng, fs = da.shape[0], da.shape[2]
    tn = _tile(d, 512)

    def body(h_ref, da_ref, db_ref, o1_ref, o3_ref):
        hv = h_ref[...]
        o1_ref[...] = lax.dot_general(hv, da_ref[...], (TN, ((), ())), preferred_element_type=F32).astype(o1_ref.dtype)
        o3_ref[...] = lax.dot_general(hv, db_ref[...], (TN, ((), ())), preferred_element_type=F32).astype(o3_ref.dtype)

    dspec = pl.BlockSpec((None, t_rows, fs), lambda g, i: (g, 0, 0))
    out = pl.BlockSpec((None, tn, fs), lambda g, i: (g, i, 0))
    o = jax.ShapeDtypeStruct((ng, d, fs), BF16)
    return pl.pallas_call(
        body, name=name, grid=(ng, d // tn),
        in_specs=[pl.BlockSpec((t_rows, tn), lambda g, i: (0, i)), dspec, dspec],
        out_specs=[out, out], out_shape=[o, o], compiler_params=_params(),
    )(h, da, db)


def _ffn_fwd(tag, w, l, pre, x, mod, seq):
    t_rows, d = x.shape
    w1, w3, w2 = w[pre + "w1"], w[pre + "w3"], w[pre + "w2"]
    ng, fs = w1.shape[0], w1.shape[3]
    shift, scale, gate = mod
    h = _normmod(tag + "_norm", x, w[pre + "norm"][l][None], shift, scale, seq)
    a, b, gact = _ffn_up(tag + "_up", h, w1, w3, l)
    tm, tn = _tile(t_rows, 512), _tile(d, 512)
    y = _mmk(tag + "_down", [gact, w2],
             [pl.BlockSpec((ng, tm, fs), lambda i, j: (0, i, 0)),
              pl.BlockSpec((ng, None, fs, tn), lambda i, j: (0, l, 0, j))],
             [(0, g, 1, g, NN) for g in range(ng)], (t_rows, d), F32, (t_rows // tm, d // tn),
             pl.BlockSpec((tm, tn), lambda i, j: (i, j)))
    xn = _resid(tag + "_res", x, y, gate, 0.5, seq)
    return xn, (x, h, a, b, gact, y)


def _ffn_bwd(tag, w, l, pre, saved, mod, dxo, seq):
    x, h, a, b, gact, y = saved
    t_rows, d = x.shape
    w1, w3, w2 = w[pre + "w1"], w[pre + "w3"], w[pre + "w2"]
    ng, fs = w1.shape[0], w1.shape[3]
    shift, scale, gate = mod
    tm, tn = _tile(t_rows, 512), _tile(d, 512)
    dy, dgate = _resid_bwd(tag + "_res_bwd", dxo, y, gate, 0.5, seq)
    da, db = _ffn_down_dx(tag + "_down_dx", dy, w2, a, b, l)
    dw2 = _mmk(tag + "_down_dw", [gact, dy],
               [pl.BlockSpec((None, t_rows, fs), lambda g, j: (g, 0, 0)),
                pl.BlockSpec((t_rows, tn), lambda g, j: (0, j))],
               [(0, ALL, 1, ALL, TN)], (ng, fs, d), BF16, (ng, d // tn),
               pl.BlockSpec((None, fs, tn), lambda g, j: (g, 0, j)))
    dw1, dw3 = _ffn_up_dw(tag + "_up_dw", h, da, db)
    dspec = pl.BlockSpec((ng, tm, fs), lambda i, j: (0, i, 0))
    wspec = pl.BlockSpec((ng, None, tn, fs), lambda i, j: (0, l, j, 0))
    dh = _mmk(tag + "_up_dx", [da, db, w1, w3], [dspec, dspec, wspec, wspec],
              [(0, g, 2, g, NT) for g in range(ng)] + [(1, g, 3, g, NT) for g in range(ng)],
              (t_rows, d), F32, (t_rows // tm, d // tn), pl.BlockSpec((tm, tn), lambda i, j: (i, j)))
    dx, dshift, dscale, dgain = _normmod_bwd(tag + "_norm_bwd", x, dh, dxo, w[pre + "norm"][l][None], scale, seq)
    grads = {pre + "w1": dw1, pre + "w3": dw3, pre + "w2": dw2, pre + "norm": dgain}
    return dx, (dshift, dscale, dgate), grads


def _shift_down(v, s, row):
    if s == 0:
        return v
    return jnp.where(row >= s, pltpu.roll(v, s, 0), 0.0)


def _shift_up(v, s, row):
    if s == 0:
        return v
    n = v.shape[0]
    return jnp.where(row < n - s, pltpu.roll(v, n - s, 0), 0.0)


def _scan_fwd(a, u, row):
    n = a.shape[0]
    s = 1
    while s < n:
        ok = row >= s
        a_sh = pltpu.roll(a, s, 0)
        u_sh = pltpu.roll(u, s, 0)
        u = jnp.where(ok, a * u_sh + u, u)
        a = jnp.where(ok, a * a_sh, a)
        s *= 2
    return u


def _scan_bwd(a_next, g, row):
    n = g.shape[0]
    a, u = a_next, g
    s = 1
    while s < n:
        ok = row < n - s
        a_sh = pltpu.roll(a, n - s, 0)
        u_sh = pltpu.roll(u, n - s, 0)
        u = jnp.where(ok, a * u_sh + u, u)
        a = jnp.where(ok, a * a_sh, a)
        s *= 2
    return u


def _rg_specs(seq, cw):
    slab = lambda off: pl.BlockSpec((seq, cw), lambda c, b: (b, off + c))
    par = lambda rows: pl.BlockSpec((rows, cw), lambda c, b: (0, c))
    wbd = pl.BlockSpec((None, cw, cw), lambda c, b: (c, 0, 0))
    return slab, par, wbd


def _rg_fwd(name, proj, p, seq, chans):
    t_rows = proj.shape[0]
    bl = t_rows // seq
    cw = LANE
    nc = chans // cw
    slab, par, wbd = _rg_specs(seq, cw)

    def body(x_ref, gt_ref, cw_ref, cb_ref, wa_ref, ba_ref, wx_ref, bx_ref, lam_ref,
             xa_ref, r_ref, i_ref, h_ref, ya_ref):
        row = lax.broadcasted_iota(jnp.int32, (seq, cw), 0)
        xv = x_ref[...]
        xa = jnp.zeros_like(xv) + cb_ref[...]
        for k in range(4):
            xa = xa + cw_ref[k:k + 1, :] * _shift_down(xv, 3 - k, row)
        xab = xa.astype(BF16)
        r = _sigmoid(jnp.dot(xab, wa_ref[...], preferred_element_type=F32) + ba_ref[...])
        ig = _sigmoid(jnp.dot(xab, wx_ref[...], preferred_element_type=F32) + bx_ref[...])
        log_a = (-RG_C) * r * _softplus(-lam_ref[...])
        a = jnp.exp(log_a)
        u = jnp.sqrt(-_expm1(2.0 * log_a)) * (ig * xa)
        h = _scan_fwd(a, u, row)
        gel, _ = _gelu_and_grad(gt_ref[...])
        xa_ref[...] = xa
        r_ref[...] = r
        i_ref[...] = ig
        h_ref[...] = h
        ya_ref[...] = (gel * h).astype(ya_ref.dtype)

    out = pl.BlockSpec((seq, cw), lambda c, b: (b, c))
    f = jax.ShapeDtypeStruct((t_rows, chans), F32)
    return pl.pallas_call(
        body, name=name, grid=(nc, bl),
        in_specs=[slab(0), slab(nc), par(4), par(1), wbd, par(1), wbd, par(1), par(1)],
        out_specs=[out] * 5,
        out_shape=[f, f, f, f, jax.ShapeDtypeStruct((t_rows, chans), BF16)],
        compiler_params=_params(),
    )(proj, proj, p["conv_w"], p["conv_b"], p["wa"], p["ba"], p["wx"], p["bx"], p["lam"])


def _rg_bwd(name, proj, dya, saved, p, seq, chans):
    xa_s, r_s, i_s, h_s = saved
    t_rows = proj.shape[0]
    bl = t_rows // seq
    cw = LANE
    nc = chans // cw
    slab, par, wbd = _rg_specs(seq, cw)

    def body(x_ref, gt_ref, dya_ref, xa_ref, r_ref, i_ref, h_ref, cw_ref, wa_ref, wx_ref, lam_ref,
             dx_ref, dgt_ref, sm_ref, dwa_ref, dwx_ref):
        b = pl.program_id(1)
        row = lax.broadcasted_iota(jnp.int32, (seq, cw), 0)
        xv, xa, r, ig, h = x_ref[...], xa_ref[...], r_ref[...], i_ref[...], h_ref[...]
        dyav = dya_ref[...]
        gel, dgel = _gelu_and_grad(gt_ref[...])
        dgt_ref[...] = (dyav * h * dgel).astype(dgt_ref.dtype)
        dh = dyav * gel
        lam = lam_ref[...]
        sp = _softplus(-lam)
        log_a = (-RG_C) * r * sp
        a = jnp.exp(log_a)
        s = jnp.sqrt(-_expm1(2.0 * log_a))
        lamb = _scan_bwd(_shift_up(a, 1, row), dh, row)
        da = lamb * _shift_down(h, 1, row)
        xi = ig * xa
        ds = lamb * xi
        dxi = lamb * s
        dlog = da * a - ds * (a * a) / s
        dr = dlog * ((-RG_C) * sp)
        dsp = jnp.sum(dlog * ((-RG_C) * r), axis=0, keepdims=True)
        dlam = -dsp * _sigmoid(-lam)
        dzr = dr * r * (1.0 - r)
        dzi = (dxi * xa) * ig * (1.0 - ig)
        dzrb, dzib, xab = dzr.astype(BF16), dzi.astype(BF16), xa.astype(BF16)
        dxa = dxi * ig
        dxa = dxa + lax.dot_general(dzrb, wa_ref[...], (NT, ((), ())), preferred_element_type=F32)
        dxa = dxa + lax.dot_general(dzib, wx_ref[...], (NT, ((), ())), preferred_element_type=F32)
        dwa = lax.dot_general(xab, dzrb, (TN, ((), ())), preferred_element_type=F32)
        dwx = lax.dot_general(xab, dzib, (TN, ((), ())), preferred_element_type=F32)
        dxv = jnp.zeros_like(xv)
        rows = []
        for k in range(4):
            dxv = dxv + cw_ref[k:k + 1, :] * _shift_up(dxa, 3 - k, row)
            rows.append(jnp.sum(dxa * _shift_down(xv, 3 - k, row), axis=0, keepdims=True))
        dx_ref[...] = dxv.astype(dx_ref.dtype)
        rows += [jnp.sum(dxa, axis=0, keepdims=True), jnp.sum(dzr, axis=0, keepdims=True),
                 jnp.sum(dzi, axis=0, keepdims=True), dlam]

        @pl.when(b == 0)
        def _():
            sm_ref[...] = jnp.zeros_like(sm_ref)
            dwa_ref[...] = jnp.zeros_like(dwa_ref)
            dwx_ref[...] = jnp.zeros_like(dwx_ref)

        for k, val in enumerate(rows):
            sm_ref[k:k + 1, :] += val
        dwa_ref[...] += dwa
        dwx_ref[...] += dwx

    plain = pl.BlockSpec((seq, cw), lambda c, b: (b, c))
    return pl.pallas_call(
        body, name=name, grid=(nc, bl),
        in_specs=[slab(0), slab(nc), plain, plain, plain, plain, plain, par(4), wbd, wbd, par(1)],
        out_specs=[plain, plain, par(8), wbd, wbd],
        out_shape=[jax.ShapeDtypeStruct((t_rows, chans), BF16), jax.ShapeDtypeStruct((t_rows, chans), BF16),
                   jax.ShapeDtypeStruct((8, chans), F32),
                   jax.ShapeDtypeStruct((nc, cw, cw), F32), jax.ShapeDtypeStruct((nc, cw, cw), F32)],
        compiler_params=_params(),
    )(proj, proj, dya, xa_s, r_s, i_s, h_s, p["conv_w"], p["wa"], p["wx"], p["lam"])


ATT_Q_BLOCK = 256
ATT_K_BLOCK = 256
PAIR = LANE // HEAD_DIM
NEG = -1e30
SCALE = HEAD_DIM ** -0.5
assert math.log2(HEAD_DIM) % 2 == 0


def _att_blocks(seq):
    return _tile(seq, ATT_Q_BLOCK), _tile(seq, ATT_K_BLOCK)


def _key_blocks(qi, tq, bk):
    return (qi * tq) // bk, (qi * tq + tq - 1) // bk + 1


def _tri(n, kind):
    r = lax.broadcasted_iota(jnp.int32, (n, n), 0)
    c = lax.broadcasted_iota(jnp.int32, (n, n), 1)
    m = {"gt": r > c, "le": r <= c, "lt": r < c}[kind]
    return m.astype(BF16)


def _cumsum_mm(v, tri):
    hi = v.astype(BF16)
    lo = (v - hi.astype(F32)).astype(BF16)
    return jnp.dot(hi, tri, preferred_element_type=F32) + jnp.dot(lo, tri, preferred_element_type=F32)


def _head_masks():
    lane = lax.broadcasted_iota(jnp.int32, (1, LANE), 1)
    return [(lane >= h * HEAD_DIM) & (lane < (h + 1) * HEAD_DIM) for h in range(PAIR)]


def _only(mask, v):
    return jnp.where(mask, v, jnp.zeros_like(v))


def _att_specs(seq, blk, nq, off):
    npair = None
    qs = lambda o: pl.BlockSpec((blk, LANE), lambda b, p, i: (b * nq + i, o + p))
    ks = lambda o: pl.BlockSpec((seq, LANE), lambda b, p, i: (b, o + p))
    col = pl.BlockSpec((None, PAIR, blk, 1), lambda b, p, i: (b, p, i, 0))
    lane = pl.BlockSpec((None, PAIR, 1, seq), lambda b, p, i: (b, p, 0, 0))
    return qs, ks, col, lane


def _sb_fwd(name, qkv, off, width, bl, seq):
    t_rows = qkv.shape[0]
    tq, bk = _att_blocks(seq)
    nq = seq // tq
    nb = width // LANE
    qs, ks, col, _ = _att_specs(seq, tq, nq, off)

    def body(q_ref, k_ref, v_ref, o_ref, lt_ref):
        qi = pl.program_id(2)
        masks = _head_masks()
        qv = q_ref[...] * SCALE
        qh = [_only(m, qv) for m in masks]
        row = lax.broadcasted_iota(jnp.int32, (tq, bk), 0)
        cix = lax.broadcasted_iota(jnp.int32, (tq, bk), 1)
        tri = _tri(bk, "gt")

        def step(masked, top):
            def go(it, carry):
                acc, cls = carry
                kb = top - it
                ks_ = pl.multiple_of(kb * bk, bk)
                kv = k_ref[pl.ds(ks_, bk), :]
                vv = v_ref[pl.ds(ks_, bk), :]
                strict = (kb * bk + cix) < (qi * tq + row)
                new_cls = []
                for h in range(PAIR):
                    z = lax.dot_general(qh[h], kv, (NT, ((), ())), preferred_element_type=F32)
                    sp = _softplus(z)
                    lk = jnp.where(strict, -sp, 0.0) if masked else -sp
                    wgt = jnp.exp(z - sp + (cls[h] + _cumsum_mm(lk, tri)))
                    if masked:
                        wgt = jnp.where(strict, wgt, 0.0)
                    acc = acc + jnp.dot(wgt.astype(BF16), _only(masks[h], vv), preferred_element_type=F32)
                    new_cls.append(cls[h] + jnp.sum(lk, axis=1, keepdims=True))
                return acc, tuple(new_cls)
            return go

        n_full, n_all = _key_blocks(qi, tq, bk)
        zero = jnp.zeros((tq, 1), F32)
        carry = (jnp.zeros((tq, LANE), F32), (zero,) * PAIR)
        carry = lax.fori_loop(0, n_all - n_full, step(True, n_all - 1), carry)
        acc, cls = lax.fori_loop(0, n_full, step(False, n_full - 1), carry)
        o_ref[...] = acc.astype(o_ref.dtype)
        for h in range(PAIR):
            lt_ref[h] = cls[h]

    return pl.pallas_call(
        body, name=name, grid=(bl, nb, nq), in_specs=[qs(off), ks(off + nb), ks(off + 2 * nb)],
        out_specs=[qs(0), col],
        out_shape=[jax.ShapeDtypeStruct((t_rows, width), BF16),
                   jax.ShapeDtypeStruct((bl, nb * PAIR, seq, 1), F32)],
        compiler_params=_params(),
    )(qkv, qkv, qkv)


def _sb_bwd(name, qkv, off, width, bl, seq, ltot, do):
    t_rows = qkv.shape[0]
    tq, bk = _att_blocks(seq)
    nq = seq // tq
    nb = width // LANE
    qs, ks, col, _ = _att_specs(seq, tq, nq, off)

    def body(q_ref, k_ref, v_ref, lt_ref, do_ref, dq_ref, dk_ref, dv_ref, dk_acc, dv_acc):
        qi = pl.program_id(2)

        @pl.when(qi == 0)
        def _():
            dk_acc[...] = jnp.zeros_like(dk_acc)
            dv_acc[...] = jnp.zeros_like(dv_acc)

        masks = _head_masks()
        qv = q_ref[...] * SCALE
        dov = do_ref[...].astype(BF16)
        qh = [_only(m, qv) for m in masks]
        doh = [_only(m, dov) for m in masks]
        ltv = [lt_ref[h] for h in range(PAIR)]
        row = lax.broadcasted_iota(jnp.int32, (tq, bk), 0)
        cix = lax.broadcasted_iota(jnp.int32, (tq, bk), 1)
        tri_le = _tri(bk, "le")
        tri_lt = _tri(bk, "lt")

        def step(masked):
            def go(kb, carry):
                dq, cls, ces = carry
                ks_ = pl.multiple_of(kb * bk, bk)
                kv = k_ref[pl.ds(ks_, bk), :]
                vv = v_ref[pl.ds(ks_, bk), :]
                kvs = kv * SCALE
                strict = (kb * bk + cix) < (qi * tq + row)
                dk_new = jnp.zeros((bk, LANE), F32)
                dv_new = jnp.zeros((bk, LANE), F32)
                new_cls, new_ces = [], []
                for h in range(PAIR):
                    z = lax.dot_general(qh[h], kv, (NT, ((), ())), preferred_element_type=F32)
                    sp = _softplus(z)
                    lk = jnp.where(strict, -sp, 0.0) if masked else -sp
                    sig = jnp.exp(z - sp)
                    wgt = sig * jnp.exp(ltv[h] - cls[h] - _cumsum_mm(lk, tri_le))
                    if masked:
                        wgt = jnp.where(strict, wgt, 0.0)
                    dw = lax.dot_general(doh[h], vv, (NT, ((), ())), preferred_element_type=F32)
                    e = dw * wgt
                    pre = ces[h] + _cumsum_mm(e, tri_lt)
                    dz = e * (1.0 - sig) - pre * sig
                    if masked:
                        dz = jnp.where(strict, dz, 0.0)
                    dzb = dz.astype(BF16)
                    dq = dq + jnp.dot(dzb, _only(masks[h], kvs), preferred_element_type=F32)
                    dk_new = dk_new + lax.dot_general(dzb, qh[h], (TN, ((), ())), preferred_element_type=F32)
                    dv_new = dv_new + lax.dot_general(wgt.astype(BF16), doh[h], (TN, ((), ())),
                                                      preferred_element_type=F32)
                    new_cls.append(cls[h] + jnp.sum(lk, axis=1, keepdims=True))
                    new_ces.append(ces[h] + jnp.sum(e, axis=1, keepdims=True))
                dk_acc[pl.ds(ks_, bk), :] += dk_new
                dv_acc[pl.ds(ks_, bk), :] += dv_new
                return dq, tuple(new_cls), tuple(new_ces)
            return go

        n_full, n_all = _key_blocks(qi, tq, bk)
        zero = (jnp.zeros((tq, 1), F32),) * PAIR
        carry = lax.fori_loop(0, n_full, step(False), (jnp.zeros((tq, LANE), F32), zero, zero))
        dq, _, _ = lax.fori_loop(n_full, n_all, step(True), carry)
        dq_ref[...] = dq.astype(dq_ref.dtype)

        @pl.when(qi == nq - 1)
        def _():
            dk_ref[...] = dk_acc[...].astype(dk_ref.dtype)
            dv_ref[...] = dv_acc[...].astype(dv_ref.dtype)

    o = jax.ShapeDtypeStruct((t_rows, width), BF16)
    return pl.pallas_call(
        body, name=name, grid=(bl, nb, nq),
        in_specs=[qs(off), ks(off + nb), ks(off + 2 * nb), col, qs(0)], out_specs=[qs(0), ks(0), ks(0)],
        out_shape=[o, o, o], scratch_shapes=[pltpu.VMEM((seq, LANE), F32), pltpu.VMEM((seq, LANE), F32)],
        compiler_params=_params(),
    )(qkv, qkv, qkv, ltot, do)


def _fox_fwd(name, qkv, off, width, bl, seq, cum_q, cum_k):
    t_rows = qkv.shape[0]
    tq, bk = _att_blocks(seq)
    nq = seq // tq
    nb = width // LANE
    qs, ks, col, lane = _att_specs(seq, tq, nq, off)

    def body(q_ref, k_ref, v_ref, cq_ref, ck_ref, ob_ref, of_ref, lse_ref):
        qi = pl.program_id(2)
        masks = _head_masks()
        qv = q_ref[...] * SCALE
        qh = [_only(m, qv) for m in masks]
        cq = [cq_ref[h] for h in range(PAIR)]
        row = lax.broadcasted_iota(jnp.int32, (tq, bk), 0)
        cix = lax.broadcasted_iota(jnp.int32, (tq, bk), 1)

        def step(masked):
            def go(kb, carry):
                ms, ls, accs = carry
                ks_ = pl.multiple_of(kb * bk, bk)
                kv = k_ref[pl.ds(ks_, bk), :]
                vv = v_ref[pl.ds(ks_, bk), :]
                nm, nl, na = [], [], []
                for h in range(PAIR):
                    z = lax.dot_general(qh[h], kv, (NT, ((), ())), preferred_element_type=F32)
                    z = z + cq[h] - ck_ref[h, :, pl.ds(ks_, bk)]
                    if masked:
                        z = jnp.where((kb * bk + cix) <= (qi * tq + row), z, NEG)
                    m_new = jnp.maximum(ms[h], jnp.max(z, axis=1, keepdims=True))
                    pv = jnp.exp(z - m_new)
                    alpha = jnp.exp(ms[h] - m_new)
                    nm.append(m_new)
                    nl.append(alpha * ls[h] + jnp.sum(pv, axis=1, keepdims=True))
                    na.append(alpha * accs[h] + jnp.dot(pv.astype(BF16), _only(masks[h], vv),
                                                        preferred_element_type=F32))
                return tuple(nm), tuple(nl), tuple(na)
            return go

        n_full, n_all = _key_blocks(qi, tq, bk)
        init = ((jnp.full((tq, 1), NEG, F32),) * PAIR, (jnp.zeros((tq, 1), F32),) * PAIR,
                (jnp.zeros((tq, LANE), F32),) * PAIR)
        carry = lax.fori_loop(0, n_full, step(False), init)
        ms, ls, accs = lax.fori_loop(n_full, n_all, step(True), carry)
        out = accs[0] / ls[0]
        for h in range(1, PAIR):
            out = out + accs[h] / ls[h]
        ob_ref[...] = out.astype(ob_ref.dtype)
        of_ref[...] = out
        for h in range(PAIR):
            lse_ref[h] = ms[h] + jnp.log(ls[h])

    return pl.pallas_call(
        body, name=name, grid=(bl, nb, nq),
        in_specs=[qs(off), ks(off + nb), ks(off + 2 * nb), col, lane], out_specs=[qs(0), qs(0), col],
        out_shape=[jax.ShapeDtypeStruct((t_rows, width), BF16), jax.ShapeDtypeStruct((t_rows, width), F32),
                   jax.ShapeDtypeStruct((bl, nb * PAIR, seq, 1), F32)],
        compiler_params=_params(),
    )(qkv, qkv, qkv, cum_q, cum_k)


def _fox_bwd(name, qkv, off, width, bl, seq, cum_q, cum_k, lse, o, do):
    t_rows = qkv.shape[0]
    tq, bk = _att_blocks(seq)
    nq = seq // tq
    nb = width // LANE
    qs, ks, col, lane = _att_specs(seq, tq, nq, off)

    def body(q_ref, k_ref, v_ref, cq_ref, ck_ref, lse_ref, o_ref, do_ref,
             dq_ref, dk_ref, dv_ref, dcq_ref, dck_ref, dk_acc, dv_acc):
        qi = pl.program_id(2)

        @pl.when(qi == 0)
        def _():
            dk_acc[...] = jnp.zeros_like(dk_acc)
            dv_acc[...] = jnp.zeros_like(dv_acc)
            dck_ref[...] = jnp.zeros_like(dck_ref)

        masks = _head_masks()
        qv = q_ref[...] * SCALE
        dof = do_ref[...]
        dov = dof.astype(BF16)
        prod = dof * o_ref[...]
        qh = [_only(m, qv) for m in masks]
        doh = [_only(m, dov) for m in masks]
        delta = [jnp.sum(_only(m, prod), axis=1, keepdims=True) for m in masks]
        shift = [cq_ref[h] - lse_ref[h] for h in range(PAIR)]
        row = lax.broadcasted_iota(jnp.int32, (tq, bk), 0)
        cix = lax.broadcasted_iota(jnp.int32, (tq, bk), 1)

        def step(masked):
            def go(kb, carry):
                dq, dcqs = carry
                ks_ = pl.multiple_of(kb * bk, bk)
                kv = k_ref[pl.ds(ks_, bk), :]
                vv = v_ref[pl.ds(ks_, bk), :]
                kvs = kv * SCALE
                dk_new = jnp.zeros((bk, LANE), F32)
                dv_new = jnp.zeros((bk, LANE), F32)
                new_dcq = []
                for h in range(PAIR):
                    z = lax.dot_general(qh[h], kv, (NT, ((), ())), preferred_element_type=F32)
                    pv = jnp.exp(z + shift[h] - ck_ref[h, :, pl.ds(ks_, bk)])
                    if masked:
                        pv = jnp.where((kb * bk + cix) <= (qi * tq + row), pv, 0.0)
                    dp = lax.dot_general(doh[h], vv, (NT, ((), ())), preferred_element_type=F32)
                    ds = pv * (dp - delta[h])
                    dsb = ds.astype(BF16)
                    dq = dq + jnp.dot(dsb, _only(masks[h], kvs), preferred_element_type=F32)
                    dk_new = dk_new + lax.dot_general(dsb, qh[h], (TN, ((), ())), preferred_element_type=F32)
                    dv_new = dv_new + lax.dot_general(pv.astype(BF16), doh[h], (TN, ((), ())),
                                                      preferred_element_type=F32)
                    dck_ref[h, :, pl.ds(ks_, bk)] += -jnp.sum(ds, axis=0, keepdims=True)
                    new_dcq.append(dcqs[h] + jnp.sum(ds, axis=1, keepdims=True))
                dk_acc[pl.ds(ks_, bk), :] += dk_new
                dv_acc[pl.ds(ks_, bk), :] += dv_new
                return dq, tuple(new_dcq)
            return go

        n_full, n_all = _key_blocks(qi, tq, bk)
        zero = (jnp.zeros((tq, 1), F32),) * PAIR
        carry = lax.fori_loop(0, n_full, step(False), (jnp.zeros((tq, LANE), F32), zero))
        dq, dcqs = lax.fori_loop(n_full, n_all, step(True), carry)
        dq_ref[...] = dq.astype(dq_ref.dtype)
        for h in range(PAIR):
            dcq_ref[h] = dcqs[h]

        @pl.when(qi == nq - 1)
        def _():
            dk_ref[...] = dk_acc[...].astype(dk_ref.dtype)
            dv_ref[...] = dv_acc[...].astype(dv_ref.dtype)

    ob = jax.ShapeDtypeStruct((t_rows, width), BF16)
    nh = nb * PAIR
    return pl.pallas_call(
        body, name=name, grid=(bl, nb, nq),
        in_specs=[qs(off), ks(off + nb), ks(off + 2 * nb), col, lane, col, qs(0), qs(0)],
        out_specs=[qs(0), ks(0), ks(0), col, lane],
        out_shape=[ob, ob, ob, jax.ShapeDtypeStruct((bl, nh, seq, 1), F32), jax.ShapeDtypeStruct((bl, nh, 1, seq), F32)],
        scratch_shapes=[pltpu.VMEM((seq, LANE), F32), pltpu.VMEM((seq, LANE), F32)],
        compiler_params=_params(),
    )(qkv, qkv, qkv, cum_q, cum_k, lse, o, do)


def _lane_cumsum(v, reverse):
    n = v.shape[1]
    cix = lax.broadcasted_iota(jnp.int32, v.shape, 1)
    s = 1
    while s < n:
        if reverse:
            v = v + jnp.where(cix < n - s, pltpu.roll(v, n - s, 1), 0.0)
        else:
            v = v + jnp.where(cix >= s, pltpu.roll(v, s, 1), 0.0)
        s *= 2
    return v


def _forget_cum(name, fl, bf):
    def body(fl_ref, bf_ref, o_ref):
        xv = fl_ref[...] + bf_ref[...]
        o_ref[...] = _lane_cumsum(-_softplus(-xv), False)

    return pl.pallas_call(body, name=name, out_shape=jax.ShapeDtypeStruct(fl.shape, F32),
                          compiler_params=_params())(fl, bf)


def _forget_cum_bwd(name, fl, bf, dcum, nh):
    rows = fl.shape[0]

    def body(fl_ref, bf_ref, dc_ref, dfl_ref, dbf_ref):
        xv = fl_ref[...] + bf_ref[...]
        dlogf = _lane_cumsum(dc_ref[...], True)
        dfl = dlogf * _sigmoid(-xv)
        dfl_ref[...] = dfl
        per_row = jnp.sum(dfl, axis=1, keepdims=True)
        tot = per_row[0:nh]
        for b in range(1, rows // nh):
            tot = tot + per_row[b * nh:(b + 1) * nh]
        dbf_ref[...] = tot

    return pl.pallas_call(
        body, name=name,
        out_shape=[jax.ShapeDtypeStruct(fl.shape, F32), jax.ShapeDtypeStruct((nh, 1), F32)],
        compiler_params=_params(),
    )(fl, bf, dcum)


def _merge_fwd(name, proj, off, merge_b, pa, pb, pc):
    t_rows, d = pa.shape
    tm = _tile(t_rows, 256)

    def body(l0, l1, l2, mb, a_ref, b_ref, c_ref, o_ref):
        g0 = _sigmoid(l0[...] + mb[:, 0:d])
        g1 = _sigmoid(l1[...] + mb[:, d:2 * d])
        g2 = _sigmoid(l2[...] + mb[:, 2 * d:3 * d])
        o_ref[...] = (g0 * a_ref[...] + g1 * b_ref[...] + g2 * c_ref[...]).astype(o_ref.dtype)

    row = pl.BlockSpec((tm, d), lambda i: (i, 0))
    lg = lambda j: pl.BlockSpec((tm, d), lambda i: (i, off + j))
    return pl.pallas_call(
        body, name=name, grid=(t_rows // tm,),
        in_specs=[lg(0), lg(1), lg(2), pl.BlockSpec((1, 3 * d), lambda i: (0, 0)), row, row, row],
        out_specs=row, out_shape=jax.ShapeDtypeStruct((t_rows, d), BF16), compiler_params=_params(),
    )(proj, proj, proj, merge_b, pa, pb, pc)


def _merge_bwd(name, proj, off, merge_b, pa, pb, pc, dmixed):
    t_rows, d = pa.shape
    tm = _tile(t_rows, 256)

    def body(l0, l1, l2, mb, a_ref, b_ref, c_ref, dm_ref, da_ref, db_ref, dc_ref, dl_ref, dmb_ref):
        i = pl.program_id(0)
        dm = dm_ref[...]
        parts = []
        for j, (lref, pref, dref) in enumerate(((l0, a_ref, da_ref), (l1, b_ref, db_ref), (l2, c_ref, dc_ref))):
            g = _sigmoid(lref[...] + mb[:, j * d:(j + 1) * d])
            dref[...] = (g * dm).astype(dref.dtype)
            dl = dm * pref[...] * g * (1.0 - g)
            dl_ref[:, j * d:(j + 1) * d] = dl.astype(dl_ref.dtype)
            parts.append(jnp.sum(dl, axis=0, keepdims=True))
        tot = jnp.concatenate(parts, axis=1)

        @pl.when(i == 0)
        def _():
            dmb_ref[...] = tot

        @pl.when(i > 0)
        def _():
            dmb_ref[...] += tot

    row = pl.BlockSpec((tm, d), lambda i: (i, 0))
    lg = lambda j: pl.BlockSpec((tm, d), lambda i: (i, off + j))
    one = pl.BlockSpec((1, 3 * d), lambda i: (0, 0))
    b16 = jax.ShapeDtypeStruct((t_rows, d), BF16)
    return pl.pallas_call(
        body, name=name, grid=(t_rows // tm,),
        in_specs=[lg(0), lg(1), lg(2), one, row, row, row, row],
        out_specs=[row, row, row, pl.BlockSpec((tm, 3 * d), lambda i: (i, 0)), one],
        out_shape=[b16, b16, b16, jax.ShapeDtypeStruct((t_rows, 3 * d), BF16), jax.ShapeDtypeStruct((1, 3 * d), F32)],
        compiler_params=_params(),
    )(proj, proj, proj, merge_b, pa, pb, pc, dmixed)


def _grouped_nn(name, a, wg, l, out_dtype):
    t_rows, kk = a.shape
    ng, ncol = wg.shape[0], wg.shape[3]
    tm = _tile(t_rows, 512)
    return _mmk(name, [a, wg],
                [pl.BlockSpec((tm, kk), lambda i, g: (i, 0)),
                 pl.BlockSpec((None, None, kk, ncol), lambda i, g: (g, l, 0, 0))],
                [(0, ALL, 1, ALL, NN)], (t_rows, ng * ncol), out_dtype, (t_rows // tm, ng),
                pl.BlockSpec((tm, ncol), lambda i, g: (i, g)))


def _grouped_nt(name, da, wg, l, out_dtype):
    t_rows = da.shape[0]
    ng, kk, ncol = wg.shape[0], wg.shape[2], wg.shape[3]
    tm = _tile(t_rows, 512)
    return _mmk(name, [da, wg],
                [pl.BlockSpec((tm, ng * ncol), lambda i: (i, 0)),
                 pl.BlockSpec((ng, None, kk, ncol), lambda i: (0, l, 0, 0))],
                [(0, (ALL, slice(g * ncol, (g + 1) * ncol)), 1, g, NT) for g in range(ng)],
                (t_rows, kk), out_dtype, (t_rows // tm,), pl.BlockSpec((tm, kk), lambda i: (i, 0)))


def _grouped_tn(name, a, da, ng, out_dtype):
    t_rows, kk = a.shape
    ncol = da.shape[1] // ng
    return _mmk(name, [a, da],
                [pl.BlockSpec((t_rows, kk), lambda g: (0, 0)), pl.BlockSpec((t_rows, ncol), lambda g: (0, g))],
                [(0, ALL, 1, ALL, TN)], (ng, kk, ncol), out_dtype, (ng,),
                pl.BlockSpec((None, kk, ncol), lambda g: (g, 0, 0)))


def _rows_nn(name, a, wr, l, out_dtype):
    t_rows = a.shape[0]
    ng, kg, n = wr.shape[0], wr.shape[2], wr.shape[3]
    tm, tn = _tile(t_rows, 512), _tile(n, 512)
    return _mmk(name, [a, wr],
                [pl.BlockSpec((tm, ng * kg), lambda i, j: (i, 0)),
                 pl.BlockSpec((ng, None, kg, tn), lambda i, j: (0, l, 0, j))],
                [(0, (ALL, slice(g * kg, (g + 1) * kg)), 1, g, NN) for g in range(ng)],
                (t_rows, n), out_dtype, (t_rows // tm, n // tn), pl.BlockSpec((tm, tn), lambda i, j: (i, j)))


def _rows_nt(name, dy, wr, l, out_dtype):
    t_rows, n = dy.shape
    ng, kg = wr.shape[0], wr.shape[2]
    tm = _tile(t_rows, 512)
    return _mmk(name, [dy, wr],
                [pl.BlockSpec((tm, n), lambda i, g: (i, 0)),
                 pl.BlockSpec((None, None, kg, n), lambda i, g: (g, l, 0, 0))],
                [(0, ALL, 1, ALL, NT)], (t_rows, ng * kg), out_dtype, (t_rows // tm, ng),
                pl.BlockSpec((tm, kg), lambda i, g: (i, g)))


def _rows_tn(name, a, dy, ng, out_dtype):
    t_rows, n = dy.shape
    kg = a.shape[1] // ng
    tn = _tile(n, 512)
    return _mmk(name, [a, dy],
                [pl.BlockSpec((t_rows, kg), lambda g, j: (0, g)), pl.BlockSpec((t_rows, tn), lambda g, j: (0, j))],
                [(0, ALL, 1, ALL, TN)], (ng, kg, n), out_dtype, (ng, n // tn),
                pl.BlockSpec((None, kg, tn), lambda g, j: (g, 0, j)))


def _mix_fwd(tag, w, l, x, mod, seq):
    t_rows, d = x.shape
    bl = t_rows // seq
    shift, scale, gate = mod
    chans, nh = w["layout"]["chans"], w["layout"]["heads"]
    width = nh * HEAD_DIM
    nb = width // LANE
    h = _normmod(tag + "_norm", x, w["mix_norm"][l][None], shift, scale, seq)
    proj = _mm(tag + "_in_a", h, w["w_a"][l], NN, F32)
    qkv = _mm(tag + "_in_b", h, w["w_b"][l], NN, BF16)
    flp = _mm(tag + "_in_f", h, w["w_f"][l], NN, F32)
    xa, r, ig, hs, ya = _rg_fwd(tag + "_rg", proj, w["rg"][l], seq, chans)
    yb, ltot = _sb_fwd(tag + "_sb", qkv, 0, width, bl, seq)
    fl = flp[:, :nh].reshape(bl, seq, nh).transpose(0, 2, 1).reshape(bl * nh, seq)
    bf = jnp.tile(w["fox_bf"][l].reshape(nh, 1), (bl, 1))
    cum = _forget_cum(tag + "_cum", fl, bf)
    cum_q = cum.reshape(bl, nh, seq, 1)
    cum_k = cum.reshape(bl, nh, 1, seq)
    yc, oc, lse = _fox_fwd(tag + "_fox", qkv, 3 * nb, width, bl, seq, cum_q, cum_k)
    pa = _rows_nn(tag + "_prg", ya, w["w_rg"], l, F32)
    pb = _grouped_nn(tag + "_psb", yb, w["w_sb"], l, F32)
    pc = _grouped_nn(tag + "_pfox", yc, w["w_fox"], l, F32)
    moff = 2 * chans // d
    mb = w["merge_b"][l][None]
    mixed = _merge_fwd(tag + "_merge", proj, moff, mb, pa, pb, pc)
    y = _rows_nn(tag + "_out", mixed, w["w_o"], l, F32)
    xn = _resid(tag + "_res", x, y, gate, 1.0, seq)
    saved = dict(x=x, h=h, proj=proj, qkv=qkv, rg=(xa, r, ig, hs), ya=ya, ltot=ltot,
                 fox=(cum_q, cum_k, lse, oc), fl=fl, bf=bf, yb=yb, yc=yc, pa=pa, pb=pb, pc=pc, mixed=mixed, y=y)
    return xn, saved


def _mix_bwd(tag, w, l, s, mod, dxo, seq):
    x = s["x"]
    t_rows, d = x.shape
    bl = t_rows // seq
    shift, scale, gate = mod
    chans, nh = w["layout"]["chans"], w["layout"]["heads"]
    width = nh * HEAD_DIM
    nb = width // LANE
    moff = 2 * chans // d
    mb = w["merge_b"][l][None]
    ng = NUM_CHIPS
    dy, dgate = _resid_bwd(tag + "_res_bwd", dxo, s["y"], gate, 1.0, seq)
    dmixed = _rows_nt(tag + "_out_dx", dy, w["w_o"], l, F32)
    dw_o = _rows_tn(tag + "_out_dw", s["mixed"], dy, ng, BF16)
    dpa, dpb, dpc, dlog, dmb = _merge_bwd(tag + "_merge_bwd", s["proj"], moff, mb, s["pa"], s["pb"], s["pc"], dmixed)
    dya = _rows_nt(tag + "_prg_dx", dpa, w["w_rg"], l, F32)
    dw_rg = _rows_tn(tag + "_prg_dw", s["ya"], dpa, ng, BF16)
    dyb = _grouped_nt(tag + "_psb_dx", dpb, w["w_sb"], l, F32)
    dw_sb = _grouped_tn(tag + "_psb_dw", s["yb"], dpb, ng, BF16)
    dyc = _grouped_nt(tag + "_pfox_dx", dpc, w["w_fox"], l, F32)
    dw_fox = _grouped_tn(tag + "_pfox_dw", s["yc"], dpc, ng, BF16)
    qkv = s["qkv"]
    dq_b, dk_b, dv_b = _sb_bwd(tag + "_sb_bwd", qkv, 0, width, bl, seq, s["ltot"], dyb)
    cum_q, cum_k, lse, oc = s["fox"]
    dq_c, dk_c, dv_c, dcq, dck = _fox_bwd(tag + "_fox_bwd", qkv, 3 * nb, width, bl, seq, cum_q, cum_k, lse, oc, dyc)
    dcum = dcq.reshape(bl * nh, seq) + dck.reshape(bl * nh, seq)
    dfl, dbf = _forget_cum_bwd(tag + "_cum_bwd", s["fl"], s["bf"], dcum, nh)
    dfl_t = dfl.reshape(bl, nh, seq).transpose(0, 2, 1).reshape(t_rows, nh)
    dflp = jnp.pad(dfl_t, ((0, 0), (0, LANE - nh))).astype(BF16)
    drgx, dgt, rg_small, dwa, dwx = _rg_bwd(tag + "_rg_bwd", s["proj"], dya, s["rg"], w["rg"][l], seq, chans)
    dproj = jnp.concatenate([drgx, dgt, dlog], axis=1)
    dqkv = jnp.concatenate([dq_b, dk_b, dv_b, dq_c, dk_c, dv_c], axis=1)
    w_a, w_b, w_f = w["w_a"][l], w["w_b"][l], w["w_f"][l]
    pa_w, pb_w = w_a.shape[1], w_b.shape[1]
    tm, tn = _tile(t_rows, 512), _tile(d, 512)
    rows = lambda n: pl.BlockSpec((tm, n), lambda i, j: (i, 0))
    wrow = lambda n: pl.BlockSpec((tn, n), lambda i, j: (j, 0))
    dh = _mmk(tag + "_in_dx", [dproj, dqkv, dflp, w_a, w_b, w_f],
              [rows(pa_w), rows(pb_w), rows(LANE), wrow(pa_w), wrow(pb_w), wrow(LANE)],
              [(0, ALL, 3, ALL, NT), (1, ALL, 4, ALL, NT), (2, ALL, 5, ALL, NT)],
              (t_rows, d), F32, (t_rows // tm, d // tn), pl.BlockSpec((tm, tn), lambda i, j: (i, j)))
    hb = s["h"]
    dw_a = _mm(tag + "_in_a_dw", hb, dproj, TN, BF16)
    dw_b = _mm(tag + "_in_b_dw", hb, dqkv, TN, BF16)
    dw_f = _mm(tag + "_in_f_dw", hb, dflp, TN, BF16)
    dx, dshift, dscale, dgain = _normmod_bwd(tag + "_norm_bwd", x, dh, dxo, w["mix_norm"][l][None], scale, seq)
    grads = dict(w_in=(dw_a, dw_b, dw_f), w_rg=dw_rg, w_sb=dw_sb, w_fox=dw_fox, w_o=dw_o, mix_norm=dgain,
                 rg_small=rg_small, rg_dwa=dwa, rg_dwx=dwx, fox_bf=dbf, merge_b=dmb)
    return dx, (dshift, dscale, dgate), grads


def _silu(name, c):
    def body(c_ref, o_ref):
        v = c_ref[...]
        o_ref[...] = v * _sigmoid(v)

    return pl.pallas_call(body, name=name, out_shape=jax.ShapeDtypeStruct(c.shape, F32),
                          compiler_params=_params())(c)


def _blockdiag(wb):
    nb, bd, _ = wb.shape
    per = LANE // bd
    t = wb.reshape(nb // per, per, bd, 1, bd)
    eye = jnp.eye(per, dtype=wb.dtype).reshape(1, per, 1, per, 1)
    return (t * eye).reshape(nb // per, LANE, LANE).astype(BF16)


def _unblockdiag(t, bd):
    n = t.shape[0]
    per = LANE // bd
    t5 = t.reshape(n, per, bd, per, bd)
    return jnp.stack([t5[:, p, :, p, :] for p in range(per)], axis=1).reshape(n * per, bd, bd)


def _prepare(gw, a, d, chans, nh):
    depth = a["ada_b"].shape[0]
    wq = 3 * nh * HEAD_DIM
    o_m = 2 * chans + 2 * wq
    w = {"layout": dict(chans=chans, heads=nh)}
    for n in ("ffn1_w1", "ffn1_w3", "ffn1_w2", "ffn2_w1", "ffn2_w3", "ffn2_w2", "w_rg", "w_sb", "w_fox", "w_o"):
        w[n] = gw[n]
    for n in ("ffn1_norm", "ffn2_norm", "mix_norm", "fox_bf", "merge_b", "final_norm"):
        w[n] = a[n]
    w_a, w_b, w_f, rg = [], [], [], []
    for l in range(depth):
        full = gw["w_in"][:, l].transpose(1, 0, 2).reshape(d, -1)
        w_a.append(jnp.concatenate([full[:, :2 * chans], full[:, o_m + nh:]], axis=1))
        w_b.append(full[:, 2 * chans:o_m])
        w_f.append(jnp.pad(full[:, o_m:o_m + nh], ((0, 0), (0, LANE - nh))))
        conv_w = gw["conv_w"][:, l].transpose(1, 0, 2).reshape(-1, chans)
        rg.append(dict(conv_w=conv_w, conv_b=a["conv_b"][l][None], ba=a["rg_ba"][l][None], bx=a["rg_bx"][l][None],
                       lam=a["rg_lam"][l][None], wa=_blockdiag(a["rg_wa"][l]), wx=_blockdiag(a["rg_wx"][l])))
    w["w_a"], w["w_b"], w["w_f"], w["rg"] = w_a, w_b, w_f, rg
    return w


def _local_step(w, x, tgt, mods, fm):
    bl, seq, d = x.shape
    t_rows = bl * seq
    depth = len(mods)
    mod3 = []
    for l in range(depth):
        m4 = mods[l].reshape(bl, 9, 1, d)
        mod3.append([(m4[:, 3 * k], m4[:, 3 * k + 1], m4[:, 3 * k + 2]) for k in range(3)])
    fm4 = fm.reshape(bl, 2, 1, d)
    saved = []
    xc = x.reshape(t_rows, d)
    for l in range(depth):
        xc, s1 = _ffn_fwd(f"l{l}_ffn1", w, l, "ffn1_", xc, mod3[l][0], seq)
        xc, s2 = _mix_fwd(f"l{l}_mix", w, l, xc, mod3[l][1], seq)
        xc, s3 = _ffn_fwd(f"l{l}_ffn2", w, l, "ffn2_", xc, mod3[l][2], seq)
        saved.append((s1, s2, s3))
    lpart, dx, dfs, dfc, dfg = _final_loss("final", xc, tgt.reshape(t_rows, d), w["final_norm"][None],
                                           fm4[:, 0], fm4[:, 1], seq)
    loss = jnp.sum(lpart[:, 0, 0])
    grads = {"final_norm": dfg, "layers": [None] * depth}
    dmods = [None] * depth
    for l in reversed(range(depth)):
        s1, s2, s3 = saved[l]
        dx, dm3, g3 = _ffn_bwd(f"l{l}_ffn2", w, l, "ffn2_", s3, mod3[l][2], dx, seq)
        dx, dm2, g2 = _mix_bwd(f"l{l}_mix", w, l, s2, mod3[l][1], dx, seq)
        dx, dm1, g1 = _ffn_bwd(f"l{l}_ffn1", w, l, "ffn1_", s1, mod3[l][0], dx, seq)
        dmods[l] = jnp.concatenate([*dm1, *dm2, *dm3], axis=1).reshape(bl, 9 * d)
        grads["layers"][l] = {**g1, **g2, **g3}
    dfm = jnp.concatenate([dfs, dfc], axis=1).reshape(bl, 2 * d)
    return loss, dx.reshape(bl, seq, d), grads, dmods, dfm


def _mesh_pos():
    return lax.axis_index("x"), lax.axis_index("y"), lax.axis_index("c")


def _other_chips(x, y):
    return ((1 - x, y), (x, 1 - y), (1 - x, 1 - y))


def _gather_two_level(name, arrs):
    n = len(arrs)

    def body(*refs):
        ins, outs = refs[:n], refs[n:2 * n]
        send, recv, send2, recv2, send3, recv3 = refs[2 * n:]
        x, y, c = _mesh_pos()
        me = 2 * x + y
        chips = _other_chips(x, y)
        sib = (x, y, 1 - c)
        own = [pltpu.make_async_remote_copy(
            src_ref=ins[i], dst_ref=outs[i].at[me], send_sem=send3.at[i], recv_sem=recv3.at[i],
            device_id=sib, device_id_type=MESH) for i in range(n)]
        first = []
        for j, (px, py) in enumerate(chips):
            for i in range(n):
                first.append(pltpu.make_async_remote_copy(
                    src_ref=ins[i].at[c], dst_ref=outs[i].at[me, c], send_sem=send.at[j * n + i],
                    recv_sem=recv.at[j * n + i], device_id=(px, py, c), device_id_type=MESH))
        for cp in first + own:
            cp.start()
        passed = []
        for j, (px, py) in enumerate(chips):
            for i in range(n):
                landed = outs[i].at[2 * px + py, c]
                pltpu.make_async_remote_copy(
                    src_ref=ins[i].at[c], dst_ref=landed, send_sem=send.at[j * n + i],
                    recv_sem=recv.at[j * n + i], device_id=(px, py, c), device_id_type=MESH).wait_recv()
                fwd = pltpu.make_async_remote_copy(
                    src_ref=landed, dst_ref=landed, send_sem=send2.at[j * n + i],
                    recv_sem=recv2.at[j * n + i], device_id=sib, device_id_type=MESH)
                fwd.start()
                passed.append(fwd)
        for j, (px, py) in enumerate(chips):
            for i in range(n):
                theirs = outs[i].at[2 * px + py, 1 - c]
                pltpu.make_async_remote_copy(
                    src_ref=theirs, dst_ref=theirs, send_sem=send2.at[j * n + i],
                    recv_sem=recv2.at[j * n + i], device_id=sib, device_id_type=MESH).wait_recv()
        for cp in first + passed:
            cp.wait_send()
        for cp in own:
            cp.wait()

    return pl.pallas_call(
        body, name=name, in_specs=[ANY] * n, out_specs=[ANY] * n,
        out_shape=[jax.ShapeDtypeStruct((NUM_CHIPS,) + a.shape, a.dtype) for a in arrs],
        scratch_shapes=[pltpu.SemaphoreType.DMA((3 * n,)), pltpu.SemaphoreType.DMA((3 * n,)),
                        pltpu.SemaphoreType.DMA((3 * n,)), pltpu.SemaphoreType.DMA((3 * n,)),
                        pltpu.SemaphoreType.DMA((n,)), pltpu.SemaphoreType.DMA((n,))],
    )(*arrs)


def _split_to_sibling(name, arrs):
    n = len(arrs)
    slabs = arrs[0].shape[0]

    def body(*refs):
        ins, theirs = refs[:n], refs[n:2 * n]
        send, recv = refs[2 * n:]
        x, y, c = _mesh_pos()
        sib = (x, y, 1 - c)
        for i in range(n):
            for s in range(slabs):
                pltpu.make_async_remote_copy(
                    src_ref=ins[i].at[s, 1 - c], dst_ref=theirs[i].at[s], send_sem=send.at[i],
                    recv_sem=recv.at[i], device_id=sib, device_id_type=MESH).start()
        for i in range(n):
            pltpu.make_async_remote_copy(
                src_ref=ins[i].at[:, 0], dst_ref=theirs[i], send_sem=send.at[i], recv_sem=recv.at[i],
                device_id=sib, device_id_type=MESH).wait()

    return pl.pallas_call(
        body, name=name, in_specs=[ANY] * n, out_specs=[ANY] * n,
        out_shape=[jax.ShapeDtypeStruct((a.shape[0],) + a.shape[2:], a.dtype) for a in arrs],
        scratch_shapes=[pltpu.SemaphoreType.DMA((n,)), pltpu.SemaphoreType.DMA((n,))],
    )(*arrs)


def _scatter_chips(name, arrs):
    n = len(arrs)

    def body(*refs):
        ins, outs = refs[:n], refs[n:2 * n]
        send, recv = refs[2 * n:]
        x, y, c = _mesh_pos()
        chips = _other_chips(x, y)
        sends = []
        for j, (px, py) in enumerate(chips):
            for i in range(n):
                sends.append(pltpu.make_async_remote_copy(
                    src_ref=ins[i].at[2 * px + py], dst_ref=outs[i].at[j], send_sem=send.at[j * n + i],
                    recv_sem=recv.at[j * n + i], device_id=(px, py, c), device_id_type=MESH))
        for s in sends:
            s.start()
        for s in sends:
            s.wait()

    return pl.pallas_call(
        body, name=name, in_specs=[ANY] * n, out_specs=[ANY] * n,
        out_shape=[jax.ShapeDtypeStruct((NUM_CHIPS - 1,) + a.shape[1:], a.dtype) for a in arrs],
        scratch_shapes=[pltpu.SemaphoreType.DMA((3 * n,)), pltpu.SemaphoreType.DMA((3 * n,))],
    )(*arrs)


def _join_halves(name, arrs):
    n = len(arrs)

    def body(*refs):
        ins, outs = refs[:n], refs[n:2 * n]
        send, recv = refs[2 * n:]
        x, y, c = _mesh_pos()
        copies = [pltpu.make_async_remote_copy(
            src_ref=ins[i], dst_ref=outs[i], send_sem=send.at[i], recv_sem=recv.at[i],
            device_id=(x, y, 1 - c), device_id_type=MESH) for i in range(n)]
        for cp in copies:
            cp.start()
        for cp in copies:
            cp.wait()

    return pl.pallas_call(
        body, name=name, in_specs=[ANY] * n, out_specs=[ANY] * n,
        out_shape=[jax.ShapeDtypeStruct(a.shape, a.dtype) for a in arrs],
        scratch_shapes=[pltpu.SemaphoreType.DMA((n,)), pltpu.SemaphoreType.DMA((n,))],
    )(*arrs)


def _gather_all(name, pack):
    def body(in_ref, out_ref, send, recv, loc):
        x, y, c = _mesh_pos()
        me = 4 * x + 2 * y + c
        mine = pltpu.make_async_copy(in_ref, out_ref.at[me], loc)
        mine.start()
        peers = []
        for mask in range(1, NUM_DEVICES):
            px = 1 - x if mask & 4 else x
            py = 1 - y if mask & 2 else y
            pc = 1 - c if mask & 1 else c
            peers.append((px, py, pc))
        sends = [pltpu.make_async_remote_copy(
            src_ref=in_ref, dst_ref=out_ref.at[me], send_sem=send.at[k], recv_sem=recv.at[k],
            device_id=p, device_id_type=MESH) for k, p in enumerate(peers)]
        for s in sends:
            s.start()
        for k, (px, py, pc) in enumerate(peers):
            pltpu.make_async_remote_copy(
                src_ref=in_ref, dst_ref=out_ref.at[4 * px + 2 * py + pc], send_sem=send.at[k], recv_sem=recv.at[k],
                device_id=(px, py, pc), device_id_type=MESH).wait_recv()
        for s in sends:
            s.wait_send()
        mine.wait()

    return pl.pallas_call(
        body, name=name, in_specs=[ANY], out_specs=ANY,
        out_shape=jax.ShapeDtypeStruct((NUM_DEVICES,) + pack.shape, pack.dtype),
        scratch_shapes=[pltpu.SemaphoreType.DMA((NUM_DEVICES - 1,)), pltpu.SemaphoreType.DMA((NUM_DEVICES - 1,)),
                        pltpu.SemaphoreType.DMA],
    )(pack)


def _sum_slots(name, slots, out_dtype):
    g, rows, cols = slots.shape
    tr = _rtile(rows, 256)

    def body(s_ref, o_ref):
        acc = s_ref[0].astype(F32)
        for k in range(1, g):
            acc = acc + s_ref[k].astype(F32)
        o_ref[...] = acc.astype(o_ref.dtype)

    return pl.pallas_call(
        body, name=name, grid=(rows // tr,),
        in_specs=[pl.BlockSpec((g, tr, cols), lambda i: (0, i, 0))],
        out_specs=pl.BlockSpec((tr, cols), lambda i: (i, 0)),
        out_shape=jax.ShapeDtypeStruct((rows, cols), out_dtype), compiler_params=_params(),
    )(slots)


def _add_pair(name, p, q, core):
    g, _, rows, cols = p.shape
    tr = _rtile(rows, 128)

    def body(c_ref, p_ref, q_ref, o_ref):
        mine = jnp.where(c_ref[0] == 0, p_ref[:, 0].astype(F32), p_ref[:, 1].astype(F32))
        o_ref[...] = (mine + q_ref[...].astype(F32)).astype(o_ref.dtype)

    spec = pl.BlockSpec((g, tr, cols), lambda i: (0, i, 0))
    return pl.pallas_call(
        body, name=name, grid=(rows // tr,),
        in_specs=[SCALAR, pl.BlockSpec((g, 2, tr, cols), lambda i: (0, 0, i, 0)), spec],
        out_specs=spec, out_shape=jax.ShapeDtypeStruct(q.shape, BF16), compiler_params=_params(),
    )(core, p, q)


def _sum_chips(name, slots, part, chip):
    g, rows, cols = part.shape
    tr = _rtile(rows, 128)

    def body(c_ref, s_ref, p_ref, o_ref):
        acc = p_ref[c_ref[0]].astype(F32)
        for k in range(slots.shape[0]):
            acc = acc + s_ref[k].astype(F32)
        o_ref[...] = acc

    return pl.pallas_call(
        body, name=name, grid=(rows // tr,),
        in_specs=[SCALAR,
                  pl.BlockSpec((slots.shape[0], tr, cols), lambda i: (0, i, 0)),
                  pl.BlockSpec((g, tr, cols), lambda i: (0, i, 0))],
        out_specs=pl.BlockSpec((tr, cols), lambda i: (i, 0)),
        out_shape=jax.ShapeDtypeStruct((rows, cols), F32), compiler_params=_params(),
    )(chip, slots, part)


def _adamw(name, g, w, m, v, l=None):
    rows, cols = g.shape
    tr = _rtile(rows, 128)

    def body(g_ref, w_ref, m_ref, v_ref, d_o, m_o, v_o):
        gv = g_ref[...]
        mn = ADAM_B1 * m_ref[...] + (1.0 - ADAM_B1) * gv
        vn = ADAM_B2 * v_ref[...] + (1.0 - ADAM_B2) * (gv * gv)
        m_hat = mn / (1.0 - ADAM_B1 ** ADAM_STEP)
        v_hat = vn / (1.0 - ADAM_B2 ** ADAM_STEP)
        d_o[...] = -ADAM_LR * (m_hat / (jnp.sqrt(v_hat) + ADAM_EPS) + ADAM_WD * w_ref[...])
        m_o[...] = mn
        v_o[...] = vn

    gspec = pl.BlockSpec((tr, cols), lambda i: (i, 0))
    wspec = gspec if l is None else pl.BlockSpec((None, tr, cols), lambda i: (l, i, 0))
    f = jax.ShapeDtypeStruct((rows, cols), F32)
    return pl.pallas_call(
        body, name=name, grid=(rows // tr,), in_specs=[gspec] + [wspec] * 3, out_specs=[gspec] * 3,
        out_shape=[f] * 3, compiler_params=_params(),
    )(g, w, m, v)


def _adamw_layers(name, g0, g1, w, m, v):
    rows, cols = g0.shape
    tr = _rtile(rows, 128)
    nt = rows // tr

    def body(g0_ref, g1_ref, w_ref, m_ref, v_ref, g_o, d_o, m_o, v_o):
        gv = jnp.where(pl.program_id(0) == 0, g0_ref[...], g1_ref[...])
        _adamw_math(gv, w_ref, m_ref, v_ref, g_o, d_o, m_o, v_o)

    g0spec = pl.BlockSpec((tr, cols), lambda l, i: (i * (1 - l) + (nt - 1) * l, 0))
    g1spec = pl.BlockSpec((tr, cols), lambda l, i: (i * l, 0))
    wspec = pl.BlockSpec((None, tr, cols), lambda l, i: (l, i, 0))
    f = jax.ShapeDtypeStruct((2, rows, cols), F32)
    return pl.pallas_call(
        body, name=name, grid=(2, nt), in_specs=[g0spec, g1spec, wspec, wspec, wspec], out_specs=[wspec] * 4,
        out_shape=[f] * 4, compiler_params=_params(),
    )(g0, g1, w, m, v)


def _adamw_math(gv, w_ref, m_ref, v_ref, g_o, d_o, m_o, v_o):
    mn = ADAM_B1 * m_ref[...] + (1.0 - ADAM_B1) * gv
    vn = ADAM_B2 * v_ref[...] + (1.0 - ADAM_B2) * (gv * gv)
    m_hat = mn / (1.0 - ADAM_B1 ** ADAM_STEP)
    v_hat = vn / (1.0 - ADAM_B2 ** ADAM_STEP)
    g_o[...] = gv
    d_o[...] = -ADAM_LR * (m_hat / (jnp.sqrt(v_hat) + ADAM_EPS) + ADAM_WD * w_ref[...])
    m_o[...] = mn
    v_o[...] = vn


def _adamw_halves(name, mine, theirs, core, w, m, v):
    half, cols = mine[0].shape
    tr = _rtile(half, 128)
    nt = half // tr

    def body(c_ref, a0, b0, a1, b1, w_ref, m_ref, v_ref, g_o, d_o, m_o, v_o):
        first = pl.program_id(0) == 0
        own = pl.program_id(1) == c_ref[0]
        gv = jnp.where(first, jnp.where(own, a0[...], b0[...]), jnp.where(own, a1[...], b1[...]))
        _adamw_math(gv, w_ref, m_ref, v_ref, g_o, d_o, m_o, v_o)

    lay0 = pl.BlockSpec((tr, cols), lambda l, h, i: (i * (1 - l) + (nt - 1) * l, 0))
    lay1 = pl.BlockSpec((tr, cols), lambda l, h, i: (i * l, 0))
    wspec = pl.BlockSpec((None, tr, cols), lambda l, h, i: (l, h * nt + i, 0))
    f = jax.ShapeDtypeStruct((2, 2 * half, cols), F32)
    return pl.pallas_call(
        body, name=name, grid=(2, 2, nt), in_specs=[SCALAR, lay0, lay0, lay1, lay1, wspec, wspec, wspec],
        out_specs=[wspec] * 4, out_shape=[f] * 4, compiler_params=_params(),
    )(core, mine[0], theirs[0], mine[1], theirs[1], w, m, v)


def _colsum(name, a):
    def body(a_ref, o_ref):
        o_ref[...] = jnp.sum(a_ref[...], axis=0, keepdims=True)

    return pl.pallas_call(body, name=name, out_shape=jax.ShapeDtypeStruct((1, a.shape[1]), F32),
                          compiler_params=_params())(a)


PACK_UNIT = SUBLANE * LANE


def _pack(items):
    flat, layout, o = [], [], 0
    for it in items:
        n = it.size
        pad = -n % PACK_UNIT
        flat.append(jnp.pad(it.reshape(-1).astype(F32), (0, pad)))
        layout.append((o, n, it.shape))
        o += n + pad
    return jnp.concatenate(flat).reshape(-1, LANE), layout


def _unpack(pack, layout):
    flat = pack.reshape(-1)
    return [flat[o:o + n].reshape(shape) for o, n, shape in layout]


WEIGHTS = ("ffn1_norm", "ffn1_w1", "ffn1_w3", "ffn1_w2", "mix_norm", "w_in", "conv_w", "conv_b", "rg_wa", "rg_ba",
           "rg_wx", "rg_bx", "rg_lam", "fox_bf", "merge_b", "w_rg", "w_sb", "w_fox", "w_o", "ffn2_norm", "ffn2_w1",
           "ffn2_w3", "ffn2_w2", "ada_w", "ada_b", "final_norm", "final_ada_w", "final_ada_b")
DENSE = ("ffn1_w1", "ffn1_w3", "ffn1_w2", "w_in", "w_rg", "w_sb", "w_fox", "w_o", "ffn2_w1", "ffn2_w3", "ffn2_w2")
SMALL = ("ffn1_norm", "mix_norm", "ffn2_norm", "rg_small", "rg_wa", "rg_wx", "fox_bf", "merge_b")


def _step(a):
    x, c, tgt = a["x"], a["c"], a["loss_target"]
    bl, seq, d = x.shape
    depth, nh = a["fox_bf"].shape
    chans = a["rg_lam"].shape[1]
    bd = a["rg_wa"].shape[2]
    wq = 3 * nh * HEAD_DIM
    o_m = 2 * chans + 2 * wq
    batch = NUM_DEVICES * bl
    mx, my, mc = _mesh_pos()
    me = 2 * mx + my
    dev = 4 * mx + 2 * my + mc

    c_rows = -(-bl * d // LANE // SUBLANE) * SUBLANE
    c_pack = jnp.pad(c.reshape(-1, LANE), ((0, c_rows - bl * d // LANE), (0, 0)))
    c_all = _gather_all("gather_c", c_pack)[:, :bl * d // LANE].reshape(batch, d)
    c_act = _silu("c_act", c_all)
    c_b = c_act.astype(BF16)
    ncol, fcol = a["ada_w"].shape[2], a["final_ada_w"].shape[1]
    cols = []
    for l in range(depth):
        bias = jnp.broadcast_to(lax.dynamic_slice_in_dim(a["ada_b"][l], me * ncol, ncol)[None], (batch, ncol))
        cols.append(_mm(f"ada{l}", c_b, a["ada_w"][l].astype(BF16), NN, F32, acc=bias))
    bias = jnp.broadcast_to(lax.dynamic_slice_in_dim(a["final_ada_b"], me * fcol, fcol)[None], (batch, fcol))
    cols.append(_mm("ada_final", c_b, a["final_ada_w"].astype(BF16), NN, F32, acc=bias))
    mod_cols = jnp.concatenate(cols, axis=1).reshape(2, batch // 2, depth * ncol + fcol)

    names = DENSE + ("conv_w", "mod_cols")
    got = _gather_two_level("gather_weights", [a[n].astype(BF16) for n in DENSE] + [a["conv_w"], mod_cols])
    gw = dict(zip(names, got))
    w = _prepare(gw, a, d, chans, nh)
    mod_all = gw["mod_cols"].reshape(NUM_CHIPS, batch, -1)
    mine = lambda full: lax.dynamic_slice_in_dim(full, dev * bl, bl, axis=0)
    mods = [mine(mod_all[:, :, l * ncol:(l + 1) * ncol].transpose(1, 0, 2).reshape(batch, NUM_CHIPS * ncol))
            for l in range(depth)]
    fm = mine(mod_all[:, :, depth * ncol:].transpose(1, 0, 2).reshape(batch, NUM_CHIPS * fcol))

    loss, grad_x, grads, dmods, dfm = _local_step(w, x, tgt, mods, fm)
    loss = lax.psum(loss, ("x", "y", "c"))

    rs_in = []
    for l in range(depth):
        for n in DENSE:
            if n == "w_in":
                ga, gb, gf = grads["layers"][l]["w_in"]
                orig = jnp.concatenate([ga[:, :2 * chans], gb, gf[:, :nh], ga[:, 2 * chans:]], axis=1)
                rs_in.append(orig.reshape(d, NUM_CHIPS, -1).transpose(1, 0, 2))
            else:
                rs_in.append(grads["layers"][l][n])
    rs_in = [g.reshape(g.shape[0], 2, g.shape[1] // 2, g.shape[2]) for g in rs_in]
    core = jnp.reshape(mc, (1,)).astype(jnp.int32)
    chip = jnp.reshape(me, (1,)).astype(jnp.int32)
    theirs = _split_to_sibling("split_grads", rs_in)
    chip_part = [_add_pair(f"add_cores_{k}", g, t, core) for k, (g, t) in enumerate(zip(rs_in, theirs))]
    slots = _scatter_chips("scatter_grads", chip_part)
    reduced = [_sum_chips(f"sum_chips_{k}", s, p, chip) for k, (s, p) in enumerate(zip(slots, chip_part))]
    other = _join_halves("join_grads", reduced)

    out = {}

    def put(n, res, per_layer):
        for kind, val in zip(("grad_", "delta_", "new_m_", "new_v_"), res):
            out[kind + n] = jnp.stack(val).reshape(a[n].shape) if per_layer else val.reshape(a[n].shape)

    def flat3(v):
        return v.reshape(depth, -1, v.shape[-1])

    assert depth == 2
    nd = len(DENSE)
    for k, n in enumerate(DENSE):
        put(n, _adamw_halves(f"adamw_{n}", (reduced[k], reduced[nd + k]), (other[k], other[nd + k]), core,
                             flat3(a[n]), flat3(a["m_" + n]), flat3(a["v_" + n])), False)

    items = []
    for l in range(depth):
        g = grads["layers"][l]
        items += [g["ffn1_norm"], g["mix_norm"], g["ffn2_norm"], g["rg_small"], _unblockdiag(g["rg_dwa"], bd),
                  _unblockdiag(g["rg_dwx"], bd), g["fox_bf"], g["merge_b"], dmods[l]]
    items += [grads["final_norm"], dfm]
    pack, layout = _pack(items)
    gath = _gather_all("gather_small", pack)
    tot = _sum_slots("sum_small", gath, F32)

    def wpack(pre):
        its = []
        for l in range(depth):
            rg_rows = jnp.concatenate([jnp.zeros((4, chans), F32), a[pre + "conv_b"][l][None], a[pre + "rg_ba"][l][None],
                                       a[pre + "rg_bx"][l][None], a[pre + "rg_lam"][l][None]], axis=0)
            its += [a[pre + "ffn1_norm"][l], a[pre + "mix_norm"][l], a[pre + "ffn2_norm"][l], rg_rows,
                    a[pre + "rg_wa"][l], a[pre + "rg_wx"][l], a[pre + "fox_bf"][l], a[pre + "merge_b"][l],
                    jnp.zeros((bl, 9 * d), F32)]
        its += [a[pre + "final_norm"], jnp.zeros((bl, 2 * d), F32)]
        return _pack(its)[0]

    res_small = [_unpack(r, layout) for r in [tot] + list(_adamw("adamw_small", tot, wpack(""), wpack("m_"), wpack("v_")))]
    per = len(SMALL) + 1
    for j, n in enumerate(SMALL):
        if n == "rg_small":
            for row, nm in ((4, "conv_b"), (5, "rg_ba"), (6, "rg_bx"), (7, "rg_lam")):
                put(nm, [[r[l * per + j][row] for l in range(depth)] for r in res_small], True)
        else:
            put(n, [[r[l * per + j] for l in range(depth)] for r in res_small], True)
    put("final_norm", [r[depth * per] for r in res_small], False)

    gflat = gath.reshape(NUM_DEVICES, -1)

    def rows_of(idx):
        o, n, shape = layout[idx]
        return gflat[:, o:o + n].reshape(NUM_DEVICES * shape[0], shape[1])

    late_g, ada = [], []
    for l in range(depth):
        dmod_all = rows_of(l * per + per - 1)
        late_g.append(_colsum(f"ada_b_grad_{l}", dmod_all))
        cut = lax.dynamic_slice_in_dim(dmod_all, me * ncol, ncol, axis=1).astype(BF16)
        ada.append(_mm(f"ada_w_grad_{l}", c_b, cut, TN, F32))
    put("ada_w", _adamw_layers("adamw_ada_w", ada[0], ada[1], a["ada_w"], a["m_ada_w"], a["v_ada_w"]), False)
    dfm_all = rows_of(depth * per + 1)
    late_g.append(_colsum("final_ada_b_grad", dfm_all))
    cut = lax.dynamic_slice_in_dim(dfm_all, me * fcol, fcol, axis=1).astype(BF16)
    gl = _mm("final_ada_w_grad", c_b, cut, TN, F32)
    put("final_ada_w", [gl] + list(_adamw("adamw_final_ada_w", gl, a["final_ada_w"], a["m_final_ada_w"],
                                          a["v_final_ada_w"])), False)
    cshard = a["conv_w"].shape[2]
    for l in range(depth):
        rg_tot = res_small[0][l * per + SMALL.index("rg_small")]
        late_g.append(lax.dynamic_slice_in_dim(rg_tot[:4], me * cshard, cshard, axis=1))
    gp2, layout2 = _pack(late_g)

    def wpack2(pre):
        return _pack([a[pre + "ada_b"][l][None] for l in range(depth)] + [a[pre + "final_ada_b"][None]]
                     + [a[pre + "conv_w"][l] for l in range(depth)])[0]

    res_late = [_unpack(r, layout2) for r in [gp2] + list(_adamw("adamw_late", gp2, wpack2(""), wpack2("m_"), wpack2("v_")))]
    put("ada_b", [[r[l] for l in range(depth)] for r in res_late], True)
    put("final_ada_b", [r[depth] for r in res_late], False)
    put("conv_w", [[r[depth + 1 + l] for l in range(depth)] for r in res_late], True)

    outs = [loss, grad_x]
    for kind in ("grad_", "delta_", "new_m_", "new_v_"):
        outs += [out[kind + n] for n in WEIGHTS]
    return tuple(outs)


def kernel(x, c, ffn1_norm, ffn1_w1, ffn1_w3, ffn1_w2, mix_norm, w_in, conv_w, conv_b, rg_wa, rg_ba, rg_wx, rg_bx, rg_lam, fox_bf, merge_b, w_rg, w_sb, w_fox, w_o, ffn2_norm, ffn2_w1, ffn2_w3, ffn2_w2, ada_w, ada_b, final_norm, final_ada_w, final_ada_b, loss_target, m_ffn1_norm, m_ffn1_w1, m_ffn1_w3, m_ffn1_w2, m_mix_norm, m_w_in, m_conv_w, m_conv_b, m_rg_wa, m_rg_ba, m_rg_wx, m_rg_bx, m_rg_lam, m_fox_bf, m_merge_b, m_w_rg, m_w_sb, m_w_fox, m_w_o, m_ffn2_norm, m_ffn2_w1, m_ffn2_w3, m_ffn2_w2, m_ada_w, m_ada_b, m_final_norm, m_final_ada_w, m_final_ada_b, v_ffn1_norm, v_ffn1_w1, v_ffn1_w3, v_ffn1_w2, v_mix_norm, v_w_in, v_conv_w, v_conv_b, v_rg_wa, v_rg_ba, v_rg_wx, v_rg_bx, v_rg_lam, v_fox_bf, v_merge_b, v_w_rg, v_w_sb, v_w_fox, v_w_o, v_ffn2_norm, v_ffn2_w1, v_ffn2_w3, v_ffn2_w2, v_ada_w, v_ada_b, v_final_norm, v_final_ada_w, v_final_ada_b):
    args = dict(locals())
    return _step(args)
```

```python
import math

import jax
import jax.numpy as jnp
from jax import lax
from jax.experimental import pallas as pl
from jax.experimental.pallas import tpu as pltpu

F32 = jnp.float32
BF16 = jnp.bfloat16

NUM_CHIPS = 4
NUM_DEVICES = 8
HEAD_DIM = 64
LANE = 128
SUBLANE = 8
VMEM_LIMIT = 56 * 1024 * 1024
EPS = 1e-6
RG_C = 8.0
ADAM_LR = 0.001
ADAM_B1 = 0.9
ADAM_B2 = 0.999
ADAM_EPS = 1e-08
ADAM_WD = 0.01
ADAM_STEP = 10
MESH = pl.DeviceIdType.MESH
ANY = pl.BlockSpec(memory_space=pl.ANY)
SCALAR = pl.BlockSpec(memory_space=pltpu.SMEM)


def _params():
    return pltpu.CompilerParams(vmem_limit_bytes=VMEM_LIMIT)


def _tile(dim, pref):
    if dim <= pref:
        return dim
    t = (pref // LANE) * LANE
    while t >= LANE:
        if dim % t == 0:
            return t
        t -= LANE
    return dim


def _rtile(rows, pref, unit=2 * SUBLANE):
    if rows <= pref:
        return rows
    t = (pref // unit) * unit
    while t >= unit:
        if rows % t == 0:
            return t
        t -= unit
    return rows


def _sigmoid(x):
    return 0.5 * jnp.tanh(0.5 * x) + 0.5


def _softplus(x):
    return jnp.maximum(x, 0.0) + jnp.log(1.0 + jnp.exp(-jnp.abs(x)))


def _expm1(x):
    small = x * (1.0 + x * (0.5 + x * (1.0 / 6.0 + x * (1.0 / 24.0))))
    return jnp.where(jnp.abs(x) < 0.01, small, jnp.exp(x) - 1.0)


_GELU_K = math.sqrt(2.0 / math.pi)


def _gelu_and_grad(x):
    inner = _GELU_K * (x + 0.044715 * x * x * x)
    t = jnp.tanh(inner)
    val = 0.5 * x * (1.0 + t)
    dinner = _GELU_K * (1.0 + 3.0 * 0.044715 * x * x)
    grad = 0.5 * (1.0 + t) + 0.5 * x * (1.0 - t * t) * dinner
    return val, grad


NN = ((1,), (0,))
NT = ((1,), (1,))
TN = ((0,), (0,))
ALL = slice(None)


def _mmk(name, ops, specs, terms, out_shape, out_dtype, grid, o_spec, acc=None):
    n_ops = len(ops)

    def body(*refs):
        o_ref = refs[-1]
        p = None
        for ia, xa, ib, xb, dims in terms:
            t = lax.dot_general(refs[ia][xa], refs[ib][xb], (dims, ((), ())), preferred_element_type=F32)
            p = t if p is None else p + t
        if acc is not None:
            p = p + refs[n_ops][...].astype(F32)
        o_ref[...] = p.astype(o_ref.dtype)

    in_specs = list(specs)
    args = list(ops)
    if acc is not None:
        in_specs.append(pl.BlockSpec(o_spec.block_shape, o_spec.index_map))
        args.append(acc)
    return pl.pallas_call(
        body, name=name, grid=grid, in_specs=in_specs, out_specs=o_spec,
        out_shape=jax.ShapeDtypeStruct(out_shape, out_dtype), compiler_params=_params(),
    )(*args)


def _mm(name, a, b, dims, out_dtype, acc=None, tm=512, tn=512):
    if dims == NN:
        (m, kk), n = a.shape, b.shape[1]
    elif dims == NT:
        (m, kk), n = a.shape, b.shape[0]
    else:
        (kk, m), n = a.shape, b.shape[1]
    tm, tn = _tile(m, tm), _tile(n, tn)
    if dims == TN:
        a_spec = pl.BlockSpec((kk, tm), lambda i, j: (0, i))
    else:
        a_spec = pl.BlockSpec((tm, kk), lambda i, j: (i, 0))
    if dims == NT:
        b_spec = pl.BlockSpec((tn, kk), lambda i, j: (j, 0))
    else:
        b_spec = pl.BlockSpec((kk, tn), lambda i, j: (0, j))
    return _mmk(name, [a, b], [a_spec, b_spec], [(0, ALL, 1, ALL, dims)], (m, n), out_dtype,
                (m // tm, n // tn), pl.BlockSpec((tm, tn), lambda i, j: (i, j)), acc)


def _row_grid(t_rows, seq, pref=256):
    tm = _tile(seq, pref)
    return tm, seq // tm


def _normmod(name, x, gain, shift, scale, seq):
    t_rows, d = x.shape
    bl = t_rows // seq
    tm, per = _row_grid(t_rows, seq)

    def body(x_ref, g_ref, sh_ref, sc_ref, o_ref):
        xv = x_ref[...]
        rstd = lax.rsqrt(jnp.mean(xv * xv, axis=-1, keepdims=True) + EPS)
        hn = (xv * rstd) * g_ref[...]
        o_ref[...] = (hn * (1.0 + sc_ref[...]) + sh_ref[...]).astype(o_ref.dtype)

    row = pl.BlockSpec((tm, d), lambda b, i: (b * per + i, 0))
    vec = pl.BlockSpec((None, 1, d), lambda b, i: (b, 0, 0))
    return pl.pallas_call(
        body, name=name, grid=(bl, per),
        in_specs=[row, pl.BlockSpec((1, d), lambda b, i: (0, 0)), vec, vec],
        out_specs=row, out_shape=jax.ShapeDtypeStruct((t_rows, d), BF16),
        compiler_params=_params(),
    )(x, gain, shift, scale)


def _normmod_bwd(name, x, dh, dxo, gain, scale, seq):
    t_rows, d = x.shape
    bl = t_rows // seq
    tm, per = _row_grid(t_rows, seq)

    def body(x_ref, dh_ref, dxo_ref, g_ref, sc_ref, dx_ref, dsh_ref, dsc_ref, dg_ref):
        b, i = pl.program_id(0), pl.program_id(1)
        xv = x_ref[...]
        dhv = dh_ref[...]
        rstd = lax.rsqrt(jnp.mean(xv * xv, axis=-1, keepdims=True) + EPS)
        xhat = xv * rstd
        gain_v = g_ref[...]
        dhn = dhv * (1.0 + sc_ref[...])
        dxhat = dhn * gain_v
        dx = rstd * (dxhat - xhat * jnp.mean(dxhat * xhat, axis=-1, keepdims=True))
        dx_ref[...] = dxo_ref[...] + dx

        @pl.when(i == 0)
        def _():
            dsh_ref[...] = jnp.zeros_like(dsh_ref)
            dsc_ref[...] = jnp.zeros_like(dsc_ref)

        @pl.when((i == 0) & (b == 0))
        def _():
            dg_ref[...] = jnp.zeros_like(dg_ref)

        dsh_ref[...] += jnp.sum(dhv, axis=0, keepdims=True)
        dsc_ref[...] += jnp.sum(dhv * (xhat * gain_v), axis=0, keepdims=True)
        dg_ref[...] += jnp.sum(dhn * xhat, axis=0, keepdims=True)

    row = pl.BlockSpec((tm, d), lambda b, i: (b * per + i, 0))
    vec = pl.BlockSpec((None, 1, d), lambda b, i: (b, 0, 0))
    one = pl.BlockSpec((1, d), lambda b, i: (0, 0))
    return pl.pallas_call(
        body, name=name, grid=(bl, per),
        in_specs=[row, row, row, one, vec],
        out_specs=[row, vec, vec, one],
        out_shape=[jax.ShapeDtypeStruct((t_rows, d), F32), jax.ShapeDtypeStruct((bl, 1, d), F32),
                   jax.ShapeDtypeStruct((bl, 1, d), F32), jax.ShapeDtypeStruct((1, d), F32)],
        compiler_params=_params(),
    )(x, dh, dxo, gain, scale)


def _resid(name, x, y, gate, coef, seq):
    t_rows, d = x.shape
    bl = t_rows // seq
    tm, per = _row_grid(t_rows, seq)

    def body(x_ref, y_ref, g_ref, o_ref):
        o_ref[...] = x_ref[...] + (coef * (1.0 + g_ref[...])) * y_ref[...]

    row = pl.BlockSpec((tm, d), lambda b, i: (b * per + i, 0))
    vec = pl.BlockSpec((None, 1, d), lambda b, i: (b, 0, 0))
    return pl.pallas_call(
        body, name=name, grid=(bl, per), in_specs=[row, row, vec], out_specs=row,
        out_shape=jax.ShapeDtypeStruct((t_rows, d), F32), compiler_params=_params(),
    )(x, y, gate)


def _resid_bwd(name, dxo, y, gate, coef, seq):
    t_rows, d = dxo.shape
    bl = t_rows // seq
    tm, per = _row_grid(t_rows, seq)

    def body(dxo_ref, y_ref, g_ref, dy_ref, dg_ref):
        i = pl.program_id(1)
        dxov = dxo_ref[...]
        dy_ref[...] = ((coef * (1.0 + g_ref[...])) * dxov).astype(dy_ref.dtype)

        @pl.when(i == 0)
        def _():
            dg_ref[...] = jnp.zeros_like(dg_ref)

        dg_ref[...] += jnp.sum((coef * y_ref[...]) * dxov, axis=0, keepdims=True)

    row = pl.BlockSpec((tm, d), lambda b, i: (b * per + i, 0))
    vec = pl.BlockSpec((None, 1, d), lambda b, i: (b, 0, 0))
    return pl.pallas_call(
        body, name=name, grid=(bl, per), in_specs=[row, row, vec], out_specs=[row, vec],
        out_shape=[jax.ShapeDtypeStruct((t_rows, d), BF16), jax.ShapeDtypeStruct((bl, 1, d), F32)],
        compiler_params=_params(),
    )(dxo, y, gate)


def _final_loss(name, x, tgt, gain, shift, scale, seq):
    t_rows, d = x.shape
    bl = t_rows // seq
    tm, per = _row_grid(t_rows, seq)

    def body(x_ref, t_ref, g_ref, sh_ref, sc_ref, l_ref, dx_ref, dsh_ref, dsc_ref, dg_ref):
        b, i = pl.program_id(0), pl.program_id(1)
        xv = x_ref[...]
        rstd = lax.rsqrt(jnp.mean(xv * xv, axis=-1, keepdims=True) + EPS)
        xhat = xv * rstd
        gain_v = g_ref[...]
        hn = xhat * gain_v
        yv = hn * (1.0 + sc_ref[...]) + sh_ref[...]
        err = yv - t_ref[...]
        dyv = err * (1.0 / d)
        dhn = dyv * (1.0 + sc_ref[...])
        dxhat = dhn * gain_v
        dx_ref[...] = rstd * (dxhat - xhat * jnp.mean(dxhat * xhat, axis=-1, keepdims=True))

        @pl.when(i == 0)
        def _():
            l_ref[...] = jnp.zeros_like(l_ref)
            dsh_ref[...] = jnp.zeros_like(dsh_ref)
            dsc_ref[...] = jnp.zeros_like(dsc_ref)

        @pl.when((i == 0) & (b == 0))
        def _():
            dg_ref[...] = jnp.zeros_like(dg_ref)

        part = jnp.sum(jnp.sum(err * err, axis=-1, keepdims=True), axis=0, keepdims=True) * (0.5 / d)
        l_ref[...] += jnp.broadcast_to(part, l_ref.shape)
        dsh_ref[...] += jnp.sum(dyv, axis=0, keepdims=True)
        dsc_ref[...] += jnp.sum(dyv * hn, axis=0, keepdims=True)
        dg_ref[...] += jnp.sum(dhn * xhat, axis=0, keepdims=True)

    row = pl.BlockSpec((tm, d), lambda b, i: (b * per + i, 0))
    vec = pl.BlockSpec((None, 1, d), lambda b, i: (b, 0, 0))
    one = pl.BlockSpec((1, d), lambda b, i: (0, 0))
    lvec = pl.BlockSpec((None, 1, LANE), lambda b, i: (b, 0, 0))
    return pl.pallas_call(
        body, name=name, grid=(bl, per),
        in_specs=[row, row, one, vec, vec],
        out_specs=[lvec, row, vec, vec, one],
        out_shape=[jax.ShapeDtypeStruct((bl, 1, LANE), F32), jax.ShapeDtypeStruct((t_rows, d), F32),
                   jax.ShapeDtypeStruct((bl, 1, d), F32), jax.ShapeDtypeStruct((bl, 1, d), F32),
                   jax.ShapeDtypeStruct((1, d), F32)],
        compiler_params=_params(),
    )(x, tgt, gain, shift, scale)


def _ffn_up(name, h, w1, w3, l):
    t_rows, d = h.shape
    ng, fs = w1.shape[0], w1.shape[3]
    tm = _tile(t_rows, 512)

    def body(h_ref, w1_ref, w3_ref, a_ref, b_ref, g_ref):
        hv = h_ref[...]
        av = jnp.dot(hv, w1_ref[...], preferred_element_type=F32)
        bv = jnp.dot(hv, w3_ref[...], preferred_element_type=F32)
        a_ref[...] = av
        b_ref[...] = bv
        g_ref[...] = (av * _sigmoid(av) * bv).astype(g_ref.dtype)

    wspec = pl.BlockSpec((None, None, d, fs), lambda g, i: (g, l, 0, 0))
    out = pl.BlockSpec((None, tm, fs), lambda g, i: (g, i, 0))
    f = jax.ShapeDtypeStruct((ng, t_rows, fs), F32)
    return pl.pallas_call(
        body, name=name, grid=(ng, t_rows // tm),
        in_specs=[pl.BlockSpec((tm, d), lambda g, i: (i, 0)), wspec, wspec], out_specs=[out, out, out],
        out_shape=[f, f, jax.ShapeDtypeStruct((ng, t_rows, fs), BF16)], compiler_params=_params(),
    )(h, w1, w3)


def _ffn_down_dx(name, dy, w2, a, b, l):
    t_rows, d = dy.shape
    ng, fs = w2.shape[0], w2.shape[2]
    tm = _tile(t_rows, 512)

    def body(dy_ref, w2_ref, a_ref, b_ref, da_ref, db_ref):
        dgv = lax.dot_general(dy_ref[...], w2_ref[...], (NT, ((), ())), preferred_element_type=F32)
        av = a_ref[...]
        sig = _sigmoid(av)
        da_ref[...] = (dgv * b_ref[...] * (sig * (1.0 + av * (1.0 - sig)))).astype(da_ref.dtype)
        db_ref[...] = (dgv * (av * sig)).astype(db_ref.dtype)

    blk = pl.BlockSpec((None, tm, fs), lambda g, i: (g, i, 0))
    o = jax.ShapeDtypeStruct((ng, t_rows, fs), BF16)
    return pl.pallas_call(
        body, name=name, grid=(ng, t_rows // tm),
        in_specs=[pl.BlockSpec((tm, d), lambda g, i: (i, 0)),
                  pl.BlockSpec((None, None, fs, d), lambda g, i: (g, l, 0, 0)), blk, blk],
        out_specs=[blk, blk], out_shape=[o, o], compiler_params=_params(),
    )(dy, w2, a, b)


def _ffn_up_dw(name, h, da, db):
    t_rows, d = h.shape
    ng, fs = da.shape[0], da.shape[2]
    tn = _tile(d, 512)

    def body(h_ref, da_ref, db_ref, o1_ref, o3_ref):
        hv = h_ref[...]
        o1_ref[...] = lax.dot_general(hv, da_ref[...], (TN, ((), ())), preferred_element_type=F32).astype(o1_ref.dtype)
        o3_ref[...] = lax.dot_general(hv, db_ref[...], (TN, ((), ())), preferred_element_type=F32).astype(o3_ref.dtype)

    dspec = pl.BlockSpec((None, t_rows, fs), lambda g, i: (g, 0, 0))
    out = pl.BlockSpec((None, tn, fs), lambda g, i: (g, i, 0))
    o = jax.ShapeDtypeStruct((ng, d, fs), BF16)
    return pl.pallas_call(
        body, name=name, grid=(ng, d // tn),
        in_specs=[pl.BlockSpec((t_rows, tn), lambda g, i: (0, i)), dspec, dspec],
        out_specs=[out, out], out_shape=[o, o], compiler_params=_params(),
    )(h, da, db)


def _ffn_fwd(tag, w, l, pre, x, mod, seq):
    t_rows, d = x.shape
    w1, w3, w2 = w[pre + "w1"], w[pre + "w3"], w[pre + "w2"]
    ng, fs = w1.shape[0], w1.shape[3]
    shift, scale, gate = mod
    h = _normmod(tag + "_norm", x, w[pre + "norm"][l][None], shift, scale, seq)
    a, b, gact = _ffn_up(tag + "_up", h, w1, w3, l)
    tm, tn = _tile(t_rows, 512), _tile(d, 512)
    y = _mmk(tag + "_down", [gact, w2],
             [pl.BlockSpec((ng, tm, fs), lambda i, j: (0, i, 0)),
              pl.BlockSpec((ng, None, fs, tn), lambda i, j: (0, l, 0, j))],
             [(0, g, 1, g, NN) for g in range(ng)], (t_rows, d), F32, (t_rows // tm, d // tn),
             pl.BlockSpec((tm, tn), lambda i, j: (i, j)))
    xn = _resid(tag + "_res", x, y, gate, 0.5, seq)
    return xn, (x, h, a, b, gact, y)


def _ffn_bwd(tag, w, l, pre, saved, mod, dxo, seq):
    x, h, a, b, gact, y = saved
    t_rows, d = x.shape
    w1, w3, w2 = w[pre + "w1"], w[pre + "w3"], w[pre + "w2"]
    ng, fs = w1.shape[0], w1.shape[3]
    shift, scale, gate = mod
    tm, tn = _tile(t_rows, 512), _tile(d, 512)
    dy, dgate = _resid_bwd(tag + "_res_bwd", dxo, y, gate, 0.5, seq)
    da, db = _ffn_down_dx(tag + "_down_dx", dy, w2, a, b, l)
    dw2 = _mmk(tag + "_down_dw", [gact, dy],
               [pl.BlockSpec((None, t_rows, fs), lambda g, j: (g, 0, 0)),
                pl.BlockSpec((t_rows, tn), lambda g, j: (0, j))],
               [(0, ALL, 1, ALL, TN)], (ng, fs, d), BF16, (ng, d // tn),
               pl.BlockSpec((None, fs, tn), lambda g, j: (g, 0, j)))
    dw1, dw3 = _ffn_up_dw(tag + "_up_dw", h, da, db)
    dspec = pl.BlockSpec((ng, tm, fs), lambda i, j: (0, i, 0))
    wspec = pl.BlockSpec((ng, None, tn, fs), lambda i, j: (0, l, j, 0))
    dh = _mmk(tag + "_up_dx", [da, db, w1, w3], [dspec, dspec, wspec, wspec],
              [(0, g, 2, g, NT) for g in range(ng)] + [(1, g, 3, g, NT) for g in range(ng)],
              (t_rows, d), F32, (t_rows // tm, d // tn), pl.BlockSpec((tm, tn), lambda i, j: (i, j)))
    dx, dshift, dscale, dgain = _normmod_bwd(tag + "_norm_bwd", x, dh, dxo, w[pre + "norm"][l][None], scale, seq)
    grads = {pre + "w1": dw1, pre + "w3": dw3, pre + "w2": dw2, pre + "norm": dgain}
    return dx, (dshift, dscale, dgate), grads


def _shift_down(v, s, row):
    if s == 0:
        return v
    return jnp.where(row >= s, pltpu.roll(v, s, 0), 0.0)


def _shift_up(v, s, row):
    if s == 0:
        return v
    n = v.shape[0]
    return jnp.where(row < n - s, pltpu.roll(v, n - s, 0), 0.0)


def _scan_fwd(a, u, row):
    n = a.shape[0]
    s = 1
    while s < n:
        ok = row >= s
        a_sh = pltpu.roll(a, s, 0)
        u_sh = pltpu.roll(u, s, 0)
        u = jnp.where(ok, a * u_sh + u, u)
        a = jnp.where(ok, a * a_sh, a)
        s *= 2
    return u


def _scan_bwd(a_next, g, row):
    n = g.shape[0]
    a, u = a_next, g
    s = 1
    while s < n:
        ok = row < n - s
        a_sh = pltpu.roll(a, n - s, 0)
        u_sh = pltpu.roll(u, n - s, 0)
        u = jnp.where(ok, a * u_sh + u, u)
        a = jnp.where(ok, a * a_sh, a)
        s *= 2
    return u


def _rg_specs(seq, cw):
    slab = lambda off: pl.BlockSpec((seq, cw), lambda c, b: (b, off + c))
    par = lambda rows: pl.BlockSpec((rows, cw), lambda c, b: (0, c))
    wbd = pl.BlockSpec((None, cw, cw), lambda c, b: (c, 0, 0))
    return slab, par, wbd


def _rg_fwd(name, proj, p, seq, chans):
    t_rows = proj.shape[0]
    bl = t_rows // seq
    cw = LANE
    nc = chans // cw
    slab, par, wbd = _rg_specs(seq, cw)

    def body(x_ref, gt_ref, cw_ref, cb_ref, wa_ref, ba_ref, wx_ref, bx_ref, lam_ref,
             xa_ref, r_ref, i_ref, h_ref, ya_ref):
        row = lax.broadcasted_iota(jnp.int32, (seq, cw), 0)
        xv = x_ref[...]
        xa = jnp.zeros_like(xv) + cb_ref[...]
        for k in range(4):
            xa = xa + cw_ref[k:k + 1, :] * _shift_down(xv, 3 - k, row)
        xab = xa.astype(BF16)
        r = _sigmoid(jnp.dot(xab, wa_ref[...], preferred_element_type=F32) + ba_ref[...])
        ig = _sigmoid(jnp.dot(xab, wx_ref[...], preferred_element_type=F32) + bx_ref[...])
        log_a = (-RG_C) * r * _softplus(-lam_ref[...])
        a = jnp.exp(log_a)
        u = jnp.sqrt(-_expm1(2.0 * log_a)) * (ig * xa)
        h = _scan_fwd(a, u, row)
        gel, _ = _gelu_and_grad(gt_ref[...])
        xa_ref[...] = xa
        r_ref[...] = r
        i_ref[...] = ig
        h_ref[...] = h
        ya_ref[...] = (gel * h).astype(ya_ref.dtype)

    out = pl.BlockSpec((seq, cw), lambda c, b: (b, c))
    f = jax.ShapeDtypeStruct((t_rows, chans), F32)
    return pl.pallas_call(
        body, name=name, grid=(nc, bl),
        in_specs=[slab(0), slab(nc), par(4), par(1), wbd, par(1), wbd, par(1), par(1)],
        out_specs=[out] * 5,
        out_shape=[f, f, f, f, jax.ShapeDtypeStruct((t_rows, chans), BF16)],
        compiler_params=_params(),
    )(proj, proj, p["conv_w"], p["conv_b"], p["wa"], p["ba"], p["wx"], p["bx"], p["lam"])


def _rg_bwd(name, proj, dya, saved, p, seq, chans):
    xa_s, r_s, i_s, h_s = saved
    t_rows = proj.shape[0]
    bl = t_rows // seq
    cw = LANE
    nc = chans // cw
    slab, par, wbd = _rg_specs(seq, cw)

    def body(x_ref, gt_ref, dya_ref, xa_ref, r_ref, i_ref, h_ref, cw_ref, wa_ref, wx_ref, lam_ref,
             dx_ref, dgt_ref, sm_ref, dwa_ref, dwx_ref):
        b = pl.program_id(1)
        row = lax.broadcasted_iota(jnp.int32, (seq, cw), 0)
        xv, xa, r, ig, h = x_ref[...], xa_ref[...], r_ref[...], i_ref[...], h_ref[...]
        dyav = dya_ref[...]
        gel, dgel = _gelu_and_grad(gt_ref[...])
        dgt_ref[...] = (dyav * h * dgel).astype(dgt_ref.dtype)
        dh = dyav * gel
        lam = lam_ref[...]
        sp = _softplus(-lam)
        log_a = (-RG_C) * r * sp
        a = jnp.exp(log_a)
        s = jnp.sqrt(-_expm1(2.0 * log_a))
        lamb = _scan_bwd(_shift_up(a, 1, row), dh, row)
        da = lamb * _shift_down(h, 1, row)
        xi = ig * xa
        ds = lamb * xi
        dxi = lamb * s
        dlog = da * a - ds * (a * a) / s
        dr = dlog * ((-RG_C) * sp)
        dsp = jnp.sum(dlog * ((-RG_C) * r), axis=0, keepdims=True)
        dlam = -dsp * _sigmoid(-lam)
        dzr = dr * r * (1.0 - r)
        dzi = (dxi * xa) * ig * (1.0 - ig)
        dzrb, dzib, xab = dzr.astype(BF16), dzi.astype(BF16), xa.astype(BF16)
        dxa = dxi * ig
        dxa = dxa + lax.dot_general(dzrb, wa_ref[...], (NT, ((), ())), preferred_element_type=F32)
        dxa = dxa + lax.dot_general(dzib, wx_ref[...], (NT, ((), ())), preferred_element_type=F32)
        dwa = lax.dot_general(xab, dzrb, (TN, ((), ())), preferred_element_type=F32)
        dwx = lax.dot_general(xab, dzib, (TN, ((), ())), preferred_element_type=F32)
        dxv = jnp.zeros_like(xv)
        rows = []
        for k in range(4):
            dxv = dxv + cw_ref[k:k + 1, :] * _shift_up(dxa, 3 - k, row)
            rows.append(jnp.sum(dxa * _shift_down(xv, 3 - k, row), axis=0, keepdims=True))
        dx_ref[...] = dxv.astype(dx_ref.dtype)
        rows += [jnp.sum(dxa, axis=0, keepdims=True), jnp.sum(dzr, axis=0, keepdims=True),
                 jnp.sum(dzi, axis=0, keepdims=True), dlam]

        @pl.when(b == 0)
        def _():
            sm_ref[...] = jnp.zeros_like(sm_ref)
            dwa_ref[...] = jnp.zeros_like(dwa_ref)
            dwx_ref[...] = jnp.zeros_like(dwx_ref)

        for k, val in enumerate(rows):
            sm_ref[k:k + 1, :] += val
        dwa_ref[...] += dwa
        dwx_ref[...] += dwx

    plain = pl.BlockSpec((seq, cw), lambda c, b: (b, c))
    return pl.pallas_call(
        body, name=name, grid=(nc, bl),
        in_specs=[slab(0), slab(nc), plain, plain, plain, plain, plain, par(4), wbd, wbd, par(1)],
        out_specs=[plain, plain, par(8), wbd, wbd],
        out_shape=[jax.ShapeDtypeStruct((t_rows, chans), BF16), jax.ShapeDtypeStruct((t_rows, chans), BF16),
                   jax.ShapeDtypeStruct((8, chans), F32),
                   jax.ShapeDtypeStruct((nc, cw, cw), F32), jax.ShapeDtypeStruct((nc, cw, cw), F32)],
        compiler_params=_params(),
    )(proj, proj, dya, xa_s, r_s, i_s, h_s, p["conv_w"], p["wa"], p["wx"], p["lam"])


ATT_Q_BLOCK = 512
ATT_K_BLOCK = 256
PAIR = LANE // HEAD_DIM
NEG = -1e30
SCALE = HEAD_DIM ** -0.5
assert math.log2(HEAD_DIM) % 2 == 0


def _att_blocks(seq):
    return _tile(seq, ATT_Q_BLOCK), _tile(seq, ATT_K_BLOCK)


def _key_blocks(qi, tq, bk):
    return (qi * tq) // bk, (qi * tq + tq - 1) // bk + 1


def _tri(n, kind):
    r = lax.broadcasted_iota(jnp.int32, (2 * n, n), 0)
    r = jnp.where(r >= n, r - n, r)
    c = lax.broadcasted_iota(jnp.int32, (2 * n, n), 1)
    m = {"gt": r > c, "le": r <= c, "lt": r < c}[kind]
    return m.astype(BF16)


def _cumsum_mm(v, tri):
    hi = v.astype(BF16)
    lo = (v - hi.astype(F32)).astype(BF16)
    return jnp.dot(jnp.concatenate([hi, lo], axis=1), tri, preferred_element_type=F32)


def _head_masks():
    lane = lax.broadcasted_iota(jnp.int32, (1, LANE), 1)
    return [(lane >= h * HEAD_DIM) & (lane < (h + 1) * HEAD_DIM) for h in range(PAIR)]


def _only(mask, v):
    return jnp.where(mask, v, jnp.zeros_like(v))


def _stack_heads(v, masks):
    return jnp.concatenate([_only(m, v) for m in masks], axis=0)


def _unstack_heads(v, masks):
    tq = v.shape[0] // PAIR
    out = _only(masks[0], v[0:tq])
    for h in range(1, PAIR):
        out = out + _only(masks[h], v[h * tq:(h + 1) * tq])
    return out


def _stacked_iotas(tq, bk):
    row = lax.broadcasted_iota(jnp.int32, (PAIR * tq, bk), 0)
    for h in range(1, PAIR):
        row = jnp.where(row >= h * tq, row - tq, row)
    return row, lax.broadcasted_iota(jnp.int32, (PAIR * tq, bk), 1)


def _att_specs(seq, blk, nq, off):
    npair = None
    qs = lambda o: pl.BlockSpec((blk, LANE), lambda b, p, i: (b * nq + i, o + p))
    ks = lambda o: pl.BlockSpec((seq, LANE), lambda b, p, i: (b, o + p))
    col = pl.BlockSpec((None, PAIR, blk, 1), lambda b, p, i: (b, p, i, 0))
    lane = pl.BlockSpec((None, PAIR, 1, seq), lambda b, p, i: (b, p, 0, 0))
    return qs, ks, col, lane


def _sb_fwd(name, qkv, off, width, bl, seq):
    t_rows = qkv.shape[0]
    tq, bk = _att_blocks(seq)
    nq = seq // tq
    nb = width // LANE
    qs, ks, col, _ = _att_specs(seq, tq, nq, off)

    def body(q_ref, k_ref, v_ref, o_ref, lt_ref):
        qi = pl.program_id(2)
        masks = _head_masks()
        qs_ = _stack_heads(q_ref[...] * SCALE, masks)
        row, cix = _stacked_iotas(tq, bk)
        tri = _tri(bk, "gt")

        def step(masked, top):
            def go(it, carry):
                acc, cl = carry
                kb = top - it
                ks_ = pl.multiple_of(kb * bk, bk)
                kv = k_ref[pl.ds(ks_, bk), :]
                vv = v_ref[pl.ds(ks_, bk), :]
                strict = (kb * bk + cix) < (qi * tq + row)
                z = lax.dot_general(qs_, kv, (NT, ((), ())), preferred_element_type=F32)
                sp = _softplus(z)
                lk = jnp.where(strict, -sp, 0.0) if masked else -sp
                wgt = jnp.exp(z - sp + (cl + _cumsum_mm(lk, tri)))
                if masked:
                    wgt = jnp.where(strict, wgt, 0.0)
                acc = acc + _unstack_heads(jnp.dot(wgt.astype(BF16), vv, preferred_element_type=F32), masks)
                return acc, cl + jnp.sum(lk, axis=1, keepdims=True)
            return go

        n_full, n_all = _key_blocks(qi, tq, bk)
        carry = (jnp.zeros((tq, LANE), F32), jnp.zeros((PAIR * tq, 1), F32))
        carry = lax.fori_loop(0, n_all - n_full, step(True, n_all - 1), carry)
        acc, cl = lax.fori_loop(0, n_full, step(False, n_full - 1), carry)
        o_ref[...] = acc.astype(o_ref.dtype)
        for h in range(PAIR):
            lt_ref[h] = cl[h * tq:(h + 1) * tq]

    return pl.pallas_call(
        body, name=name, grid=(bl, nb, nq), in_specs=[qs(off), ks(off + nb), ks(off + 2 * nb)],
        out_specs=[qs(0), col],
        out_shape=[jax.ShapeDtypeStruct((t_rows, width), BF16),
                   jax.ShapeDtypeStruct((bl, nb * PAIR, seq, 1), F32)],
        compiler_params=_params(),
    )(qkv, qkv, qkv)


def _sb_bwd(name, qkv, off, width, bl, seq, ltot, do):
    t_rows = qkv.shape[0]
    tq, bk = _att_blocks(seq)
    nq = seq // tq
    nb = width // LANE
    qs, ks, col, _ = _att_specs(seq, tq, nq, off)

    def body(q_ref, k_ref, v_ref, lt_ref, do_ref, dq_ref, dk_ref, dv_ref, dk_acc, dv_acc):
        qi = pl.program_id(2)

        @pl.when(qi == 0)
        def _():
            dk_acc[...] = jnp.zeros_like(dk_acc)
            dv_acc[...] = jnp.zeros_like(dv_acc)

        masks = _head_masks()
        qs_ = _stack_heads(q_ref[...] * SCALE, masks)
        dos = _stack_heads(do_ref[...].astype(BF16), masks)
        lts = jnp.concatenate([lt_ref[h] for h in range(PAIR)], axis=0)
        row, cix = _stacked_iotas(tq, bk)
        tri_le = _tri(bk, "le")
        tri_lt = _tri(bk, "lt")

        def step(masked):
            def go(kb, carry):
                dq, cl, ce = carry
                ks_ = pl.multiple_of(kb * bk, bk)
                kv = k_ref[pl.ds(ks_, bk), :]
                vv = v_ref[pl.ds(ks_, bk), :]
                strict = (kb * bk + cix) < (qi * tq + row)
                z = lax.dot_general(qs_, kv, (NT, ((), ())), preferred_element_type=F32)
                sp = _softplus(z)
                lk = jnp.where(strict, -sp, 0.0) if masked else -sp
                sig = jnp.exp(z - sp)
                wgt = sig * jnp.exp(lts - cl - _cumsum_mm(lk, tri_le))
                if masked:
                    wgt = jnp.where(strict, wgt, 0.0)
                dw = lax.dot_general(dos, vv, (NT, ((), ())), preferred_element_type=F32)
                e = dw * wgt
                pre = ce + _cumsum_mm(e, tri_lt)
                dz = e * (1.0 - sig) - pre * sig
                if masked:
                    dz = jnp.where(strict, dz, 0.0)
                dzb = dz.astype(BF16)
                dq = dq + _unstack_heads(jnp.dot(dzb, kv * SCALE, preferred_element_type=F32), masks)
                dk_acc[pl.ds(ks_, bk), :] += lax.dot_general(dzb, qs_, (TN, ((), ())), preferred_element_type=F32)
                dv_acc[pl.ds(ks_, bk), :] += lax.dot_general(wgt.astype(BF16), dos, (TN, ((), ())),
                                                             preferred_element_type=F32)
                return dq, cl + jnp.sum(lk, axis=1, keepdims=True), ce + jnp.sum(e, axis=1, keepdims=True)
            return go

        n_full, n_all = _key_blocks(qi, tq, bk)
        zero = jnp.zeros((PAIR * tq, 1), F32)
        carry = lax.fori_loop(0, n_full, step(False), (jnp.zeros((tq, LANE), F32), zero, zero))
        dq, _, _ = lax.fori_loop(n_full, n_all, step(True), carry)
        dq_ref[...] = dq.astype(dq_ref.dtype)

        @pl.when(qi == nq - 1)
        def _():
            dk_ref[...] = dk_acc[...].astype(dk_ref.dtype)
            dv_ref[...] = dv_acc[...].astype(dv_ref.dtype)

    o = jax.ShapeDtypeStruct((t_rows, width), BF16)
    return pl.pallas_call(
        body, name=name, grid=(bl, nb, nq),
        in_specs=[qs(off), ks(off + nb), ks(off + 2 * nb), col, qs(0)], out_specs=[qs(0), ks(0), ks(0)],
        out_shape=[o, o, o], scratch_shapes=[pltpu.VMEM((seq, LANE), F32), pltpu.VMEM((seq, LANE), F32)],
        compiler_params=_params(),
    )(qkv, qkv, qkv, ltot, do)


def _fox_fwd(name, qkv, off, width, bl, seq, cum_q, cum_k):
    t_rows = qkv.shape[0]
    tq, bk = _att_blocks(seq)
    nq = seq // tq
    nb = width // LANE
    qs, ks, col, lane = _att_specs(seq, tq, nq, off)

    def body(q_ref, k_ref, v_ref, cq_ref, ck_ref, ob_ref, of_ref, lse_ref):
        qi = pl.program_id(2)
        masks = _head_masks()
        qs_ = _stack_heads(q_ref[...] * SCALE, masks)
        cqs = jnp.concatenate([cq_ref[h] for h in range(PAIR)], axis=0)
        row, cix = _stacked_iotas(tq, bk)

        def step(masked):
            def go(kb, carry):
                m, lsum, acc = carry
                ks_ = pl.multiple_of(kb * bk, bk)
                kv = k_ref[pl.ds(ks_, bk), :]
                vv = v_ref[pl.ds(ks_, bk), :]
                bias = jnp.concatenate([cqs[h * tq:(h + 1) * tq] - ck_ref[h, :, pl.ds(ks_, bk)] for h in range(PAIR)],
                                       axis=0)
                z = lax.dot_general(qs_, kv, (NT, ((), ())), preferred_element_type=F32) + bias
                if masked:
                    z = jnp.where((kb * bk + cix) <= (qi * tq + row), z, NEG)
                m_new = jnp.maximum(m, jnp.max(z, axis=1, keepdims=True))
                pv = jnp.exp(z - m_new)
                alpha = jnp.exp(m - m_new)
                lsum = alpha * lsum + jnp.sum(pv, axis=1, keepdims=True)
                acc = alpha * acc + jnp.dot(pv.astype(BF16), vv, preferred_element_type=F32)
                return m_new, lsum, acc
            return go

        n_full, n_all = _key_blocks(qi, tq, bk)
        init = (jnp.full((PAIR * tq, 1), NEG, F32), jnp.zeros((PAIR * tq, 1), F32),
                jnp.zeros((PAIR * tq, LANE), F32))
        carry = lax.fori_loop(0, n_full, step(False), init)
        m, lsum, acc = lax.fori_loop(n_full, n_all, step(True), carry)
        out = _unstack_heads(acc / lsum, masks)
        ob_ref[...] = out.astype(ob_ref.dtype)
        of_ref[...] = out
        lse = m + jnp.log(lsum)
        for h in range(PAIR):
            lse_ref[h] = lse[h * tq:(h + 1) * tq]

    return pl.pallas_call(
        body, name=name, grid=(bl, nb, nq),
        in_specs=[qs(off), ks(off + nb), ks(off + 2 * nb), col, lane], out_specs=[qs(0), qs(0), col],
        out_shape=[jax.ShapeDtypeStruct((t_rows, width), BF16), jax.ShapeDtypeStruct((t_rows, width), F32),
                   jax.ShapeDtypeStruct((bl, nb * PAIR, seq, 1), F32)],
        compiler_params=_params(),
    )(qkv, qkv, qkv, cum_q, cum_k)


def _fox_bwd(name, qkv, off, width, bl, seq, cum_q, cum_k, lse, o, do):
    t_rows = qkv.shape[0]
    tq, bk = _att_blocks(seq)
    nq = seq // tq
    nb = width // LANE
    qs, ks, col, lane = _att_specs(seq, tq, nq, off)

    def body(q_ref, k_ref, v_ref, cq_ref, ck_ref, lse_ref, o_ref, do_ref,
             dq_ref, dk_ref, dv_ref, dcq_ref, dck_ref, dk_acc, dv_acc):
        qi = pl.program_id(2)

        @pl.when(qi == 0)
        def _():
            dk_acc[...] = jnp.zeros_like(dk_acc)
            dv_acc[...] = jnp.zeros_like(dv_acc)
            dck_ref[...] = jnp.zeros_like(dck_ref)

        masks = _head_masks()
        qs_ = _stack_heads(q_ref[...] * SCALE, masks)
        dof = do_ref[...]
        dos = _stack_heads(dof.astype(BF16), masks)
        prod = dof * o_ref[...]
        delta = jnp.concatenate([jnp.sum(_only(m, prod), axis=1, keepdims=True) for m in masks], axis=0)
        shift = jnp.concatenate([cq_ref[h] - lse_ref[h] for h in range(PAIR)], axis=0)
        row, cix = _stacked_iotas(tq, bk)

        def step(masked):
            def go(kb, carry):
                dq, dcq = carry
                ks_ = pl.multiple_of(kb * bk, bk)
                kv = k_ref[pl.ds(ks_, bk), :]
                vv = v_ref[pl.ds(ks_, bk), :]
                bias = jnp.concatenate(
                    [shift[h * tq:(h + 1) * tq] - ck_ref[h, :, pl.ds(ks_, bk)] for h in range(PAIR)], axis=0)
                pv = jnp.exp(lax.dot_general(qs_, kv, (NT, ((), ())), preferred_element_type=F32) + bias)
                if masked:
                    pv = jnp.where((kb * bk + cix) <= (qi * tq + row), pv, 0.0)
                dp = lax.dot_general(dos, vv, (NT, ((), ())), preferred_element_type=F32)
                ds = pv * (dp - delta)
                dsb = ds.astype(BF16)
                dq = dq + _unstack_heads(jnp.dot(dsb, kv * SCALE, preferred_element_type=F32), masks)
                dk_acc[pl.ds(ks_, bk), :] += lax.dot_general(dsb, qs_, (TN, ((), ())), preferred_element_type=F32)
                dv_acc[pl.ds(ks_, bk), :] += lax.dot_general(pv.astype(BF16), dos, (TN, ((), ())),
                                                             preferred_element_type=F32)
                for h in range(PAIR):
                    dck_ref[h, :, pl.ds(ks_, bk)] += -jnp.sum(ds[h * tq:(h + 1) * tq], axis=0, keepdims=True)
                return dq, dcq + jnp.sum(ds, axis=1, keepdims=True)
            return go

        n_full, n_all = _key_blocks(qi, tq, bk)
        carry = lax.fori_loop(0, n_full, step(False), (jnp.zeros((tq, LANE), F32), jnp.zeros((PAIR * tq, 1), F32)))
        dq, dcq = lax.fori_loop(n_full, n_all, step(True), carry)
        dq_ref[...] = dq.astype(dq_ref.dtype)
        for h in range(PAIR):
            dcq_ref[h] = dcq[h * tq:(h + 1) * tq]

        @pl.when(qi == nq - 1)
        def _():
            dk_ref[...] = dk_acc[...].astype(dk_ref.dtype)
            dv_ref[...] = dv_acc[...].astype(dv_ref.dtype)

    ob = jax.ShapeDtypeStruct((t_rows, width), BF16)
    nh = nb * PAIR
    return pl.pallas_call(
        body, name=name, grid=(bl, nb, nq),
        in_specs=[qs(off), ks(off + nb), ks(off + 2 * nb), col, lane, col, qs(0), qs(0)],
        out_specs=[qs(0), ks(0), ks(0), col, lane],
        out_shape=[ob, ob, ob, jax.ShapeDtypeStruct((bl, nh, seq, 1), F32), jax.ShapeDtypeStruct((bl, nh, 1, seq), F32)],
        scratch_shapes=[pltpu.VMEM((seq, LANE), F32), pltpu.VMEM((seq, LANE), F32)],
        compiler_params=_params(),
    )(qkv, qkv, qkv, cum_q, cum_k, lse, o, do)


def _lane_cumsum(v, reverse):
    n = v.shape[1]
    cix = lax.broadcasted_iota(jnp.int32, v.shape, 1)
    s = 1
    while s < n:
        if reverse:
            v = v + jnp.where(cix < n - s, pltpu.roll(v, n - s, 1), 0.0)
        else:
            v = v + jnp.where(cix >= s, pltpu.roll(v, s, 1), 0.0)
        s *= 2
    return v


def _forget_cum(name, fl, bf):
    def body(fl_ref, bf_ref, o_ref):
        xv = fl_ref[...] + bf_ref[...]
        o_ref[...] = _lane_cumsum(-_softplus(-xv), False)

    return pl.pallas_call(body, name=name, out_shape=jax.ShapeDtypeStruct(fl.shape, F32),
                          compiler_params=_params())(fl, bf)


def _forget_cum_bwd(name, fl, bf, dcum, nh):
    rows = fl.shape[0]

    def body(fl_ref, bf_ref, dc_ref, dfl_ref, dbf_ref):
        xv = fl_ref[...] + bf_ref[...]
        dlogf = _lane_cumsum(dc_ref[...], True)
        dfl = dlogf * _sigmoid(-xv)
        dfl_ref[...] = dfl
        per_row = jnp.sum(dfl, axis=1, keepdims=True)
        tot = per_row[0:nh]
        for b in range(1, rows // nh):
            tot = tot + per_row[b * nh:(b + 1) * nh]
        dbf_ref[...] = tot

    return pl.pallas_call(
        body, name=name,
        out_shape=[jax.ShapeDtypeStruct(fl.shape, F32), jax.ShapeDtypeStruct((nh, 1), F32)],
        compiler_params=_params(),
    )(fl, bf, dcum)


def _merge_fwd(name, proj, off, merge_b, pa, pb, pc):
    t_rows, d = pa.shape
    tm = _tile(t_rows, 256)

    def body(l0, l1, l2, mb, a_ref, b_ref, c_ref, o_ref):
        g0 = _sigmoid(l0[...] + mb[:, 0:d])
        g1 = _sigmoid(l1[...] + mb[:, d:2 * d])
        g2 = _sigmoid(l2[...] + mb[:, 2 * d:3 * d])
        o_ref[...] = (g0 * a_ref[...] + g1 * b_ref[...] + g2 * c_ref[...]).astype(o_ref.dtype)

    row = pl.BlockSpec((tm, d), lambda i: (i, 0))
    lg = lambda j: pl.BlockSpec((tm, d), lambda i: (i, off + j))
    return pl.pallas_call(
        body, name=name, grid=(t_rows // tm,),
        in_specs=[lg(0), lg(1), lg(2), pl.BlockSpec((1, 3 * d), lambda i: (0, 0)), row, row, row],
        out_specs=row, out_shape=jax.ShapeDtypeStruct((t_rows, d), BF16), compiler_params=_params(),
    )(proj, proj, proj, merge_b, pa, pb, pc)


def _merge_bwd(name, proj, off, merge_b, pa, pb, pc, dmixed):
    t_rows, d = pa.shape
    tm = _tile(t_rows, 256)

    def body(l0, l1, l2, mb, a_ref, b_ref, c_ref, dm_ref, da_ref, db_ref, dc_ref, dl_ref, dmb_ref):
        i = pl.program_id(0)
        dm = dm_ref[...]
        parts = []
        for j, (lref, pref, dref) in enumerate(((l0, a_ref, da_ref), (l1, b_ref, db_ref), (l2, c_ref, dc_ref))):
            g = _sigmoid(lref[...] + mb[:, j * d:(j + 1) * d])
            dref[...] = (g * dm).astype(dref.dtype)
            dl = dm * pref[...] * g * (1.0 - g)
            dl_ref[:, j * d:(j + 1) * d] = dl.astype(dl_ref.dtype)
            parts.append(jnp.sum(dl, axis=0, keepdims=True))
        tot = jnp.concatenate(parts, axis=1)

        @pl.when(i == 0)
        def _():
            dmb_ref[...] = tot

        @pl.when(i > 0)
        def _():
            dmb_ref[...] += tot

    row = pl.BlockSpec((tm, d), lambda i: (i, 0))
    lg = lambda j: pl.BlockSpec((tm, d), lambda i: (i, off + j))
    one = pl.BlockSpec((1, 3 * d), lambda i: (0, 0))
    b16 = jax.ShapeDtypeStruct((t_rows, d), BF16)
    return pl.pallas_call(
        body, name=name, grid=(t_rows // tm,),
        in_specs=[lg(0), lg(1), lg(2), one, row, row, row, row],
        out_specs=[row, row, row, pl.BlockSpec((tm, 3 * d), lambda i: (i, 0)), one],
        out_shape=[b16, b16, b16, jax.ShapeDtypeStruct((t_rows, 3 * d), BF16), jax.ShapeDtypeStruct((1, 3 * d), F32)],
        compiler_params=_params(),
    )(proj, proj, proj, merge_b, pa, pb, pc, dmixed)


def _grouped_nn(name, a, wg, l, out_dtype):
    t_rows, kk = a.shape
    ng, ncol = wg.shape[0], wg.shape[3]
    tm = _tile(t_rows, 512)
    return _mmk(name, [a, wg],
                [pl.BlockSpec((tm, kk), lambda i, g: (i, 0)),
                 pl.BlockSpec((None, None, kk, ncol), lambda i, g: (g, l, 0, 0))],
                [(0, ALL, 1, ALL, NN)], (t_rows, ng * ncol), out_dtype, (t_rows // tm, ng),
                pl.BlockSpec((tm, ncol), lambda i, g: (i, g)))


def _grouped_nt(name, da, wg, l, out_dtype):
    t_rows = da.shape[0]
    ng, kk, ncol = wg.shape[0], wg.shape[2], wg.shape[3]
    tm = _tile(t_rows, 512)
    return _mmk(name, [da, wg],
                [pl.BlockSpec((tm, ng * ncol), lambda i: (i, 0)),
                 pl.BlockSpec((ng, None, kk, ncol), lambda i: (0, l, 0, 0))],
                [(0, (ALL, slice(g * ncol, (g + 1) * ncol)), 1, g, NT) for g in range(ng)],
                (t_rows, kk), out_dtype, (t_rows // tm,), pl.BlockSpec((tm, kk), lambda i: (i, 0)))


def _grouped_tn(name, a, da, ng, out_dtype):
    t_rows, kk = a.shape
    ncol = da.shape[1] // ng
    return _mmk(name, [a, da],
                [pl.BlockSpec((t_rows, kk), lambda g: (0, 0)), pl.BlockSpec((t_rows, ncol), lambda g: (0, g))],
                [(0, ALL, 1, ALL, TN)], (ng, kk, ncol), out_dtype, (ng,),
                pl.BlockSpec((None, kk, ncol), lambda g: (g, 0, 0)))


def _rows_nn(name, a, wr, l, out_dtype):
    t_rows = a.shape[0]
    ng, kg, n = wr.shape[0], wr.shape[2], wr.shape[3]
    tm, tn = _tile(t_rows, 512), _tile(n, 512)
    return _mmk(name, [a, wr],
                [pl.BlockSpec((tm, ng * kg), lambda i, j: (i, 0)),
                 pl.BlockSpec((ng, None, kg, tn), lambda i, j: (0, l, 0, j))],
                [(0, (ALL, slice(g * kg, (g + 1) * kg)), 1, g, NN) for g in range(ng)],
                (t_rows, n), out_dtype, (t_rows // tm, n // tn), pl.BlockSpec((tm, tn), lambda i, j: (i, j)))


def _rows_nt(name, dy, wr, l, out_dtype):
    t_rows, n = dy.shape
    ng, kg = wr.shape[0], wr.shape[2]
    tm = _tile(t_rows, 512)
    return _mmk(name, [dy, wr],
                [pl.BlockSpec((tm, n), lambda i, g: (i, 0)),
                 pl.BlockSpec((None, None, kg, n), lambda i, g: (g, l, 0, 0))],
                [(0, ALL, 1, ALL, NT)], (t_rows, ng * kg), out_dtype, (t_rows // tm, ng),
                pl.BlockSpec((tm, kg), lambda i, g: (i, g)))


def _rows_tn(name, a, dy, ng, out_dtype):
    t_rows, n = dy.shape
    kg = a.shape[1] // ng
    tn = _tile(n, 512)
    return _mmk(name, [a, dy],
                [pl.BlockSpec((t_rows, kg), lambda g, j: (0, g)), pl.BlockSpec((t_rows, tn), lambda g, j: (0, j))],
                [(0, ALL, 1, ALL, TN)], (ng, kg, n), out_dtype, (ng, n // tn),
                pl.BlockSpec((None, kg, tn), lambda g, j: (g, 0, j)))


def _mix_fwd(tag, w, l, x, mod, seq):
    t_rows, d = x.shape
    bl = t_rows // seq
    shift, scale, gate = mod
    chans, nh = w["layout"]["chans"], w["layout"]["heads"]
    width = nh * HEAD_DIM
    nb = width // LANE
    h = _normmod(tag + "_norm", x, w["mix_norm"][l][None], shift, scale, seq)
    proj = _mm(tag + "_in_a", h, w["w_a"][l], NN, F32)
    qkv = _mm(tag + "_in_b", h, w["w_b"][l], NN, BF16)
    flp = _mm(tag + "_in_f", h, w["w_f"][l], NN, F32)
    xa, r, ig, hs, ya = _rg_fwd(tag + "_rg", proj, w["rg"][l], seq, chans)
    yb, ltot = _sb_fwd(tag + "_sb", qkv, 0, width, bl, seq)
    fl = flp[:, :nh].reshape(bl, seq, nh).transpose(0, 2, 1).reshape(bl * nh, seq)
    bf = jnp.tile(w["fox_bf"][l].reshape(nh, 1), (bl, 1))
    cum = _forget_cum(tag + "_cum", fl, bf)
    cum_q = cum.reshape(bl, nh, seq, 1)
    cum_k = cum.reshape(bl, nh, 1, seq)
    yc, oc, lse = _fox_fwd(tag + "_fox", qkv, 3 * nb, width, bl, seq, cum_q, cum_k)
    pa = _rows_nn(tag + "_prg", ya, w["w_rg"], l, F32)
    pb = _grouped_nn(tag + "_psb", yb, w["w_sb"], l, F32)
    pc = _grouped_nn(tag + "_pfox", yc, w["w_fox"], l, F32)
    moff = 2 * chans // d
    mb = w["merge_b"][l][None]
    mixed = _merge_fwd(tag + "_merge", proj, moff, mb, pa, pb, pc)
    y = _rows_nn(tag + "_out", mixed, w["w_o"], l, F32)
    xn = _resid(tag + "_res", x, y, gate, 1.0, seq)
    saved = dict(x=x, h=h, proj=proj, qkv=qkv, rg=(xa, r, ig, hs), ya=ya, ltot=ltot,
                 fox=(cum_q, cum_k, lse, oc), fl=fl, bf=bf, yb=yb, yc=yc, pa=pa, pb=pb, pc=pc, mixed=mixed, y=y)
    return xn, saved


def _mix_bwd(tag, w, l, s, mod, dxo, seq):
    x = s["x"]
    t_rows, d = x.shape
    bl = t_rows // seq
    shift, scale, gate = mod
    chans, nh = w["layout"]["chans"], w["layout"]["heads"]
    width = nh * HEAD_DIM
    nb = width // LANE
    moff = 2 * chans // d
    mb = w["merge_b"][l][None]
    ng = NUM_CHIPS
    dy, dgate = _resid_bwd(tag + "_res_bwd", dxo, s["y"], gate, 1.0, seq)
    dmixed = _rows_nt(tag + "_out_dx", dy, w["w_o"], l, F32)
    dw_o = _rows_tn(tag + "_out_dw", s["mixed"], dy, ng, BF16)
    dpa, dpb, dpc, dlog, dmb = _merge_bwd(tag + "_merge_bwd", s["proj"], moff, mb, s["pa"], s["pb"], s["pc"], dmixed)
    dya = _rows_nt(tag + "_prg_dx", dpa, w["w_rg"], l, F32)
    dw_rg = _rows_tn(tag + "_prg_dw", s["ya"], dpa, ng, BF16)
    dyb = _grouped_nt(tag + "_psb_dx", dpb, w["w_sb"], l, F32)
    dw_sb = _grouped_tn(tag + "_psb_dw", s["yb"], dpb, ng, BF16)
    dyc = _grouped_nt(tag + "_pfox_dx", dpc, w["w_fox"], l, F32)
    dw_fox = _grouped_tn(tag + "_pfox_dw", s["yc"], dpc, ng, BF16)
    qkv = s["qkv"]
    dq_b, dk_b, dv_b = _sb_bwd(tag + "_sb_bwd", qkv, 0, width, bl, seq, s["ltot"], dyb)
    cum_q, cum_k, lse, oc = s["fox"]
    dq_c, dk_c, dv_c, dcq, dck = _fox_bwd(tag + "_fox_bwd", qkv, 3 * nb, width, bl, seq, cum_q, cum_k, lse, oc, dyc)
    dcum = dcq.reshape(bl * nh, seq) + dck.reshape(bl * nh, seq)
    dfl, dbf = _forget_cum_bwd(tag + "_cum_bwd", s["fl"], s["bf"], dcum, nh)
    dfl_t = dfl.reshape(bl, nh, seq).transpose(0, 2, 1).reshape(t_rows, nh)
    dflp = jnp.pad(dfl_t, ((0, 0), (0, LANE - nh))).astype(BF16)
    drgx, dgt, rg_small, dwa, dwx = _rg_bwd(tag + "_rg_bwd", s["proj"], dya, s["rg"], w["rg"][l], seq, chans)
    dproj = jnp.concatenate([drgx, dgt, dlog], axis=1)
    dqkv = jnp.concatenate([dq_b, dk_b, dv_b, dq_c, dk_c, dv_c], axis=1)
    w_a, w_b, w_f = w["w_a"][l], w["w_b"][l], w["w_f"][l]
    pa_w, pb_w = w_a.shape[1], w_b.shape[1]
    tm, tn = _tile(t_rows, 512), _tile(d, 512)
    rows = lambda n: pl.BlockSpec((tm, n), lambda i, j: (i, 0))
    wrow = lambda n: pl.BlockSpec((tn, n), lambda i, j: (j, 0))
    dh = _mmk(tag + "_in_dx", [dproj, dqkv, dflp, w_a, w_b, w_f],
              [rows(pa_w), rows(pb_w), rows(LANE), wrow(pa_w), wrow(pb_w), wrow(LANE)],
              [(0, ALL, 3, ALL, NT), (1, ALL, 4, ALL, NT), (2, ALL, 5, ALL, NT)],
              (t_rows, d), F32, (t_rows // tm, d // tn), pl.BlockSpec((tm, tn), lambda i, j: (i, j)))
    hb = s["h"]
    dw_a = _mm(tag + "_in_a_dw", hb, dproj, TN, BF16)
    dw_b = _mm(tag + "_in_b_dw", hb, dqkv, TN, BF16)
    dw_f = _mm(tag + "_in_f_dw", hb, dflp, TN, BF16)
    dx, dshift, dscale, dgain = _normmod_bwd(tag + "_norm_bwd", x, dh, dxo, w["mix_norm"][l][None], scale, seq)
    grads = dict(w_in=(dw_a, dw_b, dw_f), w_rg=dw_rg, w_sb=dw_sb, w_fox=dw_fox, w_o=dw_o, mix_norm=dgain,
                 rg_small=rg_small, rg_dwa=dwa, rg_dwx=dwx, fox_bf=dbf, merge_b=dmb)
    return dx, (dshift, dscale, dgate), grads


def _silu(name, c):
    def body(c_ref, o_ref):
        v = c_ref[...]
        o_ref[...] = v * _sigmoid(v)

    return pl.pallas_call(body, name=name, out_shape=jax.ShapeDtypeStruct(c.shape, F32),
                          compiler_params=_params())(c)


def _blockdiag(wb):
    nb, bd, _ = wb.shape
    per = LANE // bd
    t = wb.reshape(nb // per, per, bd, 1, bd)
    eye = jnp.eye(per, dtype=wb.dtype).reshape(1, per, 1, per, 1)
    return (t * eye).reshape(nb // per, LANE, LANE).astype(BF16)


def _unblockdiag(t, bd):
    n = t.shape[0]
    per = LANE // bd
    t5 = t.reshape(n, per, bd, per, bd)
    return jnp.stack([t5[:, p, :, p, :] for p in range(per)], axis=1).reshape(n * per, bd, bd)


def _prepare(gw, a, d, chans, nh):
    depth = a["ada_b"].shape[0]
    wq = 3 * nh * HEAD_DIM
    o_m = 2 * chans + 2 * wq
    w = {"layout": dict(chans=chans, heads=nh)}
    for n in ("ffn1_w1", "ffn1_w3", "ffn1_w2", "ffn2_w1", "ffn2_w3", "ffn2_w2", "w_rg", "w_sb", "w_fox", "w_o"):
        w[n] = gw[n]
    for n in ("ffn1_norm", "ffn2_norm", "mix_norm", "fox_bf", "merge_b", "final_norm"):
        w[n] = a[n]
    w_a, w_b, w_f, rg = [], [], [], []
    for l in range(depth):
        full = gw["w_in"][:, l].transpose(1, 0, 2).reshape(d, -1)
        w_a.append(jnp.concatenate([full[:, :2 * chans], full[:, o_m + nh:]], axis=1))
        w_b.append(full[:, 2 * chans:o_m])
        w_f.append(jnp.pad(full[:, o_m:o_m + nh], ((0, 0), (0, LANE - nh))))
        conv_w = gw["conv_w"][:, l].transpose(1, 0, 2).reshape(-1, chans)
        rg.append(dict(conv_w=conv_w, conv_b=a["conv_b"][l][None], ba=a["rg_ba"][l][None], bx=a["rg_bx"][l][None],
                       lam=a["rg_lam"][l][None], wa=_blockdiag(a["rg_wa"][l]), wx=_blockdiag(a["rg_wx"][l])))
    w["w_a"], w["w_b"], w["w_f"], w["rg"] = w_a, w_b, w_f, rg
    return w


def _local_step(w, x, tgt, mods, fm):
    bl, seq, d = x.shape
    t_rows = bl * seq
    depth = len(mods)
    mod3 = []
    for l in range(depth):
        m4 = mods[l].reshape(bl, 9, 1, d)
        mod3.append([(m4[:, 3 * k], m4[:, 3 * k + 1], m4[:, 3 * k + 2]) for k in range(3)])
    fm4 = fm.reshape(bl, 2, 1, d)
    saved = []
    xc = x.reshape(t_rows, d)
    for l in range(depth):
        xc, s1 = _ffn_fwd(f"l{l}_ffn1", w, l, "ffn1_", xc, mod3[l][0], seq)
        xc, s2 = _mix_fwd(f"l{l}_mix", w, l, xc, mod3[l][1], seq)
        xc, s3 = _ffn_fwd(f"l{l}_ffn2", w, l, "ffn2_", xc, mod3[l][2], seq)
        saved.append((s1, s2, s3))
    lpart, dx, dfs, dfc, dfg = _final_loss("final", xc, tgt.reshape(t_rows, d), w["final_norm"][None],
                                           fm4[:, 0], fm4[:, 1], seq)
    loss = jnp.sum(lpart[:, 0, 0])
    grads = {"final_norm": dfg, "layers": [None] * depth}
    dmods = [None] * depth
    for l in reversed(range(depth)):
        s1, s2, s3 = saved[l]
        dx, dm3, g3 = _ffn_bwd(f"l{l}_ffn2", w, l, "ffn2_", s3, mod3[l][2], dx, seq)
        dx, dm2, g2 = _mix_bwd(f"l{l}_mix", w, l, s2, mod3[l][1], dx, seq)
        dx, dm1, g1 = _ffn_bwd(f"l{l}_ffn1", w, l, "ffn1_", s1, mod3[l][0], dx, seq)
        dmods[l] = jnp.concatenate([*dm1, *dm2, *dm3], axis=1).reshape(bl, 9 * d)
        grads["layers"][l] = {**g1, **g2, **g3}
    dfm = jnp.concatenate([dfs, dfc], axis=1).reshape(bl, 2 * d)
    return loss, dx.reshape(bl, seq, d), grads, dmods, dfm


def _mesh_pos():
    return lax.axis_index("x"), lax.axis_index("y"), lax.axis_index("c")


def _other_chips(x, y):
    return ((1 - x, y), (x, 1 - y), (1 - x, 1 - y))


def _gather_two_level(name, arrs):
    n = len(arrs)

    def body(*refs):
        ins, outs = refs[:n], refs[n:2 * n]
        send, recv, send2, recv2, send3, recv3 = refs[2 * n:]
        x, y, c = _mesh_pos()
        me = 2 * x + y
        chips = _other_chips(x, y)
        sib = (x, y, 1 - c)
        own = [pltpu.make_async_remote_copy(
            src_ref=ins[i], dst_ref=outs[i].at[me], send_sem=send3.at[i], recv_sem=recv3.at[i],
            device_id=sib, device_id_type=MESH) for i in range(n)]
        first = []
        for j, (px, py) in enumerate(chips):
            for i in range(n):
                first.append(pltpu.make_async_remote_copy(
                    src_ref=ins[i].at[c], dst_ref=outs[i].at[me, c], send_sem=send.at[j * n + i],
                    recv_sem=recv.at[j * n + i], device_id=(px, py, c), device_id_type=MESH))
        for cp in first + own:
            cp.start()
        passed = []
        for j, (px, py) in enumerate(chips):
            for i in range(n):
                landed = outs[i].at[2 * px + py, c]
                pltpu.make_async_remote_copy(
                    src_ref=ins[i].at[c], dst_ref=landed, send_sem=send.at[j * n + i],
                    recv_sem=recv.at[j * n + i], device_id=(px, py, c), device_id_type=MESH).wait_recv()
                fwd = pltpu.make_async_remote_copy(
                    src_ref=landed, dst_ref=landed, send_sem=send2.at[j * n + i],
                    recv_sem=recv2.at[j * n + i], device_id=sib, device_id_type=MESH)
                fwd.start()
                passed.append(fwd)
        for j, (px, py) in enumerate(chips):
            for i in range(n):
                theirs = outs[i].at[2 * px + py, 1 - c]
                pltpu.make_async_remote_copy(
                    src_ref=theirs, dst_ref=theirs, send_sem=send2.at[j * n + i],
                    recv_sem=recv2.at[j * n + i], device_id=sib, device_id_type=MESH).wait_recv()
        for cp in first + passed:
            cp.wait_send()
        for cp in own:
            cp.wait()

    return pl.pallas_call(
        body, name=name, in_specs=[ANY] * n, out_specs=[ANY] * n,
        out_shape=[jax.ShapeDtypeStruct((NUM_CHIPS,) + a.shape, a.dtype) for a in arrs],
        scratch_shapes=[pltpu.SemaphoreType.DMA((3 * n,)), pltpu.SemaphoreType.DMA((3 * n,)),
                        pltpu.SemaphoreType.DMA((3 * n,)), pltpu.SemaphoreType.DMA((3 * n,)),
                        pltpu.SemaphoreType.DMA((n,)), pltpu.SemaphoreType.DMA((n,))],
    )(*arrs)


def _split_to_sibling(name, arrs):
    n = len(arrs)
    slabs = arrs[0].shape[0]

    def body(*refs):
        ins, theirs = refs[:n], refs[n:2 * n]
        send, recv = refs[2 * n:]
        x, y, c = _mesh_pos()
        sib = (x, y, 1 - c)
        for i in range(n):
            for s in range(slabs):
                pltpu.make_async_remote_copy(
                    src_ref=ins[i].at[s, 1 - c], dst_ref=theirs[i].at[s], send_sem=send.at[i],
                    recv_sem=recv.at[i], device_id=sib, device_id_type=MESH).start()
        for i in range(n):
            pltpu.make_async_remote_copy(
                src_ref=ins[i].at[:, 0], dst_ref=theirs[i], send_sem=send.at[i], recv_sem=recv.at[i],
                device_id=sib, device_id_type=MESH).wait()

    return pl.pallas_call(
        body, name=name, in_specs=[ANY] * n, out_specs=[ANY] * n,
        out_shape=[jax.ShapeDtypeStruct((a.shape[0],) + a.shape[2:], a.dtype) for a in arrs],
        scratch_shapes=[pltpu.SemaphoreType.DMA((n,)), pltpu.SemaphoreType.DMA((n,))],
    )(*arrs)


def _scatter_chips(name, arrs):
    n = len(arrs)

    def body(*refs):
        ins, outs = refs[:n], refs[n:2 * n]
        send, recv = refs[2 * n:]
        x, y, c = _mesh_pos()
        chips = _other_chips(x, y)
        sends = []
        for j, (px, py) in enumerate(chips):
            for i in range(n):
                sends.append(pltpu.make_async_remote_copy(
                    src_ref=ins[i].at[2 * px + py], dst_ref=outs[i].at[j], send_sem=send.at[j * n + i],
                    recv_sem=recv.at[j * n + i], device_id=(px, py, c), device_id_type=MESH))
        for s in sends:
            s.start()
        for s in sends:
            s.wait()

    return pl.pallas_call(
        body, name=name, in_specs=[ANY] * n, out_specs=[ANY] * n,
        out_shape=[jax.ShapeDtypeStruct((NUM_CHIPS - 1,) + a.shape[1:], a.dtype) for a in arrs],
        scratch_shapes=[pltpu.SemaphoreType.DMA((3 * n,)), pltpu.SemaphoreType.DMA((3 * n,))],
    )(*arrs)


def _join_halves(name, arrs):
    n = len(arrs)

    def body(*refs):
        ins, outs = refs[:n], refs[n:2 * n]
        send, recv = refs[2 * n:]
        x, y, c = _mesh_pos()
        copies = [pltpu.make_async_remote_copy(
            src_ref=ins[i], dst_ref=outs[i], send_sem=send.at[i], recv_sem=recv.at[i],
            device_id=(x, y, 1 - c), device_id_type=MESH) for i in range(n)]
        for cp in copies:
            cp.start()
        for cp in copies:
            cp.wait()

    return pl.pallas_call(
        body, name=name, in_specs=[ANY] * n, out_specs=[ANY] * n,
        out_shape=[jax.ShapeDtypeStruct(a.shape, a.dtype) for a in arrs],
        scratch_shapes=[pltpu.SemaphoreType.DMA((n,)), pltpu.SemaphoreType.DMA((n,))],
    )(*arrs)


def _gather_all(name, pack):
    def body(in_ref, out_ref, send, recv, loc):
        x, y, c = _mesh_pos()
        me = 4 * x + 2 * y + c
        mine = pltpu.make_async_copy(in_ref, out_ref.at[me], loc)
        mine.start()
        peers = []
        for mask in range(1, NUM_DEVICES):
            px = 1 - x if mask & 4 else x
            py = 1 - y if mask & 2 else y
            pc = 1 - c if mask & 1 else c
            peers.append((px, py, pc))
        sends = [pltpu.make_async_remote_copy(
            src_ref=in_ref, dst_ref=out_ref.at[me], send_sem=send.at[k], recv_sem=recv.at[k],
            device_id=p, device_id_type=MESH) for k, p in enumerate(peers)]
        for s in sends:
            s.start()
        for k, (px, py, pc) in enumerate(peers):
            pltpu.make_async_remote_copy(
                src_ref=in_ref, dst_ref=out_ref.at[4 * px + 2 * py + pc], send_sem=send.at[k], recv_sem=recv.at[k],
                device_id=(px, py, pc), device_id_type=MESH).wait_recv()
        for s in sends:
            s.wait_send()
        mine.wait()

    return pl.pallas_call(
        body, name=name, in_specs=[ANY], out_specs=ANY,
        out_shape=jax.ShapeDtypeStruct((NUM_DEVICES,) + pack.shape, pack.dtype),
        scratch_shapes=[pltpu.SemaphoreType.DMA((NUM_DEVICES - 1,)), pltpu.SemaphoreType.DMA((NUM_DEVICES - 1,)),
                        pltpu.SemaphoreType.DMA],
    )(pack)


def _sum_slots(name, slots, out_dtype):
    g, rows, cols = slots.shape
    tr = _rtile(rows, 256)

    def body(s_ref, o_ref):
        acc = s_ref[0].astype(F32)
        for k in range(1, g):
            acc = acc + s_ref[k].astype(F32)
        o_ref[...] = acc.astype(o_ref.dtype)

    return pl.pallas_call(
        body, name=name, grid=(rows // tr,),
        in_specs=[pl.BlockSpec((g, tr, cols), lambda i: (0, i, 0))],
        out_specs=pl.BlockSpec((tr, cols), lambda i: (i, 0)),
        out_shape=jax.ShapeDtypeStruct((rows, cols), out_dtype), compiler_params=_params(),
    )(slots)


def _add_pair(name, p, q, core):
    g, _, rows, cols = p.shape
    tr = _rtile(rows, 128)

    def body(c_ref, p_ref, q_ref, o_ref):
        mine = jnp.where(c_ref[0] == 0, p_ref[:, 0].astype(F32), p_ref[:, 1].astype(F32))
        o_ref[...] = (mine + q_ref[...].astype(F32)).astype(o_ref.dtype)

    spec = pl.BlockSpec((g, tr, cols), lambda i: (0, i, 0))
    return pl.pallas_call(
        body, name=name, grid=(rows // tr,),
        in_specs=[SCALAR, pl.BlockSpec((g, 2, tr, cols), lambda i: (0, 0, i, 0)), spec],
        out_specs=spec, out_shape=jax.ShapeDtypeStruct(q.shape, BF16), compiler_params=_params(),
    )(core, p, q)


def _sum_chips(name, slots, part, chip):
    g, rows, cols = part.shape
    tr = _rtile(rows, 128)

    def body(c_ref, s_ref, p_ref, o_ref):
        acc = p_ref[c_ref[0]].astype(F32)
        for k in range(slots.shape[0]):
            acc = acc + s_ref[k].astype(F32)
        o_ref[...] = acc

    return pl.pallas_call(
        body, name=name, grid=(rows // tr,),
        in_specs=[SCALAR,
                  pl.BlockSpec((slots.shape[0], tr, cols), lambda i: (0, i, 0)),
                  pl.BlockSpec((g, tr, cols), lambda i: (0, i, 0))],
        out_specs=pl.BlockSpec((tr, cols), lambda i: (i, 0)),
        out_shape=jax.ShapeDtypeStruct((rows, cols), F32), compiler_params=_params(),
    )(chip, slots, part)


def _adamw(name, g, w, m, v, l=None):
    rows, cols = g.shape
    tr = _rtile(rows, 128)

    def body(g_ref, w_ref, m_ref, v_ref, d_o, m_o, v_o):
        gv = g_ref[...]
        mn = ADAM_B1 * m_ref[...] + (1.0 - ADAM_B1) * gv
        vn = ADAM_B2 * v_ref[...] + (1.0 - ADAM_B2) * (gv * gv)
        m_hat = mn / (1.0 - ADAM_B1 ** ADAM_STEP)
        v_hat = vn / (1.0 - ADAM_B2 ** ADAM_STEP)
        d_o[...] = -ADAM_LR * (m_hat / (jnp.sqrt(v_hat) + ADAM_EPS) + ADAM_WD * w_ref[...])
        m_o[...] = mn
        v_o[...] = vn

    gspec = pl.BlockSpec((tr, cols), lambda i: (i, 0))
    wspec = gspec if l is None else pl.BlockSpec((None, tr, cols), lambda i: (l, i, 0))
    f = jax.ShapeDtypeStruct((rows, cols), F32)
    return pl.pallas_call(
        body, name=name, grid=(rows // tr,), in_specs=[gspec] + [wspec] * 3, out_specs=[gspec] * 3,
        out_shape=[f] * 3, compiler_params=_params(),
    )(g, w, m, v)


def _adamw_layers(name, g0, g1, w, m, v):
    rows, cols = g0.shape
    tr = _rtile(rows, 128)
    nt = rows // tr

    def body(g0_ref, g1_ref, w_ref, m_ref, v_ref, g_o, d_o, m_o, v_o):
        gv = jnp.where(pl.program_id(0) == 0, g0_ref[...], g1_ref[...])
        _adamw_math(gv, w_ref, m_ref, v_ref, g_o, d_o, m_o, v_o)

    g0spec = pl.BlockSpec((tr, cols), lambda l, i: (i * (1 - l) + (nt - 1) * l, 0))
    g1spec = pl.BlockSpec((tr, cols), lambda l, i: (i * l, 0))
    wspec = pl.BlockSpec((None, tr, cols), lambda l, i: (l, i, 0))
    f = jax.ShapeDtypeStruct((2, rows, cols), F32)
    return pl.pallas_call(
        body, name=name, grid=(2, nt), in_specs=[g0spec, g1spec, wspec, wspec, wspec], out_specs=[wspec] * 4,
        out_shape=[f] * 4, compiler_params=_params(),
    )(g0, g1, w, m, v)


def _adamw_math(gv, w_ref, m_ref, v_ref, g_o, d_o, m_o, v_o):
    mn = ADAM_B1 * m_ref[...] + (1.0 - ADAM_B1) * gv
    vn = ADAM_B2 * v_ref[...] + (1.0 - ADAM_B2) * (gv * gv)
    m_hat = mn / (1.0 - ADAM_B1 ** ADAM_STEP)
    v_hat = vn / (1.0 - ADAM_B2 ** ADAM_STEP)
    g_o[...] = gv
    d_o[...] = -ADAM_LR * (m_hat / (jnp.sqrt(v_hat) + ADAM_EPS) + ADAM_WD * w_ref[...])
    m_o[...] = mn
    v_o[...] = vn


def _adamw_halves(name, mine, theirs, core, w, m, v):
    half, cols = mine[0].shape
    tr = _rtile(half, 128)
    nt = half // tr

    def body(c_ref, a0, b0, a1, b1, w_ref, m_ref, v_ref, g_o, d_o, m_o, v_o):
        first = pl.program_id(0) == 0
        own = pl.program_id(1) == c_ref[0]
        gv = jnp.where(first, jnp.where(own, a0[...], b0[...]), jnp.where(own, a1[...], b1[...]))
        _adamw_math(gv, w_ref, m_ref, v_ref, g_o, d_o, m_o, v_o)

    lay0 = pl.BlockSpec((tr, cols), lambda l, h, i: (i * (1 - l) + (nt - 1) * l, 0))
    lay1 = pl.BlockSpec((tr, cols), lambda l, h, i: (i * l, 0))
    wspec = pl.BlockSpec((None, tr, cols), lambda l, h, i: (l, h * nt + i, 0))
    f = jax.ShapeDtypeStruct((2, 2 * half, cols), F32)
    return pl.pallas_call(
        body, name=name, grid=(2, 2, nt), in_specs=[SCALAR, lay0, lay0, lay1, lay1, wspec, wspec, wspec],
        out_specs=[wspec] * 4, out_shape=[f] * 4, compiler_params=_params(),
    )(core, mine[0], theirs[0], mine[1], theirs[1], w, m, v)


def _colsum(name, a):
    def body(a_ref, o_ref):
        o_ref[...] = jnp.sum(a_ref[...], axis=0, keepdims=True)

    return pl.pallas_call(body, name=name, out_shape=jax.ShapeDtypeStruct((1, a.shape[1]), F32),
                          compiler_params=_params())(a)


PACK_UNIT = SUBLANE * LANE


def _pack(items):
    flat, layout, o = [], [], 0
    for it in items:
        n = it.size
        pad = -n % PACK_UNIT
        flat.append(jnp.pad(it.reshape(-1).astype(F32), (0, pad)))
        layout.append((o, n, it.shape))
        o += n + pad
    return jnp.concatenate(flat).reshape(-1, LANE), layout


def _unpack(pack, layout):
    flat = pack.reshape(-1)
    return [flat[o:o + n].reshape(shape) for o, n, shape in layout]


WEIGHTS = ("ffn1_norm", "ffn1_w1", "ffn1_w3", "ffn1_w2", "mix_norm", "w_in", "conv_w", "conv_b", "rg_wa", "rg_ba",
           "rg_wx", "rg_bx", "rg_lam", "fox_bf", "merge_b", "w_rg", "w_sb", "w_fox", "w_o", "ffn2_norm", "ffn2_w1",
           "ffn2_w3", "ffn2_w2", "ada_w", "ada_b", "final_norm", "final_ada_w", "final_ada_b")
DENSE = ("ffn1_w1", "ffn1_w3", "ffn1_w2", "w_in", "w_rg", "w_sb", "w_fox", "w_o", "ffn2_w1", "ffn2_w3", "ffn2_w2")
SMALL = ("ffn1_norm", "mix_norm", "ffn2_norm", "rg_small", "rg_wa", "rg_wx", "fox_bf", "merge_b")


def _step(a):
    x, c, tgt = a["x"], a["c"], a["loss_target"]
    bl, seq, d = x.shape
    depth, nh = a["fox_bf"].shape
    chans = a["rg_lam"].shape[1]
    bd = a["rg_wa"].shape[2]
    wq = 3 * nh * HEAD_DIM
    o_m = 2 * chans + 2 * wq
    batch = NUM_DEVICES * bl
    mx, my, mc = _mesh_pos()
    me = 2 * mx + my
    dev = 4 * mx + 2 * my + mc

    c_rows = -(-bl * d // LANE // SUBLANE) * SUBLANE
    c_pack = jnp.pad(c.reshape(-1, LANE), ((0, c_rows - bl * d // LANE), (0, 0)))
    c_all = _gather_all("gather_c", c_pack)[:, :bl * d // LANE].reshape(batch, d)
    c_act = _silu("c_act", c_all)
    c_b = c_act.astype(BF16)
    ncol, fcol = a["ada_w"].shape[2], a["final_ada_w"].shape[1]
    cols = []
    for l in range(depth):
        bias = jnp.broadcast_to(lax.dynamic_slice_in_dim(a["ada_b"][l], me * ncol, ncol)[None], (batch, ncol))
        cols.append(_mm(f"ada{l}", c_b, a["ada_w"][l].astype(BF16), NN, F32, acc=bias))
    bias = jnp.broadcast_to(lax.dynamic_slice_in_dim(a["final_ada_b"], me * fcol, fcol)[None], (batch, fcol))
    cols.append(_mm("ada_final", c_b, a["final_ada_w"].astype(BF16), NN, F32, acc=bias))
    mod_cols = jnp.concatenate(cols, axis=1).reshape(2, batch // 2, depth * ncol + fcol)

    names = DENSE + ("conv_w", "mod_cols")
    got = _gather_two_level("gather_weights", [a[n].astype(BF16) for n in DENSE] + [a["conv_w"], mod_cols])
    gw = dict(zip(names, got))
    w = _prepare(gw, a, d, chans, nh)
    mod_all = gw["mod_cols"].reshape(NUM_CHIPS, batch, -1)
    mine = lambda full: lax.dynamic_slice_in_dim(full, dev * bl, bl, axis=0)
    mods = [mine(mod_all[:, :, l * ncol:(l + 1) * ncol].transpose(1, 0, 2).reshape(batch, NUM_CHIPS * ncol))
            for l in range(depth)]
    fm = mine(mod_all[:, :, depth * ncol:].transpose(1, 0, 2).reshape(batch, NUM_CHIPS * fcol))

    loss, grad_x, grads, dmods, dfm = _local_step(w, x, tgt, mods, fm)
    loss = lax.psum(loss, ("x", "y", "c"))

    rs_in = []
    for l in range(depth):
        for n in DENSE:
            if n == "w_in":
                ga, gb, gf = grads["layers"][l]["w_in"]
                orig = jnp.concatenate([ga[:, :2 * chans], gb, gf[:, :nh], ga[:, 2 * chans:]], axis=1)
                rs_in.append(orig.reshape(d, NUM_CHIPS, -1).transpose(1, 0, 2))
            else:
                rs_in.append(grads["layers"][l][n])
    rs_in = [g.reshape(g.shape[0], 2, g.shape[1] // 2, g.shape[2]) for g in rs_in]
    core = jnp.reshape(mc, (1,)).astype(jnp.int32)
    chip = jnp.reshape(me, (1,)).astype(jnp.int32)
    theirs = _split_to_sibling("split_grads", rs_in)
    chip_part = [_add_pair(f"add_cores_{k}", g, t, core) for k, (g, t) in enumerate(zip(rs_in, theirs))]
    slots = _scatter_chips("scatter_grads", chip_part)
    reduced = [_sum_chips(f"sum_chips_{k}", s, p, chip) for k, (s, p) in enumerate(zip(slots, chip_part))]
    other = _join_halves("join_grads", reduced)

    out = {}

    def put(n, res, per_layer):
        for kind, val in zip(("grad_", "delta_", "new_m_", "new_v_"), res):
            out[kind + n] = jnp.stack(val).reshape(a[n].shape) if per_layer else val.reshape(a[n].shape)

    def flat3(v):
        return v.reshape(depth, -1, v.shape[-1])

    assert depth == 2
    nd = len(DENSE)
    for k, n in enumerate(DENSE):
        put(n, _adamw_halves(f"adamw_{n}", (reduced[k], reduced[nd + k]), (other[k], other[nd + k]), core,
                             flat3(a[n]), flat3(a["m_" + n]), flat3(a["v_" + n])), False)

    items = []
    for l in range(depth):
        g = grads["layers"][l]
        items += [g["ffn1_norm"], g["mix_norm"], g["ffn2_norm"], g["rg_small"], _unblockdiag(g["rg_dwa"], bd),
                  _unblockdiag(g["rg_dwx"], bd), g["fox_bf"], g["merge_b"], dmods[l]]
    items += [grads["final_norm"], dfm]
    pack, layout = _pack(items)
    gath = _gather_all("gather_small", pack)
    tot = _sum_slots("sum_small", gath, F32)

    def wpack(pre):
        its = []
        for l in range(depth):
            rg_rows = jnp.concatenate([jnp.zeros((4, chans), F32), a[pre + "conv_b"][l][None], a[pre + "rg_ba"][l][None],
                                       a[pre + "rg_bx"][l][None], a[pre + "rg_lam"][l][None]], axis=0)
            its += [a[pre + "ffn1_norm"][l], a[pre + "mix_norm"][l], a[pre + "ffn2_norm"][l], rg_rows,
                    a[pre + "rg_wa"][l], a[pre + "rg_wx"][l], a[pre + "fox_bf"][l], a[pre + "merge_b"][l],
                    jnp.zeros((bl, 9 * d), F32)]
        its += [a[pre + "final_norm"], jnp.zeros((bl, 2 * d), F32)]
        return _pack(its)[0]

    res_small = [_unpack(r, layout) for r in [tot] + list(_adamw("adamw_small", tot, wpack(""), wpack("m_"), wpack("v_")))]
    per = len(SMALL) + 1
    for j, n in enumerate(SMALL):
        if n == "rg_small":
            for row, nm in ((4, "conv_b"), (5, "rg_ba"), (6, "rg_bx"), (7, "rg_lam")):
                put(nm, [[r[l * per + j][row] for l in range(depth)] for r in res_small], True)
        else:
            put(n, [[r[l * per + j] for l in range(depth)] for r in res_small], True)
    put("final_norm", [r[depth * per] for r in res_small], False)

    gflat = gath.reshape(NUM_DEVICES, -1)

    def rows_of(idx):
        o, n, shape = layout[idx]
        return gflat[:, o:o + n].reshape(NUM_DEVICES * shape[0], shape[1])

    late_g, ada = [], []
    for l in range(depth):
        dmod_all = rows_of(l * per + per - 1)
        late_g.append(_colsum(f"ada_b_grad_{l}", dmod_all))
        cut = lax.dynamic_slice_in_dim(dmod_all, me * ncol, ncol, axis=1).astype(BF16)
        ada.append(_mm(f"ada_w_grad_{l}", c_b, cut, TN, F32))
    put("ada_w", _adamw_layers("adamw_ada_w", ada[0], ada[1], a["ada_w"], a["m_ada_w"], a["v_ada_w"]), False)
    dfm_all = rows_of(depth * per + 1)
    late_g.append(_colsum("final_ada_b_grad", dfm_all))
    cut = lax.dynamic_slice_in_dim(dfm_all, me * fcol, fcol, axis=1).astype(BF16)
    gl = _mm("final_ada_w_grad", c_b, cut, TN, F32)
    put("final_ada_w", [gl] + list(_adamw("adamw_final_ada_w", gl, a["final_ada_w"], a["m_final_ada_w"],
                                          a["v_final_ada_w"])), False)
    cshard = a["conv_w"].shape[2]
    for l in range(depth):
        rg_tot = res_small[0][l * per + SMALL.index("rg_small")]
        late_g.append(lax.dynamic_slice_in_dim(rg_tot[:4], me * cshard, cshard, axis=1))
    gp2, layout2 = _pack(late_g)

    def wpack2(pre):
        return _pack([a[pre + "ada_b"][l][None] for l in range(depth)] + [a[pre + "final_ada_b"][None]]
                     + [a[pre + "conv_w"][l] for l in range(depth)])[0]

    res_late = [_unpack(r, layout2) for r in [gp2] + list(_adamw("adamw_late", gp2, wpack2(""), wpack2("m_"), wpack2("v_")))]
    put("ada_b", [[r[l] for l in range(depth)] for r in res_late], True)
    put("final_ada_b", [r[depth] for r in res_late], False)
    put("conv_w", [[r[depth + 1 + l] for l in range(depth)] for r in res_late], True)

    outs = [loss, grad_x]
    for kind in ("grad_", "delta_", "new_m_", "new_v_"):
        outs += [out[kind + n] for n in WEIGHTS]
    return tuple(outs)


def kernel(x, c, ffn1_norm, ffn1_w1, ffn1_w3, ffn1_w2, mix_norm, w_in, conv_w, conv_b, rg_wa, rg_ba, rg_wx, rg_bx, rg_lam, fox_bf, merge_b, w_rg, w_sb, w_fox, w_o, ffn2_norm, ffn2_w1, ffn2_w3, ffn2_w2, ada_w, ada_b, final_norm, final_ada_w, final_ada_b, loss_target, m_ffn1_norm, m_ffn1_w1, m_ffn1_w3, m_ffn1_w2, m_mix_norm, m_w_in, m_conv_w, m_conv_b, m_rg_wa, m_rg_ba, m_rg_wx, m_rg_bx, m_rg_lam, m_fox_bf, m_merge_b, m_w_rg, m_w_sb, m_w_fox, m_w_o, m_ffn2_norm, m_ffn2_w1, m_ffn2_w3, m_ffn2_w2, m_ada_w, m_ada_b, m_final_norm, m_final_ada_w, m_final_ada_b, v_ffn1_norm, v_ffn1_w1, v_ffn1_w3, v_ffn1_w2, v_mix_norm, v_w_in, v_conv_w, v_conv_b, v_rg_wa, v_rg_ba, v_rg_wx, v_rg_bx, v_rg_lam, v_fox_bf, v_merge_b, v_w_rg, v_w_sb, v_w_fox, v_w_o, v_ffn2_norm, v_ffn2_w1, v_ffn2_w3, v_ffn2_w2, v_ada_w, v_ada_b, v_final_norm, v_final_ada_w, v_final_ada_b):
    args = dict(locals())
    return _step(args)
```

```python
import math

import jax
import jax.numpy as jnp
from jax import lax
from jax.experimental import pallas as pl
from jax.experimental.pallas import tpu as pltpu

F32 = jnp.float32
BF16 = jnp.bfloat16

NUM_CHIPS = 4
NUM_DEVICES = 8
HEAD_DIM = 64
LANE = 128
SUBLANE = 8
VMEM_LIMIT = 56 * 1024 * 1024
EPS = 1e-6
RG_C = 8.0
ADAM_LR = 0.001
ADAM_B1 = 0.9
ADAM_B2 = 0.999
ADAM_EPS = 1e-08
ADAM_WD = 0.01
ADAM_STEP = 10
MESH = pl.DeviceIdType.MESH
ANY = pl.BlockSpec(memory_space=pl.ANY)
SCALAR = pl.BlockSpec(memory_space=pltpu.SMEM)


def _params():
    return pltpu.CompilerParams(vmem_limit_bytes=VMEM_LIMIT)


def _tile(dim, pref):
    if dim <= pref:
        return dim
    t = (pref // LANE) * LANE
    while t >= LANE:
        if dim % t == 0:
            return t
        t -= LANE
    return dim


def _rtile(rows, pref, unit=2 * SUBLANE):
    if rows <= pref:
        return rows
    t = (pref // unit) * unit
    while t >= unit:
        if rows % t == 0:
            return t
        t -= unit
    return rows


def _sigmoid(x):
    return 0.5 * jnp.tanh(0.5 * x) + 0.5


def _softplus(x):
    return jnp.maximum(x, 0.0) + jnp.log(1.0 + jnp.exp(-jnp.abs(x)))


def _expm1(x):
    small = x * (1.0 + x * (0.5 + x * (1.0 / 6.0 + x * (1.0 / 24.0))))
    return jnp.where(jnp.abs(x) < 0.01, small, jnp.exp(x) - 1.0)


_GELU_K = math.sqrt(2.0 / math.pi)


def _gelu_and_grad(x):
    inner = _GELU_K * (x + 0.044715 * x * x * x)
    t = jnp.tanh(inner)
    val = 0.5 * x * (1.0 + t)
    dinner = _GELU_K * (1.0 + 3.0 * 0.044715 * x * x)
    grad = 0.5 * (1.0 + t) + 0.5 * x * (1.0 - t * t) * dinner
    return val, grad


NN = ((1,), (0,))
NT = ((1,), (1,))
TN = ((0,), (0,))
ALL = slice(None)


def _mmk(name, ops, specs, terms, out_shape, out_dtype, grid, o_spec, acc=None):
    n_ops = len(ops)

    def body(*refs):
        o_ref = refs[-1]
        p = None
        for ia, xa, ib, xb, dims in terms:
            t = lax.dot_general(refs[ia][xa], refs[ib][xb], (dims, ((), ())), preferred_element_type=F32)
            p = t if p is None else p + t
        if acc is not None:
            p = p + refs[n_ops][...].astype(F32)
        o_ref[...] = p.astype(o_ref.dtype)

    in_specs = list(specs)
    args = list(ops)
    if acc is not None:
        in_specs.append(pl.BlockSpec(o_spec.block_shape, o_spec.index_map))
        args.append(acc)
    return pl.pallas_call(
        body, name=name, grid=grid, in_specs=in_specs, out_specs=o_spec,
        out_shape=jax.ShapeDtypeStruct(out_shape, out_dtype), compiler_params=_params(),
    )(*args)


def _mm(name, a, b, dims, out_dtype, acc=None, tm=512, tn=512):
    if dims == NN:
        (m, kk), n = a.shape, b.shape[1]
    elif dims == NT:
        (m, kk), n = a.shape, b.shape[0]
    else:
        (kk, m), n = a.shape, b.shape[1]
    tm, tn = _tile(m, tm), _tile(n, tn)
    if dims == TN:
        a_spec = pl.BlockSpec((kk, tm), lambda i, j: (0, i))
    else:
        a_spec = pl.BlockSpec((tm, kk), lambda i, j: (i, 0))
    if dims == NT:
        b_spec = pl.BlockSpec((tn, kk), lambda i, j: (j, 0))
    else:
        b_spec = pl.BlockSpec((kk, tn), lambda i, j: (0, j))
    return _mmk(name, [a, b], [a_spec, b_spec], [(0, ALL, 1, ALL, dims)], (m, n), out_dtype,
                (m // tm, n // tn), pl.BlockSpec((tm, tn), lambda i, j: (i, j)), acc)


def _row_grid(t_rows, seq, pref=256):
    tm = _tile(seq, pref)
    return tm, seq // tm


def _normmod(name, x, gain, shift, scale, seq):
    t_rows, d = x.shape
    bl = t_rows // seq
    tm, per = _row_grid(t_rows, seq)

    def body(x_ref, g_ref, sh_ref, sc_ref, o_ref):
        xv = x_ref[...]
        rstd = lax.rsqrt(jnp.mean(xv * xv, axis=-1, keepdims=True) + EPS)
        hn = (xv * rstd) * g_ref[...]
        o_ref[...] = (hn * (1.0 + sc_ref[...]) + sh_ref[...]).astype(o_ref.dtype)

    row = pl.BlockSpec((tm, d), lambda b, i: (b * per + i, 0))
    vec = pl.BlockSpec((None, 1, d), lambda b, i: (b, 0, 0))
    return pl.pallas_call(
        body, name=name, grid=(bl, per),
        in_specs=[row, pl.BlockSpec((1, d), lambda b, i: (0, 0)), vec, vec],
        out_specs=row, out_shape=jax.ShapeDtypeStruct((t_rows, d), BF16),
        compiler_params=_params(),
    )(x, gain, shift, scale)


def _normmod_bwd(name, x, dh, dxo, gain, scale, seq):
    t_rows, d = x.shape
    bl = t_rows // seq
    tm, per = _row_grid(t_rows, seq)

    def body(x_ref, dh_ref, dxo_ref, g_ref, sc_ref, dx_ref, dsh_ref, dsc_ref, dg_ref):
        b, i = pl.program_id(0), pl.program_id(1)
        xv = x_ref[...]
        dhv = dh_ref[...]
        rstd = lax.rsqrt(jnp.mean(xv * xv, axis=-1, keepdims=True) + EPS)
        xhat = xv * rstd
        gain_v = g_ref[...]
        dhn = dhv * (1.0 + sc_ref[...])
        dxhat = dhn * gain_v
        dx = rstd * (dxhat - xhat * jnp.mean(dxhat * xhat, axis=-1, keepdims=True))
        dx_ref[...] = dxo_ref[...] + dx

        @pl.when(i == 0)
        def _():
            dsh_ref[...] = jnp.zeros_like(dsh_ref)
            dsc_ref[...] = jnp.zeros_like(dsc_ref)

        @pl.when((i == 0) & (b == 0))
        def _():
            dg_ref[...] = jnp.zeros_like(dg_ref)

        dsh_ref[...] += jnp.sum(dhv, axis=0, keepdims=True)
        dsc_ref[...] += jnp.sum(dhv * (xhat * gain_v), axis=0, keepdims=True)
        dg_ref[...] += jnp.sum(dhn * xhat, axis=0, keepdims=True)

    row = pl.BlockSpec((tm, d), lambda b, i: (b * per + i, 0))
    vec = pl.BlockSpec((None, 1, d), lambda b, i: (b, 0, 0))
    one = pl.BlockSpec((1, d), lambda b, i: (0, 0))
    return pl.pallas_call(
        body, name=name, grid=(bl, per),
        in_specs=[row, row, row, one, vec],
        out_specs=[row, vec, vec, one],
        out_shape=[jax.ShapeDtypeStruct((t_rows, d), F32), jax.ShapeDtypeStruct((bl, 1, d), F32),
                   jax.ShapeDtypeStruct((bl, 1, d), F32), jax.ShapeDtypeStruct((1, d), F32)],
        compiler_params=_params(),
    )(x, dh, dxo, gain, scale)


def _resid(name, x, y, gate, coef, seq):
    t_rows, d = x.shape
    bl = t_rows // seq
    tm, per = _row_grid(t_rows, seq)

    def body(x_ref, y_ref, g_ref, o_ref):
        o_ref[...] = x_ref[...] + (coef * (1.0 + g_ref[...])) * y_ref[...]

    row = pl.BlockSpec((tm, d), lambda b, i: (b * per + i, 0))
    vec = pl.BlockSpec((None, 1, d), lambda b, i: (b, 0, 0))
    return pl.pallas_call(
        body, name=name, grid=(bl, per), in_specs=[row, row, vec], out_specs=row,
        out_shape=jax.ShapeDtypeStruct((t_rows, d), F32), compiler_params=_params(),
    )(x, y, gate)


def _resid_bwd(name, dxo, y, gate, coef, seq):
    t_rows, d = dxo.shape
    bl = t_rows // seq
    tm, per = _row_grid(t_rows, seq)

    def body(dxo_ref, y_ref, g_ref, dy_ref, dg_ref):
        i = pl.program_id(1)
        dxov = dxo_ref[...]
        dy_ref[...] = ((coef * (1.0 + g_ref[...])) * dxov).astype(dy_ref.dtype)

        @pl.when(i == 0)
        def _():
            dg_ref[...] = jnp.zeros_like(dg_ref)

        dg_ref[...] += jnp.sum((coef * y_ref[...]) * dxov, axis=0, keepdims=True)

    row = pl.BlockSpec((tm, d), lambda b, i: (b * per + i, 0))
    vec = pl.BlockSpec((None, 1, d), lambda b, i: (b, 0, 0))
    return pl.pallas_call(
        body, name=name, grid=(bl, per), in_specs=[row, row, vec], out_specs=[row, vec],
        out_shape=[jax.ShapeDtypeStruct((t_rows, d), BF16), jax.ShapeDtypeStruct((bl, 1, d), F32)],
        compiler_params=_params(),
    )(dxo, y, gate)


def _final_loss(name, x, tgt, gain, shift, scale, seq):
    t_rows, d = x.shape
    bl = t_rows // seq
    tm, per = _row_grid(t_rows, seq)

    def body(x_ref, t_ref, g_ref, sh_ref, sc_ref, l_ref, dx_ref, dsh_ref, dsc_ref, dg_ref):
        b, i = pl.program_id(0), pl.program_id(1)
        xv = x_ref[...]
        rstd = lax.rsqrt(jnp.mean(xv * xv, axis=-1, keepdims=True) + EPS)
        xhat = xv * rstd
        gain_v = g_ref[...]
        hn = xhat * gain_v
        yv = hn * (1.0 + sc_ref[...]) + sh_ref[...]
        err = yv - t_ref[...]
        dyv = err * (1.0 / d)
        dhn = dyv * (1.0 + sc_ref[...])
        dxhat = dhn * gain_v
        dx_ref[...] = rstd * (dxhat - xhat * jnp.mean(dxhat * xhat, axis=-1, keepdims=True))

        @pl.when(i == 0)
        def _():
            l_ref[...] = jnp.zeros_like(l_ref)
            dsh_ref[...] = jnp.zeros_like(dsh_ref)
            dsc_ref[...] = jnp.zeros_like(dsc_ref)

        @pl.when((i == 0) & (b == 0))
        def _():
            dg_ref[...] = jnp.zeros_like(dg_ref)

        part = jnp.sum(jnp.sum(err * err, axis=-1, keepdims=True), axis=0, keepdims=True) * (0.5 / d)
        l_ref[...] += jnp.broadcast_to(part, l_ref.shape)
        dsh_ref[...] += jnp.sum(dyv, axis=0, keepdims=True)
        dsc_ref[...] += jnp.sum(dyv * hn, axis=0, keepdims=True)
        dg_ref[...] += jnp.sum(dhn * xhat, axis=0, keepdims=True)

    row = pl.BlockSpec((tm, d), lambda b, i: (b * per + i, 0))
    vec = pl.BlockSpec((None, 1, d), lambda b, i: (b, 0, 0))
    one = pl.BlockSpec((1, d), lambda b, i: (0, 0))
    lvec = pl.BlockSpec((None, 1, LANE), lambda b, i: (b, 0, 0))
    return pl.pallas_call(
        body, name=name, grid=(bl, per),
        in_specs=[row, row, one, vec, vec],
        out_specs=[lvec, row, vec, vec, one],
        out_shape=[jax.ShapeDtypeStruct((bl, 1, LANE), F32), jax.ShapeDtypeStruct((t_rows, d), F32),
                   jax.ShapeDtypeStruct((bl, 1, d), F32), jax.ShapeDtypeStruct((bl, 1, d), F32),
                   jax.ShapeDtypeStruct((1, d), F32)],
        compiler_params=_params(),
    )(x, tgt, gain, shift, scale)


def _ffn_up(name, h, w1, w3, l):
    t_rows, d = h.shape
    ng, fs = w1.shape[0], w1.shape[3]
    tm = _tile(t_rows, 512)

    def body(h_ref, w1_ref, w3_ref, a_ref, b_ref, g_ref):
        hv = h_ref[...]
        av = jnp.dot(hv, w1_ref[...], preferred_element_type=F32)
        bv = jnp.dot(hv, w3_ref[...], preferred_element_type=F32)
        a_ref[...] = av.astype(a_ref.dtype)
        b_ref[...] = bv.astype(b_ref.dtype)
        g_ref[...] = (av * _sigmoid(av) * bv).astype(g_ref.dtype)

    wspec = pl.BlockSpec((None, None, d, fs), lambda g, i: (g, l, 0, 0))
    out = pl.BlockSpec((None, tm, fs), lambda g, i: (g, i, 0))
    f = jax.ShapeDtypeStruct((ng, t_rows, fs), BF16)
    return pl.pallas_call(
        body, name=name, grid=(ng, t_rows // tm),
        in_specs=[pl.BlockSpec((tm, d), lambda g, i: (i, 0)), wspec, wspec], out_specs=[out, out, out],
        out_shape=[f, f, f], compiler_params=_params(),
    )(h, w1, w3)


def _ffn_down_dx(name, dy, w2, a, b, l):
    t_rows, d = dy.shape
    ng, fs = w2.shape[0], w2.shape[2]
    tm = _tile(t_rows, 512)

    def body(dy_ref, w2_ref, a_ref, b_ref, da_ref, db_ref):
        dgv = lax.dot_general(dy_ref[...], w2_ref[...], (NT, ((), ())), preferred_element_type=F32)
        av = a_ref[...].astype(F32)
        sig = _sigmoid(av)
        da_ref[...] = (dgv * b_ref[...].astype(F32) * (sig * (1.0 + av * (1.0 - sig)))).astype(da_ref.dtype)
        db_ref[...] = (dgv * (av * sig)).astype(db_ref.dtype)

    blk = pl.BlockSpec((None, tm, fs), lambda g, i: (g, i, 0))
    o = jax.ShapeDtypeStruct((ng, t_rows, fs), BF16)
    return pl.pallas_call(
        body, name=name, grid=(ng, t_rows // tm),
        in_specs=[pl.BlockSpec((tm, d), lambda g, i: (i, 0)),
                  pl.BlockSpec((None, None, fs, d), lambda g, i: (g, l, 0, 0)), blk, blk],
        out_specs=[blk, blk], out_shape=[o, o], compiler_params=_params(),
    )(dy, w2, a, b)


def _ffn_up_dw(name, h, da, db):
    t_rows, d = h.shape
    ng, fs = da.shape[0], da.shape[2]
    tn = _tile(d, 512)

    def body(h_ref, da_ref, db_ref, o1_ref, o3_ref):
        hv = h_ref[...]
        o1_ref[...] = lax.dot_general(hv, da_ref[...], (TN, ((), ())), preferred_element_type=F32).astype(o1_ref.dtype)
        o3_ref[...] = lax.dot_general(hv, db_ref[...], (TN, ((), ())), preferred_element_type=F32).astype(o3_ref.dtype)

    dspec = pl.BlockSpec((None, t_rows, fs), lambda g, i: (g, 0, 0))
    out = pl.BlockSpec((None, tn, fs), lambda g, i: (g, i, 0))
    o = jax.ShapeDtypeStruct((ng, d, fs), BF16)
    return pl.pallas_call(
        body, name=name, grid=(ng, d // tn),
        in_specs=[pl.BlockSpec((t_rows, tn), lambda g, i: (0, i)), dspec, dspec],
        out_specs=[out, out], out_shape=[o, o], compiler_params=_params(),
    )(h, da, db)


def _ffn_fwd(tag, w, l, pre, x, mod, seq):
    t_rows, d = x.shape
    w1, w3, w2 = w[pre + "w1"], w[pre + "w3"], w[pre + "w2"]
    ng, fs = w1.shape[0], w1.shape[3]
    shift, scale, gate = mod
    h = _normmod(tag + "_norm", x, w[pre + "norm"][l][None], shift, scale, seq)
    a, b, gact = _ffn_up(tag + "_up", h, w1, w3, l)
    tm, tn = _tile(t_rows, 512), _tile(d, 512)
    y = _mmk(tag + "_down", [gact, w2],
             [pl.BlockSpec((ng, tm, fs), lambda i, j: (0, i, 0)),
              pl.BlockSpec((ng, None, fs, tn), lambda i, j: (0, l, 0, j))],
             [(0, g, 1, g, NN) for g in range(ng)], (t_rows, d), F32, (t_rows // tm, d // tn),
             pl.BlockSpec((tm, tn), lambda i, j: (i, j)))
    xn = _resid(tag + "_res", x, y, gate, 0.5, seq)
    return xn, (x, h, a, b, gact, y)


def _ffn_bwd(tag, w, l, pre, saved, mod, dxo, seq):
    x, h, a, b, gact, y = saved
    t_rows, d = x.shape
    w1, w3, w2 = w[pre + "w1"], w[pre + "w3"], w[pre + "w2"]
    ng, fs = w1.shape[0], w1.shape[3]
    shift, scale, gate = mod
    tm, tn = _tile(t_rows, 512), _tile(d, 512)
    dy, dgate = _resid_bwd(tag + "_res_bwd", dxo, y, gate, 0.5, seq)
    da, db = _ffn_down_dx(tag + "_down_dx", dy, w2, a, b, l)
    dw2 = _mmk(tag + "_down_dw", [gact, dy],
               [pl.BlockSpec((None, t_rows, fs), lambda g, j: (g, 0, 0)),
                pl.BlockSpec((t_rows, tn), lambda g, j: (0, j))],
               [(0, ALL, 1, ALL, TN)], (ng, fs, d), BF16, (ng, d // tn),
               pl.BlockSpec((None, fs, tn), lambda g, j: (g, 0, j)))
    dw1, dw3 = _ffn_up_dw(tag + "_up_dw", h, da, db)
    dspec = pl.BlockSpec((ng, tm, fs), lambda i, j: (0, i, 0))
    wspec = pl.BlockSpec((ng, None, tn, fs), lambda i, j: (0, l, j, 0))
    dh = _mmk(tag + "_up_dx", [da, db, w1, w3], [dspec, dspec, wspec, wspec],
              [(0, g, 2, g, NT) for g in range(ng)] + [(1, g, 3, g, NT) for g in range(ng)],
              (t_rows, d), F32, (t_rows // tm, d // tn), pl.BlockSpec((tm, tn), lambda i, j: (i, j)))
    dx, dshift, dscale, dgain = _normmod_bwd(tag + "_norm_bwd", x, dh, dxo, w[pre + "norm"][l][None], scale, seq)
    grads = {pre + "w1": dw1, pre + "w3": dw3, pre + "w2": dw2, pre + "norm": dgain}
    return dx, (dshift, dscale, dgate), grads


def _shift_down(v, s, row):
    if s == 0:
        return v
    return jnp.where(row >= s, pltpu.roll(v, s, 0), 0.0)


def _shift_up(v, s, row):
    if s == 0:
        return v
    n = v.shape[0]
    return jnp.where(row < n - s, pltpu.roll(v, n - s, 0), 0.0)


def _scan_fwd(a, u, row):
    n = a.shape[0]
    s = 1
    while s < n:
        ok = row >= s
        a_sh = pltpu.roll(a, s, 0)
        u_sh = pltpu.roll(u, s, 0)
        u = jnp.where(ok, a * u_sh + u, u)
        a = jnp.where(ok, a * a_sh, a)
        s *= 2
    return u


def _scan_bwd(a_next, g, row):
    n = g.shape[0]
    a, u = a_next, g
    s = 1
    while s < n:
        ok = row < n - s
        a_sh = pltpu.roll(a, n - s, 0)
        u_sh = pltpu.roll(u, n - s, 0)
        u = jnp.where(ok, a * u_sh + u, u)
        a = jnp.where(ok, a * a_sh, a)
        s *= 2
    return u


def _rg_specs(seq, cw):
    slab = lambda off: pl.BlockSpec((seq, cw), lambda c, b: (b, off + c))
    par = lambda rows: pl.BlockSpec((rows, cw), lambda c, b: (0, c))
    wbd = pl.BlockSpec((None, cw, cw), lambda c, b: (c, 0, 0))
    return slab, par, wbd


def _rg_fwd(name, proj, p, seq, chans):
    t_rows = proj.shape[0]
    bl = t_rows // seq
    cw = LANE
    nc = chans // cw
    slab, par, wbd = _rg_specs(seq, cw)

    def body(x_ref, gt_ref, cw_ref, cb_ref, wa_ref, ba_ref, wx_ref, bx_ref, lam_ref,
             xa_ref, r_ref, i_ref, h_ref, ya_ref):
        row = lax.broadcasted_iota(jnp.int32, (seq, cw), 0)
        xv = x_ref[...]
        xa = jnp.zeros_like(xv) + cb_ref[...]
        for k in range(4):
            xa = xa + cw_ref[k:k + 1, :] * _shift_down(xv, 3 - k, row)
        xab = xa.astype(BF16)
        r = _sigmoid(jnp.dot(xab, wa_ref[...], preferred_element_type=F32) + ba_ref[...])
        ig = _sigmoid(jnp.dot(xab, wx_ref[...], preferred_element_type=F32) + bx_ref[...])
        log_a = (-RG_C) * r * _softplus(-lam_ref[...])
        a = jnp.exp(log_a)
        u = jnp.sqrt(-_expm1(2.0 * log_a)) * (ig * xa)
        h = _scan_fwd(a, u, row)
        gel, _ = _gelu_and_grad(gt_ref[...])
        xa_ref[...] = xa
        r_ref[...] = r
        i_ref[...] = ig
        h_ref[...] = h
        ya_ref[...] = (gel * h).astype(ya_ref.dtype)

    out = pl.BlockSpec((seq, cw), lambda c, b: (b, c))
    f = jax.ShapeDtypeStruct((t_rows, chans), F32)
    return pl.pallas_call(
        body, name=name, grid=(nc, bl),
        in_specs=[slab(0), slab(nc), par(4), par(1), wbd, par(1), wbd, par(1), par(1)],
        out_specs=[out] * 5,
        out_shape=[f, f, f, f, jax.ShapeDtypeStruct((t_rows, chans), BF16)],
        compiler_params=_params(),
    )(proj, proj, p["conv_w"], p["conv_b"], p["wa"], p["ba"], p["wx"], p["bx"], p["lam"])


def _rg_bwd(name, proj, dya, saved, p, seq, chans):
    xa_s, r_s, i_s, h_s = saved
    t_rows = proj.shape[0]
    bl = t_rows // seq
    cw = LANE
    nc = chans // cw
    slab, par, wbd = _rg_specs(seq, cw)

    def body(x_ref, gt_ref, dya_ref, xa_ref, r_ref, i_ref, h_ref, cw_ref, wa_ref, wx_ref, lam_ref,
             dx_ref, dgt_ref, sm_ref, dwa_ref, dwx_ref):
        b = pl.program_id(1)
        row = lax.broadcasted_iota(jnp.int32, (seq, cw), 0)
        xv, xa, r, ig, h = x_ref[...], xa_ref[...], r_ref[...], i_ref[...], h_ref[...]
        dyav = dya_ref[...]
        gel, dgel = _gelu_and_grad(gt_ref[...])
        dgt_ref[...] = (dyav * h * dgel).astype(dgt_ref.dtype)
        dh = dyav * gel
        lam = lam_ref[...]
        sp = _softplus(-lam)
        log_a = (-RG_C) * r * sp
        a = jnp.exp(log_a)
        s = jnp.sqrt(-_expm1(2.0 * log_a))
        lamb = _scan_bwd(_shift_up(a, 1, row), dh, row)
        da = lamb * _shift_down(h, 1, row)
        xi = ig * xa
        ds = lamb * xi
        dxi = lamb * s
        dlog = da * a - ds * (a * a) / s
        dr = dlog * ((-RG_C) * sp)
        dsp = jnp.sum(dlog * ((-RG_C) * r), axis=0, keepdims=True)
        dlam = -dsp * _sigmoid(-lam)
        dzr = dr * r * (1.0 - r)
        dzi = (dxi * xa) * ig * (1.0 - ig)
        dzrb, dzib, xab = dzr.astype(BF16), dzi.astype(BF16), xa.astype(BF16)
        dxa = dxi * ig
        dxa = dxa + lax.dot_general(dzrb, wa_ref[...], (NT, ((), ())), preferred_element_type=F32)
        dxa = dxa + lax.dot_general(dzib, wx_ref[...], (NT, ((), ())), preferred_element_type=F32)
        dwa = lax.dot_general(xab, dzrb, (TN, ((), ())), preferred_element_type=F32)
        dwx = lax.dot_general(xab, dzib, (TN, ((), ())), preferred_element_type=F32)
        dxv = jnp.zeros_like(xv)
        rows = []
        for k in range(4):
            dxv = dxv + cw_ref[k:k + 1, :] * _shift_up(dxa, 3 - k, row)
            rows.append(jnp.sum(dxa * _shift_down(xv, 3 - k, row), axis=0, keepdims=True))
        dx_ref[...] = dxv.astype(dx_ref.dtype)
        rows += [jnp.sum(dxa, axis=0, keepdims=True), jnp.sum(dzr, axis=0, keepdims=True),
                 jnp.sum(dzi, axis=0, keepdims=True), dlam]

        @pl.when(b == 0)
        def _():
            sm_ref[...] = jnp.zeros_like(sm_ref)
            dwa_ref[...] = jnp.zeros_like(dwa_ref)
            dwx_ref[...] = jnp.zeros_like(dwx_ref)

        for k, val in enumerate(rows):
            sm_ref[k:k + 1, :] += val
        dwa_ref[...] += dwa
        dwx_ref[...] += dwx

    plain = pl.BlockSpec((seq, cw), lambda c, b: (b, c))
    return pl.pallas_call(
        body, name=name, grid=(nc, bl),
        in_specs=[slab(0), slab(nc), plain, plain, plain, plain, plain, par(4), wbd, wbd, par(1)],
        out_specs=[plain, plain, par(8), wbd, wbd],
        out_shape=[jax.ShapeDtypeStruct((t_rows, chans), BF16), jax.ShapeDtypeStruct((t_rows, chans), BF16),
                   jax.ShapeDtypeStruct((8, chans), F32),
                   jax.ShapeDtypeStruct((nc, cw, cw), F32), jax.ShapeDtypeStruct((nc, cw, cw), F32)],
        compiler_params=_params(),
    )(proj, proj, dya, xa_s, r_s, i_s, h_s, p["conv_w"], p["wa"], p["wx"], p["lam"])


ATT_Q_BLOCK = 512
SB_K_BLOCK = 256
FOX_K_BLOCK = 512
PAIR = LANE // HEAD_DIM
NEG = -1e30
SCALE = HEAD_DIM ** -0.5
assert math.log2(HEAD_DIM) % 2 == 0


def _att_blocks(seq, k_block):
    return _tile(seq, ATT_Q_BLOCK), _tile(seq, k_block)


def _key_blocks(qi, tq, bk):
    return (qi * tq) // bk, (qi * tq + tq - 1) // bk + 1


def _tri(n, kind):
    r = lax.broadcasted_iota(jnp.int32, (2 * n, n), 0)
    r = jnp.where(r >= n, r - n, r)
    c = lax.broadcasted_iota(jnp.int32, (2 * n, n), 1)
    m = {"gt": r > c, "le": r <= c, "lt": r < c}[kind]
    return m.astype(BF16)


def _cumsum_mm(v, tri):
    hi = v.astype(BF16)
    lo = (v - hi.astype(F32)).astype(BF16)
    return jnp.dot(jnp.concatenate([hi, lo], axis=1), tri, preferred_element_type=F32)


def _head_masks():
    lane = lax.broadcasted_iota(jnp.int32, (1, LANE), 1)
    return [(lane >= h * HEAD_DIM) & (lane < (h + 1) * HEAD_DIM) for h in range(PAIR)]


def _only(mask, v):
    return jnp.where(mask, v, jnp.zeros_like(v))


def _stack_heads(v, masks):
    return jnp.concatenate([_only(m, v) for m in masks], axis=0)


def _unstack_heads(v, masks):
    tq = v.shape[0] // PAIR
    out = _only(masks[0], v[0:tq])
    for h in range(1, PAIR):
        out = out + _only(masks[h], v[h * tq:(h + 1) * tq])
    return out


def _stacked_iotas(tq, bk):
    row = lax.broadcasted_iota(jnp.int32, (PAIR * tq, bk), 0)
    for h in range(1, PAIR):
        row = jnp.where(row >= h * tq, row - tq, row)
    return row, lax.broadcasted_iota(jnp.int32, (PAIR * tq, bk), 1)


def _att_specs(seq, blk, nq, off):
    npair = None
    qs = lambda o: pl.BlockSpec((blk, LANE), lambda b, p, i: (b * nq + i, o + p))
    ks = lambda o: pl.BlockSpec((seq, LANE), lambda b, p, i: (b, o + p))
    col = pl.BlockSpec((None, PAIR, blk, 1), lambda b, p, i: (b, p, i, 0))
    lane = pl.BlockSpec((None, PAIR, 1, seq), lambda b, p, i: (b, p, 0, 0))
    return qs, ks, col, lane


def _sb_fwd(name, qkv, off, width, bl, seq):
    t_rows = qkv.shape[0]
    tq, bk = _att_blocks(seq, SB_K_BLOCK)
    nq = seq // tq
    nb = width // LANE
    qs, ks, col, _ = _att_specs(seq, tq, nq, off)

    def body(q_ref, k_ref, v_ref, o_ref, lt_ref):
        qi = pl.program_id(2)
        masks = _head_masks()
        qs_ = _stack_heads(q_ref[...] * SCALE, masks)
        row, cix = _stacked_iotas(tq, bk)
        tri = _tri(bk, "gt")

        def step(masked, top):
            def go(it, carry):
                acc, cl = carry
                kb = top - it
                ks_ = pl.multiple_of(kb * bk, bk)
                kv = k_ref[pl.ds(ks_, bk), :]
                vv = v_ref[pl.ds(ks_, bk), :]
                strict = (kb * bk + cix) < (qi * tq + row)
                z = lax.dot_general(qs_, kv, (NT, ((), ())), preferred_element_type=F32)
                sp = _softplus(z)
                lk = jnp.where(strict, -sp, 0.0) if masked else -sp
                wgt = jnp.exp(z - sp + (cl + _cumsum_mm(lk, tri)))
                if masked:
                    wgt = jnp.where(strict, wgt, 0.0)
                acc = acc + _unstack_heads(jnp.dot(wgt.astype(BF16), vv, preferred_element_type=F32), masks)
                return acc, cl + jnp.sum(lk, axis=1, keepdims=True)
            return go

        n_full, n_all = _key_blocks(qi, tq, bk)
        carry = (jnp.zeros((tq, LANE), F32), jnp.zeros((PAIR * tq, 1), F32))
        carry = lax.fori_loop(0, n_all - n_full, step(True, n_all - 1), carry)
        acc, cl = lax.fori_loop(0, n_full, step(False, n_full - 1), carry)
        o_ref[...] = acc.astype(o_ref.dtype)
        for h in range(PAIR):
            lt_ref[h] = cl[h * tq:(h + 1) * tq]

    return pl.pallas_call(
        body, name=name, grid=(bl, nb, nq), in_specs=[qs(off), ks(off + nb), ks(off + 2 * nb)],
        out_specs=[qs(0), col],
        out_shape=[jax.ShapeDtypeStruct((t_rows, width), BF16),
                   jax.ShapeDtypeStruct((bl, nb * PAIR, seq, 1), F32)],
        compiler_params=_params(),
    )(qkv, qkv, qkv)


def _sb_bwd(name, qkv, off, width, bl, seq, ltot, do):
    t_rows = qkv.shape[0]
    tq, bk = _att_blocks(seq, SB_K_BLOCK)
    nq = seq // tq
    nb = width // LANE
    qs, ks, col, _ = _att_specs(seq, tq, nq, off)

    def body(q_ref, k_ref, v_ref, lt_ref, do_ref, dq_ref, dk_ref, dv_ref, dk_acc, dv_acc):
        qi = pl.program_id(2)

        @pl.when(qi == 0)
        def _():
            dk_acc[...] = jnp.zeros_like(dk_acc)
            dv_acc[...] = jnp.zeros_like(dv_acc)

        masks = _head_masks()
        qs_ = _stack_heads(q_ref[...] * SCALE, masks)
        dos = _stack_heads(do_ref[...].astype(BF16), masks)
        lts = jnp.concatenate([lt_ref[h] for h in range(PAIR)], axis=0)
        row, cix = _stacked_iotas(tq, bk)
        tri_le = _tri(bk, "le")
        tri_lt = _tri(bk, "lt")

        def step(masked):
            def go(kb, carry):
                dq, cl, ce = carry
                ks_ = pl.multiple_of(kb * bk, bk)
                kv = k_ref[pl.ds(ks_, bk), :]
                vv = v_ref[pl.ds(ks_, bk), :]
                strict = (kb * bk + cix) < (qi * tq + row)
                z = lax.dot_general(qs_, kv, (NT, ((), ())), preferred_element_type=F32)
                sp = _softplus(z)
                lk = jnp.where(strict, -sp, 0.0) if masked else -sp
                sig = jnp.exp(z - sp)
                wgt = sig * jnp.exp(lts - cl - _cumsum_mm(lk, tri_le))
                if masked:
                    wgt = jnp.where(strict, wgt, 0.0)
                dw = lax.dot_general(dos, vv, (NT, ((), ())), preferred_element_type=F32)
                e = dw * wgt
                pre = ce + _cumsum_mm(e, tri_lt)
                dz = e * (1.0 - sig) - pre * sig
                if masked:
                    dz = jnp.where(strict, dz, 0.0)
                dzb = dz.astype(BF16)
                dq = dq + _unstack_heads(jnp.dot(dzb, kv * SCALE, preferred_element_type=F32), masks)
                dk_acc[pl.ds(ks_, bk), :] += lax.dot_general(dzb, qs_, (TN, ((), ())), preferred_element_type=F32)
                dv_acc[pl.ds(ks_, bk), :] += lax.dot_general(wgt.astype(BF16), dos, (TN, ((), ())),
                                                             preferred_element_type=F32)
                return dq, cl + jnp.sum(lk, axis=1, keepdims=True), ce + jnp.sum(e, axis=1, keepdims=True)
            return go

        n_full, n_all = _key_blocks(qi, tq, bk)
        zero = jnp.zeros((PAIR * tq, 1), F32)
        carry = lax.fori_loop(0, n_full, step(False), (jnp.zeros((tq, LANE), F32), zero, zero))
        dq, _, _ = lax.fori_loop(n_full, n_all, step(True), carry)
        dq_ref[...] = dq.astype(dq_ref.dtype)

        @pl.when(qi == nq - 1)
        def _():
            dk_ref[...] = dk_acc[...].astype(dk_ref.dtype)
            dv_ref[...] = dv_acc[...].astype(dv_ref.dtype)

    o = jax.ShapeDtypeStruct((t_rows, width), BF16)
    return pl.pallas_call(
        body, name=name, grid=(bl, nb, nq),
        in_specs=[qs(off), ks(off + nb), ks(off + 2 * nb), col, qs(0)], out_specs=[qs(0), ks(0), ks(0)],
        out_shape=[o, o, o], scratch_shapes=[pltpu.VMEM((seq, LANE), F32), pltpu.VMEM((seq, LANE), F32)],
        compiler_params=_params(),
    )(qkv, qkv, qkv, ltot, do)


def _fox_fwd(name, qkv, off, width, bl, seq, cum_q, cum_k):
    t_rows = qkv.shape[0]
    tq, bk = _att_blocks(seq, FOX_K_BLOCK)
    nq = seq // tq
    nb = width // LANE
    qs, ks, col, lane = _att_specs(seq, tq, nq, off)

    def body(q_ref, k_ref, v_ref, cq_ref, ck_ref, ob_ref, of_ref, lse_ref):
        qi = pl.program_id(2)
        masks = _head_masks()
        qs_ = _stack_heads(q_ref[...] * SCALE, masks)
        cqs = jnp.concatenate([cq_ref[h] for h in range(PAIR)], axis=0)
        row, cix = _stacked_iotas(tq, bk)

        def step(masked):
            def go(kb, carry):
                m, lsum, acc = carry
                ks_ = pl.multiple_of(kb * bk, bk)
                kv = k_ref[pl.ds(ks_, bk), :]
                vv = v_ref[pl.ds(ks_, bk), :]
                bias = jnp.concatenate([cqs[h * tq:(h + 1) * tq] - ck_ref[h, :, pl.ds(ks_, bk)] for h in range(PAIR)],
                                       axis=0)
                z = lax.dot_general(qs_, kv, (NT, ((), ())), preferred_element_type=F32) + bias
                if masked:
                    z = jnp.where((kb * bk + cix) <= (qi * tq + row), z, NEG)
                m_new = jnp.maximum(m, jnp.max(z, axis=1, keepdims=True))
                pv = jnp.exp(z - m_new)
                alpha = jnp.exp(m - m_new)
                lsum = alpha * lsum + jnp.sum(pv, axis=1, keepdims=True)
                acc = alpha * acc + jnp.dot(pv.astype(BF16), vv, preferred_element_type=F32)
                return m_new, lsum, acc
            return go

        n_full, n_all = _key_blocks(qi, tq, bk)
        init = (jnp.full((PAIR * tq, 1), NEG, F32), jnp.zeros((PAIR * tq, 1), F32),
                jnp.zeros((PAIR * tq, LANE), F32))
        carry = lax.fori_loop(0, n_full, step(False), init)
        m, lsum, acc = lax.fori_loop(n_full, n_all, step(True), carry)
        out = _unstack_heads(acc / lsum, masks)
        ob_ref[...] = out.astype(ob_ref.dtype)
        of_ref[...] = out
        lse = m + jnp.log(lsum)
        for h in range(PAIR):
            lse_ref[h] = lse[h * tq:(h + 1) * tq]

    return pl.pallas_call(
        body, name=name, grid=(bl, nb, nq),
        in_specs=[qs(off), ks(off + nb), ks(off + 2 * nb), col, lane], out_specs=[qs(0), qs(0), col],
        out_shape=[jax.ShapeDtypeStruct((t_rows, width), BF16), jax.ShapeDtypeStruct((t_rows, width), F32),
                   jax.ShapeDtypeStruct((bl, nb * PAIR, seq, 1), F32)],
        compiler_params=_params(),
    )(qkv, qkv, qkv, cum_q, cum_k)


def _fox_bwd(name, qkv, off, width, bl, seq, cum_q, cum_k, lse, o, do):
    t_rows = qkv.shape[0]
    tq, bk = _att_blocks(seq, FOX_K_BLOCK)
    nq = seq // tq
    nb = width // LANE
    qs, ks, col, lane = _att_specs(seq, tq, nq, off)

    def body(q_ref, k_ref, v_ref, cq_ref, ck_ref, lse_ref, o_ref, do_ref,
             dq_ref, dk_ref, dv_ref, dcq_ref, dck_ref, dk_acc, dv_acc):
        qi = pl.program_id(2)

        @pl.when(qi == 0)
        def _():
            dk_acc[...] = jnp.zeros_like(dk_acc)
            dv_acc[...] = jnp.zeros_like(dv_acc)
            dck_ref[...] = jnp.zeros_like(dck_ref)

        masks = _head_masks()
        qs_ = _stack_heads(q_ref[...] * SCALE, masks)
        dof = do_ref[...]
        dos = _stack_heads(dof.astype(BF16), masks)
        prod = dof * o_ref[...]
        delta = jnp.concatenate([jnp.sum(_only(m, prod), axis=1, keepdims=True) for m in masks], axis=0)
        shift = jnp.concatenate([cq_ref[h] - lse_ref[h] for h in range(PAIR)], axis=0)
        row, cix = _stacked_iotas(tq, bk)

        def step(masked):
            def go(kb, carry):
                dq, dcq = carry
                ks_ = pl.multiple_of(kb * bk, bk)
                kv = k_ref[pl.ds(ks_, bk), :]
                vv = v_ref[pl.ds(ks_, bk), :]
                bias = jnp.concatenate(
                    [shift[h * tq:(h + 1) * tq] - ck_ref[h, :, pl.ds(ks_, bk)] for h in range(PAIR)], axis=0)
                pv = jnp.exp(lax.dot_general(qs_, kv, (NT, ((), ())), preferred_element_type=F32) + bias)
                if masked:
                    pv = jnp.where((kb * bk + cix) <= (qi * tq + row), pv, 0.0)
                dp = lax.dot_general(dos, vv, (NT, ((), ())), preferred_element_type=F32)
                ds = pv * (dp - delta)
                dsb = ds.astype(BF16)
                dq = dq + _unstack_heads(jnp.dot(dsb, kv * SCALE, preferred_element_type=F32), masks)
                dk_acc[pl.ds(ks_, bk), :] += lax.dot_general(dsb, qs_, (TN, ((), ())), preferred_element_type=F32)
                dv_acc[pl.ds(ks_, bk), :] += lax.dot_general(pv.astype(BF16), dos, (TN, ((), ())),
                                                             preferred_element_type=F32)
                for h in range(PAIR):
                    dck_ref[h, :, pl.ds(ks_, bk)] += -jnp.sum(ds[h * tq:(h + 1) * tq], axis=0, keepdims=True)
                return dq, dcq + jnp.sum(ds, axis=1, keepdims=True)
            return go

        n_full, n_all = _key_blocks(qi, tq, bk)
        carry = lax.fori_loop(0, n_full, step(False), (jnp.zeros((tq, LANE), F32), jnp.zeros((PAIR * tq, 1), F32)))
        dq, dcq = lax.fori_loop(n_full, n_all, step(True), carry)
        dq_ref[...] = dq.astype(dq_ref.dtype)
        for h in range(PAIR):
            dcq_ref[h] = dcq[h * tq:(h + 1) * tq]

        @pl.when(qi == nq - 1)
        def _():
            dk_ref[...] = dk_acc[...].astype(dk_ref.dtype)
            dv_ref[...] = dv_acc[...].astype(dv_ref.dtype)

    ob = jax.ShapeDtypeStruct((t_rows, width), BF16)
    nh = nb * PAIR
    return pl.pallas_call(
        body, name=name, grid=(bl, nb, nq),
        in_specs=[qs(off), ks(off + nb), ks(off + 2 * nb), col, lane, col, qs(0), qs(0)],
        out_specs=[qs(0), ks(0), ks(0), col, lane],
        out_shape=[ob, ob, ob, jax.ShapeDtypeStruct((bl, nh, seq, 1), F32), jax.ShapeDtypeStruct((bl, nh, 1, seq), F32)],
        scratch_shapes=[pltpu.VMEM((seq, LANE), F32), pltpu.VMEM((seq, LANE), F32)],
        compiler_params=_params(),
    )(qkv, qkv, qkv, cum_q, cum_k, lse, o, do)


def _lane_cumsum(v, reverse):
    n = v.shape[1]
    cix = lax.broadcasted_iota(jnp.int32, v.shape, 1)
    s = 1
    while s < n:
        if reverse:
            v = v + jnp.where(cix < n - s, pltpu.roll(v, n - s, 1), 0.0)
        else:
            v = v + jnp.where(cix >= s, pltpu.roll(v, s, 1), 0.0)
        s *= 2
    return v


def _forget_cum(name, fl, bf):
    def body(fl_ref, bf_ref, o_ref):
        xv = fl_ref[...] + bf_ref[...]
        o_ref[...] = _lane_cumsum(-_softplus(-xv), False)

    return pl.pallas_call(body, name=name, out_shape=jax.ShapeDtypeStruct(fl.shape, F32),
                          compiler_params=_params())(fl, bf)


def _forget_cum_bwd(name, fl, bf, dcum, nh):
    rows = fl.shape[0]

    def body(fl_ref, bf_ref, dc_ref, dfl_ref, dbf_ref):
        xv = fl_ref[...] + bf_ref[...]
        dlogf = _lane_cumsum(dc_ref[...], True)
        dfl = dlogf * _sigmoid(-xv)
        dfl_ref[...] = dfl
        per_row = jnp.sum(dfl, axis=1, keepdims=True)
        tot = per_row[0:nh]
        for b in range(1, rows // nh):
            tot = tot + per_row[b * nh:(b + 1) * nh]
        dbf_ref[...] = tot

    return pl.pallas_call(
        body, name=name,
        out_shape=[jax.ShapeDtypeStruct(fl.shape, F32), jax.ShapeDtypeStruct((nh, 1), F32)],
        compiler_params=_params(),
    )(fl, bf, dcum)


def _merge_fwd(name, proj, off, merge_b, pa, pb, pc):
    t_rows, d = pa.shape
    tm = _tile(t_rows, 256)

    def body(l0, l1, l2, mb, a_ref, b_ref, c_ref, o_ref):
        g0 = _sigmoid(l0[...] + mb[:, 0:d])
        g1 = _sigmoid(l1[...] + mb[:, d:2 * d])
        g2 = _sigmoid(l2[...] + mb[:, 2 * d:3 * d])
        o_ref[...] = (g0 * a_ref[...] + g1 * b_ref[...] + g2 * c_ref[...]).astype(o_ref.dtype)

    row = pl.BlockSpec((tm, d), lambda i: (i, 0))
    lg = lambda j: pl.BlockSpec((tm, d), lambda i: (i, off + j))
    return pl.pallas_call(
        body, name=name, grid=(t_rows // tm,),
        in_specs=[lg(0), lg(1), lg(2), pl.BlockSpec((1, 3 * d), lambda i: (0, 0)), row, row, row],
        out_specs=row, out_shape=jax.ShapeDtypeStruct((t_rows, d), BF16), compiler_params=_params(),
    )(proj, proj, proj, merge_b, pa, pb, pc)


def _merge_bwd(name, proj, off, merge_b, pa, pb, pc, dmixed):
    t_rows, d = pa.shape
    tm = _tile(t_rows, 256)

    def body(l0, l1, l2, mb, a_ref, b_ref, c_ref, dm_ref, da_ref, db_ref, dc_ref, dl_ref, dmb_ref):
        i = pl.program_id(0)
        dm = dm_ref[...]
        parts = []
        for j, (lref, pref, dref) in enumerate(((l0, a_ref, da_ref), (l1, b_ref, db_ref), (l2, c_ref, dc_ref))):
            g = _sigmoid(lref[...] + mb[:, j * d:(j + 1) * d])
            dref[...] = (g * dm).astype(dref.dtype)
            dl = dm * pref[...] * g * (1.0 - g)
            dl_ref[:, j * d:(j + 1) * d] = dl.astype(dl_ref.dtype)
            parts.append(jnp.sum(dl, axis=0, keepdims=True))
        tot = jnp.concatenate(parts, axis=1)

        @pl.when(i == 0)
        def _():
            dmb_ref[...] = tot

        @pl.when(i > 0)
        def _():
            dmb_ref[...] += tot

    row = pl.BlockSpec((tm, d), lambda i: (i, 0))
    lg = lambda j: pl.BlockSpec((tm, d), lambda i: (i, off + j))
    one = pl.BlockSpec((1, 3 * d), lambda i: (0, 0))
    b16 = jax.ShapeDtypeStruct((t_rows, d), BF16)
    return pl.pallas_call(
        body, name=name, grid=(t_rows // tm,),
        in_specs=[lg(0), lg(1), lg(2), one, row, row, row, row],
        out_specs=[row, row, row, pl.BlockSpec((tm, 3 * d), lambda i: (i, 0)), one],
        out_shape=[b16, b16, b16, jax.ShapeDtypeStruct((t_rows, 3 * d), BF16), jax.ShapeDtypeStruct((1, 3 * d), F32)],
        compiler_params=_params(),
    )(proj, proj, proj, merge_b, pa, pb, pc, dmixed)


def _grouped_nn(name, a, wg, l, out_dtype):
    t_rows, kk = a.shape
    ng, ncol = wg.shape[0], wg.shape[3]
    tm = _tile(t_rows, 512)
    return _mmk(name, [a, wg],
                [pl.BlockSpec((tm, kk), lambda i, g: (i, 0)),
                 pl.BlockSpec((None, None, kk, ncol), lambda i, g: (g, l, 0, 0))],
                [(0, ALL, 1, ALL, NN)], (t_rows, ng * ncol), out_dtype, (t_rows // tm, ng),
                pl.BlockSpec((tm, ncol), lambda i, g: (i, g)))


def _grouped_nt(name, da, wg, l, out_dtype):
    t_rows = da.shape[0]
    ng, kk, ncol = wg.shape[0], wg.shape[2], wg.shape[3]
    tm = _tile(t_rows, 512)
    return _mmk(name, [da, wg],
                [pl.BlockSpec((tm, ng * ncol), lambda i: (i, 0)),
                 pl.BlockSpec((ng, None, kk, ncol), lambda i: (0, l, 0, 0))],
                [(0, (ALL, slice(g * ncol, (g + 1) * ncol)), 1, g, NT) for g in range(ng)],
                (t_rows, kk), out_dtype, (t_rows // tm,), pl.BlockSpec((tm, kk), lambda i: (i, 0)))


def _grouped_tn(name, a, da, ng, out_dtype):
    t_rows, kk = a.shape
    ncol = da.shape[1] // ng
    return _mmk(name, [a, da],
                [pl.BlockSpec((t_rows, kk), lambda g: (0, 0)), pl.BlockSpec((t_rows, ncol), lambda g: (0, g))],
                [(0, ALL, 1, ALL, TN)], (ng, kk, ncol), out_dtype, (ng,),
                pl.BlockSpec((None, kk, ncol), lambda g: (g, 0, 0)))


def _rows_nn(name, a, wr, l, out_dtype):
    t_rows = a.shape[0]
    ng, kg, n = wr.shape[0], wr.shape[2], wr.shape[3]
    tm, tn = _tile(t_rows, 512), _tile(n, 512)
    return _mmk(name, [a, wr],
                [pl.BlockSpec((tm, ng * kg), lambda i, j: (i, 0)),
                 pl.BlockSpec((ng, None, kg, tn), lambda i, j: (0, l, 0, j))],
                [(0, (ALL, slice(g * kg, (g + 1) * kg)), 1, g, NN) for g in range(ng)],
                (t_rows, n), out_dtype, (t_rows // tm, n // tn), pl.BlockSpec((tm, tn), lambda i, j: (i, j)))


def _rows_nt(name, dy, wr, l, out_dtype):
    t_rows, n = dy.shape
    ng, kg = wr.shape[0], wr.shape[2]
    tm = _tile(t_rows, 512)
    return _mmk(name, [dy, wr],
                [pl.BlockSpec((tm, n), lambda i, g: (i, 0)),
                 pl.BlockSpec((None, None, kg, n), lambda i, g: (g, l, 0, 0))],
                [(0, ALL, 1, ALL, NT)], (t_rows, ng * kg), out_dtype, (t_rows // tm, ng),
                pl.BlockSpec((tm, kg), lambda i, g: (i, g)))


def _rows_tn(name, a, dy, ng, out_dtype):
    t_rows, n = dy.shape
    kg = a.shape[1] // ng
    tn = _tile(n, 512)
    return _mmk(name, [a, dy],
                [pl.BlockSpec((t_rows, kg), lambda g, j: (0, g)), pl.BlockSpec((t_rows, tn), lambda g, j: (0, j))],
                [(0, ALL, 1, ALL, TN)], (ng, kg, n), out_dtype, (ng, n // tn),
                pl.BlockSpec((None, kg, tn), lambda g, j: (g, 0, j)))


def _mix_fwd(tag, w, l, x, mod, seq):
    t_rows, d = x.shape
    bl = t_rows // seq
    shift, scale, gate = mod
    chans, nh = w["layout"]["chans"], w["layout"]["heads"]
    width = nh * HEAD_DIM
    nb = width // LANE
    h = _normmod(tag + "_norm", x, w["mix_norm"][l][None], shift, scale, seq)
    proj = _mm(tag + "_in_a", h, w["w_a"][l], NN, F32)
    qkv = _mm(tag + "_in_b", h, w["w_b"][l], NN, BF16)
    flp = _mm(tag + "_in_f", h, w["w_f"][l], NN, F32)
    xa, r, ig, hs, ya = _rg_fwd(tag + "_rg", proj, w["rg"][l], seq, chans)
    yb, ltot = _sb_fwd(tag + "_sb", qkv, 0, width, bl, seq)
    fl = flp[:, :nh].reshape(bl, seq, nh).transpose(0, 2, 1).reshape(bl * nh, seq)
    bf = jnp.tile(w["fox_bf"][l].reshape(nh, 1), (bl, 1))
    cum = _forget_cum(tag + "_cum", fl, bf)
    cum_q = cum.reshape(bl, nh, seq, 1)
    cum_k = cum.reshape(bl, nh, 1, seq)
    yc, oc, lse = _fox_fwd(tag + "_fox", qkv, 3 * nb, width, bl, seq, cum_q, cum_k)
    pa = _rows_nn(tag + "_prg", ya, w["w_rg"], l, F32)
    pb = _grouped_nn(tag + "_psb", yb, w["w_sb"], l, F32)
    pc = _grouped_nn(tag + "_pfox", yc, w["w_fox"], l, F32)
    moff = 2 * chans // d
    mb = w["merge_b"][l][None]
    mixed = _merge_fwd(tag + "_merge", proj, moff, mb, pa, pb, pc)
    y = _rows_nn(tag + "_out", mixed, w["w_o"], l, F32)
    xn = _resid(tag + "_res", x, y, gate, 1.0, seq)
    saved = dict(x=x, h=h, proj=proj, qkv=qkv, rg=(xa, r, ig, hs), ya=ya, ltot=ltot,
                 fox=(cum_q, cum_k, lse, oc), fl=fl, bf=bf, yb=yb, yc=yc, pa=pa, pb=pb, pc=pc, mixed=mixed, y=y)
    return xn, saved


def _mix_bwd(tag, w, l, s, mod, dxo, seq):
    x = s["x"]
    t_rows, d = x.shape
    bl = t_rows // seq
    shift, scale, gate = mod
    chans, nh = w["layout"]["chans"], w["layout"]["heads"]
    width = nh * HEAD_DIM
    nb = width // LANE
    moff = 2 * chans // d
    mb = w["merge_b"][l][None]
    ng = NUM_CHIPS
    dy, dgate = _resid_bwd(tag + "_res_bwd", dxo, s["y"], gate, 1.0, seq)
    dmixed = _rows_nt(tag + "_out_dx", dy, w["w_o"], l, F32)
    dw_o = _rows_tn(tag + "_out_dw", s["mixed"], dy, ng, BF16)
    dpa, dpb, dpc, dlog, dmb = _merge_bwd(tag + "_merge_bwd", s["proj"], moff, mb, s["pa"], s["pb"], s["pc"], dmixed)
    dya = _rows_nt(tag + "_prg_dx", dpa, w["w_rg"], l, F32)
    dw_rg = _rows_tn(tag + "_prg_dw", s["ya"], dpa, ng, BF16)
    dyb = _grouped_nt(tag + "_psb_dx", dpb, w["w_sb"], l, F32)
    dw_sb = _grouped_tn(tag + "_psb_dw", s["yb"], dpb, ng, BF16)
    dyc = _grouped_nt(tag + "_pfox_dx", dpc, w["w_fox"], l, F32)
    dw_fox = _grouped_tn(tag + "_pfox_dw", s["yc"], dpc, ng, BF16)
    qkv = s["qkv"]
    dq_b, dk_b, dv_b = _sb_bwd(tag + "_sb_bwd", qkv, 0, width, bl, seq, s["ltot"], dyb)
    cum_q, cum_k, lse, oc = s["fox"]
    dq_c, dk_c, dv_c, dcq, dck = _fox_bwd(tag + "_fox_bwd", qkv, 3 * nb, width, bl, seq, cum_q, cum_k, lse, oc, dyc)
    dcum = dcq.reshape(bl * nh, seq) + dck.reshape(bl * nh, seq)
    dfl, dbf = _forget_cum_bwd(tag + "_cum_bwd", s["fl"], s["bf"], dcum, nh)
    dfl_t = dfl.reshape(bl, nh, seq).transpose(0, 2, 1).reshape(t_rows, nh)
    dflp = jnp.pad(dfl_t, ((0, 0), (0, LANE - nh))).astype(BF16)
    drgx, dgt, rg_small, dwa, dwx = _rg_bwd(tag + "_rg_bwd", s["proj"], dya, s["rg"], w["rg"][l], seq, chans)
    dproj = jnp.concatenate([drgx, dgt, dlog], axis=1)
    dqkv = jnp.concatenate([dq_b, dk_b, dv_b, dq_c, dk_c, dv_c], axis=1)
    w_a, w_b, w_f = w["w_a"][l], w["w_b"][l], w["w_f"][l]
    pa_w, pb_w = w_a.shape[1], w_b.shape[1]
    tm, tn = _tile(t_rows, 512), _tile(d, 512)
    rows = lambda n: pl.BlockSpec((tm, n), lambda i, j: (i, 0))
    wrow = lambda n: pl.BlockSpec((tn, n), lambda i, j: (j, 0))
    dh = _mmk(tag + "_in_dx", [dproj, dqkv, dflp, w_a, w_b, w_f],
              [rows(pa_w), rows(pb_w), rows(LANE), wrow(pa_w), wrow(pb_w), wrow(LANE)],
              [(0, ALL, 3, ALL, NT), (1, ALL, 4, ALL, NT), (2, ALL, 5, ALL, NT)],
              (t_rows, d), F32, (t_rows // tm, d // tn), pl.BlockSpec((tm, tn), lambda i, j: (i, j)))
    hb = s["h"]
    dw_a = _mm(tag + "_in_a_dw", hb, dproj, TN, BF16)
    dw_b = _mm(tag + "_in_b_dw", hb, dqkv, TN, BF16)
    dw_f = _mm(tag + "_in_f_dw", hb, dflp, TN, BF16)
    dx, dshift, dscale, dgain = _normmod_bwd(tag + "_norm_bwd", x, dh, dxo, w["mix_norm"][l][None], scale, seq)
    grads = dict(w_in=(dw_a, dw_b, dw_f), w_rg=dw_rg, w_sb=dw_sb, w_fox=dw_fox, w_o=dw_o, mix_norm=dgain,
                 rg_small=rg_small, rg_dwa=dwa, rg_dwx=dwx, fox_bf=dbf, merge_b=dmb)
    return dx, (dshift, dscale, dgate), grads


def _silu(name, c):
    def body(c_ref, o_ref):
        v = c_ref[...]
        o_ref[...] = v * _sigmoid(v)

    return pl.pallas_call(body, name=name, out_shape=jax.ShapeDtypeStruct(c.shape, F32),
                          compiler_params=_params())(c)


def _blockdiag(wb):
    nb, bd, _ = wb.shape
    per = LANE // bd
    t = wb.reshape(nb // per, per, bd, 1, bd)
    eye = jnp.eye(per, dtype=wb.dtype).reshape(1, per, 1, per, 1)
    return (t * eye).reshape(nb // per, LANE, LANE).astype(BF16)


def _unblockdiag(t, bd):
    n = t.shape[0]
    per = LANE // bd
    t5 = t.reshape(n, per, bd, per, bd)
    return jnp.stack([t5[:, p, :, p, :] for p in range(per)], axis=1).reshape(n * per, bd, bd)


def _prepare(gw, a, d, chans, nh):
    depth = a["ada_b"].shape[0]
    wq = 3 * nh * HEAD_DIM
    o_m = 2 * chans + 2 * wq
    w = {"layout": dict(chans=chans, heads=nh)}
    for n in ("ffn1_w1", "ffn1_w3", "ffn1_w2", "ffn2_w1", "ffn2_w3", "ffn2_w2", "w_rg", "w_sb", "w_fox", "w_o"):
        w[n] = gw[n]
    for n in ("ffn1_norm", "ffn2_norm", "mix_norm", "fox_bf", "merge_b", "final_norm"):
        w[n] = a[n]
    w_a, w_b, w_f, rg = [], [], [], []
    for l in range(depth):
        full = gw["w_in"][:, l].transpose(1, 0, 2).reshape(d, -1)
        w_a.append(jnp.concatenate([full[:, :2 * chans], full[:, o_m + nh:]], axis=1))
        w_b.append(full[:, 2 * chans:o_m])
        w_f.append(jnp.pad(full[:, o_m:o_m + nh], ((0, 0), (0, LANE - nh))))
        conv_w = gw["conv_w"][:, l].transpose(1, 0, 2).reshape(-1, chans)
        rg.append(dict(conv_w=conv_w, conv_b=a["conv_b"][l][None], ba=a["rg_ba"][l][None], bx=a["rg_bx"][l][None],
                       lam=a["rg_lam"][l][None], wa=_blockdiag(a["rg_wa"][l]), wx=_blockdiag(a["rg_wx"][l])))
    w["w_a"], w["w_b"], w["w_f"], w["rg"] = w_a, w_b, w_f, rg
    return w


def _local_step(w, x, tgt, mods, fm):
    bl, seq, d = x.shape
    t_rows = bl * seq
    depth = len(mods)
    mod3 = []
    for l in range(depth):
        m4 = mods[l].reshape(bl, 9, 1, d)
        mod3.append([(m4[:, 3 * k], m4[:, 3 * k + 1], m4[:, 3 * k + 2]) for k in range(3)])
    fm4 = fm.reshape(bl, 2, 1, d)
    saved = []
    xc = x.reshape(t_rows, d)
    for l in range(depth):
        xc, s1 = _ffn_fwd(f"l{l}_ffn1", w, l, "ffn1_", xc, mod3[l][0], seq)
        xc, s2 = _mix_fwd(f"l{l}_mix", w, l, xc, mod3[l][1], seq)
        xc, s3 = _ffn_fwd(f"l{l}_ffn2", w, l, "ffn2_", xc, mod3[l][2], seq)
        saved.append((s1, s2, s3))
    lpart, dx, dfs, dfc, dfg = _final_loss("final", xc, tgt.reshape(t_rows, d), w["final_norm"][None],
                                           fm4[:, 0], fm4[:, 1], seq)
    loss = jnp.sum(lpart[:, 0, 0])
    grads = {"final_norm": dfg, "layers": [None] * depth}
    dmods = [None] * depth
    for l in reversed(range(depth)):
        s1, s2, s3 = saved[l]
        dx, dm3, g3 = _ffn_bwd(f"l{l}_ffn2", w, l, "ffn2_", s3, mod3[l][2], dx, seq)
        dx, dm2, g2 = _mix_bwd(f"l{l}_mix", w, l, s2, mod3[l][1], dx, seq)
        dx, dm1, g1 = _ffn_bwd(f"l{l}_ffn1", w, l, "ffn1_", s1, mod3[l][0], dx, seq)
        dmods[l] = jnp.concatenate([*dm1, *dm2, *dm3], axis=1).reshape(bl, 9 * d)
        grads["layers"][l] = {**g1, **g2, **g3}
    dfm = jnp.concatenate([dfs, dfc], axis=1).reshape(bl, 2 * d)
    return loss, dx.reshape(bl, seq, d), grads, dmods, dfm


def _mesh_pos():
    return lax.axis_index("x"), lax.axis_index("y"), lax.axis_index("c")


def _other_chips(x, y):
    return ((1 - x, y), (x, 1 - y), (1 - x, 1 - y))


def _gather_two_level(name, arrs):
    n = len(arrs)

    def body(*refs):
        ins, outs = refs[:n], refs[n:2 * n]
        send, recv, send2, recv2, send3, recv3 = refs[2 * n:]
        x, y, c = _mesh_pos()
        me = 2 * x + y
        chips = _other_chips(x, y)
        sib = (x, y, 1 - c)
        own = [pltpu.make_async_remote_copy(
            src_ref=ins[i], dst_ref=outs[i].at[me], send_sem=send3.at[i], recv_sem=recv3.at[i],
            device_id=sib, device_id_type=MESH) for i in range(n)]
        first = []
        for j, (px, py) in enumerate(chips):
            for i in range(n):
                first.append(pltpu.make_async_remote_copy(
                    src_ref=ins[i].at[c], dst_ref=outs[i].at[me, c], send_sem=send.at[j * n + i],
                    recv_sem=recv.at[j * n + i], device_id=(px, py, c), device_id_type=MESH))
        for cp in first + own:
            cp.start()
        passed = []
        for j, (px, py) in enumerate(chips):
            for i in range(n):
                landed = outs[i].at[2 * px + py, c]
                pltpu.make_async_remote_copy(
                    src_ref=ins[i].at[c], dst_ref=landed, send_sem=send.at[j * n + i],
                    recv_sem=recv.at[j * n + i], device_id=(px, py, c), device_id_type=MESH).wait_recv()
                fwd = pltpu.make_async_remote_copy(
                    src_ref=landed, dst_ref=landed, send_sem=send2.at[j * n + i],
                    recv_sem=recv2.at[j * n + i], device_id=sib, device_id_type=MESH)
                fwd.start()
                passed.append(fwd)
        for j, (px, py) in enumerate(chips):
            for i in range(n):
                theirs = outs[i].at[2 * px + py, 1 - c]
                pltpu.make_async_remote_copy(
                    src_ref=theirs, dst_ref=theirs, send_sem=send2.at[j * n + i],
                    recv_sem=recv2.at[j * n + i], device_id=sib, device_id_type=MESH).wait_recv()
        for cp in first + passed:
            cp.wait_send()
        for cp in own:
            cp.wait()

    return pl.pallas_call(
        body, name=name, in_specs=[ANY] * n, out_specs=[ANY] * n,
        out_shape=[jax.ShapeDtypeStruct((NUM_CHIPS,) + a.shape, a.dtype) for a in arrs],
        scratch_shapes=[pltpu.SemaphoreType.DMA((3 * n,)), pltpu.SemaphoreType.DMA((3 * n,)),
                        pltpu.SemaphoreType.DMA((3 * n,)), pltpu.SemaphoreType.DMA((3 * n,)),
                        pltpu.SemaphoreType.DMA((n,)), pltpu.SemaphoreType.DMA((n,))],
    )(*arrs)


def _split_to_sibling(name, arrs):
    n = len(arrs)
    slabs = arrs[0].shape[0]

    def body(*refs):
        ins, theirs = refs[:n], refs[n:2 * n]
        send, recv = refs[2 * n:]
        x, y, c = _mesh_pos()
        sib = (x, y, 1 - c)
        for i in range(n):
            for s in range(slabs):
                pltpu.make_async_remote_copy(
                    src_ref=ins[i].at[s, 1 - c], dst_ref=theirs[i].at[s], send_sem=send.at[i],
                    recv_sem=recv.at[i], device_id=sib, device_id_type=MESH).start()
        for i in range(n):
            pltpu.make_async_remote_copy(
                src_ref=ins[i].at[:, 0], dst_ref=theirs[i], send_sem=send.at[i], recv_sem=recv.at[i],
                device_id=sib, device_id_type=MESH).wait()

    return pl.pallas_call(
        body, name=name, in_specs=[ANY] * n, out_specs=[ANY] * n,
        out_shape=[jax.ShapeDtypeStruct((a.shape[0],) + a.shape[2:], a.dtype) for a in arrs],
        scratch_shapes=[pltpu.SemaphoreType.DMA((n,)), pltpu.SemaphoreType.DMA((n,))],
    )(*arrs)


def _scatter_chips(name, arrs):
    n = len(arrs)

    def body(*refs):
        ins, outs = refs[:n], refs[n:2 * n]
        send, recv = refs[2 * n:]
        x, y, c = _mesh_pos()
        chips = _other_chips(x, y)
        sends = []
        for j, (px, py) in enumerate(chips):
            for i in range(n):
                sends.append(pltpu.make_async_remote_copy(
                    src_ref=ins[i].at[2 * px + py], dst_ref=outs[i].at[j], send_sem=send.at[j * n + i],
                    recv_sem=recv.at[j * n + i], device_id=(px, py, c), device_id_type=MESH))
        for s in sends:
            s.start()
        for s in sends:
            s.wait()

    return pl.pallas_call(
        body, name=name, in_specs=[ANY] * n, out_specs=[ANY] * n,
        out_shape=[jax.ShapeDtypeStruct((NUM_CHIPS - 1,) + a.shape[1:], a.dtype) for a in arrs],
        scratch_shapes=[pltpu.SemaphoreType.DMA((3 * n,)), pltpu.SemaphoreType.DMA((3 * n,))],
    )(*arrs)


def _join_halves(name, arrs):
    n = len(arrs)

    def body(*refs):
        ins, outs = refs[:n], refs[n:2 * n]
        send, recv = refs[2 * n:]
        x, y, c = _mesh_pos()
        copies = [pltpu.make_async_remote_copy(
            src_ref=ins[i], dst_ref=outs[i], send_sem=send.at[i], recv_sem=recv.at[i],
            device_id=(x, y, 1 - c), device_id_type=MESH) for i in range(n)]
        for cp in copies:
            cp.start()
        for cp in copies:
            cp.wait()

    return pl.pallas_call(
        body, name=name, in_specs=[ANY] * n, out_specs=[ANY] * n,
        out_shape=[jax.ShapeDtypeStruct(a.shape, a.dtype) for a in arrs],
        scratch_shapes=[pltpu.SemaphoreType.DMA((n,)), pltpu.SemaphoreType.DMA((n,))],
    )(*arrs)


def _gather_all(name, pack):
    def body(in_ref, out_ref, send, recv, loc):
        x, y, c = _mesh_pos()
        me = 4 * x + 2 * y + c
        mine = pltpu.make_async_copy(in_ref, out_ref.at[me], loc)
        mine.start()
        peers = []
        for mask in range(1, NUM_DEVICES):
            px = 1 - x if mask & 4 else x
            py = 1 - y if mask & 2 else y
            pc = 1 - c if mask & 1 else c
            peers.append((px, py, pc))
        sends = [pltpu.make_async_remote_copy(
            src_ref=in_ref, dst_ref=out_ref.at[me], send_sem=send.at[k], recv_sem=recv.at[k],
            device_id=p, device_id_type=MESH) for k, p in enumerate(peers)]
        for s in sends:
            s.start()
        for k, (px, py, pc) in enumerate(peers):
            pltpu.make_async_remote_copy(
                src_ref=in_ref, dst_ref=out_ref.at[4 * px + 2 * py + pc], send_sem=send.at[k], recv_sem=recv.at[k],
                device_id=(px, py, pc), device_id_type=MESH).wait_recv()
        for s in sends:
            s.wait_send()
        mine.wait()

    return pl.pallas_call(
        body, name=name, in_specs=[ANY], out_specs=ANY,
        out_shape=jax.ShapeDtypeStruct((NUM_DEVICES,) + pack.shape, pack.dtype),
        scratch_shapes=[pltpu.SemaphoreType.DMA((NUM_DEVICES - 1,)), pltpu.SemaphoreType.DMA((NUM_DEVICES - 1,)),
                        pltpu.SemaphoreType.DMA],
    )(pack)


def _sum_slots(name, slots, out_dtype):
    g, rows, cols = slots.shape
    tr = _rtile(rows, 256)

    def body(s_ref, o_ref):
        acc = s_ref[0].astype(F32)
        for k in range(1, g):
            acc = acc + s_ref[k].astype(F32)
        o_ref[...] = acc.astype(o_ref.dtype)

    return pl.pallas_call(
        body, name=name, grid=(rows // tr,),
        in_specs=[pl.BlockSpec((g, tr, cols), lambda i: (0, i, 0))],
        out_specs=pl.BlockSpec((tr, cols), lambda i: (i, 0)),
        out_shape=jax.ShapeDtypeStruct((rows, cols), out_dtype), compiler_params=_params(),
    )(slots)


def _add_pair(name, p, q, core):
    g, _, rows, cols = p.shape
    tr = _rtile(rows, 128)

    def body(c_ref, p_ref, q_ref, o_ref):
        mine = jnp.where(c_ref[0] == 0, p_ref[:, 0].astype(F32), p_ref[:, 1].astype(F32))
        o_ref[...] = (mine + q_ref[...].astype(F32)).astype(o_ref.dtype)

    spec = pl.BlockSpec((g, tr, cols), lambda i: (0, i, 0))
    return pl.pallas_call(
        body, name=name, grid=(rows // tr,),
        in_specs=[SCALAR, pl.BlockSpec((g, 2, tr, cols), lambda i: (0, 0, i, 0)), spec],
        out_specs=spec, out_shape=jax.ShapeDtypeStruct(q.shape, BF16), compiler_params=_params(),
    )(core, p, q)


def _sum_chips(name, slots, part, chip):
    g, rows, cols = part.shape
    tr = _rtile(rows, 128)

    def body(c_ref, s_ref, p_ref, o_ref):
        acc = p_ref[c_ref[0]].astype(F32)
        for k in range(slots.shape[0]):
            acc = acc + s_ref[k].astype(F32)
        o_ref[...] = acc

    return pl.pallas_call(
        body, name=name, grid=(rows // tr,),
        in_specs=[SCALAR,
                  pl.BlockSpec((slots.shape[0], tr, cols), lambda i: (0, i, 0)),
                  pl.BlockSpec((g, tr, cols), lambda i: (0, i, 0))],
        out_specs=pl.BlockSpec((tr, cols), lambda i: (i, 0)),
        out_shape=jax.ShapeDtypeStruct((rows, cols), F32), compiler_params=_params(),
    )(chip, slots, part)


def _adamw(name, g, w, m, v, l=None):
    rows, cols = g.shape
    tr = _rtile(rows, 128)

    def body(g_ref, w_ref, m_ref, v_ref, d_o, m_o, v_o):
        gv = g_ref[...]
        mn = ADAM_B1 * m_ref[...] + (1.0 - ADAM_B1) * gv
        vn = ADAM_B2 * v_ref[...] + (1.0 - ADAM_B2) * (gv * gv)
        m_hat = mn / (1.0 - ADAM_B1 ** ADAM_STEP)
        v_hat = vn / (1.0 - ADAM_B2 ** ADAM_STEP)
        d_o[...] = -ADAM_LR * (m_hat / (jnp.sqrt(v_hat) + ADAM_EPS) + ADAM_WD * w_ref[...])
        m_o[...] = mn
        v_o[...] = vn

    gspec = pl.BlockSpec((tr, cols), lambda i: (i, 0))
    wspec = gspec if l is None else pl.BlockSpec((None, tr, cols), lambda i: (l, i, 0))
    f = jax.ShapeDtypeStruct((rows, cols), F32)
    return pl.pallas_call(
        body, name=name, grid=(rows // tr,), in_specs=[gspec] + [wspec] * 3, out_specs=[gspec] * 3,
        out_shape=[f] * 3, compiler_params=_params(),
    )(g, w, m, v)


def _adamw_layers(name, g0, g1, w, m, v):
    rows, cols = g0.shape
    tr = _rtile(rows, 128)
    nt = rows // tr

    def body(g0_ref, g1_ref, w_ref, m_ref, v_ref, g_o, d_o, m_o, v_o):
        gv = jnp.where(pl.program_id(0) == 0, g0_ref[...], g1_ref[...])
        _adamw_math(gv, w_ref, m_ref, v_ref, g_o, d_o, m_o, v_o)

    g0spec = pl.BlockSpec((tr, cols), lambda l, i: (i * (1 - l) + (nt - 1) * l, 0))
    g1spec = pl.BlockSpec((tr, cols), lambda l, i: (i * l, 0))
    wspec = pl.BlockSpec((None, tr, cols), lambda l, i: (l, i, 0))
    f = jax.ShapeDtypeStruct((2, rows, cols), F32)
    return pl.pallas_call(
        body, name=name, grid=(2, nt), in_specs=[g0spec, g1spec, wspec, wspec, wspec], out_specs=[wspec] * 4,
        out_shape=[f] * 4, compiler_params=_params(),
    )(g0, g1, w, m, v)


def _adamw_math(gv, w_ref, m_ref, v_ref, g_o, d_o, m_o, v_o):
    mn = ADAM_B1 * m_ref[...] + (1.0 - ADAM_B1) * gv
    vn = ADAM_B2 * v_ref[...] + (1.0 - ADAM_B2) * (gv * gv)
    m_hat = mn / (1.0 - ADAM_B1 ** ADAM_STEP)
    v_hat = vn / (1.0 - ADAM_B2 ** ADAM_STEP)
    g_o[...] = gv
    d_o[...] = -ADAM_LR * (m_hat / (jnp.sqrt(v_hat) + ADAM_EPS) + ADAM_WD * w_ref[...])
    m_o[...] = mn
    v_o[...] = vn


def _adamw_halves(name, mine, theirs, core, w, m, v):
    half, cols = mine[0].shape
    tr = _rtile(half, 128)
    nt = half // tr

    def body(c_ref, a0, b0, a1, b1, w_ref, m_ref, v_ref, g_o, d_o, m_o, v_o):
        first = pl.program_id(0) == 0
        own = pl.program_id(1) == c_ref[0]
        gv = jnp.where(first, jnp.where(own, a0[...], b0[...]), jnp.where(own, a1[...], b1[...]))
        _adamw_math(gv, w_ref, m_ref, v_ref, g_o, d_o, m_o, v_o)

    lay0 = pl.BlockSpec((tr, cols), lambda l, h, i: (i * (1 - l) + (nt - 1) * l, 0))
    lay1 = pl.BlockSpec((tr, cols), lambda l, h, i: (i * l, 0))
    wspec = pl.BlockSpec((None, tr, cols), lambda l, h, i: (l, h * nt + i, 0))
    f = jax.ShapeDtypeStruct((2, 2 * half, cols), F32)
    return pl.pallas_call(
        body, name=name, grid=(2, 2, nt), in_specs=[SCALAR, lay0, lay0, lay1, lay1, wspec, wspec, wspec],
        out_specs=[wspec] * 4, out_shape=[f] * 4, compiler_params=_params(),
    )(core, mine[0], theirs[0], mine[1], theirs[1], w, m, v)


def _colsum(name, a):
    def body(a_ref, o_ref):
        o_ref[...] = jnp.sum(a_ref[...], axis=0, keepdims=True)

    return pl.pallas_call(body, name=name, out_shape=jax.ShapeDtypeStruct((1, a.shape[1]), F32),
                          compiler_params=_params())(a)


PACK_UNIT = SUBLANE * LANE


def _pack(items):
    flat, layout, o = [], [], 0
    for it in items:
        n = it.size
        pad = -n % PACK_UNIT
        flat.append(jnp.pad(it.reshape(-1).astype(F32), (0, pad)))
        layout.append((o, n, it.shape))
        o += n + pad
    return jnp.concatenate(flat).reshape(-1, LANE), layout


def _unpack(pack, layout):
    flat = pack.reshape(-1)
    return [flat[o:o + n].reshape(shape) for o, n, shape in layout]


WEIGHTS = ("ffn1_norm", "ffn1_w1", "ffn1_w3", "ffn1_w2", "mix_norm", "w_in", "conv_w", "conv_b", "rg_wa", "rg_ba",
           "rg_wx", "rg_bx", "rg_lam", "fox_bf", "merge_b", "w_rg", "w_sb", "w_fox", "w_o", "ffn2_norm", "ffn2_w1",
           "ffn2_w3", "ffn2_w2", "ada_w", "ada_b", "final_norm", "final_ada_w", "final_ada_b")
DENSE = ("ffn1_w1", "ffn1_w3", "ffn1_w2", "w_in", "w_rg", "w_sb", "w_fox", "w_o", "ffn2_w1", "ffn2_w3", "ffn2_w2")
SMALL = ("ffn1_norm", "mix_norm", "ffn2_norm", "rg_small", "rg_wa", "rg_wx", "fox_bf", "merge_b")


def _step(a):
    x, c, tgt = a["x"], a["c"], a["loss_target"]
    bl, seq, d = x.shape
    depth, nh = a["fox_bf"].shape
    chans = a["rg_lam"].shape[1]
    bd = a["rg_wa"].shape[2]
    wq = 3 * nh * HEAD_DIM
    o_m = 2 * chans + 2 * wq
    batch = NUM_DEVICES * bl
    mx, my, mc = _mesh_pos()
    me = 2 * mx + my
    dev = 4 * mx + 2 * my + mc

    c_rows = -(-bl * d // LANE // SUBLANE) * SUBLANE
    c_pack = jnp.pad(c.reshape(-1, LANE), ((0, c_rows - bl * d // LANE), (0, 0)))
    c_all = _gather_all("gather_c", c_pack)[:, :bl * d // LANE].reshape(batch, d)
    c_act = _silu("c_act", c_all)
    c_b = c_act.astype(BF16)
    ncol, fcol = a["ada_w"].shape[2], a["final_ada_w"].shape[1]
    cols = []
    for l in range(depth):
        bias = jnp.broadcast_to(lax.dynamic_slice_in_dim(a["ada_b"][l], me * ncol, ncol)[None], (batch, ncol))
        cols.append(_mm(f"ada{l}", c_b, a["ada_w"][l].astype(BF16), NN, F32, acc=bias))
    bias = jnp.broadcast_to(lax.dynamic_slice_in_dim(a["final_ada_b"], me * fcol, fcol)[None], (batch, fcol))
    cols.append(_mm("ada_final", c_b, a["final_ada_w"].astype(BF16), NN, F32, acc=bias))
    mod_cols = jnp.concatenate(cols, axis=1).reshape(2, batch // 2, depth * ncol + fcol)

    names = DENSE + ("conv_w", "mod_cols")
    got = _gather_two_level("gather_weights", [a[n].astype(BF16) for n in DENSE] + [a["conv_w"], mod_cols])
    gw = dict(zip(names, got))
    w = _prepare(gw, a, d, chans, nh)
    mod_all = gw["mod_cols"].reshape(NUM_CHIPS, batch, -1)
    mine = lambda full: lax.dynamic_slice_in_dim(full, dev * bl, bl, axis=0)
    mods = [mine(mod_all[:, :, l * ncol:(l + 1) * ncol].transpose(1, 0, 2).reshape(batch, NUM_CHIPS * ncol))
            for l in range(depth)]
    fm = mine(mod_all[:, :, depth * ncol:].transpose(1, 0, 2).reshape(batch, NUM_CHIPS * fcol))

    loss, grad_x, grads, dmods, dfm = _local_step(w, x, tgt, mods, fm)
    loss = lax.psum(loss, ("x", "y", "c"))

    rs_in = []
    for l in range(depth):
        for n in DENSE:
            if n == "w_in":
                ga, gb, gf = grads["layers"][l]["w_in"]
                orig = jnp.concatenate([ga[:, :2 * chans], gb, gf[:, :nh], ga[:, 2 * chans:]], axis=1)
                rs_in.append(orig.reshape(d, NUM_CHIPS, -1).transpose(1, 0, 2))
            else:
                rs_in.append(grads["layers"][l][n])
    rs_in = [g.reshape(g.shape[0], 2, g.shape[1] // 2, g.shape[2]) for g in rs_in]
    core = jnp.reshape(mc, (1,)).astype(jnp.int32)
    chip = jnp.reshape(me, (1,)).astype(jnp.int32)
    theirs = _split_to_sibling("split_grads", rs_in)
    chip_part = [_add_pair(f"add_cores_{k}", g, t, core) for k, (g, t) in enumerate(zip(rs_in, theirs))]
    slots = _scatter_chips("scatter_grads", chip_part)
    reduced = [_sum_chips(f"sum_chips_{k}", s, p, chip) for k, (s, p) in enumerate(zip(slots, chip_part))]
    other = _join_halves("join_grads", reduced)

    out = {}

    def put(n, res, per_layer):
        for kind, val in zip(("grad_", "delta_", "new_m_", "new_v_"), res):
            out[kind + n] = jnp.stack(val).reshape(a[n].shape) if per_layer else val.reshape(a[n].shape)

    def flat3(v):
        return v.reshape(depth, -1, v.shape[-1])

    assert depth == 2
    nd = len(DENSE)
    for k, n in enumerate(DENSE):
        put(n, _adamw_halves(f"adamw_{n}", (reduced[k], reduced[nd + k]), (other[k], other[nd + k]), core,
                             flat3(a[n]), flat3(a["m_" + n]), flat3(a["v_" + n])), False)

    items = []
    for l in range(depth):
        g = grads["layers"][l]
        items += [g["ffn1_norm"], g["mix_norm"], g["ffn2_norm"], g["rg_small"], _unblockdiag(g["rg_dwa"], bd),
                  _unblockdiag(g["rg_dwx"], bd), g["fox_bf"], g["merge_b"], dmods[l]]
    items += [grads["final_norm"], dfm]
    pack, layout = _pack(items)
    gath = _gather_all("gather_small", pack)
    tot = _sum_slots("sum_small", gath, F32)

    def wpack(pre):
        its = []
        for l in range(depth):
            rg_rows = jnp.concatenate([jnp.zeros((4, chans), F32), a[pre + "conv_b"][l][None], a[pre + "rg_ba"][l][None],
                                       a[pre + "rg_bx"][l][None], a[pre + "rg_lam"][l][None]], axis=0)
            its += [a[pre + "ffn1_norm"][l], a[pre + "mix_norm"][l], a[pre + "ffn2_norm"][l], rg_rows,
                    a[pre + "rg_wa"][l], a[pre + "rg_wx"][l], a[pre + "fox_bf"][l], a[pre + "merge_b"][l],
                    jnp.zeros((bl, 9 * d), F32)]
        its += [a[pre + "final_norm"], jnp.zeros((bl, 2 * d), F32)]
        return _pack(its)[0]

    res_small = [_unpack(r, layout) for r in [tot] + list(_adamw("adamw_small", tot, wpack(""), wpack("m_"), wpack("v_")))]
    per = len(SMALL) + 1
    for j, n in enumerate(SMALL):
        if n == "rg_small":
            for row, nm in ((4, "conv_b"), (5, "rg_ba"), (6, "rg_bx"), (7, "rg_lam")):
                put(nm, [[r[l * per + j][row] for l in range(depth)] for r in res_small], True)
        else:
            put(n, [[r[l * per + j] for l in range(depth)] for r in res_small], True)
    put("final_norm", [r[depth * per] for r in res_small], False)

    gflat = gath.reshape(NUM_DEVICES, -1)

    def rows_of(idx):
        o, n, shape = layout[idx]
        return gflat[:, o:o + n].reshape(NUM_DEVICES * shape[0], shape[1])

    late_g, ada = [], []
    for l in range(depth):
        dmod_all = rows_of(l * per + per - 1)
        late_g.append(_colsum(f"ada_b_grad_{l}", dmod_all))
        cut = lax.dynamic_slice_in_dim(dmod_all, me * ncol, ncol, axis=1).astype(BF16)
        ada.append(_mm(f"ada_w_grad_{l}", c_b, cut, TN, F32))
    put("ada_w", _adamw_layers("adamw_ada_w", ada[0], ada[1], a["ada_w"], a["m_ada_w"], a["v_ada_w"]), False)
    dfm_all = rows_of(depth * per + 1)
    late_g.append(_colsum("final_ada_b_grad", dfm_all))
    cut = lax.dynamic_slice_in_dim(dfm_all, me * fcol, fcol, axis=1).astype(BF16)
    gl = _mm("final_ada_w_grad", c_b, cut, TN, F32)
    put("final_ada_w", [gl] + list(_adamw("adamw_final_ada_w", gl, a["final_ada_w"], a["m_final_ada_w"],
                                          a["v_final_ada_w"])), False)
    cshard = a["conv_w"].shape[2]
    for l in range(depth):
        rg_tot = res_small[0][l * per + SMALL.index("rg_small")]
        late_g.append(lax.dynamic_slice_in_dim(rg_tot[:4], me * cshard, cshard, axis=1))
    gp2, layout2 = _pack(late_g)

    def wpack2(pre):
        return _pack([a[pre + "ada_b"][l][None] for l in range(depth)] + [a[pre + "final_ada_b"][None]]
                     + [a[pre + "conv_w"][l] for l in range(depth)])[0]

    res_late = [_unpack(r, layout2) for r in [gp2] + list(_adamw("adamw_late", gp2, wpack2(""), wpack2("m_"), wpack2("v_")))]
    put("ada_b", [[r[l] for l in range(depth)] for r in res_late], True)
    put("final_ada_b", [r[depth] for r in res_late], False)
    put("conv_w", [[r[depth + 1 + l] for l in range(depth)] for r in res_late], True)

    outs = [loss, grad_x]
    for kind in ("grad_", "delta_", "new_m_", "new_v_"):
        outs += [out[kind + n] for n in WEIGHTS]
    return tuple(outs)


def kernel(x, c, ffn1_norm, ffn1_w1, ffn1_w3, ffn1_w2, mix_norm, w_in, conv_w, conv_b, rg_wa, rg_ba, rg_wx, rg_bx, rg_lam, fox_bf, merge_b, w_rg, w_sb, w_fox, w_o, ffn2_norm, ffn2_w1, ffn2_w3, ffn2_w2, ada_w, ada_b, final_norm, final_ada_w, final_ada_b, loss_target, m_ffn1_norm, m_ffn1_w1, m_ffn1_w3, m_ffn1_w2, m_mix_norm, m_w_in, m_conv_w, m_conv_b, m_rg_wa, m_rg_ba, m_rg_wx, m_rg_bx, m_rg_lam, m_fox_bf, m_merge_b, m_w_rg, m_w_sb, m_w_fox, m_w_o, m_ffn2_norm, m_ffn2_w1, m_ffn2_w3, m_ffn2_w2, m_ada_w, m_ada_b, m_final_norm, m_final_ada_w, m_final_ada_b, v_ffn1_norm, v_ffn1_w1, v_ffn1_w3, v_ffn1_w2, v_mix_norm, v_w_in, v_conv_w, v_conv_b, v_rg_wa, v_rg_ba, v_rg_wx, v_rg_bx, v_rg_lam, v_fox_bf, v_merge_b, v_w_rg, v_w_sb, v_w_fox, v_w_o, v_ffn2_norm, v_ffn2_w1, v_ffn2_w3, v_ffn2_w2, v_ada_w, v_ada_b, v_final_norm, v_final_ada_w, v_final_ada_b):
    args = dict(locals())
    return _step(args)
```

```python
import math

import jax
import jax.numpy as jnp
from jax import lax
from jax.experimental import pallas as pl
from jax.experimental.pallas import tpu as pltpu

F32 = jnp.float32
BF16 = jnp.bfloat16

NUM_CHIPS = 4
NUM_DEVICES = 8
HEAD_DIM = 64
LANE = 128
SUBLANE = 8
VMEM_LIMIT = 56 * 1024 * 1024
EPS = 1e-6
RG_C = 8.0
ADAM_LR = 0.001
ADAM_B1 = 0.9
ADAM_B2 = 0.999
ADAM_EPS = 1e-08
ADAM_WD = 0.01
ADAM_STEP = 10
MESH = pl.DeviceIdType.MESH
ANY = pl.BlockSpec(memory_space=pl.ANY)
SCALAR = pl.BlockSpec(memory_space=pltpu.SMEM)


def _params():
    return pltpu.CompilerParams(vmem_limit_bytes=VMEM_LIMIT)


def _tile(dim, pref):
    if dim <= pref:
        return dim
    t = (pref // LANE) * LANE
    while t >= LANE:
        if dim % t == 0:
            return t
        t -= LANE
    return dim


def _rtile(rows, pref, unit=2 * SUBLANE):
    if rows <= pref:
        return rows
    t = (pref // unit) * unit
    while t >= unit:
        if rows % t == 0:
            return t
        t -= unit
    return rows


def _sigmoid(x):
    return 0.5 * jnp.tanh(0.5 * x) + 0.5


def _softplus(x):
    return jnp.maximum(x, 0.0) + jnp.log(1.0 + jnp.exp(-jnp.abs(x)))


def _expm1(x):
    small = x * (1.0 + x * (0.5 + x * (1.0 / 6.0 + x * (1.0 / 24.0))))
    return jnp.where(jnp.abs(x) < 0.01, small, jnp.exp(x) - 1.0)


_GELU_K = math.sqrt(2.0 / math.pi)


def _gelu_and_grad(x):
    inner = _GELU_K * (x + 0.044715 * x * x * x)
    t = jnp.tanh(inner)
    val = 0.5 * x * (1.0 + t)
    dinner = _GELU_K * (1.0 + 3.0 * 0.044715 * x * x)
    grad = 0.5 * (1.0 + t) + 0.5 * x * (1.0 - t * t) * dinner
    return val, grad


NN = ((1,), (0,))
NT = ((1,), (1,))
TN = ((0,), (0,))
ALL = slice(None)


def _mmk(name, ops, specs, terms, out_shape, out_dtype, grid, o_spec, acc=None):
    n_ops = len(ops)

    def body(*refs):
        o_ref = refs[-1]
        p = None
        for ia, xa, ib, xb, dims in terms:
            t = lax.dot_general(refs[ia][xa], refs[ib][xb], (dims, ((), ())), preferred_element_type=F32)
            p = t if p is None else p + t
        if acc is not None:
            p = p + refs[n_ops][...].astype(F32)
        o_ref[...] = p.astype(o_ref.dtype)

    in_specs = list(specs)
    args = list(ops)
    if acc is not None:
        in_specs.append(pl.BlockSpec(o_spec.block_shape, o_spec.index_map))
        args.append(acc)
    return pl.pallas_call(
        body, name=name, grid=grid, in_specs=in_specs, out_specs=o_spec,
        out_shape=jax.ShapeDtypeStruct(out_shape, out_dtype), compiler_params=_params(),
    )(*args)


def _mm(name, a, b, dims, out_dtype, acc=None, tm=512, tn=512):
    if dims == NN:
        (m, kk), n = a.shape, b.shape[1]
    elif dims == NT:
        (m, kk), n = a.shape, b.shape[0]
    else:
        (kk, m), n = a.shape, b.shape[1]
    tm, tn = _tile(m, tm), _tile(n, tn)
    if dims == TN:
        a_spec = pl.BlockSpec((kk, tm), lambda i, j: (0, i))
    else:
        a_spec = pl.BlockSpec((tm, kk), lambda i, j: (i, 0))
    if dims == NT:
        b_spec = pl.BlockSpec((tn, kk), lambda i, j: (j, 0))
    else:
        b_spec = pl.BlockSpec((kk, tn), lambda i, j: (0, j))
    return _mmk(name, [a, b], [a_spec, b_spec], [(0, ALL, 1, ALL, dims)], (m, n), out_dtype,
                (m // tm, n // tn), pl.BlockSpec((tm, tn), lambda i, j: (i, j)), acc)


def _row_grid(t_rows, seq, pref=256):
    tm = _tile(seq, pref)
    return tm, seq // tm


def _normmod(name, x, gain, shift, scale, seq):
    t_rows, d = x.shape
    bl = t_rows // seq
    tm, per = _row_grid(t_rows, seq)

    def body(x_ref, g_ref, sh_ref, sc_ref, o_ref):
        xv = x_ref[...]
        rstd = lax.rsqrt(jnp.mean(xv * xv, axis=-1, keepdims=True) + EPS)
        hn = (xv * rstd) * g_ref[...]
        o_ref[...] = (hn * (1.0 + sc_ref[...]) + sh_ref[...]).astype(o_ref.dtype)

    row = pl.BlockSpec((tm, d), lambda b, i: (b * per + i, 0))
    vec = pl.BlockSpec((None, 1, d), lambda b, i: (b, 0, 0))
    return pl.pallas_call(
        body, name=name, grid=(bl, per),
        in_specs=[row, pl.BlockSpec((1, d), lambda b, i: (0, 0)), vec, vec],
        out_specs=row, out_shape=jax.ShapeDtypeStruct((t_rows, d), BF16),
        compiler_params=_params(),
    )(x, gain, shift, scale)


def _normmod_bwd(name, x, dh, dxo, gain, scale, seq):
    t_rows, d = x.shape
    bl = t_rows // seq
    tm, per = _row_grid(t_rows, seq)

    def body(x_ref, dh_ref, dxo_ref, g_ref, sc_ref, dx_ref, dsh_ref, dsc_ref, dg_ref):
        b, i = pl.program_id(0), pl.program_id(1)
        xv = x_ref[...]
        dhv = dh_ref[...]
        rstd = lax.rsqrt(jnp.mean(xv * xv, axis=-1, keepdims=True) + EPS)
        xhat = xv * rstd
        gain_v = g_ref[...]
        dhn = dhv * (1.0 + sc_ref[...])
        dxhat = dhn * gain_v
        dx = rstd * (dxhat - xhat * jnp.mean(dxhat * xhat, axis=-1, keepdims=True))
        dx_ref[...] = dxo_ref[...] + dx

        @pl.when(i == 0)
        def _():
            dsh_ref[...] = jnp.zeros_like(dsh_ref)
            dsc_ref[...] = jnp.zeros_like(dsc_ref)

        @pl.when((i == 0) & (b == 0))
        def _():
            dg_ref[...] = jnp.zeros_like(dg_ref)

        dsh_ref[...] += jnp.sum(dhv, axis=0, keepdims=True)
        dsc_ref[...] += jnp.sum(dhv * (xhat * gain_v), axis=0, keepdims=True)
        dg_ref[...] += jnp.sum(dhn * xhat, axis=0, keepdims=True)

    row = pl.BlockSpec((tm, d), lambda b, i: (b * per + i, 0))
    vec = pl.BlockSpec((None, 1, d), lambda b, i: (b, 0, 0))
    one = pl.BlockSpec((1, d), lambda b, i: (0, 0))
    return pl.pallas_call(
        body, name=name, grid=(bl, per),
        in_specs=[row, row, row, one, vec],
        out_specs=[row, vec, vec, one],
        out_shape=[jax.ShapeDtypeStruct((t_rows, d), F32), jax.ShapeDtypeStruct((bl, 1, d), F32),
                   jax.ShapeDtypeStruct((bl, 1, d), F32), jax.ShapeDtypeStruct((1, d), F32)],
        compiler_params=_params(),
    )(x, dh, dxo, gain, scale)


def _resid(name, x, y, gate, coef, seq):
    t_rows, d = x.shape
    bl = t_rows // seq
    tm, per = _row_grid(t_rows, seq)

    def body(x_ref, y_ref, g_ref, o_ref):
        o_ref[...] = x_ref[...] + (coef * (1.0 + g_ref[...])) * y_ref[...]

    row = pl.BlockSpec((tm, d), lambda b, i: (b * per + i, 0))
    vec = pl.BlockSpec((None, 1, d), lambda b, i: (b, 0, 0))
    return pl.pallas_call(
        body, name=name, grid=(bl, per), in_specs=[row, row, vec], out_specs=row,
        out_shape=jax.ShapeDtypeStruct((t_rows, d), F32), compiler_params=_params(),
    )(x, y, gate)


def _resid_bwd(name, dxo, y, gate, coef, seq):
    t_rows, d = dxo.shape
    bl = t_rows // seq
    tm, per = _row_grid(t_rows, seq)

    def body(dxo_ref, y_ref, g_ref, dy_ref, dg_ref):
        i = pl.program_id(1)
        dxov = dxo_ref[...]
        dy_ref[...] = ((coef * (1.0 + g_ref[...])) * dxov).astype(dy_ref.dtype)

        @pl.when(i == 0)
        def _():
            dg_ref[...] = jnp.zeros_like(dg_ref)

        dg_ref[...] += jnp.sum((coef * y_ref[...]) * dxov, axis=0, keepdims=True)

    row = pl.BlockSpec((tm, d), lambda b, i: (b * per + i, 0))
    vec = pl.BlockSpec((None, 1, d), lambda b, i: (b, 0, 0))
    return pl.pallas_call(
        body, name=name, grid=(bl, per), in_specs=[row, row, vec], out_specs=[row, vec],
        out_shape=[jax.ShapeDtypeStruct((t_rows, d), BF16), jax.ShapeDtypeStruct((bl, 1, d), F32)],
        compiler_params=_params(),
    )(dxo, y, gate)


def _final_loss(name, x, tgt, gain, shift, scale, seq):
    t_rows, d = x.shape
    bl = t_rows // seq
    tm, per = _row_grid(t_rows, seq)

    def body(x_ref, t_ref, g_ref, sh_ref, sc_ref, l_ref, dx_ref, dsh_ref, dsc_ref, dg_ref):
        b, i = pl.program_id(0), pl.program_id(1)
        xv = x_ref[...]
        rstd = lax.rsqrt(jnp.mean(xv * xv, axis=-1, keepdims=True) + EPS)
        xhat = xv * rstd
        gain_v = g_ref[...]
        hn = xhat * gain_v
        yv = hn * (1.0 + sc_ref[...]) + sh_ref[...]
        err = yv - t_ref[...]
        dyv = err * (1.0 / d)
        dhn = dyv * (1.0 + sc_ref[...])
        dxhat = dhn * gain_v
        dx_ref[...] = rstd * (dxhat - xhat * jnp.mean(dxhat * xhat, axis=-1, keepdims=True))

        @pl.when(i == 0)
        def _():
            l_ref[...] = jnp.zeros_like(l_ref)
            dsh_ref[...] = jnp.zeros_like(dsh_ref)
            dsc_ref[...] = jnp.zeros_like(dsc_ref)

        @pl.when((i == 0) & (b == 0))
        def _():
            dg_ref[...] = jnp.zeros_like(dg_ref)

        part = jnp.sum(jnp.sum(err * err, axis=-1, keepdims=True), axis=0, keepdims=True) * (0.5 / d)
        l_ref[...] += jnp.broadcast_to(part, l_ref.shape)
        dsh_ref[...] += jnp.sum(dyv, axis=0, keepdims=True)
        dsc_ref[...] += jnp.sum(dyv * hn, axis=0, keepdims=True)
        dg_ref[...] += jnp.sum(dhn * xhat, axis=0, keepdims=True)

    row = pl.BlockSpec((tm, d), lambda b, i: (b * per + i, 0))
    vec = pl.BlockSpec((None, 1, d), lambda b, i: (b, 0, 0))
    one = pl.BlockSpec((1, d), lambda b, i: (0, 0))
    lvec = pl.BlockSpec((None, 1, LANE), lambda b, i: (b, 0, 0))
    return pl.pallas_call(
        body, name=name, grid=(bl, per),
        in_specs=[row, row, one, vec, vec],
        out_specs=[lvec, row, vec, vec, one],
        out_shape=[jax.ShapeDtypeStruct((bl, 1, LANE), F32), jax.ShapeDtypeStruct((t_rows, d), F32),
                   jax.ShapeDtypeStruct((bl, 1, d), F32), jax.ShapeDtypeStruct((bl, 1, d), F32),
                   jax.ShapeDtypeStruct((1, d), F32)],
        compiler_params=_params(),
    )(x, tgt, gain, shift, scale)


def _ffn_up(name, h, w1, w3, l):
    t_rows, d = h.shape
    ng, fs = w1.shape[0], w1.shape[3]
    tm = _tile(t_rows, 512)

    def body(h_ref, w1_ref, w3_ref, a_ref, b_ref, g_ref):
        hv = h_ref[...]
        av = jnp.dot(hv, w1_ref[...], preferred_element_type=F32)
        bv = jnp.dot(hv, w3_ref[...], preferred_element_type=F32)
        a_ref[...] = av.astype(a_ref.dtype)
        b_ref[...] = bv.astype(b_ref.dtype)
        g_ref[...] = (av * _sigmoid(av) * bv).astype(g_ref.dtype)

    wspec = pl.BlockSpec((None, None, d, fs), lambda g, i: (g, l, 0, 0))
    out = pl.BlockSpec((None, tm, fs), lambda g, i: (g, i, 0))
    f = jax.ShapeDtypeStruct((ng, t_rows, fs), BF16)
    return pl.pallas_call(
        body, name=name, grid=(ng, t_rows // tm),
        in_specs=[pl.BlockSpec((tm, d), lambda g, i: (i, 0)), wspec, wspec], out_specs=[out, out, out],
        out_shape=[f, f, f], compiler_params=_params(),
    )(h, w1, w3)


def _ffn_down_dx(name, dy, w2, a, b, l):
    t_rows, d = dy.shape
    ng, fs = w2.shape[0], w2.shape[2]
    tm = _tile(t_rows, 512)

    def body(dy_ref, w2_ref, a_ref, b_ref, da_ref, db_ref):
        dgv = lax.dot_general(dy_ref[...], w2_ref[...], (NT, ((), ())), preferred_element_type=F32)
        av = a_ref[...].astype(F32)
        sig = _sigmoid(av)
        da_ref[...] = (dgv * b_ref[...].astype(F32) * (sig * (1.0 + av * (1.0 - sig)))).astype(da_ref.dtype)
        db_ref[...] = (dgv * (av * sig)).astype(db_ref.dtype)

    blk = pl.BlockSpec((None, tm, fs), lambda g, i: (g, i, 0))
    o = jax.ShapeDtypeStruct((ng, t_rows, fs), BF16)
    return pl.pallas_call(
        body, name=name, grid=(ng, t_rows // tm),
        in_specs=[pl.BlockSpec((tm, d), lambda g, i: (i, 0)),
                  pl.BlockSpec((None, None, fs, d), lambda g, i: (g, l, 0, 0)), blk, blk],
        out_specs=[blk, blk], out_shape=[o, o], compiler_params=_params(),
    )(dy, w2, a, b)


def _ffn_up_dw(name, h, da, db):
    t_rows, d = h.shape
    ng, fs = da.shape[0], da.shape[2]
    tn = _tile(d, 512)

    def body(h_ref, da_ref, db_ref, o1_ref, o3_ref):
        hv = h_ref[...]
        o1_ref[...] = lax.dot_general(hv, da_ref[...], (TN, ((), ())), preferred_element_type=F32).astype(o1_ref.dtype)
        o3_ref[...] = lax.dot_general(hv, db_ref[...], (TN, ((), ())), preferred_element_type=F32).astype(o3_ref.dtype)

    dspec = pl.BlockSpec((None, t_rows, fs), lambda g, i: (g, 0, 0))
    out = pl.BlockSpec((None, tn, fs), lambda g, i: (g, i, 0))
    o = jax.ShapeDtypeStruct((ng, d, fs), BF16)
    return pl.pallas_call(
        body, name=name, grid=(ng, d // tn),
        in_specs=[pl.BlockSpec((t_rows, tn), lambda g, i: (0, i)), dspec, dspec],
        out_specs=[out, out], out_shape=[o, o], compiler_params=_params(),
    )(h, da, db)


def _ffn_fwd(tag, w, l, pre, x, mod, seq):
    t_rows, d = x.shape
    w1, w3, w2 = w[pre + "w1"], w[pre + "w3"], w[pre + "w2"]
    ng, fs = w1.shape[0], w1.shape[3]
    shift, scale, gate = mod
    h = _normmod(tag + "_norm", x, w[pre + "norm"][l][None], shift, scale, seq)
    a, b, gact = _ffn_up(tag + "_up", h, w1, w3, l)
    tm, tn = _tile(t_rows, 512), _tile(d, 512)
    y = _mmk(tag + "_down", [gact, w2],
             [pl.BlockSpec((ng, tm, fs), lambda i, j: (0, i, 0)),
              pl.BlockSpec((ng, None, fs, tn), lambda i, j: (0, l, 0, j))],
             [(0, g, 1, g, NN) for g in range(ng)], (t_rows, d), F32, (t_rows // tm, d // tn),
             pl.BlockSpec((tm, tn), lambda i, j: (i, j)))
    xn = _resid(tag + "_res", x, y, gate, 0.5, seq)
    return xn, (x, h, a, b, gact, y)


def _ffn_bwd(tag, w, l, pre, saved, mod, dxo, seq):
    x, h, a, b, gact, y = saved
    t_rows, d = x.shape
    w1, w3, w2 = w[pre + "w1"], w[pre + "w3"], w[pre + "w2"]
    ng, fs = w1.shape[0], w1.shape[3]
    shift, scale, gate = mod
    tm, tn = _tile(t_rows, 512), _tile(d, 512)
    dy, dgate = _resid_bwd(tag + "_res_bwd", dxo, y, gate, 0.5, seq)
    da, db = _ffn_down_dx(tag + "_down_dx", dy, w2, a, b, l)
    dw2 = _mmk(tag + "_down_dw", [gact, dy],
               [pl.BlockSpec((None, t_rows, fs), lambda g, j: (g, 0, 0)),
                pl.BlockSpec((t_rows, tn), lambda g, j: (0, j))],
               [(0, ALL, 1, ALL, TN)], (ng, fs, d), BF16, (ng, d // tn),
               pl.BlockSpec((None, fs, tn), lambda g, j: (g, 0, j)))
    dw1, dw3 = _ffn_up_dw(tag + "_up_dw", h, da, db)
    dspec = pl.BlockSpec((ng, tm, fs), lambda i, j: (0, i, 0))
    wspec = pl.BlockSpec((ng, None, tn, fs), lambda i, j: (0, l, j, 0))
    dh = _mmk(tag + "_up_dx", [da, db, w1, w3], [dspec, dspec, wspec, wspec],
              [(0, g, 2, g, NT) for g in range(ng)] + [(1, g, 3, g, NT) for g in range(ng)],
              (t_rows, d), F32, (t_rows // tm, d // tn), pl.BlockSpec((tm, tn), lambda i, j: (i, j)))
    dx, dshift, dscale, dgain = _normmod_bwd(tag + "_norm_bwd", x, dh, dxo, w[pre + "norm"][l][None], scale, seq)
    grads = {pre + "w1": dw1, pre + "w3": dw3, pre + "w2": dw2, pre + "norm": dgain}
    return dx, (dshift, dscale, dgate), grads


def _shift_down(v, s, row):
    if s == 0:
        return v
    return jnp.where(row >= s, pltpu.roll(v, s, 0), 0.0)


def _shift_up(v, s, row):
    if s == 0:
        return v
    n = v.shape[0]
    return jnp.where(row < n - s, pltpu.roll(v, n - s, 0), 0.0)


def _scan_fwd(a, u, row):
    n = a.shape[0]
    s = 1
    while s < n:
        ok = row >= s
        a_sh = pltpu.roll(a, s, 0)
        u_sh = pltpu.roll(u, s, 0)
        u = jnp.where(ok, a * u_sh + u, u)
        a = jnp.where(ok, a * a_sh, a)
        s *= 2
    return u


def _scan_bwd(a_next, g, row):
    n = g.shape[0]
    a, u = a_next, g
    s = 1
    while s < n:
        ok = row < n - s
        a_sh = pltpu.roll(a, n - s, 0)
        u_sh = pltpu.roll(u, n - s, 0)
        u = jnp.where(ok, a * u_sh + u, u)
        a = jnp.where(ok, a * a_sh, a)
        s *= 2
    return u


def _rg_specs(seq, cw):
    slab = lambda off: pl.BlockSpec((seq, cw), lambda c, b: (b, off + c))
    par = lambda rows: pl.BlockSpec((rows, cw), lambda c, b: (0, c))
    wbd = pl.BlockSpec((None, cw, cw), lambda c, b: (c, 0, 0))
    return slab, par, wbd


def _rg_fwd(name, proj, p, seq, chans):
    t_rows = proj.shape[0]
    bl = t_rows // seq
    cw = LANE
    nc = chans // cw
    slab, par, wbd = _rg_specs(seq, cw)

    def body(x_ref, gt_ref, cw_ref, cb_ref, wa_ref, ba_ref, wx_ref, bx_ref, lam_ref,
             xa_ref, r_ref, i_ref, h_ref, ya_ref):
        row = lax.broadcasted_iota(jnp.int32, (seq, cw), 0)
        xv = x_ref[...]
        xa = jnp.zeros_like(xv) + cb_ref[...]
        for k in range(4):
            xa = xa + cw_ref[k:k + 1, :] * _shift_down(xv, 3 - k, row)
        xab = xa.astype(BF16)
        r = _sigmoid(jnp.dot(xab, wa_ref[...], preferred_element_type=F32) + ba_ref[...])
        ig = _sigmoid(jnp.dot(xab, wx_ref[...], preferred_element_type=F32) + bx_ref[...])
        log_a = (-RG_C) * r * _softplus(-lam_ref[...])
        a = jnp.exp(log_a)
        u = jnp.sqrt(-_expm1(2.0 * log_a)) * (ig * xa)
        h = _scan_fwd(a, u, row)
        gel, _ = _gelu_and_grad(gt_ref[...])
        xa_ref[...] = xa
        r_ref[...] = r
        i_ref[...] = ig
        h_ref[...] = h
        ya_ref[...] = (gel * h).astype(ya_ref.dtype)

    out = pl.BlockSpec((seq, cw), lambda c, b: (b, c))
    f = jax.ShapeDtypeStruct((t_rows, chans), F32)
    return pl.pallas_call(
        body, name=name, grid=(nc, bl),
        in_specs=[slab(0), slab(nc), par(4), par(1), wbd, par(1), wbd, par(1), par(1)],
        out_specs=[out] * 5,
        out_shape=[f, f, f, f, jax.ShapeDtypeStruct((t_rows, chans), BF16)],
        compiler_params=_params(),
    )(proj, proj, p["conv_w"], p["conv_b"], p["wa"], p["ba"], p["wx"], p["bx"], p["lam"])


def _rg_bwd(name, proj, dya, saved, p, seq, chans):
    xa_s, r_s, i_s, h_s = saved
    t_rows = proj.shape[0]
    bl = t_rows // seq
    cw = LANE
    nc = chans // cw
    slab, par, wbd = _rg_specs(seq, cw)

    def body(x_ref, gt_ref, dya_ref, xa_ref, r_ref, i_ref, h_ref, cw_ref, wa_ref, wx_ref, lam_ref,
             dx_ref, dgt_ref, sm_ref, dwa_ref, dwx_ref):
        b = pl.program_id(1)
        row = lax.broadcasted_iota(jnp.int32, (seq, cw), 0)
        xv, xa, r, ig, h = x_ref[...], xa_ref[...], r_ref[...], i_ref[...], h_ref[...]
        dyav = dya_ref[...]
        gel, dgel = _gelu_and_grad(gt_ref[...])
        dgt_ref[...] = (dyav * h * dgel).astype(dgt_ref.dtype)
        dh = dyav * gel
        lam = lam_ref[...]
        sp = _softplus(-lam)
        log_a = (-RG_C) * r * sp
        a = jnp.exp(log_a)
        s = jnp.sqrt(-_expm1(2.0 * log_a))
        lamb = _scan_bwd(_shift_up(a, 1, row), dh, row)
        da = lamb * _shift_down(h, 1, row)
        xi = ig * xa
        ds = lamb * xi
        dxi = lamb * s
        dlog = da * a - ds * (a * a) / s
        dr = dlog * ((-RG_C) * sp)
        dsp = jnp.sum(dlog * ((-RG_C) * r), axis=0, keepdims=True)
        dlam = -dsp * _sigmoid(-lam)
        dzr = dr * r * (1.0 - r)
        dzi = (dxi * xa) * ig * (1.0 - ig)
        dzrb, dzib, xab = dzr.astype(BF16), dzi.astype(BF16), xa.astype(BF16)
        dxa = dxi * ig
        dxa = dxa + lax.dot_general(dzrb, wa_ref[...], (NT, ((), ())), preferred_element_type=F32)
        dxa = dxa + lax.dot_general(dzib, wx_ref[...], (NT, ((), ())), preferred_element_type=F32)
        dwa = lax.dot_general(xab, dzrb, (TN, ((), ())), preferred_element_type=F32)
        dwx = lax.dot_general(xab, dzib, (TN, ((), ())), preferred_element_type=F32)
        dxv = jnp.zeros_like(xv)
        rows = []
        for k in range(4):
            dxv = dxv + cw_ref[k:k + 1, :] * _shift_up(dxa, 3 - k, row)
            rows.append(jnp.sum(dxa * _shift_down(xv, 3 - k, row), axis=0, keepdims=True))
        dx_ref[...] = dxv.astype(dx_ref.dtype)
        rows += [jnp.sum(dxa, axis=0, keepdims=True), jnp.sum(dzr, axis=0, keepdims=True),
                 jnp.sum(dzi, axis=0, keepdims=True), dlam]

        @pl.when(b == 0)
        def _():
            sm_ref[...] = jnp.zeros_like(sm_ref)
            dwa_ref[...] = jnp.zeros_like(dwa_ref)
            dwx_ref[...] = jnp.zeros_like(dwx_ref)

        for k, val in enumerate(rows):
            sm_ref[k:k + 1, :] += val
        dwa_ref[...] += dwa
        dwx_ref[...] += dwx

    plain = pl.BlockSpec((seq, cw), lambda c, b: (b, c))
    return pl.pallas_call(
        body, name=name, grid=(nc, bl),
        in_specs=[slab(0), slab(nc), plain, plain, plain, plain, plain, par(4), wbd, wbd, par(1)],
        out_specs=[plain, plain, par(8), wbd, wbd],
        out_shape=[jax.ShapeDtypeStruct((t_rows, chans), BF16), jax.ShapeDtypeStruct((t_rows, chans), BF16),
                   jax.ShapeDtypeStruct((8, chans), F32),
                   jax.ShapeDtypeStruct((nc, cw, cw), F32), jax.ShapeDtypeStruct((nc, cw, cw), F32)],
        compiler_params=_params(),
    )(proj, proj, dya, xa_s, r_s, i_s, h_s, p["conv_w"], p["wa"], p["wx"], p["lam"])


ATT_Q_BLOCK = 512
SB_K_BLOCK = 256
FOX_K_BLOCK = 512
PAIR = LANE // HEAD_DIM
NEG = -1e30
SCALE = HEAD_DIM ** -0.5
assert math.log2(HEAD_DIM) % 2 == 0


def _att_blocks(seq, k_block):
    return _tile(seq, ATT_Q_BLOCK), _tile(seq, k_block)


def _key_blocks(qi, tq, bk):
    return (qi * tq) // bk, (qi * tq + tq - 1) // bk + 1


def _tri(n, kind):
    r = lax.broadcasted_iota(jnp.int32, (2 * n, n), 0)
    r = jnp.where(r >= n, r - n, r)
    c = lax.broadcasted_iota(jnp.int32, (2 * n, n), 1)
    m = {"gt": r > c, "le": r <= c, "lt": r < c}[kind]
    return m.astype(BF16)


def _cumsum_mm(v, tri):
    hi = v.astype(BF16)
    lo = (v - hi.astype(F32)).astype(BF16)
    return jnp.dot(jnp.concatenate([hi, lo], axis=1), tri, preferred_element_type=F32)


def _head_masks():
    lane = lax.broadcasted_iota(jnp.int32, (1, LANE), 1)
    return [(lane >= h * HEAD_DIM) & (lane < (h + 1) * HEAD_DIM) for h in range(PAIR)]


def _only(mask, v):
    return jnp.where(mask, v, jnp.zeros_like(v))


def _stack_heads(v, masks):
    return jnp.concatenate([_only(m, v) for m in masks], axis=0)


def _unstack_heads(v, masks):
    tq = v.shape[0] // PAIR
    out = _only(masks[0], v[0:tq])
    for h in range(1, PAIR):
        out = out + _only(masks[h], v[h * tq:(h + 1) * tq])
    return out


def _stacked_iotas(tq, bk):
    row = lax.broadcasted_iota(jnp.int32, (PAIR * tq, bk), 0)
    for h in range(1, PAIR):
        row = jnp.where(row >= h * tq, row - tq, row)
    return row, lax.broadcasted_iota(jnp.int32, (PAIR * tq, bk), 1)


def _att_specs(seq, blk, nq, off):
    npair = None
    qs = lambda o: pl.BlockSpec((blk, LANE), lambda b, p, i: (b * nq + i, o + p))
    ks = lambda o: pl.BlockSpec((seq, LANE), lambda b, p, i: (b, o + p))
    col = pl.BlockSpec((None, PAIR, blk, 1), lambda b, p, i: (b, p, i, 0))
    lane = pl.BlockSpec((None, PAIR, 1, seq), lambda b, p, i: (b, p, 0, 0))
    return qs, ks, col, lane


def _sb_fwd(name, qkv, off, width, bl, seq):
    t_rows = qkv.shape[0]
    tq, bk = _att_blocks(seq, SB_K_BLOCK)
    nq = seq // tq
    nb = width // LANE
    qs, ks, col, _ = _att_specs(seq, tq, nq, off)

    def body(q_ref, k_ref, v_ref, o_ref, lt_ref):
        qi = pl.program_id(2)
        masks = _head_masks()
        qs_ = _stack_heads(q_ref[...] * SCALE, masks)
        row, cix = _stacked_iotas(tq, bk)
        tri = _tri(bk, "gt")

        def step(masked, top):
            def go(it, carry):
                acc, cl = carry
                kb = top - it
                ks_ = pl.multiple_of(kb * bk, bk)
                kv = k_ref[pl.ds(ks_, bk), :]
                vv = v_ref[pl.ds(ks_, bk), :]
                strict = (kb * bk + cix) < (qi * tq + row)
                z = lax.dot_general(qs_, kv, (NT, ((), ())), preferred_element_type=F32)
                sp = _softplus(z)
                lk = jnp.where(strict, -sp, 0.0) if masked else -sp
                wgt = jnp.exp(z - sp + (cl + _cumsum_mm(lk, tri)))
                if masked:
                    wgt = jnp.where(strict, wgt, 0.0)
                acc = acc + _unstack_heads(jnp.dot(wgt.astype(BF16), vv, preferred_element_type=F32), masks)
                return acc, cl + jnp.sum(lk, axis=1, keepdims=True)
            return go

        n_full, n_all = _key_blocks(qi, tq, bk)
        carry = (jnp.zeros((tq, LANE), F32), jnp.zeros((PAIR * tq, 1), F32))
        carry = lax.fori_loop(0, n_all - n_full, step(True, n_all - 1), carry)
        acc, cl = lax.fori_loop(0, n_full, step(False, n_full - 1), carry)
        o_ref[...] = acc.astype(o_ref.dtype)
        for h in range(PAIR):
            lt_ref[h] = cl[h * tq:(h + 1) * tq]

    return pl.pallas_call(
        body, name=name, grid=(bl, nb, nq), in_specs=[qs(off), ks(off + nb), ks(off + 2 * nb)],
        out_specs=[qs(0), col],
        out_shape=[jax.ShapeDtypeStruct((t_rows, width), BF16),
                   jax.ShapeDtypeStruct((bl, nb * PAIR, seq, 1), F32)],
        compiler_params=_params(),
    )(qkv, qkv, qkv)


def _sb_bwd(name, qkv, off, width, bl, seq, ltot, do, riders=()):
    t_rows = qkv.shape[0]
    tq, bk = _att_blocks(seq, SB_K_BLOCK)
    nq = seq // tq
    nb = width // LANE
    qs, ks, col, _ = _att_specs(seq, tq, nq, off)
    nr = len(riders)

    def body(*refs):
        q_ref, k_ref, v_ref, lt_ref, do_ref = refs[:5]
        ride_in = refs[5:5 + nr]
        dq_ref, dk_ref, dv_ref = refs[5 + nr:8 + nr]
        ride_out = refs[8 + nr:8 + 2 * nr]
        dk_acc, dv_acc = refs[8 + 2 * nr:10 + 2 * nr]
        qi = pl.program_id(2)
        if nr:
            send, recv = refs[10 + 2 * nr:]
            x, y, c = _mesh_pos()
            copies = []
            for j, (px, py) in enumerate(_other_chips(x, y)):
                for i in range(nr):
                    copies.append(pltpu.make_async_remote_copy(
                        src_ref=ride_in[i].at[2 * px + py], dst_ref=ride_out[i].at[j], send_sem=send.at[j * nr + i],
                        recv_sem=recv.at[j * nr + i], device_id=(px, py, c), device_id_type=MESH))
            first = (pl.program_id(0) == 0) & (pl.program_id(1) == 0) & (qi == 0)
            last = (pl.program_id(0) == bl - 1) & (pl.program_id(1) == nb - 1) & (qi == nq - 1)

            @pl.when(first)
            def _():
                for cp in copies:
                    cp.start()

        @pl.when(qi == 0)
        def _():
            dk_acc[...] = jnp.zeros_like(dk_acc)
            dv_acc[...] = jnp.zeros_like(dv_acc)

        masks = _head_masks()
        qs_ = _stack_heads(q_ref[...] * SCALE, masks)
        dos = _stack_heads(do_ref[...].astype(BF16), masks)
        lts = jnp.concatenate([lt_ref[h] for h in range(PAIR)], axis=0)
        row, cix = _stacked_iotas(tq, bk)
        tri_le = _tri(bk, "le")
        tri_lt = _tri(bk, "lt")

        def step(masked):
            def go(kb, carry):
                dq, cl, ce = carry
                ks_ = pl.multiple_of(kb * bk, bk)
                kv = k_ref[pl.ds(ks_, bk), :]
                vv = v_ref[pl.ds(ks_, bk), :]
                strict = (kb * bk + cix) < (qi * tq + row)
                z = lax.dot_general(qs_, kv, (NT, ((), ())), preferred_element_type=F32)
                sp = _softplus(z)
                lk = jnp.where(strict, -sp, 0.0) if masked else -sp
                sig = jnp.exp(z - sp)
                wgt = sig * jnp.exp(lts - cl - _cumsum_mm(lk, tri_le))
                if masked:
                    wgt = jnp.where(strict, wgt, 0.0)
                dw = lax.dot_general(dos, vv, (NT, ((), ())), preferred_element_type=F32)
                e = dw * wgt
                pre = ce + _cumsum_mm(e, tri_lt)
                dz = e * (1.0 - sig) - pre * sig
                if masked:
                    dz = jnp.where(strict, dz, 0.0)
                dzb = dz.astype(BF16)
                dq = dq + _unstack_heads(jnp.dot(dzb, kv * SCALE, preferred_element_type=F32), masks)
                dk_acc[pl.ds(ks_, bk), :] += lax.dot_general(dzb, qs_, (TN, ((), ())), preferred_element_type=F32)
                dv_acc[pl.ds(ks_, bk), :] += lax.dot_general(wgt.astype(BF16), dos, (TN, ((), ())),
                                                             preferred_element_type=F32)
                return dq, cl + jnp.sum(lk, axis=1, keepdims=True), ce + jnp.sum(e, axis=1, keepdims=True)
            return go

        n_full, n_all = _key_blocks(qi, tq, bk)
        zero = jnp.zeros((PAIR * tq, 1), F32)
        carry = lax.fori_loop(0, n_full, step(False), (jnp.zeros((tq, LANE), F32), zero, zero))
        dq, _, _ = lax.fori_loop(n_full, n_all, step(True), carry)
        dq_ref[...] = dq.astype(dq_ref.dtype)

        @pl.when(qi == nq - 1)
        def _():
            dk_ref[...] = dk_acc[...].astype(dk_ref.dtype)
            dv_ref[...] = dv_acc[...].astype(dv_ref.dtype)

        if nr:
            @pl.when(last)
            def _():
                for cp in copies:
                    cp.wait()

    o = jax.ShapeDtypeStruct((t_rows, width), BF16)
    slots = [jax.ShapeDtypeStruct((NUM_CHIPS - 1,) + r.shape[1:], r.dtype) for r in riders]
    sems = [pltpu.SemaphoreType.DMA((3 * nr,)), pltpu.SemaphoreType.DMA((3 * nr,))] if nr else []
    return pl.pallas_call(
        body, name=name, grid=(bl, nb, nq),
        in_specs=[qs(off), ks(off + nb), ks(off + 2 * nb), col, qs(0)] + [ANY] * nr,
        out_specs=[qs(0), ks(0), ks(0)] + [ANY] * nr, out_shape=[o, o, o] + slots,
        scratch_shapes=[pltpu.VMEM((seq, LANE), F32), pltpu.VMEM((seq, LANE), F32)] + sems,
        compiler_params=_params(),
    )(qkv, qkv, qkv, ltot, do, *riders)


def _fox_fwd(name, qkv, off, width, bl, seq, cum_q, cum_k):
    t_rows = qkv.shape[0]
    tq, bk = _att_blocks(seq, FOX_K_BLOCK)
    nq = seq // tq
    nb = width // LANE
    qs, ks, col, lane = _att_specs(seq, tq, nq, off)

    def body(q_ref, k_ref, v_ref, cq_ref, ck_ref, ob_ref, of_ref, lse_ref):
        qi = pl.program_id(2)
        masks = _head_masks()
        qs_ = _stack_heads(q_ref[...] * SCALE, masks)
        cqs = jnp.concatenate([cq_ref[h] for h in range(PAIR)], axis=0)
        row, cix = _stacked_iotas(tq, bk)

        def step(masked):
            def go(kb, carry):
                m, lsum, acc = carry
                ks_ = pl.multiple_of(kb * bk, bk)
                kv = k_ref[pl.ds(ks_, bk), :]
                vv = v_ref[pl.ds(ks_, bk), :]
                bias = jnp.concatenate([cqs[h * tq:(h + 1) * tq] - ck_ref[h, :, pl.ds(ks_, bk)] for h in range(PAIR)],
                                       axis=0)
                z = lax.dot_general(qs_, kv, (NT, ((), ())), preferred_element_type=F32) + bias
                if masked:
                    z = jnp.where((kb * bk + cix) <= (qi * tq + row), z, NEG)
                m_new = jnp.maximum(m, jnp.max(z, axis=1, keepdims=True))
                pv = jnp.exp(z - m_new)
                alpha = jnp.exp(m - m_new)
                lsum = alpha * lsum + jnp.sum(pv, axis=1, keepdims=True)
                acc = alpha * acc + jnp.dot(pv.astype(BF16), vv, preferred_element_type=F32)
                return m_new, lsum, acc
            return go

        n_full, n_all = _key_blocks(qi, tq, bk)
        init = (jnp.full((PAIR * tq, 1), NEG, F32), jnp.zeros((PAIR * tq, 1), F32),
                jnp.zeros((PAIR * tq, LANE), F32))
        carry = lax.fori_loop(0, n_full, step(False), init)
        m, lsum, acc = lax.fori_loop(n_full, n_all, step(True), carry)
        out = _unstack_heads(acc / lsum, masks)
        ob_ref[...] = out.astype(ob_ref.dtype)
        of_ref[...] = out
        lse = m + jnp.log(lsum)
        for h in range(PAIR):
            lse_ref[h] = lse[h * tq:(h + 1) * tq]

    return pl.pallas_call(
        body, name=name, grid=(bl, nb, nq),
        in_specs=[qs(off), ks(off + nb), ks(off + 2 * nb), col, lane], out_specs=[qs(0), qs(0), col],
        out_shape=[jax.ShapeDtypeStruct((t_rows, width), BF16), jax.ShapeDtypeStruct((t_rows, width), F32),
                   jax.ShapeDtypeStruct((bl, nb * PAIR, seq, 1), F32)],
        compiler_params=_params(),
    )(qkv, qkv, qkv, cum_q, cum_k)


def _fox_bwd(name, qkv, off, width, bl, seq, cum_q, cum_k, lse, o, do):
    t_rows = qkv.shape[0]
    tq, bk = _att_blocks(seq, FOX_K_BLOCK)
    nq = seq // tq
    nb = width // LANE
    qs, ks, col, lane = _att_specs(seq, tq, nq, off)

    def body(q_ref, k_ref, v_ref, cq_ref, ck_ref, lse_ref, o_ref, do_ref,
             dq_ref, dk_ref, dv_ref, dcq_ref, dck_ref, dk_acc, dv_acc):
        qi = pl.program_id(2)

        @pl.when(qi == 0)
        def _():
            dk_acc[...] = jnp.zeros_like(dk_acc)
            dv_acc[...] = jnp.zeros_like(dv_acc)
            dck_ref[...] = jnp.zeros_like(dck_ref)

        masks = _head_masks()
        qs_ = _stack_heads(q_ref[...] * SCALE, masks)
        dof = do_ref[...]
        dos = _stack_heads(dof.astype(BF16), masks)
        prod = dof * o_ref[...]
        delta = jnp.concatenate([jnp.sum(_only(m, prod), axis=1, keepdims=True) for m in masks], axis=0)
        shift = jnp.concatenate([cq_ref[h] - lse_ref[h] for h in range(PAIR)], axis=0)
        row, cix = _stacked_iotas(tq, bk)

        def step(masked):
            def go(kb, carry):
                dq, dcq = carry
                ks_ = pl.multiple_of(kb * bk, bk)
                kv = k_ref[pl.ds(ks_, bk), :]
                vv = v_ref[pl.ds(ks_, bk), :]
                bias = jnp.concatenate(
                    [shift[h * tq:(h + 1) * tq] - ck_ref[h, :, pl.ds(ks_, bk)] for h in range(PAIR)], axis=0)
                pv = jnp.exp(lax.dot_general(qs_, kv, (NT, ((), ())), preferred_element_type=F32) + bias)
                if masked:
                    pv = jnp.where((kb * bk + cix) <= (qi * tq + row), pv, 0.0)
                dp = lax.dot_general(dos, vv, (NT, ((), ())), preferred_element_type=F32)
                ds = pv * (dp - delta)
                dsb = ds.astype(BF16)
                dq = dq + _unstack_heads(jnp.dot(dsb, kv * SCALE, preferred_element_type=F32), masks)
                dk_acc[pl.ds(ks_, bk), :] += lax.dot_general(dsb, qs_, (TN, ((), ())), preferred_element_type=F32)
                dv_acc[pl.ds(ks_, bk), :] += lax.dot_general(pv.astype(BF16), dos, (TN, ((), ())),
                                                             preferred_element_type=F32)
                for h in range(PAIR):
                    dck_ref[h, :, pl.ds(ks_, bk)] += -jnp.sum(ds[h * tq:(h + 1) * tq], axis=0, keepdims=True)
                return dq, dcq + jnp.sum(ds, axis=1, keepdims=True)
            return go

        n_full, n_all = _key_blocks(qi, tq, bk)
        carry = lax.fori_loop(0, n_full, step(False), (jnp.zeros((tq, LANE), F32), jnp.zeros((PAIR * tq, 1), F32)))
        dq, dcq = lax.fori_loop(n_full, n_all, step(True), carry)
        dq_ref[...] = dq.astype(dq_ref.dtype)
        for h in range(PAIR):
            dcq_ref[h] = dcq[h * tq:(h + 1) * tq]

        @pl.when(qi == nq - 1)
        def _():
            dk_ref[...] = dk_acc[...].astype(dk_ref.dtype)
            dv_ref[...] = dv_acc[...].astype(dv_ref.dtype)

    ob = jax.ShapeDtypeStruct((t_rows, width), BF16)
    nh = nb * PAIR
    return pl.pallas_call(
        body, name=name, grid=(bl, nb, nq),
        in_specs=[qs(off), ks(off + nb), ks(off + 2 * nb), col, lane, col, qs(0), qs(0)],
        out_specs=[qs(0), ks(0), ks(0), col, lane],
        out_shape=[ob, ob, ob, jax.ShapeDtypeStruct((bl, nh, seq, 1), F32), jax.ShapeDtypeStruct((bl, nh, 1, seq), F32)],
        scratch_shapes=[pltpu.VMEM((seq, LANE), F32), pltpu.VMEM((seq, LANE), F32)],
        compiler_params=_params(),
    )(qkv, qkv, qkv, cum_q, cum_k, lse, o, do)


def _lane_cumsum(v, reverse):
    n = v.shape[1]
    cix = lax.broadcasted_iota(jnp.int32, v.shape, 1)
    s = 1
    while s < n:
        if reverse:
            v = v + jnp.where(cix < n - s, pltpu.roll(v, n - s, 1), 0.0)
        else:
            v = v + jnp.where(cix >= s, pltpu.roll(v, s, 1), 0.0)
        s *= 2
    return v


def _forget_cum(name, fl, bf):
    def body(fl_ref, bf_ref, o_ref):
        xv = fl_ref[...] + bf_ref[...]
        o_ref[...] = _lane_cumsum(-_softplus(-xv), False)

    return pl.pallas_call(body, name=name, out_shape=jax.ShapeDtypeStruct(fl.shape, F32),
                          compiler_params=_params())(fl, bf)


def _forget_cum_bwd(name, fl, bf, dcum, nh):
    rows = fl.shape[0]

    def body(fl_ref, bf_ref, dc_ref, dfl_ref, dbf_ref):
        xv = fl_ref[...] + bf_ref[...]
        dlogf = _lane_cumsum(dc_ref[...], True)
        dfl = dlogf * _sigmoid(-xv)
        dfl_ref[...] = dfl
        per_row = jnp.sum(dfl, axis=1, keepdims=True)
        tot = per_row[0:nh]
        for b in range(1, rows // nh):
            tot = tot + per_row[b * nh:(b + 1) * nh]
        dbf_ref[...] = tot

    return pl.pallas_call(
        body, name=name,
        out_shape=[jax.ShapeDtypeStruct(fl.shape, F32), jax.ShapeDtypeStruct((nh, 1), F32)],
        compiler_params=_params(),
    )(fl, bf, dcum)


def _merge_fwd(name, proj, off, merge_b, pa, pb, pc):
    t_rows, d = pa.shape
    tm = _tile(t_rows, 256)

    def body(l0, l1, l2, mb, a_ref, b_ref, c_ref, o_ref):
        g0 = _sigmoid(l0[...] + mb[:, 0:d])
        g1 = _sigmoid(l1[...] + mb[:, d:2 * d])
        g2 = _sigmoid(l2[...] + mb[:, 2 * d:3 * d])
        o_ref[...] = (g0 * a_ref[...] + g1 * b_ref[...] + g2 * c_ref[...]).astype(o_ref.dtype)

    row = pl.BlockSpec((tm, d), lambda i: (i, 0))
    lg = lambda j: pl.BlockSpec((tm, d), lambda i: (i, off + j))
    return pl.pallas_call(
        body, name=name, grid=(t_rows // tm,),
        in_specs=[lg(0), lg(1), lg(2), pl.BlockSpec((1, 3 * d), lambda i: (0, 0)), row, row, row],
        out_specs=row, out_shape=jax.ShapeDtypeStruct((t_rows, d), BF16), compiler_params=_params(),
    )(proj, proj, proj, merge_b, pa, pb, pc)


def _merge_bwd(name, proj, off, merge_b, pa, pb, pc, dmixed):
    t_rows, d = pa.shape
    tm = _tile(t_rows, 256)

    def body(l0, l1, l2, mb, a_ref, b_ref, c_ref, dm_ref, da_ref, db_ref, dc_ref, dl_ref, dmb_ref):
        i = pl.program_id(0)
        dm = dm_ref[...]
        parts = []
        for j, (lref, pref, dref) in enumerate(((l0, a_ref, da_ref), (l1, b_ref, db_ref), (l2, c_ref, dc_ref))):
            g = _sigmoid(lref[...] + mb[:, j * d:(j + 1) * d])
            dref[...] = (g * dm).astype(dref.dtype)
            dl = dm * pref[...] * g * (1.0 - g)
            dl_ref[:, j * d:(j + 1) * d] = dl.astype(dl_ref.dtype)
            parts.append(jnp.sum(dl, axis=0, keepdims=True))
        tot = jnp.concatenate(parts, axis=1)

        @pl.when(i == 0)
        def _():
            dmb_ref[...] = tot

        @pl.when(i > 0)
        def _():
            dmb_ref[...] += tot

    row = pl.BlockSpec((tm, d), lambda i: (i, 0))
    lg = lambda j: pl.BlockSpec((tm, d), lambda i: (i, off + j))
    one = pl.BlockSpec((1, 3 * d), lambda i: (0, 0))
    b16 = jax.ShapeDtypeStruct((t_rows, d), BF16)
    return pl.pallas_call(
        body, name=name, grid=(t_rows // tm,),
        in_specs=[lg(0), lg(1), lg(2), one, row, row, row, row],
        out_specs=[row, row, row, pl.BlockSpec((tm, 3 * d), lambda i: (i, 0)), one],
        out_shape=[b16, b16, b16, jax.ShapeDtypeStruct((t_rows, 3 * d), BF16), jax.ShapeDtypeStruct((1, 3 * d), F32)],
        compiler_params=_params(),
    )(proj, proj, proj, merge_b, pa, pb, pc, dmixed)


def _grouped_nn(name, a, wg, l, out_dtype):
    t_rows, kk = a.shape
    ng, ncol = wg.shape[0], wg.shape[3]
    tm = _tile(t_rows, 512)
    return _mmk(name, [a, wg],
                [pl.BlockSpec((tm, kk), lambda i, g: (i, 0)),
                 pl.BlockSpec((None, None, kk, ncol), lambda i, g: (g, l, 0, 0))],
                [(0, ALL, 1, ALL, NN)], (t_rows, ng * ncol), out_dtype, (t_rows // tm, ng),
                pl.BlockSpec((tm, ncol), lambda i, g: (i, g)))


def _grouped_nt(name, da, wg, l, out_dtype):
    t_rows = da.shape[0]
    ng, kk, ncol = wg.shape[0], wg.shape[2], wg.shape[3]
    tm = _tile(t_rows, 512)
    return _mmk(name, [da, wg],
                [pl.BlockSpec((tm, ng * ncol), lambda i: (i, 0)),
                 pl.BlockSpec((ng, None, kk, ncol), lambda i: (0, l, 0, 0))],
                [(0, (ALL, slice(g * ncol, (g + 1) * ncol)), 1, g, NT) for g in range(ng)],
                (t_rows, kk), out_dtype, (t_rows // tm,), pl.BlockSpec((tm, kk), lambda i: (i, 0)))


def _grouped_tn(name, a, da, ng, out_dtype):
    t_rows, kk = a.shape
    ncol = da.shape[1] // ng
    return _mmk(name, [a, da],
                [pl.BlockSpec((t_rows, kk), lambda g: (0, 0)), pl.BlockSpec((t_rows, ncol), lambda g: (0, g))],
                [(0, ALL, 1, ALL, TN)], (ng, kk, ncol), out_dtype, (ng,),
                pl.BlockSpec((None, kk, ncol), lambda g: (g, 0, 0)))


def _rows_nn(name, a, wr, l, out_dtype):
    t_rows = a.shape[0]
    ng, kg, n = wr.shape[0], wr.shape[2], wr.shape[3]
    tm, tn = _tile(t_rows, 512), _tile(n, 512)
    return _mmk(name, [a, wr],
                [pl.BlockSpec((tm, ng * kg), lambda i, j: (i, 0)),
                 pl.BlockSpec((ng, None, kg, tn), lambda i, j: (0, l, 0, j))],
                [(0, (ALL, slice(g * kg, (g + 1) * kg)), 1, g, NN) for g in range(ng)],
                (t_rows, n), out_dtype, (t_rows // tm, n // tn), pl.BlockSpec((tm, tn), lambda i, j: (i, j)))


def _rows_nt(name, dy, wr, l, out_dtype):
    t_rows, n = dy.shape
    ng, kg = wr.shape[0], wr.shape[2]
    tm = _tile(t_rows, 512)
    return _mmk(name, [dy, wr],
                [pl.BlockSpec((tm, n), lambda i, g: (i, 0)),
                 pl.BlockSpec((None, None, kg, n), lambda i, g: (g, l, 0, 0))],
                [(0, ALL, 1, ALL, NT)], (t_rows, ng * kg), out_dtype, (t_rows // tm, ng),
                pl.BlockSpec((tm, kg), lambda i, g: (i, g)))


def _rows_tn(name, a, dy, ng, out_dtype):
    t_rows, n = dy.shape
    kg = a.shape[1] // ng
    tn = _tile(n, 512)
    return _mmk(name, [a, dy],
                [pl.BlockSpec((t_rows, kg), lambda g, j: (0, g)), pl.BlockSpec((t_rows, tn), lambda g, j: (0, j))],
                [(0, ALL, 1, ALL, TN)], (ng, kg, n), out_dtype, (ng, n // tn),
                pl.BlockSpec((None, kg, tn), lambda g, j: (g, 0, j)))


def _mix_fwd(tag, w, l, x, mod, seq):
    t_rows, d = x.shape
    bl = t_rows // seq
    shift, scale, gate = mod
    chans, nh = w["layout"]["chans"], w["layout"]["heads"]
    width = nh * HEAD_DIM
    nb = width // LANE
    h = _normmod(tag + "_norm", x, w["mix_norm"][l][None], shift, scale, seq)
    proj = _mm(tag + "_in_a", h, w["w_a"][l], NN, F32)
    qkv = _mm(tag + "_in_b", h, w["w_b"][l], NN, BF16)
    flp = _mm(tag + "_in_f", h, w["w_f"][l], NN, F32)
    xa, r, ig, hs, ya = _rg_fwd(tag + "_rg", proj, w["rg"][l], seq, chans)
    yb, ltot = _sb_fwd(tag + "_sb", qkv, 0, width, bl, seq)
    fl = flp[:, :nh].reshape(bl, seq, nh).transpose(0, 2, 1).reshape(bl * nh, seq)
    bf = jnp.tile(w["fox_bf"][l].reshape(nh, 1), (bl, 1))
    cum = _forget_cum(tag + "_cum", fl, bf)
    cum_q = cum.reshape(bl, nh, seq, 1)
    cum_k = cum.reshape(bl, nh, 1, seq)
    yc, oc, lse = _fox_fwd(tag + "_fox", qkv, 3 * nb, width, bl, seq, cum_q, cum_k)
    pa = _rows_nn(tag + "_prg", ya, w["w_rg"], l, F32)
    pb = _grouped_nn(tag + "_psb", yb, w["w_sb"], l, F32)
    pc = _grouped_nn(tag + "_pfox", yc, w["w_fox"], l, F32)
    moff = 2 * chans // d
    mb = w["merge_b"][l][None]
    mixed = _merge_fwd(tag + "_merge", proj, moff, mb, pa, pb, pc)
    y = _rows_nn(tag + "_out", mixed, w["w_o"], l, F32)
    xn = _resid(tag + "_res", x, y, gate, 1.0, seq)
    saved = dict(x=x, h=h, proj=proj, qkv=qkv, rg=(xa, r, ig, hs), ya=ya, ltot=ltot,
                 fox=(cum_q, cum_k, lse, oc), fl=fl, bf=bf, yb=yb, yc=yc, pa=pa, pb=pb, pc=pc, mixed=mixed, y=y)
    return xn, saved


def _mix_bwd(tag, w, l, s, mod, dxo, seq, riders=()):
    x = s["x"]
    t_rows, d = x.shape
    bl = t_rows // seq
    shift, scale, gate = mod
    chans, nh = w["layout"]["chans"], w["layout"]["heads"]
    width = nh * HEAD_DIM
    nb = width // LANE
    moff = 2 * chans // d
    mb = w["merge_b"][l][None]
    ng = NUM_CHIPS
    dy, dgate = _resid_bwd(tag + "_res_bwd", dxo, s["y"], gate, 1.0, seq)
    dmixed = _rows_nt(tag + "_out_dx", dy, w["w_o"], l, F32)
    dw_o = _rows_tn(tag + "_out_dw", s["mixed"], dy, ng, BF16)
    dpa, dpb, dpc, dlog, dmb = _merge_bwd(tag + "_merge_bwd", s["proj"], moff, mb, s["pa"], s["pb"], s["pc"], dmixed)
    dya = _rows_nt(tag + "_prg_dx", dpa, w["w_rg"], l, F32)
    dw_rg = _rows_tn(tag + "_prg_dw", s["ya"], dpa, ng, BF16)
    dyb = _grouped_nt(tag + "_psb_dx", dpb, w["w_sb"], l, F32)
    dw_sb = _grouped_tn(tag + "_psb_dw", s["yb"], dpb, ng, BF16)
    dyc = _grouped_nt(tag + "_pfox_dx", dpc, w["w_fox"], l, F32)
    dw_fox = _grouped_tn(tag + "_pfox_dw", s["yc"], dpc, ng, BF16)
    qkv = s["qkv"]
    dq_b, dk_b, dv_b, *slots = _sb_bwd(tag + "_sb_bwd", qkv, 0, width, bl, seq, s["ltot"], dyb, riders)
    cum_q, cum_k, lse, oc = s["fox"]
    dq_c, dk_c, dv_c, dcq, dck = _fox_bwd(tag + "_fox_bwd", qkv, 3 * nb, width, bl, seq, cum_q, cum_k, lse, oc, dyc)
    dcum = dcq.reshape(bl * nh, seq) + dck.reshape(bl * nh, seq)
    dfl, dbf = _forget_cum_bwd(tag + "_cum_bwd", s["fl"], s["bf"], dcum, nh)
    dfl_t = dfl.reshape(bl, nh, seq).transpose(0, 2, 1).reshape(t_rows, nh)
    dflp = jnp.pad(dfl_t, ((0, 0), (0, LANE - nh))).astype(BF16)
    drgx, dgt, rg_small, dwa, dwx = _rg_bwd(tag + "_rg_bwd", s["proj"], dya, s["rg"], w["rg"][l], seq, chans)
    dproj = jnp.concatenate([drgx, dgt, dlog], axis=1)
    dqkv = jnp.concatenate([dq_b, dk_b, dv_b, dq_c, dk_c, dv_c], axis=1)
    w_a, w_b, w_f = w["w_a"][l], w["w_b"][l], w["w_f"][l]
    pa_w, pb_w = w_a.shape[1], w_b.shape[1]
    tm, tn = _tile(t_rows, 512), _tile(d, 512)
    rows = lambda n: pl.BlockSpec((tm, n), lambda i, j: (i, 0))
    wrow = lambda n: pl.BlockSpec((tn, n), lambda i, j: (j, 0))
    dh = _mmk(tag + "_in_dx", [dproj, dqkv, dflp, w_a, w_b, w_f],
              [rows(pa_w), rows(pb_w), rows(LANE), wrow(pa_w), wrow(pb_w), wrow(LANE)],
              [(0, ALL, 3, ALL, NT), (1, ALL, 4, ALL, NT), (2, ALL, 5, ALL, NT)],
              (t_rows, d), F32, (t_rows // tm, d // tn), pl.BlockSpec((tm, tn), lambda i, j: (i, j)))
    hb = s["h"]
    dw_a = _mm(tag + "_in_a_dw", hb, dproj, TN, BF16)
    dw_b = _mm(tag + "_in_b_dw", hb, dqkv, TN, BF16)
    dw_f = _mm(tag + "_in_f_dw", hb, dflp, TN, BF16)
    dx, dshift, dscale, dgain = _normmod_bwd(tag + "_norm_bwd", x, dh, dxo, w["mix_norm"][l][None], scale, seq)
    grads = dict(w_in=(dw_a, dw_b, dw_f), w_rg=dw_rg, w_sb=dw_sb, w_fox=dw_fox, w_o=dw_o, mix_norm=dgain,
                 rg_small=rg_small, rg_dwa=dwa, rg_dwx=dwx, fox_bf=dbf, merge_b=dmb)
    return dx, (dshift, dscale, dgate), grads, slots


def _silu(name, c):
    def body(c_ref, o_ref):
        v = c_ref[...]
        o_ref[...] = v * _sigmoid(v)

    return pl.pallas_call(body, name=name, out_shape=jax.ShapeDtypeStruct(c.shape, F32),
                          compiler_params=_params())(c)


def _blockdiag(wb):
    nb, bd, _ = wb.shape
    per = LANE // bd
    t = wb.reshape(nb // per, per, bd, 1, bd)
    eye = jnp.eye(per, dtype=wb.dtype).reshape(1, per, 1, per, 1)
    return (t * eye).reshape(nb // per, LANE, LANE).astype(BF16)


def _unblockdiag(t, bd):
    n = t.shape[0]
    per = LANE // bd
    t5 = t.reshape(n, per, bd, per, bd)
    return jnp.stack([t5[:, p, :, p, :] for p in range(per)], axis=1).reshape(n * per, bd, bd)


def _prepare(gw, a, d, chans, nh):
    depth = a["ada_b"].shape[0]
    wq = 3 * nh * HEAD_DIM
    o_m = 2 * chans + 2 * wq
    w = {"layout": dict(chans=chans, heads=nh)}
    for n in ("ffn1_w1", "ffn1_w3", "ffn1_w2", "ffn2_w1", "ffn2_w3", "ffn2_w2", "w_rg", "w_sb", "w_fox", "w_o"):
        w[n] = gw[n]
    for n in ("ffn1_norm", "ffn2_norm", "mix_norm", "fox_bf", "merge_b", "final_norm"):
        w[n] = a[n]
    w_a, w_b, w_f, rg = [], [], [], []
    for l in range(depth):
        full = gw["w_in"][:, l].transpose(1, 0, 2).reshape(d, -1)
        w_a.append(jnp.concatenate([full[:, :2 * chans], full[:, o_m + nh:]], axis=1))
        w_b.append(full[:, 2 * chans:o_m])
        w_f.append(jnp.pad(full[:, o_m:o_m + nh], ((0, 0), (0, LANE - nh))))
        conv_w = gw["conv_w"][:, l].transpose(1, 0, 2).reshape(-1, chans)
        rg.append(dict(conv_w=conv_w, conv_b=a["conv_b"][l][None], ba=a["rg_ba"][l][None], bx=a["rg_bx"][l][None],
                       lam=a["rg_lam"][l][None], wa=_blockdiag(a["rg_wa"][l]), wx=_blockdiag(a["rg_wx"][l])))
    w["w_a"], w["w_b"], w["w_f"], w["rg"] = w_a, w_b, w_f, rg
    return w


def _local_step(w, x, tgt, mods, fm, ride=None):
    bl, seq, d = x.shape
    t_rows = bl * seq
    depth = len(mods)
    mod3 = []
    for l in range(depth):
        m4 = mods[l].reshape(bl, 9, 1, d)
        mod3.append([(m4[:, 3 * k], m4[:, 3 * k + 1], m4[:, 3 * k + 2]) for k in range(3)])
    fm4 = fm.reshape(bl, 2, 1, d)
    saved = []
    xc = x.reshape(t_rows, d)
    for l in range(depth):
        xc, s1 = _ffn_fwd(f"l{l}_ffn1", w, l, "ffn1_", xc, mod3[l][0], seq)
        xc, s2 = _mix_fwd(f"l{l}_mix", w, l, xc, mod3[l][1], seq)
        xc, s3 = _ffn_fwd(f"l{l}_ffn2", w, l, "ffn2_", xc, mod3[l][2], seq)
        saved.append((s1, s2, s3))
    lpart, dx, dfs, dfc, dfg = _final_loss("final", xc, tgt.reshape(t_rows, d), w["final_norm"][None],
                                           fm4[:, 0], fm4[:, 1], seq)
    loss = jnp.sum(lpart[:, 0, 0])
    grads = {"final_norm": dfg, "layers": [None] * depth}
    dmods = [None] * depth
    riders, rode = (), None
    for l in reversed(range(depth)):
        s1, s2, s3 = saved[l]
        dx, dm3, g3 = _ffn_bwd(f"l{l}_ffn2", w, l, "ffn2_", s3, mod3[l][2], dx, seq)
        dx, dm2, g2, slots = _mix_bwd(f"l{l}_mix", w, l, s2, mod3[l][1], dx, seq, riders)
        dx, dm1, g1 = _ffn_bwd(f"l{l}_ffn1", w, l, "ffn1_", s1, mod3[l][0], dx, seq)
        dmods[l] = jnp.concatenate([*dm1, *dm2, *dm3], axis=1).reshape(bl, 9 * d)
        grads["layers"][l] = {**g1, **g2, **g3}
        if riders:
            rode = (riders, slots)
        riders = tuple(ride(grads["layers"][l])) if ride is not None and l > 0 else ()
    dfm = jnp.concatenate([dfs, dfc], axis=1).reshape(bl, 2 * d)
    return loss, dx.reshape(bl, seq, d), grads, dmods, dfm, rode


def _mesh_pos():
    return lax.axis_index("x"), lax.axis_index("y"), lax.axis_index("c")


def _other_chips(x, y):
    return ((1 - x, y), (x, 1 - y), (1 - x, 1 - y))


def _gather_two_level(name, arrs):
    n = len(arrs)

    def body(*refs):
        ins, outs = refs[:n], refs[n:2 * n]
        send, recv, send2, recv2, send3, recv3 = refs[2 * n:]
        x, y, c = _mesh_pos()
        me = 2 * x + y
        chips = _other_chips(x, y)
        sib = (x, y, 1 - c)
        own = [pltpu.make_async_remote_copy(
            src_ref=ins[i], dst_ref=outs[i].at[me], send_sem=send3.at[i], recv_sem=recv3.at[i],
            device_id=sib, device_id_type=MESH) for i in range(n)]
        first = []
        for j, (px, py) in enumerate(chips):
            for i in range(n):
                first.append(pltpu.make_async_remote_copy(
                    src_ref=ins[i].at[c], dst_ref=outs[i].at[me, c], send_sem=send.at[j * n + i],
                    recv_sem=recv.at[j * n + i], device_id=(px, py, c), device_id_type=MESH))
        for cp in first + own:
            cp.start()
        passed = []
        for j, (px, py) in enumerate(chips):
            for i in range(n):
                landed = outs[i].at[2 * px + py, c]
                pltpu.make_async_remote_copy(
                    src_ref=ins[i].at[c], dst_ref=landed, send_sem=send.at[j * n + i],
                    recv_sem=recv.at[j * n + i], device_id=(px, py, c), device_id_type=MESH).wait_recv()
                fwd = pltpu.make_async_remote_copy(
                    src_ref=landed, dst_ref=landed, send_sem=send2.at[j * n + i],
                    recv_sem=recv2.at[j * n + i], device_id=sib, device_id_type=MESH)
                fwd.start()
                passed.append(fwd)
        for j, (px, py) in enumerate(chips):
            for i in range(n):
                theirs = outs[i].at[2 * px + py, 1 - c]
                pltpu.make_async_remote_copy(
                    src_ref=theirs, dst_ref=theirs, send_sem=send2.at[j * n + i],
                    recv_sem=recv2.at[j * n + i], device_id=sib, device_id_type=MESH).wait_recv()
        for cp in first + passed:
            cp.wait_send()
        for cp in own:
            cp.wait()

    return pl.pallas_call(
        body, name=name, in_specs=[ANY] * n, out_specs=[ANY] * n,
        out_shape=[jax.ShapeDtypeStruct((NUM_CHIPS,) + a.shape, a.dtype) for a in arrs],
        scratch_shapes=[pltpu.SemaphoreType.DMA((3 * n,)), pltpu.SemaphoreType.DMA((3 * n,)),
                        pltpu.SemaphoreType.DMA((3 * n,)), pltpu.SemaphoreType.DMA((3 * n,)),
                        pltpu.SemaphoreType.DMA((n,)), pltpu.SemaphoreType.DMA((n,))],
    )(*arrs)


def _split_to_sibling(name, arrs):
    n = len(arrs)
    slabs = arrs[0].shape[0]

    def body(*refs):
        ins, theirs = refs[:n], refs[n:2 * n]
        send, recv = refs[2 * n:]
        x, y, c = _mesh_pos()
        sib = (x, y, 1 - c)
        for i in range(n):
            for s in range(slabs):
                pltpu.make_async_remote_copy(
                    src_ref=ins[i].at[s, 1 - c], dst_ref=theirs[i].at[s], send_sem=send.at[i],
                    recv_sem=recv.at[i], device_id=sib, device_id_type=MESH).start()
        for i in range(n):
            pltpu.make_async_remote_copy(
                src_ref=ins[i].at[:, 0], dst_ref=theirs[i], send_sem=send.at[i], recv_sem=recv.at[i],
                device_id=sib, device_id_type=MESH).wait()

    return pl.pallas_call(
        body, name=name, in_specs=[ANY] * n, out_specs=[ANY] * n,
        out_shape=[jax.ShapeDtypeStruct((a.shape[0],) + a.shape[2:], a.dtype) for a in arrs],
        scratch_shapes=[pltpu.SemaphoreType.DMA((n,)), pltpu.SemaphoreType.DMA((n,))],
    )(*arrs)


def _scatter_chips(name, arrs):
    n = len(arrs)

    def body(*refs):
        ins, outs = refs[:n], refs[n:2 * n]
        send, recv = refs[2 * n:]
        x, y, c = _mesh_pos()
        chips = _other_chips(x, y)
        sends = []
        for j, (px, py) in enumerate(chips):
            for i in range(n):
                sends.append(pltpu.make_async_remote_copy(
                    src_ref=ins[i].at[2 * px + py], dst_ref=outs[i].at[j], send_sem=send.at[j * n + i],
                    recv_sem=recv.at[j * n + i], device_id=(px, py, c), device_id_type=MESH))
        for s in sends:
            s.start()
        for s in sends:
            s.wait()

    return pl.pallas_call(
        body, name=name, in_specs=[ANY] * n, out_specs=[ANY] * n,
        out_shape=[jax.ShapeDtypeStruct((NUM_CHIPS - 1,) + a.shape[1:], a.dtype) for a in arrs],
        scratch_shapes=[pltpu.SemaphoreType.DMA((3 * n,)), pltpu.SemaphoreType.DMA((3 * n,))],
    )(*arrs)


def _join_halves(name, arrs):
    n = len(arrs)

    def body(*refs):
        ins, outs = refs[:n], refs[n:2 * n]
        send, recv = refs[2 * n:]
        x, y, c = _mesh_pos()
        copies = [pltpu.make_async_remote_copy(
            src_ref=ins[i], dst_ref=outs[i], send_sem=send.at[i], recv_sem=recv.at[i],
            device_id=(x, y, 1 - c), device_id_type=MESH) for i in range(n)]
        for cp in copies:
            cp.start()
        for cp in copies:
            cp.wait()

    return pl.pallas_call(
        body, name=name, in_specs=[ANY] * n, out_specs=[ANY] * n,
        out_shape=[jax.ShapeDtypeStruct(a.shape, a.dtype) for a in arrs],
        scratch_shapes=[pltpu.SemaphoreType.DMA((n,)), pltpu.SemaphoreType.DMA((n,))],
    )(*arrs)


def _gather_all(name, pack):
    def body(in_ref, out_ref, send, recv, loc):
        x, y, c = _mesh_pos()
        me = 4 * x + 2 * y + c
        mine = pltpu.make_async_copy(in_ref, out_ref.at[me], loc)
        mine.start()
        peers = []
        for mask in range(1, NUM_DEVICES):
            px = 1 - x if mask & 4 else x
            py = 1 - y if mask & 2 else y
            pc = 1 - c if mask & 1 else c
            peers.append((px, py, pc))
        sends = [pltpu.make_async_remote_copy(
            src_ref=in_ref, dst_ref=out_ref.at[me], send_sem=send.at[k], recv_sem=recv.at[k],
            device_id=p, device_id_type=MESH) for k, p in enumerate(peers)]
        for s in sends:
            s.start()
        for k, (px, py, pc) in enumerate(peers):
            pltpu.make_async_remote_copy(
                src_ref=in_ref, dst_ref=out_ref.at[4 * px + 2 * py + pc], send_sem=send.at[k], recv_sem=recv.at[k],
                device_id=(px, py, pc), device_id_type=MESH).wait_recv()
        for s in sends:
            s.wait_send()
        mine.wait()

    return pl.pallas_call(
        body, name=name, in_specs=[ANY], out_specs=ANY,
        out_shape=jax.ShapeDtypeStruct((NUM_DEVICES,) + pack.shape, pack.dtype),
        scratch_shapes=[pltpu.SemaphoreType.DMA((NUM_DEVICES - 1,)), pltpu.SemaphoreType.DMA((NUM_DEVICES - 1,)),
                        pltpu.SemaphoreType.DMA],
    )(pack)


def _sum_slots(name, slots, out_dtype):
    g, rows, cols = slots.shape
    tr = _rtile(rows, 256)

    def body(s_ref, o_ref):
        acc = s_ref[0].astype(F32)
        for k in range(1, g):
            acc = acc + s_ref[k].astype(F32)
        o_ref[...] = acc.astype(o_ref.dtype)

    return pl.pallas_call(
        body, name=name, grid=(rows // tr,),
        in_specs=[pl.BlockSpec((g, tr, cols), lambda i: (0, i, 0))],
        out_specs=pl.BlockSpec((tr, cols), lambda i: (i, 0)),
        out_shape=jax.ShapeDtypeStruct((rows, cols), out_dtype), compiler_params=_params(),
    )(slots)


def _add_pair(name, p, q, core):
    g, _, rows, cols = p.shape
    tr = _rtile(rows, 128)

    def body(c_ref, p_ref, q_ref, o_ref):
        mine = jnp.where(c_ref[0] == 0, p_ref[:, 0].astype(F32), p_ref[:, 1].astype(F32))
        o_ref[...] = (mine + q_ref[...].astype(F32)).astype(o_ref.dtype)

    spec = pl.BlockSpec((g, tr, cols), lambda i: (0, i, 0))
    return pl.pallas_call(
        body, name=name, grid=(rows // tr,),
        in_specs=[SCALAR, pl.BlockSpec((g, 2, tr, cols), lambda i: (0, 0, i, 0)), spec],
        out_specs=spec, out_shape=jax.ShapeDtypeStruct(q.shape, BF16), compiler_params=_params(),
    )(core, p, q)


def _sum_chips(name, slots, part, chip):
    g, rows, cols = part.shape
    tr = _rtile(rows, 128)

    def body(c_ref, s_ref, p_ref, o_ref):
        acc = p_ref[c_ref[0]].astype(F32)
        for k in range(slots.shape[0]):
            acc = acc + s_ref[k].astype(F32)
        o_ref[...] = acc

    return pl.pallas_call(
        body, name=name, grid=(rows // tr,),
        in_specs=[SCALAR,
                  pl.BlockSpec((slots.shape[0], tr, cols), lambda i: (0, i, 0)),
                  pl.BlockSpec((g, tr, cols), lambda i: (0, i, 0))],
        out_specs=pl.BlockSpec((tr, cols), lambda i: (i, 0)),
        out_shape=jax.ShapeDtypeStruct((rows, cols), F32), compiler_params=_params(),
    )(chip, slots, part)


def _adamw(name, g, w, m, v, l=None):
    rows, cols = g.shape
    tr = _rtile(rows, 128)

    def body(g_ref, w_ref, m_ref, v_ref, d_o, m_o, v_o):
        gv = g_ref[...]
        mn = ADAM_B1 * m_ref[...] + (1.0 - ADAM_B1) * gv
        vn = ADAM_B2 * v_ref[...] + (1.0 - ADAM_B2) * (gv * gv)
        m_hat = mn / (1.0 - ADAM_B1 ** ADAM_STEP)
        v_hat = vn / (1.0 - ADAM_B2 ** ADAM_STEP)
        d_o[...] = -ADAM_LR * (m_hat / (jnp.sqrt(v_hat) + ADAM_EPS) + ADAM_WD * w_ref[...])
        m_o[...] = mn
        v_o[...] = vn

    gspec = pl.BlockSpec((tr, cols), lambda i: (i, 0))
    wspec = gspec if l is None else pl.BlockSpec((None, tr, cols), lambda i: (l, i, 0))
    f = jax.ShapeDtypeStruct((rows, cols), F32)
    return pl.pallas_call(
        body, name=name, grid=(rows // tr,), in_specs=[gspec] + [wspec] * 3, out_specs=[gspec] * 3,
        out_shape=[f] * 3, compiler_params=_params(),
    )(g, w, m, v)


def _adamw_layers(name, g0, g1, w, m, v):
    rows, cols = g0.shape
    tr = _rtile(rows, 128)
    nt = rows // tr

    def body(g0_ref, g1_ref, w_ref, m_ref, v_ref, g_o, d_o, m_o, v_o):
        gv = jnp.where(pl.program_id(0) == 0, g0_ref[...], g1_ref[...])
        _adamw_math(gv, w_ref, m_ref, v_ref, g_o, d_o, m_o, v_o)

    g0spec = pl.BlockSpec((tr, cols), lambda l, i: (i * (1 - l) + (nt - 1) * l, 0))
    g1spec = pl.BlockSpec((tr, cols), lambda l, i: (i * l, 0))
    wspec = pl.BlockSpec((None, tr, cols), lambda l, i: (l, i, 0))
    f = jax.ShapeDtypeStruct((2, rows, cols), F32)
    return pl.pallas_call(
        body, name=name, grid=(2, nt), in_specs=[g0spec, g1spec, wspec, wspec, wspec], out_specs=[wspec] * 4,
        out_shape=[f] * 4, compiler_params=_params(),
    )(g0, g1, w, m, v)


def _adamw_math(gv, w_ref, m_ref, v_ref, g_o, d_o, m_o, v_o):
    mn = ADAM_B1 * m_ref[...] + (1.0 - ADAM_B1) * gv
    vn = ADAM_B2 * v_ref[...] + (1.0 - ADAM_B2) * (gv * gv)
    m_hat = mn / (1.0 - ADAM_B1 ** ADAM_STEP)
    v_hat = vn / (1.0 - ADAM_B2 ** ADAM_STEP)
    g_o[...] = gv
    d_o[...] = -ADAM_LR * (m_hat / (jnp.sqrt(v_hat) + ADAM_EPS) + ADAM_WD * w_ref[...])
    m_o[...] = mn
    v_o[...] = vn


def _adamw_halves(name, mine, theirs, core, w, m, v):
    half, cols = mine[0].shape
    tr = _rtile(half, 128)
    nt = half // tr

    def body(c_ref, a0, b0, a1, b1, w_ref, m_ref, v_ref, g_o, d_o, m_o, v_o):
        first = pl.program_id(0) == 0
        own = pl.program_id(1) == c_ref[0]
        gv = jnp.where(first, jnp.where(own, a0[...], b0[...]), jnp.where(own, a1[...], b1[...]))
        _adamw_math(gv, w_ref, m_ref, v_ref, g_o, d_o, m_o, v_o)

    lay0 = pl.BlockSpec((tr, cols), lambda l, h, i: (i * (1 - l) + (nt - 1) * l, 0))
    lay1 = pl.BlockSpec((tr, cols), lambda l, h, i: (i * l, 0))
    wspec = pl.BlockSpec((None, tr, cols), lambda l, h, i: (l, h * nt + i, 0))
    f = jax.ShapeDtypeStruct((2, 2 * half, cols), F32)
    return pl.pallas_call(
        body, name=name, grid=(2, 2, nt), in_specs=[SCALAR, lay0, lay0, lay1, lay1, wspec, wspec, wspec],
        out_specs=[wspec] * 4, out_shape=[f] * 4, compiler_params=_params(),
    )(core, mine[0], theirs[0], mine[1], theirs[1], w, m, v)


def _colsum(name, a):
    def body(a_ref, o_ref):
        o_ref[...] = jnp.sum(a_ref[...], axis=0, keepdims=True)

    return pl.pallas_call(body, name=name, out_shape=jax.ShapeDtypeStruct((1, a.shape[1]), F32),
                          compiler_params=_params())(a)


PACK_UNIT = SUBLANE * LANE


def _pack(items):
    flat, layout, o = [], [], 0
    for it in items:
        n = it.size
        pad = -n % PACK_UNIT
        flat.append(jnp.pad(it.reshape(-1).astype(F32), (0, pad)))
        layout.append((o, n, it.shape))
        o += n + pad
    return jnp.concatenate(flat).reshape(-1, LANE), layout


def _unpack(pack, layout):
    flat = pack.reshape(-1)
    return [flat[o:o + n].reshape(shape) for o, n, shape in layout]


WEIGHTS = ("ffn1_norm", "ffn1_w1", "ffn1_w3", "ffn1_w2", "mix_norm", "w_in", "conv_w", "conv_b", "rg_wa", "rg_ba",
           "rg_wx", "rg_bx", "rg_lam", "fox_bf", "merge_b", "w_rg", "w_sb", "w_fox", "w_o", "ffn2_norm", "ffn2_w1",
           "ffn2_w3", "ffn2_w2", "ada_w", "ada_b", "final_norm", "final_ada_w", "final_ada_b")
DENSE = ("ffn1_w1", "ffn1_w3", "ffn1_w2", "w_in", "w_rg", "w_sb", "w_fox", "w_o", "ffn2_w1", "ffn2_w3", "ffn2_w2")
SMALL = ("ffn1_norm", "mix_norm", "ffn2_norm", "rg_small", "rg_wa", "rg_wx", "fox_bf", "merge_b")


def _step(a):
    x, c, tgt = a["x"], a["c"], a["loss_target"]
    bl, seq, d = x.shape
    depth, nh = a["fox_bf"].shape
    chans = a["rg_lam"].shape[1]
    bd = a["rg_wa"].shape[2]
    wq = 3 * nh * HEAD_DIM
    o_m = 2 * chans + 2 * wq
    batch = NUM_DEVICES * bl
    mx, my, mc = _mesh_pos()
    me = 2 * mx + my
    dev = 4 * mx + 2 * my + mc

    c_rows = -(-bl * d // LANE // SUBLANE) * SUBLANE
    c_pack = jnp.pad(c.reshape(-1, LANE), ((0, c_rows - bl * d // LANE), (0, 0)))
    c_all = _gather_all("gather_c", c_pack)[:, :bl * d // LANE].reshape(batch, d)
    c_act = _silu("c_act", c_all)
    c_b = c_act.astype(BF16)
    ncol, fcol = a["ada_w"].shape[2], a["final_ada_w"].shape[1]
    cols = []
    for l in range(depth):
        bias = jnp.broadcast_to(lax.dynamic_slice_in_dim(a["ada_b"][l], me * ncol, ncol)[None], (batch, ncol))
        cols.append(_mm(f"ada{l}", c_b, a["ada_w"][l].astype(BF16), NN, F32, acc=bias))
    bias = jnp.broadcast_to(lax.dynamic_slice_in_dim(a["final_ada_b"], me * fcol, fcol)[None], (batch, fcol))
    cols.append(_mm("ada_final", c_b, a["final_ada_w"].astype(BF16), NN, F32, acc=bias))
    mod_cols = jnp.concatenate(cols, axis=1).reshape(2, batch // 2, depth * ncol + fcol)

    names = DENSE + ("conv_w", "mod_cols")
    got = _gather_two_level("gather_weights", [a[n].astype(BF16) for n in DENSE] + [a["conv_w"], mod_cols])
    gw = dict(zip(names, got))
    w = _prepare(gw, a, d, chans, nh)
    mod_all = gw["mod_cols"].reshape(NUM_CHIPS, batch, -1)
    mine = lambda full: lax.dynamic_slice_in_dim(full, dev * bl, bl, axis=0)
    mods = [mine(mod_all[:, :, l * ncol:(l + 1) * ncol].transpose(1, 0, 2).reshape(batch, NUM_CHIPS * ncol))
            for l in range(depth)]
    fm = mine(mod_all[:, :, depth * ncol:].transpose(1, 0, 2).reshape(batch, NUM_CHIPS * fcol))

    core = jnp.reshape(mc, (1,)).astype(jnp.int32)
    chip = jnp.reshape(me, (1,)).astype(jnp.int32)

    def chip_partials(gl, tag):
        rs_in = []
        for n in DENSE:
            if n == "w_in":
                ga, gb, gf = gl["w_in"]
                orig = jnp.concatenate([ga[:, :2 * chans], gb, gf[:, :nh], ga[:, 2 * chans:]], axis=1)
                rs_in.append(orig.reshape(d, NUM_CHIPS, -1).transpose(1, 0, 2))
            else:
                rs_in.append(gl[n])
        rs_in = [g.reshape(g.shape[0], 2, g.shape[1] // 2, g.shape[2]) for g in rs_in]
        theirs = _split_to_sibling(f"split_grads_{tag}", rs_in)
        return [_add_pair(f"add_cores_{tag}_{k}", g, t, core) for k, (g, t) in enumerate(zip(rs_in, theirs))]

    assert depth == 2
    loss, grad_x, grads, dmods, dfm, rode = _local_step(w, x, tgt, mods, fm, lambda gl: chip_partials(gl, "l1"))
    loss = lax.psum(loss, ("x", "y", "c"))
    part0 = chip_partials(grads["layers"][0], "l0")
    chip_part = part0 + list(rode[0])
    slots = list(_scatter_chips("scatter_grads", part0)) + list(rode[1])
    reduced = [_sum_chips(f"sum_chips_{k}", s, p, chip) for k, (s, p) in enumerate(zip(slots, chip_part))]
    other = _join_halves("join_grads", reduced)

    out = {}

    def put(n, res, per_layer):
        for kind, val in zip(("grad_", "delta_", "new_m_", "new_v_"), res):
            out[kind + n] = jnp.stack(val).reshape(a[n].shape) if per_layer else val.reshape(a[n].shape)

    def flat3(v):
        return v.reshape(depth, -1, v.shape[-1])

    assert depth == 2
    nd = len(DENSE)
    for k, n in enumerate(DENSE):
        put(n, _adamw_halves(f"adamw_{n}", (reduced[k], reduced[nd + k]), (other[k], other[nd + k]), core,
                             flat3(a[n]), flat3(a["m_" + n]), flat3(a["v_" + n])), False)

    items = []
    for l in range(depth):
        g = grads["layers"][l]
        items += [g["ffn1_norm"], g["mix_norm"], g["ffn2_norm"], g["rg_small"], _unblockdiag(g["rg_dwa"], bd),
                  _unblockdiag(g["rg_dwx"], bd), g["fox_bf"], g["merge_b"], dmods[l]]
    items += [grads["final_norm"], dfm]
    pack, layout = _pack(items)
    gath = _gather_all("gather_small", pack)
    tot = _sum_slots("sum_small", gath, F32)

    def wpack(pre):
        its = []
        for l in range(depth):
            rg_rows = jnp.concatenate([jnp.zeros((4, chans), F32), a[pre + "conv_b"][l][None], a[pre + "rg_ba"][l][None],
                                       a[pre + "rg_bx"][l][None], a[pre + "rg_lam"][l][None]], axis=0)
            its += [a[pre + "ffn1_norm"][l], a[pre + "mix_norm"][l], a[pre + "ffn2_norm"][l], rg_rows,
                    a[pre + "rg_wa"][l], a[pre + "rg_wx"][l], a[pre + "fox_bf"][l], a[pre + "merge_b"][l],
                    jnp.zeros((bl, 9 * d), F32)]
        its += [a[pre + "final_norm"], jnp.zeros((bl, 2 * d), F32)]
        return _pack(its)[0]

    res_small = [_unpack(r, layout) for r in [tot] + list(_adamw("adamw_small", tot, wpack(""), wpack("m_"), wpack("v_")))]
    per = len(SMALL) + 1
    for j, n in enumerate(SMALL):
        if n == "rg_small":
            for row, nm in ((4, "conv_b"), (5, "rg_ba"), (6, "rg_bx"), (7, "rg_lam")):
                put(nm, [[r[l * per + j][row] for l in range(depth)] for r in res_small], True)
        else:
            put(n, [[r[l * per + j] for l in range(depth)] for r in res_small], True)
    put("final_norm", [r[depth * per] for r in res_small], False)

    gflat = gath.reshape(NUM_DEVICES, -1)

    def rows_of(idx):
        o, n, shape = layout[idx]
        return gflat[:, o:o + n].reshape(NUM_DEVICES * shape[0], shape[1])

    late_g, ada = [], []
    for l in range(depth):
        dmod_all = rows_of(l * per + per - 1)
        late_g.append(_colsum(f"ada_b_grad_{l}", dmod_all))
        cut = lax.dynamic_slice_in_dim(dmod_all, me * ncol, ncol, axis=1).astype(BF16)
        ada.append(_mm(f"ada_w_grad_{l}", c_b, cut, TN, F32))
    put("ada_w", _adamw_layers("adamw_ada_w", ada[0], ada[1], a["ada_w"], a["m_ada_w"], a["v_ada_w"]), False)
    dfm_all = rows_of(depth * per + 1)
    late_g.append(_colsum("final_ada_b_grad", dfm_all))
    cut = lax.dynamic_slice_in_dim(dfm_all, me * fcol, fcol, axis=1).astype(BF16)
    gl = _mm("final_ada_w_grad", c_b, cut, TN, F32)
    put("final_ada_w", [gl] + list(_adamw("adamw_final_ada_w", gl, a["final_ada_w"], a["m_final_ada_w"],
                                          a["v_final_ada_w"])), False)
    cshard = a["conv_w"].shape[2]
    for l in range(depth):
        rg_tot = res_small[0][l * per + SMALL.index("rg_small")]
        late_g.append(lax.dynamic_slice_in_dim(rg_tot[:4], me * cshard, cshard, axis=1))
    gp2, layout2 = _pack(late_g)

    def wpack2(pre):
        return _pack([a[pre + "ada_b"][l][None] for l in range(depth)] + [a[pre + "final_ada_b"][None]]
                     + [a[pre + "conv_w"][l] for l in range(depth)])[0]

    res_late = [_unpack(r, layout2) for r in [gp2] + list(_adamw("adamw_late", gp2, wpack2(""), wpack2("m_"), wpack2("v_")))]
    put("ada_b", [[r[l] for l in range(depth)] for r in res_late], True)
    put("final_ada_b", [r[depth] for r in res_late], False)
    put("conv_w", [[r[depth + 1 + l] for l in range(depth)] for r in res_late], True)

    outs = [loss, grad_x]
    for kind in ("grad_", "delta_", "new_m_", "new_v_"):
        outs += [out[kind + n] for n in WEIGHTS]
    return tuple(outs)


def kernel(x, c, ffn1_norm, ffn1_w1, ffn1_w3, ffn1_w2, mix_norm, w_in, conv_w, conv_b, rg_wa, rg_ba, rg_wx, rg_bx, rg_lam, fox_bf, merge_b, w_rg, w_sb, w_fox, w_o, ffn2_norm, ffn2_w1, ffn2_w3, ffn2_w2, ada_w, ada_b, final_norm, final_ada_w, final_ada_b, loss_target, m_ffn1_norm, m_ffn1_w1, m_ffn1_w3, m_ffn1_w2, m_mix_norm, m_w_in, m_conv_w, m_conv_b, m_rg_wa, m_rg_ba, m_rg_wx, m_rg_bx, m_rg_lam, m_fox_bf, m_merge_b, m_w_rg, m_w_sb, m_w_fox, m_w_o, m_ffn2_norm, m_ffn2_w1, m_ffn2_w3, m_ffn2_w2, m_ada_w, m_ada_b, m_final_norm, m_final_ada_w, m_final_ada_b, v_ffn1_norm, v_ffn1_w1, v_ffn1_w3, v_ffn1_w2, v_mix_norm, v_w_in, v_conv_w, v_conv_b, v_rg_wa, v_rg_ba, v_rg_wx, v_rg_bx, v_rg_lam, v_fox_bf, v_merge_b, v_w_rg, v_w_sb, v_w_fox, v_w_o, v_ffn2_norm, v_ffn2_w1, v_ffn2_w3, v_ffn2_w2, v_ada_w, v_ada_b, v_final_norm, v_final_ada_w, v_final_ada_b):
    args = dict(locals())
    return _step(args)
```

```python
import math

import jax
import jax.numpy as jnp
from jax import lax
from jax.experimental import pallas as pl
from jax.experimental.pallas import tpu as pltpu

F32 = jnp.float32
BF16 = jnp.bfloat16

NUM_CHIPS = 4
NUM_DEVICES = 8
HEAD_DIM = 64
LANE = 128
SUBLANE = 8
VMEM_LIMIT = 56 * 1024 * 1024
EPS = 1e-6
RG_C = 8.0
ADAM_LR = 0.001
ADAM_B1 = 0.9
ADAM_B2 = 0.999
ADAM_EPS = 1e-08
ADAM_WD = 0.01
ADAM_STEP = 10
MESH = pl.DeviceIdType.MESH
ANY = pl.BlockSpec(memory_space=pl.ANY)
SCALAR = pl.BlockSpec(memory_space=pltpu.SMEM)


def _params():
    return pltpu.CompilerParams(vmem_limit_bytes=VMEM_LIMIT)


def _tile(dim, pref):
    if dim <= pref:
        return dim
    t = (pref // LANE) * LANE
    while t >= LANE:
        if dim % t == 0:
            return t
        t -= LANE
    return dim


def _rtile(rows, pref, unit=2 * SUBLANE):
    if rows <= pref:
        return rows
    t = (pref // unit) * unit
    while t >= unit:
        if rows % t == 0:
            return t
        t -= unit
    return rows


def _sigmoid(x):
    return 0.5 * jnp.tanh(0.5 * x) + 0.5


def _softplus(x):
    return jnp.maximum(x, 0.0) + jnp.log(1.0 + jnp.exp(-jnp.abs(x)))


def _expm1(x):
    small = x * (1.0 + x * (0.5 + x * (1.0 / 6.0 + x * (1.0 / 24.0))))
    return jnp.where(jnp.abs(x) < 0.01, small, jnp.exp(x) - 1.0)


_GELU_K = math.sqrt(2.0 / math.pi)


def _gelu_and_grad(x):
    inner = _GELU_K * (x + 0.044715 * x * x * x)
    t = jnp.tanh(inner)
    val = 0.5 * x * (1.0 + t)
    dinner = _GELU_K * (1.0 + 3.0 * 0.044715 * x * x)
    grad = 0.5 * (1.0 + t) + 0.5 * x * (1.0 - t * t) * dinner
    return val, grad


NN = ((1,), (0,))
NT = ((1,), (1,))
TN = ((0,), (0,))
ALL = slice(None)


def _ride_call(name, body, grid, in_specs, out_specs, out_shape, scratch_shapes, args, riders=()):
    nr = len(riders)
    if not nr:
        return pl.pallas_call(body, name=name, grid=grid, in_specs=in_specs, out_specs=out_specs, out_shape=out_shape,
                              scratch_shapes=scratch_shapes, compiler_params=_params())(*args)
    n_in, n_out, n_scr = len(in_specs), len(out_specs), len(scratch_shapes)

    def hosted(*refs):
        ins, ride_in = refs[:n_in], refs[n_in:n_in + nr]
        outs = refs[n_in + nr:n_in + nr + n_out]
        ride_out = refs[n_in + nr + n_out:n_in + 2 * nr + n_out]
        scr = refs[n_in + 2 * nr + n_out:n_in + 2 * nr + n_out + n_scr]
        send, recv = refs[-2:]
        x, y, c = _mesh_pos()
        copies = []
        for j, (px, py) in enumerate(_other_chips(x, y)):
            for i in range(nr):
                copies.append(pltpu.make_async_remote_copy(
                    src_ref=ride_in[i].at[2 * px + py], dst_ref=ride_out[i].at[j], send_sem=send.at[j * nr + i],
                    recv_sem=recv.at[j * nr + i], device_id=(px, py, c), device_id_type=MESH))
        first = pl.program_id(0) == 0
        last = pl.program_id(0) == grid[0] - 1
        for k in range(1, len(grid)):
            first = first & (pl.program_id(k) == 0)
            last = last & (pl.program_id(k) == grid[k] - 1)

        @pl.when(first)
        def _():
            for cp in copies:
                cp.start()

        body(*ins, *outs, *scr)

        @pl.when(last)
        def _():
            for cp in copies:
                cp.wait()

    slots = [jax.ShapeDtypeStruct((NUM_CHIPS - 1,) + r.shape[1:], r.dtype) for r in riders]
    return pl.pallas_call(
        hosted, name=name, grid=grid, in_specs=list(in_specs) + [ANY] * nr, out_specs=list(out_specs) + [ANY] * nr,
        out_shape=list(out_shape) + slots,
        scratch_shapes=list(scratch_shapes) + [pltpu.SemaphoreType.DMA((3 * nr,)), pltpu.SemaphoreType.DMA((3 * nr,))],
        compiler_params=_params(),
    )(*args, *riders)


def _mmk(name, ops, specs, terms, out_shape, out_dtype, grid, o_spec, acc=None, riders=()):
    n_ops = len(ops)

    def body(*refs):
        o_ref = refs[-1]
        p = None
        for ia, xa, ib, xb, dims in terms:
            t = lax.dot_general(refs[ia][xa], refs[ib][xb], (dims, ((), ())), preferred_element_type=F32)
            p = t if p is None else p + t
        if acc is not None:
            p = p + refs[n_ops][...].astype(F32)
        o_ref[...] = p.astype(o_ref.dtype)

    in_specs = list(specs)
    args = list(ops)
    if acc is not None:
        in_specs.append(pl.BlockSpec(o_spec.block_shape, o_spec.index_map))
        args.append(acc)
    res = _ride_call(name, body, grid, in_specs, [o_spec], [jax.ShapeDtypeStruct(out_shape, out_dtype)], [], args,
                     riders)
    return res if riders else res[0]


def _mm(name, a, b, dims, out_dtype, acc=None, tm=512, tn=512):
    if dims == NN:
        (m, kk), n = a.shape, b.shape[1]
    elif dims == NT:
        (m, kk), n = a.shape, b.shape[0]
    else:
        (kk, m), n = a.shape, b.shape[1]
    tm, tn = _tile(m, tm), _tile(n, tn)
    if dims == TN:
        a_spec = pl.BlockSpec((kk, tm), lambda i, j: (0, i))
    else:
        a_spec = pl.BlockSpec((tm, kk), lambda i, j: (i, 0))
    if dims == NT:
        b_spec = pl.BlockSpec((tn, kk), lambda i, j: (j, 0))
    else:
        b_spec = pl.BlockSpec((kk, tn), lambda i, j: (0, j))
    return _mmk(name, [a, b], [a_spec, b_spec], [(0, ALL, 1, ALL, dims)], (m, n), out_dtype,
                (m // tm, n // tn), pl.BlockSpec((tm, tn), lambda i, j: (i, j)), acc)


def _row_grid(t_rows, seq, pref=256):
    tm = _tile(seq, pref)
    return tm, seq // tm


def _normmod(name, x, gain, shift, scale, seq):
    t_rows, d = x.shape
    bl = t_rows // seq
    tm, per = _row_grid(t_rows, seq)

    def body(x_ref, g_ref, sh_ref, sc_ref, o_ref):
        xv = x_ref[...]
        rstd = lax.rsqrt(jnp.mean(xv * xv, axis=-1, keepdims=True) + EPS)
        hn = (xv * rstd) * g_ref[...]
        o_ref[...] = (hn * (1.0 + sc_ref[...]) + sh_ref[...]).astype(o_ref.dtype)

    row = pl.BlockSpec((tm, d), lambda b, i: (b * per + i, 0))
    vec = pl.BlockSpec((None, 1, d), lambda b, i: (b, 0, 0))
    return pl.pallas_call(
        body, name=name, grid=(bl, per),
        in_specs=[row, pl.BlockSpec((1, d), lambda b, i: (0, 0)), vec, vec],
        out_specs=row, out_shape=jax.ShapeDtypeStruct((t_rows, d), BF16),
        compiler_params=_params(),
    )(x, gain, shift, scale)


def _normmod_bwd(name, x, dh, dxo, gain, scale, seq):
    t_rows, d = x.shape
    bl = t_rows // seq
    tm, per = _row_grid(t_rows, seq)

    def body(x_ref, dh_ref, dxo_ref, g_ref, sc_ref, dx_ref, dsh_ref, dsc_ref, dg_ref):
        b, i = pl.program_id(0), pl.program_id(1)
        xv = x_ref[...]
        dhv = dh_ref[...]
        rstd = lax.rsqrt(jnp.mean(xv * xv, axis=-1, keepdims=True) + EPS)
        xhat = xv * rstd
        gain_v = g_ref[...]
        dhn = dhv * (1.0 + sc_ref[...])
        dxhat = dhn * gain_v
        dx = rstd * (dxhat - xhat * jnp.mean(dxhat * xhat, axis=-1, keepdims=True))
        dx_ref[...] = dxo_ref[...] + dx

        @pl.when(i == 0)
        def _():
            dsh_ref[...] = jnp.zeros_like(dsh_ref)
            dsc_ref[...] = jnp.zeros_like(dsc_ref)

        @pl.when((i == 0) & (b == 0))
        def _():
            dg_ref[...] = jnp.zeros_like(dg_ref)

        dsh_ref[...] += jnp.sum(dhv, axis=0, keepdims=True)
        dsc_ref[...] += jnp.sum(dhv * (xhat * gain_v), axis=0, keepdims=True)
        dg_ref[...] += jnp.sum(dhn * xhat, axis=0, keepdims=True)

    row = pl.BlockSpec((tm, d), lambda b, i: (b * per + i, 0))
    vec = pl.BlockSpec((None, 1, d), lambda b, i: (b, 0, 0))
    one = pl.BlockSpec((1, d), lambda b, i: (0, 0))
    return pl.pallas_call(
        body, name=name, grid=(bl, per),
        in_specs=[row, row, row, one, vec],
        out_specs=[row, vec, vec, one],
        out_shape=[jax.ShapeDtypeStruct((t_rows, d), F32), jax.ShapeDtypeStruct((bl, 1, d), F32),
                   jax.ShapeDtypeStruct((bl, 1, d), F32), jax.ShapeDtypeStruct((1, d), F32)],
        compiler_params=_params(),
    )(x, dh, dxo, gain, scale)


def _resid(name, x, y, gate, coef, seq):
    t_rows, d = x.shape
    bl = t_rows // seq
    tm, per = _row_grid(t_rows, seq)

    def body(x_ref, y_ref, g_ref, o_ref):
        o_ref[...] = x_ref[...] + (coef * (1.0 + g_ref[...])) * y_ref[...]

    row = pl.BlockSpec((tm, d), lambda b, i: (b * per + i, 0))
    vec = pl.BlockSpec((None, 1, d), lambda b, i: (b, 0, 0))
    return pl.pallas_call(
        body, name=name, grid=(bl, per), in_specs=[row, row, vec], out_specs=row,
        out_shape=jax.ShapeDtypeStruct((t_rows, d), F32), compiler_params=_params(),
    )(x, y, gate)


def _resid_bwd(name, dxo, y, gate, coef, seq):
    t_rows, d = dxo.shape
    bl = t_rows // seq
    tm, per = _row_grid(t_rows, seq)

    def body(dxo_ref, y_ref, g_ref, dy_ref, dg_ref):
        i = pl.program_id(1)
        dxov = dxo_ref[...]
        dy_ref[...] = ((coef * (1.0 + g_ref[...])) * dxov).astype(dy_ref.dtype)

        @pl.when(i == 0)
        def _():
            dg_ref[...] = jnp.zeros_like(dg_ref)

        dg_ref[...] += jnp.sum((coef * y_ref[...]) * dxov, axis=0, keepdims=True)

    row = pl.BlockSpec((tm, d), lambda b, i: (b * per + i, 0))
    vec = pl.BlockSpec((None, 1, d), lambda b, i: (b, 0, 0))
    return pl.pallas_call(
        body, name=name, grid=(bl, per), in_specs=[row, row, vec], out_specs=[row, vec],
        out_shape=[jax.ShapeDtypeStruct((t_rows, d), BF16), jax.ShapeDtypeStruct((bl, 1, d), F32)],
        compiler_params=_params(),
    )(dxo, y, gate)


def _final_loss(name, x, tgt, gain, shift, scale, seq):
    t_rows, d = x.shape
    bl = t_rows // seq
    tm, per = _row_grid(t_rows, seq)

    def body(x_ref, t_ref, g_ref, sh_ref, sc_ref, l_ref, dx_ref, dsh_ref, dsc_ref, dg_ref):
        b, i = pl.program_id(0), pl.program_id(1)
        xv = x_ref[...]
        rstd = lax.rsqrt(jnp.mean(xv * xv, axis=-1, keepdims=True) + EPS)
        xhat = xv * rstd
        gain_v = g_ref[...]
        hn = xhat * gain_v
        yv = hn * (1.0 + sc_ref[...]) + sh_ref[...]
        err = yv - t_ref[...]
        dyv = err * (1.0 / d)
        dhn = dyv * (1.0 + sc_ref[...])
        dxhat = dhn * gain_v
        dx_ref[...] = rstd * (dxhat - xhat * jnp.mean(dxhat * xhat, axis=-1, keepdims=True))

        @pl.when(i == 0)
        def _():
            l_ref[...] = jnp.zeros_like(l_ref)
            dsh_ref[...] = jnp.zeros_like(dsh_ref)
            dsc_ref[...] = jnp.zeros_like(dsc_ref)

        @pl.when((i == 0) & (b == 0))
        def _():
            dg_ref[...] = jnp.zeros_like(dg_ref)

        part = jnp.sum(jnp.sum(err * err, axis=-1, keepdims=True), axis=0, keepdims=True) * (0.5 / d)
        l_ref[...] += jnp.broadcast_to(part, l_ref.shape)
        dsh_ref[...] += jnp.sum(dyv, axis=0, keepdims=True)
        dsc_ref[...] += jnp.sum(dyv * hn, axis=0, keepdims=True)
        dg_ref[...] += jnp.sum(dhn * xhat, axis=0, keepdims=True)

    row = pl.BlockSpec((tm, d), lambda b, i: (b * per + i, 0))
    vec = pl.BlockSpec((None, 1, d), lambda b, i: (b, 0, 0))
    one = pl.BlockSpec((1, d), lambda b, i: (0, 0))
    lvec = pl.BlockSpec((None, 1, LANE), lambda b, i: (b, 0, 0))
    return pl.pallas_call(
        body, name=name, grid=(bl, per),
        in_specs=[row, row, one, vec, vec],
        out_specs=[lvec, row, vec, vec, one],
        out_shape=[jax.ShapeDtypeStruct((bl, 1, LANE), F32), jax.ShapeDtypeStruct((t_rows, d), F32),
                   jax.ShapeDtypeStruct((bl, 1, d), F32), jax.ShapeDtypeStruct((bl, 1, d), F32),
                   jax.ShapeDtypeStruct((1, d), F32)],
        compiler_params=_params(),
    )(x, tgt, gain, shift, scale)


def _ffn_up(name, h, w1, w3, l):
    t_rows, d = h.shape
    ng, fs = w1.shape[0], w1.shape[3]
    tm = _tile(t_rows, 512)

    def body(h_ref, w1_ref, w3_ref, a_ref, b_ref, g_ref):
        hv = h_ref[...]
        av = jnp.dot(hv, w1_ref[...], preferred_element_type=F32)
        bv = jnp.dot(hv, w3_ref[...], preferred_element_type=F32)
        a_ref[...] = av.astype(a_ref.dtype)
        b_ref[...] = bv.astype(b_ref.dtype)
        g_ref[...] = (av * _sigmoid(av) * bv).astype(g_ref.dtype)

    wspec = pl.BlockSpec((None, None, d, fs), lambda g, i: (g, l, 0, 0))
    out = pl.BlockSpec((None, tm, fs), lambda g, i: (g, i, 0))
    f = jax.ShapeDtypeStruct((ng, t_rows, fs), BF16)
    return pl.pallas_call(
        body, name=name, grid=(ng, t_rows // tm),
        in_specs=[pl.BlockSpec((tm, d), lambda g, i: (i, 0)), wspec, wspec], out_specs=[out, out, out],
        out_shape=[f, f, f], compiler_params=_params(),
    )(h, w1, w3)


def _ffn_down_dx(name, dy, w2, a, b, l, riders=()):
    t_rows, d = dy.shape
    ng, fs = w2.shape[0], w2.shape[2]
    tm = _tile(t_rows, 512)

    def body(dy_ref, w2_ref, a_ref, b_ref, da_ref, db_ref):
        dgv = lax.dot_general(dy_ref[...], w2_ref[...], (NT, ((), ())), preferred_element_type=F32)
        av = a_ref[...].astype(F32)
        sig = _sigmoid(av)
        da_ref[...] = (dgv * b_ref[...].astype(F32) * (sig * (1.0 + av * (1.0 - sig)))).astype(da_ref.dtype)
        db_ref[...] = (dgv * (av * sig)).astype(db_ref.dtype)

    blk = pl.BlockSpec((None, tm, fs), lambda g, i: (g, i, 0))
    o = jax.ShapeDtypeStruct((ng, t_rows, fs), BF16)
    return _ride_call(
        name, body, (ng, t_rows // tm),
        [pl.BlockSpec((tm, d), lambda g, i: (i, 0)), pl.BlockSpec((None, None, fs, d), lambda g, i: (g, l, 0, 0)),
         blk, blk],
        [blk, blk], [o, o], [], (dy, w2, a, b), riders)


def _ffn_up_dw(name, h, da, db):
    t_rows, d = h.shape
    ng, fs = da.shape[0], da.shape[2]
    tn = _tile(d, 512)

    def body(h_ref, da_ref, db_ref, o1_ref, o3_ref):
        hv = h_ref[...]
        o1_ref[...] = lax.dot_general(hv, da_ref[...], (TN, ((), ())), preferred_element_type=F32).astype(o1_ref.dtype)
        o3_ref[...] = lax.dot_general(hv, db_ref[...], (TN, ((), ())), preferred_element_type=F32).astype(o3_ref.dtype)

    dspec = pl.BlockSpec((None, t_rows, fs), lambda g, i: (g, 0, 0))
    out = pl.BlockSpec((None, tn, fs), lambda g, i: (g, i, 0))
    o = jax.ShapeDtypeStruct((ng, d, fs), BF16)
    return pl.pallas_call(
        body, name=name, grid=(ng, d // tn),
        in_specs=[pl.BlockSpec((t_rows, tn), lambda g, i: (0, i)), dspec, dspec],
        out_specs=[out, out], out_shape=[o, o], compiler_params=_params(),
    )(h, da, db)


def _ffn_fwd(tag, w, l, pre, x, mod, seq):
    t_rows, d = x.shape
    w1, w3, w2 = w[pre + "w1"], w[pre + "w3"], w[pre + "w2"]
    ng, fs = w1.shape[0], w1.shape[3]
    shift, scale, gate = mod
    h = _normmod(tag + "_norm", x, w[pre + "norm"][l][None], shift, scale, seq)
    a, b, gact = _ffn_up(tag + "_up", h, w1, w3, l)
    tm, tn = _tile(t_rows, 512), _tile(d, 512)
    y = _mmk(tag + "_down", [gact, w2],
             [pl.BlockSpec((ng, tm, fs), lambda i, j: (0, i, 0)),
              pl.BlockSpec((ng, None, fs, tn), lambda i, j: (0, l, 0, j))],
             [(0, g, 1, g, NN) for g in range(ng)], (t_rows, d), F32, (t_rows // tm, d // tn),
             pl.BlockSpec((tm, tn), lambda i, j: (i, j)))
    xn = _resid(tag + "_res", x, y, gate, 0.5, seq)
    return xn, (x, h, a, b, gact, y)


def _ffn_bwd(tag, w, l, pre, saved, mod, dxo, seq, riders_dn=(), riders_up=()):
    x, h, a, b, gact, y = saved
    t_rows, d = x.shape
    w1, w3, w2 = w[pre + "w1"], w[pre + "w3"], w[pre + "w2"]
    ng, fs = w1.shape[0], w1.shape[3]
    shift, scale, gate = mod
    tm, tn = _tile(t_rows, 512), _tile(d, 512)
    dy, dgate = _resid_bwd(tag + "_res_bwd", dxo, y, gate, 0.5, seq)
    da, db, *slots_dn = _ffn_down_dx(tag + "_down_dx", dy, w2, a, b, l, riders_dn)
    dw2 = _mmk(tag + "_down_dw", [gact, dy],
               [pl.BlockSpec((None, t_rows, fs), lambda g, j: (g, 0, 0)),
                pl.BlockSpec((t_rows, tn), lambda g, j: (0, j))],
               [(0, ALL, 1, ALL, TN)], (ng, fs, d), BF16, (ng, d // tn),
               pl.BlockSpec((None, fs, tn), lambda g, j: (g, 0, j)))
    dw1, dw3 = _ffn_up_dw(tag + "_up_dw", h, da, db)
    dspec = pl.BlockSpec((ng, tm, fs), lambda i, j: (0, i, 0))
    wspec = pl.BlockSpec((ng, None, tn, fs), lambda i, j: (0, l, j, 0))
    dh = _mmk(tag + "_up_dx", [da, db, w1, w3], [dspec, dspec, wspec, wspec],
              [(0, g, 2, g, NT) for g in range(ng)] + [(1, g, 3, g, NT) for g in range(ng)],
              (t_rows, d), F32, (t_rows // tm, d // tn), pl.BlockSpec((tm, tn), lambda i, j: (i, j)),
              riders=riders_up)
    slots_up = []
    if riders_up:
        dh, *slots_up = dh
    dx, dshift, dscale, dgain = _normmod_bwd(tag + "_norm_bwd", x, dh, dxo, w[pre + "norm"][l][None], scale, seq)
    grads = {pre + "w1": dw1, pre + "w3": dw3, pre + "w2": dw2, pre + "norm": dgain}
    return dx, (dshift, dscale, dgate), grads, slots_dn, slots_up


def _shift_down(v, s, row):
    if s == 0:
        return v
    return jnp.where(row >= s, pltpu.roll(v, s, 0), 0.0)


def _shift_up(v, s, row):
    if s == 0:
        return v
    n = v.shape[0]
    return jnp.where(row < n - s, pltpu.roll(v, n - s, 0), 0.0)


def _scan_fwd(a, u, row):
    n = a.shape[0]
    s = 1
    while s < n:
        ok = row >= s
        a_sh = pltpu.roll(a, s, 0)
        u_sh = pltpu.roll(u, s, 0)
        u = jnp.where(ok, a * u_sh + u, u)
        a = jnp.where(ok, a * a_sh, a)
        s *= 2
    return u


def _scan_bwd(a_next, g, row):
    n = g.shape[0]
    a, u = a_next, g
    s = 1
    while s < n:
        ok = row < n - s
        a_sh = pltpu.roll(a, n - s, 0)
        u_sh = pltpu.roll(u, n - s, 0)
        u = jnp.where(ok, a * u_sh + u, u)
        a = jnp.where(ok, a * a_sh, a)
        s *= 2
    return u


def _rg_specs(seq, cw):
    slab = lambda off: pl.BlockSpec((seq, cw), lambda c, b: (b, off + c))
    par = lambda rows: pl.BlockSpec((rows, cw), lambda c, b: (0, c))
    wbd = pl.BlockSpec((None, cw, cw), lambda c, b: (c, 0, 0))
    return slab, par, wbd


def _rg_fwd(name, proj, p, seq, chans):
    t_rows = proj.shape[0]
    bl = t_rows // seq
    cw = LANE
    nc = chans // cw
    slab, par, wbd = _rg_specs(seq, cw)

    def body(x_ref, gt_ref, cw_ref, cb_ref, wa_ref, ba_ref, wx_ref, bx_ref, lam_ref,
             xa_ref, r_ref, i_ref, h_ref, ya_ref):
        row = lax.broadcasted_iota(jnp.int32, (seq, cw), 0)
        xv = x_ref[...]
        xa = jnp.zeros_like(xv) + cb_ref[...]
        for k in range(4):
            xa = xa + cw_ref[k:k + 1, :] * _shift_down(xv, 3 - k, row)
        xab = xa.astype(BF16)
        r = _sigmoid(jnp.dot(xab, wa_ref[...], preferred_element_type=F32) + ba_ref[...])
        ig = _sigmoid(jnp.dot(xab, wx_ref[...], preferred_element_type=F32) + bx_ref[...])
        log_a = (-RG_C) * r * _softplus(-lam_ref[...])
        a = jnp.exp(log_a)
        u = jnp.sqrt(-_expm1(2.0 * log_a)) * (ig * xa)
        h = _scan_fwd(a, u, row)
        gel, _ = _gelu_and_grad(gt_ref[...])
        xa_ref[...] = xa
        r_ref[...] = r
        i_ref[...] = ig
        h_ref[...] = h
        ya_ref[...] = (gel * h).astype(ya_ref.dtype)

    out = pl.BlockSpec((seq, cw), lambda c, b: (b, c))
    f = jax.ShapeDtypeStruct((t_rows, chans), F32)
    return pl.pallas_call(
        body, name=name, grid=(nc, bl),
        in_specs=[slab(0), slab(nc), par(4), par(1), wbd, par(1), wbd, par(1), par(1)],
        out_specs=[out] * 5,
        out_shape=[f, f, f, f, jax.ShapeDtypeStruct((t_rows, chans), BF16)],
        compiler_params=_params(),
    )(proj, proj, p["conv_w"], p["conv_b"], p["wa"], p["ba"], p["wx"], p["bx"], p["lam"])


def _rg_bwd(name, proj, dya, saved, p, seq, chans):
    xa_s, r_s, i_s, h_s = saved
    t_rows = proj.shape[0]
    bl = t_rows // seq
    cw = LANE
    nc = chans // cw
    slab, par, wbd = _rg_specs(seq, cw)

    def body(x_ref, gt_ref, dya_ref, xa_ref, r_ref, i_ref, h_ref, cw_ref, wa_ref, wx_ref, lam_ref,
             dx_ref, dgt_ref, sm_ref, dwa_ref, dwx_ref):
        b = pl.program_id(1)
        row = lax.broadcasted_iota(jnp.int32, (seq, cw), 0)
        xv, xa, r, ig, h = x_ref[...], xa_ref[...], r_ref[...], i_ref[...], h_ref[...]
        dyav = dya_ref[...]
        gel, dgel = _gelu_and_grad(gt_ref[...])
        dgt_ref[...] = (dyav * h * dgel).astype(dgt_ref.dtype)
        dh = dyav * gel
        lam = lam_ref[...]
        sp = _softplus(-lam)
        log_a = (-RG_C) * r * sp
        a = jnp.exp(log_a)
        s = jnp.sqrt(-_expm1(2.0 * log_a))
        lamb = _scan_bwd(_shift_up(a, 1, row), dh, row)
        da = lamb * _shift_down(h, 1, row)
        xi = ig * xa
        ds = lamb * xi
        dxi = lamb * s
        dlog = da * a - ds * (a * a) / s
        dr = dlog * ((-RG_C) * sp)
        dsp = jnp.sum(dlog * ((-RG_C) * r), axis=0, keepdims=True)
        dlam = -dsp * _sigmoid(-lam)
        dzr = dr * r * (1.0 - r)
        dzi = (dxi * xa) * ig * (1.0 - ig)
        dzrb, dzib, xab = dzr.astype(BF16), dzi.astype(BF16), xa.astype(BF16)
        dxa = dxi * ig
        dxa = dxa + lax.dot_general(dzrb, wa_ref[...], (NT, ((), ())), preferred_element_type=F32)
        dxa = dxa + lax.dot_general(dzib, wx_ref[...], (NT, ((), ())), preferred_element_type=F32)
        dwa = lax.dot_general(xab, dzrb, (TN, ((), ())), preferred_element_type=F32)
        dwx = lax.dot_general(xab, dzib, (TN, ((), ())), preferred_element_type=F32)
        dxv = jnp.zeros_like(xv)
        rows = []
        for k in range(4):
            dxv = dxv + cw_ref[k:k + 1, :] * _shift_up(dxa, 3 - k, row)
            rows.append(jnp.sum(dxa * _shift_down(xv, 3 - k, row), axis=0, keepdims=True))
        dx_ref[...] = dxv.astype(dx_ref.dtype)
        rows += [jnp.sum(dxa, axis=0, keepdims=True), jnp.sum(dzr, axis=0, keepdims=True),
                 jnp.sum(dzi, axis=0, keepdims=True), dlam]

        @pl.when(b == 0)
        def _():
            sm_ref[...] = jnp.zeros_like(sm_ref)
            dwa_ref[...] = jnp.zeros_like(dwa_ref)
            dwx_ref[...] = jnp.zeros_like(dwx_ref)

        for k, val in enumerate(rows):
            sm_ref[k:k + 1, :] += val
        dwa_ref[...] += dwa
        dwx_ref[...] += dwx

    plain = pl.BlockSpec((seq, cw), lambda c, b: (b, c))
    return pl.pallas_call(
        body, name=name, grid=(nc, bl),
        in_specs=[slab(0), slab(nc), plain, plain, plain, plain, plain, par(4), wbd, wbd, par(1)],
        out_specs=[plain, plain, par(8), wbd, wbd],
        out_shape=[jax.ShapeDtypeStruct((t_rows, chans), BF16), jax.ShapeDtypeStruct((t_rows, chans), BF16),
                   jax.ShapeDtypeStruct((8, chans), F32),
                   jax.ShapeDtypeStruct((nc, cw, cw), F32), jax.ShapeDtypeStruct((nc, cw, cw), F32)],
        compiler_params=_params(),
    )(proj, proj, dya, xa_s, r_s, i_s, h_s, p["conv_w"], p["wa"], p["wx"], p["lam"])


ATT_Q_BLOCK = 512
SB_K_BLOCK = 256
FOX_K_BLOCK = 512
PAIR = LANE // HEAD_DIM
NEG = -1e30
SCALE = HEAD_DIM ** -0.5
assert math.log2(HEAD_DIM) % 2 == 0


def _att_blocks(seq, k_block):
    return _tile(seq, ATT_Q_BLOCK), _tile(seq, k_block)


def _key_blocks(qi, tq, bk):
    return (qi * tq) // bk, (qi * tq + tq - 1) // bk + 1


def _tri(n, kind):
    r = lax.broadcasted_iota(jnp.int32, (2 * n, n), 0)
    r = jnp.where(r >= n, r - n, r)
    c = lax.broadcasted_iota(jnp.int32, (2 * n, n), 1)
    m = {"gt": r > c, "le": r <= c, "lt": r < c}[kind]
    return m.astype(BF16)


def _cumsum_mm(v, tri):
    hi = v.astype(BF16)
    lo = (v - hi.astype(F32)).astype(BF16)
    return jnp.dot(jnp.concatenate([hi, lo], axis=1), tri, preferred_element_type=F32)


def _head_masks():
    lane = lax.broadcasted_iota(jnp.int32, (1, LANE), 1)
    return [(lane >= h * HEAD_DIM) & (lane < (h + 1) * HEAD_DIM) for h in range(PAIR)]


def _only(mask, v):
    return jnp.where(mask, v, jnp.zeros_like(v))


def _stack_heads(v, masks):
    return jnp.concatenate([_only(m, v) for m in masks], axis=0)


def _unstack_heads(v, masks):
    tq = v.shape[0] // PAIR
    out = _only(masks[0], v[0:tq])
    for h in range(1, PAIR):
        out = out + _only(masks[h], v[h * tq:(h + 1) * tq])
    return out


def _stacked_iotas(tq, bk):
    row = lax.broadcasted_iota(jnp.int32, (PAIR * tq, bk), 0)
    for h in range(1, PAIR):
        row = jnp.where(row >= h * tq, row - tq, row)
    return row, lax.broadcasted_iota(jnp.int32, (PAIR * tq, bk), 1)


def _att_specs(seq, blk, nq, off):
    npair = None
    qs = lambda o: pl.BlockSpec((blk, LANE), lambda b, p, i: (b * nq + i, o + p))
    ks = lambda o: pl.BlockSpec((seq, LANE), lambda b, p, i: (b, o + p))
    col = pl.BlockSpec((None, PAIR, blk, 1), lambda b, p, i: (b, p, i, 0))
    lane = pl.BlockSpec((None, PAIR, 1, seq), lambda b, p, i: (b, p, 0, 0))
    return qs, ks, col, lane


def _sb_fwd(name, qkv, off, width, bl, seq):
    t_rows = qkv.shape[0]
    tq, bk = _att_blocks(seq, SB_K_BLOCK)
    nq = seq // tq
    nb = width // LANE
    qs, ks, col, _ = _att_specs(seq, tq, nq, off)

    def body(q_ref, k_ref, v_ref, o_ref, lt_ref):
        qi = pl.program_id(2)
        masks = _head_masks()
        qs_ = _stack_heads(q_ref[...] * SCALE, masks)
        row, cix = _stacked_iotas(tq, bk)
        tri = _tri(bk, "gt")

        def step(masked, top):
            def go(it, carry):
                acc, cl = carry
                kb = top - it
                ks_ = pl.multiple_of(kb * bk, bk)
                kv = k_ref[pl.ds(ks_, bk), :]
                vv = v_ref[pl.ds(ks_, bk), :]
                strict = (kb * bk + cix) < (qi * tq + row)
                z = lax.dot_general(qs_, kv, (NT, ((), ())), preferred_element_type=F32)
                sp = _softplus(z)
                lk = jnp.where(strict, -sp, 0.0) if masked else -sp
                wgt = jnp.exp(z - sp + (cl + _cumsum_mm(lk, tri)))
                if masked:
                    wgt = jnp.where(strict, wgt, 0.0)
                acc = acc + _unstack_heads(jnp.dot(wgt.astype(BF16), vv, preferred_element_type=F32), masks)
                return acc, cl + jnp.sum(lk, axis=1, keepdims=True)
            return go

        n_full, n_all = _key_blocks(qi, tq, bk)
        carry = (jnp.zeros((tq, LANE), F32), jnp.zeros((PAIR * tq, 1), F32))
        carry = lax.fori_loop(0, n_all - n_full, step(True, n_all - 1), carry)
        acc, cl = lax.fori_loop(0, n_full, step(False, n_full - 1), carry)
        o_ref[...] = acc.astype(o_ref.dtype)
        for h in range(PAIR):
            lt_ref[h] = cl[h * tq:(h + 1) * tq]

    return pl.pallas_call(
        body, name=name, grid=(bl, nb, nq), in_specs=[qs(off), ks(off + nb), ks(off + 2 * nb)],
        out_specs=[qs(0), col],
        out_shape=[jax.ShapeDtypeStruct((t_rows, width), BF16),
                   jax.ShapeDtypeStruct((bl, nb * PAIR, seq, 1), F32)],
        compiler_params=_params(),
    )(qkv, qkv, qkv)


def _sb_bwd(name, qkv, off, width, bl, seq, ltot, do, riders=()):
    t_rows = qkv.shape[0]
    tq, bk = _att_blocks(seq, SB_K_BLOCK)
    nq = seq // tq
    nb = width // LANE
    qs, ks, col, _ = _att_specs(seq, tq, nq, off)
    nr = len(riders)

    def body(*refs):
        q_ref, k_ref, v_ref, lt_ref, do_ref = refs[:5]
        ride_in = refs[5:5 + nr]
        dq_ref, dk_ref, dv_ref = refs[5 + nr:8 + nr]
        ride_out = refs[8 + nr:8 + 2 * nr]
        dk_acc, dv_acc = refs[8 + 2 * nr:10 + 2 * nr]
        qi = pl.program_id(2)
        if nr:
            send, recv = refs[10 + 2 * nr:]
            x, y, c = _mesh_pos()
            copies = []
            for j, (px, py) in enumerate(_other_chips(x, y)):
                for i in range(nr):
                    copies.append(pltpu.make_async_remote_copy(
                        src_ref=ride_in[i].at[2 * px + py], dst_ref=ride_out[i].at[j], send_sem=send.at[j * nr + i],
                        recv_sem=recv.at[j * nr + i], device_id=(px, py, c), device_id_type=MESH))
            first = (pl.program_id(0) == 0) & (pl.program_id(1) == 0) & (qi == 0)
            last = (pl.program_id(0) == bl - 1) & (pl.program_id(1) == nb - 1) & (qi == nq - 1)

            @pl.when(first)
            def _():
                for cp in copies:
                    cp.start()

        @pl.when(qi == 0)
        def _():
            dk_acc[...] = jnp.zeros_like(dk_acc)
            dv_acc[...] = jnp.zeros_like(dv_acc)

        masks = _head_masks()
        qs_ = _stack_heads(q_ref[...] * SCALE, masks)
        dos = _stack_heads(do_ref[...].astype(BF16), masks)
        lts = jnp.concatenate([lt_ref[h] for h in range(PAIR)], axis=0)
        row, cix = _stacked_iotas(tq, bk)
        tri_le = _tri(bk, "le")
        tri_lt = _tri(bk, "lt")

        def step(masked):
            def go(kb, carry):
                dq, cl, ce = carry
                ks_ = pl.multiple_of(kb * bk, bk)
                kv = k_ref[pl.ds(ks_, bk), :]
                vv = v_ref[pl.ds(ks_, bk), :]
                strict = (kb * bk + cix) < (qi * tq + row)
                z = lax.dot_general(qs_, kv, (NT, ((), ())), preferred_element_type=F32)
                sp = _softplus(z)
                lk = jnp.where(strict, -sp, 0.0) if masked else -sp
                sig = jnp.exp(z - sp)
                wgt = sig * jnp.exp(lts - cl - _cumsum_mm(lk, tri_le))
                if masked:
                    wgt = jnp.where(strict, wgt, 0.0)
                dw = lax.dot_general(dos, vv, (NT, ((), ())), preferred_element_type=F32)
                e = dw * wgt
                pre = ce + _cumsum_mm(e, tri_lt)
                dz = e * (1.0 - sig) - pre * sig
                if masked:
                    dz = jnp.where(strict, dz, 0.0)
                dzb = dz.astype(BF16)
                dq = dq + _unstack_heads(jnp.dot(dzb, kv * SCALE, preferred_element_type=F32), masks)
                dk_acc[pl.ds(ks_, bk), :] += lax.dot_general(dzb, qs_, (TN, ((), ())), preferred_element_type=F32)
                dv_acc[pl.ds(ks_, bk), :] += lax.dot_general(wgt.astype(BF16), dos, (TN, ((), ())),
                                                             preferred_element_type=F32)
                return dq, cl + jnp.sum(lk, axis=1, keepdims=True), ce + jnp.sum(e, axis=1, keepdims=True)
            return go

        n_full, n_all = _key_blocks(qi, tq, bk)
        zero = jnp.zeros((PAIR * tq, 1), F32)
        carry = lax.fori_loop(0, n_full, step(False), (jnp.zeros((tq, LANE), F32), zero, zero))
        dq, _, _ = lax.fori_loop(n_full, n_all, step(True), carry)
        dq_ref[...] = dq.astype(dq_ref.dtype)

        @pl.when(qi == nq - 1)
        def _():
            dk_ref[...] = dk_acc[...].astype(dk_ref.dtype)
            dv_ref[...] = dv_acc[...].astype(dv_ref.dtype)

        if nr:
            @pl.when(last)
            def _():
                for cp in copies:
                    cp.wait()

    o = jax.ShapeDtypeStruct((t_rows, width), BF16)
    slots = [jax.ShapeDtypeStruct((NUM_CHIPS - 1,) + r.shape[1:], r.dtype) for r in riders]
    sems = [pltpu.SemaphoreType.DMA((3 * nr,)), pltpu.SemaphoreType.DMA((3 * nr,))] if nr else []
    return pl.pallas_call(
        body, name=name, grid=(bl, nb, nq),
        in_specs=[qs(off), ks(off + nb), ks(off + 2 * nb), col, qs(0)] + [ANY] * nr,
        out_specs=[qs(0), ks(0), ks(0)] + [ANY] * nr, out_shape=[o, o, o] + slots,
        scratch_shapes=[pltpu.VMEM((seq, LANE), F32), pltpu.VMEM((seq, LANE), F32)] + sems,
        compiler_params=_params(),
    )(qkv, qkv, qkv, ltot, do, *riders)


def _fox_fwd(name, qkv, off, width, bl, seq, cum_q, cum_k):
    t_rows = qkv.shape[0]
    tq, bk = _att_blocks(seq, FOX_K_BLOCK)
    nq = seq // tq
    nb = width // LANE
    qs, ks, col, lane = _att_specs(seq, tq, nq, off)

    def body(q_ref, k_ref, v_ref, cq_ref, ck_ref, ob_ref, of_ref, lse_ref):
        qi = pl.program_id(2)
        masks = _head_masks()
        qs_ = _stack_heads(q_ref[...] * SCALE, masks)
        cqs = jnp.concatenate([cq_ref[h] for h in range(PAIR)], axis=0)
        row, cix = _stacked_iotas(tq, bk)

        def step(masked):
            def go(kb, carry):
                m, lsum, acc = carry
                ks_ = pl.multiple_of(kb * bk, bk)
                kv = k_ref[pl.ds(ks_, bk), :]
                vv = v_ref[pl.ds(ks_, bk), :]
                bias = jnp.concatenate([cqs[h * tq:(h + 1) * tq] - ck_ref[h, :, pl.ds(ks_, bk)] for h in range(PAIR)],
                                       axis=0)
                z = lax.dot_general(qs_, kv, (NT, ((), ())), preferred_element_type=F32) + bias
                if masked:
                    z = jnp.where((kb * bk + cix) <= (qi * tq + row), z, NEG)
                m_new = jnp.maximum(m, jnp.max(z, axis=1, keepdims=True))
                pv = jnp.exp(z - m_new)
                alpha = jnp.exp(m - m_new)
                lsum = alpha * lsum + jnp.sum(pv, axis=1, keepdims=True)
                acc = alpha * acc + jnp.dot(pv.astype(BF16), vv, preferred_element_type=F32)
                return m_new, lsum, acc
            return go

        n_full, n_all = _key_blocks(qi, tq, bk)
        init = (jnp.full((PAIR * tq, 1), NEG, F32), jnp.zeros((PAIR * tq, 1), F32),
                jnp.zeros((PAIR * tq, LANE), F32))
        carry = lax.fori_loop(0, n_full, step(False), init)
        m, lsum, acc = lax.fori_loop(n_full, n_all, step(True), carry)
        out = _unstack_heads(acc / lsum, masks)
        ob_ref[...] = out.astype(ob_ref.dtype)
        of_ref[...] = out
        lse = m + jnp.log(lsum)
        for h in range(PAIR):
            lse_ref[h] = lse[h * tq:(h + 1) * tq]

    return pl.pallas_call(
        body, name=name, grid=(bl, nb, nq),
        in_specs=[qs(off), ks(off + nb), ks(off + 2 * nb), col, lane], out_specs=[qs(0), qs(0), col],
        out_shape=[jax.ShapeDtypeStruct((t_rows, width), BF16), jax.ShapeDtypeStruct((t_rows, width), F32),
                   jax.ShapeDtypeStruct((bl, nb * PAIR, seq, 1), F32)],
        compiler_params=_params(),
    )(qkv, qkv, qkv, cum_q, cum_k)


def _fox_bwd(name, qkv, off, width, bl, seq, cum_q, cum_k, lse, o, do, riders=()):
    t_rows = qkv.shape[0]
    tq, bk = _att_blocks(seq, FOX_K_BLOCK)
    nq = seq // tq
    nb = width // LANE
    qs, ks, col, lane = _att_specs(seq, tq, nq, off)

    def body(q_ref, k_ref, v_ref, cq_ref, ck_ref, lse_ref, o_ref, do_ref,
             dq_ref, dk_ref, dv_ref, dcq_ref, dck_ref, dk_acc, dv_acc):
        qi = pl.program_id(2)

        @pl.when(qi == 0)
        def _():
            dk_acc[...] = jnp.zeros_like(dk_acc)
            dv_acc[...] = jnp.zeros_like(dv_acc)
            dck_ref[...] = jnp.zeros_like(dck_ref)

        masks = _head_masks()
        qs_ = _stack_heads(q_ref[...] * SCALE, masks)
        dof = do_ref[...]
        dos = _stack_heads(dof.astype(BF16), masks)
        prod = dof * o_ref[...]
        delta = jnp.concatenate([jnp.sum(_only(m, prod), axis=1, keepdims=True) for m in masks], axis=0)
        shift = jnp.concatenate([cq_ref[h] - lse_ref[h] for h in range(PAIR)], axis=0)
        row, cix = _stacked_iotas(tq, bk)

        def step(masked):
            def go(kb, carry):
                dq, dcq = carry
                ks_ = pl.multiple_of(kb * bk, bk)
                kv = k_ref[pl.ds(ks_, bk), :]
                vv = v_ref[pl.ds(ks_, bk), :]
                bias = jnp.concatenate(
                    [shift[h * tq:(h + 1) * tq] - ck_ref[h, :, pl.ds(ks_, bk)] for h in range(PAIR)], axis=0)
                pv = jnp.exp(lax.dot_general(qs_, kv, (NT, ((), ())), preferred_element_type=F32) + bias)
                if masked:
                    pv = jnp.where((kb * bk + cix) <= (qi * tq + row), pv, 0.0)
                dp = lax.dot_general(dos, vv, (NT, ((), ())), preferred_element_type=F32)
                ds = pv * (dp - delta)
                dsb = ds.astype(BF16)
                dq = dq + _unstack_heads(jnp.dot(dsb, kv * SCALE, preferred_element_type=F32), masks)
                dk_acc[pl.ds(ks_, bk), :] += lax.dot_general(dsb, qs_, (TN, ((), ())), preferred_element_type=F32)
                dv_acc[pl.ds(ks_, bk), :] += lax.dot_general(pv.astype(BF16), dos, (TN, ((), ())),
                                                             preferred_element_type=F32)
                for h in range(PAIR):
                    dck_ref[h, :, pl.ds(ks_, bk)] += -jnp.sum(ds[h * tq:(h + 1) * tq], axis=0, keepdims=True)
                return dq, dcq + jnp.sum(ds, axis=1, keepdims=True)
            return go

        n_full, n_all = _key_blocks(qi, tq, bk)
        carry = lax.fori_loop(0, n_full, step(False), (jnp.zeros((tq, LANE), F32), jnp.zeros((PAIR * tq, 1), F32)))
        dq, dcq = lax.fori_loop(n_full, n_all, step(True), carry)
        dq_ref[...] = dq.astype(dq_ref.dtype)
        for h in range(PAIR):
            dcq_ref[h] = dcq[h * tq:(h + 1) * tq]

        @pl.when(qi == nq - 1)
        def _():
            dk_ref[...] = dk_acc[...].astype(dk_ref.dtype)
            dv_ref[...] = dv_acc[...].astype(dv_ref.dtype)

    ob = jax.ShapeDtypeStruct((t_rows, width), BF16)
    nh = nb * PAIR
    return _ride_call(
        name, body, (bl, nb, nq),
        [qs(off), ks(off + nb), ks(off + 2 * nb), col, lane, col, qs(0), qs(0)],
        [qs(0), ks(0), ks(0), col, lane],
        [ob, ob, ob, jax.ShapeDtypeStruct((bl, nh, seq, 1), F32), jax.ShapeDtypeStruct((bl, nh, 1, seq), F32)],
        [pltpu.VMEM((seq, LANE), F32), pltpu.VMEM((seq, LANE), F32)],
        (qkv, qkv, qkv, cum_q, cum_k, lse, o, do), riders)


def _lane_cumsum(v, reverse):
    n = v.shape[1]
    cix = lax.broadcasted_iota(jnp.int32, v.shape, 1)
    s = 1
    while s < n:
        if reverse:
            v = v + jnp.where(cix < n - s, pltpu.roll(v, n - s, 1), 0.0)
        else:
            v = v + jnp.where(cix >= s, pltpu.roll(v, s, 1), 0.0)
        s *= 2
    return v


def _forget_cum(name, fl, bf):
    def body(fl_ref, bf_ref, o_ref):
        xv = fl_ref[...] + bf_ref[...]
        o_ref[...] = _lane_cumsum(-_softplus(-xv), False)

    return pl.pallas_call(body, name=name, out_shape=jax.ShapeDtypeStruct(fl.shape, F32),
                          compiler_params=_params())(fl, bf)


def _forget_cum_bwd(name, fl, bf, dcum, nh):
    rows = fl.shape[0]

    def body(fl_ref, bf_ref, dc_ref, dfl_ref, dbf_ref):
        xv = fl_ref[...] + bf_ref[...]
        dlogf = _lane_cumsum(dc_ref[...], True)
        dfl = dlogf * _sigmoid(-xv)
        dfl_ref[...] = dfl
        per_row = jnp.sum(dfl, axis=1, keepdims=True)
        tot = per_row[0:nh]
        for b in range(1, rows // nh):
            tot = tot + per_row[b * nh:(b + 1) * nh]
        dbf_ref[...] = tot

    return pl.pallas_call(
        body, name=name,
        out_shape=[jax.ShapeDtypeStruct(fl.shape, F32), jax.ShapeDtypeStruct((nh, 1), F32)],
        compiler_params=_params(),
    )(fl, bf, dcum)


def _merge_fwd(name, proj, off, merge_b, pa, pb, pc):
    t_rows, d = pa.shape
    tm = _tile(t_rows, 256)

    def body(l0, l1, l2, mb, a_ref, b_ref, c_ref, o_ref):
        g0 = _sigmoid(l0[...] + mb[:, 0:d])
        g1 = _sigmoid(l1[...] + mb[:, d:2 * d])
        g2 = _sigmoid(l2[...] + mb[:, 2 * d:3 * d])
        o_ref[...] = (g0 * a_ref[...] + g1 * b_ref[...] + g2 * c_ref[...]).astype(o_ref.dtype)

    row = pl.BlockSpec((tm, d), lambda i: (i, 0))
    lg = lambda j: pl.BlockSpec((tm, d), lambda i: (i, off + j))
    return pl.pallas_call(
        body, name=name, grid=(t_rows // tm,),
        in_specs=[lg(0), lg(1), lg(2), pl.BlockSpec((1, 3 * d), lambda i: (0, 0)), row, row, row],
        out_specs=row, out_shape=jax.ShapeDtypeStruct((t_rows, d), BF16), compiler_params=_params(),
    )(proj, proj, proj, merge_b, pa, pb, pc)


def _merge_bwd(name, proj, off, merge_b, pa, pb, pc, dmixed):
    t_rows, d = pa.shape
    tm = _tile(t_rows, 256)

    def body(l0, l1, l2, mb, a_ref, b_ref, c_ref, dm_ref, da_ref, db_ref, dc_ref, dl_ref, dmb_ref):
        i = pl.program_id(0)
        dm = dm_ref[...]
        parts = []
        for j, (lref, pref, dref) in enumerate(((l0, a_ref, da_ref), (l1, b_ref, db_ref), (l2, c_ref, dc_ref))):
            g = _sigmoid(lref[...] + mb[:, j * d:(j + 1) * d])
            dref[...] = (g * dm).astype(dref.dtype)
            dl = dm * pref[...] * g * (1.0 - g)
            dl_ref[:, j * d:(j + 1) * d] = dl.astype(dl_ref.dtype)
            parts.append(jnp.sum(dl, axis=0, keepdims=True))
        tot = jnp.concatenate(parts, axis=1)

        @pl.when(i == 0)
        def _():
            dmb_ref[...] = tot

        @pl.when(i > 0)
        def _():
            dmb_ref[...] += tot

    row = pl.BlockSpec((tm, d), lambda i: (i, 0))
    lg = lambda j: pl.BlockSpec((tm, d), lambda i: (i, off + j))
    one = pl.BlockSpec((1, 3 * d), lambda i: (0, 0))
    b16 = jax.ShapeDtypeStruct((t_rows, d), BF16)
    return pl.pallas_call(
        body, name=name, grid=(t_rows // tm,),
        in_specs=[lg(0), lg(1), lg(2), one, row, row, row, row],
        out_specs=[row, row, row, pl.BlockSpec((tm, 3 * d), lambda i: (i, 0)), one],
        out_shape=[b16, b16, b16, jax.ShapeDtypeStruct((t_rows, 3 * d), BF16), jax.ShapeDtypeStruct((1, 3 * d), F32)],
        compiler_params=_params(),
    )(proj, proj, proj, merge_b, pa, pb, pc, dmixed)


def _grouped_nn(name, a, wg, l, out_dtype):
    t_rows, kk = a.shape
    ng, ncol = wg.shape[0], wg.shape[3]
    tm = _tile(t_rows, 512)
    return _mmk(name, [a, wg],
                [pl.BlockSpec((tm, kk), lambda i, g: (i, 0)),
                 pl.BlockSpec((None, None, kk, ncol), lambda i, g: (g, l, 0, 0))],
                [(0, ALL, 1, ALL, NN)], (t_rows, ng * ncol), out_dtype, (t_rows // tm, ng),
                pl.BlockSpec((tm, ncol), lambda i, g: (i, g)))


def _grouped_nt(name, da, wg, l, out_dtype):
    t_rows = da.shape[0]
    ng, kk, ncol = wg.shape[0], wg.shape[2], wg.shape[3]
    tm = _tile(t_rows, 512)
    return _mmk(name, [da, wg],
                [pl.BlockSpec((tm, ng * ncol), lambda i: (i, 0)),
                 pl.BlockSpec((ng, None, kk, ncol), lambda i: (0, l, 0, 0))],
                [(0, (ALL, slice(g * ncol, (g + 1) * ncol)), 1, g, NT) for g in range(ng)],
                (t_rows, kk), out_dtype, (t_rows // tm,), pl.BlockSpec((tm, kk), lambda i: (i, 0)))


def _grouped_tn(name, a, da, ng, out_dtype):
    t_rows, kk = a.shape
    ncol = da.shape[1] // ng
    return _mmk(name, [a, da],
                [pl.BlockSpec((t_rows, kk), lambda g: (0, 0)), pl.BlockSpec((t_rows, ncol), lambda g: (0, g))],
                [(0, ALL, 1, ALL, TN)], (ng, kk, ncol), out_dtype, (ng,),
                pl.BlockSpec((None, kk, ncol), lambda g: (g, 0, 0)))


def _rows_nn(name, a, wr, l, out_dtype):
    t_rows = a.shape[0]
    ng, kg, n = wr.shape[0], wr.shape[2], wr.shape[3]
    tm, tn = _tile(t_rows, 512), _tile(n, 512)
    return _mmk(name, [a, wr],
                [pl.BlockSpec((tm, ng * kg), lambda i, j: (i, 0)),
                 pl.BlockSpec((ng, None, kg, tn), lambda i, j: (0, l, 0, j))],
                [(0, (ALL, slice(g * kg, (g + 1) * kg)), 1, g, NN) for g in range(ng)],
                (t_rows, n), out_dtype, (t_rows // tm, n // tn), pl.BlockSpec((tm, tn), lambda i, j: (i, j)))


def _rows_nt(name, dy, wr, l, out_dtype):
    t_rows, n = dy.shape
    ng, kg = wr.shape[0], wr.shape[2]
    tm = _tile(t_rows, 512)
    return _mmk(name, [dy, wr],
                [pl.BlockSpec((tm, n), lambda i, g: (i, 0)),
                 pl.BlockSpec((None, None, kg, n), lambda i, g: (g, l, 0, 0))],
                [(0, ALL, 1, ALL, NT)], (t_rows, ng * kg), out_dtype, (t_rows // tm, ng),
                pl.BlockSpec((tm, kg), lambda i, g: (i, g)))


def _rows_tn(name, a, dy, ng, out_dtype):
    t_rows, n = dy.shape
    kg = a.shape[1] // ng
    tn = _tile(n, 512)
    return _mmk(name, [a, dy],
                [pl.BlockSpec((t_rows, kg), lambda g, j: (0, g)), pl.BlockSpec((t_rows, tn), lambda g, j: (0, j))],
                [(0, ALL, 1, ALL, TN)], (ng, kg, n), out_dtype, (ng, n // tn),
                pl.BlockSpec((None, kg, tn), lambda g, j: (g, 0, j)))


def _mix_fwd(tag, w, l, x, mod, seq):
    t_rows, d = x.shape
    bl = t_rows // seq
    shift, scale, gate = mod
    chans, nh = w["layout"]["chans"], w["layout"]["heads"]
    width = nh * HEAD_DIM
    nb = width // LANE
    h = _normmod(tag + "_norm", x, w["mix_norm"][l][None], shift, scale, seq)
    proj = _mm(tag + "_in_a", h, w["w_a"][l], NN, F32)
    qkv = _mm(tag + "_in_b", h, w["w_b"][l], NN, BF16)
    flp = _mm(tag + "_in_f", h, w["w_f"][l], NN, F32)
    xa, r, ig, hs, ya = _rg_fwd(tag + "_rg", proj, w["rg"][l], seq, chans)
    yb, ltot = _sb_fwd(tag + "_sb", qkv, 0, width, bl, seq)
    fl = flp[:, :nh].reshape(bl, seq, nh).transpose(0, 2, 1).reshape(bl * nh, seq)
    bf = jnp.tile(w["fox_bf"][l].reshape(nh, 1), (bl, 1))
    cum = _forget_cum(tag + "_cum", fl, bf)
    cum_q = cum.reshape(bl, nh, seq, 1)
    cum_k = cum.reshape(bl, nh, 1, seq)
    yc, oc, lse = _fox_fwd(tag + "_fox", qkv, 3 * nb, width, bl, seq, cum_q, cum_k)
    pa = _rows_nn(tag + "_prg", ya, w["w_rg"], l, F32)
    pb = _grouped_nn(tag + "_psb", yb, w["w_sb"], l, F32)
    pc = _grouped_nn(tag + "_pfox", yc, w["w_fox"], l, F32)
    moff = 2 * chans // d
    mb = w["merge_b"][l][None]
    mixed = _merge_fwd(tag + "_merge", proj, moff, mb, pa, pb, pc)
    y = _rows_nn(tag + "_out", mixed, w["w_o"], l, F32)
    xn = _resid(tag + "_res", x, y, gate, 1.0, seq)
    saved = dict(x=x, h=h, proj=proj, qkv=qkv, rg=(xa, r, ig, hs), ya=ya, ltot=ltot,
                 fox=(cum_q, cum_k, lse, oc), fl=fl, bf=bf, yb=yb, yc=yc, pa=pa, pb=pb, pc=pc, mixed=mixed, y=y)
    return xn, saved


def _mix_bwd(tag, w, l, s, mod, dxo, seq, riders=(), riders_fox=()):
    x = s["x"]
    t_rows, d = x.shape
    bl = t_rows // seq
    shift, scale, gate = mod
    chans, nh = w["layout"]["chans"], w["layout"]["heads"]
    width = nh * HEAD_DIM
    nb = width // LANE
    moff = 2 * chans // d
    mb = w["merge_b"][l][None]
    ng = NUM_CHIPS
    dy, dgate = _resid_bwd(tag + "_res_bwd", dxo, s["y"], gate, 1.0, seq)
    dmixed = _rows_nt(tag + "_out_dx", dy, w["w_o"], l, F32)
    dw_o = _rows_tn(tag + "_out_dw", s["mixed"], dy, ng, BF16)
    dpa, dpb, dpc, dlog, dmb = _merge_bwd(tag + "_merge_bwd", s["proj"], moff, mb, s["pa"], s["pb"], s["pc"], dmixed)
    dya = _rows_nt(tag + "_prg_dx", dpa, w["w_rg"], l, F32)
    dw_rg = _rows_tn(tag + "_prg_dw", s["ya"], dpa, ng, BF16)
    dyb = _grouped_nt(tag + "_psb_dx", dpb, w["w_sb"], l, F32)
    dw_sb = _grouped_tn(tag + "_psb_dw", s["yb"], dpb, ng, BF16)
    dyc = _grouped_nt(tag + "_pfox_dx", dpc, w["w_fox"], l, F32)
    dw_fox = _grouped_tn(tag + "_pfox_dw", s["yc"], dpc, ng, BF16)
    qkv = s["qkv"]
    dq_b, dk_b, dv_b, *slots = _sb_bwd(tag + "_sb_bwd", qkv, 0, width, bl, seq, s["ltot"], dyb, riders)
    cum_q, cum_k, lse, oc = s["fox"]
    dq_c, dk_c, dv_c, dcq, dck, *slots_fox = _fox_bwd(tag + "_fox_bwd", qkv, 3 * nb, width, bl, seq, cum_q, cum_k,
                                                      lse, oc, dyc, riders_fox)
    dcum = dcq.reshape(bl * nh, seq) + dck.reshape(bl * nh, seq)
    dfl, dbf = _forget_cum_bwd(tag + "_cum_bwd", s["fl"], s["bf"], dcum, nh)
    dfl_t = dfl.reshape(bl, nh, seq).transpose(0, 2, 1).reshape(t_rows, nh)
    dflp = jnp.pad(dfl_t, ((0, 0), (0, LANE - nh))).astype(BF16)
    drgx, dgt, rg_small, dwa, dwx = _rg_bwd(tag + "_rg_bwd", s["proj"], dya, s["rg"], w["rg"][l], seq, chans)
    dproj = jnp.concatenate([drgx, dgt, dlog], axis=1)
    dqkv = jnp.concatenate([dq_b, dk_b, dv_b, dq_c, dk_c, dv_c], axis=1)
    w_a, w_b, w_f = w["w_a"][l], w["w_b"][l], w["w_f"][l]
    pa_w, pb_w = w_a.shape[1], w_b.shape[1]
    tm, tn = _tile(t_rows, 512), _tile(d, 512)
    rows = lambda n: pl.BlockSpec((tm, n), lambda i, j: (i, 0))
    wrow = lambda n: pl.BlockSpec((tn, n), lambda i, j: (j, 0))
    dh = _mmk(tag + "_in_dx", [dproj, dqkv, dflp, w_a, w_b, w_f],
              [rows(pa_w), rows(pb_w), rows(LANE), wrow(pa_w), wrow(pb_w), wrow(LANE)],
              [(0, ALL, 3, ALL, NT), (1, ALL, 4, ALL, NT), (2, ALL, 5, ALL, NT)],
              (t_rows, d), F32, (t_rows // tm, d // tn), pl.BlockSpec((tm, tn), lambda i, j: (i, j)))
    hb = s["h"]
    dw_a = _mm(tag + "_in_a_dw", hb, dproj, TN, BF16)
    dw_b = _mm(tag + "_in_b_dw", hb, dqkv, TN, BF16)
    dw_f = _mm(tag + "_in_f_dw", hb, dflp, TN, BF16)
    dx, dshift, dscale, dgain = _normmod_bwd(tag + "_norm_bwd", x, dh, dxo, w["mix_norm"][l][None], scale, seq)
    grads = dict(w_in=(dw_a, dw_b, dw_f), w_rg=dw_rg, w_sb=dw_sb, w_fox=dw_fox, w_o=dw_o, mix_norm=dgain,
                 rg_small=rg_small, rg_dwa=dwa, rg_dwx=dwx, fox_bf=dbf, merge_b=dmb)
    return dx, (dshift, dscale, dgate), grads, slots, slots_fox


def _silu(name, c):
    def body(c_ref, o_ref):
        v = c_ref[...]
        o_ref[...] = v * _sigmoid(v)

    return pl.pallas_call(body, name=name, out_shape=jax.ShapeDtypeStruct(c.shape, F32),
                          compiler_params=_params())(c)


def _blockdiag(wb):
    nb, bd, _ = wb.shape
    per = LANE // bd
    t = wb.reshape(nb // per, per, bd, 1, bd)
    eye = jnp.eye(per, dtype=wb.dtype).reshape(1, per, 1, per, 1)
    return (t * eye).reshape(nb // per, LANE, LANE).astype(BF16)


def _unblockdiag(t, bd):
    n = t.shape[0]
    per = LANE // bd
    t5 = t.reshape(n, per, bd, per, bd)
    return jnp.stack([t5[:, p, :, p, :] for p in range(per)], axis=1).reshape(n * per, bd, bd)


def _prepare(gw, a, d, chans, nh):
    depth = a["ada_b"].shape[0]
    wq = 3 * nh * HEAD_DIM
    o_m = 2 * chans + 2 * wq
    w = {"layout": dict(chans=chans, heads=nh)}
    for n in ("ffn1_w1", "ffn1_w3", "ffn1_w2", "ffn2_w1", "ffn2_w3", "ffn2_w2", "w_rg", "w_sb", "w_fox", "w_o"):
        w[n] = gw[n]
    for n in ("ffn1_norm", "ffn2_norm", "mix_norm", "fox_bf", "merge_b", "final_norm"):
        w[n] = a[n]
    w_a, w_b, w_f, rg = [], [], [], []
    for l in range(depth):
        full = gw["w_in"][:, l].transpose(1, 0, 2).reshape(d, -1)
        w_a.append(jnp.concatenate([full[:, :2 * chans], full[:, o_m + nh:]], axis=1))
        w_b.append(full[:, 2 * chans:o_m])
        w_f.append(jnp.pad(full[:, o_m:o_m + nh], ((0, 0), (0, LANE - nh))))
        conv_w = gw["conv_w"][:, l].transpose(1, 0, 2).reshape(-1, chans)
        rg.append(dict(conv_w=conv_w, conv_b=a["conv_b"][l][None], ba=a["rg_ba"][l][None], bx=a["rg_bx"][l][None],
                       lam=a["rg_lam"][l][None], wa=_blockdiag(a["rg_wa"][l]), wx=_blockdiag(a["rg_wx"][l])))
    w["w_a"], w["w_b"], w["w_f"], w["rg"] = w_a, w_b, w_f, rg
    return w


def _local_step(w, x, tgt, mods, fm, ride=None):
    bl, seq, d = x.shape
    t_rows = bl * seq
    depth = len(mods)
    mod3 = []
    for l in range(depth):
        m4 = mods[l].reshape(bl, 9, 1, d)
        mod3.append([(m4[:, 3 * k], m4[:, 3 * k + 1], m4[:, 3 * k + 2]) for k in range(3)])
    fm4 = fm.reshape(bl, 2, 1, d)
    saved = []
    xc = x.reshape(t_rows, d)
    for l in range(depth):
        xc, s1 = _ffn_fwd(f"l{l}_ffn1", w, l, "ffn1_", xc, mod3[l][0], seq)
        xc, s2 = _mix_fwd(f"l{l}_mix", w, l, xc, mod3[l][1], seq)
        xc, s3 = _ffn_fwd(f"l{l}_ffn2", w, l, "ffn2_", xc, mod3[l][2], seq)
        saved.append((s1, s2, s3))
    lpart, dx, dfs, dfc, dfg = _final_loss("final", xc, tgt.reshape(t_rows, d), w["final_norm"][None],
                                           fm4[:, 0], fm4[:, 1], seq)
    loss = jnp.sum(lpart[:, 0, 0])
    grads = {"final_norm": dfg, "layers": [None] * depth}
    dmods = [None] * depth
    parts, slots = {}, {}
    queue = {"sb": [], "fox": [], "dn": [], "up": []}

    def take(host):
        keys, queue[host] = queue[host], []
        return keys, [parts[k] for k in keys]

    def enqueue(l, names, gl, tag, hosts):
        for n, p in ride(names, gl, tag).items():
            parts[(l, n)] = p
            queue[hosts(n)].append((l, n))

    for l in reversed(range(depth)):
        s1, s2, s3 = saved[l]
        dx, dm3, g3, _, _ = _ffn_bwd(f"l{l}_ffn2", w, l, "ffn2_", s3, mod3[l][2], dx, seq)
        if ride is not None and l == 0:
            enqueue(l, FFN2, g3, "l0_ffn2", lambda n: "sb")
        (k_sb, r_sb), (k_fox, r_fox) = take("sb"), take("fox")
        dx, dm2, g2, s_sb, s_fox = _mix_bwd(f"l{l}_mix", w, l, s2, mod3[l][1], dx, seq, r_sb, r_fox)
        slots.update(zip(k_sb + k_fox, list(s_sb) + list(s_fox)))
        if ride is not None and l == 0:
            enqueue(l, MIXER, g2, "l0_mix", lambda n: "up" if n == "w_in" else "dn")
        (k_dn, r_dn), (k_up, r_up) = take("dn"), take("up")
        dx, dm1, g1, s_dn, s_up = _ffn_bwd(f"l{l}_ffn1", w, l, "ffn1_", s1, mod3[l][0], dx, seq, r_dn, r_up)
        slots.update(zip(k_dn + k_up, list(s_dn) + list(s_up)))
        dmods[l] = jnp.concatenate([*dm1, *dm2, *dm3], axis=1).reshape(bl, 9 * d)
        grads["layers"][l] = {**g1, **g2, **g3}
        if ride is not None and l > 0:
            enqueue(l, DENSE, grads["layers"][l], f"l{l}", lambda n: "fox" if n == "w_in" else "sb")
    dfm = jnp.concatenate([dfs, dfc], axis=1).reshape(bl, 2 * d)
    return loss, dx.reshape(bl, seq, d), grads, dmods, dfm, (parts, slots)


def _mesh_pos():
    return lax.axis_index("x"), lax.axis_index("y"), lax.axis_index("c")


def _other_chips(x, y):
    return ((1 - x, y), (x, 1 - y), (1 - x, 1 - y))


def _gather_two_level(name, arrs):
    n = len(arrs)

    def body(*refs):
        ins, outs = refs[:n], refs[n:2 * n]
        send, recv, send2, recv2, send3, recv3 = refs[2 * n:]
        x, y, c = _mesh_pos()
        me = 2 * x + y
        chips = _other_chips(x, y)
        sib = (x, y, 1 - c)
        own = [pltpu.make_async_remote_copy(
            src_ref=ins[i], dst_ref=outs[i].at[me], send_sem=send3.at[i], recv_sem=recv3.at[i],
            device_id=sib, device_id_type=MESH) for i in range(n)]
        first = []
        for j, (px, py) in enumerate(chips):
            for i in range(n):
                first.append(pltpu.make_async_remote_copy(
                    src_ref=ins[i].at[c], dst_ref=outs[i].at[me, c], send_sem=send.at[j * n + i],
                    recv_sem=recv.at[j * n + i], device_id=(px, py, c), device_id_type=MESH))
        for cp in first + own:
            cp.start()
        passed = []
        for j, (px, py) in enumerate(chips):
            for i in range(n):
                landed = outs[i].at[2 * px + py, c]
                pltpu.make_async_remote_copy(
                    src_ref=ins[i].at[c], dst_ref=landed, send_sem=send.at[j * n + i],
                    recv_sem=recv.at[j * n + i], device_id=(px, py, c), device_id_type=MESH).wait_recv()
                fwd = pltpu.make_async_remote_copy(
                    src_ref=landed, dst_ref=landed, send_sem=send2.at[j * n + i],
                    recv_sem=recv2.at[j * n + i], device_id=sib, device_id_type=MESH)
                fwd.start()
                passed.append(fwd)
        for j, (px, py) in enumerate(chips):
            for i in range(n):
                theirs = outs[i].at[2 * px + py, 1 - c]
                pltpu.make_async_remote_copy(
                    src_ref=theirs, dst_ref=theirs, send_sem=send2.at[j * n + i],
                    recv_sem=recv2.at[j * n + i], device_id=sib, device_id_type=MESH).wait_recv()
        for cp in first + passed:
            cp.wait_send()
        for cp in own:
            cp.wait()

    return pl.pallas_call(
        body, name=name, in_specs=[ANY] * n, out_specs=[ANY] * n,
        out_shape=[jax.ShapeDtypeStruct((NUM_CHIPS,) + a.shape, a.dtype) for a in arrs],
        scratch_shapes=[pltpu.SemaphoreType.DMA((3 * n,)), pltpu.SemaphoreType.DMA((3 * n,)),
                        pltpu.SemaphoreType.DMA((3 * n,)), pltpu.SemaphoreType.DMA((3 * n,)),
                        pltpu.SemaphoreType.DMA((n,)), pltpu.SemaphoreType.DMA((n,))],
    )(*arrs)


def _split_to_sibling(name, arrs):
    n = len(arrs)
    slabs = arrs[0].shape[0]

    def body(*refs):
        ins, theirs = refs[:n], refs[n:2 * n]
        send, recv = refs[2 * n:]
        x, y, c = _mesh_pos()
        sib = (x, y, 1 - c)
        for i in range(n):
            for s in range(slabs):
                pltpu.make_async_remote_copy(
                    src_ref=ins[i].at[s, 1 - c], dst_ref=theirs[i].at[s], send_sem=send.at[i],
                    recv_sem=recv.at[i], device_id=sib, device_id_type=MESH).start()
        for i in range(n):
            pltpu.make_async_remote_copy(
                src_ref=ins[i].at[:, 0], dst_ref=theirs[i], send_sem=send.at[i], recv_sem=recv.at[i],
                device_id=sib, device_id_type=MESH).wait()

    return pl.pallas_call(
        body, name=name, in_specs=[ANY] * n, out_specs=[ANY] * n,
        out_shape=[jax.ShapeDtypeStruct((a.shape[0],) + a.shape[2:], a.dtype) for a in arrs],
        scratch_shapes=[pltpu.SemaphoreType.DMA((n,)), pltpu.SemaphoreType.DMA((n,))],
    )(*arrs)


def _scatter_chips(name, arrs):
    n = len(arrs)

    def body(*refs):
        ins, outs = refs[:n], refs[n:2 * n]
        send, recv = refs[2 * n:]
        x, y, c = _mesh_pos()
        chips = _other_chips(x, y)
        sends = []
        for j, (px, py) in enumerate(chips):
            for i in range(n):
                sends.append(pltpu.make_async_remote_copy(
                    src_ref=ins[i].at[2 * px + py], dst_ref=outs[i].at[j], send_sem=send.at[j * n + i],
                    recv_sem=recv.at[j * n + i], device_id=(px, py, c), device_id_type=MESH))
        for s in sends:
            s.start()
        for s in sends:
            s.wait()

    return pl.pallas_call(
        body, name=name, in_specs=[ANY] * n, out_specs=[ANY] * n,
        out_shape=[jax.ShapeDtypeStruct((NUM_CHIPS - 1,) + a.shape[1:], a.dtype) for a in arrs],
        scratch_shapes=[pltpu.SemaphoreType.DMA((3 * n,)), pltpu.SemaphoreType.DMA((3 * n,))],
    )(*arrs)


def _join_halves(name, arrs):
    n = len(arrs)

    def body(*refs):
        ins, outs = refs[:n], refs[n:2 * n]
        send, recv = refs[2 * n:]
        x, y, c = _mesh_pos()
        copies = [pltpu.make_async_remote_copy(
            src_ref=ins[i], dst_ref=outs[i], send_sem=send.at[i], recv_sem=recv.at[i],
            device_id=(x, y, 1 - c), device_id_type=MESH) for i in range(n)]
        for cp in copies:
            cp.start()
        for cp in copies:
            cp.wait()

    return pl.pallas_call(
        body, name=name, in_specs=[ANY] * n, out_specs=[ANY] * n,
        out_shape=[jax.ShapeDtypeStruct(a.shape, a.dtype) for a in arrs],
        scratch_shapes=[pltpu.SemaphoreType.DMA((n,)), pltpu.SemaphoreType.DMA((n,))],
    )(*arrs)


def _gather_all(name, pack):
    def body(in_ref, out_ref, send, recv, loc):
        x, y, c = _mesh_pos()
        me = 4 * x + 2 * y + c
        mine = pltpu.make_async_copy(in_ref, out_ref.at[me], loc)
        mine.start()
        peers = []
        for mask in range(1, NUM_DEVICES):
            px = 1 - x if mask & 4 else x
            py = 1 - y if mask & 2 else y
            pc = 1 - c if mask & 1 else c
            peers.append((px, py, pc))
        sends = [pltpu.make_async_remote_copy(
            src_ref=in_ref, dst_ref=out_ref.at[me], send_sem=send.at[k], recv_sem=recv.at[k],
            device_id=p, device_id_type=MESH) for k, p in enumerate(peers)]
        for s in sends:
            s.start()
        for k, (px, py, pc) in enumerate(peers):
            pltpu.make_async_remote_copy(
                src_ref=in_ref, dst_ref=out_ref.at[4 * px + 2 * py + pc], send_sem=send.at[k], recv_sem=recv.at[k],
                device_id=(px, py, pc), device_id_type=MESH).wait_recv()
        for s in sends:
            s.wait_send()
        mine.wait()

    return pl.pallas_call(
        body, name=name, in_specs=[ANY], out_specs=ANY,
        out_shape=jax.ShapeDtypeStruct((NUM_DEVICES,) + pack.shape, pack.dtype),
        scratch_shapes=[pltpu.SemaphoreType.DMA((NUM_DEVICES - 1,)), pltpu.SemaphoreType.DMA((NUM_DEVICES - 1,)),
                        pltpu.SemaphoreType.DMA],
    )(pack)


def _sum_slots(name, slots, out_dtype):
    g, rows, cols = slots.shape
    tr = _rtile(rows, 256)

    def body(s_ref, o_ref):
        acc = s_ref[0].astype(F32)
        for k in range(1, g):
            acc = acc + s_ref[k].astype(F32)
        o_ref[...] = acc.astype(o_ref.dtype)

    return pl.pallas_call(
        body, name=name, grid=(rows // tr,),
        in_specs=[pl.BlockSpec((g, tr, cols), lambda i: (0, i, 0))],
        out_specs=pl.BlockSpec((tr, cols), lambda i: (i, 0)),
        out_shape=jax.ShapeDtypeStruct((rows, cols), out_dtype), compiler_params=_params(),
    )(slots)


def _add_pair(name, p, q, core):
    g, _, rows, cols = p.shape
    tr = _rtile(rows, 128)

    def body(c_ref, p_ref, q_ref, o_ref):
        mine = jnp.where(c_ref[0] == 0, p_ref[:, 0].astype(F32), p_ref[:, 1].astype(F32))
        o_ref[...] = (mine + q_ref[...].astype(F32)).astype(o_ref.dtype)

    spec = pl.BlockSpec((g, tr, cols), lambda i: (0, i, 0))
    return pl.pallas_call(
        body, name=name, grid=(rows // tr,),
        in_specs=[SCALAR, pl.BlockSpec((g, 2, tr, cols), lambda i: (0, 0, i, 0)), spec],
        out_specs=spec, out_shape=jax.ShapeDtypeStruct(q.shape, BF16), compiler_params=_params(),
    )(core, p, q)


def _sum_chips(name, slots, part, chip):
    g, rows, cols = part.shape
    tr = _rtile(rows, 128)

    def body(c_ref, s_ref, p_ref, o_ref):
        acc = p_ref[c_ref[0]].astype(F32)
        for k in range(slots.shape[0]):
            acc = acc + s_ref[k].astype(F32)
        o_ref[...] = acc

    return pl.pallas_call(
        body, name=name, grid=(rows // tr,),
        in_specs=[SCALAR,
                  pl.BlockSpec((slots.shape[0], tr, cols), lambda i: (0, i, 0)),
                  pl.BlockSpec((g, tr, cols), lambda i: (0, i, 0))],
        out_specs=pl.BlockSpec((tr, cols), lambda i: (i, 0)),
        out_shape=jax.ShapeDtypeStruct((rows, cols), F32), compiler_params=_params(),
    )(chip, slots, part)


def _adamw(name, g, w, m, v, l=None):
    rows, cols = g.shape
    tr = _rtile(rows, 128)

    def body(g_ref, w_ref, m_ref, v_ref, d_o, m_o, v_o):
        gv = g_ref[...]
        mn = ADAM_B1 * m_ref[...] + (1.0 - ADAM_B1) * gv
        vn = ADAM_B2 * v_ref[...] + (1.0 - ADAM_B2) * (gv * gv)
        m_hat = mn / (1.0 - ADAM_B1 ** ADAM_STEP)
        v_hat = vn / (1.0 - ADAM_B2 ** ADAM_STEP)
        d_o[...] = -ADAM_LR * (m_hat / (jnp.sqrt(v_hat) + ADAM_EPS) + ADAM_WD * w_ref[...])
        m_o[...] = mn
        v_o[...] = vn

    gspec = pl.BlockSpec((tr, cols), lambda i: (i, 0))
    wspec = gspec if l is None else pl.BlockSpec((None, tr, cols), lambda i: (l, i, 0))
    f = jax.ShapeDtypeStruct((rows, cols), F32)
    return pl.pallas_call(
        body, name=name, grid=(rows // tr,), in_specs=[gspec] + [wspec] * 3, out_specs=[gspec] * 3,
        out_shape=[f] * 3, compiler_params=_params(),
    )(g, w, m, v)


def _adamw_layers(name, g0, g1, w, m, v):
    rows, cols = g0.shape
    tr = _rtile(rows, 128)
    nt = rows // tr

    def body(g0_ref, g1_ref, w_ref, m_ref, v_ref, g_o, d_o, m_o, v_o):
        gv = jnp.where(pl.program_id(0) == 0, g0_ref[...], g1_ref[...])
        _adamw_math(gv, w_ref, m_ref, v_ref, g_o, d_o, m_o, v_o)

    g0spec = pl.BlockSpec((tr, cols), lambda l, i: (i * (1 - l) + (nt - 1) * l, 0))
    g1spec = pl.BlockSpec((tr, cols), lambda l, i: (i * l, 0))
    wspec = pl.BlockSpec((None, tr, cols), lambda l, i: (l, i, 0))
    f = jax.ShapeDtypeStruct((2, rows, cols), F32)
    return pl.pallas_call(
        body, name=name, grid=(2, nt), in_specs=[g0spec, g1spec, wspec, wspec, wspec], out_specs=[wspec] * 4,
        out_shape=[f] * 4, compiler_params=_params(),
    )(g0, g1, w, m, v)


def _adamw_math(gv, w_ref, m_ref, v_ref, g_o, d_o, m_o, v_o):
    mn = ADAM_B1 * m_ref[...] + (1.0 - ADAM_B1) * gv
    vn = ADAM_B2 * v_ref[...] + (1.0 - ADAM_B2) * (gv * gv)
    m_hat = mn / (1.0 - ADAM_B1 ** ADAM_STEP)
    v_hat = vn / (1.0 - ADAM_B2 ** ADAM_STEP)
    g_o[...] = gv
    d_o[...] = -ADAM_LR * (m_hat / (jnp.sqrt(v_hat) + ADAM_EPS) + ADAM_WD * w_ref[...])
    m_o[...] = mn
    v_o[...] = vn


def _adamw_halves(name, mine, theirs, core, w, m, v):
    half, cols = mine[0].shape
    tr = _rtile(half, 128)
    nt = half // tr

    def body(c_ref, a0, b0, a1, b1, w_ref, m_ref, v_ref, g_o, d_o, m_o, v_o):
        first = pl.program_id(0) == 0
        own = pl.program_id(1) == c_ref[0]
        gv = jnp.where(first, jnp.where(own, a0[...], b0[...]), jnp.where(own, a1[...], b1[...]))
        _adamw_math(gv, w_ref, m_ref, v_ref, g_o, d_o, m_o, v_o)

    lay0 = pl.BlockSpec((tr, cols), lambda l, h, i: (i * (1 - l) + (nt - 1) * l, 0))
    lay1 = pl.BlockSpec((tr, cols), lambda l, h, i: (i * l, 0))
    wspec = pl.BlockSpec((None, tr, cols), lambda l, h, i: (l, h * nt + i, 0))
    f = jax.ShapeDtypeStruct((2, 2 * half, cols), F32)
    return pl.pallas_call(
        body, name=name, grid=(2, 2, nt), in_specs=[SCALAR, lay0, lay0, lay1, lay1, wspec, wspec, wspec],
        out_specs=[wspec] * 4, out_shape=[f] * 4, compiler_params=_params(),
    )(core, mine[0], theirs[0], mine[1], theirs[1], w, m, v)


def _colsum(name, a):
    def body(a_ref, o_ref):
        o_ref[...] = jnp.sum(a_ref[...], axis=0, keepdims=True)

    return pl.pallas_call(body, name=name, out_shape=jax.ShapeDtypeStruct((1, a.shape[1]), F32),
                          compiler_params=_params())(a)


PACK_UNIT = SUBLANE * LANE


def _pack(items):
    flat, layout, o = [], [], 0
    for it in items:
        n = it.size
        pad = -n % PACK_UNIT
        flat.append(jnp.pad(it.reshape(-1).astype(F32), (0, pad)))
        layout.append((o, n, it.shape))
        o += n + pad
    return jnp.concatenate(flat).reshape(-1, LANE), layout


def _unpack(pack, layout):
    flat = pack.reshape(-1)
    return [flat[o:o + n].reshape(shape) for o, n, shape in layout]


WEIGHTS = ("ffn1_norm", "ffn1_w1", "ffn1_w3", "ffn1_w2", "mix_norm", "w_in", "conv_w", "conv_b", "rg_wa", "rg_ba",
           "rg_wx", "rg_bx", "rg_lam", "fox_bf", "merge_b", "w_rg", "w_sb", "w_fox", "w_o", "ffn2_norm", "ffn2_w1",
           "ffn2_w3", "ffn2_w2", "ada_w", "ada_b", "final_norm", "final_ada_w", "final_ada_b")
DENSE = ("ffn1_w1", "ffn1_w3", "ffn1_w2", "w_in", "w_rg", "w_sb", "w_fox", "w_o", "ffn2_w1", "ffn2_w3", "ffn2_w2")
FFN1 = ("ffn1_w1", "ffn1_w3", "ffn1_w2")
FFN2 = ("ffn2_w1", "ffn2_w3", "ffn2_w2")
MIXER = ("w_in", "w_rg", "w_sb", "w_fox", "w_o")
SMALL = ("ffn1_norm", "mix_norm", "ffn2_norm", "rg_small", "rg_wa", "rg_wx", "fox_bf", "merge_b")


def _step(a):
    x, c, tgt = a["x"], a["c"], a["loss_target"]
    bl, seq, d = x.shape
    depth, nh = a["fox_bf"].shape
    chans = a["rg_lam"].shape[1]
    bd = a["rg_wa"].shape[2]
    wq = 3 * nh * HEAD_DIM
    o_m = 2 * chans + 2 * wq
    batch = NUM_DEVICES * bl
    mx, my, mc = _mesh_pos()
    me = 2 * mx + my
    dev = 4 * mx + 2 * my + mc

    c_rows = -(-bl * d // LANE // SUBLANE) * SUBLANE
    c_pack = jnp.pad(c.reshape(-1, LANE), ((0, c_rows - bl * d // LANE), (0, 0)))
    c_all = _gather_all("gather_c", c_pack)[:, :bl * d // LANE].reshape(batch, d)
    c_act = _silu("c_act", c_all)
    c_b = c_act.astype(BF16)
    ncol, fcol = a["ada_w"].shape[2], a["final_ada_w"].shape[1]
    cols = []
    for l in range(depth):
        bias = jnp.broadcast_to(lax.dynamic_slice_in_dim(a["ada_b"][l], me * ncol, ncol)[None], (batch, ncol))
        cols.append(_mm(f"ada{l}", c_b, a["ada_w"][l].astype(BF16), NN, F32, acc=bias))
    bias = jnp.broadcast_to(lax.dynamic_slice_in_dim(a["final_ada_b"], me * fcol, fcol)[None], (batch, fcol))
    cols.append(_mm("ada_final", c_b, a["final_ada_w"].astype(BF16), NN, F32, acc=bias))
    mod_cols = jnp.concatenate(cols, axis=1).reshape(2, batch // 2, depth * ncol + fcol)

    names = DENSE + ("conv_w", "mod_cols")
    got = _gather_two_level("gather_weights", [a[n].astype(BF16) for n in DENSE] + [a["conv_w"], mod_cols])
    gw = dict(zip(names, got))
    w = _prepare(gw, a, d, chans, nh)
    mod_all = gw["mod_cols"].reshape(NUM_CHIPS, batch, -1)
    mine = lambda full: lax.dynamic_slice_in_dim(full, dev * bl, bl, axis=0)
    mods = [mine(mod_all[:, :, l * ncol:(l + 1) * ncol].transpose(1, 0, 2).reshape(batch, NUM_CHIPS * ncol))
            for l in range(depth)]
    fm = mine(mod_all[:, :, depth * ncol:].transpose(1, 0, 2).reshape(batch, NUM_CHIPS * fcol))

    core = jnp.reshape(mc, (1,)).astype(jnp.int32)
    chip = jnp.reshape(me, (1,)).astype(jnp.int32)

    def chip_partials(names, gl, tag):
        rs_in = []
        for n in names:
            if n == "w_in":
                ga, gb, gf = gl["w_in"]
                orig = jnp.concatenate([ga[:, :2 * chans], gb, gf[:, :nh], ga[:, 2 * chans:]], axis=1)
                rs_in.append(orig.reshape(d, NUM_CHIPS, -1).transpose(1, 0, 2))
            else:
                rs_in.append(gl[n])
        rs_in = [g.reshape(g.shape[0], 2, g.shape[1] // 2, g.shape[2]) for g in rs_in]
        theirs = _split_to_sibling(f"split_grads_{tag}", rs_in)
        return {n: _add_pair(f"add_cores_{tag}_{n}", g, t, core) for n, g, t in zip(names, rs_in, theirs)}

    assert depth == 2
    loss, grad_x, grads, dmods, dfm, (parts, slots) = _local_step(w, x, tgt, mods, fm, chip_partials)
    loss = lax.psum(loss, ("x", "y", "c"))
    left = [(l, n) for l in range(depth) for n in DENSE if (l, n) not in slots]
    for (l, n), p in zip(left, chip_partials([n for _, n in left], grads["layers"][0], "l0_ffn1").values()):
        parts[(l, n)] = p
    slots.update(zip(left, _scatter_chips("scatter_grads", [parts[k] for k in left])))
    order = [(l, n) for l in range(depth) for n in DENSE]
    reduced = [_sum_chips(f"sum_chips_{l}_{n}", slots[(l, n)], parts[(l, n)], chip) for l, n in order]
    other = _join_halves("join_grads", reduced)

    out = {}

    def put(n, res, per_layer):
        for kind, val in zip(("grad_", "delta_", "new_m_", "new_v_"), res):
            out[kind + n] = jnp.stack(val).reshape(a[n].shape) if per_layer else val.reshape(a[n].shape)

    def flat3(v):
        return v.reshape(depth, -1, v.shape[-1])

    assert depth == 2
    nd = len(DENSE)
    for k, n in enumerate(DENSE):
        put(n, _adamw_halves(f"adamw_{n}", (reduced[k], reduced[nd + k]), (other[k], other[nd + k]), core,
                             flat3(a[n]), flat3(a["m_" + n]), flat3(a["v_" + n])), False)

    items = []
    for l in range(depth):
        g = grads["layers"][l]
        items += [g["ffn1_norm"], g["mix_norm"], g["ffn2_norm"], g["rg_small"], _unblockdiag(g["rg_dwa"], bd),
                  _unblockdiag(g["rg_dwx"], bd), g["fox_bf"], g["merge_b"], dmods[l]]
    items += [grads["final_norm"], dfm]
    pack, layout = _pack(items)
    gath = _gather_all("gather_small", pack)
    tot = _sum_slots("sum_small", gath, F32)

    def wpack(pre):
        its = []
        for l in range(depth):
            rg_rows = jnp.concatenate([jnp.zeros((4, chans), F32), a[pre + "conv_b"][l][None], a[pre + "rg_ba"][l][None],
                                       a[pre + "rg_bx"][l][None], a[pre + "rg_lam"][l][None]], axis=0)
            its += [a[pre + "ffn1_norm"][l], a[pre + "mix_norm"][l], a[pre + "ffn2_norm"][l], rg_rows,
                    a[pre + "rg_wa"][l], a[pre + "rg_wx"][l], a[pre + "fox_bf"][l], a[pre + "merge_b"][l],
                    jnp.zeros((bl, 9 * d), F32)]
        its += [a[pre + "final_norm"], jnp.zeros((bl, 2 * d), F32)]
        return _pack(its)[0]

    res_small = [_unpack(r, layout) for r in [tot] + list(_adamw("adamw_small", tot, wpack(""), wpack("m_"), wpack("v_")))]
    per = len(SMALL) + 1
    for j, n in enumerate(SMALL):
        if n == "rg_small":
            for row, nm in ((4, "conv_b"), (5, "rg_ba"), (6, "rg_bx"), (7, "rg_lam")):
                put(nm, [[r[l * per + j][row] for l in range(depth)] for r in res_small], True)
        else:
            put(n, [[r[l * per + j] for l in range(depth)] for r in res_small], True)
    put("final_norm", [r[depth * per] for r in res_small], False)

    gflat = gath.reshape(NUM_DEVICES, -1)

    def rows_of(idx):
        o, n, shape = layout[idx]
        return gflat[:, o:o + n].reshape(NUM_DEVICES * shape[0], shape[1])

    late_g, ada = [], []
    for l in range(depth):
        dmod_all = rows_of(l * per + per - 1)
        late_g.append(_colsum(f"ada_b_grad_{l}", dmod_all))
        cut = lax.dynamic_slice_in_dim(dmod_all, me * ncol, ncol, axis=1).astype(BF16)
        ada.append(_mm(f"ada_w_grad_{l}", c_b, cut, TN, F32))
    put("ada_w", _adamw_layers("adamw_ada_w", ada[0], ada[1], a["ada_w"], a["m_ada_w"], a["v_ada_w"]), False)
    dfm_all = rows_of(depth * per + 1)
    late_g.append(_colsum("final_ada_b_grad", dfm_all))
    cut = lax.dynamic_slice_in_dim(dfm_all, me * fcol, fcol, axis=1).astype(BF16)
    gl = _mm("final_ada_w_grad", c_b, cut, TN, F32)
    put("final_ada_w", [gl] + list(_adamw("adamw_final_ada_w", gl, a["final_ada_w"], a["m_final_ada_w"],
                                          a["v_final_ada_w"])), False)
    cshard = a["conv_w"].shape[2]
    for l in range(depth):
        rg_tot = res_small[0][l * per + SMALL.index("rg_small")]
        late_g.append(lax.dynamic_slice_in_dim(rg_tot[:4], me * cshard, cshard, axis=1))
    gp2, layout2 = _pack(late_g)

    def wpack2(pre):
        return _pack([a[pre + "ada_b"][l][None] for l in range(depth)] + [a[pre + "final_ada_b"][None]]
                     + [a[pre + "conv_w"][l] for l in range(depth)])[0]

    res_late = [_unpack(r, layout2) for r in [gp2] + list(_adamw("adamw_late", gp2, wpack2(""), wpack2("m_"), wpack2("v_")))]
    put("ada_b", [[r[l] for l in range(depth)] for r in res_late], True)
    put("final_ada_b", [r[depth] for r in res_late], False)
    put("conv_w", [[r[depth + 1 + l] for l in range(depth)] for r in res_late], True)

    outs = [loss, grad_x]
    for kind in ("grad_", "delta_", "new_m_", "new_v_"):
        outs += [out[kind + n] for n in WEIGHTS]
    return tuple(outs)


def kernel(x, c, ffn1_norm, ffn1_w1, ffn1_w3, ffn1_w2, mix_norm, w_in, conv_w, conv_b, rg_wa, rg_ba, rg_wx, rg_bx, rg_lam, fox_bf, merge_b, w_rg, w_sb, w_fox, w_o, ffn2_norm, ffn2_w1, ffn2_w3, ffn2_w2, ada_w, ada_b, final_norm, final_ada_w, final_ada_b, loss_target, m_ffn1_norm, m_ffn1_w1, m_ffn1_w3, m_ffn1_w2, m_mix_norm, m_w_in, m_conv_w, m_conv_b, m_rg_wa, m_rg_ba, m_rg_wx, m_rg_bx, m_rg_lam, m_fox_bf, m_merge_b, m_w_rg, m_w_sb, m_w_fox, m_w_o, m_ffn2_norm, m_ffn2_w1, m_ffn2_w3, m_ffn2_w2, m_ada_w, m_ada_b, m_final_norm, m_final_ada_w, m_final_ada_b, v_ffn1_norm, v_ffn1_w1, v_ffn1_w3, v_ffn1_w2, v_mix_norm, v_w_in, v_conv_w, v_conv_b, v_rg_wa, v_rg_ba, v_rg_wx, v_rg_bx, v_rg_lam, v_fox_bf, v_merge_b, v_w_rg, v_w_sb, v_w_fox, v_w_o, v_ffn2_norm, v_ffn2_w1, v_ffn2_w3, v_ffn2_w2, v_ada_w, v_ada_b, v_final_norm, v_final_ada_w, v_final_ada_b):
    args = dict(locals())
    return _step(args)
```

```python
import math

import jax
import jax.numpy as jnp
from jax import lax
from jax.experimental import pallas as pl
from jax.experimental.pallas import tpu as pltpu

F32 = jnp.float32
BF16 = jnp.bfloat16

NUM_CHIPS = 4
NUM_DEVICES = 8
HEAD_DIM = 64
LANE = 128
SUBLANE = 8
VMEM_LIMIT = 56 * 1024 * 1024
EPS = 1e-6
RG_C = 8.0
ADAM_LR = 0.001
ADAM_B1 = 0.9
ADAM_B2 = 0.999
ADAM_EPS = 1e-08
ADAM_WD = 0.01
ADAM_STEP = 10
MESH = pl.DeviceIdType.MESH
ANY = pl.BlockSpec(memory_space=pl.ANY)
SCALAR = pl.BlockSpec(memory_space=pltpu.SMEM)


def _params():
    return pltpu.CompilerParams(vmem_limit_bytes=VMEM_LIMIT)


def _tile(dim, pref):
    if dim <= pref:
        return dim
    t = (pref // LANE) * LANE
    while t >= LANE:
        if dim % t == 0:
            return t
        t -= LANE
    return dim


def _rtile(rows, pref, unit=2 * SUBLANE):
    if rows <= pref:
        return rows
    t = (pref // unit) * unit
    while t >= unit:
        if rows % t == 0:
            return t
        t -= unit
    return rows


def _sigmoid(x):
    return 0.5 * jnp.tanh(0.5 * x) + 0.5


def _softplus(x):
    return jnp.maximum(x, 0.0) + jnp.log(1.0 + jnp.exp(-jnp.abs(x)))


def _expm1(x):
    small = x * (1.0 + x * (0.5 + x * (1.0 / 6.0 + x * (1.0 / 24.0))))
    return jnp.where(jnp.abs(x) < 0.01, small, jnp.exp(x) - 1.0)


_GELU_K = math.sqrt(2.0 / math.pi)


def _gelu_and_grad(x):
    inner = _GELU_K * (x + 0.044715 * x * x * x)
    t = jnp.tanh(inner)
    val = 0.5 * x * (1.0 + t)
    dinner = _GELU_K * (1.0 + 3.0 * 0.044715 * x * x)
    grad = 0.5 * (1.0 + t) + 0.5 * x * (1.0 - t * t) * dinner
    return val, grad


NN = ((1,), (0,))
NT = ((1,), (1,))
TN = ((0,), (0,))
ALL = slice(None)


def _ride_call(name, body, grid, in_specs, out_specs, out_shape, scratch_shapes, args, riders=()):
    nr = len(riders)
    if not nr:
        return pl.pallas_call(body, name=name, grid=grid, in_specs=in_specs, out_specs=out_specs, out_shape=out_shape,
                              scratch_shapes=scratch_shapes, compiler_params=_params())(*args)
    n_in, n_out, n_scr = len(in_specs), len(out_specs), len(scratch_shapes)

    def hosted(*refs):
        ins, ride_in = refs[:n_in], refs[n_in:n_in + nr]
        outs = refs[n_in + nr:n_in + nr + n_out]
        ride_out = refs[n_in + nr + n_out:n_in + 2 * nr + n_out]
        scr = refs[n_in + 2 * nr + n_out:n_in + 2 * nr + n_out + n_scr]
        send, recv = refs[-2:]
        x, y, c = _mesh_pos()
        copies = []
        for j, (px, py) in enumerate(_other_chips(x, y)):
            for i in range(nr):
                copies.append(pltpu.make_async_remote_copy(
                    src_ref=ride_in[i].at[2 * px + py], dst_ref=ride_out[i].at[j], send_sem=send.at[j * nr + i],
                    recv_sem=recv.at[j * nr + i], device_id=(px, py, c), device_id_type=MESH))
        first = pl.program_id(0) == 0
        last = pl.program_id(0) == grid[0] - 1
        for k in range(1, len(grid)):
            first = first & (pl.program_id(k) == 0)
            last = last & (pl.program_id(k) == grid[k] - 1)

        @pl.when(first)
        def _():
            for cp in copies:
                cp.start()

        body(*ins, *outs, *scr)

        @pl.when(last)
        def _():
            for cp in copies:
                cp.wait()

    slots = [jax.ShapeDtypeStruct((NUM_CHIPS - 1,) + r.shape[1:], r.dtype) for r in riders]
    return pl.pallas_call(
        hosted, name=name, grid=grid, in_specs=list(in_specs) + [ANY] * nr, out_specs=list(out_specs) + [ANY] * nr,
        out_shape=list(out_shape) + slots,
        scratch_shapes=list(scratch_shapes) + [pltpu.SemaphoreType.DMA((3 * nr,)), pltpu.SemaphoreType.DMA((3 * nr,))],
        compiler_params=_params(),
    )(*args, *riders)


def _mmk(name, ops, specs, terms, out_shape, out_dtype, grid, o_spec, acc=None, riders=()):
    n_ops = len(ops)

    def body(*refs):
        o_ref = refs[-1]
        p = None
        for ia, xa, ib, xb, dims in terms:
            t = lax.dot_general(refs[ia][xa], refs[ib][xb], (dims, ((), ())), preferred_element_type=F32)
            p = t if p is None else p + t
        if acc is not None:
            p = p + refs[n_ops][...].astype(F32)
        o_ref[...] = p.astype(o_ref.dtype)

    in_specs = list(specs)
    args = list(ops)
    if acc is not None:
        in_specs.append(pl.BlockSpec(o_spec.block_shape, o_spec.index_map))
        args.append(acc)
    res = _ride_call(name, body, grid, in_specs, [o_spec], [jax.ShapeDtypeStruct(out_shape, out_dtype)], [], args,
                     riders)
    return res if riders else res[0]


def _mm(name, a, b, dims, out_dtype, acc=None, tm=512, tn=512):
    if dims == NN:
        (m, kk), n = a.shape, b.shape[1]
    elif dims == NT:
        (m, kk), n = a.shape, b.shape[0]
    else:
        (kk, m), n = a.shape, b.shape[1]
    tm, tn = _tile(m, tm), _tile(n, tn)
    if dims == TN:
        a_spec = pl.BlockSpec((kk, tm), lambda i, j: (0, i))
    else:
        a_spec = pl.BlockSpec((tm, kk), lambda i, j: (i, 0))
    if dims == NT:
        b_spec = pl.BlockSpec((tn, kk), lambda i, j: (j, 0))
    else:
        b_spec = pl.BlockSpec((kk, tn), lambda i, j: (0, j))
    return _mmk(name, [a, b], [a_spec, b_spec], [(0, ALL, 1, ALL, dims)], (m, n), out_dtype,
                (m // tm, n // tn), pl.BlockSpec((tm, tn), lambda i, j: (i, j)), acc)


def _row_grid(t_rows, seq, pref=256):
    tm = _tile(seq, pref)
    return tm, seq // tm


def _normmod(name, x, gain, shift, scale, seq):
    t_rows, d = x.shape
    bl = t_rows // seq
    tm, per = _row_grid(t_rows, seq)

    def body(x_ref, g_ref, sh_ref, sc_ref, o_ref):
        xv = x_ref[...]
        rstd = lax.rsqrt(jnp.mean(xv * xv, axis=-1, keepdims=True) + EPS)
        hn = (xv * rstd) * g_ref[...]
        o_ref[...] = (hn * (1.0 + sc_ref[...]) + sh_ref[...]).astype(o_ref.dtype)

    row = pl.BlockSpec((tm, d), lambda b, i: (b * per + i, 0))
    vec = pl.BlockSpec((None, 1, d), lambda b, i: (b, 0, 0))
    return pl.pallas_call(
        body, name=name, grid=(bl, per),
        in_specs=[row, pl.BlockSpec((1, d), lambda b, i: (0, 0)), vec, vec],
        out_specs=row, out_shape=jax.ShapeDtypeStruct((t_rows, d), BF16),
        compiler_params=_params(),
    )(x, gain, shift, scale)


def _normmod_bwd(name, x, dh, dxo, gain, scale, seq):
    t_rows, d = x.shape
    bl = t_rows // seq
    tm, per = _row_grid(t_rows, seq)

    def body(x_ref, dh_ref, dxo_ref, g_ref, sc_ref, dx_ref, dsh_ref, dsc_ref, dg_ref):
        b, i = pl.program_id(0), pl.program_id(1)
        xv = x_ref[...]
        dhv = dh_ref[...]
        rstd = lax.rsqrt(jnp.mean(xv * xv, axis=-1, keepdims=True) + EPS)
        xhat = xv * rstd
        gain_v = g_ref[...]
        dhn = dhv * (1.0 + sc_ref[...])
        dxhat = dhn * gain_v
        dx = rstd * (dxhat - xhat * jnp.mean(dxhat * xhat, axis=-1, keepdims=True))
        dx_ref[...] = dxo_ref[...] + dx

        @pl.when(i == 0)
        def _():
            dsh_ref[...] = jnp.zeros_like(dsh_ref)
            dsc_ref[...] = jnp.zeros_like(dsc_ref)

        @pl.when((i == 0) & (b == 0))
        def _():
            dg_ref[...] = jnp.zeros_like(dg_ref)

        dsh_ref[...] += jnp.sum(dhv, axis=0, keepdims=True)
        dsc_ref[...] += jnp.sum(dhv * (xhat * gain_v), axis=0, keepdims=True)
        dg_ref[...] += jnp.sum(dhn * xhat, axis=0, keepdims=True)

    row = pl.BlockSpec((tm, d), lambda b, i: (b * per + i, 0))
    vec = pl.BlockSpec((None, 1, d), lambda b, i: (b, 0, 0))
    one = pl.BlockSpec((1, d), lambda b, i: (0, 0))
    return pl.pallas_call(
        body, name=name, grid=(bl, per),
        in_specs=[row, row, row, one, vec],
        out_specs=[row, vec, vec, one],
        out_shape=[jax.ShapeDtypeStruct((t_rows, d), F32), jax.ShapeDtypeStruct((bl, 1, d), F32),
                   jax.ShapeDtypeStruct((bl, 1, d), F32), jax.ShapeDtypeStruct((1, d), F32)],
        compiler_params=_params(),
    )(x, dh, dxo, gain, scale)


def _resid(name, x, y, gate, coef, seq):
    t_rows, d = x.shape
    bl = t_rows // seq
    tm, per = _row_grid(t_rows, seq)

    def body(x_ref, y_ref, g_ref, o_ref):
        o_ref[...] = x_ref[...] + (coef * (1.0 + g_ref[...])) * y_ref[...]

    row = pl.BlockSpec((tm, d), lambda b, i: (b * per + i, 0))
    vec = pl.BlockSpec((None, 1, d), lambda b, i: (b, 0, 0))
    return pl.pallas_call(
        body, name=name, grid=(bl, per), in_specs=[row, row, vec], out_specs=row,
        out_shape=jax.ShapeDtypeStruct((t_rows, d), F32), compiler_params=_params(),
    )(x, y, gate)


def _resid_bwd(name, dxo, y, gate, coef, seq):
    t_rows, d = dxo.shape
    bl = t_rows // seq
    tm, per = _row_grid(t_rows, seq)

    def body(dxo_ref, y_ref, g_ref, dy_ref, dg_ref):
        i = pl.program_id(1)
        dxov = dxo_ref[...]
        dy_ref[...] = ((coef * (1.0 + g_ref[...])) * dxov).astype(dy_ref.dtype)

        @pl.when(i == 0)
        def _():
            dg_ref[...] = jnp.zeros_like(dg_ref)

        dg_ref[...] += jnp.sum((coef * y_ref[...]) * dxov, axis=0, keepdims=True)

    row = pl.BlockSpec((tm, d), lambda b, i: (b * per + i, 0))
    vec = pl.BlockSpec((None, 1, d), lambda b, i: (b, 0, 0))
    return pl.pallas_call(
        body, name=name, grid=(bl, per), in_specs=[row, row, vec], out_specs=[row, vec],
        out_shape=[jax.ShapeDtypeStruct((t_rows, d), BF16), jax.ShapeDtypeStruct((bl, 1, d), F32)],
        compiler_params=_params(),
    )(dxo, y, gate)


def _final_loss(name, x, tgt, gain, shift, scale, seq):
    t_rows, d = x.shape
    bl = t_rows // seq
    tm, per = _row_grid(t_rows, seq)

    def body(x_ref, t_ref, g_ref, sh_ref, sc_ref, l_ref, dx_ref, dsh_ref, dsc_ref, dg_ref):
        b, i = pl.program_id(0), pl.program_id(1)
        xv = x_ref[...]
        rstd = lax.rsqrt(jnp.mean(xv * xv, axis=-1, keepdims=True) + EPS)
        xhat = xv * rstd
        gain_v = g_ref[...]
        hn = xhat * gain_v
        yv = hn * (1.0 + sc_ref[...]) + sh_ref[...]
        err = yv - t_ref[...]
        dyv = err * (1.0 / d)
        dhn = dyv * (1.0 + sc_ref[...])
        dxhat = dhn * gain_v
        dx_ref[...] = rstd * (dxhat - xhat * jnp.mean(dxhat * xhat, axis=-1, keepdims=True))

        @pl.when(i == 0)
        def _():
            l_ref[...] = jnp.zeros_like(l_ref)
            dsh_ref[...] = jnp.zeros_like(dsh_ref)
            dsc_ref[...] = jnp.zeros_like(dsc_ref)

        @pl.when((i == 0) & (b == 0))
        def _():
            dg_ref[...] = jnp.zeros_like(dg_ref)

        part = jnp.sum(jnp.sum(err * err, axis=-1, keepdims=True), axis=0, keepdims=True) * (0.5 / d)
        l_ref[...] += jnp.broadcast_to(part, l_ref.shape)
        dsh_ref[...] += jnp.sum(dyv, axis=0, keepdims=True)
        dsc_ref[...] += jnp.sum(dyv * hn, axis=0, keepdims=True)
        dg_ref[...] += jnp.sum(dhn * xhat, axis=0, keepdims=True)

    row = pl.BlockSpec((tm, d), lambda b, i: (b * per + i, 0))
    vec = pl.BlockSpec((None, 1, d), lambda b, i: (b, 0, 0))
    one = pl.BlockSpec((1, d), lambda b, i: (0, 0))
    lvec = pl.BlockSpec((None, 1, LANE), lambda b, i: (b, 0, 0))
    return pl.pallas_call(
        body, name=name, grid=(bl, per),
        in_specs=[row, row, one, vec, vec],
        out_specs=[lvec, row, vec, vec, one],
        out_shape=[jax.ShapeDtypeStruct((bl, 1, LANE), F32), jax.ShapeDtypeStruct((t_rows, d), F32),
                   jax.ShapeDtypeStruct((bl, 1, d), F32), jax.ShapeDtypeStruct((bl, 1, d), F32),
                   jax.ShapeDtypeStruct((1, d), F32)],
        compiler_params=_params(),
    )(x, tgt, gain, shift, scale)


def _ffn_up(name, h, w1, w3, l):
    t_rows, d = h.shape
    ng, fs = w1.shape[0], w1.shape[3]
    tm = _tile(t_rows, 512)

    def body(h_ref, w1_ref, w3_ref, a_ref, b_ref, g_ref):
        hv = h_ref[...]
        av = jnp.dot(hv, w1_ref[...], preferred_element_type=F32)
        bv = jnp.dot(hv, w3_ref[...], preferred_element_type=F32)
        a_ref[...] = av.astype(a_ref.dtype)
        b_ref[...] = bv.astype(b_ref.dtype)
        g_ref[...] = (av * _sigmoid(av) * bv).astype(g_ref.dtype)

    wspec = pl.BlockSpec((None, None, d, fs), lambda g, i: (g, l, 0, 0))
    out = pl.BlockSpec((None, tm, fs), lambda g, i: (g, i, 0))
    f = jax.ShapeDtypeStruct((ng, t_rows, fs), BF16)
    return pl.pallas_call(
        body, name=name, grid=(ng, t_rows // tm),
        in_specs=[pl.BlockSpec((tm, d), lambda g, i: (i, 0)), wspec, wspec], out_specs=[out, out, out],
        out_shape=[f, f, f], compiler_params=_params(),
    )(h, w1, w3)


def _ffn_down_dx(name, dy, w2, a, b, l, riders=()):
    t_rows, d = dy.shape
    ng, fs = w2.shape[0], w2.shape[2]
    tm = _tile(t_rows, 512)

    def body(dy_ref, w2_ref, a_ref, b_ref, da_ref, db_ref):
        dgv = lax.dot_general(dy_ref[...], w2_ref[...], (NT, ((), ())), preferred_element_type=F32)
        av = a_ref[...].astype(F32)
        sig = _sigmoid(av)
        da_ref[...] = (dgv * b_ref[...].astype(F32) * (sig * (1.0 + av * (1.0 - sig)))).astype(da_ref.dtype)
        db_ref[...] = (dgv * (av * sig)).astype(db_ref.dtype)

    blk = pl.BlockSpec((None, tm, fs), lambda g, i: (g, i, 0))
    o = jax.ShapeDtypeStruct((ng, t_rows, fs), BF16)
    return _ride_call(
        name, body, (ng, t_rows // tm),
        [pl.BlockSpec((tm, d), lambda g, i: (i, 0)), pl.BlockSpec((None, None, fs, d), lambda g, i: (g, l, 0, 0)),
         blk, blk],
        [blk, blk], [o, o], [], (dy, w2, a, b), riders)


def _ffn_up_dw(name, h, da, db):
    t_rows, d = h.shape
    ng, fs = da.shape[0], da.shape[2]
    tn = _tile(d, 512)

    def body(h_ref, da_ref, db_ref, o1_ref, o3_ref):
        hv = h_ref[...]
        o1_ref[...] = lax.dot_general(hv, da_ref[...], (TN, ((), ())), preferred_element_type=F32).astype(o1_ref.dtype)
        o3_ref[...] = lax.dot_general(hv, db_ref[...], (TN, ((), ())), preferred_element_type=F32).astype(o3_ref.dtype)

    dspec = pl.BlockSpec((None, t_rows, fs), lambda g, i: (g, 0, 0))
    out = pl.BlockSpec((None, tn, fs), lambda g, i: (g, i, 0))
    o = jax.ShapeDtypeStruct((ng, d, fs), BF16)
    return pl.pallas_call(
        body, name=name, grid=(ng, d // tn),
        in_specs=[pl.BlockSpec((t_rows, tn), lambda g, i: (0, i)), dspec, dspec],
        out_specs=[out, out], out_shape=[o, o], compiler_params=_params(),
    )(h, da, db)


def _ffn_fwd(tag, w, l, pre, x, mod, seq):
    t_rows, d = x.shape
    w1, w3, w2 = w[pre + "w1"], w[pre + "w3"], w[pre + "w2"]
    ng, fs = w1.shape[0], w1.shape[3]
    shift, scale, gate = mod
    h = _normmod(tag + "_norm", x, w[pre + "norm"][l][None], shift, scale, seq)
    a, b, gact = _ffn_up(tag + "_up", h, w1, w3, l)
    tm, tn = _tile(t_rows, 512), _tile(d, 512)
    y = _mmk(tag + "_down", [gact, w2],
             [pl.BlockSpec((ng, tm, fs), lambda i, j: (0, i, 0)),
              pl.BlockSpec((ng, None, fs, tn), lambda i, j: (0, l, 0, j))],
             [(0, g, 1, g, NN) for g in range(ng)], (t_rows, d), F32, (t_rows // tm, d // tn),
             pl.BlockSpec((tm, tn), lambda i, j: (i, j)))
    xn = _resid(tag + "_res", x, y, gate, 0.5, seq)
    return xn, (x, h, a, b, gact, y)


def _ffn_bwd(tag, w, l, pre, saved, mod, dxo, seq, riders_dn=(), riders_up=()):
    x, h, a, b, gact, y = saved
    t_rows, d = x.shape
    w1, w3, w2 = w[pre + "w1"], w[pre + "w3"], w[pre + "w2"]
    ng, fs = w1.shape[0], w1.shape[3]
    shift, scale, gate = mod
    tm, tn = _tile(t_rows, 512), _tile(d, 512)
    dy, dgate = _resid_bwd(tag + "_res_bwd", dxo, y, gate, 0.5, seq)
    da, db, *slots_dn = _ffn_down_dx(tag + "_down_dx", dy, w2, a, b, l, riders_dn)
    dw2 = _mmk(tag + "_down_dw", [gact, dy],
               [pl.BlockSpec((None, t_rows, fs), lambda g, j: (g, 0, 0)),
                pl.BlockSpec((t_rows, tn), lambda g, j: (0, j))],
               [(0, ALL, 1, ALL, TN)], (ng, fs, d), BF16, (ng, d // tn),
               pl.BlockSpec((None, fs, tn), lambda g, j: (g, 0, j)))
    dw1, dw3 = _ffn_up_dw(tag + "_up_dw", h, da, db)
    dspec = pl.BlockSpec((ng, tm, fs), lambda i, j: (0, i, 0))
    wspec = pl.BlockSpec((ng, None, tn, fs), lambda i, j: (0, l, j, 0))
    dh = _mmk(tag + "_up_dx", [da, db, w1, w3], [dspec, dspec, wspec, wspec],
              [(0, g, 2, g, NT) for g in range(ng)] + [(1, g, 3, g, NT) for g in range(ng)],
              (t_rows, d), F32, (t_rows // tm, d // tn), pl.BlockSpec((tm, tn), lambda i, j: (i, j)),
              riders=riders_up)
    slots_up = []
    if riders_up:
        dh, *slots_up = dh
    dx, dshift, dscale, dgain = _normmod_bwd(tag + "_norm_bwd", x, dh, dxo, w[pre + "norm"][l][None], scale, seq)
    grads = {pre + "w1": dw1, pre + "w3": dw3, pre + "w2": dw2, pre + "norm": dgain}
    return dx, (dshift, dscale, dgate), grads, slots_dn, slots_up


def _shift_down(v, s, row):
    if s == 0:
        return v
    return jnp.where(row >= s, pltpu.roll(v, s, 0), 0.0)


def _shift_up(v, s, row):
    if s == 0:
        return v
    n = v.shape[0]
    return jnp.where(row < n - s, pltpu.roll(v, n - s, 0), 0.0)


def _scan_fwd(a, u, row):
    n = a.shape[0]
    s = 1
    while s < n:
        ok = row >= s
        a_sh = pltpu.roll(a, s, 0)
        u_sh = pltpu.roll(u, s, 0)
        u = jnp.where(ok, a * u_sh + u, u)
        a = jnp.where(ok, a * a_sh, a)
        s *= 2
    return u


def _scan_bwd(a_next, g, row):
    n = g.shape[0]
    a, u = a_next, g
    s = 1
    while s < n:
        ok = row < n - s
        a_sh = pltpu.roll(a, n - s, 0)
        u_sh = pltpu.roll(u, n - s, 0)
        u = jnp.where(ok, a * u_sh + u, u)
        a = jnp.where(ok, a * a_sh, a)
        s *= 2
    return u


def _rg_specs(seq, cw):
    slab = lambda off: pl.BlockSpec((seq, cw), lambda c, b: (b, off + c))
    par = lambda rows: pl.BlockSpec((rows, cw), lambda c, b: (0, c))
    wbd = pl.BlockSpec((None, cw, cw), lambda c, b: (c, 0, 0))
    return slab, par, wbd


def _rg_fwd(name, proj, p, seq, chans):
    t_rows = proj.shape[0]
    bl = t_rows // seq
    cw = LANE
    nc = chans // cw
    slab, par, wbd = _rg_specs(seq, cw)

    def body(x_ref, gt_ref, cw_ref, cb_ref, wa_ref, ba_ref, wx_ref, bx_ref, lam_ref,
             xa_ref, r_ref, i_ref, h_ref, ya_ref):
        row = lax.broadcasted_iota(jnp.int32, (seq, cw), 0)
        xv = x_ref[...]
        xa = jnp.zeros_like(xv) + cb_ref[...]
        for k in range(4):
            xa = xa + cw_ref[k:k + 1, :] * _shift_down(xv, 3 - k, row)
        xab = xa.astype(BF16)
        r = _sigmoid(jnp.dot(xab, wa_ref[...], preferred_element_type=F32) + ba_ref[...])
        ig = _sigmoid(jnp.dot(xab, wx_ref[...], preferred_element_type=F32) + bx_ref[...])
        log_a = (-RG_C) * r * _softplus(-lam_ref[...])
        a = jnp.exp(log_a)
        u = jnp.sqrt(-_expm1(2.0 * log_a)) * (ig * xa)
        h = _scan_fwd(a, u, row)
        gel, _ = _gelu_and_grad(gt_ref[...])
        xa_ref[...] = xa
        r_ref[...] = r
        i_ref[...] = ig
        h_ref[...] = h
        ya_ref[...] = (gel * h).astype(ya_ref.dtype)

    out = pl.BlockSpec((seq, cw), lambda c, b: (b, c))
    f = jax.ShapeDtypeStruct((t_rows, chans), F32)
    return pl.pallas_call(
        body, name=name, grid=(nc, bl),
        in_specs=[slab(0), slab(nc), par(4), par(1), wbd, par(1), wbd, par(1), par(1)],
        out_specs=[out] * 5,
        out_shape=[f, f, f, f, jax.ShapeDtypeStruct((t_rows, chans), BF16)],
        compiler_params=_params(),
    )(proj, proj, p["conv_w"], p["conv_b"], p["wa"], p["ba"], p["wx"], p["bx"], p["lam"])


def _rg_bwd(name, proj, dya, saved, p, seq, chans):
    xa_s, r_s, i_s, h_s = saved
    t_rows = proj.shape[0]
    bl = t_rows // seq
    cw = LANE
    nc = chans // cw
    slab, par, wbd = _rg_specs(seq, cw)

    def body(x_ref, gt_ref, dya_ref, xa_ref, r_ref, i_ref, h_ref, cw_ref, wa_ref, wx_ref, lam_ref,
             dx_ref, dgt_ref, sm_ref, dwa_ref, dwx_ref):
        b = pl.program_id(1)
        row = lax.broadcasted_iota(jnp.int32, (seq, cw), 0)
        xv, xa, r, ig, h = x_ref[...], xa_ref[...], r_ref[...], i_ref[...], h_ref[...]
        dyav = dya_ref[...]
        gel, dgel = _gelu_and_grad(gt_ref[...])
        dgt_ref[...] = (dyav * h * dgel).astype(dgt_ref.dtype)
        dh = dyav * gel
        lam = lam_ref[...]
        sp = _softplus(-lam)
        log_a = (-RG_C) * r * sp
        a = jnp.exp(log_a)
        s = jnp.sqrt(-_expm1(2.0 * log_a))
        lamb = _scan_bwd(_shift_up(a, 1, row), dh, row)
        da = lamb * _shift_down(h, 1, row)
        xi = ig * xa
        ds = lamb * xi
        dxi = lamb * s
        dlog = da * a - ds * (a * a) / s
        dr = dlog * ((-RG_C) * sp)
        dsp = jnp.sum(dlog * ((-RG_C) * r), axis=0, keepdims=True)
        dlam = -dsp * _sigmoid(-lam)
        dzr = dr * r * (1.0 - r)
        dzi = (dxi * xa) * ig * (1.0 - ig)
        dzrb, dzib, xab = dzr.astype(BF16), dzi.astype(BF16), xa.astype(BF16)
        dxa = dxi * ig
        dxa = dxa + lax.dot_general(dzrb, wa_ref[...], (NT, ((), ())), preferred_element_type=F32)
        dxa = dxa + lax.dot_general(dzib, wx_ref[...], (NT, ((), ())), preferred_element_type=F32)
        dwa = lax.dot_general(xab, dzrb, (TN, ((), ())), preferred_element_type=F32)
        dwx = lax.dot_general(xab, dzib, (TN, ((), ())), preferred_element_type=F32)
        dxv = jnp.zeros_like(xv)
        rows = []
        for k in range(4):
            dxv = dxv + cw_ref[k:k + 1, :] * _shift_up(dxa, 3 - k, row)
            rows.append(jnp.sum(dxa * _shift_down(xv, 3 - k, row), axis=0, keepdims=True))
        dx_ref[...] = dxv.astype(dx_ref.dtype)
        rows += [jnp.sum(dxa, axis=0, keepdims=True), jnp.sum(dzr, axis=0, keepdims=True),
                 jnp.sum(dzi, axis=0, keepdims=True), dlam]

        @pl.when(b == 0)
        def _():
            sm_ref[...] = jnp.zeros_like(sm_ref)
            dwa_ref[...] = jnp.zeros_like(dwa_ref)
            dwx_ref[...] = jnp.zeros_like(dwx_ref)

        for k, val in enumerate(rows):
            sm_ref[k:k + 1, :] += val
        dwa_ref[...] += dwa
        dwx_ref[...] += dwx

    plain = pl.BlockSpec((seq, cw), lambda c, b: (b, c))
    return pl.pallas_call(
        body, name=name, grid=(nc, bl),
        in_specs=[slab(0), slab(nc), plain, plain, plain, plain, plain, par(4), wbd, wbd, par(1)],
        out_specs=[plain, plain, par(8), wbd, wbd],
        out_shape=[jax.ShapeDtypeStruct((t_rows, chans), BF16), jax.ShapeDtypeStruct((t_rows, chans), BF16),
                   jax.ShapeDtypeStruct((8, chans), F32),
                   jax.ShapeDtypeStruct((nc, cw, cw), F32), jax.ShapeDtypeStruct((nc, cw, cw), F32)],
        compiler_params=_params(),
    )(proj, proj, dya, xa_s, r_s, i_s, h_s, p["conv_w"], p["wa"], p["wx"], p["lam"])


ATT_Q_BLOCK = 512
SB_K_BLOCK = 256
FOX_K_BLOCK = 512
PAIR = LANE // HEAD_DIM
NEG = -1e30
SCALE = HEAD_DIM ** -0.5
assert math.log2(HEAD_DIM) % 2 == 0


def _att_blocks(seq, k_block):
    return _tile(seq, ATT_Q_BLOCK), _tile(seq, k_block)


def _key_blocks(qi, tq, bk):
    return (qi * tq) // bk, (qi * tq + tq - 1) // bk + 1


def _tri(n, kind):
    r = lax.broadcasted_iota(jnp.int32, (2 * n, n), 0)
    r = jnp.where(r >= n, r - n, r)
    c = lax.broadcasted_iota(jnp.int32, (2 * n, n), 1)
    m = {"gt": r > c, "le": r <= c, "lt": r < c}[kind]
    return m.astype(BF16)


def _cumsum_mm(v, tri):
    hi = v.astype(BF16)
    lo = (v - hi.astype(F32)).astype(BF16)
    return jnp.dot(jnp.concatenate([hi, lo], axis=1), tri, preferred_element_type=F32)


def _head_masks():
    lane = lax.broadcasted_iota(jnp.int32, (1, LANE), 1)
    return [(lane >= h * HEAD_DIM) & (lane < (h + 1) * HEAD_DIM) for h in range(PAIR)]


def _only(mask, v):
    return jnp.where(mask, v, jnp.zeros_like(v))


def _stack_heads(v, masks):
    return jnp.concatenate([_only(m, v) for m in masks], axis=0)


def _unstack_heads(v, masks):
    tq = v.shape[0] // PAIR
    out = _only(masks[0], v[0:tq])
    for h in range(1, PAIR):
        out = out + _only(masks[h], v[h * tq:(h + 1) * tq])
    return out


def _stacked_iotas(tq, bk):
    row = lax.broadcasted_iota(jnp.int32, (PAIR * tq, bk), 0)
    for h in range(1, PAIR):
        row = jnp.where(row >= h * tq, row - tq, row)
    return row, lax.broadcasted_iota(jnp.int32, (PAIR * tq, bk), 1)


def _att_specs(seq, blk, nq, off):
    npair = None
    qs = lambda o: pl.BlockSpec((blk, LANE), lambda b, p, i: (b * nq + i, o + p))
    ks = lambda o: pl.BlockSpec((seq, LANE), lambda b, p, i: (b, o + p))
    col = pl.BlockSpec((None, PAIR, blk, 1), lambda b, p, i: (b, p, i, 0))
    lane = pl.BlockSpec((None, PAIR, 1, seq), lambda b, p, i: (b, p, 0, 0))
    return qs, ks, col, lane


def _sb_fwd(name, qkv, off, width, bl, seq):
    t_rows = qkv.shape[0]
    tq, bk = _att_blocks(seq, SB_K_BLOCK)
    nq = seq // tq
    nb = width // LANE
    qs, ks, col, _ = _att_specs(seq, tq, nq, off)

    def body(q_ref, k_ref, v_ref, o_ref, lt_ref):
        qi = pl.program_id(2)
        masks = _head_masks()
        qs_ = _stack_heads(q_ref[...] * SCALE, masks)
        row, cix = _stacked_iotas(tq, bk)
        tri = _tri(bk, "gt")

        def step(masked, top):
            def go(it, carry):
                acc, cl = carry
                kb = top - it
                ks_ = pl.multiple_of(kb * bk, bk)
                kv = k_ref[pl.ds(ks_, bk), :]
                vv = v_ref[pl.ds(ks_, bk), :]
                strict = (kb * bk + cix) < (qi * tq + row)
                z = lax.dot_general(qs_, kv, (NT, ((), ())), preferred_element_type=F32)
                sp = _softplus(z)
                lk = jnp.where(strict, -sp, 0.0) if masked else -sp
                wgt = jnp.exp(z - sp + (cl + _cumsum_mm(lk, tri)))
                if masked:
                    wgt = jnp.where(strict, wgt, 0.0)
                acc = acc + _unstack_heads(jnp.dot(wgt.astype(BF16), vv, preferred_element_type=F32), masks)
                return acc, cl + jnp.sum(lk, axis=1, keepdims=True)
            return go

        n_full, n_all = _key_blocks(qi, tq, bk)
        carry = (jnp.zeros((tq, LANE), F32), jnp.zeros((PAIR * tq, 1), F32))
        carry = lax.fori_loop(0, n_all - n_full, step(True, n_all - 1), carry)
        acc, cl = lax.fori_loop(0, n_full, step(False, n_full - 1), carry)
        o_ref[...] = acc.astype(o_ref.dtype)
        for h in range(PAIR):
            lt_ref[h] = cl[h * tq:(h + 1) * tq]

    return pl.pallas_call(
        body, name=name, grid=(bl, nb, nq), in_specs=[qs(off), ks(off + nb), ks(off + 2 * nb)],
        out_specs=[qs(0), col],
        out_shape=[jax.ShapeDtypeStruct((t_rows, width), BF16),
                   jax.ShapeDtypeStruct((bl, nb * PAIR, seq, 1), F32)],
        compiler_params=_params(),
    )(qkv, qkv, qkv)


def _sb_bwd(name, qkv, off, width, bl, seq, ltot, do, riders=()):
    t_rows = qkv.shape[0]
    tq, bk = _att_blocks(seq, SB_K_BLOCK)
    nq = seq // tq
    nb = width // LANE
    qs, ks, col, _ = _att_specs(seq, tq, nq, off)

    def body(q_ref, k_ref, v_ref, lt_ref, do_ref, dq_ref, dk_ref, dv_ref, dk_acc, dv_acc):
        qi = pl.program_id(2)

        @pl.when(qi == 0)
        def _():
            dk_acc[...] = jnp.zeros_like(dk_acc)
            dv_acc[...] = jnp.zeros_like(dv_acc)

        masks = _head_masks()
        qs_ = _stack_heads(q_ref[...] * SCALE, masks)
        dos = _stack_heads(do_ref[...].astype(BF16), masks)
        lts = jnp.concatenate([lt_ref[h] for h in range(PAIR)], axis=0)
        row, cix = _stacked_iotas(tq, bk)
        tri_le = _tri(bk, "le")
        tri_lt = _tri(bk, "lt")

        def step(masked):
            def go(kb, carry):
                dq, cl, ce = carry
                ks_ = pl.multiple_of(kb * bk, bk)
                kv = k_ref[pl.ds(ks_, bk), :]
                vv = v_ref[pl.ds(ks_, bk), :]
                strict = (kb * bk + cix) < (qi * tq + row)
                z = lax.dot_general(qs_, kv, (NT, ((), ())), preferred_element_type=F32)
                sp = _softplus(z)
                lk = jnp.where(strict, -sp, 0.0) if masked else -sp
                sig = jnp.exp(z - sp)
                wgt = sig * jnp.exp(lts - cl - _cumsum_mm(lk, tri_le))
                if masked:
                    wgt = jnp.where(strict, wgt, 0.0)
                dw = lax.dot_general(dos, vv, (NT, ((), ())), preferred_element_type=F32)
                e = dw * wgt
                pre = ce + _cumsum_mm(e, tri_lt)
                dz = e * (1.0 - sig) - pre * sig
                if masked:
                    dz = jnp.where(strict, dz, 0.0)
                dzb = dz.astype(BF16)
                dq = dq + _unstack_heads(jnp.dot(dzb, kv * SCALE, preferred_element_type=F32), masks)
                dk_acc[pl.ds(ks_, bk), :] += lax.dot_general(dzb, qs_, (TN, ((), ())), preferred_element_type=F32)
                dv_acc[pl.ds(ks_, bk), :] += lax.dot_general(wgt.astype(BF16), dos, (TN, ((), ())),
                                                             preferred_element_type=F32)
                return dq, cl + jnp.sum(lk, axis=1, keepdims=True), ce + jnp.sum(e, axis=1, keepdims=True)
            return go

        n_full, n_all = _key_blocks(qi, tq, bk)
        zero = jnp.zeros((PAIR * tq, 1), F32)
        carry = lax.fori_loop(0, n_full, step(False), (jnp.zeros((tq, LANE), F32), zero, zero))
        dq, _, _ = lax.fori_loop(n_full, n_all, step(True), carry)
        dq_ref[...] = dq.astype(dq_ref.dtype)

        @pl.when(qi == nq - 1)
        def _():
            dk_ref[...] = dk_acc[...].astype(dk_ref.dtype)
            dv_ref[...] = dv_acc[...].astype(dv_ref.dtype)

    o = jax.ShapeDtypeStruct((t_rows, width), BF16)
    return _ride_call(
        name, body, (bl, nb, nq), [qs(off), ks(off + nb), ks(off + 2 * nb), col, qs(0)], [qs(0), ks(0), ks(0)],
        [o, o, o], [pltpu.VMEM((seq, LANE), F32), pltpu.VMEM((seq, LANE), F32)], (qkv, qkv, qkv, ltot, do), riders)


def _fox_fwd(name, qkv, off, width, bl, seq, cum_q, cum_k):
    t_rows = qkv.shape[0]
    tq, bk = _att_blocks(seq, FOX_K_BLOCK)
    nq = seq // tq
    nb = width // LANE
    qs, ks, col, lane = _att_specs(seq, tq, nq, off)

    def body(q_ref, k_ref, v_ref, cq_ref, ck_ref, ob_ref, of_ref, lse_ref):
        qi = pl.program_id(2)
        masks = _head_masks()
        qs_ = _stack_heads(q_ref[...] * SCALE, masks)
        cqs = jnp.concatenate([cq_ref[h] for h in range(PAIR)], axis=0)
        row, cix = _stacked_iotas(tq, bk)

        def step(masked):
            def go(kb, carry):
                m, lsum, acc = carry
                ks_ = pl.multiple_of(kb * bk, bk)
                kv = k_ref[pl.ds(ks_, bk), :]
                vv = v_ref[pl.ds(ks_, bk), :]
                bias = jnp.concatenate([cqs[h * tq:(h + 1) * tq] - ck_ref[h, :, pl.ds(ks_, bk)] for h in range(PAIR)],
                                       axis=0)
                z = lax.dot_general(qs_, kv, (NT, ((), ())), preferred_element_type=F32) + bias
                if masked:
                    z = jnp.where((kb * bk + cix) <= (qi * tq + row), z, NEG)
                m_new = jnp.maximum(m, jnp.max(z, axis=1, keepdims=True))
                pv = jnp.exp(z - m_new)
                alpha = jnp.exp(m - m_new)
                lsum = alpha * lsum + jnp.sum(pv, axis=1, keepdims=True)
                acc = alpha * acc + jnp.dot(pv.astype(BF16), vv, preferred_element_type=F32)
                return m_new, lsum, acc
            return go

        n_full, n_all = _key_blocks(qi, tq, bk)
        init = (jnp.full((PAIR * tq, 1), NEG, F32), jnp.zeros((PAIR * tq, 1), F32),
                jnp.zeros((PAIR * tq, LANE), F32))
        carry = lax.fori_loop(0, n_full, step(False), init)
        m, lsum, acc = lax.fori_loop(n_full, n_all, step(True), carry)
        out = _unstack_heads(acc / lsum, masks)
        ob_ref[...] = out.astype(ob_ref.dtype)
        of_ref[...] = out
        lse = m + jnp.log(lsum)
        for h in range(PAIR):
            lse_ref[h] = lse[h * tq:(h + 1) * tq]

    return pl.pallas_call(
        body, name=name, grid=(bl, nb, nq),
        in_specs=[qs(off), ks(off + nb), ks(off + 2 * nb), col, lane], out_specs=[qs(0), qs(0), col],
        out_shape=[jax.ShapeDtypeStruct((t_rows, width), BF16), jax.ShapeDtypeStruct((t_rows, width), F32),
                   jax.ShapeDtypeStruct((bl, nb * PAIR, seq, 1), F32)],
        compiler_params=_params(),
    )(qkv, qkv, qkv, cum_q, cum_k)


def _fox_bwd(name, qkv, off, width, bl, seq, cum_q, cum_k, lse, o, do, riders=()):
    t_rows = qkv.shape[0]
    tq, bk = _att_blocks(seq, FOX_K_BLOCK)
    nq = seq // tq
    nb = width // LANE
    qs, ks, col, lane = _att_specs(seq, tq, nq, off)

    def body(q_ref, k_ref, v_ref, cq_ref, ck_ref, lse_ref, o_ref, do_ref,
             dq_ref, dk_ref, dv_ref, dcq_ref, dck_ref, dk_acc, dv_acc):
        qi = pl.program_id(2)

        @pl.when(qi == 0)
        def _():
            dk_acc[...] = jnp.zeros_like(dk_acc)
            dv_acc[...] = jnp.zeros_like(dv_acc)
            dck_ref[...] = jnp.zeros_like(dck_ref)

        masks = _head_masks()
        qs_ = _stack_heads(q_ref[...] * SCALE, masks)
        dof = do_ref[...]
        dos = _stack_heads(dof.astype(BF16), masks)
        prod = dof * o_ref[...]
        delta = jnp.concatenate([jnp.sum(_only(m, prod), axis=1, keepdims=True) for m in masks], axis=0)
        shift = jnp.concatenate([cq_ref[h] - lse_ref[h] for h in range(PAIR)], axis=0)
        row, cix = _stacked_iotas(tq, bk)

        def step(masked):
            def go(kb, carry):
                dq, dcq = carry
                ks_ = pl.multiple_of(kb * bk, bk)
                kv = k_ref[pl.ds(ks_, bk), :]
                vv = v_ref[pl.ds(ks_, bk), :]
                bias = jnp.concatenate(
                    [shift[h * tq:(h + 1) * tq] - ck_ref[h, :, pl.ds(ks_, bk)] for h in range(PAIR)], axis=0)
                pv = jnp.exp(lax.dot_general(qs_, kv, (NT, ((), ())), preferred_element_type=F32) + bias)
                if masked:
                    pv = jnp.where((kb * bk + cix) <= (qi * tq + row), pv, 0.0)
                dp = lax.dot_general(dos, vv, (NT, ((), ())), preferred_element_type=F32)
                ds = pv * (dp - delta)
                dsb = ds.astype(BF16)
                dq = dq + _unstack_heads(jnp.dot(dsb, kv * SCALE, preferred_element_type=F32), masks)
                dk_acc[pl.ds(ks_, bk), :] += lax.dot_general(dsb, qs_, (TN, ((), ())), preferred_element_type=F32)
                dv_acc[pl.ds(ks_, bk), :] += lax.dot_general(pv.astype(BF16), dos, (TN, ((), ())),
                                                             preferred_element_type=F32)
                for h in range(PAIR):
                    dck_ref[h, :, pl.ds(ks_, bk)] += -jnp.sum(ds[h * tq:(h + 1) * tq], axis=0, keepdims=True)
                return dq, dcq + jnp.sum(ds, axis=1, keepdims=True)
            return go

        n_full, n_all = _key_blocks(qi, tq, bk)
        carry = lax.fori_loop(0, n_full, step(False), (jnp.zeros((tq, LANE), F32), jnp.zeros((PAIR * tq, 1), F32)))
        dq, dcq = lax.fori_loop(n_full, n_all, step(True), carry)
        dq_ref[...] = dq.astype(dq_ref.dtype)
        for h in range(PAIR):
            dcq_ref[h] = dcq[h * tq:(h + 1) * tq]

        @pl.when(qi == nq - 1)
        def _():
            dk_ref[...] = dk_acc[...].astype(dk_ref.dtype)
            dv_ref[...] = dv_acc[...].astype(dv_ref.dtype)

    ob = jax.ShapeDtypeStruct((t_rows, width), BF16)
    nh = nb * PAIR
    return _ride_call(
        name, body, (bl, nb, nq),
        [qs(off), ks(off + nb), ks(off + 2 * nb), col, lane, col, qs(0), qs(0)],
        [qs(0), ks(0), ks(0), col, lane],
        [ob, ob, ob, jax.ShapeDtypeStruct((bl, nh, seq, 1), F32), jax.ShapeDtypeStruct((bl, nh, 1, seq), F32)],
        [pltpu.VMEM((seq, LANE), F32), pltpu.VMEM((seq, LANE), F32)],
        (qkv, qkv, qkv, cum_q, cum_k, lse, o, do), riders)


def _lane_cumsum(v, reverse):
    n = v.shape[1]
    cix = lax.broadcasted_iota(jnp.int32, v.shape, 1)
    s = 1
    while s < n:
        if reverse:
            v = v + jnp.where(cix < n - s, pltpu.roll(v, n - s, 1), 0.0)
        else:
            v = v + jnp.where(cix >= s, pltpu.roll(v, s, 1), 0.0)
        s *= 2
    return v


def _forget_cum(name, fl, bf):
    def body(fl_ref, bf_ref, o_ref):
        xv = fl_ref[...] + bf_ref[...]
        o_ref[...] = _lane_cumsum(-_softplus(-xv), False)

    return pl.pallas_call(body, name=name, out_shape=jax.ShapeDtypeStruct(fl.shape, F32),
                          compiler_params=_params())(fl, bf)


def _forget_cum_bwd(name, fl, bf, dcum, nh):
    rows = fl.shape[0]

    def body(fl_ref, bf_ref, dc_ref, dfl_ref, dbf_ref):
        xv = fl_ref[...] + bf_ref[...]
        dlogf = _lane_cumsum(dc_ref[...], True)
        dfl = dlogf * _sigmoid(-xv)
        dfl_ref[...] = dfl
        per_row = jnp.sum(dfl, axis=1, keepdims=True)
        tot = per_row[0:nh]
        for b in range(1, rows // nh):
            tot = tot + per_row[b * nh:(b + 1) * nh]
        dbf_ref[...] = tot

    return pl.pallas_call(
        body, name=name,
        out_shape=[jax.ShapeDtypeStruct(fl.shape, F32), jax.ShapeDtypeStruct((nh, 1), F32)],
        compiler_params=_params(),
    )(fl, bf, dcum)


def _merge_fwd(name, proj, off, merge_b, pa, pb, pc):
    t_rows, d = pa.shape
    tm = _tile(t_rows, 256)

    def body(l0, l1, l2, mb, a_ref, b_ref, c_ref, o_ref):
        g0 = _sigmoid(l0[...] + mb[:, 0:d])
        g1 = _sigmoid(l1[...] + mb[:, d:2 * d])
        g2 = _sigmoid(l2[...] + mb[:, 2 * d:3 * d])
        o_ref[...] = (g0 * a_ref[...] + g1 * b_ref[...] + g2 * c_ref[...]).astype(o_ref.dtype)

    row = pl.BlockSpec((tm, d), lambda i: (i, 0))
    lg = lambda j: pl.BlockSpec((tm, d), lambda i: (i, off + j))
    return pl.pallas_call(
        body, name=name, grid=(t_rows // tm,),
        in_specs=[lg(0), lg(1), lg(2), pl.BlockSpec((1, 3 * d), lambda i: (0, 0)), row, row, row],
        out_specs=row, out_shape=jax.ShapeDtypeStruct((t_rows, d), BF16), compiler_params=_params(),
    )(proj, proj, proj, merge_b, pa, pb, pc)


def _merge_bwd(name, proj, off, merge_b, pa, pb, pc, dmixed):
    t_rows, d = pa.shape
    tm = _tile(t_rows, 256)

    def body(l0, l1, l2, mb, a_ref, b_ref, c_ref, dm_ref, da_ref, db_ref, dc_ref, dl_ref, dmb_ref):
        i = pl.program_id(0)
        dm = dm_ref[...]
        parts = []
        for j, (lref, pref, dref) in enumerate(((l0, a_ref, da_ref), (l1, b_ref, db_ref), (l2, c_ref, dc_ref))):
            g = _sigmoid(lref[...] + mb[:, j * d:(j + 1) * d])
            dref[...] = (g * dm).astype(dref.dtype)
            dl = dm * pref[...] * g * (1.0 - g)
            dl_ref[:, j * d:(j + 1) * d] = dl.astype(dl_ref.dtype)
            parts.append(jnp.sum(dl, axis=0, keepdims=True))
        tot = jnp.concatenate(parts, axis=1)

        @pl.when(i == 0)
        def _():
            dmb_ref[...] = tot

        @pl.when(i > 0)
        def _():
            dmb_ref[...] += tot

    row = pl.BlockSpec((tm, d), lambda i: (i, 0))
    lg = lambda j: pl.BlockSpec((tm, d), lambda i: (i, off + j))
    one = pl.BlockSpec((1, 3 * d), lambda i: (0, 0))
    b16 = jax.ShapeDtypeStruct((t_rows, d), BF16)
    return pl.pallas_call(
        body, name=name, grid=(t_rows // tm,),
        in_specs=[lg(0), lg(1), lg(2), one, row, row, row, row],
        out_specs=[row, row, row, pl.BlockSpec((tm, 3 * d), lambda i: (i, 0)), one],
        out_shape=[b16, b16, b16, jax.ShapeDtypeStruct((t_rows, 3 * d), BF16), jax.ShapeDtypeStruct((1, 3 * d), F32)],
        compiler_params=_params(),
    )(proj, proj, proj, merge_b, pa, pb, pc, dmixed)


def _grouped_nn(name, a, wg, l, out_dtype):
    t_rows, kk = a.shape
    ng, ncol = wg.shape[0], wg.shape[3]
    tm = _tile(t_rows, 512)
    return _mmk(name, [a, wg],
                [pl.BlockSpec((tm, kk), lambda i, g: (i, 0)),
                 pl.BlockSpec((None, None, kk, ncol), lambda i, g: (g, l, 0, 0))],
                [(0, ALL, 1, ALL, NN)], (t_rows, ng * ncol), out_dtype, (t_rows // tm, ng),
                pl.BlockSpec((tm, ncol), lambda i, g: (i, g)))


def _grouped_nt(name, da, wg, l, out_dtype):
    t_rows = da.shape[0]
    ng, kk, ncol = wg.shape[0], wg.shape[2], wg.shape[3]
    tm = _tile(t_rows, 512)
    return _mmk(name, [da, wg],
                [pl.BlockSpec((tm, ng * ncol), lambda i: (i, 0)),
                 pl.BlockSpec((ng, None, kk, ncol), lambda i: (0, l, 0, 0))],
                [(0, (ALL, slice(g * ncol, (g + 1) * ncol)), 1, g, NT) for g in range(ng)],
                (t_rows, kk), out_dtype, (t_rows // tm,), pl.BlockSpec((tm, kk), lambda i: (i, 0)))


def _grouped_tn(name, a, da, ng, out_dtype):
    t_rows, kk = a.shape
    ncol = da.shape[1] // ng
    return _mmk(name, [a, da],
                [pl.BlockSpec((t_rows, kk), lambda g: (0, 0)), pl.BlockSpec((t_rows, ncol), lambda g: (0, g))],
                [(0, ALL, 1, ALL, TN)], (ng, kk, ncol), out_dtype, (ng,),
                pl.BlockSpec((None, kk, ncol), lambda g: (g, 0, 0)))


def _rows_nn(name, a, wr, l, out_dtype):
    t_rows = a.shape[0]
    ng, kg, n = wr.shape[0], wr.shape[2], wr.shape[3]
    tm, tn = _tile(t_rows, 512), _tile(n, 512)
    return _mmk(name, [a, wr],
                [pl.BlockSpec((tm, ng * kg), lambda i, j: (i, 0)),
                 pl.BlockSpec((ng, None, kg, tn), lambda i, j: (0, l, 0, j))],
                [(0, (ALL, slice(g * kg, (g + 1) * kg)), 1, g, NN) for g in range(ng)],
                (t_rows, n), out_dtype, (t_rows // tm, n // tn), pl.BlockSpec((tm, tn), lambda i, j: (i, j)))


def _rows_nt(name, dy, wr, l, out_dtype):
    t_rows, n = dy.shape
    ng, kg = wr.shape[0], wr.shape[2]
    tm = _tile(t_rows, 512)
    return _mmk(name, [dy, wr],
                [pl.BlockSpec((tm, n), lambda i, g: (i, 0)),
                 pl.BlockSpec((None, None, kg, n), lambda i, g: (g, l, 0, 0))],
                [(0, ALL, 1, ALL, NT)], (t_rows, ng * kg), out_dtype, (t_rows // tm, ng),
                pl.BlockSpec((tm, kg), lambda i, g: (i, g)))


def _rows_tn(name, a, dy, ng, out_dtype):
    t_rows, n = dy.shape
    kg = a.shape[1] // ng
    tn = _tile(n, 512)
    return _mmk(name, [a, dy],
                [pl.BlockSpec((t_rows, kg), lambda g, j: (0, g)), pl.BlockSpec((t_rows, tn), lambda g, j: (0, j))],
                [(0, ALL, 1, ALL, TN)], (ng, kg, n), out_dtype, (ng, n // tn),
                pl.BlockSpec((None, kg, tn), lambda g, j: (g, 0, j)))


def _mix_fwd(tag, w, l, x, mod, seq):
    t_rows, d = x.shape
    bl = t_rows // seq
    shift, scale, gate = mod
    chans, nh = w["layout"]["chans"], w["layout"]["heads"]
    width = nh * HEAD_DIM
    nb = width // LANE
    h = _normmod(tag + "_norm", x, w["mix_norm"][l][None], shift, scale, seq)
    proj = _mm(tag + "_in_a", h, w["w_a"][l], NN, F32)
    qkv = _mm(tag + "_in_b", h, w["w_b"][l], NN, BF16)
    flp = _mm(tag + "_in_f", h, w["w_f"][l], NN, F32)
    xa, r, ig, hs, ya = _rg_fwd(tag + "_rg", proj, w["rg"][l], seq, chans)
    yb, ltot = _sb_fwd(tag + "_sb", qkv, 0, width, bl, seq)
    fl = flp[:, :nh].reshape(bl, seq, nh).transpose(0, 2, 1).reshape(bl * nh, seq)
    bf = jnp.tile(w["fox_bf"][l].reshape(nh, 1), (bl, 1))
    cum = _forget_cum(tag + "_cum", fl, bf)
    cum_q = cum.reshape(bl, nh, seq, 1)
    cum_k = cum.reshape(bl, nh, 1, seq)
    yc, oc, lse = _fox_fwd(tag + "_fox", qkv, 3 * nb, width, bl, seq, cum_q, cum_k)
    pa = _rows_nn(tag + "_prg", ya, w["w_rg"], l, F32)
    pb = _grouped_nn(tag + "_psb", yb, w["w_sb"], l, F32)
    pc = _grouped_nn(tag + "_pfox", yc, w["w_fox"], l, F32)
    moff = 2 * chans // d
    mb = w["merge_b"][l][None]
    mixed = _merge_fwd(tag + "_merge", proj, moff, mb, pa, pb, pc)
    y = _rows_nn(tag + "_out", mixed, w["w_o"], l, F32)
    xn = _resid(tag + "_res", x, y, gate, 1.0, seq)
    saved = dict(x=x, h=h, proj=proj, qkv=qkv, rg=(xa, r, ig, hs), ya=ya, ltot=ltot,
                 fox=(cum_q, cum_k, lse, oc), fl=fl, bf=bf, yb=yb, yc=yc, pa=pa, pb=pb, pc=pc, mixed=mixed, y=y)
    return xn, saved


def _mix_bwd(tag, w, l, s, mod, dxo, seq, riders=(), riders_fox=()):
    x = s["x"]
    t_rows, d = x.shape
    bl = t_rows // seq
    shift, scale, gate = mod
    chans, nh = w["layout"]["chans"], w["layout"]["heads"]
    width = nh * HEAD_DIM
    nb = width // LANE
    moff = 2 * chans // d
    mb = w["merge_b"][l][None]
    ng = NUM_CHIPS
    dy, dgate = _resid_bwd(tag + "_res_bwd", dxo, s["y"], gate, 1.0, seq)
    dmixed = _rows_nt(tag + "_out_dx", dy, w["w_o"], l, F32)
    dw_o = _rows_tn(tag + "_out_dw", s["mixed"], dy, ng, BF16)
    dpa, dpb, dpc, dlog, dmb = _merge_bwd(tag + "_merge_bwd", s["proj"], moff, mb, s["pa"], s["pb"], s["pc"], dmixed)
    dya = _rows_nt(tag + "_prg_dx", dpa, w["w_rg"], l, F32)
    dw_rg = _rows_tn(tag + "_prg_dw", s["ya"], dpa, ng, BF16)
    dyb = _grouped_nt(tag + "_psb_dx", dpb, w["w_sb"], l, F32)
    dw_sb = _grouped_tn(tag + "_psb_dw", s["yb"], dpb, ng, BF16)
    dyc = _grouped_nt(tag + "_pfox_dx", dpc, w["w_fox"], l, F32)
    dw_fox = _grouped_tn(tag + "_pfox_dw", s["yc"], dpc, ng, BF16)
    qkv = s["qkv"]
    dq_b, dk_b, dv_b, *slots = _sb_bwd(tag + "_sb_bwd", qkv, 0, width, bl, seq, s["ltot"], dyb, riders)
    cum_q, cum_k, lse, oc = s["fox"]
    dq_c, dk_c, dv_c, dcq, dck, *slots_fox = _fox_bwd(tag + "_fox_bwd", qkv, 3 * nb, width, bl, seq, cum_q, cum_k,
                                                      lse, oc, dyc, riders_fox)
    dcum = dcq.reshape(bl * nh, seq) + dck.reshape(bl * nh, seq)
    dfl, dbf = _forget_cum_bwd(tag + "_cum_bwd", s["fl"], s["bf"], dcum, nh)
    dfl_t = dfl.reshape(bl, nh, seq).transpose(0, 2, 1).reshape(t_rows, nh)
    dflp = jnp.pad(dfl_t, ((0, 0), (0, LANE - nh))).astype(BF16)
    drgx, dgt, rg_small, dwa, dwx = _rg_bwd(tag + "_rg_bwd", s["proj"], dya, s["rg"], w["rg"][l], seq, chans)
    dproj = jnp.concatenate([drgx, dgt, dlog], axis=1)
    dqkv = jnp.concatenate([dq_b, dk_b, dv_b, dq_c, dk_c, dv_c], axis=1)
    w_a, w_b, w_f = w["w_a"][l], w["w_b"][l], w["w_f"][l]
    pa_w, pb_w = w_a.shape[1], w_b.shape[1]
    tm, tn = _tile(t_rows, 512), _tile(d, 512)
    rows = lambda n: pl.BlockSpec((tm, n), lambda i, j: (i, 0))
    wrow = lambda n: pl.BlockSpec((tn, n), lambda i, j: (j, 0))
    dh = _mmk(tag + "_in_dx", [dproj, dqkv, dflp, w_a, w_b, w_f],
              [rows(pa_w), rows(pb_w), rows(LANE), wrow(pa_w), wrow(pb_w), wrow(LANE)],
              [(0, ALL, 3, ALL, NT), (1, ALL, 4, ALL, NT), (2, ALL, 5, ALL, NT)],
              (t_rows, d), F32, (t_rows // tm, d // tn), pl.BlockSpec((tm, tn), lambda i, j: (i, j)))
    hb = s["h"]
    dw_a = _mm(tag + "_in_a_dw", hb, dproj, TN, BF16)
    dw_b = _mm(tag + "_in_b_dw", hb, dqkv, TN, BF16)
    dw_f = _mm(tag + "_in_f_dw", hb, dflp, TN, BF16)
    dx, dshift, dscale, dgain = _normmod_bwd(tag + "_norm_bwd", x, dh, dxo, w["mix_norm"][l][None], scale, seq)
    grads = dict(w_in=(dw_a, dw_b, dw_f), w_rg=dw_rg, w_sb=dw_sb, w_fox=dw_fox, w_o=dw_o, mix_norm=dgain,
                 rg_small=rg_small, rg_dwa=dwa, rg_dwx=dwx, fox_bf=dbf, merge_b=dmb)
    return dx, (dshift, dscale, dgate), grads, slots, slots_fox


def _silu(name, c):
    def body(c_ref, o_ref):
        v = c_ref[...]
        o_ref[...] = v * _sigmoid(v)

    return pl.pallas_call(body, name=name, out_shape=jax.ShapeDtypeStruct(c.shape, F32),
                          compiler_params=_params())(c)


def _blockdiag(wb):
    nb, bd, _ = wb.shape
    per = LANE // bd
    t = wb.reshape(nb // per, per, bd, 1, bd)
    eye = jnp.eye(per, dtype=wb.dtype).reshape(1, per, 1, per, 1)
    return (t * eye).reshape(nb // per, LANE, LANE).astype(BF16)


def _unblockdiag(t, bd):
    n = t.shape[0]
    per = LANE // bd
    t5 = t.reshape(n, per, bd, per, bd)
    return jnp.stack([t5[:, p, :, p, :] for p in range(per)], axis=1).reshape(n * per, bd, bd)


def _prepare(gw, a, d, chans, nh):
    depth = a["ada_b"].shape[0]
    wq = 3 * nh * HEAD_DIM
    o_m = 2 * chans + 2 * wq
    w = {"layout": dict(chans=chans, heads=nh)}
    for n in ("ffn1_w1", "ffn1_w3", "ffn1_w2", "ffn2_w1", "ffn2_w3", "ffn2_w2", "w_rg", "w_sb", "w_fox", "w_o"):
        w[n] = gw[n]
    for n in ("ffn1_norm", "ffn2_norm", "mix_norm", "fox_bf", "merge_b", "final_norm"):
        w[n] = a[n]
    w_a, w_b, w_f, rg = [], [], [], []
    for l in range(depth):
        full = gw["w_in"][:, l].transpose(1, 0, 2).reshape(d, -1)
        w_a.append(jnp.concatenate([full[:, :2 * chans], full[:, o_m + nh:]], axis=1))
        w_b.append(full[:, 2 * chans:o_m])
        w_f.append(jnp.pad(full[:, o_m:o_m + nh], ((0, 0), (0, LANE - nh))))
        conv_w = gw["conv_w"][:, l].transpose(1, 0, 2).reshape(-1, chans)
        rg.append(dict(conv_w=conv_w, conv_b=a["conv_b"][l][None], ba=a["rg_ba"][l][None], bx=a["rg_bx"][l][None],
                       lam=a["rg_lam"][l][None], wa=_blockdiag(a["rg_wa"][l]), wx=_blockdiag(a["rg_wx"][l])))
    w["w_a"], w["w_b"], w["w_f"], w["rg"] = w_a, w_b, w_f, rg
    return w


def _local_step(w, x, tgt, mods, fm, ride=None):
    bl, seq, d = x.shape
    t_rows = bl * seq
    depth = len(mods)
    mod3 = []
    for l in range(depth):
        m4 = mods[l].reshape(bl, 9, 1, d)
        mod3.append([(m4[:, 3 * k], m4[:, 3 * k + 1], m4[:, 3 * k + 2]) for k in range(3)])
    fm4 = fm.reshape(bl, 2, 1, d)
    saved = []
    xc = x.reshape(t_rows, d)
    for l in range(depth):
        xc, s1 = _ffn_fwd(f"l{l}_ffn1", w, l, "ffn1_", xc, mod3[l][0], seq)
        xc, s2 = _mix_fwd(f"l{l}_mix", w, l, xc, mod3[l][1], seq)
        xc, s3 = _ffn_fwd(f"l{l}_ffn2", w, l, "ffn2_", xc, mod3[l][2], seq)
        saved.append((s1, s2, s3))
    lpart, dx, dfs, dfc, dfg = _final_loss("final", xc, tgt.reshape(t_rows, d), w["final_norm"][None],
                                           fm4[:, 0], fm4[:, 1], seq)
    loss = jnp.sum(lpart[:, 0, 0])
    grads = {"final_norm": dfg, "layers": [None] * depth}
    dmods = [None] * depth
    parts, slots = {}, {}
    queue = {"sb": [], "fox": [], "dn": [], "up": []}

    def take(host):
        keys, queue[host] = queue[host], []
        return keys, [parts[k] for k in keys]

    def enqueue(l, names, gl, tag, hosts):
        for n, p in ride(names, gl, tag).items():
            parts[(l, n)] = p
            queue[hosts(n)].append((l, n))

    for l in reversed(range(depth)):
        s1, s2, s3 = saved[l]
        dx, dm3, g3, _, _ = _ffn_bwd(f"l{l}_ffn2", w, l, "ffn2_", s3, mod3[l][2], dx, seq)
        if ride is not None and l == 0:
            enqueue(l, FFN2, g3, "l0_ffn2", lambda n: "sb")
        (k_sb, r_sb), (k_fox, r_fox) = take("sb"), take("fox")
        dx, dm2, g2, s_sb, s_fox = _mix_bwd(f"l{l}_mix", w, l, s2, mod3[l][1], dx, seq, r_sb, r_fox)
        slots.update(zip(k_sb + k_fox, list(s_sb) + list(s_fox)))
        if ride is not None and l == 0:
            enqueue(l, MIXER, g2, "l0_mix", lambda n: "up" if n == "w_in" else "dn")
        (k_dn, r_dn), (k_up, r_up) = take("dn"), take("up")
        dx, dm1, g1, s_dn, s_up = _ffn_bwd(f"l{l}_ffn1", w, l, "ffn1_", s1, mod3[l][0], dx, seq, r_dn, r_up)
        slots.update(zip(k_dn + k_up, list(s_dn) + list(s_up)))
        dmods[l] = jnp.concatenate([*dm1, *dm2, *dm3], axis=1).reshape(bl, 9 * d)
        grads["layers"][l] = {**g1, **g2, **g3}
        if ride is not None and l > 0:
            enqueue(l, DENSE, grads["layers"][l], f"l{l}", lambda n: "fox" if n == "w_in" else "sb")
    dfm = jnp.concatenate([dfs, dfc], axis=1).reshape(bl, 2 * d)
    return loss, dx.reshape(bl, seq, d), grads, dmods, dfm, (parts, slots)


def _mesh_pos():
    return lax.axis_index("x"), lax.axis_index("y"), lax.axis_index("c")


def _other_chips(x, y):
    return ((1 - x, y), (x, 1 - y), (1 - x, 1 - y))


def _gather_two_level(name, arrs):
    n = len(arrs)

    def body(*refs):
        ins, outs = refs[:n], refs[n:2 * n]
        send, recv, send2, recv2, send3, recv3 = refs[2 * n:]
        x, y, c = _mesh_pos()
        me = 2 * x + y
        chips = _other_chips(x, y)
        sib = (x, y, 1 - c)
        own = [pltpu.make_async_remote_copy(
            src_ref=ins[i], dst_ref=outs[i].at[me], send_sem=send3.at[i], recv_sem=recv3.at[i],
            device_id=sib, device_id_type=MESH) for i in range(n)]
        first = []
        for j, (px, py) in enumerate(chips):
            for i in range(n):
                first.append(pltpu.make_async_remote_copy(
                    src_ref=ins[i].at[c], dst_ref=outs[i].at[me, c], send_sem=send.at[j * n + i],
                    recv_sem=recv.at[j * n + i], device_id=(px, py, c), device_id_type=MESH))
        for cp in first + own:
            cp.start()
        passed = []
        for j, (px, py) in enumerate(chips):
            for i in range(n):
                landed = outs[i].at[2 * px + py, c]
                pltpu.make_async_remote_copy(
                    src_ref=ins[i].at[c], dst_ref=landed, send_sem=send.at[j * n + i],
                    recv_sem=recv.at[j * n + i], device_id=(px, py, c), device_id_type=MESH).wait_recv()
                fwd = pltpu.make_async_remote_copy(
                    src_ref=landed, dst_ref=landed, send_sem=send2.at[j * n + i],
                    recv_sem=recv2.at[j * n + i], device_id=sib, device_id_type=MESH)
                fwd.start()
                passed.append(fwd)
        for j, (px, py) in enumerate(chips):
            for i in range(n):
                theirs = outs[i].at[2 * px + py, 1 - c]
                pltpu.make_async_remote_copy(
                    src_ref=theirs, dst_ref=theirs, send_sem=send2.at[j * n + i],
                    recv_sem=recv2.at[j * n + i], device_id=sib, device_id_type=MESH).wait_recv()
        for cp in first + passed:
            cp.wait_send()
        for cp in own:
            cp.wait()

    return pl.pallas_call(
        body, name=name, in_specs=[ANY] * n, out_specs=[ANY] * n,
        out_shape=[jax.ShapeDtypeStruct((NUM_CHIPS,) + a.shape, a.dtype) for a in arrs],
        scratch_shapes=[pltpu.SemaphoreType.DMA((3 * n,)), pltpu.SemaphoreType.DMA((3 * n,)),
                        pltpu.SemaphoreType.DMA((3 * n,)), pltpu.SemaphoreType.DMA((3 * n,)),
                        pltpu.SemaphoreType.DMA((n,)), pltpu.SemaphoreType.DMA((n,))],
    )(*arrs)


def _split_to_sibling(name, arrs):
    n = len(arrs)
    slabs = arrs[0].shape[0]

    def body(*refs):
        ins, theirs = refs[:n], refs[n:2 * n]
        send, recv = refs[2 * n:]
        x, y, c = _mesh_pos()
        sib = (x, y, 1 - c)
        for i in range(n):
            for s in range(slabs):
                pltpu.make_async_remote_copy(
                    src_ref=ins[i].at[s, 1 - c], dst_ref=theirs[i].at[s], send_sem=send.at[i],
                    recv_sem=recv.at[i], device_id=sib, device_id_type=MESH).start()
        for i in range(n):
            pltpu.make_async_remote_copy(
                src_ref=ins[i].at[:, 0], dst_ref=theirs[i], send_sem=send.at[i], recv_sem=recv.at[i],
                device_id=sib, device_id_type=MESH).wait()

    return pl.pallas_call(
        body, name=name, in_specs=[ANY] * n, out_specs=[ANY] * n,
        out_shape=[jax.ShapeDtypeStruct((a.shape[0],) + a.shape[2:], a.dtype) for a in arrs],
        scratch_shapes=[pltpu.SemaphoreType.DMA((n,)), pltpu.SemaphoreType.DMA((n,))],
    )(*arrs)


def _scatter_chips(name, arrs):
    n = len(arrs)

    def body(*refs):
        ins, outs = refs[:n], refs[n:2 * n]
        send, recv = refs[2 * n:]
        x, y, c = _mesh_pos()
        chips = _other_chips(x, y)
        sends = []
        for j, (px, py) in enumerate(chips):
            for i in range(n):
                sends.append(pltpu.make_async_remote_copy(
                    src_ref=ins[i].at[2 * px + py], dst_ref=outs[i].at[j], send_sem=send.at[j * n + i],
                    recv_sem=recv.at[j * n + i], device_id=(px, py, c), device_id_type=MESH))
        for s in sends:
            s.start()
        for s in sends:
            s.wait()

    return pl.pallas_call(
        body, name=name, in_specs=[ANY] * n, out_specs=[ANY] * n,
        out_shape=[jax.ShapeDtypeStruct((NUM_CHIPS - 1,) + a.shape[1:], a.dtype) for a in arrs],
        scratch_shapes=[pltpu.SemaphoreType.DMA((3 * n,)), pltpu.SemaphoreType.DMA((3 * n,))],
    )(*arrs)


def _join_halves(name, arrs):
    n = len(arrs)

    def body(*refs):
        ins, outs = refs[:n], refs[n:2 * n]
        send, recv = refs[2 * n:]
        x, y, c = _mesh_pos()
        copies = [pltpu.make_async_remote_copy(
            src_ref=ins[i], dst_ref=outs[i], send_sem=send.at[i], recv_sem=recv.at[i],
            device_id=(x, y, 1 - c), device_id_type=MESH) for i in range(n)]
        for cp in copies:
            cp.start()
        for cp in copies:
            cp.wait()

    return pl.pallas_call(
        body, name=name, in_specs=[ANY] * n, out_specs=[ANY] * n,
        out_shape=[jax.ShapeDtypeStruct(a.shape, a.dtype) for a in arrs],
        scratch_shapes=[pltpu.SemaphoreType.DMA((n,)), pltpu.SemaphoreType.DMA((n,))],
    )(*arrs)


def _gather_all(name, pack, own=True):
    def body(in_ref, out_ref, send, recv, loc):
        x, y, c = _mesh_pos()
        me = 4 * x + 2 * y + c
        mine = pltpu.make_async_copy(in_ref, out_ref.at[me], loc)
        if own:
            mine.start()
        peers = []
        for mask in range(1, NUM_DEVICES):
            px = 1 - x if mask & 4 else x
            py = 1 - y if mask & 2 else y
            pc = 1 - c if mask & 1 else c
            peers.append((px, py, pc))
        sends = [pltpu.make_async_remote_copy(
            src_ref=in_ref, dst_ref=out_ref.at[me], send_sem=send.at[k], recv_sem=recv.at[k],
            device_id=p, device_id_type=MESH) for k, p in enumerate(peers)]
        for s in sends:
            s.start()
        for k, (px, py, pc) in enumerate(peers):
            pltpu.make_async_remote_copy(
                src_ref=in_ref, dst_ref=out_ref.at[4 * px + 2 * py + pc], send_sem=send.at[k], recv_sem=recv.at[k],
                device_id=(px, py, pc), device_id_type=MESH).wait_recv()
        for s in sends:
            s.wait_send()
        if own:
            mine.wait()

    return pl.pallas_call(
        body, name=name, in_specs=[ANY], out_specs=ANY,
        out_shape=jax.ShapeDtypeStruct((NUM_DEVICES,) + pack.shape, pack.dtype),
        scratch_shapes=[pltpu.SemaphoreType.DMA((NUM_DEVICES - 1,)), pltpu.SemaphoreType.DMA((NUM_DEVICES - 1,)),
                        pltpu.SemaphoreType.DMA],
    )(pack)


def _sum_devices(name, slots, pack, dev):
    g, rows, cols = slots.shape
    tr = _rtile(rows, 256)

    def body(d_ref, s_ref, p_ref, full_ref, o_ref):
        acc = None
        for k in range(g):
            v = jnp.where(d_ref[0] == k, p_ref[...], s_ref[k])
            full_ref[k] = v
            acc = v if acc is None else acc + v
        o_ref[...] = acc

    blk = pl.BlockSpec((g, tr, cols), lambda i: (0, i, 0))
    row = pl.BlockSpec((tr, cols), lambda i: (i, 0))
    return pl.pallas_call(
        body, name=name, grid=(rows // tr,), in_specs=[SCALAR, blk, row], out_specs=[blk, row],
        out_shape=[jax.ShapeDtypeStruct(slots.shape, F32), jax.ShapeDtypeStruct((rows, cols), F32)],
        compiler_params=_params(),
    )(dev, slots, pack)


def _add_pair(name, p, q, core):
    g, _, rows, cols = p.shape
    tr = _rtile(rows, 128)

    def body(c_ref, p_ref, q_ref, o_ref):
        mine = jnp.where(c_ref[0] == 0, p_ref[:, 0].astype(F32), p_ref[:, 1].astype(F32))
        o_ref[...] = (mine + q_ref[...].astype(F32)).astype(o_ref.dtype)

    spec = pl.BlockSpec((g, tr, cols), lambda i: (0, i, 0))
    return pl.pallas_call(
        body, name=name, grid=(rows // tr,),
        in_specs=[SCALAR, pl.BlockSpec((g, 2, tr, cols), lambda i: (0, 0, i, 0)), spec],
        out_specs=spec, out_shape=jax.ShapeDtypeStruct(q.shape, BF16), compiler_params=_params(),
    )(core, p, q)


def _sum_chips(name, slots, part, chip):
    g, rows, cols = part.shape
    tr = _rtile(rows, 128)

    def body(c_ref, s_ref, p_ref, o_ref):
        acc = p_ref[c_ref[0]].astype(F32)
        for k in range(slots.shape[0]):
            acc = acc + s_ref[k].astype(F32)
        o_ref[...] = acc

    return pl.pallas_call(
        body, name=name, grid=(rows // tr,),
        in_specs=[SCALAR,
                  pl.BlockSpec((slots.shape[0], tr, cols), lambda i: (0, i, 0)),
                  pl.BlockSpec((g, tr, cols), lambda i: (0, i, 0))],
        out_specs=pl.BlockSpec((tr, cols), lambda i: (i, 0)),
        out_shape=jax.ShapeDtypeStruct((rows, cols), F32), compiler_params=_params(),
    )(chip, slots, part)


def _adamw(name, g, w, m, v, l=None):
    rows, cols = g.shape
    tr = _rtile(rows, 128)

    def body(g_ref, w_ref, m_ref, v_ref, d_o, m_o, v_o):
        gv = g_ref[...]
        mn = ADAM_B1 * m_ref[...] + (1.0 - ADAM_B1) * gv
        vn = ADAM_B2 * v_ref[...] + (1.0 - ADAM_B2) * (gv * gv)
        m_hat = mn / (1.0 - ADAM_B1 ** ADAM_STEP)
        v_hat = vn / (1.0 - ADAM_B2 ** ADAM_STEP)
        d_o[...] = -ADAM_LR * (m_hat / (jnp.sqrt(v_hat) + ADAM_EPS) + ADAM_WD * w_ref[...])
        m_o[...] = mn
        v_o[...] = vn

    gspec = pl.BlockSpec((tr, cols), lambda i: (i, 0))
    wspec = gspec if l is None else pl.BlockSpec((None, tr, cols), lambda i: (l, i, 0))
    f = jax.ShapeDtypeStruct((rows, cols), F32)
    return pl.pallas_call(
        body, name=name, grid=(rows // tr,), in_specs=[gspec] + [wspec] * 3, out_specs=[gspec] * 3,
        out_shape=[f] * 3, compiler_params=_params(),
    )(g, w, m, v)


def _adamw_layers(name, g0, g1, w, m, v):
    rows, cols = g0.shape
    tr = _rtile(rows, 128)
    nt = rows // tr

    def body(g0_ref, g1_ref, w_ref, m_ref, v_ref, g_o, d_o, m_o, v_o):
        gv = jnp.where(pl.program_id(0) == 0, g0_ref[...], g1_ref[...])
        _adamw_math(gv, w_ref, m_ref, v_ref, g_o, d_o, m_o, v_o)

    g0spec = pl.BlockSpec((tr, cols), lambda l, i: (i * (1 - l) + (nt - 1) * l, 0))
    g1spec = pl.BlockSpec((tr, cols), lambda l, i: (i * l, 0))
    wspec = pl.BlockSpec((None, tr, cols), lambda l, i: (l, i, 0))
    f = jax.ShapeDtypeStruct((2, rows, cols), F32)
    return pl.pallas_call(
        body, name=name, grid=(2, nt), in_specs=[g0spec, g1spec, wspec, wspec, wspec], out_specs=[wspec] * 4,
        out_shape=[f] * 4, compiler_params=_params(),
    )(g0, g1, w, m, v)


def _adamw_math(gv, w_ref, m_ref, v_ref, g_o, d_o, m_o, v_o):
    mn = ADAM_B1 * m_ref[...] + (1.0 - ADAM_B1) * gv
    vn = ADAM_B2 * v_ref[...] + (1.0 - ADAM_B2) * (gv * gv)
    m_hat = mn / (1.0 - ADAM_B1 ** ADAM_STEP)
    v_hat = vn / (1.0 - ADAM_B2 ** ADAM_STEP)
    g_o[...] = gv
    d_o[...] = -ADAM_LR * (m_hat / (jnp.sqrt(v_hat) + ADAM_EPS) + ADAM_WD * w_ref[...])
    m_o[...] = mn
    v_o[...] = vn


def _adamw_halves(name, mine, theirs, core, w, m, v):
    half, cols = mine[0].shape
    tr = _rtile(half, 128)
    nt = half // tr

    def body(c_ref, a0, b0, a1, b1, w_ref, m_ref, v_ref, g_o, d_o, m_o, v_o):
        first = pl.program_id(0) == 0
        own = pl.program_id(1) == c_ref[0]
        gv = jnp.where(first, jnp.where(own, a0[...], b0[...]), jnp.where(own, a1[...], b1[...]))
        _adamw_math(gv, w_ref, m_ref, v_ref, g_o, d_o, m_o, v_o)

    lay0 = pl.BlockSpec((tr, cols), lambda l, h, i: (i * (1 - l) + (nt - 1) * l, 0))
    lay1 = pl.BlockSpec((tr, cols), lambda l, h, i: (i * l, 0))
    wspec = pl.BlockSpec((None, tr, cols), lambda l, h, i: (l, h * nt + i, 0))
    f = jax.ShapeDtypeStruct((2, 2 * half, cols), F32)
    return pl.pallas_call(
        body, name=name, grid=(2, 2, nt), in_specs=[SCALAR, lay0, lay0, lay1, lay1, wspec, wspec, wspec],
        out_specs=[wspec] * 4, out_shape=[f] * 4, compiler_params=_params(),
    )(core, mine[0], theirs[0], mine[1], theirs[1], w, m, v)


def _colsum(name, a):
    def body(a_ref, o_ref):
        o_ref[...] = jnp.sum(a_ref[...], axis=0, keepdims=True)

    return pl.pallas_call(body, name=name, out_shape=jax.ShapeDtypeStruct((1, a.shape[1]), F32),
                          compiler_params=_params())(a)


PACK_UNIT = SUBLANE * LANE


def _pack(items):
    flat, layout, o = [], [], 0
    for it in items:
        n = it.size
        pad = -n % PACK_UNIT
        flat.append(jnp.pad(it.reshape(-1).astype(F32), (0, pad)))
        layout.append((o, n, it.shape))
        o += n + pad
    return jnp.concatenate(flat).reshape(-1, LANE), layout


def _unpack(pack, layout):
    flat = pack.reshape(-1)
    return [flat[o:o + n].reshape(shape) for o, n, shape in layout]


WEIGHTS = ("ffn1_norm", "ffn1_w1", "ffn1_w3", "ffn1_w2", "mix_norm", "w_in", "conv_w", "conv_b", "rg_wa", "rg_ba",
           "rg_wx", "rg_bx", "rg_lam", "fox_bf", "merge_b", "w_rg", "w_sb", "w_fox", "w_o", "ffn2_norm", "ffn2_w1",
           "ffn2_w3", "ffn2_w2", "ada_w", "ada_b", "final_norm", "final_ada_w", "final_ada_b")
DENSE = ("ffn1_w1", "ffn1_w3", "ffn1_w2", "w_in", "w_rg", "w_sb", "w_fox", "w_o", "ffn2_w1", "ffn2_w3", "ffn2_w2")
FFN1 = ("ffn1_w1", "ffn1_w3", "ffn1_w2")
FFN2 = ("ffn2_w1", "ffn2_w3", "ffn2_w2")
MIXER = ("w_in", "w_rg", "w_sb", "w_fox", "w_o")
SMALL = ("ffn1_norm", "mix_norm", "ffn2_norm", "rg_small", "rg_wa", "rg_wx", "fox_bf", "merge_b")


def _step(a):
    x, c, tgt = a["x"], a["c"], a["loss_target"]
    bl, seq, d = x.shape
    depth, nh = a["fox_bf"].shape
    chans = a["rg_lam"].shape[1]
    bd = a["rg_wa"].shape[2]
    wq = 3 * nh * HEAD_DIM
    o_m = 2 * chans + 2 * wq
    batch = NUM_DEVICES * bl
    mx, my, mc = _mesh_pos()
    me = 2 * mx + my
    dev = 4 * mx + 2 * my + mc

    c_rows = -(-bl * d // LANE // SUBLANE) * SUBLANE
    c_pack = jnp.pad(c.reshape(-1, LANE), ((0, c_rows - bl * d // LANE), (0, 0)))
    c_all = _gather_all("gather_c", c_pack)[:, :bl * d // LANE].reshape(batch, d)
    c_act = _silu("c_act", c_all)
    c_b = c_act.astype(BF16)
    ncol, fcol = a["ada_w"].shape[2], a["final_ada_w"].shape[1]
    cols = []
    for l in range(depth):
        bias = jnp.broadcast_to(lax.dynamic_slice_in_dim(a["ada_b"][l], me * ncol, ncol)[None], (batch, ncol))
        cols.append(_mm(f"ada{l}", c_b, a["ada_w"][l].astype(BF16), NN, F32, acc=bias))
    bias = jnp.broadcast_to(lax.dynamic_slice_in_dim(a["final_ada_b"], me * fcol, fcol)[None], (batch, fcol))
    cols.append(_mm("ada_final", c_b, a["final_ada_w"].astype(BF16), NN, F32, acc=bias))
    mod_cols = jnp.concatenate(cols, axis=1).reshape(2, batch // 2, depth * ncol + fcol)

    names = DENSE + ("conv_w", "mod_cols")
    got = _gather_two_level("gather_weights", [a[n].astype(BF16) for n in DENSE] + [a["conv_w"], mod_cols])
    gw = dict(zip(names, got))
    w = _prepare(gw, a, d, chans, nh)
    mod_all = gw["mod_cols"].reshape(NUM_CHIPS, batch, -1)
    mine = lambda full: lax.dynamic_slice_in_dim(full, dev * bl, bl, axis=0)
    mods = [mine(mod_all[:, :, l * ncol:(l + 1) * ncol].transpose(1, 0, 2).reshape(batch, NUM_CHIPS * ncol))
            for l in range(depth)]
    fm = mine(mod_all[:, :, depth * ncol:].transpose(1, 0, 2).reshape(batch, NUM_CHIPS * fcol))

    core = jnp.reshape(mc, (1,)).astype(jnp.int32)
    chip = jnp.reshape(me, (1,)).astype(jnp.int32)

    def chip_partials(names, gl, tag):
        rs_in = []
        for n in names:
            if n == "w_in":
                ga, gb, gf = gl["w_in"]
                orig = jnp.concatenate([ga[:, :2 * chans], gb, gf[:, :nh], ga[:, 2 * chans:]], axis=1)
                rs_in.append(orig.reshape(d, NUM_CHIPS, -1).transpose(1, 0, 2))
            else:
                rs_in.append(gl[n])
        rs_in = [g.reshape(g.shape[0], 2, g.shape[1] // 2, g.shape[2]) for g in rs_in]
        theirs = _split_to_sibling(f"split_grads_{tag}", rs_in)
        return {n: _add_pair(f"add_cores_{tag}_{n}", g, t, core) for n, g, t in zip(names, rs_in, theirs)}

    assert depth == 2
    loss, grad_x, grads, dmods, dfm, (parts, slots) = _local_step(w, x, tgt, mods, fm, chip_partials)
    loss = lax.psum(loss, ("x", "y", "c"))
    left = [(l, n) for l in range(depth) for n in DENSE if (l, n) not in slots]
    for (l, n), p in zip(left, chip_partials([n for _, n in left], grads["layers"][0], "l0_ffn1").values()):
        parts[(l, n)] = p
    slots.update(zip(left, _scatter_chips("scatter_grads", [parts[k] for k in left])))
    order = [(l, n) for l in range(depth) for n in DENSE]
    reduced = [_sum_chips(f"sum_chips_{l}_{n}", slots[(l, n)], parts[(l, n)], chip) for l, n in order]
    other = _join_halves("join_grads", reduced)

    out = {}

    def put(n, res, per_layer):
        for kind, val in zip(("grad_", "delta_", "new_m_", "new_v_"), res):
            out[kind + n] = jnp.stack(val).reshape(a[n].shape) if per_layer else val.reshape(a[n].shape)

    def flat3(v):
        return v.reshape(depth, -1, v.shape[-1])

    assert depth == 2
    nd = len(DENSE)
    for k, n in enumerate(DENSE):
        put(n, _adamw_halves(f"adamw_{n}", (reduced[k], reduced[nd + k]), (other[k], other[nd + k]), core,
                             flat3(a[n]), flat3(a["m_" + n]), flat3(a["v_" + n])), False)

    items = []
    for l in range(depth):
        g = grads["layers"][l]
        items += [g["ffn1_norm"], g["mix_norm"], g["ffn2_norm"], g["rg_small"], _unblockdiag(g["rg_dwa"], bd),
                  _unblockdiag(g["rg_dwx"], bd), g["fox_bf"], g["merge_b"], dmods[l]]
    items += [grads["final_norm"], dfm]
    pack, layout = _pack(items)
    gath, tot = _sum_devices("sum_small", _gather_all("gather_small", pack, own=False), pack,
                             jnp.reshape(dev, (1,)).astype(jnp.int32))

    def wpack(pre):
        its = []
        for l in range(depth):
            rg_rows = jnp.concatenate([jnp.zeros((4, chans), F32), a[pre + "conv_b"][l][None], a[pre + "rg_ba"][l][None],
                                       a[pre + "rg_bx"][l][None], a[pre + "rg_lam"][l][None]], axis=0)
            its += [a[pre + "ffn1_norm"][l], a[pre + "mix_norm"][l], a[pre + "ffn2_norm"][l], rg_rows,
                    a[pre + "rg_wa"][l], a[pre + "rg_wx"][l], a[pre + "fox_bf"][l], a[pre + "merge_b"][l],
                    jnp.zeros((bl, 9 * d), F32)]
        its += [a[pre + "final_norm"], jnp.zeros((bl, 2 * d), F32)]
        return _pack(its)[0]

    res_small = [_unpack(r, layout) for r in [tot] + list(_adamw("adamw_small", tot, wpack(""), wpack("m_"), wpack("v_")))]
    per = len(SMALL) + 1
    for j, n in enumerate(SMALL):
        if n == "rg_small":
            for row, nm in ((4, "conv_b"), (5, "rg_ba"), (6, "rg_bx"), (7, "rg_lam")):
                put(nm, [[r[l * per + j][row] for l in range(depth)] for r in res_small], True)
        else:
            put(n, [[r[l * per + j] for l in range(depth)] for r in res_small], True)
    put("final_norm", [r[depth * per] for r in res_small], False)

    gflat = gath.reshape(NUM_DEVICES, -1)

    def rows_of(idx):
        o, n, shape = layout[idx]
        return gflat[:, o:o + n].reshape(NUM_DEVICES * shape[0], shape[1])

    late_g, ada = [], []
    for l in range(depth):
        dmod_all = rows_of(l * per + per - 1)
        late_g.append(_colsum(f"ada_b_grad_{l}", dmod_all))
        cut = lax.dynamic_slice_in_dim(dmod_all, me * ncol, ncol, axis=1).astype(BF16)
        ada.append(_mm(f"ada_w_grad_{l}", c_b, cut, TN, F32))
    put("ada_w", _adamw_layers("adamw_ada_w", ada[0], ada[1], a["ada_w"], a["m_ada_w"], a["v_ada_w"]), False)
    dfm_all = rows_of(depth * per + 1)
    late_g.append(_colsum("final_ada_b_grad", dfm_all))
    cut = lax.dynamic_slice_in_dim(dfm_all, me * fcol, fcol, axis=1).astype(BF16)
    gl = _mm("final_ada_w_grad", c_b, cut, TN, F32)
    put("final_ada_w", [gl] + list(_adamw("adamw_final_ada_w", gl, a["final_ada_w"], a["m_final_ada_w"],
                                          a["v_final_ada_w"])), False)
    cshard = a["conv_w"].shape[2]
    for l in range(depth):
        rg_tot = res_small[0][l * per + SMALL.index("rg_small")]
        late_g.append(lax.dynamic_slice_in_dim(rg_tot[:4], me * cshard, cshard, axis=1))
    gp2, layout2 = _pack(late_g)

    def wpack2(pre):
        return _pack([a[pre + "ada_b"][l][None] for l in range(depth)] + [a[pre + "final_ada_b"][None]]
                     + [a[pre + "conv_w"][l] for l in range(depth)])[0]

    res_late = [_unpack(r, layout2) for r in [gp2] + list(_adamw("adamw_late", gp2, wpack2(""), wpack2("m_"), wpack2("v_")))]
    put("ada_b", [[r[l] for l in range(depth)] for r in res_late], True)
    put("final_ada_b", [r[depth] for r in res_late], False)
    put("conv_w", [[r[depth + 1 + l] for l in range(depth)] for r in res_late], True)

    outs = [loss, grad_x]
    for kind in ("grad_", "delta_", "new_m_", "new_v_"):
        outs += [out[kind + n] for n in WEIGHTS]
    return tuple(outs)


def kernel(x, c, ffn1_norm, ffn1_w1, ffn1_w3, ffn1_w2, mix_norm, w_in, conv_w, conv_b, rg_wa, rg_ba, rg_wx, rg_bx, rg_lam, fox_bf, merge_b, w_rg, w_sb, w_fox, w_o, ffn2_norm, ffn2_w1, ffn2_w3, ffn2_w2, ada_w, ada_b, final_norm, final_ada_w, final_ada_b, loss_target, m_ffn1_norm, m_ffn1_w1, m_ffn1_w3, m_ffn1_w2, m_mix_norm, m_w_in, m_conv_w, m_conv_b, m_rg_wa, m_rg_ba, m_rg_wx, m_rg_bx, m_rg_lam, m_fox_bf, m_merge_b, m_w_rg, m_w_sb, m_w_fox, m_w_o, m_ffn2_norm, m_ffn2_w1, m_ffn2_w3, m_ffn2_w2, m_ada_w, m_ada_b, m_final_norm, m_final_ada_w, m_final_ada_b, v_ffn1_norm, v_ffn1_w1, v_ffn1_w3, v_ffn1_w2, v_mix_norm, v_w_in, v_conv_w, v_conv_b, v_rg_wa, v_rg_ba, v_rg_wx, v_rg_bx, v_rg_lam, v_fox_bf, v_merge_b, v_w_rg, v_w_sb, v_w_fox, v_w_o, v_ffn2_norm, v_ffn2_w1, v_ffn2_w3, v_ffn2_w2, v_ada_w, v_ada_b, v_final_norm, v_final_ada_w, v_final_ada_b):
    args = dict(locals())
    return _step(args)
```

```python
import math

import jax
import jax.numpy as jnp
from jax import lax
from jax.experimental import pallas as pl
from jax.experimental.pallas import tpu as pltpu

F32 = jnp.float32
BF16 = jnp.bfloat16

NUM_CHIPS = 4
NUM_DEVICES = 8
HEAD_DIM = 64
LANE = 128
SUBLANE = 8
VMEM_LIMIT = 56 * 1024 * 1024
EPS = 1e-6
RG_C = 8.0
ADAM_LR = 0.001
ADAM_B1 = 0.9
ADAM_B2 = 0.999
ADAM_EPS = 1e-08
ADAM_WD = 0.01
ADAM_STEP = 10
MESH = pl.DeviceIdType.MESH
ANY = pl.BlockSpec(memory_space=pl.ANY)
SCALAR = pl.BlockSpec(memory_space=pltpu.SMEM)


def _params():
    return pltpu.CompilerParams(vmem_limit_bytes=VMEM_LIMIT)


def _tile(dim, pref):
    if dim <= pref:
        return dim
    t = (pref // LANE) * LANE
    while t >= LANE:
        if dim % t == 0:
            return t
        t -= LANE
    return dim


def _rtile(rows, pref, unit=2 * SUBLANE):
    if rows <= pref:
        return rows
    t = (pref // unit) * unit
    while t >= unit:
        if rows % t == 0:
            return t
        t -= unit
    return rows


def _sigmoid(x):
    return 0.5 * jnp.tanh(0.5 * x) + 0.5


def _softplus(x):
    return jnp.maximum(x, 0.0) + jnp.log(1.0 + jnp.exp(-jnp.abs(x)))


def _expm1(x):
    small = x * (1.0 + x * (0.5 + x * (1.0 / 6.0 + x * (1.0 / 24.0))))
    return jnp.where(jnp.abs(x) < 0.01, small, jnp.exp(x) - 1.0)


_GELU_K = math.sqrt(2.0 / math.pi)


def _gelu_and_grad(x):
    inner = _GELU_K * (x + 0.044715 * x * x * x)
    t = jnp.tanh(inner)
    val = 0.5 * x * (1.0 + t)
    dinner = _GELU_K * (1.0 + 3.0 * 0.044715 * x * x)
    grad = 0.5 * (1.0 + t) + 0.5 * x * (1.0 - t * t) * dinner
    return val, grad


NN = ((1,), (0,))
NT = ((1,), (1,))
TN = ((0,), (0,))
ALL = slice(None)


def _ride_call(name, body, grid, in_specs, out_specs, out_shape, scratch_shapes, args, riders=()):
    nr = len(riders)
    if not nr:
        return pl.pallas_call(body, name=name, grid=grid, in_specs=in_specs, out_specs=out_specs, out_shape=out_shape,
                              scratch_shapes=scratch_shapes, compiler_params=_params())(*args)
    n_in, n_out, n_scr = len(in_specs), len(out_specs), len(scratch_shapes)

    def hosted(*refs):
        ins, ride_in = refs[:n_in], refs[n_in:n_in + nr]
        outs = refs[n_in + nr:n_in + nr + n_out]
        ride_out = refs[n_in + nr + n_out:n_in + 2 * nr + n_out]
        scr = refs[n_in + 2 * nr + n_out:n_in + 2 * nr + n_out + n_scr]
        send, recv = refs[-2:]
        x, y, c = _mesh_pos()
        copies = []
        for j, (px, py) in enumerate(_other_chips(x, y)):
            for i in range(nr):
                copies.append(pltpu.make_async_remote_copy(
                    src_ref=ride_in[i].at[2 * px + py], dst_ref=ride_out[i].at[j], send_sem=send.at[j * nr + i],
                    recv_sem=recv.at[j * nr + i], device_id=(px, py, c), device_id_type=MESH))
        first = pl.program_id(0) == 0
        last = pl.program_id(0) == grid[0] - 1
        for k in range(1, len(grid)):
            first = first & (pl.program_id(k) == 0)
            last = last & (pl.program_id(k) == grid[k] - 1)

        @pl.when(first)
        def _():
            for cp in copies:
                cp.start()

        body(*ins, *outs, *scr)

        @pl.when(last)
        def _():
            for cp in copies:
                cp.wait()

    slots = [jax.ShapeDtypeStruct((NUM_CHIPS - 1,) + r.shape[1:], r.dtype) for r in riders]
    return pl.pallas_call(
        hosted, name=name, grid=grid, in_specs=list(in_specs) + [ANY] * nr, out_specs=list(out_specs) + [ANY] * nr,
        out_shape=list(out_shape) + slots,
        scratch_shapes=list(scratch_shapes) + [pltpu.SemaphoreType.DMA((3 * nr,)), pltpu.SemaphoreType.DMA((3 * nr,))],
        compiler_params=_params(),
    )(*args, *riders)


def _mmk(name, ops, specs, terms, out_shape, out_dtype, grid, o_spec, acc=None, riders=()):
    n_ops = len(ops)

    def body(*refs):
        o_ref = refs[-1]
        p = None
        for ia, xa, ib, xb, dims in terms:
            t = lax.dot_general(refs[ia][xa], refs[ib][xb], (dims, ((), ())), preferred_element_type=F32)
            p = t if p is None else p + t
        if acc is not None:
            p = p + refs[n_ops][...].astype(F32)
        o_ref[...] = p.astype(o_ref.dtype)

    in_specs = list(specs)
    args = list(ops)
    if acc is not None:
        in_specs.append(pl.BlockSpec(o_spec.block_shape, o_spec.index_map))
        args.append(acc)
    res = _ride_call(name, body, grid, in_specs, [o_spec], [jax.ShapeDtypeStruct(out_shape, out_dtype)], [], args,
                     riders)
    return res if riders else res[0]


def _mmk_resid(name, ops, specs, terms, x, gate, coef, seq, tm, tn):
    t_rows, d = x.shape
    n_ops = len(ops)
    per = seq // tm

    def body(*refs):
        x_ref, g_ref, y_ref, xn_ref = refs[n_ops:]
        p = None
        for ia, xa, ib, xb, dims in terms:
            t = lax.dot_general(refs[ia][xa], refs[ib][xb], (dims, ((), ())), preferred_element_type=F32)
            p = t if p is None else p + t
        y_ref[...] = p
        xn_ref[...] = x_ref[...] + (coef * (1.0 + g_ref[...])) * p

    blk = pl.BlockSpec((tm, tn), lambda i, j: (i, j))
    f = jax.ShapeDtypeStruct((t_rows, d), F32)
    return pl.pallas_call(
        body, name=name, grid=(t_rows // tm, d // tn),
        in_specs=list(specs) + [blk, pl.BlockSpec((None, 1, tn), lambda i, j: (i // per, 0, j))],
        out_specs=[blk, blk], out_shape=[f, f], compiler_params=_params(),
    )(*ops, x, gate)


def _mm(name, a, b, dims, out_dtype, acc=None, tm=512, tn=512):
    if dims == NN:
        (m, kk), n = a.shape, b.shape[1]
    elif dims == NT:
        (m, kk), n = a.shape, b.shape[0]
    else:
        (kk, m), n = a.shape, b.shape[1]
    tm, tn = _tile(m, tm), _tile(n, tn)
    if dims == TN:
        a_spec = pl.BlockSpec((kk, tm), lambda i, j: (0, i))
    else:
        a_spec = pl.BlockSpec((tm, kk), lambda i, j: (i, 0))
    if dims == NT:
        b_spec = pl.BlockSpec((tn, kk), lambda i, j: (j, 0))
    else:
        b_spec = pl.BlockSpec((kk, tn), lambda i, j: (0, j))
    return _mmk(name, [a, b], [a_spec, b_spec], [(0, ALL, 1, ALL, dims)], (m, n), out_dtype,
                (m // tm, n // tn), pl.BlockSpec((tm, tn), lambda i, j: (i, j)), acc)


def _row_grid(t_rows, seq, pref=256):
    tm = _tile(seq, pref)
    return tm, seq // tm


def _normmod(name, x, gain, shift, scale, seq):
    t_rows, d = x.shape
    bl = t_rows // seq
    tm, per = _row_grid(t_rows, seq)

    def body(x_ref, g_ref, sh_ref, sc_ref, o_ref):
        xv = x_ref[...]
        rstd = lax.rsqrt(jnp.mean(xv * xv, axis=-1, keepdims=True) + EPS)
        hn = (xv * rstd) * g_ref[...]
        o_ref[...] = (hn * (1.0 + sc_ref[...]) + sh_ref[...]).astype(o_ref.dtype)

    row = pl.BlockSpec((tm, d), lambda b, i: (b * per + i, 0))
    vec = pl.BlockSpec((None, 1, d), lambda b, i: (b, 0, 0))
    return pl.pallas_call(
        body, name=name, grid=(bl, per),
        in_specs=[row, pl.BlockSpec((1, d), lambda b, i: (0, 0)), vec, vec],
        out_specs=row, out_shape=jax.ShapeDtypeStruct((t_rows, d), BF16),
        compiler_params=_params(),
    )(x, gain, shift, scale)


def _normmod_bwd(name, x, dh, dxo, gain, scale, seq):
    t_rows, d = x.shape
    bl = t_rows // seq
    tm, per = _row_grid(t_rows, seq)

    def body(x_ref, dh_ref, dxo_ref, g_ref, sc_ref, dx_ref, dsh_ref, dsc_ref, dg_ref):
        b, i = pl.program_id(0), pl.program_id(1)
        xv = x_ref[...]
        dhv = dh_ref[...]
        rstd = lax.rsqrt(jnp.mean(xv * xv, axis=-1, keepdims=True) + EPS)
        xhat = xv * rstd
        gain_v = g_ref[...]
        dhn = dhv * (1.0 + sc_ref[...])
        dxhat = dhn * gain_v
        dx = rstd * (dxhat - xhat * jnp.mean(dxhat * xhat, axis=-1, keepdims=True))
        dx_ref[...] = dxo_ref[...] + dx

        @pl.when(i == 0)
        def _():
            dsh_ref[...] = jnp.zeros_like(dsh_ref)
            dsc_ref[...] = jnp.zeros_like(dsc_ref)

        @pl.when((i == 0) & (b == 0))
        def _():
            dg_ref[...] = jnp.zeros_like(dg_ref)

        dsh_ref[...] += jnp.sum(dhv, axis=0, keepdims=True)
        dsc_ref[...] += jnp.sum(dhv * (xhat * gain_v), axis=0, keepdims=True)
        dg_ref[...] += jnp.sum(dhn * xhat, axis=0, keepdims=True)

    row = pl.BlockSpec((tm, d), lambda b, i: (b * per + i, 0))
    vec = pl.BlockSpec((None, 1, d), lambda b, i: (b, 0, 0))
    one = pl.BlockSpec((1, d), lambda b, i: (0, 0))
    return pl.pallas_call(
        body, name=name, grid=(bl, per),
        in_specs=[row, row, row, one, vec],
        out_specs=[row, vec, vec, one],
        out_shape=[jax.ShapeDtypeStruct((t_rows, d), F32), jax.ShapeDtypeStruct((bl, 1, d), F32),
                   jax.ShapeDtypeStruct((bl, 1, d), F32), jax.ShapeDtypeStruct((1, d), F32)],
        compiler_params=_params(),
    )(x, dh, dxo, gain, scale)


def _resid_bwd(name, dxo, y, gate, coef, seq):
    t_rows, d = dxo.shape
    bl = t_rows // seq
    tm, per = _row_grid(t_rows, seq)

    def body(dxo_ref, y_ref, g_ref, dy_ref, dg_ref):
        i = pl.program_id(1)
        dxov = dxo_ref[...]
        dy_ref[...] = ((coef * (1.0 + g_ref[...])) * dxov).astype(dy_ref.dtype)

        @pl.when(i == 0)
        def _():
            dg_ref[...] = jnp.zeros_like(dg_ref)

        dg_ref[...] += jnp.sum((coef * y_ref[...]) * dxov, axis=0, keepdims=True)

    row = pl.BlockSpec((tm, d), lambda b, i: (b * per + i, 0))
    vec = pl.BlockSpec((None, 1, d), lambda b, i: (b, 0, 0))
    return pl.pallas_call(
        body, name=name, grid=(bl, per), in_specs=[row, row, vec], out_specs=[row, vec],
        out_shape=[jax.ShapeDtypeStruct((t_rows, d), BF16), jax.ShapeDtypeStruct((bl, 1, d), F32)],
        compiler_params=_params(),
    )(dxo, y, gate)


def _final_loss(name, x, tgt, gain, shift, scale, seq):
    t_rows, d = x.shape
    bl = t_rows // seq
    tm, per = _row_grid(t_rows, seq)

    def body(x_ref, t_ref, g_ref, sh_ref, sc_ref, l_ref, dx_ref, dsh_ref, dsc_ref, dg_ref):
        b, i = pl.program_id(0), pl.program_id(1)
        xv = x_ref[...]
        rstd = lax.rsqrt(jnp.mean(xv * xv, axis=-1, keepdims=True) + EPS)
        xhat = xv * rstd
        gain_v = g_ref[...]
        hn = xhat * gain_v
        yv = hn * (1.0 + sc_ref[...]) + sh_ref[...]
        err = yv - t_ref[...]
        dyv = err * (1.0 / d)
        dhn = dyv * (1.0 + sc_ref[...])
        dxhat = dhn * gain_v
        dx_ref[...] = rstd * (dxhat - xhat * jnp.mean(dxhat * xhat, axis=-1, keepdims=True))

        @pl.when(i == 0)
        def _():
            l_ref[...] = jnp.zeros_like(l_ref)
            dsh_ref[...] = jnp.zeros_like(dsh_ref)
            dsc_ref[...] = jnp.zeros_like(dsc_ref)

        @pl.when((i == 0) & (b == 0))
        def _():
            dg_ref[...] = jnp.zeros_like(dg_ref)

        part = jnp.sum(jnp.sum(err * err, axis=-1, keepdims=True), axis=0, keepdims=True) * (0.5 / d)
        l_ref[...] += jnp.broadcast_to(part, l_ref.shape)
        dsh_ref[...] += jnp.sum(dyv, axis=0, keepdims=True)
        dsc_ref[...] += jnp.sum(dyv * hn, axis=0, keepdims=True)
        dg_ref[...] += jnp.sum(dhn * xhat, axis=0, keepdims=True)

    row = pl.BlockSpec((tm, d), lambda b, i: (b * per + i, 0))
    vec = pl.BlockSpec((None, 1, d), lambda b, i: (b, 0, 0))
    one = pl.BlockSpec((1, d), lambda b, i: (0, 0))
    lvec = pl.BlockSpec((None, 1, LANE), lambda b, i: (b, 0, 0))
    return pl.pallas_call(
        body, name=name, grid=(bl, per),
        in_specs=[row, row, one, vec, vec],
        out_specs=[lvec, row, vec, vec, one],
        out_shape=[jax.ShapeDtypeStruct((bl, 1, LANE), F32), jax.ShapeDtypeStruct((t_rows, d), F32),
                   jax.ShapeDtypeStruct((bl, 1, d), F32), jax.ShapeDtypeStruct((bl, 1, d), F32),
                   jax.ShapeDtypeStruct((1, d), F32)],
        compiler_params=_params(),
    )(x, tgt, gain, shift, scale)


def _ffn_up(name, h, w1, w3, l):
    t_rows, d = h.shape
    ng, fs = w1.shape[0], w1.shape[3]
    tm = _tile(t_rows, 512)

    def body(h_ref, w1_ref, w3_ref, a_ref, b_ref, g_ref):
        hv = h_ref[...]
        av = jnp.dot(hv, w1_ref[...], preferred_element_type=F32)
        bv = jnp.dot(hv, w3_ref[...], preferred_element_type=F32)
        a_ref[...] = av.astype(a_ref.dtype)
        b_ref[...] = bv.astype(b_ref.dtype)
        g_ref[...] = (av * _sigmoid(av) * bv).astype(g_ref.dtype)

    wspec = pl.BlockSpec((None, None, d, fs), lambda g, i: (g, l, 0, 0))
    out = pl.BlockSpec((None, tm, fs), lambda g, i: (g, i, 0))
    f = jax.ShapeDtypeStruct((ng, t_rows, fs), BF16)
    return pl.pallas_call(
        body, name=name, grid=(ng, t_rows // tm),
        in_specs=[pl.BlockSpec((tm, d), lambda g, i: (i, 0)), wspec, wspec], out_specs=[out, out, out],
        out_shape=[f, f, f], compiler_params=_params(),
    )(h, w1, w3)


def _ffn_down_dx(name, dy, w2, a, b, l, riders=()):
    t_rows, d = dy.shape
    ng, fs = w2.shape[0], w2.shape[2]
    tm = _tile(t_rows, 512)

    def body(dy_ref, w2_ref, a_ref, b_ref, da_ref, db_ref):
        dgv = lax.dot_general(dy_ref[...], w2_ref[...], (NT, ((), ())), preferred_element_type=F32)
        av = a_ref[...].astype(F32)
        sig = _sigmoid(av)
        da_ref[...] = (dgv * b_ref[...].astype(F32) * (sig * (1.0 + av * (1.0 - sig)))).astype(da_ref.dtype)
        db_ref[...] = (dgv * (av * sig)).astype(db_ref.dtype)

    blk = pl.BlockSpec((None, tm, fs), lambda g, i: (g, i, 0))
    o = jax.ShapeDtypeStruct((ng, t_rows, fs), BF16)
    return _ride_call(
        name, body, (ng, t_rows // tm),
        [pl.BlockSpec((tm, d), lambda g, i: (i, 0)), pl.BlockSpec((None, None, fs, d), lambda g, i: (g, l, 0, 0)),
         blk, blk],
        [blk, blk], [o, o], [], (dy, w2, a, b), riders)


def _ffn_up_dw(name, h, da, db):
    t_rows, d = h.shape
    ng, fs = da.shape[0], da.shape[2]
    tn = _tile(d, 512)

    def body(h_ref, da_ref, db_ref, o1_ref, o3_ref):
        hv = h_ref[...]
        o1_ref[...] = lax.dot_general(hv, da_ref[...], (TN, ((), ())), preferred_element_type=F32).astype(o1_ref.dtype)
        o3_ref[...] = lax.dot_general(hv, db_ref[...], (TN, ((), ())), preferred_element_type=F32).astype(o3_ref.dtype)

    dspec = pl.BlockSpec((None, t_rows, fs), lambda g, i: (g, 0, 0))
    out = pl.BlockSpec((None, tn, fs), lambda g, i: (g, i, 0))
    o = jax.ShapeDtypeStruct((ng, d, fs), BF16)
    return pl.pallas_call(
        body, name=name, grid=(ng, d // tn),
        in_specs=[pl.BlockSpec((t_rows, tn), lambda g, i: (0, i)), dspec, dspec],
        out_specs=[out, out], out_shape=[o, o], compiler_params=_params(),
    )(h, da, db)


def _ffn_fwd(tag, w, l, pre, x, mod, seq):
    t_rows, d = x.shape
    w1, w3, w2 = w[pre + "w1"], w[pre + "w3"], w[pre + "w2"]
    ng, fs = w1.shape[0], w1.shape[3]
    shift, scale, gate = mod
    h = _normmod(tag + "_norm", x, w[pre + "norm"][l][None], shift, scale, seq)
    a, b, gact = _ffn_up(tag + "_up", h, w1, w3, l)
    tm, tn = _tile(seq, 512), _tile(d, 512)
    y, xn = _mmk_resid(tag + "_down", [gact, w2],
                       [pl.BlockSpec((ng, tm, fs), lambda i, j: (0, i, 0)),
                        pl.BlockSpec((ng, None, fs, tn), lambda i, j: (0, l, 0, j))],
                       [(0, g, 1, g, NN) for g in range(ng)], x, gate, 0.5, seq, tm, tn)
    return xn, (x, h, a, b, gact, y)


def _ffn_bwd(tag, w, l, pre, saved, mod, dxo, seq, riders_dn=(), riders_up=()):
    x, h, a, b, gact, y = saved
    t_rows, d = x.shape
    w1, w3, w2 = w[pre + "w1"], w[pre + "w3"], w[pre + "w2"]
    ng, fs = w1.shape[0], w1.shape[3]
    shift, scale, gate = mod
    tm, tn = _tile(t_rows, 512), _tile(d, 512)
    dy, dgate = _resid_bwd(tag + "_res_bwd", dxo, y, gate, 0.5, seq)
    da, db, *slots_dn = _ffn_down_dx(tag + "_down_dx", dy, w2, a, b, l, riders_dn)
    dw2 = _mmk(tag + "_down_dw", [gact, dy],
               [pl.BlockSpec((None, t_rows, fs), lambda g, j: (g, 0, 0)),
                pl.BlockSpec((t_rows, tn), lambda g, j: (0, j))],
               [(0, ALL, 1, ALL, TN)], (ng, fs, d), BF16, (ng, d // tn),
               pl.BlockSpec((None, fs, tn), lambda g, j: (g, 0, j)))
    dw1, dw3 = _ffn_up_dw(tag + "_up_dw", h, da, db)
    dspec = pl.BlockSpec((ng, tm, fs), lambda i, j: (0, i, 0))
    wspec = pl.BlockSpec((ng, None, tn, fs), lambda i, j: (0, l, j, 0))
    dh = _mmk(tag + "_up_dx", [da, db, w1, w3], [dspec, dspec, wspec, wspec],
              [(0, g, 2, g, NT) for g in range(ng)] + [(1, g, 3, g, NT) for g in range(ng)],
              (t_rows, d), F32, (t_rows // tm, d // tn), pl.BlockSpec((tm, tn), lambda i, j: (i, j)),
              riders=riders_up)
    slots_up = []
    if riders_up:
        dh, *slots_up = dh
    dx, dshift, dscale, dgain = _normmod_bwd(tag + "_norm_bwd", x, dh, dxo, w[pre + "norm"][l][None], scale, seq)
    grads = {pre + "w1": dw1, pre + "w3": dw3, pre + "w2": dw2, pre + "norm": dgain}
    return dx, (dshift, dscale, dgate), grads, slots_dn, slots_up


def _shift_down(v, s, row):
    if s == 0:
        return v
    return jnp.where(row >= s, pltpu.roll(v, s, 0), 0.0)


def _shift_up(v, s, row):
    if s == 0:
        return v
    n = v.shape[0]
    return jnp.where(row < n - s, pltpu.roll(v, n - s, 0), 0.0)


def _scan_fwd(a, u, row):
    n = a.shape[0]
    s = 1
    while s < n:
        ok = row >= s
        a_sh = pltpu.roll(a, s, 0)
        u_sh = pltpu.roll(u, s, 0)
        u = jnp.where(ok, a * u_sh + u, u)
        a = jnp.where(ok, a * a_sh, a)
        s *= 2
    return u


def _scan_bwd(a_next, g, row):
    n = g.shape[0]
    a, u = a_next, g
    s = 1
    while s < n:
        ok = row < n - s
        a_sh = pltpu.roll(a, n - s, 0)
        u_sh = pltpu.roll(u, n - s, 0)
        u = jnp.where(ok, a * u_sh + u, u)
        a = jnp.where(ok, a * a_sh, a)
        s *= 2
    return u


def _rg_specs(seq, cw):
    slab = lambda off: pl.BlockSpec((seq, cw), lambda c, b: (b, off + c))
    par = lambda rows: pl.BlockSpec((rows, cw), lambda c, b: (0, c))
    wbd = pl.BlockSpec((None, cw, cw), lambda c, b: (c, 0, 0))
    return slab, par, wbd


def _rg_fwd(name, proj, p, seq, chans):
    t_rows = proj.shape[0]
    bl = t_rows // seq
    cw = LANE
    nc = chans // cw
    slab, par, wbd = _rg_specs(seq, cw)

    def body(x_ref, gt_ref, cw_ref, cb_ref, wa_ref, ba_ref, wx_ref, bx_ref, lam_ref,
             xa_ref, r_ref, i_ref, h_ref, ya_ref):
        row = lax.broadcasted_iota(jnp.int32, (seq, cw), 0)
        xv = x_ref[...]
        xa = jnp.zeros_like(xv) + cb_ref[...]
        for k in range(4):
            xa = xa + cw_ref[k:k + 1, :] * _shift_down(xv, 3 - k, row)
        xab = xa.astype(BF16)
        r = _sigmoid(jnp.dot(xab, wa_ref[...], preferred_element_type=F32) + ba_ref[...])
        ig = _sigmoid(jnp.dot(xab, wx_ref[...], preferred_element_type=F32) + bx_ref[...])
        log_a = (-RG_C) * r * _softplus(-lam_ref[...])
        a = jnp.exp(log_a)
        u = jnp.sqrt(-_expm1(2.0 * log_a)) * (ig * xa)
        h = _scan_fwd(a, u, row)
        gel, _ = _gelu_and_grad(gt_ref[...])
        xa_ref[...] = xa
        r_ref[...] = r
        i_ref[...] = ig
        h_ref[...] = h
        ya_ref[...] = (gel * h).astype(ya_ref.dtype)

    out = pl.BlockSpec((seq, cw), lambda c, b: (b, c))
    f = jax.ShapeDtypeStruct((t_rows, chans), F32)
    return pl.pallas_call(
        body, name=name, grid=(nc, bl),
        in_specs=[slab(0), slab(nc), par(4), par(1), wbd, par(1), wbd, par(1), par(1)],
        out_specs=[out] * 5,
        out_shape=[f, f, f, f, jax.ShapeDtypeStruct((t_rows, chans), BF16)],
        compiler_params=_params(),
    )(proj, proj, p["conv_w"], p["conv_b"], p["wa"], p["ba"], p["wx"], p["bx"], p["lam"])


def _rg_bwd(name, proj, dya, saved, p, seq, chans):
    xa_s, r_s, i_s, h_s = saved
    t_rows = proj.shape[0]
    bl = t_rows // seq
    cw = LANE
    nc = chans // cw
    slab, par, wbd = _rg_specs(seq, cw)

    def body(x_ref, gt_ref, dya_ref, xa_ref, r_ref, i_ref, h_ref, cw_ref, wa_ref, wx_ref, lam_ref,
             dx_ref, dgt_ref, sm_ref, dwa_ref, dwx_ref):
        b = pl.program_id(1)
        row = lax.broadcasted_iota(jnp.int32, (seq, cw), 0)
        xv, xa, r, ig, h = x_ref[...], xa_ref[...], r_ref[...], i_ref[...], h_ref[...]
        dyav = dya_ref[...]
        gel, dgel = _gelu_and_grad(gt_ref[...])
        dgt_ref[...] = (dyav * h * dgel).astype(dgt_ref.dtype)
        dh = dyav * gel
        lam = lam_ref[...]
        sp = _softplus(-lam)
        log_a = (-RG_C) * r * sp
        a = jnp.exp(log_a)
        s = jnp.sqrt(-_expm1(2.0 * log_a))
        lamb = _scan_bwd(_shift_up(a, 1, row), dh, row)
        da = lamb * _shift_down(h, 1, row)
        xi = ig * xa
        ds = lamb * xi
        dxi = lamb * s
        dlog = da * a - ds * (a * a) / s
        dr = dlog * ((-RG_C) * sp)
        dsp = jnp.sum(dlog * ((-RG_C) * r), axis=0, keepdims=True)
        dlam = -dsp * _sigmoid(-lam)
        dzr = dr * r * (1.0 - r)
        dzi = (dxi * xa) * ig * (1.0 - ig)
        dzrb, dzib, xab = dzr.astype(BF16), dzi.astype(BF16), xa.astype(BF16)
        dxa = dxi * ig
        dxa = dxa + lax.dot_general(dzrb, wa_ref[...], (NT, ((), ())), preferred_element_type=F32)
        dxa = dxa + lax.dot_general(dzib, wx_ref[...], (NT, ((), ())), preferred_element_type=F32)
        dwa = lax.dot_general(xab, dzrb, (TN, ((), ())), preferred_element_type=F32)
        dwx = lax.dot_general(xab, dzib, (TN, ((), ())), preferred_element_type=F32)
        dxv = jnp.zeros_like(xv)
        rows = []
        for k in range(4):
            dxv = dxv + cw_ref[k:k + 1, :] * _shift_up(dxa, 3 - k, row)
            rows.append(jnp.sum(dxa * _shift_down(xv, 3 - k, row), axis=0, keepdims=True))
        dx_ref[...] = dxv.astype(dx_ref.dtype)
        rows += [jnp.sum(dxa, axis=0, keepdims=True), jnp.sum(dzr, axis=0, keepdims=True),
                 jnp.sum(dzi, axis=0, keepdims=True), dlam]

        @pl.when(b == 0)
        def _():
            sm_ref[...] = jnp.zeros_like(sm_ref)
            dwa_ref[...] = jnp.zeros_like(dwa_ref)
            dwx_ref[...] = jnp.zeros_like(dwx_ref)

        for k, val in enumerate(rows):
            sm_ref[k:k + 1, :] += val
        dwa_ref[...] += dwa
        dwx_ref[...] += dwx

    plain = pl.BlockSpec((seq, cw), lambda c, b: (b, c))
    return pl.pallas_call(
        body, name=name, grid=(nc, bl),
        in_specs=[slab(0), slab(nc), plain, plain, plain, plain, plain, par(4), wbd, wbd, par(1)],
        out_specs=[plain, plain, par(8), wbd, wbd],
        out_shape=[jax.ShapeDtypeStruct((t_rows, chans), BF16), jax.ShapeDtypeStruct((t_rows, chans), BF16),
                   jax.ShapeDtypeStruct((8, chans), F32),
                   jax.ShapeDtypeStruct((nc, cw, cw), F32), jax.ShapeDtypeStruct((nc, cw, cw), F32)],
        compiler_params=_params(),
    )(proj, proj, dya, xa_s, r_s, i_s, h_s, p["conv_w"], p["wa"], p["wx"], p["lam"])


ATT_Q_BLOCK = 512
SB_K_BLOCK = 256
FOX_K_BLOCK = 512
PAIR = LANE // HEAD_DIM
NEG = -1e30
SCALE = HEAD_DIM ** -0.5
assert math.log2(HEAD_DIM) % 2 == 0


def _att_blocks(seq, k_block):
    return _tile(seq, ATT_Q_BLOCK), _tile(seq, k_block)


def _key_blocks(qi, tq, bk):
    return (qi * tq) // bk, (qi * tq + tq - 1) // bk + 1


def _tri(n, kind):
    r = lax.broadcasted_iota(jnp.int32, (2 * n, n), 0)
    r = jnp.where(r >= n, r - n, r)
    c = lax.broadcasted_iota(jnp.int32, (2 * n, n), 1)
    m = {"gt": r > c, "le": r <= c, "lt": r < c}[kind]
    return m.astype(BF16)


def _cumsum_mm(v, tri):
    hi = v.astype(BF16)
    lo = (v - hi.astype(F32)).astype(BF16)
    return jnp.dot(jnp.concatenate([hi, lo], axis=1), tri, preferred_element_type=F32)


def _head_masks():
    lane = lax.broadcasted_iota(jnp.int32, (1, LANE), 1)
    return [(lane >= h * HEAD_DIM) & (lane < (h + 1) * HEAD_DIM) for h in range(PAIR)]


def _only(mask, v):
    return jnp.where(mask, v, jnp.zeros_like(v))


def _stack_heads(v, masks):
    return jnp.concatenate([_only(m, v) for m in masks], axis=0)


def _unstack_heads(v, masks):
    tq = v.shape[0] // PAIR
    out = _only(masks[0], v[0:tq])
    for h in range(1, PAIR):
        out = out + _only(masks[h], v[h * tq:(h + 1) * tq])
    return out


def _stacked_iotas(tq, bk):
    row = lax.broadcasted_iota(jnp.int32, (PAIR * tq, bk), 0)
    for h in range(1, PAIR):
        row = jnp.where(row >= h * tq, row - tq, row)
    return row, lax.broadcasted_iota(jnp.int32, (PAIR * tq, bk), 1)


def _att_specs(seq, blk, nq, off):
    npair = None
    qs = lambda o: pl.BlockSpec((blk, LANE), lambda b, p, i: (b * nq + i, o + p))
    ks = lambda o: pl.BlockSpec((seq, LANE), lambda b, p, i: (b, o + p))
    col = pl.BlockSpec((None, PAIR, blk, 1), lambda b, p, i: (b, p, i, 0))
    lane = pl.BlockSpec((None, PAIR, 1, seq), lambda b, p, i: (b, p, 0, 0))
    return qs, ks, col, lane


def _sb_fwd(name, qkv, off, width, bl, seq):
    t_rows = qkv.shape[0]
    tq, bk = _att_blocks(seq, SB_K_BLOCK)
    nq = seq // tq
    nb = width // LANE
    qs, ks, col, _ = _att_specs(seq, tq, nq, off)

    def body(q_ref, k_ref, v_ref, o_ref, lt_ref):
        qi = pl.program_id(2)
        masks = _head_masks()
        qs_ = _stack_heads(q_ref[...] * SCALE, masks)
        row, cix = _stacked_iotas(tq, bk)
        tri = _tri(bk, "gt")

        def step(masked, top):
            def go(it, carry):
                acc, cl = carry
                kb = top - it
                ks_ = pl.multiple_of(kb * bk, bk)
                kv = k_ref[pl.ds(ks_, bk), :]
                vv = v_ref[pl.ds(ks_, bk), :]
                strict = (kb * bk + cix) < (qi * tq + row)
                z = lax.dot_general(qs_, kv, (NT, ((), ())), preferred_element_type=F32)
                sp = _softplus(z)
                lk = jnp.where(strict, -sp, 0.0) if masked else -sp
                wgt = jnp.exp(z - sp + (cl + _cumsum_mm(lk, tri)))
                if masked:
                    wgt = jnp.where(strict, wgt, 0.0)
                acc = acc + _unstack_heads(jnp.dot(wgt.astype(BF16), vv, preferred_element_type=F32), masks)
                return acc, cl + jnp.sum(lk, axis=1, keepdims=True)
            return go

        n_full, n_all = _key_blocks(qi, tq, bk)
        carry = (jnp.zeros((tq, LANE), F32), jnp.zeros((PAIR * tq, 1), F32))
        carry = lax.fori_loop(0, n_all - n_full, step(True, n_all - 1), carry)
        acc, cl = lax.fori_loop(0, n_full, step(False, n_full - 1), carry)
        o_ref[...] = acc.astype(o_ref.dtype)
        for h in range(PAIR):
            lt_ref[h] = cl[h * tq:(h + 1) * tq]

    return pl.pallas_call(
        body, name=name, grid=(bl, nb, nq), in_specs=[qs(off), ks(off + nb), ks(off + 2 * nb)],
        out_specs=[qs(0), col],
        out_shape=[jax.ShapeDtypeStruct((t_rows, width), BF16),
                   jax.ShapeDtypeStruct((bl, nb * PAIR, seq, 1), F32)],
        compiler_params=_params(),
    )(qkv, qkv, qkv)


def _sb_bwd(name, qkv, off, width, bl, seq, ltot, do, riders=()):
    t_rows = qkv.shape[0]
    tq, bk = _att_blocks(seq, SB_K_BLOCK)
    nq = seq // tq
    nb = width // LANE
    qs, ks, col, _ = _att_specs(seq, tq, nq, off)

    def body(q_ref, k_ref, v_ref, lt_ref, do_ref, dq_ref, dk_ref, dv_ref, dk_acc, dv_acc):
        qi = pl.program_id(2)

        @pl.when(qi == 0)
        def _():
            dk_acc[...] = jnp.zeros_like(dk_acc)
            dv_acc[...] = jnp.zeros_like(dv_acc)

        masks = _head_masks()
        qs_ = _stack_heads(q_ref[...] * SCALE, masks)
        dos = _stack_heads(do_ref[...].astype(BF16), masks)
        lts = jnp.concatenate([lt_ref[h] for h in range(PAIR)], axis=0)
        row, cix = _stacked_iotas(tq, bk)
        tri_le = _tri(bk, "le")
        tri_lt = _tri(bk, "lt")

        def step(masked):
            def go(kb, carry):
                dq, cl, ce = carry
                ks_ = pl.multiple_of(kb * bk, bk)
                kv = k_ref[pl.ds(ks_, bk), :]
                vv = v_ref[pl.ds(ks_, bk), :]
                strict = (kb * bk + cix) < (qi * tq + row)
                z = lax.dot_general(qs_, kv, (NT, ((), ())), preferred_element_type=F32)
                sp = _softplus(z)
                lk = jnp.where(strict, -sp, 0.0) if masked else -sp
                sig = jnp.exp(z - sp)
                wgt = sig * jnp.exp(lts - cl - _cumsum_mm(lk, tri_le))
                if masked:
                    wgt = jnp.where(strict, wgt, 0.0)
                dw = lax.dot_general(dos, vv, (NT, ((), ())), preferred_element_type=F32)
                e = dw * wgt
                pre = ce + _cumsum_mm(e, tri_lt)
                dz = e * (1.0 - sig) - pre * sig
                if masked:
                    dz = jnp.where(strict, dz, 0.0)
                dzb = dz.astype(BF16)
                dq = dq + _unstack_heads(jnp.dot(dzb, kv * SCALE, preferred_element_type=F32), masks)
                dk_acc[pl.ds(ks_, bk), :] += lax.dot_general(dzb, qs_, (TN, ((), ())), preferred_element_type=F32)
                dv_acc[pl.ds(ks_, bk), :] += lax.dot_general(wgt.astype(BF16), dos, (TN, ((), ())),
                                                             preferred_element_type=F32)
                return dq, cl + jnp.sum(lk, axis=1, keepdims=True), ce + jnp.sum(e, axis=1, keepdims=True)
            return go

        n_full, n_all = _key_blocks(qi, tq, bk)
        zero = jnp.zeros((PAIR * tq, 1), F32)
        carry = lax.fori_loop(0, n_full, step(False), (jnp.zeros((tq, LANE), F32), zero, zero))
        dq, _, _ = lax.fori_loop(n_full, n_all, step(True), carry)
        dq_ref[...] = dq.astype(dq_ref.dtype)

        @pl.when(qi == nq - 1)
        def _():
            dk_ref[...] = dk_acc[...].astype(dk_ref.dtype)
            dv_ref[...] = dv_acc[...].astype(dv_ref.dtype)

    o = jax.ShapeDtypeStruct((t_rows, width), BF16)
    return _ride_call(
        name, body, (bl, nb, nq), [qs(off), ks(off + nb), ks(off + 2 * nb), col, qs(0)], [qs(0), ks(0), ks(0)],
        [o, o, o], [pltpu.VMEM((seq, LANE), F32), pltpu.VMEM((seq, LANE), F32)], (qkv, qkv, qkv, ltot, do), riders)


def _fox_fwd(name, qkv, off, width, bl, seq, cum_q, cum_k):
    t_rows = qkv.shape[0]
    tq, bk = _att_blocks(seq, FOX_K_BLOCK)
    nq = seq // tq
    nb = width // LANE
    qs, ks, col, lane = _att_specs(seq, tq, nq, off)

    def body(q_ref, k_ref, v_ref, cq_ref, ck_ref, ob_ref, of_ref, lse_ref):
        qi = pl.program_id(2)
        masks = _head_masks()
        qs_ = _stack_heads(q_ref[...] * SCALE, masks)
        cqs = jnp.concatenate([cq_ref[h] for h in range(PAIR)], axis=0)
        row, cix = _stacked_iotas(tq, bk)

        def step(masked):
            def go(kb, carry):
                m, lsum, acc = carry
                ks_ = pl.multiple_of(kb * bk, bk)
                kv = k_ref[pl.ds(ks_, bk), :]
                vv = v_ref[pl.ds(ks_, bk), :]
                bias = jnp.concatenate([cqs[h * tq:(h + 1) * tq] - ck_ref[h, :, pl.ds(ks_, bk)] for h in range(PAIR)],
                                       axis=0)
                z = lax.dot_general(qs_, kv, (NT, ((), ())), preferred_element_type=F32) + bias
                if masked:
                    z = jnp.where((kb * bk + cix) <= (qi * tq + row), z, NEG)
                m_new = jnp.maximum(m, jnp.max(z, axis=1, keepdims=True))
                pv = jnp.exp(z - m_new)
                alpha = jnp.exp(m - m_new)
                lsum = alpha * lsum + jnp.sum(pv, axis=1, keepdims=True)
                acc = alpha * acc + jnp.dot(pv.astype(BF16), vv, preferred_element_type=F32)
                return m_new, lsum, acc
            return go

        n_full, n_all = _key_blocks(qi, tq, bk)
        init = (jnp.full((PAIR * tq, 1), NEG, F32), jnp.zeros((PAIR * tq, 1), F32),
                jnp.zeros((PAIR * tq, LANE), F32))
        carry = lax.fori_loop(0, n_full, step(False), init)
        m, lsum, acc = lax.fori_loop(n_full, n_all, step(True), carry)
        out = _unstack_heads(acc / lsum, masks)
        ob_ref[...] = out.astype(ob_ref.dtype)
        of_ref[...] = out
        lse = m + jnp.log(lsum)
        for h in range(PAIR):
            lse_ref[h] = lse[h * tq:(h + 1) * tq]

    return pl.pallas_call(
        body, name=name, grid=(bl, nb, nq),
        in_specs=[qs(off), ks(off + nb), ks(off + 2 * nb), col, lane], out_specs=[qs(0), qs(0), col],
        out_shape=[jax.ShapeDtypeStruct((t_rows, width), BF16), jax.ShapeDtypeStruct((t_rows, width), F32),
                   jax.ShapeDtypeStruct((bl, nb * PAIR, seq, 1), F32)],
        compiler_params=_params(),
    )(qkv, qkv, qkv, cum_q, cum_k)


def _fox_bwd(name, qkv, off, width, bl, seq, cum_q, cum_k, lse, o, do, riders=()):
    t_rows = qkv.shape[0]
    tq, bk = _att_blocks(seq, FOX_K_BLOCK)
    nq = seq // tq
    nb = width // LANE
    qs, ks, col, lane = _att_specs(seq, tq, nq, off)

    def body(q_ref, k_ref, v_ref, cq_ref, ck_ref, lse_ref, o_ref, do_ref,
             dq_ref, dk_ref, dv_ref, dcq_ref, dck_ref, dk_acc, dv_acc):
        qi = pl.program_id(2)

        @pl.when(qi == 0)
        def _():
            dk_acc[...] = jnp.zeros_like(dk_acc)
            dv_acc[...] = jnp.zeros_like(dv_acc)
            dck_ref[...] = jnp.zeros_like(dck_ref)

        masks = _head_masks()
        qs_ = _stack_heads(q_ref[...] * SCALE, masks)
        dof = do_ref[...]
        dos = _stack_heads(dof.astype(BF16), masks)
        prod = dof * o_ref[...]
        delta = jnp.concatenate([jnp.sum(_only(m, prod), axis=1, keepdims=True) for m in masks], axis=0)
        shift = jnp.concatenate([cq_ref[h] - lse_ref[h] for h in range(PAIR)], axis=0)
        row, cix = _stacked_iotas(tq, bk)

        def step(masked):
            def go(kb, carry):
                dq, dcq = carry
                ks_ = pl.multiple_of(kb * bk, bk)
                kv = k_ref[pl.ds(ks_, bk), :]
                vv = v_ref[pl.ds(ks_, bk), :]
                bias = jnp.concatenate(
                    [shift[h * tq:(h + 1) * tq] - ck_ref[h, :, pl.ds(ks_, bk)] for h in range(PAIR)], axis=0)
                pv = jnp.exp(lax.dot_general(qs_, kv, (NT, ((), ())), preferred_element_type=F32) + bias)
                if masked:
                    pv = jnp.where((kb * bk + cix) <= (qi * tq + row), pv, 0.0)
                dp = lax.dot_general(dos, vv, (NT, ((), ())), preferred_element_type=F32)
                ds = pv * (dp - delta)
                dsb = ds.astype(BF16)
                dq = dq + _unstack_heads(jnp.dot(dsb, kv * SCALE, preferred_element_type=F32), masks)
                dk_acc[pl.ds(ks_, bk), :] += lax.dot_general(dsb, qs_, (TN, ((), ())), preferred_element_type=F32)
                dv_acc[pl.ds(ks_, bk), :] += lax.dot_general(pv.astype(BF16), dos, (TN, ((), ())),
                                                             preferred_element_type=F32)
                for h in range(PAIR):
                    dck_ref[h, :, pl.ds(ks_, bk)] += -jnp.sum(ds[h * tq:(h + 1) * tq], axis=0, keepdims=True)
                return dq, dcq + jnp.sum(ds, axis=1, keepdims=True)
            return go

        n_full, n_all = _key_blocks(qi, tq, bk)
        carry = lax.fori_loop(0, n_full, step(False), (jnp.zeros((tq, LANE), F32), jnp.zeros((PAIR * tq, 1), F32)))
        dq, dcq = lax.fori_loop(n_full, n_all, step(True), carry)
        dq_ref[...] = dq.astype(dq_ref.dtype)
        for h in range(PAIR):
            dcq_ref[h] = dcq[h * tq:(h + 1) * tq]

        @pl.when(qi == nq - 1)
        def _():
            dk_ref[...] = dk_acc[...].astype(dk_ref.dtype)
            dv_ref[...] = dv_acc[...].astype(dv_ref.dtype)

    ob = jax.ShapeDtypeStruct((t_rows, width), BF16)
    nh = nb * PAIR
    return _ride_call(
        name, body, (bl, nb, nq),
        [qs(off), ks(off + nb), ks(off + 2 * nb), col, lane, col, qs(0), qs(0)],
        [qs(0), ks(0), ks(0), col, lane],
        [ob, ob, ob, jax.ShapeDtypeStruct((bl, nh, seq, 1), F32), jax.ShapeDtypeStruct((bl, nh, 1, seq), F32)],
        [pltpu.VMEM((seq, LANE), F32), pltpu.VMEM((seq, LANE), F32)],
        (qkv, qkv, qkv, cum_q, cum_k, lse, o, do), riders)


def _lane_cumsum(v, reverse):
    n = v.shape[1]
    cix = lax.broadcasted_iota(jnp.int32, v.shape, 1)
    s = 1
    while s < n:
        if reverse:
            v = v + jnp.where(cix < n - s, pltpu.roll(v, n - s, 1), 0.0)
        else:
            v = v + jnp.where(cix >= s, pltpu.roll(v, s, 1), 0.0)
        s *= 2
    return v


def _forget_cum(name, fl, bf):
    def body(fl_ref, bf_ref, o_ref):
        xv = fl_ref[...] + bf_ref[...]
        o_ref[...] = _lane_cumsum(-_softplus(-xv), False)

    return pl.pallas_call(body, name=name, out_shape=jax.ShapeDtypeStruct(fl.shape, F32),
                          compiler_params=_params())(fl, bf)


def _forget_cum_bwd(name, fl, bf, dcum, nh):
    rows = fl.shape[0]

    def body(fl_ref, bf_ref, dc_ref, dfl_ref, dbf_ref):
        xv = fl_ref[...] + bf_ref[...]
        dlogf = _lane_cumsum(dc_ref[...], True)
        dfl = dlogf * _sigmoid(-xv)
        dfl_ref[...] = dfl
        per_row = jnp.sum(dfl, axis=1, keepdims=True)
        tot = per_row[0:nh]
        for b in range(1, rows // nh):
            tot = tot + per_row[b * nh:(b + 1) * nh]
        dbf_ref[...] = tot

    return pl.pallas_call(
        body, name=name,
        out_shape=[jax.ShapeDtypeStruct(fl.shape, F32), jax.ShapeDtypeStruct((nh, 1), F32)],
        compiler_params=_params(),
    )(fl, bf, dcum)


def _merge_fwd(name, proj, off, merge_b, pa, pb, pc):
    t_rows, d = pa.shape
    tm = _tile(t_rows, 256)

    def body(l0, l1, l2, mb, a_ref, b_ref, c_ref, o_ref):
        g0 = _sigmoid(l0[...] + mb[:, 0:d])
        g1 = _sigmoid(l1[...] + mb[:, d:2 * d])
        g2 = _sigmoid(l2[...] + mb[:, 2 * d:3 * d])
        o_ref[...] = (g0 * a_ref[...] + g1 * b_ref[...] + g2 * c_ref[...]).astype(o_ref.dtype)

    row = pl.BlockSpec((tm, d), lambda i: (i, 0))
    lg = lambda j: pl.BlockSpec((tm, d), lambda i: (i, off + j))
    return pl.pallas_call(
        body, name=name, grid=(t_rows // tm,),
        in_specs=[lg(0), lg(1), lg(2), pl.BlockSpec((1, 3 * d), lambda i: (0, 0)), row, row, row],
        out_specs=row, out_shape=jax.ShapeDtypeStruct((t_rows, d), BF16), compiler_params=_params(),
    )(proj, proj, proj, merge_b, pa, pb, pc)


def _merge_bwd(name, proj, off, merge_b, pa, pb, pc, dmixed):
    t_rows, d = pa.shape
    tm = _tile(t_rows, 256)

    def body(l0, l1, l2, mb, a_ref, b_ref, c_ref, dm_ref, da_ref, db_ref, dc_ref, dl_ref, dmb_ref):
        i = pl.program_id(0)
        dm = dm_ref[...]
        parts = []
        for j, (lref, pref, dref) in enumerate(((l0, a_ref, da_ref), (l1, b_ref, db_ref), (l2, c_ref, dc_ref))):
            g = _sigmoid(lref[...] + mb[:, j * d:(j + 1) * d])
            dref[...] = (g * dm).astype(dref.dtype)
            dl = dm * pref[...] * g * (1.0 - g)
            dl_ref[:, j * d:(j + 1) * d] = dl.astype(dl_ref.dtype)
            parts.append(jnp.sum(dl, axis=0, keepdims=True))
        tot = jnp.concatenate(parts, axis=1)

        @pl.when(i == 0)
        def _():
            dmb_ref[...] = tot

        @pl.when(i > 0)
        def _():
            dmb_ref[...] += tot

    row = pl.BlockSpec((tm, d), lambda i: (i, 0))
    lg = lambda j: pl.BlockSpec((tm, d), lambda i: (i, off + j))
    one = pl.BlockSpec((1, 3 * d), lambda i: (0, 0))
    b16 = jax.ShapeDtypeStruct((t_rows, d), BF16)
    return pl.pallas_call(
        body, name=name, grid=(t_rows // tm,),
        in_specs=[lg(0), lg(1), lg(2), one, row, row, row, row],
        out_specs=[row, row, row, pl.BlockSpec((tm, 3 * d), lambda i: (i, 0)), one],
        out_shape=[b16, b16, b16, jax.ShapeDtypeStruct((t_rows, 3 * d), BF16), jax.ShapeDtypeStruct((1, 3 * d), F32)],
        compiler_params=_params(),
    )(proj, proj, proj, merge_b, pa, pb, pc, dmixed)


def _grouped_nn(name, a, wg, l, out_dtype):
    t_rows, kk = a.shape
    ng, ncol = wg.shape[0], wg.shape[3]
    tm = _tile(t_rows, 512)
    return _mmk(name, [a, wg],
                [pl.BlockSpec((tm, kk), lambda i, g: (i, 0)),
                 pl.BlockSpec((None, None, kk, ncol), lambda i, g: (g, l, 0, 0))],
                [(0, ALL, 1, ALL, NN)], (t_rows, ng * ncol), out_dtype, (t_rows // tm, ng),
                pl.BlockSpec((tm, ncol), lambda i, g: (i, g)))


def _grouped_nt(name, da, wg, l, out_dtype):
    t_rows = da.shape[0]
    ng, kk, ncol = wg.shape[0], wg.shape[2], wg.shape[3]
    tm = _tile(t_rows, 512)
    return _mmk(name, [da, wg],
                [pl.BlockSpec((tm, ng * ncol), lambda i: (i, 0)),
                 pl.BlockSpec((ng, None, kk, ncol), lambda i: (0, l, 0, 0))],
                [(0, (ALL, slice(g * ncol, (g + 1) * ncol)), 1, g, NT) for g in range(ng)],
                (t_rows, kk), out_dtype, (t_rows // tm,), pl.BlockSpec((tm, kk), lambda i: (i, 0)))


def _grouped_tn(name, a, da, ng, out_dtype):
    t_rows, kk = a.shape
    ncol = da.shape[1] // ng
    return _mmk(name, [a, da],
                [pl.BlockSpec((t_rows, kk), lambda g: (0, 0)), pl.BlockSpec((t_rows, ncol), lambda g: (0, g))],
                [(0, ALL, 1, ALL, TN)], (ng, kk, ncol), out_dtype, (ng,),
                pl.BlockSpec((None, kk, ncol), lambda g: (g, 0, 0)))


def _rows_nn(name, a, wr, l, out_dtype):
    t_rows = a.shape[0]
    ng, kg, n = wr.shape[0], wr.shape[2], wr.shape[3]
    tm, tn = _tile(t_rows, 512), _tile(n, 512)
    return _mmk(name, [a, wr],
                [pl.BlockSpec((tm, ng * kg), lambda i, j: (i, 0)),
                 pl.BlockSpec((ng, None, kg, tn), lambda i, j: (0, l, 0, j))],
                [(0, (ALL, slice(g * kg, (g + 1) * kg)), 1, g, NN) for g in range(ng)],
                (t_rows, n), out_dtype, (t_rows // tm, n // tn), pl.BlockSpec((tm, tn), lambda i, j: (i, j)))


def _rows_nt(name, dy, wr, l, out_dtype):
    t_rows, n = dy.shape
    ng, kg = wr.shape[0], wr.shape[2]
    tm = _tile(t_rows, 512)
    return _mmk(name, [dy, wr],
                [pl.BlockSpec((tm, n), lambda i, g: (i, 0)),
                 pl.BlockSpec((None, None, kg, n), lambda i, g: (g, l, 0, 0))],
                [(0, ALL, 1, ALL, NT)], (t_rows, ng * kg), out_dtype, (t_rows // tm, ng),
                pl.BlockSpec((tm, kg), lambda i, g: (i, g)))


def _rows_tn(name, a, dy, ng, out_dtype):
    t_rows, n = dy.shape
    kg = a.shape[1] // ng
    tn = _tile(n, 512)
    return _mmk(name, [a, dy],
                [pl.BlockSpec((t_rows, kg), lambda g, j: (0, g)), pl.BlockSpec((t_rows, tn), lambda g, j: (0, j))],
                [(0, ALL, 1, ALL, TN)], (ng, kg, n), out_dtype, (ng, n // tn),
                pl.BlockSpec((None, kg, tn), lambda g, j: (g, 0, j)))


def _mix_fwd(tag, w, l, x, mod, seq):
    t_rows, d = x.shape
    bl = t_rows // seq
    shift, scale, gate = mod
    chans, nh = w["layout"]["chans"], w["layout"]["heads"]
    width = nh * HEAD_DIM
    nb = width // LANE
    h = _normmod(tag + "_norm", x, w["mix_norm"][l][None], shift, scale, seq)
    proj = _mm(tag + "_in_a", h, w["w_a"][l], NN, F32)
    qkv = _mm(tag + "_in_b", h, w["w_b"][l], NN, BF16)
    flp = _mm(tag + "_in_f", h, w["w_f"][l], NN, F32)
    xa, r, ig, hs, ya = _rg_fwd(tag + "_rg", proj, w["rg"][l], seq, chans)
    yb, ltot = _sb_fwd(tag + "_sb", qkv, 0, width, bl, seq)
    fl = flp[:, :nh].reshape(bl, seq, nh).transpose(0, 2, 1).reshape(bl * nh, seq)
    bf = jnp.tile(w["fox_bf"][l].reshape(nh, 1), (bl, 1))
    cum = _forget_cum(tag + "_cum", fl, bf)
    cum_q = cum.reshape(bl, nh, seq, 1)
    cum_k = cum.reshape(bl, nh, 1, seq)
    yc, oc, lse = _fox_fwd(tag + "_fox", qkv, 3 * nb, width, bl, seq, cum_q, cum_k)
    pa = _rows_nn(tag + "_prg", ya, w["w_rg"], l, F32)
    pb = _grouped_nn(tag + "_psb", yb, w["w_sb"], l, F32)
    pc = _grouped_nn(tag + "_pfox", yc, w["w_fox"], l, F32)
    moff = 2 * chans // d
    mb = w["merge_b"][l][None]
    mixed = _merge_fwd(tag + "_merge", proj, moff, mb, pa, pb, pc)
    w_o = w["w_o"]
    ngo, kgo = w_o.shape[0], w_o.shape[2]
    tm, tn = _tile(seq, 512), _tile(d, 512)
    y, xn = _mmk_resid(tag + "_out", [mixed, w_o],
                       [pl.BlockSpec((tm, ngo * kgo), lambda i, j: (i, 0)),
                        pl.BlockSpec((ngo, None, kgo, tn), lambda i, j: (0, l, 0, j))],
                       [(0, (ALL, slice(g * kgo, (g + 1) * kgo)), 1, g, NN) for g in range(ngo)],
                       x, gate, 1.0, seq, tm, tn)
    saved = dict(x=x, h=h, proj=proj, qkv=qkv, rg=(xa, r, ig, hs), ya=ya, ltot=ltot,
                 fox=(cum_q, cum_k, lse, oc), fl=fl, bf=bf, yb=yb, yc=yc, pa=pa, pb=pb, pc=pc, mixed=mixed, y=y)
    return xn, saved


def _mix_bwd(tag, w, l, s, mod, dxo, seq, riders=(), riders_fox=()):
    x = s["x"]
    t_rows, d = x.shape
    bl = t_rows // seq
    shift, scale, gate = mod
    chans, nh = w["layout"]["chans"], w["layout"]["heads"]
    width = nh * HEAD_DIM
    nb = width // LANE
    moff = 2 * chans // d
    mb = w["merge_b"][l][None]
    ng = NUM_CHIPS
    dy, dgate = _resid_bwd(tag + "_res_bwd", dxo, s["y"], gate, 1.0, seq)
    dmixed = _rows_nt(tag + "_out_dx", dy, w["w_o"], l, F32)
    dw_o = _rows_tn(tag + "_out_dw", s["mixed"], dy, ng, BF16)
    dpa, dpb, dpc, dlog, dmb = _merge_bwd(tag + "_merge_bwd", s["proj"], moff, mb, s["pa"], s["pb"], s["pc"], dmixed)
    dya = _rows_nt(tag + "_prg_dx", dpa, w["w_rg"], l, F32)
    dw_rg = _rows_tn(tag + "_prg_dw", s["ya"], dpa, ng, BF16)
    dyb = _grouped_nt(tag + "_psb_dx", dpb, w["w_sb"], l, F32)
    dw_sb = _grouped_tn(tag + "_psb_dw", s["yb"], dpb, ng, BF16)
    dyc = _grouped_nt(tag + "_pfox_dx", dpc, w["w_fox"], l, F32)
    dw_fox = _grouped_tn(tag + "_pfox_dw", s["yc"], dpc, ng, BF16)
    qkv = s["qkv"]
    dq_b, dk_b, dv_b, *slots = _sb_bwd(tag + "_sb_bwd", qkv, 0, width, bl, seq, s["ltot"], dyb, riders)
    cum_q, cum_k, lse, oc = s["fox"]
    dq_c, dk_c, dv_c, dcq, dck, *slots_fox = _fox_bwd(tag + "_fox_bwd", qkv, 3 * nb, width, bl, seq, cum_q, cum_k,
                                                      lse, oc, dyc, riders_fox)
    dcum = dcq.reshape(bl * nh, seq) + dck.reshape(bl * nh, seq)
    dfl, dbf = _forget_cum_bwd(tag + "_cum_bwd", s["fl"], s["bf"], dcum, nh)
    dfl_t = dfl.reshape(bl, nh, seq).transpose(0, 2, 1).reshape(t_rows, nh)
    dflp = jnp.pad(dfl_t, ((0, 0), (0, LANE - nh))).astype(BF16)
    drgx, dgt, rg_small, dwa, dwx = _rg_bwd(tag + "_rg_bwd", s["proj"], dya, s["rg"], w["rg"][l], seq, chans)
    dproj = jnp.concatenate([drgx, dgt, dlog], axis=1)
    dqkv = jnp.concatenate([dq_b, dk_b, dv_b, dq_c, dk_c, dv_c], axis=1)
    w_a, w_b, w_f = w["w_a"][l], w["w_b"][l], w["w_f"][l]
    pa_w, pb_w = w_a.shape[1], w_b.shape[1]
    tm, tn = _tile(t_rows, 512), _tile(d, 512)
    rows = lambda n: pl.BlockSpec((tm, n), lambda i, j: (i, 0))
    wrow = lambda n: pl.BlockSpec((tn, n), lambda i, j: (j, 0))
    dh = _mmk(tag + "_in_dx", [dproj, dqkv, dflp, w_a, w_b, w_f],
              [rows(pa_w), rows(pb_w), rows(LANE), wrow(pa_w), wrow(pb_w), wrow(LANE)],
              [(0, ALL, 3, ALL, NT), (1, ALL, 4, ALL, NT), (2, ALL, 5, ALL, NT)],
              (t_rows, d), F32, (t_rows // tm, d // tn), pl.BlockSpec((tm, tn), lambda i, j: (i, j)))
    hb = s["h"]
    dw_a = _mm(tag + "_in_a_dw", hb, dproj, TN, BF16)
    dw_b = _mm(tag + "_in_b_dw", hb, dqkv, TN, BF16)
    dw_f = _mm(tag + "_in_f_dw", hb, dflp, TN, BF16)
    dx, dshift, dscale, dgain = _normmod_bwd(tag + "_norm_bwd", x, dh, dxo, w["mix_norm"][l][None], scale, seq)
    grads = dict(w_in=(dw_a, dw_b, dw_f), w_rg=dw_rg, w_sb=dw_sb, w_fox=dw_fox, w_o=dw_o, mix_norm=dgain,
                 rg_small=rg_small, rg_dwa=dwa, rg_dwx=dwx, fox_bf=dbf, merge_b=dmb)
    return dx, (dshift, dscale, dgate), grads, slots, slots_fox


def _silu(name, c):
    def body(c_ref, o_ref):
        v = c_ref[...]
        o_ref[...] = v * _sigmoid(v)

    return pl.pallas_call(body, name=name, out_shape=jax.ShapeDtypeStruct(c.shape, F32),
                          compiler_params=_params())(c)


def _blockdiag(wb):
    nb, bd, _ = wb.shape
    per = LANE // bd
    t = wb.reshape(nb // per, per, bd, 1, bd)
    eye = jnp.eye(per, dtype=wb.dtype).reshape(1, per, 1, per, 1)
    return (t * eye).reshape(nb // per, LANE, LANE).astype(BF16)


def _unblockdiag(t, bd):
    n = t.shape[0]
    per = LANE // bd
    t5 = t.reshape(n, per, bd, per, bd)
    return jnp.stack([t5[:, p, :, p, :] for p in range(per)], axis=1).reshape(n * per, bd, bd)


def _prepare(gw, a, d, chans, nh):
    depth = a["ada_b"].shape[0]
    wq = 3 * nh * HEAD_DIM
    o_m = 2 * chans + 2 * wq
    w = {"layout": dict(chans=chans, heads=nh)}
    for n in ("ffn1_w1", "ffn1_w3", "ffn1_w2", "ffn2_w1", "ffn2_w3", "ffn2_w2", "w_rg", "w_sb", "w_fox", "w_o"):
        w[n] = gw[n]
    for n in ("ffn1_norm", "ffn2_norm", "mix_norm", "fox_bf", "merge_b", "final_norm"):
        w[n] = a[n]
    w_a, w_b, w_f, rg = [], [], [], []
    for l in range(depth):
        full = gw["w_in"][:, l].transpose(1, 0, 2).reshape(d, -1)
        w_a.append(jnp.concatenate([full[:, :2 * chans], full[:, o_m + nh:]], axis=1))
        w_b.append(full[:, 2 * chans:o_m])
        w_f.append(jnp.pad(full[:, o_m:o_m + nh], ((0, 0), (0, LANE - nh))))
        conv_w = gw["conv_w"][:, l].transpose(1, 0, 2).reshape(-1, chans)
        rg.append(dict(conv_w=conv_w, conv_b=a["conv_b"][l][None], ba=a["rg_ba"][l][None], bx=a["rg_bx"][l][None],
                       lam=a["rg_lam"][l][None], wa=_blockdiag(a["rg_wa"][l]), wx=_blockdiag(a["rg_wx"][l])))
    w["w_a"], w["w_b"], w["w_f"], w["rg"] = w_a, w_b, w_f, rg
    return w


def _local_step(w, x, tgt, mods, fm, ride=None):
    bl, seq, d = x.shape
    t_rows = bl * seq
    depth = len(mods)
    mod3 = []
    for l in range(depth):
        m4 = mods[l].reshape(bl, 9, 1, d)
        mod3.append([(m4[:, 3 * k], m4[:, 3 * k + 1], m4[:, 3 * k + 2]) for k in range(3)])
    fm4 = fm.reshape(bl, 2, 1, d)
    saved = []
    xc = x.reshape(t_rows, d)
    for l in range(depth):
        xc, s1 = _ffn_fwd(f"l{l}_ffn1", w, l, "ffn1_", xc, mod3[l][0], seq)
        xc, s2 = _mix_fwd(f"l{l}_mix", w, l, xc, mod3[l][1], seq)
        xc, s3 = _ffn_fwd(f"l{l}_ffn2", w, l, "ffn2_", xc, mod3[l][2], seq)
        saved.append((s1, s2, s3))
    lpart, dx, dfs, dfc, dfg = _final_loss("final", xc, tgt.reshape(t_rows, d), w["final_norm"][None],
                                           fm4[:, 0], fm4[:, 1], seq)
    loss = jnp.sum(lpart[:, 0, 0])
    grads = {"final_norm": dfg, "layers": [None] * depth}
    dmods = [None] * depth
    parts, slots = {}, {}
    queue = {"sb": [], "fox": [], "dn": [], "up": []}

    def take(host):
        keys, queue[host] = queue[host], []
        return keys, [parts[k] for k in keys]

    def enqueue(l, names, gl, tag, hosts):
        for n, p in ride(names, gl, tag).items():
            parts[(l, n)] = p
            queue[hosts(n)].append((l, n))

    for l in reversed(range(depth)):
        s1, s2, s3 = saved[l]
        dx, dm3, g3, _, _ = _ffn_bwd(f"l{l}_ffn2", w, l, "ffn2_", s3, mod3[l][2], dx, seq)
        if ride is not None and l == 0:
            enqueue(l, FFN2, g3, "l0_ffn2", lambda n: "sb")
        (k_sb, r_sb), (k_fox, r_fox) = take("sb"), take("fox")
        dx, dm2, g2, s_sb, s_fox = _mix_bwd(f"l{l}_mix", w, l, s2, mod3[l][1], dx, seq, r_sb, r_fox)
        slots.update(zip(k_sb + k_fox, list(s_sb) + list(s_fox)))
        if ride is not None and l == 0:
            enqueue(l, MIXER, g2, "l0_mix", lambda n: "up" if n == "w_in" else "dn")
        (k_dn, r_dn), (k_up, r_up) = take("dn"), take("up")
        dx, dm1, g1, s_dn, s_up = _ffn_bwd(f"l{l}_ffn1", w, l, "ffn1_", s1, mod3[l][0], dx, seq, r_dn, r_up)
        slots.update(zip(k_dn + k_up, list(s_dn) + list(s_up)))
        dmods[l] = jnp.concatenate([*dm1, *dm2, *dm3], axis=1).reshape(bl, 9 * d)
        grads["layers"][l] = {**g1, **g2, **g3}
        if ride is not None and l > 0:
            enqueue(l, DENSE, grads["layers"][l], f"l{l}", lambda n: "fox" if n == "w_in" else "sb")
    dfm = jnp.concatenate([dfs, dfc], axis=1).reshape(bl, 2 * d)
    return loss, dx.reshape(bl, seq, d), grads, dmods, dfm, (parts, slots)


def _mesh_pos():
    return lax.axis_index("x"), lax.axis_index("y"), lax.axis_index("c")


def _other_chips(x, y):
    return ((1 - x, y), (x, 1 - y), (1 - x, 1 - y))


def _gather_two_level(name, arrs):
    n = len(arrs)

    def body(*refs):
        ins, outs = refs[:n], refs[n:2 * n]
        send, recv, send2, recv2, send3, recv3 = refs[2 * n:]
        x, y, c = _mesh_pos()
        me = 2 * x + y
        chips = _other_chips(x, y)
        sib = (x, y, 1 - c)
        own = [pltpu.make_async_remote_copy(
            src_ref=ins[i], dst_ref=outs[i].at[me], send_sem=send3.at[i], recv_sem=recv3.at[i],
            device_id=sib, device_id_type=MESH) for i in range(n)]
        first = []
        for j, (px, py) in enumerate(chips):
            for i in range(n):
                first.append(pltpu.make_async_remote_copy(
                    src_ref=ins[i].at[c], dst_ref=outs[i].at[me, c], send_sem=send.at[j * n + i],
                    recv_sem=recv.at[j * n + i], device_id=(px, py, c), device_id_type=MESH))
        for cp in first + own:
            cp.start()
        passed = []
        for j, (px, py) in enumerate(chips):
            for i in range(n):
                landed = outs[i].at[2 * px + py, c]
                pltpu.make_async_remote_copy(
                    src_ref=ins[i].at[c], dst_ref=landed, send_sem=send.at[j * n + i],
                    recv_sem=recv.at[j * n + i], device_id=(px, py, c), device_id_type=MESH).wait_recv()
                fwd = pltpu.make_async_remote_copy(
                    src_ref=landed, dst_ref=landed, send_sem=send2.at[j * n + i],
                    recv_sem=recv2.at[j * n + i], device_id=sib, device_id_type=MESH)
                fwd.start()
                passed.append(fwd)
        for j, (px, py) in enumerate(chips):
            for i in range(n):
                theirs = outs[i].at[2 * px + py, 1 - c]
                pltpu.make_async_remote_copy(
                    src_ref=theirs, dst_ref=theirs, send_sem=send2.at[j * n + i],
                    recv_sem=recv2.at[j * n + i], device_id=sib, device_id_type=MESH).wait_recv()
        for cp in first + passed:
            cp.wait_send()
        for cp in own:
            cp.wait()

    return pl.pallas_call(
        body, name=name, in_specs=[ANY] * n, out_specs=[ANY] * n,
        out_shape=[jax.ShapeDtypeStruct((NUM_CHIPS,) + a.shape, a.dtype) for a in arrs],
        scratch_shapes=[pltpu.SemaphoreType.DMA((3 * n,)), pltpu.SemaphoreType.DMA((3 * n,)),
                        pltpu.SemaphoreType.DMA((3 * n,)), pltpu.SemaphoreType.DMA((3 * n,)),
                        pltpu.SemaphoreType.DMA((n,)), pltpu.SemaphoreType.DMA((n,))],
    )(*arrs)


def _split_to_sibling(name, arrs):
    n = len(arrs)
    slabs = arrs[0].shape[0]

    def body(*refs):
        ins, theirs = refs[:n], refs[n:2 * n]
        send, recv = refs[2 * n:]
        x, y, c = _mesh_pos()
        sib = (x, y, 1 - c)
        for i in range(n):
            for s in range(slabs):
                pltpu.make_async_remote_copy(
                    src_ref=ins[i].at[s, 1 - c], dst_ref=theirs[i].at[s], send_sem=send.at[i],
                    recv_sem=recv.at[i], device_id=sib, device_id_type=MESH).start()
        for i in range(n):
            pltpu.make_async_remote_copy(
                src_ref=ins[i].at[:, 0], dst_ref=theirs[i], send_sem=send.at[i], recv_sem=recv.at[i],
                device_id=sib, device_id_type=MESH).wait()

    return pl.pallas_call(
        body, name=name, in_specs=[ANY] * n, out_specs=[ANY] * n,
        out_shape=[jax.ShapeDtypeStruct((a.shape[0],) + a.shape[2:], a.dtype) for a in arrs],
        scratch_shapes=[pltpu.SemaphoreType.DMA((n,)), pltpu.SemaphoreType.DMA((n,))],
    )(*arrs)


def _scatter_chips(name, arrs):
    n = len(arrs)

    def body(*refs):
        ins, outs = refs[:n], refs[n:2 * n]
        send, recv = refs[2 * n:]
        x, y, c = _mesh_pos()
        chips = _other_chips(x, y)
        sends = []
        for j, (px, py) in enumerate(chips):
            for i in range(n):
                sends.append(pltpu.make_async_remote_copy(
                    src_ref=ins[i].at[2 * px + py], dst_ref=outs[i].at[j], send_sem=send.at[j * n + i],
                    recv_sem=recv.at[j * n + i], device_id=(px, py, c), device_id_type=MESH))
        for s in sends:
            s.start()
        for s in sends:
            s.wait()

    return pl.pallas_call(
        body, name=name, in_specs=[ANY] * n, out_specs=[ANY] * n,
        out_shape=[jax.ShapeDtypeStruct((NUM_CHIPS - 1,) + a.shape[1:], a.dtype) for a in arrs],
        scratch_shapes=[pltpu.SemaphoreType.DMA((3 * n,)), pltpu.SemaphoreType.DMA((3 * n,))],
    )(*arrs)


def _join_halves(name, arrs):
    n = len(arrs)

    def body(*refs):
        ins, outs = refs[:n], refs[n:2 * n]
        send, recv = refs[2 * n:]
        x, y, c = _mesh_pos()
        copies = [pltpu.make_async_remote_copy(
            src_ref=ins[i], dst_ref=outs[i], send_sem=send.at[i], recv_sem=recv.at[i],
            device_id=(x, y, 1 - c), device_id_type=MESH) for i in range(n)]
        for cp in copies:
            cp.start()
        for cp in copies:
            cp.wait()

    return pl.pallas_call(
        body, name=name, in_specs=[ANY] * n, out_specs=[ANY] * n,
        out_shape=[jax.ShapeDtypeStruct(a.shape, a.dtype) for a in arrs],
        scratch_shapes=[pltpu.SemaphoreType.DMA((n,)), pltpu.SemaphoreType.DMA((n,))],
    )(*arrs)


def _gather_all(name, pack, own=True):
    def body(in_ref, out_ref, send, recv, loc):
        x, y, c = _mesh_pos()
        me = 4 * x + 2 * y + c
        mine = pltpu.make_async_copy(in_ref, out_ref.at[me], loc)
        if own:
            mine.start()
        peers = []
        for mask in range(1, NUM_DEVICES):
            px = 1 - x if mask & 4 else x
            py = 1 - y if mask & 2 else y
            pc = 1 - c if mask & 1 else c
            peers.append((px, py, pc))
        sends = [pltpu.make_async_remote_copy(
            src_ref=in_ref, dst_ref=out_ref.at[me], send_sem=send.at[k], recv_sem=recv.at[k],
            device_id=p, device_id_type=MESH) for k, p in enumerate(peers)]
        for s in sends:
            s.start()
        for k, (px, py, pc) in enumerate(peers):
            pltpu.make_async_remote_copy(
                src_ref=in_ref, dst_ref=out_ref.at[4 * px + 2 * py + pc], send_sem=send.at[k], recv_sem=recv.at[k],
                device_id=(px, py, pc), device_id_type=MESH).wait_recv()
        for s in sends:
            s.wait_send()
        if own:
            mine.wait()

    return pl.pallas_call(
        body, name=name, in_specs=[ANY], out_specs=ANY,
        out_shape=jax.ShapeDtypeStruct((NUM_DEVICES,) + pack.shape, pack.dtype),
        scratch_shapes=[pltpu.SemaphoreType.DMA((NUM_DEVICES - 1,)), pltpu.SemaphoreType.DMA((NUM_DEVICES - 1,)),
                        pltpu.SemaphoreType.DMA],
    )(pack)


def _sum_devices(name, slots, pack, dev):
    g, rows, cols = slots.shape
    tr = _rtile(rows, 256)

    def body(d_ref, s_ref, p_ref, full_ref, o_ref):
        acc = None
        for k in range(g):
            v = jnp.where(d_ref[0] == k, p_ref[...], s_ref[k])
            full_ref[k] = v
            acc = v if acc is None else acc + v
        o_ref[...] = acc

    blk = pl.BlockSpec((g, tr, cols), lambda i: (0, i, 0))
    row = pl.BlockSpec((tr, cols), lambda i: (i, 0))
    return pl.pallas_call(
        body, name=name, grid=(rows // tr,), in_specs=[SCALAR, blk, row], out_specs=[blk, row],
        out_shape=[jax.ShapeDtypeStruct(slots.shape, F32), jax.ShapeDtypeStruct((rows, cols), F32)],
        compiler_params=_params(),
    )(dev, slots, pack)


def _add_pair(name, p, q, core):
    g, _, rows, cols = p.shape
    tr = _rtile(rows, 128)

    def body(c_ref, p_ref, q_ref, o_ref):
        mine = jnp.where(c_ref[0] == 0, p_ref[:, 0].astype(F32), p_ref[:, 1].astype(F32))
        o_ref[...] = (mine + q_ref[...].astype(F32)).astype(o_ref.dtype)

    spec = pl.BlockSpec((g, tr, cols), lambda i: (0, i, 0))
    return pl.pallas_call(
        body, name=name, grid=(rows // tr,),
        in_specs=[SCALAR, pl.BlockSpec((g, 2, tr, cols), lambda i: (0, 0, i, 0)), spec],
        out_specs=spec, out_shape=jax.ShapeDtypeStruct(q.shape, BF16), compiler_params=_params(),
    )(core, p, q)


def _sum_chips(name, slots, part, chip):
    g, rows, cols = part.shape
    tr = _rtile(rows, 128)

    def body(c_ref, s_ref, p_ref, o_ref):
        acc = p_ref[c_ref[0]].astype(F32)
        for k in range(slots.shape[0]):
            acc = acc + s_ref[k].astype(F32)
        o_ref[...] = acc

    return pl.pallas_call(
        body, name=name, grid=(rows // tr,),
        in_specs=[SCALAR,
                  pl.BlockSpec((slots.shape[0], tr, cols), lambda i: (0, i, 0)),
                  pl.BlockSpec((g, tr, cols), lambda i: (0, i, 0))],
        out_specs=pl.BlockSpec((tr, cols), lambda i: (i, 0)),
        out_shape=jax.ShapeDtypeStruct((rows, cols), F32), compiler_params=_params(),
    )(chip, slots, part)


def _adamw(name, g, w, m, v, l=None):
    rows, cols = g.shape
    tr = _rtile(rows, 128)

    def body(g_ref, w_ref, m_ref, v_ref, d_o, m_o, v_o):
        gv = g_ref[...]
        mn = ADAM_B1 * m_ref[...] + (1.0 - ADAM_B1) * gv
        vn = ADAM_B2 * v_ref[...] + (1.0 - ADAM_B2) * (gv * gv)
        m_hat = mn / (1.0 - ADAM_B1 ** ADAM_STEP)
        v_hat = vn / (1.0 - ADAM_B2 ** ADAM_STEP)
        d_o[...] = -ADAM_LR * (m_hat / (jnp.sqrt(v_hat) + ADAM_EPS) + ADAM_WD * w_ref[...])
        m_o[...] = mn
        v_o[...] = vn

    gspec = pl.BlockSpec((tr, cols), lambda i: (i, 0))
    wspec = gspec if l is None else pl.BlockSpec((None, tr, cols), lambda i: (l, i, 0))
    f = jax.ShapeDtypeStruct((rows, cols), F32)
    return pl.pallas_call(
        body, name=name, grid=(rows // tr,), in_specs=[gspec] + [wspec] * 3, out_specs=[gspec] * 3,
        out_shape=[f] * 3, compiler_params=_params(),
    )(g, w, m, v)


def _adamw_layers(name, g0, g1, w, m, v):
    rows, cols = g0.shape
    tr = _rtile(rows, 128)
    nt = rows // tr

    def body(g0_ref, g1_ref, w_ref, m_ref, v_ref, g_o, d_o, m_o, v_o):
        gv = jnp.where(pl.program_id(0) == 0, g0_ref[...], g1_ref[...])
        _adamw_math(gv, w_ref, m_ref, v_ref, g_o, d_o, m_o, v_o)

    g0spec = pl.BlockSpec((tr, cols), lambda l, i: (i * (1 - l) + (nt - 1) * l, 0))
    g1spec = pl.BlockSpec((tr, cols), lambda l, i: (i * l, 0))
    wspec = pl.BlockSpec((None, tr, cols), lambda l, i: (l, i, 0))
    f = jax.ShapeDtypeStruct((2, rows, cols), F32)
    return pl.pallas_call(
        body, name=name, grid=(2, nt), in_specs=[g0spec, g1spec, wspec, wspec, wspec], out_specs=[wspec] * 4,
        out_shape=[f] * 4, compiler_params=_params(),
    )(g0, g1, w, m, v)


def _adamw_math(gv, w_ref, m_ref, v_ref, g_o, d_o, m_o, v_o):
    mn = ADAM_B1 * m_ref[...] + (1.0 - ADAM_B1) * gv
    vn = ADAM_B2 * v_ref[...] + (1.0 - ADAM_B2) * (gv * gv)
    m_hat = mn / (1.0 - ADAM_B1 ** ADAM_STEP)
    v_hat = vn / (1.0 - ADAM_B2 ** ADAM_STEP)
    g_o[...] = gv
    d_o[...] = -ADAM_LR * (m_hat / (jnp.sqrt(v_hat) + ADAM_EPS) + ADAM_WD * w_ref[...])
    m_o[...] = mn
    v_o[...] = vn


def _adamw_halves(name, mine, theirs, core, w, m, v):
    half, cols = mine[0].shape
    tr = _rtile(half, 128)
    nt = half // tr

    def body(c_ref, a0, b0, a1, b1, w_ref, m_ref, v_ref, g_o, d_o, m_o, v_o):
        first = pl.program_id(0) == 0
        own = pl.program_id(1) == c_ref[0]
        gv = jnp.where(first, jnp.where(own, a0[...], b0[...]), jnp.where(own, a1[...], b1[...]))
        _adamw_math(gv, w_ref, m_ref, v_ref, g_o, d_o, m_o, v_o)

    lay0 = pl.BlockSpec((tr, cols), lambda l, h, i: (i * (1 - l) + (nt - 1) * l, 0))
    lay1 = pl.BlockSpec((tr, cols), lambda l, h, i: (i * l, 0))
    wspec = pl.BlockSpec((None, tr, cols), lambda l, h, i: (l, h * nt + i, 0))
    f = jax.ShapeDtypeStruct((2, 2 * half, cols), F32)
    return pl.pallas_call(
        body, name=name, grid=(2, 2, nt), in_specs=[SCALAR, lay0, lay0, lay1, lay1, wspec, wspec, wspec],
        out_specs=[wspec] * 4, out_shape=[f] * 4, compiler_params=_params(),
    )(core, mine[0], theirs[0], mine[1], theirs[1], w, m, v)


def _colsum(name, a):
    def body(a_ref, o_ref):
        o_ref[...] = jnp.sum(a_ref[...], axis=0, keepdims=True)

    return pl.pallas_call(body, name=name, out_shape=jax.ShapeDtypeStruct((1, a.shape[1]), F32),
                          compiler_params=_params())(a)


PACK_UNIT = SUBLANE * LANE


def _pack(items):
    flat, layout, o = [], [], 0
    for it in items:
        n = it.size
        pad = -n % PACK_UNIT
        flat.append(jnp.pad(it.reshape(-1).astype(F32), (0, pad)))
        layout.append((o, n, it.shape))
        o += n + pad
    return jnp.concatenate(flat).reshape(-1, LANE), layout


def _unpack(pack, layout):
    flat = pack.reshape(-1)
    return [flat[o:o + n].reshape(shape) for o, n, shape in layout]


WEIGHTS = ("ffn1_norm", "ffn1_w1", "ffn1_w3", "ffn1_w2", "mix_norm", "w_in", "conv_w", "conv_b", "rg_wa", "rg_ba",
           "rg_wx", "rg_bx", "rg_lam", "fox_bf", "merge_b", "w_rg", "w_sb", "w_fox", "w_o", "ffn2_norm", "ffn2_w1",
           "ffn2_w3", "ffn2_w2", "ada_w", "ada_b", "final_norm", "final_ada_w", "final_ada_b")
DENSE = ("ffn1_w1", "ffn1_w3", "ffn1_w2", "w_in", "w_rg", "w_sb", "w_fox", "w_o", "ffn2_w1", "ffn2_w3", "ffn2_w2")
FFN1 = ("ffn1_w1", "ffn1_w3", "ffn1_w2")
FFN2 = ("ffn2_w1", "ffn2_w3", "ffn2_w2")
MIXER = ("w_in", "w_rg", "w_sb", "w_fox", "w_o")
SMALL = ("ffn1_norm", "mix_norm", "ffn2_norm", "rg_small", "rg_wa", "rg_wx", "fox_bf", "merge_b")


def _step(a):
    x, c, tgt = a["x"], a["c"], a["loss_target"]
    bl, seq, d = x.shape
    depth, nh = a["fox_bf"].shape
    chans = a["rg_lam"].shape[1]
    bd = a["rg_wa"].shape[2]
    wq = 3 * nh * HEAD_DIM
    o_m = 2 * chans + 2 * wq
    batch = NUM_DEVICES * bl
    mx, my, mc = _mesh_pos()
    me = 2 * mx + my
    dev = 4 * mx + 2 * my + mc

    c_rows = -(-bl * d // LANE // SUBLANE) * SUBLANE
    c_pack = jnp.pad(c.reshape(-1, LANE), ((0, c_rows - bl * d // LANE), (0, 0)))
    c_all = _gather_all("gather_c", c_pack)[:, :bl * d // LANE].reshape(batch, d)
    c_act = _silu("c_act", c_all)
    c_b = c_act.astype(BF16)
    ncol, fcol = a["ada_w"].shape[2], a["final_ada_w"].shape[1]
    cols = []
    for l in range(depth):
        bias = jnp.broadcast_to(lax.dynamic_slice_in_dim(a["ada_b"][l], me * ncol, ncol)[None], (batch, ncol))
        cols.append(_mm(f"ada{l}", c_b, a["ada_w"][l].astype(BF16), NN, F32, acc=bias))
    bias = jnp.broadcast_to(lax.dynamic_slice_in_dim(a["final_ada_b"], me * fcol, fcol)[None], (batch, fcol))
    cols.append(_mm("ada_final", c_b, a["final_ada_w"].astype(BF16), NN, F32, acc=bias))
    mod_cols = jnp.concatenate(cols, axis=1).reshape(2, batch // 2, depth * ncol + fcol)

    names = DENSE + ("conv_w", "mod_cols")
    got = _gather_two_level("gather_weights", [a[n].astype(BF16) for n in DENSE] + [a["conv_w"], mod_cols])
    gw = dict(zip(names, got))
    w = _prepare(gw, a, d, chans, nh)
    mod_all = gw["mod_cols"].reshape(NUM_CHIPS, batch, -1)
    mine = lambda full: lax.dynamic_slice_in_dim(full, dev * bl, bl, axis=0)
    mods = [mine(mod_all[:, :, l * ncol:(l + 1) * ncol].transpose(1, 0, 2).reshape(batch, NUM_CHIPS * ncol))
            for l in range(depth)]
    fm = mine(mod_all[:, :, depth * ncol:].transpose(1, 0, 2).reshape(batch, NUM_CHIPS * fcol))

    core = jnp.reshape(mc, (1,)).astype(jnp.int32)
    chip = jnp.reshape(me, (1,)).astype(jnp.int32)

    def chip_partials(names, gl, tag):
        rs_in = []
        for n in names:
            if n == "w_in":
                ga, gb, gf = gl["w_in"]
                orig = jnp.concatenate([ga[:, :2 * chans], gb, gf[:, :nh], ga[:, 2 * chans:]], axis=1)
                rs_in.append(orig.reshape(d, NUM_CHIPS, -1).transpose(1, 0, 2))
            else:
                rs_in.append(gl[n])
        rs_in = [g.reshape(g.shape[0], 2, g.shape[1] // 2, g.shape[2]) for g in rs_in]
        theirs = _split_to_sibling(f"split_grads_{tag}", rs_in)
        return {n: _add_pair(f"add_cores_{tag}_{n}", g, t, core) for n, g, t in zip(names, rs_in, theirs)}

    assert depth == 2
    loss, grad_x, grads, dmods, dfm, (parts, slots) = _local_step(w, x, tgt, mods, fm, chip_partials)
    loss = lax.psum(loss, ("x", "y", "c"))
    left = [(l, n) for l in range(depth) for n in DENSE if (l, n) not in slots]
    for (l, n), p in zip(left, chip_partials([n for _, n in left], grads["layers"][0], "l0_ffn1").values()):
        parts[(l, n)] = p
    slots.update(zip(left, _scatter_chips("scatter_grads", [parts[k] for k in left])))
    order = [(l, n) for l in range(depth) for n in DENSE]
    reduced = [_sum_chips(f"sum_chips_{l}_{n}", slots[(l, n)], parts[(l, n)], chip) for l, n in order]
    other = _join_halves("join_grads", reduced)

    out = {}

    def put(n, res, per_layer):
        for kind, val in zip(("grad_", "delta_", "new_m_", "new_v_"), res):
            out[kind + n] = jnp.stack(val).reshape(a[n].shape) if per_layer else val.reshape(a[n].shape)

    def flat3(v):
        return v.reshape(depth, -1, v.shape[-1])

    assert depth == 2
    nd = len(DENSE)
    for k, n in enumerate(DENSE):
        put(n, _adamw_halves(f"adamw_{n}", (reduced[k], reduced[nd + k]), (other[k], other[nd + k]), core,
                             flat3(a[n]), flat3(a["m_" + n]), flat3(a["v_" + n])), False)

    items = []
    for l in range(depth):
        g = grads["layers"][l]
        items += [g["ffn1_norm"], g["mix_norm"], g["ffn2_norm"], g["rg_small"], _unblockdiag(g["rg_dwa"], bd),
                  _unblockdiag(g["rg_dwx"], bd), g["fox_bf"], g["merge_b"], dmods[l]]
    items += [grads["final_norm"], dfm]
    pack, layout = _pack(items)
    gath, tot = _sum_devices("sum_small", _gather_all("gather_small", pack, own=False), pack,
                             jnp.reshape(dev, (1,)).astype(jnp.int32))

    def wpack(pre):
        its = []
        for l in range(depth):
            rg_rows = jnp.concatenate([jnp.zeros((4, chans), F32), a[pre + "conv_b"][l][None], a[pre + "rg_ba"][l][None],
                                       a[pre + "rg_bx"][l][None], a[pre + "rg_lam"][l][None]], axis=0)
            its += [a[pre + "ffn1_norm"][l], a[pre + "mix_norm"][l], a[pre + "ffn2_norm"][l], rg_rows,
                    a[pre + "rg_wa"][l], a[pre + "rg_wx"][l], a[pre + "fox_bf"][l], a[pre + "merge_b"][l],
                    jnp.zeros((bl, 9 * d), F32)]
        its += [a[pre + "final_norm"], jnp.zeros((bl, 2 * d), F32)]
        return _pack(its)[0]

    res_small = [_unpack(r, layout) for r in [tot] + list(_adamw("adamw_small", tot, wpack(""), wpack("m_"), wpack("v_")))]
    per = len(SMALL) + 1
    for j, n in enumerate(SMALL):
        if n == "rg_small":
            for row, nm in ((4, "conv_b"), (5, "rg_ba"), (6, "rg_bx"), (7, "rg_lam")):
                put(nm, [[r[l * per + j][row] for l in range(depth)] for r in res_small], True)
        else:
            put(n, [[r[l * per + j] for l in range(depth)] for r in res_small], True)
    put("final_norm", [r[depth * per] for r in res_small], False)

    gflat = gath.reshape(NUM_DEVICES, -1)

    def rows_of(idx):
        o, n, shape = layout[idx]
        return gflat[:, o:o + n].reshape(NUM_DEVICES * shape[0], shape[1])

    late_g, ada = [], []
    for l in range(depth):
        dmod_all = rows_of(l * per + per - 1)
        late_g.append(_colsum(f"ada_b_grad_{l}", dmod_all))
        cut = lax.dynamic_slice_in_dim(dmod_all, me * ncol, ncol, axis=1).astype(BF16)
        ada.append(_mm(f"ada_w_grad_{l}", c_b, cut, TN, F32))
    put("ada_w", _adamw_layers("adamw_ada_w", ada[0], ada[1], a["ada_w"], a["m_ada_w"], a["v_ada_w"]), False)
    dfm_all = rows_of(depth * per + 1)
    late_g.append(_colsum("final_ada_b_grad", dfm_all))
    cut = lax.dynamic_slice_in_dim(dfm_all, me * fcol, fcol, axis=1).astype(BF16)
    gl = _mm("final_ada_w_grad", c_b, cut, TN, F32)
    put("final_ada_w", [gl] + list(_adamw("adamw_final_ada_w", gl, a["final_ada_w"], a["m_final_ada_w"],
                                          a["v_final_ada_w"])), False)
    cshard = a["conv_w"].shape[2]
    for l in range(depth):
        rg_tot = res_small[0][l * per + SMALL.index("rg_small")]
        late_g.append(lax.dynamic_slice_in_dim(rg_tot[:4], me * cshard, cshard, axis=1))
    gp2, layout2 = _pack(late_g)

    def wpack2(pre):
        return _pack([a[pre + "ada_b"][l][None] for l in range(depth)] + [a[pre + "final_ada_b"][None]]
                     + [a[pre + "conv_w"][l] for l in range(depth)])[0]

    res_late = [_unpack(r, layout2) for r in [gp2] + list(_adamw("adamw_late", gp2, wpack2(""), wpack2("m_"), wpack2("v_")))]
    put("ada_b", [[r[l] for l in range(depth)] for r in res_late], True)
    put("final_ada_b", [r[depth] for r in res_late], False)
    put("conv_w", [[r[depth + 1 + l] for l in range(depth)] for r in res_late], True)

    outs = [loss, grad_x]
    for kind in ("grad_", "delta_", "new_m_", "new_v_"):
        outs += [out[kind + n] for n in WEIGHTS]
    return tuple(outs)


def kernel(x, c, ffn1_norm, ffn1_w1, ffn1_w3, ffn1_w2, mix_norm, w_in, conv_w, conv_b, rg_wa, rg_ba, rg_wx, rg_bx, rg_lam, fox_bf, merge_b, w_rg, w_sb, w_fox, w_o, ffn2_norm, ffn2_w1, ffn2_w3, ffn2_w2, ada_w, ada_b, final_norm, final_ada_w, final_ada_b, loss_target, m_ffn1_norm, m_ffn1_w1, m_ffn1_w3, m_ffn1_w2, m_mix_norm, m_w_in, m_conv_w, m_conv_b, m_rg_wa, m_rg_ba, m_rg_wx, m_rg_bx, m_rg_lam, m_fox_bf, m_merge_b, m_w_rg, m_w_sb, m_w_fox, m_w_o, m_ffn2_norm, m_ffn2_w1, m_ffn2_w3, m_ffn2_w2, m_ada_w, m_ada_b, m_final_norm, m_final_ada_w, m_final_ada_b, v_ffn1_norm, v_ffn1_w1, v_ffn1_w3, v_ffn1_w2, v_mix_norm, v_w_in, v_conv_w, v_conv_b, v_rg_wa, v_rg_ba, v_rg_wx, v_rg_bx, v_rg_lam, v_fox_bf, v_merge_b, v_w_rg, v_w_sb, v_w_fox, v_w_o, v_ffn2_norm, v_ffn2_w1, v_ffn2_w3, v_ffn2_w2, v_ada_w, v_ada_b, v_final_norm, v_final_ada_w, v_final_ada_b):
    args = dict(locals())
    return _step(args)
```

```python
import math

import jax
import jax.numpy as jnp
from jax import lax
from jax.experimental import pallas as pl
from jax.experimental.pallas import tpu as pltpu

F32 = jnp.float32
BF16 = jnp.bfloat16

NUM_CHIPS = 4
NUM_DEVICES = 8
HEAD_DIM = 64
LANE = 128
SUBLANE = 8
VMEM_LIMIT = 56 * 1024 * 1024
EPS = 1e-6
RG_C = 8.0
ADAM_LR = 0.001
ADAM_B1 = 0.9
ADAM_B2 = 0.999
ADAM_EPS = 1e-08
ADAM_WD = 0.01
ADAM_STEP = 10
MESH = pl.DeviceIdType.MESH
ANY = pl.BlockSpec(memory_space=pl.ANY)
SCALAR = pl.BlockSpec(memory_space=pltpu.SMEM)


def _params():
    return pltpu.CompilerParams(vmem_limit_bytes=VMEM_LIMIT)


def _tile(dim, pref):
    if dim <= pref:
        return dim
    t = (pref // LANE) * LANE
    while t >= LANE:
        if dim % t == 0:
            return t
        t -= LANE
    return dim


def _rtile(rows, pref, unit=2 * SUBLANE):
    if rows <= pref:
        return rows
    t = (pref // unit) * unit
    while t >= unit:
        if rows % t == 0:
            return t
        t -= unit
    return rows


def _sigmoid(x):
    return 0.5 * jnp.tanh(0.5 * x) + 0.5


def _softplus(x):
    return jnp.maximum(x, 0.0) + jnp.log(1.0 + jnp.exp(-jnp.abs(x)))


def _expm1(x):
    small = x * (1.0 + x * (0.5 + x * (1.0 / 6.0 + x * (1.0 / 24.0))))
    return jnp.where(jnp.abs(x) < 0.01, small, jnp.exp(x) - 1.0)


_GELU_K = math.sqrt(2.0 / math.pi)


def _gelu_and_grad(x):
    inner = _GELU_K * (x + 0.044715 * x * x * x)
    t = jnp.tanh(inner)
    val = 0.5 * x * (1.0 + t)
    dinner = _GELU_K * (1.0 + 3.0 * 0.044715 * x * x)
    grad = 0.5 * (1.0 + t) + 0.5 * x * (1.0 - t * t) * dinner
    return val, grad


NN = ((1,), (0,))
NT = ((1,), (1,))
TN = ((0,), (0,))
ALL = slice(None)


def _ride_call(name, body, grid, in_specs, out_specs, out_shape, scratch_shapes, args, riders=()):
    nr = len(riders)
    if not nr:
        return pl.pallas_call(body, name=name, grid=grid, in_specs=in_specs, out_specs=out_specs, out_shape=out_shape,
                              scratch_shapes=scratch_shapes, compiler_params=_params())(*args)
    n_in, n_out, n_scr = len(in_specs), len(out_specs), len(scratch_shapes)

    def hosted(*refs):
        ins, ride_in = refs[:n_in], refs[n_in:n_in + nr]
        outs = refs[n_in + nr:n_in + nr + n_out]
        ride_out = refs[n_in + nr + n_out:n_in + 2 * nr + n_out]
        scr = refs[n_in + 2 * nr + n_out:n_in + 2 * nr + n_out + n_scr]
        send, recv = refs[-2:]
        x, y, c = _mesh_pos()
        copies = []
        for j, (px, py) in enumerate(_other_chips(x, y)):
            for i in range(nr):
                copies.append(pltpu.make_async_remote_copy(
                    src_ref=ride_in[i].at[2 * px + py], dst_ref=ride_out[i].at[j], send_sem=send.at[j * nr + i],
                    recv_sem=recv.at[j * nr + i], device_id=(px, py, c), device_id_type=MESH))
        first = pl.program_id(0) == 0
        last = pl.program_id(0) == grid[0] - 1
        for k in range(1, len(grid)):
            first = first & (pl.program_id(k) == 0)
            last = last & (pl.program_id(k) == grid[k] - 1)

        @pl.when(first)
        def _():
            for cp in copies:
                cp.start()

        body(*ins, *outs, *scr)

        @pl.when(last)
        def _():
            for cp in copies:
                cp.wait()

    slots = [jax.ShapeDtypeStruct((NUM_CHIPS - 1,) + r.shape[1:], r.dtype) for r in riders]
    return pl.pallas_call(
        hosted, name=name, grid=grid, in_specs=list(in_specs) + [ANY] * nr, out_specs=list(out_specs) + [ANY] * nr,
        out_shape=list(out_shape) + slots,
        scratch_shapes=list(scratch_shapes) + [pltpu.SemaphoreType.DMA((3 * nr,)), pltpu.SemaphoreType.DMA((3 * nr,))],
        compiler_params=_params(),
    )(*args, *riders)


def _mmk(name, ops, specs, terms, out_shape, out_dtype, grid, o_spec, acc=None, riders=()):
    n_ops = len(ops)

    def body(*refs):
        o_ref = refs[-1]
        p = None
        for ia, xa, ib, xb, dims in terms:
            t = lax.dot_general(refs[ia][xa], refs[ib][xb], (dims, ((), ())), preferred_element_type=F32)
            p = t if p is None else p + t
        if acc is not None:
            p = p + refs[n_ops][...].astype(F32)
        o_ref[...] = p.astype(o_ref.dtype)

    in_specs = list(specs)
    args = list(ops)
    if acc is not None:
        in_specs.append(pl.BlockSpec(o_spec.block_shape, o_spec.index_map))
        args.append(acc)
    res = _ride_call(name, body, grid, in_specs, [o_spec], [jax.ShapeDtypeStruct(out_shape, out_dtype)], [], args,
                     riders)
    return res if riders else res[0]


def _mmk_resid(name, ops, specs, terms, x, gate, coef, seq, tm, tn):
    t_rows, d = x.shape
    n_ops = len(ops)
    per = seq // tm

    def body(*refs):
        x_ref, g_ref, y_ref, xn_ref = refs[n_ops:]
        p = None
        for ia, xa, ib, xb, dims in terms:
            t = lax.dot_general(refs[ia][xa], refs[ib][xb], (dims, ((), ())), preferred_element_type=F32)
            p = t if p is None else p + t
        y_ref[...] = p
        xn_ref[...] = x_ref[...] + (coef * (1.0 + g_ref[...])) * p

    blk = pl.BlockSpec((tm, tn), lambda i, j: (i, j))
    f = jax.ShapeDtypeStruct((t_rows, d), F32)
    return pl.pallas_call(
        body, name=name, grid=(t_rows // tm, d // tn),
        in_specs=list(specs) + [blk, pl.BlockSpec((None, 1, tn), lambda i, j: (i // per, 0, j))],
        out_specs=[blk, blk], out_shape=[f, f], compiler_params=_params(),
    )(*ops, x, gate)


def _mm(name, a, b, dims, out_dtype, acc=None, tm=512, tn=512):
    if dims == NN:
        (m, kk), n = a.shape, b.shape[1]
    elif dims == NT:
        (m, kk), n = a.shape, b.shape[0]
    else:
        (kk, m), n = a.shape, b.shape[1]
    tm, tn = _tile(m, tm), _tile(n, tn)
    if dims == TN:
        a_spec = pl.BlockSpec((kk, tm), lambda i, j: (0, i))
    else:
        a_spec = pl.BlockSpec((tm, kk), lambda i, j: (i, 0))
    if dims == NT:
        b_spec = pl.BlockSpec((tn, kk), lambda i, j: (j, 0))
    else:
        b_spec = pl.BlockSpec((kk, tn), lambda i, j: (0, j))
    return _mmk(name, [a, b], [a_spec, b_spec], [(0, ALL, 1, ALL, dims)], (m, n), out_dtype,
                (m // tm, n // tn), pl.BlockSpec((tm, tn), lambda i, j: (i, j)), acc)


def _row_grid(t_rows, seq, pref=256):
    tm = _tile(seq, pref)
    return tm, seq // tm


def _normmod(name, x, gain, shift, scale, seq):
    t_rows, d = x.shape
    bl = t_rows // seq
    tm, per = _row_grid(t_rows, seq)

    def body(x_ref, g_ref, sh_ref, sc_ref, o_ref):
        xv = x_ref[...]
        rstd = lax.rsqrt(jnp.mean(xv * xv, axis=-1, keepdims=True) + EPS)
        hn = (xv * rstd) * g_ref[...]
        o_ref[...] = (hn * (1.0 + sc_ref[...]) + sh_ref[...]).astype(o_ref.dtype)

    row = pl.BlockSpec((tm, d), lambda b, i: (b * per + i, 0))
    vec = pl.BlockSpec((None, 1, d), lambda b, i: (b, 0, 0))
    return pl.pallas_call(
        body, name=name, grid=(bl, per),
        in_specs=[row, pl.BlockSpec((1, d), lambda b, i: (0, 0)), vec, vec],
        out_specs=row, out_shape=jax.ShapeDtypeStruct((t_rows, d), BF16),
        compiler_params=_params(),
    )(x, gain, shift, scale)


def _normmod_bwd(name, x, dh, dxo, gain, scale, seq, nxt=None):
    t_rows, d = x.shape
    bl = t_rows // seq
    tm, per = _row_grid(t_rows, seq)

    def body(*refs):
        x_ref, dh_ref, dxo_ref, g_ref, sc_ref = refs[:5]
        if nxt is None:
            dx_ref, dsh_ref, dsc_ref, dg_ref = refs[5:]
        else:
            yn_ref, gn_ref, dx_ref, dsh_ref, dsc_ref, dg_ref, dyn_ref, dgn_ref = refs[5:]
        b, i = pl.program_id(0), pl.program_id(1)
        xv = x_ref[...]
        dhv = dh_ref[...]
        rstd = lax.rsqrt(jnp.mean(xv * xv, axis=-1, keepdims=True) + EPS)
        xhat = xv * rstd
        gain_v = g_ref[...]
        dhn = dhv * (1.0 + sc_ref[...])
        dxhat = dhn * gain_v
        dx = rstd * (dxhat - xhat * jnp.mean(dxhat * xhat, axis=-1, keepdims=True))
        dxt = dxo_ref[...] + dx
        dx_ref[...] = dxt

        @pl.when(i == 0)
        def _():
            dsh_ref[...] = jnp.zeros_like(dsh_ref)
            dsc_ref[...] = jnp.zeros_like(dsc_ref)
            if nxt is not None:
                dgn_ref[...] = jnp.zeros_like(dgn_ref)

        @pl.when((i == 0) & (b == 0))
        def _():
            dg_ref[...] = jnp.zeros_like(dg_ref)

        dsh_ref[...] += jnp.sum(dhv, axis=0, keepdims=True)
        dsc_ref[...] += jnp.sum(dhv * (xhat * gain_v), axis=0, keepdims=True)
        dg_ref[...] += jnp.sum(dhn * xhat, axis=0, keepdims=True)
        if nxt is not None:
            dyn_ref[...] = ((nxt[2] * (1.0 + gn_ref[...])) * dxt).astype(dyn_ref.dtype)
            dgn_ref[...] += jnp.sum((nxt[2] * yn_ref[...]) * dxt, axis=0, keepdims=True)

    row = pl.BlockSpec((tm, d), lambda b, i: (b * per + i, 0))
    vec = pl.BlockSpec((None, 1, d), lambda b, i: (b, 0, 0))
    one = pl.BlockSpec((1, d), lambda b, i: (0, 0))
    f = jax.ShapeDtypeStruct((t_rows, d), F32)
    v = jax.ShapeDtypeStruct((bl, 1, d), F32)
    more = nxt is not None
    return pl.pallas_call(
        body, name=name, grid=(bl, per),
        in_specs=[row, row, row, one, vec] + ([row, vec] if more else []),
        out_specs=[row, vec, vec, one] + ([row, vec] if more else []),
        out_shape=[f, v, v, jax.ShapeDtypeStruct((1, d), F32)]
        + ([jax.ShapeDtypeStruct((t_rows, d), BF16), v] if more else []),
        compiler_params=_params(),
    )(x, dh, dxo, gain, scale, *((nxt[0], nxt[1]) if more else ()))


def _resid_bwd(name, dxo, y, gate, coef, seq):
    t_rows, d = dxo.shape
    bl = t_rows // seq
    tm, per = _row_grid(t_rows, seq)

    def body(dxo_ref, y_ref, g_ref, dy_ref, dg_ref):
        i = pl.program_id(1)
        dxov = dxo_ref[...]
        dy_ref[...] = ((coef * (1.0 + g_ref[...])) * dxov).astype(dy_ref.dtype)

        @pl.when(i == 0)
        def _():
            dg_ref[...] = jnp.zeros_like(dg_ref)

        dg_ref[...] += jnp.sum((coef * y_ref[...]) * dxov, axis=0, keepdims=True)

    row = pl.BlockSpec((tm, d), lambda b, i: (b * per + i, 0))
    vec = pl.BlockSpec((None, 1, d), lambda b, i: (b, 0, 0))
    return pl.pallas_call(
        body, name=name, grid=(bl, per), in_specs=[row, row, vec], out_specs=[row, vec],
        out_shape=[jax.ShapeDtypeStruct((t_rows, d), BF16), jax.ShapeDtypeStruct((bl, 1, d), F32)],
        compiler_params=_params(),
    )(dxo, y, gate)


def _final_loss(name, x, tgt, gain, shift, scale, seq):
    t_rows, d = x.shape
    bl = t_rows // seq
    tm, per = _row_grid(t_rows, seq)

    def body(x_ref, t_ref, g_ref, sh_ref, sc_ref, l_ref, dx_ref, dsh_ref, dsc_ref, dg_ref):
        b, i = pl.program_id(0), pl.program_id(1)
        xv = x_ref[...]
        rstd = lax.rsqrt(jnp.mean(xv * xv, axis=-1, keepdims=True) + EPS)
        xhat = xv * rstd
        gain_v = g_ref[...]
        hn = xhat * gain_v
        yv = hn * (1.0 + sc_ref[...]) + sh_ref[...]
        err = yv - t_ref[...]
        dyv = err * (1.0 / d)
        dhn = dyv * (1.0 + sc_ref[...])
        dxhat = dhn * gain_v
        dx_ref[...] = rstd * (dxhat - xhat * jnp.mean(dxhat * xhat, axis=-1, keepdims=True))

        @pl.when(i == 0)
        def _():
            l_ref[...] = jnp.zeros_like(l_ref)
            dsh_ref[...] = jnp.zeros_like(dsh_ref)
            dsc_ref[...] = jnp.zeros_like(dsc_ref)

        @pl.when((i == 0) & (b == 0))
        def _():
            dg_ref[...] = jnp.zeros_like(dg_ref)

        part = jnp.sum(jnp.sum(err * err, axis=-1, keepdims=True), axis=0, keepdims=True) * (0.5 / d)
        l_ref[...] += jnp.broadcast_to(part, l_ref.shape)
        dsh_ref[...] += jnp.sum(dyv, axis=0, keepdims=True)
        dsc_ref[...] += jnp.sum(dyv * hn, axis=0, keepdims=True)
        dg_ref[...] += jnp.sum(dhn * xhat, axis=0, keepdims=True)

    row = pl.BlockSpec((tm, d), lambda b, i: (b * per + i, 0))
    vec = pl.BlockSpec((None, 1, d), lambda b, i: (b, 0, 0))
    one = pl.BlockSpec((1, d), lambda b, i: (0, 0))
    lvec = pl.BlockSpec((None, 1, LANE), lambda b, i: (b, 0, 0))
    return pl.pallas_call(
        body, name=name, grid=(bl, per),
        in_specs=[row, row, one, vec, vec],
        out_specs=[lvec, row, vec, vec, one],
        out_shape=[jax.ShapeDtypeStruct((bl, 1, LANE), F32), jax.ShapeDtypeStruct((t_rows, d), F32),
                   jax.ShapeDtypeStruct((bl, 1, d), F32), jax.ShapeDtypeStruct((bl, 1, d), F32),
                   jax.ShapeDtypeStruct((1, d), F32)],
        compiler_params=_params(),
    )(x, tgt, gain, shift, scale)


def _ffn_up(name, h, w1, w3, l):
    t_rows, d = h.shape
    ng, fs = w1.shape[0], w1.shape[3]
    tm = _tile(t_rows, 512)

    def body(h_ref, w1_ref, w3_ref, a_ref, b_ref, g_ref):
        hv = h_ref[...]
        av = jnp.dot(hv, w1_ref[...], preferred_element_type=F32)
        bv = jnp.dot(hv, w3_ref[...], preferred_element_type=F32)
        a_ref[...] = av.astype(a_ref.dtype)
        b_ref[...] = bv.astype(b_ref.dtype)
        g_ref[...] = (av * _sigmoid(av) * bv).astype(g_ref.dtype)

    wspec = pl.BlockSpec((None, None, d, fs), lambda g, i: (g, l, 0, 0))
    out = pl.BlockSpec((None, tm, fs), lambda g, i: (g, i, 0))
    f = jax.ShapeDtypeStruct((ng, t_rows, fs), BF16)
    return pl.pallas_call(
        body, name=name, grid=(ng, t_rows // tm),
        in_specs=[pl.BlockSpec((tm, d), lambda g, i: (i, 0)), wspec, wspec], out_specs=[out, out, out],
        out_shape=[f, f, f], compiler_params=_params(),
    )(h, w1, w3)


def _ffn_down_dx(name, dy, w2, a, b, l, riders=()):
    t_rows, d = dy.shape
    ng, fs = w2.shape[0], w2.shape[2]
    tm = _tile(t_rows, 512)

    def body(dy_ref, w2_ref, a_ref, b_ref, da_ref, db_ref):
        dgv = lax.dot_general(dy_ref[...], w2_ref[...], (NT, ((), ())), preferred_element_type=F32)
        av = a_ref[...].astype(F32)
        sig = _sigmoid(av)
        da_ref[...] = (dgv * b_ref[...].astype(F32) * (sig * (1.0 + av * (1.0 - sig)))).astype(da_ref.dtype)
        db_ref[...] = (dgv * (av * sig)).astype(db_ref.dtype)

    blk = pl.BlockSpec((None, tm, fs), lambda g, i: (g, i, 0))
    o = jax.ShapeDtypeStruct((ng, t_rows, fs), BF16)
    return _ride_call(
        name, body, (ng, t_rows // tm),
        [pl.BlockSpec((tm, d), lambda g, i: (i, 0)), pl.BlockSpec((None, None, fs, d), lambda g, i: (g, l, 0, 0)),
         blk, blk],
        [blk, blk], [o, o], [], (dy, w2, a, b), riders)


def _ffn_up_dw(name, h, da, db):
    t_rows, d = h.shape
    ng, fs = da.shape[0], da.shape[2]
    tn = _tile(d, 512)

    def body(h_ref, da_ref, db_ref, o1_ref, o3_ref):
        hv = h_ref[...]
        o1_ref[...] = lax.dot_general(hv, da_ref[...], (TN, ((), ())), preferred_element_type=F32).astype(o1_ref.dtype)
        o3_ref[...] = lax.dot_general(hv, db_ref[...], (TN, ((), ())), preferred_element_type=F32).astype(o3_ref.dtype)

    dspec = pl.BlockSpec((None, t_rows, fs), lambda g, i: (g, 0, 0))
    out = pl.BlockSpec((None, tn, fs), lambda g, i: (g, i, 0))
    o = jax.ShapeDtypeStruct((ng, d, fs), BF16)
    return pl.pallas_call(
        body, name=name, grid=(ng, d // tn),
        in_specs=[pl.BlockSpec((t_rows, tn), lambda g, i: (0, i)), dspec, dspec],
        out_specs=[out, out], out_shape=[o, o], compiler_params=_params(),
    )(h, da, db)


def _ffn_fwd(tag, w, l, pre, x, mod, seq):
    t_rows, d = x.shape
    w1, w3, w2 = w[pre + "w1"], w[pre + "w3"], w[pre + "w2"]
    ng, fs = w1.shape[0], w1.shape[3]
    shift, scale, gate = mod
    h = _normmod(tag + "_norm", x, w[pre + "norm"][l][None], shift, scale, seq)
    a, b, gact = _ffn_up(tag + "_up", h, w1, w3, l)
    tm, tn = _tile(seq, 512), _tile(d, 512)
    y, xn = _mmk_resid(tag + "_down", [gact, w2],
                       [pl.BlockSpec((ng, tm, fs), lambda i, j: (0, i, 0)),
                        pl.BlockSpec((ng, None, fs, tn), lambda i, j: (0, l, 0, j))],
                       [(0, g, 1, g, NN) for g in range(ng)], x, gate, 0.5, seq, tm, tn)
    return xn, (x, h, a, b, gact, y)


def _ffn_bwd(tag, w, l, pre, saved, mod, dxo, seq, riders_dn=(), riders_up=(), done=None, nxt=None):
    x, h, a, b, gact, y = saved
    t_rows, d = x.shape
    w1, w3, w2 = w[pre + "w1"], w[pre + "w3"], w[pre + "w2"]
    ng, fs = w1.shape[0], w1.shape[3]
    shift, scale, gate = mod
    tm, tn = _tile(t_rows, 512), _tile(d, 512)
    dy, dgate = done if done is not None else _resid_bwd(tag + "_res_bwd", dxo, y, gate, 0.5, seq)
    da, db, *slots_dn = _ffn_down_dx(tag + "_down_dx", dy, w2, a, b, l, riders_dn)
    dw2 = _mmk(tag + "_down_dw", [gact, dy],
               [pl.BlockSpec((None, t_rows, fs), lambda g, j: (g, 0, 0)),
                pl.BlockSpec((t_rows, tn), lambda g, j: (0, j))],
               [(0, ALL, 1, ALL, TN)], (ng, fs, d), BF16, (ng, d // tn),
               pl.BlockSpec((None, fs, tn), lambda g, j: (g, 0, j)))
    dw1, dw3 = _ffn_up_dw(tag + "_up_dw", h, da, db)
    dspec = pl.BlockSpec((ng, tm, fs), lambda i, j: (0, i, 0))
    wspec = pl.BlockSpec((ng, None, tn, fs), lambda i, j: (0, l, j, 0))
    dh = _mmk(tag + "_up_dx", [da, db, w1, w3], [dspec, dspec, wspec, wspec],
              [(0, g, 2, g, NT) for g in range(ng)] + [(1, g, 3, g, NT) for g in range(ng)],
              (t_rows, d), F32, (t_rows // tm, d // tn), pl.BlockSpec((tm, tn), lambda i, j: (i, j)),
              riders=riders_up)
    slots_up = []
    if riders_up:
        dh, *slots_up = dh
    dx, dshift, dscale, dgain, *post = _normmod_bwd(tag + "_norm_bwd", x, dh, dxo, w[pre + "norm"][l][None], scale,
                                                    seq, nxt)
    grads = {pre + "w1": dw1, pre + "w3": dw3, pre + "w2": dw2, pre + "norm": dgain}
    return dx, (dshift, dscale, dgate), grads, slots_dn, slots_up, (tuple(post) or None)


def _shift_down(v, s, row):
    if s == 0:
        return v
    return jnp.where(row >= s, pltpu.roll(v, s, 0), 0.0)


def _shift_up(v, s, row):
    if s == 0:
        return v
    n = v.shape[0]
    return jnp.where(row < n - s, pltpu.roll(v, n - s, 0), 0.0)


def _scan_fwd(a, u, row):
    n = a.shape[0]
    s = 1
    while s < n:
        ok = row >= s
        a_sh = pltpu.roll(a, s, 0)
        u_sh = pltpu.roll(u, s, 0)
        u = jnp.where(ok, a * u_sh + u, u)
        a = jnp.where(ok, a * a_sh, a)
        s *= 2
    return u


def _scan_bwd(a_next, g, row):
    n = g.shape[0]
    a, u = a_next, g
    s = 1
    while s < n:
        ok = row < n - s
        a_sh = pltpu.roll(a, n - s, 0)
        u_sh = pltpu.roll(u, n - s, 0)
        u = jnp.where(ok, a * u_sh + u, u)
        a = jnp.where(ok, a * a_sh, a)
        s *= 2
    return u


def _rg_specs(seq, cw):
    slab = lambda off: pl.BlockSpec((seq, cw), lambda c, b: (b, off + c))
    par = lambda rows: pl.BlockSpec((rows, cw), lambda c, b: (0, c))
    wbd = pl.BlockSpec((None, cw, cw), lambda c, b: (c, 0, 0))
    return slab, par, wbd


def _rg_fwd(name, proj, p, seq, chans):
    t_rows = proj.shape[0]
    bl = t_rows // seq
    cw = LANE
    nc = chans // cw
    slab, par, wbd = _rg_specs(seq, cw)

    def body(x_ref, gt_ref, cw_ref, cb_ref, wa_ref, ba_ref, wx_ref, bx_ref, lam_ref,
             xa_ref, r_ref, i_ref, h_ref, ya_ref):
        row = lax.broadcasted_iota(jnp.int32, (seq, cw), 0)
        xv = x_ref[...]
        xa = jnp.zeros_like(xv) + cb_ref[...]
        for k in range(4):
            xa = xa + cw_ref[k:k + 1, :] * _shift_down(xv, 3 - k, row)
        xab = xa.astype(BF16)
        r = _sigmoid(jnp.dot(xab, wa_ref[...], preferred_element_type=F32) + ba_ref[...])
        ig = _sigmoid(jnp.dot(xab, wx_ref[...], preferred_element_type=F32) + bx_ref[...])
        log_a = (-RG_C) * r * _softplus(-lam_ref[...])
        a = jnp.exp(log_a)
        u = jnp.sqrt(-_expm1(2.0 * log_a)) * (ig * xa)
        h = _scan_fwd(a, u, row)
        gel, _ = _gelu_and_grad(gt_ref[...])
        xa_ref[...] = xa
        r_ref[...] = r
        i_ref[...] = ig
        h_ref[...] = h
        ya_ref[...] = (gel * h).astype(ya_ref.dtype)

    out = pl.BlockSpec((seq, cw), lambda c, b: (b, c))
    f = jax.ShapeDtypeStruct((t_rows, chans), F32)
    return pl.pallas_call(
        body, name=name, grid=(nc, bl),
        in_specs=[slab(0), slab(nc), par(4), par(1), wbd, par(1), wbd, par(1), par(1)],
        out_specs=[out] * 5,
        out_shape=[f, f, f, f, jax.ShapeDtypeStruct((t_rows, chans), BF16)],
        compiler_params=_params(),
    )(proj, proj, p["conv_w"], p["conv_b"], p["wa"], p["ba"], p["wx"], p["bx"], p["lam"])


def _rg_bwd(name, proj, dya, saved, p, seq, chans):
    xa_s, r_s, i_s, h_s = saved
    t_rows = proj.shape[0]
    bl = t_rows // seq
    cw = LANE
    nc = chans // cw
    slab, par, wbd = _rg_specs(seq, cw)

    def body(x_ref, gt_ref, dya_ref, xa_ref, r_ref, i_ref, h_ref, cw_ref, wa_ref, wx_ref, lam_ref,
             dx_ref, dgt_ref, sm_ref, dwa_ref, dwx_ref):
        b = pl.program_id(1)
        row = lax.broadcasted_iota(jnp.int32, (seq, cw), 0)
        xv, xa, r, ig, h = x_ref[...], xa_ref[...], r_ref[...], i_ref[...], h_ref[...]
        dyav = dya_ref[...]
        gel, dgel = _gelu_and_grad(gt_ref[...])
        dgt_ref[...] = (dyav * h * dgel).astype(dgt_ref.dtype)
        dh = dyav * gel
        lam = lam_ref[...]
        sp = _softplus(-lam)
        log_a = (-RG_C) * r * sp
        a = jnp.exp(log_a)
        s = jnp.sqrt(-_expm1(2.0 * log_a))
        lamb = _scan_bwd(_shift_up(a, 1, row), dh, row)
        da = lamb * _shift_down(h, 1, row)
        xi = ig * xa
        ds = lamb * xi
        dxi = lamb * s
        dlog = da * a - ds * (a * a) / s
        dr = dlog * ((-RG_C) * sp)
        dsp = jnp.sum(dlog * ((-RG_C) * r), axis=0, keepdims=True)
        dlam = -dsp * _sigmoid(-lam)
        dzr = dr * r * (1.0 - r)
        dzi = (dxi * xa) * ig * (1.0 - ig)
        dzrb, dzib, xab = dzr.astype(BF16), dzi.astype(BF16), xa.astype(BF16)
        dxa = dxi * ig
        dxa = dxa + lax.dot_general(dzrb, wa_ref[...], (NT, ((), ())), preferred_element_type=F32)
        dxa = dxa + lax.dot_general(dzib, wx_ref[...], (NT, ((), ())), preferred_element_type=F32)
        dwa = lax.dot_general(xab, dzrb, (TN, ((), ())), preferred_element_type=F32)
        dwx = lax.dot_general(xab, dzib, (TN, ((), ())), preferred_element_type=F32)
        dxv = jnp.zeros_like(xv)
        rows = []
        for k in range(4):
            dxv = dxv + cw_ref[k:k + 1, :] * _shift_up(dxa, 3 - k, row)
            rows.append(jnp.sum(dxa * _shift_down(xv, 3 - k, row), axis=0, keepdims=True))
        dx_ref[...] = dxv.astype(dx_ref.dtype)
        rows += [jnp.sum(dxa, axis=0, keepdims=True), jnp.sum(dzr, axis=0, keepdims=True),
                 jnp.sum(dzi, axis=0, keepdims=True), dlam]

        @pl.when(b == 0)
        def _():
            sm_ref[...] = jnp.zeros_like(sm_ref)
            dwa_ref[...] = jnp.zeros_like(dwa_ref)
            dwx_ref[...] = jnp.zeros_like(dwx_ref)

        for k, val in enumerate(rows):
            sm_ref[k:k + 1, :] += val
        dwa_ref[...] += dwa
        dwx_ref[...] += dwx

    plain = pl.BlockSpec((seq, cw), lambda c, b: (b, c))
    return pl.pallas_call(
        body, name=name, grid=(nc, bl),
        in_specs=[slab(0), slab(nc), plain, plain, plain, plain, plain, par(4), wbd, wbd, par(1)],
        out_specs=[plain, plain, par(8), wbd, wbd],
        out_shape=[jax.ShapeDtypeStruct((t_rows, chans), BF16), jax.ShapeDtypeStruct((t_rows, chans), BF16),
                   jax.ShapeDtypeStruct((8, chans), F32),
                   jax.ShapeDtypeStruct((nc, cw, cw), F32), jax.ShapeDtypeStruct((nc, cw, cw), F32)],
        compiler_params=_params(),
    )(proj, proj, dya, xa_s, r_s, i_s, h_s, p["conv_w"], p["wa"], p["wx"], p["lam"])


ATT_Q_BLOCK = 512
SB_K_BLOCK = 256
FOX_K_BLOCK = 512
PAIR = LANE // HEAD_DIM
NEG = -1e30
SCALE = HEAD_DIM ** -0.5
assert math.log2(HEAD_DIM) % 2 == 0


def _att_blocks(seq, k_block):
    return _tile(seq, ATT_Q_BLOCK), _tile(seq, k_block)


def _key_blocks(qi, tq, bk):
    return (qi * tq) // bk, (qi * tq + tq - 1) // bk + 1


def _tri(n, kind):
    r = lax.broadcasted_iota(jnp.int32, (2 * n, n), 0)
    r = jnp.where(r >= n, r - n, r)
    c = lax.broadcasted_iota(jnp.int32, (2 * n, n), 1)
    m = {"gt": r > c, "le": r <= c, "lt": r < c}[kind]
    return m.astype(BF16)


def _cumsum_mm(v, tri):
    hi = v.astype(BF16)
    lo = (v - hi.astype(F32)).astype(BF16)
    return jnp.dot(jnp.concatenate([hi, lo], axis=1), tri, preferred_element_type=F32)


def _head_masks():
    lane = lax.broadcasted_iota(jnp.int32, (1, LANE), 1)
    return [(lane >= h * HEAD_DIM) & (lane < (h + 1) * HEAD_DIM) for h in range(PAIR)]


def _only(mask, v):
    return jnp.where(mask, v, jnp.zeros_like(v))


def _stack_heads(v, masks):
    return jnp.concatenate([_only(m, v) for m in masks], axis=0)


def _unstack_heads(v, masks):
    tq = v.shape[0] // PAIR
    out = _only(masks[0], v[0:tq])
    for h in range(1, PAIR):
        out = out + _only(masks[h], v[h * tq:(h + 1) * tq])
    return out


def _stacked_iotas(tq, bk):
    row = lax.broadcasted_iota(jnp.int32, (PAIR * tq, bk), 0)
    for h in range(1, PAIR):
        row = jnp.where(row >= h * tq, row - tq, row)
    return row, lax.broadcasted_iota(jnp.int32, (PAIR * tq, bk), 1)


def _att_specs(seq, blk, nq, off):
    npair = None
    qs = lambda o: pl.BlockSpec((blk, LANE), lambda b, p, i: (b * nq + i, o + p))
    ks = lambda o: pl.BlockSpec((seq, LANE), lambda b, p, i: (b, o + p))
    col = pl.BlockSpec((None, PAIR, blk, 1), lambda b, p, i: (b, p, i, 0))
    lane = pl.BlockSpec((None, PAIR, 1, seq), lambda b, p, i: (b, p, 0, 0))
    return qs, ks, col, lane


def _sb_fwd(name, qkv, off, width, bl, seq):
    t_rows = qkv.shape[0]
    tq, bk = _att_blocks(seq, SB_K_BLOCK)
    nq = seq // tq
    nb = width // LANE
    qs, ks, col, _ = _att_specs(seq, tq, nq, off)

    def body(q_ref, k_ref, v_ref, o_ref, lt_ref):
        qi = pl.program_id(2)
        masks = _head_masks()
        qs_ = _stack_heads(q_ref[...] * SCALE, masks)
        row, cix = _stacked_iotas(tq, bk)
        tri = _tri(bk, "gt")

        def step(masked, top):
            def go(it, carry):
                acc, cl = carry
                kb = top - it
                ks_ = pl.multiple_of(kb * bk, bk)
                kv = k_ref[pl.ds(ks_, bk), :]
                vv = v_ref[pl.ds(ks_, bk), :]
                strict = (kb * bk + cix) < (qi * tq + row)
                z = lax.dot_general(qs_, kv, (NT, ((), ())), preferred_element_type=F32)
                sp = _softplus(z)
                lk = jnp.where(strict, -sp, 0.0) if masked else -sp
                wgt = jnp.exp(z - sp + (cl + _cumsum_mm(lk, tri)))
                if masked:
                    wgt = jnp.where(strict, wgt, 0.0)
                acc = acc + _unstack_heads(jnp.dot(wgt.astype(BF16), vv, preferred_element_type=F32), masks)
                return acc, cl + jnp.sum(lk, axis=1, keepdims=True)
            return go

        n_full, n_all = _key_blocks(qi, tq, bk)
        carry = (jnp.zeros((tq, LANE), F32), jnp.zeros((PAIR * tq, 1), F32))
        carry = lax.fori_loop(0, n_all - n_full, step(True, n_all - 1), carry)
        acc, cl = lax.fori_loop(0, n_full, step(False, n_full - 1), carry)
        o_ref[...] = acc.astype(o_ref.dtype)
        for h in range(PAIR):
            lt_ref[h] = cl[h * tq:(h + 1) * tq]

    return pl.pallas_call(
        body, name=name, grid=(bl, nb, nq), in_specs=[qs(off), ks(off + nb), ks(off + 2 * nb)],
        out_specs=[qs(0), col],
        out_shape=[jax.ShapeDtypeStruct((t_rows, width), BF16),
                   jax.ShapeDtypeStruct((bl, nb * PAIR, seq, 1), F32)],
        compiler_params=_params(),
    )(qkv, qkv, qkv)


def _sb_bwd(name, qkv, off, width, bl, seq, ltot, do, riders=()):
    t_rows = qkv.shape[0]
    tq, bk = _att_blocks(seq, SB_K_BLOCK)
    nq = seq // tq
    nb = width // LANE
    qs, ks, col, _ = _att_specs(seq, tq, nq, off)

    def body(q_ref, k_ref, v_ref, lt_ref, do_ref, dq_ref, dk_ref, dv_ref, dk_acc, dv_acc):
        qi = pl.program_id(2)

        @pl.when(qi == 0)
        def _():
            dk_acc[...] = jnp.zeros_like(dk_acc)
            dv_acc[...] = jnp.zeros_like(dv_acc)

        masks = _head_masks()
        qs_ = _stack_heads(q_ref[...] * SCALE, masks)
        dos = _stack_heads(do_ref[...].astype(BF16), masks)
        lts = jnp.concatenate([lt_ref[h] for h in range(PAIR)], axis=0)
        row, cix = _stacked_iotas(tq, bk)
        tri_le = _tri(bk, "le")
        tri_lt = _tri(bk, "lt")

        def step(masked):
            def go(kb, carry):
                dq, cl, ce = carry
                ks_ = pl.multiple_of(kb * bk, bk)
                kv = k_ref[pl.ds(ks_, bk), :]
                vv = v_ref[pl.ds(ks_, bk), :]
                strict = (kb * bk + cix) < (qi * tq + row)
                z = lax.dot_general(qs_, kv, (NT, ((), ())), preferred_element_type=F32)
                sp = _softplus(z)
                lk = jnp.where(strict, -sp, 0.0) if masked else -sp
                sig = jnp.exp(z - sp)
                wgt = sig * jnp.exp(lts - cl - _cumsum_mm(lk, tri_le))
                if masked:
                    wgt = jnp.where(strict, wgt, 0.0)
                dw = lax.dot_general(dos, vv, (NT, ((), ())), preferred_element_type=F32)
                e = dw * wgt
                pre = ce + _cumsum_mm(e, tri_lt)
                dz = e * (1.0 - sig) - pre * sig
                if masked:
                    dz = jnp.where(strict, dz, 0.0)
                dzb = dz.astype(BF16)
                dq = dq + _unstack_heads(jnp.dot(dzb, kv * SCALE, preferred_element_type=F32), masks)
                dk_acc[pl.ds(ks_, bk), :] += lax.dot_general(dzb, qs_, (TN, ((), ())), preferred_element_type=F32)
                dv_acc[pl.ds(ks_, bk), :] += lax.dot_general(wgt.astype(BF16), dos, (TN, ((), ())),
                                                             preferred_element_type=F32)
                return dq, cl + jnp.sum(lk, axis=1, keepdims=True), ce + jnp.sum(e, axis=1, keepdims=True)
            return go

        n_full, n_all = _key_blocks(qi, tq, bk)
        zero = jnp.zeros((PAIR * tq, 1), F32)
        carry = lax.fori_loop(0, n_full, step(False), (jnp.zeros((tq, LANE), F32), zero, zero))
        dq, _, _ = lax.fori_loop(n_full, n_all, step(True), carry)
        dq_ref[...] = dq.astype(dq_ref.dtype)

        @pl.when(qi == nq - 1)
        def _():
            dk_ref[...] = dk_acc[...].astype(dk_ref.dtype)
            dv_ref[...] = dv_acc[...].astype(dv_ref.dtype)

    o = jax.ShapeDtypeStruct((t_rows, width), BF16)
    return _ride_call(
        name, body, (bl, nb, nq), [qs(off), ks(off + nb), ks(off + 2 * nb), col, qs(0)], [qs(0), ks(0), ks(0)],
        [o, o, o], [pltpu.VMEM((seq, LANE), F32), pltpu.VMEM((seq, LANE), F32)], (qkv, qkv, qkv, ltot, do), riders)


def _fox_fwd(name, qkv, off, width, bl, seq, cum_q, cum_k):
    t_rows = qkv.shape[0]
    tq, bk = _att_blocks(seq, FOX_K_BLOCK)
    nq = seq // tq
    nb = width // LANE
    qs, ks, col, lane = _att_specs(seq, tq, nq, off)

    def body(q_ref, k_ref, v_ref, cq_ref, ck_ref, ob_ref, of_ref, lse_ref):
        qi = pl.program_id(2)
        masks = _head_masks()
        qs_ = _stack_heads(q_ref[...] * SCALE, masks)
        cqs = jnp.concatenate([cq_ref[h] for h in range(PAIR)], axis=0)
        row, cix = _stacked_iotas(tq, bk)

        def step(masked):
            def go(kb, carry):
                m, lsum, acc = carry
                ks_ = pl.multiple_of(kb * bk, bk)
                kv = k_ref[pl.ds(ks_, bk), :]
                vv = v_ref[pl.ds(ks_, bk), :]
                bias = jnp.concatenate([cqs[h * tq:(h + 1) * tq] - ck_ref[h, :, pl.ds(ks_, bk)] for h in range(PAIR)],
                                       axis=0)
                z = lax.dot_general(qs_, kv, (NT, ((), ())), preferred_element_type=F32) + bias
                if masked:
                    z = jnp.where((kb * bk + cix) <= (qi * tq + row), z, NEG)
                m_new = jnp.maximum(m, jnp.max(z, axis=1, keepdims=True))
                pv = jnp.exp(z - m_new)
                alpha = jnp.exp(m - m_new)
                lsum = alpha * lsum + jnp.sum(pv, axis=1, keepdims=True)
                acc = alpha * acc + jnp.dot(pv.astype(BF16), vv, preferred_element_type=F32)
                return m_new, lsum, acc
            return go

        n_full, n_all = _key_blocks(qi, tq, bk)
        init = (jnp.full((PAIR * tq, 1), NEG, F32), jnp.zeros((PAIR * tq, 1), F32),
                jnp.zeros((PAIR * tq, LANE), F32))
        carry = lax.fori_loop(0, n_full, step(False), init)
        m, lsum, acc = lax.fori_loop(n_full, n_all, step(True), carry)
        out = _unstack_heads(acc / lsum, masks)
        ob_ref[...] = out.astype(ob_ref.dtype)
        of_ref[...] = out
        lse = m + jnp.log(lsum)
        for h in range(PAIR):
            lse_ref[h] = lse[h * tq:(h + 1) * tq]

    return pl.pallas_call(
        body, name=name, grid=(bl, nb, nq),
        in_specs=[qs(off), ks(off + nb), ks(off + 2 * nb), col, lane], out_specs=[qs(0), qs(0), col],
        out_shape=[jax.ShapeDtypeStruct((t_rows, width), BF16), jax.ShapeDtypeStruct((t_rows, width), F32),
                   jax.ShapeDtypeStruct((bl, nb * PAIR, seq, 1), F32)],
        compiler_params=_params(),
    )(qkv, qkv, qkv, cum_q, cum_k)


def _fox_bwd(name, qkv, off, width, bl, seq, cum_q, cum_k, lse, o, do, riders=()):
    t_rows = qkv.shape[0]
    tq, bk = _att_blocks(seq, FOX_K_BLOCK)
    nq = seq // tq
    nb = width // LANE
    qs, ks, col, lane = _att_specs(seq, tq, nq, off)

    def body(q_ref, k_ref, v_ref, cq_ref, ck_ref, lse_ref, o_ref, do_ref,
             dq_ref, dk_ref, dv_ref, dcq_ref, dck_ref, dk_acc, dv_acc):
        qi = pl.program_id(2)

        @pl.when(qi == 0)
        def _():
            dk_acc[...] = jnp.zeros_like(dk_acc)
            dv_acc[...] = jnp.zeros_like(dv_acc)
            dck_ref[...] = jnp.zeros_like(dck_ref)

        masks = _head_masks()
        qs_ = _stack_heads(q_ref[...] * SCALE, masks)
        dof = do_ref[...]
        dos = _stack_heads(dof.astype(BF16), masks)
        prod = dof * o_ref[...]
        delta = jnp.concatenate([jnp.sum(_only(m, prod), axis=1, keepdims=True) for m in masks], axis=0)
        shift = jnp.concatenate([cq_ref[h] - lse_ref[h] for h in range(PAIR)], axis=0)
        row, cix = _stacked_iotas(tq, bk)

        def step(masked):
            def go(kb, carry):
                dq, dcq = carry
                ks_ = pl.multiple_of(kb * bk, bk)
                kv = k_ref[pl.ds(ks_, bk), :]
                vv = v_ref[pl.ds(ks_, bk), :]
                bias = jnp.concatenate(
                    [shift[h * tq:(h + 1) * tq] - ck_ref[h, :, pl.ds(ks_, bk)] for h in range(PAIR)], axis=0)
                pv = jnp.exp(lax.dot_general(qs_, kv, (NT, ((), ())), preferred_element_type=F32) + bias)
                if masked:
                    pv = jnp.where((kb * bk + cix) <= (qi * tq + row), pv, 0.0)
                dp = lax.dot_general(dos, vv, (NT, ((), ())), preferred_element_type=F32)
                ds = pv * (dp - delta)
                dsb = ds.astype(BF16)
                dq = dq + _unstack_heads(jnp.dot(dsb, kv * SCALE, preferred_element_type=F32), masks)
                dk_acc[pl.ds(ks_, bk), :] += lax.dot_general(dsb, qs_, (TN, ((), ())), preferred_element_type=F32)
                dv_acc[pl.ds(ks_, bk), :] += lax.dot_general(pv.astype(BF16), dos, (TN, ((), ())),
                                                             preferred_element_type=F32)
                for h in range(PAIR):
                    dck_ref[h, :, pl.ds(ks_, bk)] += -jnp.sum(ds[h * tq:(h + 1) * tq], axis=0, keepdims=True)
                return dq, dcq + jnp.sum(ds, axis=1, keepdims=True)
            return go

        n_full, n_all = _key_blocks(qi, tq, bk)
        carry = lax.fori_loop(0, n_full, step(False), (jnp.zeros((tq, LANE), F32), jnp.zeros((PAIR * tq, 1), F32)))
        dq, dcq = lax.fori_loop(n_full, n_all, step(True), carry)
        dq_ref[...] = dq.astype(dq_ref.dtype)
        for h in range(PAIR):
            dcq_ref[h] = dcq[h * tq:(h + 1) * tq]

        @pl.when(qi == nq - 1)
        def _():
            dk_ref[...] = dk_acc[...].astype(dk_ref.dtype)
            dv_ref[...] = dv_acc[...].astype(dv_ref.dtype)

    ob = jax.ShapeDtypeStruct((t_rows, width), BF16)
    nh = nb * PAIR
    return _ride_call(
        name, body, (bl, nb, nq),
        [qs(off), ks(off + nb), ks(off + 2 * nb), col, lane, col, qs(0), qs(0)],
        [qs(0), ks(0), ks(0), col, lane],
        [ob, ob, ob, jax.ShapeDtypeStruct((bl, nh, seq, 1), F32), jax.ShapeDtypeStruct((bl, nh, 1, seq), F32)],
        [pltpu.VMEM((seq, LANE), F32), pltpu.VMEM((seq, LANE), F32)],
        (qkv, qkv, qkv, cum_q, cum_k, lse, o, do), riders)


def _lane_cumsum(v, reverse):
    n = v.shape[1]
    cix = lax.broadcasted_iota(jnp.int32, v.shape, 1)
    s = 1
    while s < n:
        if reverse:
            v = v + jnp.where(cix < n - s, pltpu.roll(v, n - s, 1), 0.0)
        else:
            v = v + jnp.where(cix >= s, pltpu.roll(v, s, 1), 0.0)
        s *= 2
    return v


def _forget_cum(name, fl, bf):
    def body(fl_ref, bf_ref, o_ref):
        xv = fl_ref[...] + bf_ref[...]
        o_ref[...] = _lane_cumsum(-_softplus(-xv), False)

    return pl.pallas_call(body, name=name, out_shape=jax.ShapeDtypeStruct(fl.shape, F32),
                          compiler_params=_params())(fl, bf)


def _forget_cum_bwd(name, fl, bf, dcum, nh):
    rows = fl.shape[0]

    def body(fl_ref, bf_ref, dc_ref, dfl_ref, dbf_ref):
        xv = fl_ref[...] + bf_ref[...]
        dlogf = _lane_cumsum(dc_ref[...], True)
        dfl = dlogf * _sigmoid(-xv)
        dfl_ref[...] = dfl
        per_row = jnp.sum(dfl, axis=1, keepdims=True)
        tot = per_row[0:nh]
        for b in range(1, rows // nh):
            tot = tot + per_row[b * nh:(b + 1) * nh]
        dbf_ref[...] = tot

    return pl.pallas_call(
        body, name=name,
        out_shape=[jax.ShapeDtypeStruct(fl.shape, F32), jax.ShapeDtypeStruct((nh, 1), F32)],
        compiler_params=_params(),
    )(fl, bf, dcum)


def _merge_fwd(name, proj, off, merge_b, pa, pb, pc):
    t_rows, d = pa.shape
    tm = _tile(t_rows, 256)

    def body(l0, l1, l2, mb, a_ref, b_ref, c_ref, o_ref):
        g0 = _sigmoid(l0[...] + mb[:, 0:d])
        g1 = _sigmoid(l1[...] + mb[:, d:2 * d])
        g2 = _sigmoid(l2[...] + mb[:, 2 * d:3 * d])
        o_ref[...] = (g0 * a_ref[...] + g1 * b_ref[...] + g2 * c_ref[...]).astype(o_ref.dtype)

    row = pl.BlockSpec((tm, d), lambda i: (i, 0))
    lg = lambda j: pl.BlockSpec((tm, d), lambda i: (i, off + j))
    return pl.pallas_call(
        body, name=name, grid=(t_rows // tm,),
        in_specs=[lg(0), lg(1), lg(2), pl.BlockSpec((1, 3 * d), lambda i: (0, 0)), row, row, row],
        out_specs=row, out_shape=jax.ShapeDtypeStruct((t_rows, d), BF16), compiler_params=_params(),
    )(proj, proj, proj, merge_b, pa, pb, pc)


def _merge_bwd(name, proj, off, merge_b, pa, pb, pc, dmixed):
    t_rows, d = pa.shape
    tm = _tile(t_rows, 256)

    def body(l0, l1, l2, mb, a_ref, b_ref, c_ref, dm_ref, da_ref, db_ref, dc_ref, dl_ref, dmb_ref):
        i = pl.program_id(0)
        dm = dm_ref[...]
        parts = []
        for j, (lref, pref, dref) in enumerate(((l0, a_ref, da_ref), (l1, b_ref, db_ref), (l2, c_ref, dc_ref))):
            g = _sigmoid(lref[...] + mb[:, j * d:(j + 1) * d])
            dref[...] = (g * dm).astype(dref.dtype)
            dl = dm * pref[...] * g * (1.0 - g)
            dl_ref[:, j * d:(j + 1) * d] = dl.astype(dl_ref.dtype)
            parts.append(jnp.sum(dl, axis=0, keepdims=True))
        tot = jnp.concatenate(parts, axis=1)

        @pl.when(i == 0)
        def _():
            dmb_ref[...] = tot

        @pl.when(i > 0)
        def _():
            dmb_ref[...] += tot

    row = pl.BlockSpec((tm, d), lambda i: (i, 0))
    lg = lambda j: pl.BlockSpec((tm, d), lambda i: (i, off + j))
    one = pl.BlockSpec((1, 3 * d), lambda i: (0, 0))
    b16 = jax.ShapeDtypeStruct((t_rows, d), BF16)
    return pl.pallas_call(
        body, name=name, grid=(t_rows // tm,),
        in_specs=[lg(0), lg(1), lg(2), one, row, row, row, row],
        out_specs=[row, row, row, pl.BlockSpec((tm, 3 * d), lambda i: (i, 0)), one],
        out_shape=[b16, b16, b16, jax.ShapeDtypeStruct((t_rows, 3 * d), BF16), jax.ShapeDtypeStruct((1, 3 * d), F32)],
        compiler_params=_params(),
    )(proj, proj, proj, merge_b, pa, pb, pc, dmixed)


def _grouped_nn(name, a, wg, l, out_dtype):
    t_rows, kk = a.shape
    ng, ncol = wg.shape[0], wg.shape[3]
    tm = _tile(t_rows, 512)
    return _mmk(name, [a, wg],
                [pl.BlockSpec((tm, kk), lambda i, g: (i, 0)),
                 pl.BlockSpec((None, None, kk, ncol), lambda i, g: (g, l, 0, 0))],
                [(0, ALL, 1, ALL, NN)], (t_rows, ng * ncol), out_dtype, (t_rows // tm, ng),
                pl.BlockSpec((tm, ncol), lambda i, g: (i, g)))


def _grouped_nt(name, da, wg, l, out_dtype):
    t_rows = da.shape[0]
    ng, kk, ncol = wg.shape[0], wg.shape[2], wg.shape[3]
    tm = _tile(t_rows, 512)
    return _mmk(name, [da, wg],
                [pl.BlockSpec((tm, ng * ncol), lambda i: (i, 0)),
                 pl.BlockSpec((ng, None, kk, ncol), lambda i: (0, l, 0, 0))],
                [(0, (ALL, slice(g * ncol, (g + 1) * ncol)), 1, g, NT) for g in range(ng)],
                (t_rows, kk), out_dtype, (t_rows // tm,), pl.BlockSpec((tm, kk), lambda i: (i, 0)))


def _grouped_tn(name, a, da, ng, out_dtype):
    t_rows, kk = a.shape
    ncol = da.shape[1] // ng
    return _mmk(name, [a, da],
                [pl.BlockSpec((t_rows, kk), lambda g: (0, 0)), pl.BlockSpec((t_rows, ncol), lambda g: (0, g))],
                [(0, ALL, 1, ALL, TN)], (ng, kk, ncol), out_dtype, (ng,),
                pl.BlockSpec((None, kk, ncol), lambda g: (g, 0, 0)))


def _rows_nn(name, a, wr, l, out_dtype):
    t_rows = a.shape[0]
    ng, kg, n = wr.shape[0], wr.shape[2], wr.shape[3]
    tm, tn = _tile(t_rows, 512), _tile(n, 512)
    return _mmk(name, [a, wr],
                [pl.BlockSpec((tm, ng * kg), lambda i, j: (i, 0)),
                 pl.BlockSpec((ng, None, kg, tn), lambda i, j: (0, l, 0, j))],
                [(0, (ALL, slice(g * kg, (g + 1) * kg)), 1, g, NN) for g in range(ng)],
                (t_rows, n), out_dtype, (t_rows // tm, n // tn), pl.BlockSpec((tm, tn), lambda i, j: (i, j)))


def _rows_nt(name, dy, wr, l, out_dtype):
    t_rows, n = dy.shape
    ng, kg = wr.shape[0], wr.shape[2]
    tm = _tile(t_rows, 512)
    return _mmk(name, [dy, wr],
                [pl.BlockSpec((tm, n), lambda i, g: (i, 0)),
                 pl.BlockSpec((None, None, kg, n), lambda i, g: (g, l, 0, 0))],
                [(0, ALL, 1, ALL, NT)], (t_rows, ng * kg), out_dtype, (t_rows // tm, ng),
                pl.BlockSpec((tm, kg), lambda i, g: (i, g)))


def _rows_tn(name, a, dy, ng, out_dtype):
    t_rows, n = dy.shape
    kg = a.shape[1] // ng
    tn = _tile(n, 512)
    return _mmk(name, [a, dy],
                [pl.BlockSpec((t_rows, kg), lambda g, j: (0, g)), pl.BlockSpec((t_rows, tn), lambda g, j: (0, j))],
                [(0, ALL, 1, ALL, TN)], (ng, kg, n), out_dtype, (ng, n // tn),
                pl.BlockSpec((None, kg, tn), lambda g, j: (g, 0, j)))


def _mix_fwd(tag, w, l, x, mod, seq):
    t_rows, d = x.shape
    bl = t_rows // seq
    shift, scale, gate = mod
    chans, nh = w["layout"]["chans"], w["layout"]["heads"]
    width = nh * HEAD_DIM
    nb = width // LANE
    h = _normmod(tag + "_norm", x, w["mix_norm"][l][None], shift, scale, seq)
    proj = _mm(tag + "_in_a", h, w["w_a"][l], NN, F32)
    qkv = _mm(tag + "_in_b", h, w["w_b"][l], NN, BF16)
    flp = _mm(tag + "_in_f", h, w["w_f"][l], NN, F32)
    xa, r, ig, hs, ya = _rg_fwd(tag + "_rg", proj, w["rg"][l], seq, chans)
    yb, ltot = _sb_fwd(tag + "_sb", qkv, 0, width, bl, seq)
    fl = flp[:, :nh].reshape(bl, seq, nh).transpose(0, 2, 1).reshape(bl * nh, seq)
    bf = jnp.tile(w["fox_bf"][l].reshape(nh, 1), (bl, 1))
    cum = _forget_cum(tag + "_cum", fl, bf)
    cum_q = cum.reshape(bl, nh, seq, 1)
    cum_k = cum.reshape(bl, nh, 1, seq)
    yc, oc, lse = _fox_fwd(tag + "_fox", qkv, 3 * nb, width, bl, seq, cum_q, cum_k)
    pa = _rows_nn(tag + "_prg", ya, w["w_rg"], l, F32)
    pb = _grouped_nn(tag + "_psb", yb, w["w_sb"], l, F32)
    pc = _grouped_nn(tag + "_pfox", yc, w["w_fox"], l, F32)
    moff = 2 * chans // d
    mb = w["merge_b"][l][None]
    mixed = _merge_fwd(tag + "_merge", proj, moff, mb, pa, pb, pc)
    w_o = w["w_o"]
    ngo, kgo = w_o.shape[0], w_o.shape[2]
    tm, tn = _tile(seq, 512), _tile(d, 512)
    y, xn = _mmk_resid(tag + "_out", [mixed, w_o],
                       [pl.BlockSpec((tm, ngo * kgo), lambda i, j: (i, 0)),
                        pl.BlockSpec((ngo, None, kgo, tn), lambda i, j: (0, l, 0, j))],
                       [(0, (ALL, slice(g * kgo, (g + 1) * kgo)), 1, g, NN) for g in range(ngo)],
                       x, gate, 1.0, seq, tm, tn)
    saved = dict(x=x, h=h, proj=proj, qkv=qkv, rg=(xa, r, ig, hs), ya=ya, ltot=ltot,
                 fox=(cum_q, cum_k, lse, oc), fl=fl, bf=bf, yb=yb, yc=yc, pa=pa, pb=pb, pc=pc, mixed=mixed, y=y)
    return xn, saved


def _mix_bwd(tag, w, l, s, mod, dxo, seq, riders=(), riders_fox=(), done=None, nxt=None):
    x = s["x"]
    t_rows, d = x.shape
    bl = t_rows // seq
    shift, scale, gate = mod
    chans, nh = w["layout"]["chans"], w["layout"]["heads"]
    width = nh * HEAD_DIM
    nb = width // LANE
    moff = 2 * chans // d
    mb = w["merge_b"][l][None]
    ng = NUM_CHIPS
    dy, dgate = done if done is not None else _resid_bwd(tag + "_res_bwd", dxo, s["y"], gate, 1.0, seq)
    dmixed = _rows_nt(tag + "_out_dx", dy, w["w_o"], l, F32)
    dw_o = _rows_tn(tag + "_out_dw", s["mixed"], dy, ng, BF16)
    dpa, dpb, dpc, dlog, dmb = _merge_bwd(tag + "_merge_bwd", s["proj"], moff, mb, s["pa"], s["pb"], s["pc"], dmixed)
    dya = _rows_nt(tag + "_prg_dx", dpa, w["w_rg"], l, F32)
    dw_rg = _rows_tn(tag + "_prg_dw", s["ya"], dpa, ng, BF16)
    dyb = _grouped_nt(tag + "_psb_dx", dpb, w["w_sb"], l, F32)
    dw_sb = _grouped_tn(tag + "_psb_dw", s["yb"], dpb, ng, BF16)
    dyc = _grouped_nt(tag + "_pfox_dx", dpc, w["w_fox"], l, F32)
    dw_fox = _grouped_tn(tag + "_pfox_dw", s["yc"], dpc, ng, BF16)
    qkv = s["qkv"]
    dq_b, dk_b, dv_b, *slots = _sb_bwd(tag + "_sb_bwd", qkv, 0, width, bl, seq, s["ltot"], dyb, riders)
    cum_q, cum_k, lse, oc = s["fox"]
    dq_c, dk_c, dv_c, dcq, dck, *slots_fox = _fox_bwd(tag + "_fox_bwd", qkv, 3 * nb, width, bl, seq, cum_q, cum_k,
                                                      lse, oc, dyc, riders_fox)
    dcum = dcq.reshape(bl * nh, seq) + dck.reshape(bl * nh, seq)
    dfl, dbf = _forget_cum_bwd(tag + "_cum_bwd", s["fl"], s["bf"], dcum, nh)
    dfl_t = dfl.reshape(bl, nh, seq).transpose(0, 2, 1).reshape(t_rows, nh)
    dflp = jnp.pad(dfl_t, ((0, 0), (0, LANE - nh))).astype(BF16)
    drgx, dgt, rg_small, dwa, dwx = _rg_bwd(tag + "_rg_bwd", s["proj"], dya, s["rg"], w["rg"][l], seq, chans)
    dproj = jnp.concatenate([drgx, dgt, dlog], axis=1)
    dqkv = jnp.concatenate([dq_b, dk_b, dv_b, dq_c, dk_c, dv_c], axis=1)
    w_a, w_b, w_f = w["w_a"][l], w["w_b"][l], w["w_f"][l]
    pa_w, pb_w = w_a.shape[1], w_b.shape[1]
    tm, tn = _tile(t_rows, 512), _tile(d, 512)
    rows = lambda n: pl.BlockSpec((tm, n), lambda i, j: (i, 0))
    wrow = lambda n: pl.BlockSpec((tn, n), lambda i, j: (j, 0))
    dh = _mmk(tag + "_in_dx", [dproj, dqkv, dflp, w_a, w_b, w_f],
              [rows(pa_w), rows(pb_w), rows(LANE), wrow(pa_w), wrow(pb_w), wrow(LANE)],
              [(0, ALL, 3, ALL, NT), (1, ALL, 4, ALL, NT), (2, ALL, 5, ALL, NT)],
              (t_rows, d), F32, (t_rows // tm, d // tn), pl.BlockSpec((tm, tn), lambda i, j: (i, j)))
    hb = s["h"]
    dw_a = _mm(tag + "_in_a_dw", hb, dproj, TN, BF16)
    dw_b = _mm(tag + "_in_b_dw", hb, dqkv, TN, BF16)
    dw_f = _mm(tag + "_in_f_dw", hb, dflp, TN, BF16)
    dx, dshift, dscale, dgain, *post = _normmod_bwd(tag + "_norm_bwd", x, dh, dxo, w["mix_norm"][l][None], scale, seq,
                                                    nxt)
    grads = dict(w_in=(dw_a, dw_b, dw_f), w_rg=dw_rg, w_sb=dw_sb, w_fox=dw_fox, w_o=dw_o, mix_norm=dgain,
                 rg_small=rg_small, rg_dwa=dwa, rg_dwx=dwx, fox_bf=dbf, merge_b=dmb)
    return dx, (dshift, dscale, dgate), grads, slots, slots_fox, (tuple(post) or None)


def _silu(name, c):
    def body(c_ref, o_ref):
        v = c_ref[...]
        o_ref[...] = v * _sigmoid(v)

    return pl.pallas_call(body, name=name, out_shape=jax.ShapeDtypeStruct(c.shape, F32),
                          compiler_params=_params())(c)


def _blockdiag(wb):
    nb, bd, _ = wb.shape
    per = LANE // bd
    t = wb.reshape(nb // per, per, bd, 1, bd)
    eye = jnp.eye(per, dtype=wb.dtype).reshape(1, per, 1, per, 1)
    return (t * eye).reshape(nb // per, LANE, LANE).astype(BF16)


def _unblockdiag(t, bd):
    n = t.shape[0]
    per = LANE // bd
    t5 = t.reshape(n, per, bd, per, bd)
    return jnp.stack([t5[:, p, :, p, :] for p in range(per)], axis=1).reshape(n * per, bd, bd)


def _prepare(gw, a, d, chans, nh):
    depth = a["ada_b"].shape[0]
    wq = 3 * nh * HEAD_DIM
    o_m = 2 * chans + 2 * wq
    w = {"layout": dict(chans=chans, heads=nh)}
    for n in ("ffn1_w1", "ffn1_w3", "ffn1_w2", "ffn2_w1", "ffn2_w3", "ffn2_w2", "w_rg", "w_sb", "w_fox", "w_o"):
        w[n] = gw[n]
    for n in ("ffn1_norm", "ffn2_norm", "mix_norm", "fox_bf", "merge_b", "final_norm"):
        w[n] = a[n]
    w_a, w_b, w_f, rg = [], [], [], []
    for l in range(depth):
        full = gw["w_in"][:, l].transpose(1, 0, 2).reshape(d, -1)
        w_a.append(jnp.concatenate([full[:, :2 * chans], full[:, o_m + nh:]], axis=1))
        w_b.append(full[:, 2 * chans:o_m])
        w_f.append(jnp.pad(full[:, o_m:o_m + nh], ((0, 0), (0, LANE - nh))))
        conv_w = gw["conv_w"][:, l].transpose(1, 0, 2).reshape(-1, chans)
        rg.append(dict(conv_w=conv_w, conv_b=a["conv_b"][l][None], ba=a["rg_ba"][l][None], bx=a["rg_bx"][l][None],
                       lam=a["rg_lam"][l][None], wa=_blockdiag(a["rg_wa"][l]), wx=_blockdiag(a["rg_wx"][l])))
    w["w_a"], w["w_b"], w["w_f"], w["rg"] = w_a, w_b, w_f, rg
    return w


def _local_step(w, x, tgt, mods, fm, ride=None):
    bl, seq, d = x.shape
    t_rows = bl * seq
    depth = len(mods)
    mod3 = []
    for l in range(depth):
        m4 = mods[l].reshape(bl, 9, 1, d)
        mod3.append([(m4[:, 3 * k], m4[:, 3 * k + 1], m4[:, 3 * k + 2]) for k in range(3)])
    fm4 = fm.reshape(bl, 2, 1, d)
    saved = []
    xc = x.reshape(t_rows, d)
    for l in range(depth):
        xc, s1 = _ffn_fwd(f"l{l}_ffn1", w, l, "ffn1_", xc, mod3[l][0], seq)
        xc, s2 = _mix_fwd(f"l{l}_mix", w, l, xc, mod3[l][1], seq)
        xc, s3 = _ffn_fwd(f"l{l}_ffn2", w, l, "ffn2_", xc, mod3[l][2], seq)
        saved.append((s1, s2, s3))
    lpart, dx, dfs, dfc, dfg = _final_loss("final", xc, tgt.reshape(t_rows, d), w["final_norm"][None],
                                           fm4[:, 0], fm4[:, 1], seq)
    loss = jnp.sum(lpart[:, 0, 0])
    grads = {"final_norm": dfg, "layers": [None] * depth}
    dmods = [None] * depth
    parts, slots = {}, {}
    queue = {"sb": [], "fox": [], "dn": [], "up": []}

    def take(host):
        keys, queue[host] = queue[host], []
        return keys, [parts[k] for k in keys]

    def enqueue(l, names, gl, tag, hosts):
        for n, p in ride(names, gl, tag).items():
            parts[(l, n)] = p
            queue[hosts(n)].append((l, n))

    done = None
    for l in reversed(range(depth)):
        s1, s2, s3 = saved[l]
        dx, dm3, g3, _, _, done = _ffn_bwd(f"l{l}_ffn2", w, l, "ffn2_", s3, mod3[l][2], dx, seq, done=done,
                                           nxt=(s2["y"], mod3[l][1][2], 1.0))
        if ride is not None and l == 0:
            enqueue(l, FFN2, g3, "l0_ffn2", lambda n: "sb")
        (k_sb, r_sb), (k_fox, r_fox) = take("sb"), take("fox")
        dx, dm2, g2, s_sb, s_fox, done = _mix_bwd(f"l{l}_mix", w, l, s2, mod3[l][1], dx, seq, r_sb, r_fox, done=done,
                                                  nxt=(s1[5], mod3[l][0][2], 0.5))
        slots.update(zip(k_sb + k_fox, list(s_sb) + list(s_fox)))
        if ride is not None and l == 0:
            enqueue(l, MIXER, g2, "l0_mix", lambda n: "up" if n == "w_in" else "dn")
        (k_dn, r_dn), (k_up, r_up) = take("dn"), take("up")
        below = (saved[l - 1][2][5], mod3[l - 1][2][2], 0.5) if l > 0 else None
        dx, dm1, g1, s_dn, s_up, done = _ffn_bwd(f"l{l}_ffn1", w, l, "ffn1_", s1, mod3[l][0], dx, seq, r_dn, r_up,
                                                 done=done, nxt=below)
        slots.update(zip(k_dn + k_up, list(s_dn) + list(s_up)))
        dmods[l] = jnp.concatenate([*dm1, *dm2, *dm3], axis=1).reshape(bl, 9 * d)
        grads["layers"][l] = {**g1, **g2, **g3}
        if ride is not None and l > 0:
            enqueue(l, DENSE, grads["layers"][l], f"l{l}", lambda n: "fox" if n == "w_in" else "sb")
    dfm = jnp.concatenate([dfs, dfc], axis=1).reshape(bl, 2 * d)
    return loss, dx.reshape(bl, seq, d), grads, dmods, dfm, (parts, slots)


def _mesh_pos():
    return lax.axis_index("x"), lax.axis_index("y"), lax.axis_index("c")


def _other_chips(x, y):
    return ((1 - x, y), (x, 1 - y), (1 - x, 1 - y))


def _gather_two_level(name, arrs):
    n = len(arrs)

    def body(*refs):
        ins, outs = refs[:n], refs[n:2 * n]
        send, recv, send2, recv2, send3, recv3 = refs[2 * n:]
        x, y, c = _mesh_pos()
        me = 2 * x + y
        chips = _other_chips(x, y)
        sib = (x, y, 1 - c)
        own = [pltpu.make_async_remote_copy(
            src_ref=ins[i], dst_ref=outs[i].at[me], send_sem=send3.at[i], recv_sem=recv3.at[i],
            device_id=sib, device_id_type=MESH) for i in range(n)]
        first = []
        for j, (px, py) in enumerate(chips):
            for i in range(n):
                first.append(pltpu.make_async_remote_copy(
                    src_ref=ins[i].at[c], dst_ref=outs[i].at[me, c], send_sem=send.at[j * n + i],
                    recv_sem=recv.at[j * n + i], device_id=(px, py, c), device_id_type=MESH))
        for cp in first + own:
            cp.start()
        passed = []
        for j, (px, py) in enumerate(chips):
            for i in range(n):
                landed = outs[i].at[2 * px + py, c]
                pltpu.make_async_remote_copy(
                    src_ref=ins[i].at[c], dst_ref=landed, send_sem=send.at[j * n + i],
                    recv_sem=recv.at[j * n + i], device_id=(px, py, c), device_id_type=MESH).wait_recv()
                fwd = pltpu.make_async_remote_copy(
                    src_ref=landed, dst_ref=landed, send_sem=send2.at[j * n + i],
                    recv_sem=recv2.at[j * n + i], device_id=sib, device_id_type=MESH)
                fwd.start()
                passed.append(fwd)
        for j, (px, py) in enumerate(chips):
            for i in range(n):
                theirs = outs[i].at[2 * px + py, 1 - c]
                pltpu.make_async_remote_copy(
                    src_ref=theirs, dst_ref=theirs, send_sem=send2.at[j * n + i],
                    recv_sem=recv2.at[j * n + i], device_id=sib, device_id_type=MESH).wait_recv()
        for cp in first + passed:
            cp.wait_send()
        for cp in own:
            cp.wait()

    return pl.pallas_call(
        body, name=name, in_specs=[ANY] * n, out_specs=[ANY] * n,
        out_shape=[jax.ShapeDtypeStruct((NUM_CHIPS,) + a.shape, a.dtype) for a in arrs],
        scratch_shapes=[pltpu.SemaphoreType.DMA((3 * n,)), pltpu.SemaphoreType.DMA((3 * n,)),
                        pltpu.SemaphoreType.DMA((3 * n,)), pltpu.SemaphoreType.DMA((3 * n,)),
                        pltpu.SemaphoreType.DMA((n,)), pltpu.SemaphoreType.DMA((n,))],
    )(*arrs)


def _split_to_sibling(name, arrs):
    n = len(arrs)
    slabs = arrs[0].shape[0]

    def body(*refs):
        ins, theirs = refs[:n], refs[n:2 * n]
        send, recv = refs[2 * n:]
        x, y, c = _mesh_pos()
        sib = (x, y, 1 - c)
        for i in range(n):
            for s in range(slabs):
                pltpu.make_async_remote_copy(
                    src_ref=ins[i].at[s, 1 - c], dst_ref=theirs[i].at[s], send_sem=send.at[i],
                    recv_sem=recv.at[i], device_id=sib, device_id_type=MESH).start()
        for i in range(n):
            pltpu.make_async_remote_copy(
                src_ref=ins[i].at[:, 0], dst_ref=theirs[i], send_sem=send.at[i], recv_sem=recv.at[i],
                device_id=sib, device_id_type=MESH).wait()

    return pl.pallas_call(
        body, name=name, in_specs=[ANY] * n, out_specs=[ANY] * n,
        out_shape=[jax.ShapeDtypeStruct((a.shape[0],) + a.shape[2:], a.dtype) for a in arrs],
        scratch_shapes=[pltpu.SemaphoreType.DMA((n,)), pltpu.SemaphoreType.DMA((n,))],
    )(*arrs)


def _scatter_chips(name, arrs):
    n = len(arrs)

    def body(*refs):
        ins, outs = refs[:n], refs[n:2 * n]
        send, recv = refs[2 * n:]
        x, y, c = _mesh_pos()
        chips = _other_chips(x, y)
        sends = []
        for j, (px, py) in enumerate(chips):
            for i in range(n):
                sends.append(pltpu.make_async_remote_copy(
                    src_ref=ins[i].at[2 * px + py], dst_ref=outs[i].at[j], send_sem=send.at[j * n + i],
                    recv_sem=recv.at[j * n + i], device_id=(px, py, c), device_id_type=MESH))
        for s in sends:
            s.start()
        for s in sends:
            s.wait()

    return pl.pallas_call(
        body, name=name, in_specs=[ANY] * n, out_specs=[ANY] * n,
        out_shape=[jax.ShapeDtypeStruct((NUM_CHIPS - 1,) + a.shape[1:], a.dtype) for a in arrs],
        scratch_shapes=[pltpu.SemaphoreType.DMA((3 * n,)), pltpu.SemaphoreType.DMA((3 * n,))],
    )(*arrs)


def _join_halves(name, arrs):
    n = len(arrs)

    def body(*refs):
        ins, outs = refs[:n], refs[n:2 * n]
        send, recv = refs[2 * n:]
        x, y, c = _mesh_pos()
        copies = [pltpu.make_async_remote_copy(
            src_ref=ins[i], dst_ref=outs[i], send_sem=send.at[i], recv_sem=recv.at[i],
            device_id=(x, y, 1 - c), device_id_type=MESH) for i in range(n)]
        for cp in copies:
            cp.start()
        for cp in copies:
            cp.wait()

    return pl.pallas_call(
        body, name=name, in_specs=[ANY] * n, out_specs=[ANY] * n,
        out_shape=[jax.ShapeDtypeStruct(a.shape, a.dtype) for a in arrs],
        scratch_shapes=[pltpu.SemaphoreType.DMA((n,)), pltpu.SemaphoreType.DMA((n,))],
    )(*arrs)


def _gather_all(name, pack, own=True):
    def body(in_ref, out_ref, send, recv, loc):
        x, y, c = _mesh_pos()
        me = 4 * x + 2 * y + c
        mine = pltpu.make_async_copy(in_ref, out_ref.at[me], loc)
        if own:
            mine.start()
        peers = []
        for mask in range(1, NUM_DEVICES):
            px = 1 - x if mask & 4 else x
            py = 1 - y if mask & 2 else y
            pc = 1 - c if mask & 1 else c
            peers.append((px, py, pc))
        sends = [pltpu.make_async_remote_copy(
            src_ref=in_ref, dst_ref=out_ref.at[me], send_sem=send.at[k], recv_sem=recv.at[k],
            device_id=p, device_id_type=MESH) for k, p in enumerate(peers)]
        for s in sends:
            s.start()
        for k, (px, py, pc) in enumerate(peers):
            pltpu.make_async_remote_copy(
                src_ref=in_ref, dst_ref=out_ref.at[4 * px + 2 * py + pc], send_sem=send.at[k], recv_sem=recv.at[k],
                device_id=(px, py, pc), device_id_type=MESH).wait_recv()
        for s in sends:
            s.wait_send()
        if own:
            mine.wait()

    return pl.pallas_call(
        body, name=name, in_specs=[ANY], out_specs=ANY,
        out_shape=jax.ShapeDtypeStruct((NUM_DEVICES,) + pack.shape, pack.dtype),
        scratch_shapes=[pltpu.SemaphoreType.DMA((NUM_DEVICES - 1,)), pltpu.SemaphoreType.DMA((NUM_DEVICES - 1,)),
                        pltpu.SemaphoreType.DMA],
    )(pack)


def _sum_devices(name, slots, pack, dev):
    g, rows, cols = slots.shape
    tr = _rtile(rows, 256)

    def body(d_ref, s_ref, p_ref, full_ref, o_ref):
        acc = None
        for k in range(g):
            v = jnp.where(d_ref[0] == k, p_ref[...], s_ref[k])
            full_ref[k] = v
            acc = v if acc is None else acc + v
        o_ref[...] = acc

    blk = pl.BlockSpec((g, tr, cols), lambda i: (0, i, 0))
    row = pl.BlockSpec((tr, cols), lambda i: (i, 0))
    return pl.pallas_call(
        body, name=name, grid=(rows // tr,), in_specs=[SCALAR, blk, row], out_specs=[blk, row],
        out_shape=[jax.ShapeDtypeStruct(slots.shape, F32), jax.ShapeDtypeStruct((rows, cols), F32)],
        compiler_params=_params(),
    )(dev, slots, pack)


def _add_pair(name, p, q, core):
    g, _, rows, cols = p.shape
    tr = _rtile(rows, 128)

    def body(c_ref, p_ref, q_ref, o_ref):
        mine = jnp.where(c_ref[0] == 0, p_ref[:, 0].astype(F32), p_ref[:, 1].astype(F32))
        o_ref[...] = (mine + q_ref[...].astype(F32)).astype(o_ref.dtype)

    spec = pl.BlockSpec((g, tr, cols), lambda i: (0, i, 0))
    return pl.pallas_call(
        body, name=name, grid=(rows // tr,),
        in_specs=[SCALAR, pl.BlockSpec((g, 2, tr, cols), lambda i: (0, 0, i, 0)), spec],
        out_specs=spec, out_shape=jax.ShapeDtypeStruct(q.shape, BF16), compiler_params=_params(),
    )(core, p, q)


def _sum_chips(name, slots, part, chip):
    g, rows, cols = part.shape
    tr = _rtile(rows, 128)

    def body(c_ref, s_ref, p_ref, o_ref):
        acc = p_ref[c_ref[0]].astype(F32)
        for k in range(slots.shape[0]):
            acc = acc + s_ref[k].astype(F32)
        o_ref[...] = acc

    return pl.pallas_call(
        body, name=name, grid=(rows // tr,),
        in_specs=[SCALAR,
                  pl.BlockSpec((slots.shape[0], tr, cols), lambda i: (0, i, 0)),
                  pl.BlockSpec((g, tr, cols), lambda i: (0, i, 0))],
        out_specs=pl.BlockSpec((tr, cols), lambda i: (i, 0)),
        out_shape=jax.ShapeDtypeStruct((rows, cols), F32), compiler_params=_params(),
    )(chip, slots, part)


def _adamw(name, g, w, m, v, l=None):
    rows, cols = g.shape
    tr = _rtile(rows, 128)

    def body(g_ref, w_ref, m_ref, v_ref, d_o, m_o, v_o):
        gv = g_ref[...]
        mn = ADAM_B1 * m_ref[...] + (1.0 - ADAM_B1) * gv
        vn = ADAM_B2 * v_ref[...] + (1.0 - ADAM_B2) * (gv * gv)
        m_hat = mn / (1.0 - ADAM_B1 ** ADAM_STEP)
        v_hat = vn / (1.0 - ADAM_B2 ** ADAM_STEP)
        d_o[...] = -ADAM_LR * (m_hat / (jnp.sqrt(v_hat) + ADAM_EPS) + ADAM_WD * w_ref[...])
        m_o[...] = mn
        v_o[...] = vn

    gspec = pl.BlockSpec((tr, cols), lambda i: (i, 0))
    wspec = gspec if l is None else pl.BlockSpec((None, tr, cols), lambda i: (l, i, 0))
    f = jax.ShapeDtypeStruct((rows, cols), F32)
    return pl.pallas_call(
        body, name=name, grid=(rows // tr,), in_specs=[gspec] + [wspec] * 3, out_specs=[gspec] * 3,
        out_shape=[f] * 3, compiler_params=_params(),
    )(g, w, m, v)


def _adamw_layers(name, g0, g1, w, m, v):
    rows, cols = g0.shape
    tr = _rtile(rows, 128)
    nt = rows // tr

    def body(g0_ref, g1_ref, w_ref, m_ref, v_ref, g_o, d_o, m_o, v_o):
        gv = jnp.where(pl.program_id(0) == 0, g0_ref[...], g1_ref[...])
        _adamw_math(gv, w_ref, m_ref, v_ref, g_o, d_o, m_o, v_o)

    g0spec = pl.BlockSpec((tr, cols), lambda l, i: (i * (1 - l) + (nt - 1) * l, 0))
    g1spec = pl.BlockSpec((tr, cols), lambda l, i: (i * l, 0))
    wspec = pl.BlockSpec((None, tr, cols), lambda l, i: (l, i, 0))
    f = jax.ShapeDtypeStruct((2, rows, cols), F32)
    return pl.pallas_call(
        body, name=name, grid=(2, nt), in_specs=[g0spec, g1spec, wspec, wspec, wspec], out_specs=[wspec] * 4,
        out_shape=[f] * 4, compiler_params=_params(),
    )(g0, g1, w, m, v)


def _adamw_math(gv, w_ref, m_ref, v_ref, g_o, d_o, m_o, v_o):
    mn = ADAM_B1 * m_ref[...] + (1.0 - ADAM_B1) * gv
    vn = ADAM_B2 * v_ref[...] + (1.0 - ADAM_B2) * (gv * gv)
    m_hat = mn / (1.0 - ADAM_B1 ** ADAM_STEP)
    v_hat = vn / (1.0 - ADAM_B2 ** ADAM_STEP)
    g_o[...] = gv
    d_o[...] = -ADAM_LR * (m_hat / (jnp.sqrt(v_hat) + ADAM_EPS) + ADAM_WD * w_ref[...])
    m_o[...] = mn
    v_o[...] = vn


def _adamw_halves(name, mine, theirs, core, w, m, v):
    half, cols = mine[0].shape
    tr = _rtile(half, 128)
    nt = half // tr

    def body(c_ref, a0, b0, a1, b1, w_ref, m_ref, v_ref, g_o, d_o, m_o, v_o):
        first = pl.program_id(0) == 0
        own = pl.program_id(1) == c_ref[0]
        gv = jnp.where(first, jnp.where(own, a0[...], b0[...]), jnp.where(own, a1[...], b1[...]))
        _adamw_math(gv, w_ref, m_ref, v_ref, g_o, d_o, m_o, v_o)

    lay0 = pl.BlockSpec((tr, cols), lambda l, h, i: (i * (1 - l) + (nt - 1) * l, 0))
    lay1 = pl.BlockSpec((tr, cols), lambda l, h, i: (i * l, 0))
    wspec = pl.BlockSpec((None, tr, cols), lambda l, h, i: (l, h * nt + i, 0))
    f = jax.ShapeDtypeStruct((2, 2 * half, cols), F32)
    return pl.pallas_call(
        body, name=name, grid=(2, 2, nt), in_specs=[SCALAR, lay0, lay0, lay1, lay1, wspec, wspec, wspec],
        out_specs=[wspec] * 4, out_shape=[f] * 4, compiler_params=_params(),
    )(core, mine[0], theirs[0], mine[1], theirs[1], w, m, v)


def _colsum(name, a):
    def body(a_ref, o_ref):
        o_ref[...] = jnp.sum(a_ref[...], axis=0, keepdims=True)

    return pl.pallas_call(body, name=name, out_shape=jax.ShapeDtypeStruct((1, a.shape[1]), F32),
                          compiler_params=_params())(a)


PACK_UNIT = SUBLANE * LANE


def _pack(items):
    flat, layout, o = [], [], 0
    for it in items:
        n = it.size
        pad = -n % PACK_UNIT
        flat.append(jnp.pad(it.reshape(-1).astype(F32), (0, pad)))
        layout.append((o, n, it.shape))
        o += n + pad
    return jnp.concatenate(flat).reshape(-1, LANE), layout


def _unpack(pack, layout):
    flat = pack.reshape(-1)
    return [flat[o:o + n].reshape(shape) for o, n, shape in layout]


WEIGHTS = ("ffn1_norm", "ffn1_w1", "ffn1_w3", "ffn1_w2", "mix_norm", "w_in", "conv_w", "conv_b", "rg_wa", "rg_ba",
           "rg_wx", "rg_bx", "rg_lam", "fox_bf", "merge_b", "w_rg", "w_sb", "w_fox", "w_o", "ffn2_norm", "ffn2_w1",
           "ffn2_w3", "ffn2_w2", "ada_w", "ada_b", "final_norm", "final_ada_w", "final_ada_b")
DENSE = ("ffn1_w1", "ffn1_w3", "ffn1_w2", "w_in", "w_rg", "w_sb", "w_fox", "w_o", "ffn2_w1", "ffn2_w3", "ffn2_w2")
FFN1 = ("ffn1_w1", "ffn1_w3", "ffn1_w2")
FFN2 = ("ffn2_w1", "ffn2_w3", "ffn2_w2")
MIXER = ("w_in", "w_rg", "w_sb", "w_fox", "w_o")
SMALL = ("ffn1_norm", "mix_norm", "ffn2_norm", "rg_small", "rg_wa", "rg_wx", "fox_bf", "merge_b")


def _step(a):
    x, c, tgt = a["x"], a["c"], a["loss_target"]
    bl, seq, d = x.shape
    depth, nh = a["fox_bf"].shape
    chans = a["rg_lam"].shape[1]
    bd = a["rg_wa"].shape[2]
    wq = 3 * nh * HEAD_DIM
    o_m = 2 * chans + 2 * wq
    batch = NUM_DEVICES * bl
    mx, my, mc = _mesh_pos()
    me = 2 * mx + my
    dev = 4 * mx + 2 * my + mc

    c_rows = -(-bl * d // LANE // SUBLANE) * SUBLANE
    c_pack = jnp.pad(c.reshape(-1, LANE), ((0, c_rows - bl * d // LANE), (0, 0)))
    c_all = _gather_all("gather_c", c_pack)[:, :bl * d // LANE].reshape(batch, d)
    c_act = _silu("c_act", c_all)
    c_b = c_act.astype(BF16)
    ncol, fcol = a["ada_w"].shape[2], a["final_ada_w"].shape[1]
    cols = []
    for l in range(depth):
        bias = jnp.broadcast_to(lax.dynamic_slice_in_dim(a["ada_b"][l], me * ncol, ncol)[None], (batch, ncol))
        cols.append(_mm(f"ada{l}", c_b, a["ada_w"][l].astype(BF16), NN, F32, acc=bias))
    bias = jnp.broadcast_to(lax.dynamic_slice_in_dim(a["final_ada_b"], me * fcol, fcol)[None], (batch, fcol))
    cols.append(_mm("ada_final", c_b, a["final_ada_w"].astype(BF16), NN, F32, acc=bias))
    mod_cols = jnp.concatenate(cols, axis=1).reshape(2, batch // 2, depth * ncol + fcol)

    names = DENSE + ("conv_w", "mod_cols")
    got = _gather_two_level("gather_weights", [a[n].astype(BF16) for n in DENSE] + [a["conv_w"], mod_cols])
    gw = dict(zip(names, got))
    w = _prepare(gw, a, d, chans, nh)
    mod_all = gw["mod_cols"].reshape(NUM_CHIPS, batch, -1)
    mine = lambda full: lax.dynamic_slice_in_dim(full, dev * bl, bl, axis=0)
    mods = [mine(mod_all[:, :, l * ncol:(l + 1) * ncol].transpose(1, 0, 2).reshape(batch, NUM_CHIPS * ncol))
            for l in range(depth)]
    fm = mine(mod_all[:, :, depth * ncol:].transpose(1, 0, 2).reshape(batch, NUM_CHIPS * fcol))

    core = jnp.reshape(mc, (1,)).astype(jnp.int32)
    chip = jnp.reshape(me, (1,)).astype(jnp.int32)

    def chip_partials(names, gl, tag):
        rs_in = []
        for n in names:
            if n == "w_in":
                ga, gb, gf = gl["w_in"]
                orig = jnp.concatenate([ga[:, :2 * chans], gb, gf[:, :nh], ga[:, 2 * chans:]], axis=1)
                rs_in.append(orig.reshape(d, NUM_CHIPS, -1).transpose(1, 0, 2))
            else:
                rs_in.append(gl[n])
        rs_in = [g.reshape(g.shape[0], 2, g.shape[1] // 2, g.shape[2]) for g in rs_in]
        theirs = _split_to_sibling(f"split_grads_{tag}", rs_in)
        return {n: _add_pair(f"add_cores_{tag}_{n}", g, t, core) for n, g, t in zip(names, rs_in, theirs)}

    assert depth == 2
    loss, grad_x, grads, dmods, dfm, (parts, slots) = _local_step(w, x, tgt, mods, fm, chip_partials)
    loss = lax.psum(loss, ("x", "y", "c"))
    left = [(l, n) for l in range(depth) for n in DENSE if (l, n) not in slots]
    for (l, n), p in zip(left, chip_partials([n for _, n in left], grads["layers"][0], "l0_ffn1").values()):
        parts[(l, n)] = p
    slots.update(zip(left, _scatter_chips("scatter_grads", [parts[k] for k in left])))
    order = [(l, n) for l in range(depth) for n in DENSE]
    reduced = [_sum_chips(f"sum_chips_{l}_{n}", slots[(l, n)], parts[(l, n)], chip) for l, n in order]
    other = _join_halves("join_grads", reduced)

    out = {}

    def put(n, res, per_layer):
        for kind, val in zip(("grad_", "delta_", "new_m_", "new_v_"), res):
            out[kind + n] = jnp.stack(val).reshape(a[n].shape) if per_layer else val.reshape(a[n].shape)

    def flat3(v):
        return v.reshape(depth, -1, v.shape[-1])

    assert depth == 2
    nd = len(DENSE)
    for k, n in enumerate(DENSE):
        put(n, _adamw_halves(f"adamw_{n}", (reduced[k], reduced[nd + k]), (other[k], other[nd + k]), core,
                             flat3(a[n]), flat3(a["m_" + n]), flat3(a["v_" + n])), False)

    items = []
    for l in range(depth):
        g = grads["layers"][l]
        items += [g["ffn1_norm"], g["mix_norm"], g["ffn2_norm"], g["rg_small"], _unblockdiag(g["rg_dwa"], bd),
                  _unblockdiag(g["rg_dwx"], bd), g["fox_bf"], g["merge_b"], dmods[l]]
    items += [grads["final_norm"], dfm]
    pack, layout = _pack(items)
    gath, tot = _sum_devices("sum_small", _gather_all("gather_small", pack, own=False), pack,
                             jnp.reshape(dev, (1,)).astype(jnp.int32))

    def wpack(pre):
        its = []
        for l in range(depth):
            rg_rows = jnp.concatenate([jnp.zeros((4, chans), F32), a[pre + "conv_b"][l][None], a[pre + "rg_ba"][l][None],
                                       a[pre + "rg_bx"][l][None], a[pre + "rg_lam"][l][None]], axis=0)
            its += [a[pre + "ffn1_norm"][l], a[pre + "mix_norm"][l], a[pre + "ffn2_norm"][l], rg_rows,
                    a[pre + "rg_wa"][l], a[pre + "rg_wx"][l], a[pre + "fox_bf"][l], a[pre + "merge_b"][l],
                    jnp.zeros((bl, 9 * d), F32)]
        its += [a[pre + "final_norm"], jnp.zeros((bl, 2 * d), F32)]
        return _pack(its)[0]

    res_small = [_unpack(r, layout) for r in [tot] + list(_adamw("adamw_small", tot, wpack(""), wpack("m_"), wpack("v_")))]
    per = len(SMALL) + 1
    for j, n in enumerate(SMALL):
        if n == "rg_small":
            for row, nm in ((4, "conv_b"), (5, "rg_ba"), (6, "rg_bx"), (7, "rg_lam")):
                put(nm, [[r[l * per + j][row] for l in range(depth)] for r in res_small], True)
        else:
            put(n, [[r[l * per + j] for l in range(depth)] for r in res_small], True)
    put("final_norm", [r[depth * per] for r in res_small], False)

    gflat = gath.reshape(NUM_DEVICES, -1)

    def rows_of(idx):
        o, n, shape = layout[idx]
        return gflat[:, o:o + n].reshape(NUM_DEVICES * shape[0], shape[1])

    late_g, ada = [], []
    for l in range(depth):
        dmod_all = rows_of(l * per + per - 1)
        late_g.append(_colsum(f"ada_b_grad_{l}", dmod_all))
        cut = lax.dynamic_slice_in_dim(dmod_all, me * ncol, ncol, axis=1).astype(BF16)
        ada.append(_mm(f"ada_w_grad_{l}", c_b, cut, TN, F32))
    put("ada_w", _adamw_layers("adamw_ada_w", ada[0], ada[1], a["ada_w"], a["m_ada_w"], a["v_ada_w"]), False)
    dfm_all = rows_of(depth * per + 1)
    late_g.append(_colsum("final_ada_b_grad", dfm_all))
    cut = lax.dynamic_slice_in_dim(dfm_all, me * fcol, fcol, axis=1).astype(BF16)
    gl = _mm("final_ada_w_grad", c_b, cut, TN, F32)
    put("final_ada_w", [gl] + list(_adamw("adamw_final_ada_w", gl, a["final_ada_w"], a["m_final_ada_w"],
                                          a["v_final_ada_w"])), False)
    cshard = a["conv_w"].shape[2]
    for l in range(depth):
        rg_tot = res_small[0][l * per + SMALL.index("rg_small")]
        late_g.append(lax.dynamic_slice_in_dim(rg_tot[:4], me * cshard, cshard, axis=1))
    gp2, layout2 = _pack(late_g)

    def wpack2(pre):
        return _pack([a[pre + "ada_b"][l][None] for l in range(depth)] + [a[pre + "final_ada_b"][None]]
                     + [a[pre + "conv_w"][l] for l in range(depth)])[0]

    res_late = [_unpack(r, layout2) for r in [gp2] + list(_adamw("adamw_late", gp2, wpack2(""), wpack2("m_"), wpack2("v_")))]
    put("ada_b", [[r[l] for l in range(depth)] for r in res_late], True)
    put("final_ada_b", [r[depth] for r in res_late], False)
    put("conv_w", [[r[depth + 1 + l] for l in range(depth)] for r in res_late], True)

    outs = [loss, grad_x]
    for kind in ("grad_", "delta_", "new_m_", "new_v_"):
        outs += [out[kind + n] for n in WEIGHTS]
    return tuple(outs)


def kernel(x, c, ffn1_norm, ffn1_w1, ffn1_w3, ffn1_w2, mix_norm, w_in, conv_w, conv_b, rg_wa, rg_ba, rg_wx, rg_bx, rg_lam, fox_bf, merge_b, w_rg, w_sb, w_fox, w_o, ffn2_norm, ffn2_w1, ffn2_w3, ffn2_w2, ada_w, ada_b, final_norm, final_ada_w, final_ada_b, loss_target, m_ffn1_norm, m_ffn1_w1, m_ffn1_w3, m_ffn1_w2, m_mix_norm, m_w_in, m_conv_w, m_conv_b, m_rg_wa, m_rg_ba, m_rg_wx, m_rg_bx, m_rg_lam, m_fox_bf, m_merge_b, m_w_rg, m_w_sb, m_w_fox, m_w_o, m_ffn2_norm, m_ffn2_w1, m_ffn2_w3, m_ffn2_w2, m_ada_w, m_ada_b, m_final_norm, m_final_ada_w, m_final_ada_b, v_ffn1_norm, v_ffn1_w1, v_ffn1_w3, v_ffn1_w2, v_mix_norm, v_w_in, v_conv_w, v_conv_b, v_rg_wa, v_rg_ba, v_rg_wx, v_rg_bx, v_rg_lam, v_fox_bf, v_merge_b, v_w_rg, v_w_sb, v_w_fox, v_w_o, v_ffn2_norm, v_ffn2_w1, v_ffn2_w3, v_ffn2_w2, v_ada_w, v_ada_b, v_final_norm, v_final_ada_w, v_final_ada_b):
    args = dict(locals())
    return _step(args)
```
